```python
import math
import jax, jax.numpy as jnp
from jax import lax
import numpy as np

D_MODEL = 1024
BATCH = 4
SEQ = 4096
DEPTH = 2

CTX_LEN = 256
GRID_W = 64
HEAD_DIM = 64
MIX_HEADS = D_MODEL // HEAD_DIM
BLOCK = 128
WINDOW = 128
ROPE_THETA = 10000.0
LN_EPS = 1e-6
RMS_EPS = 1e-6
NEG_INF = -1e30
N_MOD = 6

A_GROUPS = MIX_HEADS // 4
A_W = A_GROUPS * HEAD_DIM
B_Q_HEADS = MIX_HEADS - A_GROUPS
B_KV_HEADS = B_Q_HEADS // 3
B_GROUP = B_Q_HEADS // B_KV_HEADS
AB_Q_W = A_W + B_Q_HEADS * HEAD_DIM
AB_KV_W = 2 * B_KV_HEADS * HEAD_DIM
AB_IN_W = AB_Q_W + AB_KV_W
AB_OUT_W = A_W + B_Q_HEADS * HEAD_DIM

C_HEADS = MIX_HEADS // 4
D_Q_HEADS = MIX_HEADS // 2
D_KV_HEADS = D_Q_HEADS // 4
D_GROUP = D_Q_HEADS // D_KV_HEADS
C_QK_W = C_HEADS * 2 * HEAD_DIM
D_Q_W = D_Q_HEADS * HEAD_DIM
D_KV_W = D_KV_HEADS * HEAD_DIM
C_V_W = C_HEADS * 2 * HEAD_DIM
CD_Q_W = C_QK_W + D_Q_W
CD_KV_W = C_QK_W + D_KV_W + C_V_W + D_KV_W
CD_IN_W = CD_Q_W + CD_KV_W
CD_OUT_W = C_V_W + D_Q_W

N_EXPERTS = 32
TOP_K = 4
EXPERT_FF = D_MODEL
SWIGLU_LIMIT = 7.0
SWIGLU_ALPHA = 1.702
MOE_BLOCK = 128

DEEPNORM_ALPHA = (2 * DEPTH) ** 0.25
DEEPNORM_BETA = (8 * DEPTH) ** -0.25
N_EVEN = (DEPTH + 1) // 2
N_ODD = DEPTH // 2

kernel_name = 'hybrid_fourier_window_diff_axial_moe_dit'


def _layernorm(x, g, b):
    xf = x.astype(jnp.float32)
    mu = jnp.mean(xf, axis=-1, keepdims=True)
    var = jnp.mean(jnp.square(xf - mu), axis=-1, keepdims=True)
    return ((xf - mu) * lax.rsqrt(var + LN_EPS) * g + b).astype(x.dtype)


def _rmsnorm(x, g):
    xf = x.astype(jnp.float32)
    y = xf * lax.rsqrt(jnp.mean(jnp.square(xf), axis=-1, keepdims=True) + RMS_EPS)
    return (y * g).astype(x.dtype)


def _post_norm(x, y, gate, g, b):
    return _layernorm(DEEPNORM_ALPHA * x + gate * y, g, b)


def _modulate(x, shift, scale):
    return x * (1 + scale) + shift


def _axial_rope_tables(rows, d):
    t = jnp.arange(rows * GRID_W)
    row = (t // GRID_W).astype(jnp.float32)
    col = (t % GRID_W).astype(jnp.float32)
    nf = d // 4
    inv = ROPE_THETA ** (-jnp.arange(nf, dtype=jnp.float32) / nf)
    ar = row[:, None] * inv[None, :]
    ac = col[:, None] * inv[None, :]
    ang = jnp.concatenate([ar, ar, ac, ac], axis=-1)
    return jnp.cos(ang), jnp.sin(ang)


def _apply_rope(x, cos, sin):
    d = x.shape[-1]
    nf = d // 4
    shp = (cos.shape[0],) + (1,) * (x.ndim - 3) + (d,)
    cs, sn = cos.reshape(shp), sin.reshape(shp)
    xf = x.astype(jnp.float32)
    rot = jnp.concatenate([-xf[..., nf:2 * nf], xf[..., :nf], -xf[..., 3 * nf:], xf[..., 2 * nf:3 * nf]], axis=-1)
    return (xf * cs + rot * sn).astype(x.dtype)


def _gqa_scores(q, k):
    return jnp.einsum('bqhgd,bkhd->bhgqk', q, k).astype(jnp.float32) * (q.shape[-1] ** -0.5)


def _gqa_apply(p, v):
    return jnp.einsum('bhgqk,bkhd->bqhgd', p.astype(v.dtype), v)


def _gqa_attention(q, k, v):
    return _gqa_apply(jax.nn.softmax(_gqa_scores(q, k), axis=-1), v)


def _sink_attention(q, k, v, sink):
    s = _gqa_scores(q, k)
    s_sink = jnp.broadcast_to(sink.reshape(B_KV_HEADS, B_GROUP).astype(jnp.float32)[None, :, :, None, None], s.shape[:-1] + (1,))
    p = jax.nn.softmax(jnp.concatenate([s, s_sink], axis=-1), axis=-1)
    return _gqa_apply(p[..., :-1], v)


def _window_attention(q, k, v, kc, vc, sink):
    B, S, Hkv, G, d = q.shape
    n_ctx = kc.shape[1]
    pad = ((0, 0), (BLOCK, BLOCK), (0, 0), (0, 0))
    kp, vp = jnp.pad(k, pad), jnp.pad(v, pad)
    sink_hg = sink.reshape(Hkv, G).astype(jnp.float32)[None, :, :, None, None]
    offs_q = jnp.arange(BLOCK)
    offs_k = jnp.arange(3 * BLOCK) - BLOCK

    def block(n):
        q0 = n * BLOCK
        qb = lax.dynamic_slice_in_dim(q, q0, BLOCK, axis=1)
        kb = lax.dynamic_slice_in_dim(kp, q0, 3 * BLOCK, axis=1)
        vb = lax.dynamic_slice_in_dim(vp, q0, 3 * BLOCK, axis=1)
        qpos = q0 + offs_q
        kpos = q0 + offs_k
        allowed = (jnp.abs(qpos[:, None] - kpos[None, :]) <= WINDOW) & (kpos >= 0)[None, :] & (kpos < S)[None, :]
        s_loc = jnp.where(allowed, _gqa_scores(qb, kb), NEG_INF)
        s_ctx = _gqa_scores(qb, kc)
        s_sink = jnp.broadcast_to(sink_hg, s_ctx.shape[:-1] + (1,))
        p = jax.nn.softmax(jnp.concatenate([s_loc, s_ctx, s_sink], axis=-1), axis=-1)
        return _gqa_apply(p[..., :3 * BLOCK], vb) + _gqa_apply(p[..., 3 * BLOCK:3 * BLOCK + n_ctx], vc)

    o = lax.map(block, jnp.arange(S // BLOCK))
    return jnp.moveaxis(o, 0, 1).reshape(B, S, Hkv * G * d)


def _fourier_mix(a):
    B, T, _ = a.shape
    g = a.reshape(B, T, A_GROUPS, HEAD_DIM).astype(jnp.float32)
    f = jnp.fft.fft2(g, axes=(1, 3), norm='ortho').real
    return f.reshape(B, T, A_W).astype(a.dtype)


def _ab_mixer(h_lat, h_ctx, w_in, sink, w_out, cos, sin, need_ctx_out):
    B, S, _ = h_lat.shape
    n_ctx = h_ctx.shape[1]
    p_lat = h_lat @ w_in
    kv_ctx = h_ctx @ w_in[:, AB_Q_W:]

    def kv_split(kv, T):
        k = kv[..., :AB_KV_W // 2].reshape(B, T, B_KV_HEADS, HEAD_DIM)
        v = kv[..., AB_KV_W // 2:].reshape(B, T, B_KV_HEADS, HEAD_DIM)
        return k, v

    k_lat, v_lat = kv_split(p_lat[..., AB_Q_W:], S)
    k_ctx, v_ctx = kv_split(kv_ctx, n_ctx)
    k_lat = _apply_rope(k_lat, cos, sin)
    q_lat = _apply_rope(p_lat[..., A_W:AB_Q_W].reshape(B, S, B_KV_HEADS, B_GROUP, HEAD_DIM), cos, sin)
    o_lat = jnp.concatenate([_fourier_mix(p_lat[..., :A_W]),
                             _window_attention(q_lat, k_lat, v_lat, k_ctx, v_ctx, sink)], axis=-1)
    y_lat = o_lat @ w_out
    y_ctx = None
    if need_ctx_out:
        qs_ctx = h_ctx @ w_in[:, :AB_Q_W]
        q_ctx = qs_ctx[..., A_W:].reshape(B, n_ctx, B_KV_HEADS, B_GROUP, HEAD_DIM)
        o_ctx = jnp.concatenate([_fourier_mix(qs_ctx[..., :A_W]),
                                 _sink_attention(q_ctx, k_ctx, v_ctx, sink).reshape(B, n_ctx, B_Q_HEADS * HEAD_DIM)], axis=-1)
        y_ctx = o_ctx @ w_out
    return y_lat, y_ctx


def _split_cd_q(qs):
    B, T, _ = qs.shape
    q_c = qs[..., :C_QK_W].reshape(B, T, C_HEADS, 2, HEAD_DIM)
    q_d = qs[..., C_QK_W:].reshape(B, T, D_KV_HEADS, D_GROUP, HEAD_DIM)
    return q_c, q_d


def _split_cd_kv(kv):
    B, T, _ = kv.shape
    o1 = C_QK_W
    o2 = o1 + D_KV_W
    o3 = o2 + C_V_W
    k_c = kv[..., :o1].reshape(B, T, C_HEADS, 2, HEAD_DIM)
    k_d = kv[..., o1:o2].reshape(B, T, D_KV_HEADS, HEAD_DIM)
    v_c = kv[..., o2:o3].reshape(B, T, C_HEADS, 2 * HEAD_DIM)
    v_d = kv[..., o3:].reshape(B, T, D_KV_HEADS, HEAD_DIM)
    return k_c, k_d, v_c, v_d


def _diff_attention(q, k, v, lam, subln_g, lambda_init):
    s = jnp.einsum('bqhmd,bkhmd->bhmqk', q, k).astype(jnp.float32) * (HEAD_DIM ** -0.5)
    p = jax.nn.softmax(s, axis=-1)
    w = p[:, :, 0] - lam * p[:, :, 1]
    o = jnp.einsum('bhqk,bkhd->bqhd', w.astype(v.dtype), v)
    return _rmsnorm(o, subln_g) * (1.0 - lambda_init)


def _lambda_init(layer):
    return 0.8 - 0.6 * math.exp(-0.3 * layer)


def _cd_mixer(h_lat, h_ctx, w_in, lam_vec, subln_g, q_norm_g, k_norm_g, w_out, lambda_init, cos, sin, need_ctx_out):
    B, S, _ = h_lat.shape
    n_ctx = h_ctx.shape[1]
    lv = lam_vec.astype(jnp.float32)
    lam = jnp.exp(jnp.sum(lv[0] * lv[1])) - jnp.exp(jnp.sum(lv[2] * lv[3])) + lambda_init
    p_lat = h_lat @ w_in
    q_c, q_d = _split_cd_q(p_lat[..., :CD_Q_W])
    k_c, k_d, v_c, v_d = _split_cd_kv(p_lat[..., CD_Q_W:])
    k_c_ctx, k_d_ctx, v_c_ctx, v_d_ctx = _split_cd_kv(h_ctx @ w_in[:, CD_Q_W:])
    q_c = _apply_rope(q_c, cos, sin)
    k_c = _apply_rope(k_c, cos, sin)
    q_d = _apply_rope(_rmsnorm(q_d, q_norm_g), cos, sin)
    k_d = _apply_rope(_rmsnorm(k_d, k_norm_g), cos, sin)
    k_d_ctx = _rmsnorm(k_d_ctx, k_norm_g)
    k_c_all = jnp.concatenate([k_c_ctx, k_c], axis=1)
    v_c_all = jnp.concatenate([v_c_ctx, v_c], axis=1)
    k_d_all = jnp.concatenate([k_d_ctx, k_d], axis=1)
    v_d_all = jnp.concatenate([v_d_ctx, v_d], axis=1)

    def block(n):
        q0 = n * BLOCK
        qcb = lax.dynamic_slice_in_dim(q_c, q0, BLOCK, axis=1)
        qdb = lax.dynamic_slice_in_dim(q_d, q0, BLOCK, axis=1)
        oc = _diff_attention(qcb, k_c_all, v_c_all, lam, subln_g, lambda_init)
        od = _gqa_attention(qdb, k_d_all, v_d_all)
        return jnp.concatenate([oc.reshape(B, BLOCK, C_V_W), od.reshape(B, BLOCK, D_Q_W)], axis=-1)

    o = lax.map(block, jnp.arange(S // BLOCK))
    y_lat = jnp.moveaxis(o, 0, 1).reshape(B, S, CD_OUT_W) @ w_out
    y_ctx = None
    if need_ctx_out:
        q_c_ctx, q_d_ctx = _split_cd_q(h_ctx @ w_in[:, :CD_Q_W])
        q_d_ctx = _rmsnorm(q_d_ctx, q_norm_g)
        o_ctx = jnp.concatenate([
            _diff_attention(q_c_ctx, k_c_ctx, v_c_ctx, lam, subln_g, lambda_init).reshape(B, n_ctx, C_V_W),
            _gqa_attention(q_d_ctx, k_d_ctx, v_d_ctx).reshape(B, n_ctx, D_Q_W)], axis=-1)
        y_ctx = o_ctx @ w_out
    return y_lat, y_ctx


def _moe(h, router_w, router_b, w_gu, b_gu, w_down, b_down):
    n_tok, d = h.shape
    logits = (h @ router_w + router_b).astype(jnp.float32)
    top_logit, top_e = lax.top_k(logits, TOP_K)
    gates = jax.nn.softmax(top_logit, axis=-1)
    n_asg = n_tok * TOP_K
    flat_e = top_e.reshape(n_asg)
    order = jnp.argsort(flat_e)
    sorted_e = flat_e[order]
    counts = jnp.bincount(flat_e, length=N_EXPERTS)
    padded = (counts + MOE_BLOCK - 1) // MOE_BLOCK * MOE_BLOCK
    start = jnp.cumsum(counts) - counts
    pend = jnp.cumsum(padded)
    pstart = pend - padded
    dest = (pstart[sorted_e] + jnp.arange(n_asg) - start[sorted_e]).astype(jnp.int32)
    n_blocks = -(-n_asg // MOE_BLOCK) + N_EXPERTS
    buf_tok = jnp.full((n_blocks * MOE_BLOCK,), n_tok, jnp.int32).at[dest].set((order // TOP_K).astype(jnp.int32))
    block_e = jnp.minimum(jnp.searchsorted(pend, jnp.arange(n_blocks) * MOE_BLOCK, side='right'), N_EXPERTS - 1)
    xb = jnp.concatenate([h, jnp.zeros((1, d), h.dtype)], axis=0)[buf_tok].reshape(n_blocks, MOE_BLOCK, d)

    def expert_block(args):
        xblk, e = args
        gu = xblk @ w_gu[e] + b_gu[e]
        g = jnp.minimum(gu[..., :EXPERT_FF], SWIGLU_LIMIT)
        u = jnp.clip(gu[..., EXPERT_FF:], -SWIGLU_LIMIT, SWIGLU_LIMIT)
        return ((u + 1.0) * (g * jax.nn.sigmoid(SWIGLU_ALPHA * g))) @ w_down[e] + b_down[e]

    yb = lax.map(expert_block, (xb, block_e)).reshape(n_blocks * MOE_BLOCK, d)
    slot = jnp.zeros((n_asg,), jnp.int32).at[order].set(dest)
    y = yb[slot].reshape(n_tok, TOP_K, d)
    return jnp.einsum('nk,nkd->nd', gates.astype(y.dtype), y)


def setup_inputs(seed: int = 0) -> dict:
    key = jax.random.key(seed)
    ks = jax.random.split(key, 23)
    f32 = jnp.float32
    D = D_MODEL

    def nrm(k, shape, scale):
        return jax.random.normal(k, shape, f32) * scale

    return {
        'x': nrm(ks[0], (BATCH, SEQ, D), 1.0),
        'c': nrm(ks[1], (BATCH, D), 1.0),
        'ctx': nrm(ks[2], (BATCH, CTX_LEN, D), 1.0),
        'c_ctx': nrm(ks[3], (D,), 1.0),
        'mod_w': nrm(ks[4], (DEPTH, D, N_MOD * D), 0.5 * D ** -0.5),
        'mod_b': nrm(ks[5], (DEPTH, N_MOD * D), 0.02),
        'ln_g': 1.0 + nrm(ks[6], (DEPTH, 2, D), 0.02),
        'ln_b': nrm(ks[7], (DEPTH, 2, D), 0.02),
        'ab_w_in': nrm(ks[8], (N_EVEN, D, AB_IN_W), D ** -0.5),
        'ab_sink': nrm(ks[9], (N_EVEN, B_Q_HEADS), 0.5),
        'ab_w_out': nrm(ks[10], (N_EVEN, AB_OUT_W, D), DEEPNORM_BETA * AB_OUT_W ** -0.5),
        'cd_w_in': nrm(ks[11], (N_ODD, D, CD_IN_W), D ** -0.5),
        'cd_lambda': nrm(ks[12], (N_ODD, 4, HEAD_DIM), 0.1),
        'cd_subln_g': 1.0 + nrm(ks[13], (N_ODD, 2 * HEAD_DIM), 0.02),
        'cd_q_norm_g': 1.0 + nrm(ks[14], (N_ODD, HEAD_DIM), 0.02),
        'cd_k_norm_g': 1.0 + nrm(ks[15], (N_ODD, HEAD_DIM), 0.02),
        'cd_w_out': nrm(ks[16], (N_ODD, CD_OUT_W, D), DEEPNORM_BETA * CD_OUT_W ** -0.5),
        'router_w': nrm(ks[17], (DEPTH, D, N_EXPERTS), D ** -0.5),
        'router_b': nrm(ks[18], (DEPTH, N_EXPERTS), 0.01),
        'expert_w_gu': nrm(ks[19], (DEPTH, N_EXPERTS, D, 2 * EXPERT_FF), D ** -0.5),
        'expert_b_gu': nrm(ks[20], (DEPTH, N_EXPERTS, 2 * EXPERT_FF), 0.02),
        'expert_w_down': nrm(ks[21], (DEPTH, N_EXPERTS, EXPERT_FF, D), DEEPNORM_BETA * EXPERT_FF ** -0.5),
        'expert_b_down': nrm(ks[22], (DEPTH, N_EXPERTS, D), 0.02),
    }


def reference(x, c, ctx, c_ctx, mod_w, mod_b, ln_g, ln_b, ab_w_in, ab_sink, ab_w_out,
              cd_w_in, cd_lambda, cd_subln_g, cd_q_norm_g, cd_k_norm_g, cd_w_out,
              router_w, router_b, expert_w_gu, expert_b_gu, expert_w_down, expert_b_down):
    B, S, D = x.shape
    n_ctx = ctx.shape[1]
    ROWS = S // GRID_W
    cos, sin = _axial_rope_tables(ROWS, HEAD_DIM)
    silu_c = jax.nn.silu(c)
    silu_c_ctx = jax.nn.silu(c_ctx)
    x_lat, x_ctx = x, ctx
    for l in range(DEPTH):
        last = l == DEPTH - 1
        m_lat = (silu_c @ mod_w[l] + mod_b[l])[:, None, :]
        m_ctx = (silu_c_ctx @ mod_w[l] + mod_b[l])[None, None, :]
        sh1, sc1, g1, sh2, sc2, g2 = jnp.split(m_lat, N_MOD, axis=-1)
        csh1, csc1, cg1, csh2, csc2, cg2 = jnp.split(m_ctx, N_MOD, axis=-1)
        h_lat = _modulate(x_lat, sh1, sc1)
        h_ctx = _modulate(x_ctx, csh1, csc1)
        i = l // 2
        if l % 2 == 0:
            y_lat, y_ctx = _ab_mixer(h_lat, h_ctx, ab_w_in[i], ab_sink[i], ab_w_out[i], cos, sin, not last)
        else:
            y_lat, y_ctx = _cd_mixer(h_lat, h_ctx, cd_w_in[i], cd_lambda[i], cd_subln_g[i], cd_q_norm_g[i],
                                     cd_k_norm_g[i], cd_w_out[i], _lambda_init(l), cos, sin, not last)
        x_lat = _post_norm(x_lat, y_lat, g1, ln_g[l, 0], ln_b[l, 0])
        h_lat = _modulate(x_lat, sh2, sc2)
        if not last:
            x_ctx = _post_norm(x_ctx, y_ctx, cg1, ln_g[l, 0], ln_b[l, 0])
            h_ctx = _modulate(x_ctx, csh2, csc2)
            h_all = jnp.concatenate([h_ctx.reshape(B * n_ctx, D), h_lat.reshape(B * S, D)], axis=0)
            f = _moe(h_all, router_w[l], router_b[l], expert_w_gu[l], expert_b_gu[l], expert_w_down[l], expert_b_down[l])
            x_ctx = _post_norm(x_ctx, f[:B * n_ctx].reshape(B, n_ctx, D), cg2, ln_g[l, 1], ln_b[l, 1])
            f_lat = f[B * n_ctx:].reshape(B, S, D)
        else:
            f_lat = _moe(h_lat.reshape(B * S, D), router_w[l], router_b[l], expert_w_gu[l], expert_b_gu[l],
                         expert_w_down[l], expert_b_down[l]).reshape(B, S, D)
        x_lat = _post_norm(x_lat, f_lat, g2, ln_g[l, 1], ln_b[l, 1])
    return x_lat
```

```python
import functools
import math

import jax
import jax.numpy as jnp
import numpy as np
from jax import lax
from jax.experimental import pallas as pl
from jax.experimental.pallas import tpu as pltpu

F32 = jnp.float32
BF16 = jnp.bfloat16

HEAD_DIM = 64
GRID_W = 64
WINDOW = 128
ROPE_THETA = 10000.0
LN_EPS = 1e-6
RMS_EPS = 1e-6
N_MOD = 6
TOP_K = 4
SWIGLU_LIMIT = 7.0
SWIGLU_ALPHA = 1.702
NEG = -1e30
LOG2E = 1.4426950408889634
QSCALE = HEAD_DIM ** -0.5 * LOG2E

LANES = 128
TM = 256
MOE_BM = 256
VMEM_LIMIT = 56 * 1024 * 1024


def _dot(a, b):
    return jnp.dot(a, b, preferred_element_type=F32)


def _dot_nt(a, b):
    return lax.dot_general(a, b, (((1,), (1,)), ((), ())), preferred_element_type=F32)


def _cparams(sem):
    return pltpu.CompilerParams(dimension_semantics=sem, vmem_limit_bytes=VMEM_LIMIT)


def _layernorm(z, g, b):
    mu = jnp.mean(z, axis=-1, keepdims=True)
    d = z - mu
    var = jnp.mean(d * d, axis=-1, keepdims=True)
    return d * lax.rsqrt(var + LN_EPS) * g + b


def _rope128(x, cos, sin_signed):
    lane = lax.broadcasted_iota(jnp.int32, x.shape, 1)
    lo = (lane % 32) < 16
    rot = jnp.where(lo, pltpu.roll(x, LANES - 16, 1), pltpu.roll(x, 16, 1))
    return x * cos + rot * sin_signed


def _rmsnorm128(x, g, bd):
    x2 = x * x
    hi = x2.astype(BF16)
    lo = (x2 - hi.astype(F32)).astype(BF16)
    ms = _dot(hi, bd) + _dot(lo, bd)
    return x * lax.rsqrt(ms + RMS_EPS) * g


def _modulated(x_ref, mlat_ref, mctx_ref, is_ctx, d):
    m = jnp.where(is_ctx, mctx_ref[...], mlat_ref[0])
    return (x_ref[0] * (1.0 + m[:, d:2 * d]) + m[:, 0:d]).astype(BF16)


def _mod_kernel(c_ref, w_ref, b_ref, o_ref):
    c = c_ref[...]
    s = (c / (1.0 + jnp.exp(-c))).astype(BF16)
    o_ref[0] = _dot(s, w_ref[0].astype(BF16)) + b_ref[0]


def _mod_vectors(c, c_ctx, mod_w, mod_b):
    depth, d, n = mod_w.shape
    b = c.shape[0]
    rows = 8 * (-(-(b + 1) // 8))
    cs = jnp.zeros((rows, d), F32).at[:b].set(c).at[b].set(c_ctx)
    tn = 1536
    out = pl.pallas_call(
        _mod_kernel,
        grid=(depth, n // tn),
        in_specs=[pl.BlockSpec((rows, d), lambda l, j: (0, 0)),
                  pl.BlockSpec((1, d, tn), lambda l, j: (l, 0, j)),
                  pl.BlockSpec((1, 1, tn), lambda l, j: (l, 0, j))],
        out_specs=pl.BlockSpec((1, rows, tn), lambda l, j: (l, 0, j)),
        out_shape=jax.ShapeDtypeStruct((depth, rows, n), F32),
        compiler_params=_cparams(("parallel", "parallel")),
        name="mod_vectors",
    )(cs, mod_w, mod_b.reshape(depth, 1, n))
    return out


def _proj_ab_kernel(nct, x_ref, mlat_ref, mctx_ref, w_ref, wvt_ref, cos_ref, sin_ref, cs_ref,
                    uw_ref, q_ref, k_ref, vt_ref):
    is_ctx = pl.program_id(1) < nct
    d = x_ref.shape[-1]
    h = _modulated(x_ref, mlat_ref, mctx_ref, is_ctx, d)
    p = _dot(h, w_ref[...])
    uw_ref[0] = _dot(p[:, 0:256].astype(BF16), cs_ref[...]).astype(BF16)
    cos = cos_ref[...]
    sin = sin_ref[...]
    for j in range(6):
        c0 = 256 + j * LANES
        q_ref[0, :, j * LANES:(j + 1) * LANES] = (_rope128(p[:, c0:c0 + LANES], cos, sin) * QSCALE).astype(BF16)
    for j in range(4):
        c0 = 1024 + j * LANES
        k_ref[0, :, j * LANES:(j + 1) * LANES] = _rope128(p[:, c0:c0 + LANES], cos, sin).astype(BF16)
    vt = _dot_nt(wvt_ref[...], h).astype(BF16)
    for c in range(TM // LANES):
        vt_ref[0, c] = vt[:, c * LANES:(c + 1) * LANES]


def _proj_ab(x_all, mlat, mctx, w, wvt, cos, sin, cs, nct):
    b, tall, d = x_all.shape
    nt = tall // TM
    nw = w.shape[1]
    nchunk = tall // LANES
    cpt = TM // LANES
    return pl.pallas_call(
        functools.partial(_proj_ab_kernel, nct),
        grid=(b, nt),
        in_specs=[pl.BlockSpec((1, TM, d), lambda bi, i: (bi, i, 0)),
                  pl.BlockSpec((1, 1, N_MOD * d), lambda bi, i: (bi, 0, 0)),
                  pl.BlockSpec((1, N_MOD * d), lambda bi, i: (0, 0)),
                  pl.BlockSpec((d, nw), lambda bi, i: (0, 0)),
                  pl.BlockSpec((256, d), lambda bi, i: (0, 0)),
                  pl.BlockSpec((TM, LANES), lambda bi, i: (i, 0)),
                  pl.BlockSpec((TM, LANES), lambda bi, i: (i, 0)),
                  pl.BlockSpec((256, 512), lambda bi, i: (0, 0))],
        out_specs=[pl.BlockSpec((1, TM, 512), lambda bi, i: (bi, i, 0)),
                   pl.BlockSpec((1, TM, 768), lambda bi, i: (bi, i, 0)),
                   pl.BlockSpec((1, TM, 512), lambda bi, i: (bi, i, 0)),
                   pl.BlockSpec((1, cpt, 256, LANES), lambda bi, i: (bi, i, 0, 0))],
        out_shape=[jax.ShapeDtypeStruct((b, tall, 512), BF16),
                   jax.ShapeDtypeStruct((b, tall, 768), BF16),
                   jax.ShapeDtypeStruct((b, tall, 512), BF16),
                   jax.ShapeDtypeStruct((b, nchunk, 256, LANES), BF16)],
        compiler_params=_cparams(("parallel", "parallel")),
        name="proj_ab",
    )(x_all, mlat, mctx, w, wvt, cos, sin, cs)


def _proj_cd_kernel(nct, x_ref, mlat_ref, mctx_ref, w_ref, wvt_ref, cos_ref, sin_ref, bd_ref,
                    gq_ref, gk_ref, qc_ref, qd_ref, kc_ref, kd_ref, vt_ref):
    is_ctx = pl.program_id(1) < nct
    d = x_ref.shape[-1]
    h = _modulated(x_ref, mlat_ref, mctx_ref, is_ctx, d)
    p = _dot(h, w_ref[...])
    cos = cos_ref[...]
    sin = sin_ref[...]
    bd = bd_ref[...]
    gq = gq_ref[...]
    gk = gk_ref[...]
    for j in range(4):
        sl = slice(j * LANES, (j + 1) * LANES)
        qc_ref[0, :, sl] = (_rope128(p[:, j * LANES:(j + 1) * LANES], cos, sin) * QSCALE).astype(BF16)
        c0 = 512 + j * LANES
        qd_ref[0, :, sl] = (_rope128(_rmsnorm128(p[:, c0:c0 + LANES], gq, bd), cos, sin) * QSCALE).astype(BF16)
        c0 = 1024 + j * LANES
        kc_ref[0, :, sl] = _rope128(p[:, c0:c0 + LANES], cos, sin).astype(BF16)
    for j in range(2):
        c0 = 1536 + j * LANES
        kd_ref[0, :, j * LANES:(j + 1) * LANES] = _rope128(
            _rmsnorm128(p[:, c0:c0 + LANES], gk, bd), cos, sin).astype(BF16)
    vt_ref[0, 0] = _dot_nt(wvt_ref[...], h).astype(BF16)


def _proj_cd(x_all, mlat, mctx, w, wvt, cos, sin, bd, gq, gk, nct):
    b, tall, d = x_all.shape
    nt = tall // TM
    nw = w.shape[1]
    nv = wvt.shape[0]
    full = lambda bi, i: (0, 0)
    tile = lambda bi, i: (bi, i, 0)
    return pl.pallas_call(
        functools.partial(_proj_cd_kernel, nct),
        grid=(b, nt),
        in_specs=[pl.BlockSpec((1, TM, d), tile),
                  pl.BlockSpec((1, 1, N_MOD * d), lambda bi, i: (bi, 0, 0)),
                  pl.BlockSpec((1, N_MOD * d), full),
                  pl.BlockSpec((d, nw), full),
                  pl.BlockSpec((nv, d), full),
                  pl.BlockSpec((TM, LANES), lambda bi, i: (i, 0)),
                  pl.BlockSpec((TM, LANES), lambda bi, i: (i, 0)),
                  pl.BlockSpec((LANES, LANES), full),
                  pl.BlockSpec((1, LANES), full),
                  pl.BlockSpec((1, LANES), full)],
        out_specs=[pl.BlockSpec((1, TM, 512), tile),
                   pl.BlockSpec((1, TM, 512), tile),
                   pl.BlockSpec((1, TM, 512), tile),
                   pl.BlockSpec((1, TM, 256), tile),
                   pl.BlockSpec((1, 1, nv, TM), lambda bi, i: (bi, i, 0, 0))],
        out_shape=[jax.ShapeDtypeStruct((b, tall, 512), BF16),
                   jax.ShapeDtypeStruct((b, tall, 512), BF16),
                   jax.ShapeDtypeStruct((b, tall, 512), BF16),
                   jax.ShapeDtypeStruct((b, tall, 256), BF16),
                   jax.ShapeDtypeStruct((b, nt, nv, TM), BF16)],
        compiler_params=_cparams(("parallel", "parallel")),
        name="proj_cd",
    )(x_all, mlat, mctx, w, wvt, cos, sin, bd, gq, gk)


def _fourier_kernel(nb, scale, uw_ref, ca_ref, sa_ref, cb_ref, sb_ref, o_ref):
    j = pl.program_id(0)
    ca = ca_ref[pl.ds(j, 1), :]
    sa = sa_ref[pl.ds(j, 1), :]
    cb = cb_ref[...]
    sb = sb_ref[...]
    ct = (ca * cb - sa * sb).astype(BF16)
    nst = (-(sa * cb + ca * sb)).astype(BF16)
    for bi in range(nb):
        acc = _dot(ct, uw_ref[bi, :, 0:256]) + _dot(nst, uw_ref[bi, :, 256:512])
        o_ref[bi] = (acc * scale).astype(BF16)


def _dft_tables(t, tmf):
    k = np.arange(t, dtype=np.int64)
    j1 = np.arange(t // tmf, dtype=np.int64) * tmf
    j0 = np.arange(tmf, dtype=np.int64)
    aa = (2.0 * np.pi / t) * ((j1[:, None] * k[None, :]) % t)
    ab = (2.0 * np.pi / t) * ((j0[:, None] * k[None, :]) % t)
    f = lambda a: jnp.asarray(a, F32)
    return f(np.cos(aa)), f(np.sin(aa)), f(np.cos(ab)), f(np.sin(ab))


def _fourier(uw):
    b, t, _ = uw.shape
    tmf = min(128, t)
    ca, sa, cb, sb = _dft_tables(t, tmf)
    scale = 1.0 / math.sqrt(t * HEAD_DIM)
    full2 = lambda j: (0, 0)
    return pl.pallas_call(
        functools.partial(_fourier_kernel, b, scale),
        grid=(t // tmf,),
        in_specs=[pl.BlockSpec((b, t, 512), lambda j: (0, 0, 0)),
                  pl.BlockSpec((t // tmf, t), full2),
                  pl.BlockSpec((t // tmf, t), full2),
                  pl.BlockSpec((tmf, t), full2),
                  pl.BlockSpec((tmf, t), full2)],
        out_specs=pl.BlockSpec((b, tmf, 256), lambda j: (0, j, 0)),
        out_shape=jax.ShapeDtypeStruct((b, t, 256), BF16),
        compiler_params=_cparams(("parallel",)),
        name="fourier_mix",
    )(uw, ca, sa, cb, sb)


def _win_attn_kernel(nct, nchunk, sink_ref, q_ref, k_ref, vt_ref, o_ref):
    i = pl.program_id(1)
    is_ctx = i < nct
    j = i - nct
    ctx_rows = nct * TM
    ctx_chunks = ctx_rows // LANES
    lw = TM + 2 * WINDOW
    lchunks = lw // LANES
    cs = jnp.clip(ctx_chunks + (TM // LANES) * j - WINDOW // LANES, 0, nchunk - lchunks)
    rs = pl.multiple_of(cs * LANES, LANES)
    qpos = j * TM + lax.broadcasted_iota(jnp.int32, (1, TM), 1)
    kpos = cs * LANES - ctx_rows + lax.broadcasted_iota(jnp.int32, (lw, 1), 0)
    allowed = (jnp.abs(qpos - kpos) <= WINDOW) & (kpos >= 0) & jnp.logical_not(is_ctx)
    lane = lax.broadcasted_iota(jnp.int32, (TM, LANES), 1)
    n_heads = q_ref.shape[-1] // HEAD_DIM
    group = n_heads // (k_ref.shape[-1] // LANES)
    for pair in range(n_heads // 2):
        qp = q_ref[0, :, pair * LANES:(pair + 1) * LANES]
        outs = []
        for half in range(2):
            hq = 2 * pair + half
            g = hq // group
            qm = jnp.where((lane >= HEAD_DIM) == (half == 1), qp, jnp.zeros_like(qp))
            kc = k_ref[0, 0:ctx_rows, g * LANES:(g + 1) * LANES]
            kl = k_ref[0, pl.ds(rs, lw), g * LANES:(g + 1) * LANES]
            s_c = _dot_nt(kc, qm)
            s_l = jnp.where(allowed, _dot_nt(kl, qm), NEG)
            sk = sink_ref[hq] * LOG2E
            m = jnp.maximum(jnp.maximum(jnp.max(s_c, axis=0, keepdims=True),
                                        jnp.max(s_l, axis=0, keepdims=True)), sk)
            p_c = jnp.exp2(s_c - m)
            p_l = jnp.exp2(s_l - m)
            l = (jnp.sum(p_c, axis=0, keepdims=True) + jnp.sum(p_l, axis=0, keepdims=True)
                 + jnp.exp2(sk - m))
            p_c = p_c.astype(BF16)
            p_l = p_l.astype(BF16)
            vrows = slice(g * HEAD_DIM, (g + 1) * HEAD_DIM)
            acc = jnp.zeros((HEAD_DIM, TM), F32)
            for c in range(ctx_chunks):
                acc = acc + _dot(vt_ref[0, c, vrows, :], p_c[c * LANES:(c + 1) * LANES, :])
            for c in range(lchunks):
                acc = acc + _dot(vt_ref[0, cs + c, vrows, :], p_l[c * LANES:(c + 1) * LANES, :])
            outs.append(acc / l)
        o2 = jnp.concatenate(outs, axis=0)
        o_ref[0, :, pair * LANES:(pair + 1) * LANES] = o2.T.astype(BF16)


def _win_attn(sink, q, k, vt, nct):
    b, tall, qw = q.shape
    nt = tall // TM
    nchunk = vt.shape[1]
    return pl.pallas_call(
        functools.partial(_win_attn_kernel, nct, nchunk),
        grid=(b, nt),
        in_specs=[pl.BlockSpec(memory_space=pltpu.SMEM),
                  pl.BlockSpec((1, TM, qw), lambda bi, i: (bi, i, 0)),
                  pl.BlockSpec((1, tall, k.shape[-1]), lambda bi, i: (bi, 0, 0)),
                  pl.BlockSpec((1, nchunk, vt.shape[2], LANES), lambda bi, i: (bi, 0, 0, 0))],
        out_specs=pl.BlockSpec((1, TM, qw), lambda bi, i: (bi, i, 0)),
        out_shape=jax.ShapeDtypeStruct((b, tall, qw), BF16),
        compiler_params=_cparams(("parallel", "parallel")),
        name="window_attention",
    )(sink, q, k, vt)


N_C_HEADS = 4
N_D_HEADS = 8
N_D_KV = 2
N_UNITS = 2 * N_C_HEADS + N_D_HEADS


def _attn_cd_kernel(nkt, lam_init, lam_ref, sg_ref, qc_ref, qd_ref, kc_ref, kd_ref, vt_ref, o_ref,
                    qm_s, m_s, l_s, acc_s):
    mq = qc_ref.shape[1]
    tk = vt_ref.shape[-1]
    lane = lax.broadcasted_iota(jnp.int32, (mq, LANES), 1)
    upper = lane >= HEAD_DIM
    for u in range(N_UNITS):
        src = qc_ref if u < 2 * N_C_HEADS else qd_ref
        pair = (u if u < 2 * N_C_HEADS else u - 2 * N_C_HEADS) // 2
        qp = src[0, :, pair * LANES:(pair + 1) * LANES]
        qm_s[u] = jnp.where(upper == (u % 2 == 1), qp, jnp.zeros_like(qp))
    m_s[...] = jnp.full(m_s.shape, NEG, F32)
    l_s[...] = jnp.zeros(l_s.shape, F32)
    acc_s[...] = jnp.zeros(acc_s.shape, F32)
    d_group = N_D_HEADS // N_D_KV
    v_c_rows = 2 * HEAD_DIM

    def body(t, carry):
        r0 = pl.multiple_of(t * tk, tk)
        for u in range(N_UNITS):
            if u < 2 * N_C_HEADS:
                hc = u // 2
                kt = kc_ref[0, pl.ds(r0, tk), hc * LANES:(hc + 1) * LANES]
                vrow0, dv = hc * v_c_rows, v_c_rows
            else:
                g = (u - 2 * N_C_HEADS) // d_group
                kt = kd_ref[0, pl.ds(r0, tk), g * LANES:(g + 1) * LANES]
                vrow0, dv = N_C_HEADS * v_c_rows + g * HEAD_DIM, HEAD_DIM
            s = _dot_nt(kt, qm_s[u])
            m_old = m_s[u:u + 1, :]
            m_new = jnp.maximum(m_old, jnp.max(s, axis=0, keepdims=True))
            alpha = jnp.exp2(m_old - m_new)
            p = jnp.exp2(s - m_new)
            l_s[u:u + 1, :] = l_s[u:u + 1, :] * alpha + jnp.sum(p, axis=0, keepdims=True)
            m_s[u:u + 1, :] = m_new
            pv = _dot(vt_ref[0, t, vrow0:vrow0 + dv, :], p.astype(BF16))
            acc_s[u, 0:dv, :] = acc_s[u, 0:dv, :] * alpha + pv
        return carry

    lax.fori_loop(0, nkt, body, 0)

    lv = lam_ref[...]
    lam = (jnp.exp(jnp.sum(lv[0:1] * lv[1:2], axis=-1, keepdims=True))
           - jnp.exp(jnp.sum(lv[2:3] * lv[3:4], axis=-1, keepdims=True)) + lam_init)
    sg = sg_ref[...]
    for hc in range(N_C_HEADS):
        o1 = acc_s[2 * hc] / l_s[2 * hc:2 * hc + 1, :]
        o2 = acc_s[2 * hc + 1] / l_s[2 * hc + 1:2 * hc + 2, :]
        o = o1 - lam * o2
        ms = jnp.mean(o * o, axis=0, keepdims=True)
        o = o * lax.rsqrt(ms + RMS_EPS) * sg * (1.0 - lam_init)
        o_ref[0, :, hc * LANES:(hc + 1) * LANES] = o.T.astype(BF16)
    for pair in range(N_D_HEADS // 2):
        outs = []
        for half in range(2):
            u = 2 * N_C_HEADS + 2 * pair + half
            outs.append(acc_s[u, 0:HEAD_DIM, :] / l_s[u:u + 1, :])
        o2 = jnp.concatenate(outs, axis=0)
        c0 = N_C_HEADS * LANES + pair * LANES
        o_ref[0, :, c0:c0 + LANES] = o2.T.astype(BF16)


def _attn_cd(lam_vec, sg, qc, qd, kc, kd, vt, nct, lam_init):
    b, tall, _ = qc.shape
    nkt, nv, tk = vt.shape[1], vt.shape[2], vt.shape[3]
    mq = TM
    nq = tall // mq - nct
    ow = N_C_HEADS * LANES + N_D_HEADS * HEAD_DIM
    qtile = lambda bi, i: (bi, i + nct, 0)
    return pl.pallas_call(
        functools.partial(_attn_cd_kernel, nkt, lam_init),
        grid=(b, nq),
        in_specs=[pl.BlockSpec((4, HEAD_DIM), lambda bi, i: (0, 0)),
                  pl.BlockSpec((LANES, 1), lambda bi, i: (0, 0)),
                  pl.BlockSpec((1, mq, qc.shape[-1]), qtile),
                  pl.BlockSpec((1, mq, qd.shape[-1]), qtile),
                  pl.BlockSpec((1, tall, kc.shape[-1]), lambda bi, i: (bi, 0, 0)),
                  pl.BlockSpec((1, tall, kd.shape[-1]), lambda bi, i: (bi, 0, 0)),
                  pl.BlockSpec((1, nkt, nv, tk), lambda bi, i: (bi, 0, 0, 0))],
        out_specs=pl.BlockSpec((1, mq, ow), lambda bi, i: (bi, i, 0)),
        out_shape=jax.ShapeDtypeStruct((b, nq * mq, ow), BF16),
        scratch_shapes=[pltpu.VMEM((N_UNITS, mq, LANES), BF16),
                        pltpu.VMEM((N_UNITS, mq), F32),
                        pltpu.VMEM((N_UNITS, mq), F32),
                        pltpu.VMEM((N_UNITS, 2 * HEAD_DIM, mq), F32)],
        compiler_params=_cparams(("parallel", "parallel")),
        name="attention_cd",
    )(lam_vec, sg, qc, qd, kc, kd, vt)


def _oproj_kernel(n_parts, widths, nct, alpha, n_exp, *refs):
    o_refs = refs[:n_parts]
    (w_ref, x_ref, mlat_ref, mctx_ref, lng_ref, lnb_ref, rwt_ref, rb_ref,
     x1_ref, h2_ref, te_ref, tg_ref) = refs[n_parts:]
    d = x_ref.shape[-1]
    is_ctx = pl.program_id(1) < nct
    y = None
    r0 = 0
    for o_ref, wd in zip(o_refs, widths):
        part = _dot(o_ref[0], w_ref[r0:r0 + wd, :])
        y = part if y is None else y + part
        r0 += wd
    m = jnp.where(is_ctx, mctx_ref[...], mlat_ref[0])
    x1 = _layernorm(alpha * x_ref[0] + m[:, 2 * d:3 * d] * y, lng_ref[...], lnb_ref[...])
    x1_ref[0] = x1
    h2 = (x1 * (1.0 + m[:, 4 * d:5 * d]) + m[:, 3 * d:4 * d]).astype(BF16)
    h2_ref[0] = h2
    logits = _dot_nt(rwt_ref[...], h2) + rb_ref[...]
    tm = logits.shape[1]
    row = lax.broadcasted_iota(jnp.int32, logits.shape, 0)
    vals, idxs = [], []
    for _ in range(TOP_K):
        mx = jnp.max(logits, axis=0, keepdims=True)
        idx = jnp.min(jnp.where(logits == mx, row, n_exp), axis=0, keepdims=True)
        vals.append(mx)
        idxs.append(idx)
        logits = jnp.where(row == idx, NEG, logits)
    es = [jnp.exp(v - vals[0]) for v in vals]
    tot = es[0] + es[1] + es[2] + es[3]
    pad_i = jnp.zeros((8 - TOP_K, tm), jnp.int32)
    pad_f = jnp.zeros((8 - TOP_K, tm), F32)
    te_ref[0] = jnp.concatenate(idxs + [pad_i], axis=0)
    tg_ref[0] = jnp.concatenate([e / tot for e in es] + [pad_f], axis=0)


def _oproj(parts, w_out, x_all, mlat, mctx, lng, lnb, rwt, rb, nct_out, x_tile_off, alpha):
    b, t, _ = parts[0].shape
    d = x_all.shape[-1]
    nt = t // TM
    n_exp = rwt.shape[0]
    widths = tuple(p.shape[-1] for p in parts)
    tile = lambda bi, i: (bi, i, 0)
    full = lambda bi, i: (0, 0)
    in_specs = [pl.BlockSpec((1, TM, wd), tile) for wd in widths] + [
        pl.BlockSpec(w_out.shape, full),
        pl.BlockSpec((1, TM, d), lambda bi, i: (bi, i + x_tile_off, 0)),
        pl.BlockSpec((1, 1, N_MOD * d), lambda bi, i: (bi, 0, 0)),
        pl.BlockSpec((1, N_MOD * d), full),
        pl.BlockSpec((1, d), full),
        pl.BlockSpec((1, d), full),
        pl.BlockSpec((n_exp, d), full),
        pl.BlockSpec((n_exp, 1), full)]
    return pl.pallas_call(
        functools.partial(_oproj_kernel, len(parts), widths, nct_out, alpha, n_exp),
        grid=(b, nt),
        in_specs=in_specs,
        out_specs=[pl.BlockSpec((1, TM, d), tile),
                   pl.BlockSpec((1, TM, d), tile),
                   pl.BlockSpec((1, 8, TM), lambda bi, i: (bi, 0, i)),
                   pl.BlockSpec((1, 8, TM), lambda bi, i: (bi, 0, i))],
        out_shape=[jax.ShapeDtypeStruct((b, t, d), F32),
                   jax.ShapeDtypeStruct((b, t, d), BF16),
                   jax.ShapeDtypeStruct((b, 8, t), jnp.int32),
                   jax.ShapeDtypeStruct((b, 8, t), F32)],
        compiler_params=_cparams(("parallel", "parallel")),
        name="out_proj_norm_router",
    )(*parts, w_out, x_all, mlat, mctx, lng, lnb, rwt, rb)


def _moe_kernel(be_ref, nu_ref, x_ref, wgu_ref, bgu_ref, wd_ref, bd_ref, y_ref, wgu_s, wd_s):
    blk = pl.program_id(0)
    ff = wd_ref.shape[1]

    @pl.when(blk < nu_ref[0])
    def _():
        prev = be_ref[jnp.maximum(blk - 1, 0)]

        @pl.when((blk == 0) | (be_ref[blk] != prev))
        def _():
            rows = 128

            def cast_gu(r, c):
                r0 = pl.multiple_of(r * rows, rows)
                wgu_s[pl.ds(r0, rows), :] = wgu_ref[0, pl.ds(r0, rows), :].astype(BF16)
                return c

            def cast_d(r, c):
                r0 = pl.multiple_of(r * rows, rows)
                wd_s[pl.ds(r0, rows), :] = wd_ref[0, pl.ds(r0, rows), :].astype(BF16)
                return c

            lax.fori_loop(0, wgu_s.shape[0] // rows, cast_gu, 0)
            lax.fori_loop(0, wd_s.shape[0] // rows, cast_d, 0)

        gu = _dot(x_ref[...], wgu_s[...]) + bgu_ref[0]
        g = jnp.minimum(gu[:, :ff], SWIGLU_LIMIT)
        u = jnp.clip(gu[:, ff:], -SWIGLU_LIMIT, SWIGLU_LIMIT)
        act = (u + 1.0) * (g / (1.0 + jnp.exp(-SWIGLU_ALPHA * g)))
        y_ref[...] = _dot(act.astype(BF16), wd_s[...]) + bd_ref[0]


def _moe_experts(block_e, n_used, xb, w_gu, b_gu, w_down, b_down):
    n_slots, d = xb.shape
    n_exp, _, ff2 = w_gu.shape
    ff = w_down.shape[1]
    nb = n_slots // MOE_BM
    rowblk = lambda i, be, nu: (jnp.minimum(i, nu[0] - 1), 0)
    wblk = lambda i, be, nu: (be[i], 0, 0)
    grid_spec = pltpu.PrefetchScalarGridSpec(
        num_scalar_prefetch=2,
        grid=(nb,),
        in_specs=[pl.BlockSpec((MOE_BM, d), rowblk),
                  pl.BlockSpec((1, d, ff2), wblk),
                  pl.BlockSpec((1, 1, ff2), wblk),
                  pl.BlockSpec((1, ff, d), wblk),
                  pl.BlockSpec((1, 1, d), wblk)],
        out_specs=pl.BlockSpec((MOE_BM, d), rowblk),
        scratch_shapes=[pltpu.VMEM((d, ff2), BF16), pltpu.VMEM((ff, d), BF16)])
    return pl.pallas_call(
        _moe_kernel,
        grid_spec=grid_spec,
        out_shape=jax.ShapeDtypeStruct((n_slots, d), F32),
        compiler_params=_cparams(("arbitrary",)),
        name="moe_experts",
    )(block_e, n_used, xb, w_gu, b_gu.reshape(n_exp, 1, ff2), w_down, b_down.reshape(n_exp, 1, d))


def _route(top_e, n_exp):
    n_tok, k = top_e.shape
    n_asg = n_tok * k
    e_flat = top_e.reshape(n_asg)
    onehot = (e_flat[:, None] == jnp.arange(n_exp, dtype=jnp.int32)[None, :]).astype(jnp.int32)
    csum = jnp.cumsum(onehot, axis=0)
    rank = jnp.take_along_axis(csum, e_flat[:, None], axis=1)[:, 0] - 1
    counts = csum[-1]
    padded = (counts + MOE_BM - 1) // MOE_BM * MOE_BM
    pend = jnp.cumsum(padded)
    pstart = pend - padded
    dest = (pstart[e_flat] + rank).astype(jnp.int32)
    n_blocks = -(-n_asg // MOE_BM) + n_exp
    n_used = (pend[-1] // MOE_BM).astype(jnp.int32)
    tok_of_slot = jnp.zeros((n_blocks * MOE_BM,), jnp.int32).at[dest].set(
        jnp.arange(n_asg, dtype=jnp.int32) // k)
    blk = jnp.minimum(jnp.arange(n_blocks, dtype=jnp.int32), n_used - 1) * MOE_BM
    block_e = jnp.minimum(jnp.searchsorted(pend, blk, side='right'), n_exp - 1).astype(jnp.int32)
    return dest, tok_of_slot, block_e, n_used.reshape(1)


def _postnorm2_kernel(nct, alpha, x1_ref, f_ref, mlat_ref, mctx_ref, lng_ref, lnb_ref, o_ref):
    d = x1_ref.shape[-1]
    is_ctx = pl.program_id(1) < nct
    m = jnp.where(is_ctx, mctx_ref[...], mlat_ref[0])
    o_ref[0] = _layernorm(alpha * x1_ref[0] + m[:, 5 * d:6 * d] * f_ref[0], lng_ref[...], lnb_ref[...])


def _postnorm2(x1, f, mlat, mctx, lng, lnb, nct, alpha):
    b, t, d = x1.shape
    tile = lambda bi, i: (bi, i, 0)
    full = lambda bi, i: (0, 0)
    return pl.pallas_call(
        functools.partial(_postnorm2_kernel, nct, alpha),
        grid=(b, t // TM),
        in_specs=[pl.BlockSpec((1, TM, d), tile),
                  pl.BlockSpec((1, TM, d), tile),
                  pl.BlockSpec((1, 1, N_MOD * d), lambda bi, i: (bi, 0, 0)),
                  pl.BlockSpec((1, N_MOD * d), full),
                  pl.BlockSpec((1, d), full),
                  pl.BlockSpec((1, d), full)],
        out_specs=pl.BlockSpec((1, TM, d), tile),
        out_shape=jax.ShapeDtypeStruct((b, t, d), F32),
        compiler_params=_cparams(("parallel", "parallel")),
        name="post_norm_ffn",
    )(x1, f, mlat, mctx, lng, lnb)


def _rope_tables(ctx_len, seq):
    t = np.arange(seq)
    row = (t // GRID_W).astype(np.float64)
    col = (t % GRID_W).astype(np.float64)
    nf = HEAD_DIM // 4
    inv = ROPE_THETA ** (-np.arange(nf, dtype=np.float64) / nf)
    ar = row[:, None] * inv[None, :]
    ac = col[:, None] * inv[None, :]
    ang = np.concatenate([ar, ar, ac, ac], axis=-1)
    cos = np.concatenate([np.ones((ctx_len, HEAD_DIM)), np.cos(ang)], axis=0)
    sin = np.concatenate([np.zeros((ctx_len, HEAD_DIM)), np.sin(ang)], axis=0)
    sign = np.where((np.arange(HEAD_DIM) % 32) < 16, -1.0, 1.0)[None, :]
    cos2 = np.tile(cos, (1, LANES // HEAD_DIM))
    sin2 = np.tile(sin * sign, (1, LANES // HEAD_DIM))
    return jnp.asarray(cos2, F32), jnp.asarray(sin2, F32)


def _channel_dft():
    c = np.arange(HEAD_DIM)
    ang = 2.0 * np.pi * ((c[:, None] * c[None, :]) % HEAD_DIM) / HEAD_DIM
    eye = np.eye(4)
    cs = np.concatenate([np.kron(eye, np.cos(ang)), np.kron(eye, np.sin(ang))], axis=1)
    return jnp.asarray(cs, BF16)


def _group_mean_matrix():
    bd = np.kron(np.eye(LANES // HEAD_DIM), np.full((HEAD_DIM, HEAD_DIM), 1.0 / HEAD_DIM))
    return jnp.asarray(bd, BF16)


def _dup_heads(wk, n_heads):
    d = wk.shape[0]
    return jnp.broadcast_to(wk.reshape(d, n_heads, 1, HEAD_DIM), (d, n_heads, 2, HEAD_DIM)).reshape(d, n_heads * LANES)


def _lambda_init(layer):
    return 0.8 - 0.6 * math.exp(-0.3 * layer)


def _moe_layer(h2, te, tg, router_n_exp, w_gu, b_gu, w_down, b_down):
    b, t, d = h2.shape
    top_e = jnp.transpose(te[:, :TOP_K, :], (0, 2, 1)).reshape(b * t, TOP_K)
    gates = jnp.transpose(tg[:, :TOP_K, :], (0, 2, 1)).reshape(b * t, TOP_K)
    dest, tok_of_slot, block_e, n_used = _route(top_e, router_n_exp)
    xb = h2.reshape(b * t, d)[tok_of_slot]
    yb = _moe_experts(block_e, n_used, xb, w_gu, b_gu, w_down, b_down)
    y = yb[dest].reshape(b * t, TOP_K, d)
    return jnp.sum(gates[:, :, None] * y, axis=1).reshape(b, t, d)


def kernel(x, c, ctx, c_ctx, mod_w, mod_b, ln_g, ln_b, ab_w_in, ab_sink, ab_w_out,
           cd_w_in, cd_lambda, cd_subln_g, cd_q_norm_g, cd_k_norm_g, cd_w_out,
           router_w, router_b, expert_w_gu, expert_b_gu, expert_w_down, expert_b_down):
    b, s, d = x.shape
    n_ctx = ctx.shape[1]
    depth = mod_w.shape[0]
    n_exp = router_w.shape[-1]
    assert d == 16 * HEAD_DIM and n_ctx % TM == 0 and s % TM == 0 and s % GRID_W == 0
    nct = n_ctx // TM
    alpha = (2 * depth) ** 0.25

    cos, sin = _rope_tables(n_ctx, s)
    cs_dft = _channel_dft()
    bd = _group_mean_matrix()
    mods = _mod_vectors(c, c_ctx, mod_w, mod_b)

    x_all = jnp.concatenate([ctx, x], axis=1)
    for l in range(depth):
        last = l == depth - 1
        i = l // 2
        mlat = mods[l, :b].reshape(b, 1, N_MOD * d)
        mctx = mods[l, b:b + 1]
        lng1, lnb1 = ln_g[l, 0].reshape(1, d), ln_b[l, 0].reshape(1, d)
        lng2, lnb2 = ln_g[l, 1].reshape(1, d), ln_b[l, 1].reshape(1, d)
        rwt = router_w[l].T.astype(BF16)
        rb = router_b[l].reshape(n_exp, 1)
        if l % 2 == 0:
            w = ab_w_in[i]
            w_n = jnp.concatenate([w[:, :1024], _dup_heads(w[:, 1024:1280], 4)], axis=1).astype(BF16)
            wvt = w[:, 1280:1536].T.astype(BF16)
            uw, q, k, vt = _proj_ab(x_all, mlat, mctx, w_n, wvt, cos, sin, cs_dft, nct)
            oa = jnp.concatenate([_fourier(uw[:, :n_ctx]), _fourier(uw[:, n_ctx:])], axis=1)
            ob = _win_attn(ab_sink[i], q, k, vt, nct)
            if last:
                parts = [oa[:, n_ctx:], ob[:, n_ctx:]]
            else:
                parts = [oa, ob]
            w_out = ab_w_out[i].astype(BF16)
        else:
            w = cd_w_in[i]
            w_n = jnp.concatenate([w[:, :1536], _dup_heads(w[:, 1536:1664], 2)], axis=1).astype(BF16)
            wvt = w[:, 1664:2304].T.astype(BF16)
            gq = jnp.tile(cd_q_norm_g[i], 2).reshape(1, LANES)
            gk = jnp.tile(cd_k_norm_g[i], 2).reshape(1, LANES)
            qc, qd, kc, kd, vt = _proj_cd(x_all, mlat, mctx, w_n, wvt, cos, sin, bd, gq, gk, nct)
            sg = cd_subln_g[i].reshape(LANES, 1)
            o_lat = _attn_cd(cd_lambda[i], sg, qc, qd, kc, kd, vt, nct, _lambda_init(l))
            if last:
                parts = [o_lat]
            else:
                raise NotImplementedError("context outputs of a differential/axial layer")
            w_out = cd_w_out[i].astype(BF16)
        if last:
            x1, h2, te, tg = _oproj(parts, w_out, x_all, mlat, mctx, lng1, lnb1, rwt, rb, 0, nct, alpha)
            f = _moe_layer(h2, te, tg, n_exp, expert_w_gu[l], expert_b_gu[l], expert_w_down[l], expert_b_down[l])
            return _postnorm2(x1, f, mlat, mctx, lng2, lnb2, 0, alpha)
        x1, h2, te, tg = _oproj(parts, w_out, x_all, mlat, mctx, lng1, lnb1, rwt, rb, nct, 0, alpha)
        f = _moe_layer(h2, te, tg, n_exp, expert_w_gu[l], expert_b_gu[l], expert_w_down[l], expert_b_down[l])
        x_all = _postnorm2(x1, f, mlat, mctx, lng2, lnb2, nct, alpha)
    return x_all[:, n_ctx:]
```

```python
import functools
import math

import jax
import jax.numpy as jnp
import numpy as np
from jax import lax
from jax.experimental import pallas as pl
from jax.experimental.pallas import tpu as pltpu

F32 = jnp.float32
BF16 = jnp.bfloat16

HEAD_DIM = 64
GRID_W = 64
WINDOW = 128
ROPE_THETA = 10000.0
LN_EPS = 1e-6
RMS_EPS = 1e-6
N_MOD = 6
TOP_K = 4
SWIGLU_LIMIT = 7.0
SWIGLU_ALPHA = 1.702
NEG = -1e30
LOG2E = 1.4426950408889634
QSCALE = HEAD_DIM ** -0.5 * LOG2E

N_C_HEADS = 4
N_D_HEADS = 8
N_D_KV = 2
N_UNITS = 2 * N_C_HEADS + N_D_HEADS
ONES_ROWS = 16
_VT_LAYOUT = tuple((h * (2 * HEAD_DIM + ONES_ROWS), 2 * HEAD_DIM) for h in range(N_C_HEADS)) + tuple(
    (N_C_HEADS * (2 * HEAD_DIM + ONES_ROWS) + g * (HEAD_DIM + ONES_ROWS), HEAD_DIM) for g in range(N_D_KV))
VT_ROWS = _VT_LAYOUT[-1][0] + HEAD_DIM + ONES_ROWS

LANES = 128
TM = 256
MOE_BM = 256
VMEM_LIMIT = 56 * 1024 * 1024


def _dot(a, b):
    return jnp.dot(a, b, preferred_element_type=F32)


def _dot_nt(a, b):
    return lax.dot_general(a, b, (((1,), (1,)), ((), ())), preferred_element_type=F32)


def _cparams(sem):
    return pltpu.CompilerParams(dimension_semantics=sem, vmem_limit_bytes=VMEM_LIMIT)


def _layernorm(z, g, b):
    mu = jnp.mean(z, axis=-1, keepdims=True)
    d = z - mu
    var = jnp.mean(d * d, axis=-1, keepdims=True)
    return d * lax.rsqrt(var + LN_EPS) * g + b


def _rope128(x, cos, sin_signed):
    lane = lax.broadcasted_iota(jnp.int32, x.shape, 1)
    lo = (lane % 32) < 16
    rot = jnp.where(lo, pltpu.roll(x, LANES - 16, 1), pltpu.roll(x, 16, 1))
    return x * cos + rot * sin_signed


def _rmsnorm128(x, g, bd):
    x2 = x * x
    hi = x2.astype(BF16)
    lo = (x2 - hi.astype(F32)).astype(BF16)
    ms = _dot(hi, bd) + _dot(lo, bd)
    return x * lax.rsqrt(ms + RMS_EPS) * g


def _modulated(x_ref, mlat_ref, mctx_ref, is_ctx, d):
    m = jnp.where(is_ctx, mctx_ref[...], mlat_ref[0])
    return (x_ref[0] * (1.0 + m[:, d:2 * d]) + m[:, 0:d]).astype(BF16)


def _mod_kernel(c_ref, w_ref, b_ref, o_ref):
    c = c_ref[...]
    s = (c / (1.0 + jnp.exp(-c))).astype(BF16)
    o_ref[0] = _dot(s, w_ref[0].astype(BF16)) + b_ref[0]


def _mod_vectors(c, c_ctx, mod_w, mod_b):
    depth, d, n = mod_w.shape
    b = c.shape[0]
    rows = 8 * (-(-(b + 1) // 8))
    cs = jnp.zeros((rows, d), F32).at[:b].set(c).at[b].set(c_ctx)
    tn = 1536
    out = pl.pallas_call(
        _mod_kernel,
        grid=(depth, n // tn),
        in_specs=[pl.BlockSpec((rows, d), lambda l, j: (0, 0)),
                  pl.BlockSpec((1, d, tn), lambda l, j: (l, 0, j)),
                  pl.BlockSpec((1, 1, tn), lambda l, j: (l, 0, j))],
        out_specs=pl.BlockSpec((1, rows, tn), lambda l, j: (l, 0, j)),
        out_shape=jax.ShapeDtypeStruct((depth, rows, n), F32),
        compiler_params=_cparams(("parallel", "parallel")),
        name="mod_vectors",
    )(cs, mod_w, mod_b.reshape(depth, 1, n))
    return out


def _proj_ab_kernel(nct, x_ref, mlat_ref, mctx_ref, w_ref, wvt_ref, cos_ref, sin_ref, cs_ref,
                    uw_ref, q_ref, k_ref, vt_ref):
    is_ctx = pl.program_id(1) < nct
    d = x_ref.shape[-1]
    h = _modulated(x_ref, mlat_ref, mctx_ref, is_ctx, d)
    p = _dot(h, w_ref[...])
    uw_ref[0] = _dot(p[:, 0:256].astype(BF16), cs_ref[...]).astype(BF16)
    cos = cos_ref[...]
    sin = sin_ref[...]
    for j in range(6):
        c0 = 256 + j * LANES
        q_ref[0, :, j * LANES:(j + 1) * LANES] = (_rope128(p[:, c0:c0 + LANES], cos, sin) * QSCALE).astype(BF16)
    for j in range(4):
        c0 = 1024 + j * LANES
        k_ref[0, :, j * LANES:(j + 1) * LANES] = _rope128(p[:, c0:c0 + LANES], cos, sin).astype(BF16)
    vt = _dot_nt(wvt_ref[...], h).astype(BF16)
    for c in range(TM // LANES):
        vt_ref[0, c] = vt[:, c * LANES:(c + 1) * LANES]


def _proj_ab(x_all, mlat, mctx, w, wvt, cos, sin, cs, nct):
    b, tall, d = x_all.shape
    nt = tall // TM
    nw = w.shape[1]
    nchunk = tall // LANES
    cpt = TM // LANES
    return pl.pallas_call(
        functools.partial(_proj_ab_kernel, nct),
        grid=(b, nt),
        in_specs=[pl.BlockSpec((1, TM, d), lambda bi, i: (bi, i, 0)),
                  pl.BlockSpec((1, 1, N_MOD * d), lambda bi, i: (bi, 0, 0)),
                  pl.BlockSpec((1, N_MOD * d), lambda bi, i: (0, 0)),
                  pl.BlockSpec((d, nw), lambda bi, i: (0, 0)),
                  pl.BlockSpec((256, d), lambda bi, i: (0, 0)),
                  pl.BlockSpec((TM, LANES), lambda bi, i: (i, 0)),
                  pl.BlockSpec((TM, LANES), lambda bi, i: (i, 0)),
                  pl.BlockSpec((256, 512), lambda bi, i: (0, 0))],
        out_specs=[pl.BlockSpec((1, TM, 512), lambda bi, i: (bi, i, 0)),
                   pl.BlockSpec((1, TM, 768), lambda bi, i: (bi, i, 0)),
                   pl.BlockSpec((1, TM, 512), lambda bi, i: (bi, i, 0)),
                   pl.BlockSpec((1, cpt, 256, LANES), lambda bi, i: (bi, i, 0, 0))],
        out_shape=[jax.ShapeDtypeStruct((b, tall, 512), BF16),
                   jax.ShapeDtypeStruct((b, tall, 768), BF16),
                   jax.ShapeDtypeStruct((b, tall, 512), BF16),
                   jax.ShapeDtypeStruct((b, nchunk, 256, LANES), BF16)],
        compiler_params=_cparams(("parallel", "parallel")),
        name="proj_ab",
    )(x_all, mlat, mctx, w, wvt, cos, sin, cs)


def _proj_cd_kernel(nct, x_ref, mlat_ref, mctx_ref, w_ref, wvt_ref, cos_ref, sin_ref, bd_ref,
                    gq_ref, gk_ref, qc_ref, qd_ref, kc_ref, kd_ref, vt_ref):
    is_ctx = pl.program_id(1) < nct
    d = x_ref.shape[-1]
    h = _modulated(x_ref, mlat_ref, mctx_ref, is_ctx, d)
    p = _dot(h, w_ref[...])
    cos = cos_ref[...]
    sin = sin_ref[...]
    bd = bd_ref[...]
    gq = gq_ref[...]
    gk = gk_ref[...]
    for j in range(4):
        sl = slice(j * LANES, (j + 1) * LANES)
        qc_ref[0, :, sl] = (_rope128(p[:, j * LANES:(j + 1) * LANES], cos, sin) * QSCALE).astype(BF16)
        c0 = 512 + j * LANES
        qd_ref[0, :, sl] = (_rope128(_rmsnorm128(p[:, c0:c0 + LANES], gq, bd), cos, sin) * QSCALE).astype(BF16)
        c0 = 1024 + j * LANES
        kc_ref[0, :, sl] = _rope128(p[:, c0:c0 + LANES], cos, sin).astype(BF16)
    for j in range(2):
        c0 = 1536 + j * LANES
        kd_ref[0, :, j * LANES:(j + 1) * LANES] = _rope128(
            _rmsnorm128(p[:, c0:c0 + LANES], gk, bd), cos, sin).astype(BF16)
    vt = _dot_nt(wvt_ref[...], h).astype(BF16)
    ones = jnp.ones((ONES_ROWS, vt.shape[1]), BF16)
    src = 0
    for dst, dv in _VT_LAYOUT:
        vt_ref[0, 0, dst:dst + dv, :] = vt[src:src + dv, :]
        vt_ref[0, 0, dst + dv:dst + dv + ONES_ROWS, :] = ones
        src += dv


def _proj_cd(x_all, mlat, mctx, w, wvt, cos, sin, bd, gq, gk, nct):
    b, tall, d = x_all.shape
    nt = tall // TM
    nw = w.shape[1]
    nv = VT_ROWS
    full = lambda bi, i: (0, 0)
    tile = lambda bi, i: (bi, i, 0)
    return pl.pallas_call(
        functools.partial(_proj_cd_kernel, nct),
        grid=(b, nt),
        in_specs=[pl.BlockSpec((1, TM, d), tile),
                  pl.BlockSpec((1, 1, N_MOD * d), lambda bi, i: (bi, 0, 0)),
                  pl.BlockSpec((1, N_MOD * d), full),
                  pl.BlockSpec((d, nw), full),
                  pl.BlockSpec(wvt.shape, full),
                  pl.BlockSpec((TM, LANES), lambda bi, i: (i, 0)),
                  pl.BlockSpec((TM, LANES), lambda bi, i: (i, 0)),
                  pl.BlockSpec((LANES, LANES), full),
                  pl.BlockSpec((1, LANES), full),
                  pl.BlockSpec((1, LANES), full)],
        out_specs=[pl.BlockSpec((1, TM, 512), tile),
                   pl.BlockSpec((1, TM, 512), tile),
                   pl.BlockSpec((1, TM, 512), tile),
                   pl.BlockSpec((1, TM, 256), tile),
                   pl.BlockSpec((1, 1, nv, TM), lambda bi, i: (bi, i, 0, 0))],
        out_shape=[jax.ShapeDtypeStruct((b, tall, 512), BF16),
                   jax.ShapeDtypeStruct((b, tall, 512), BF16),
                   jax.ShapeDtypeStruct((b, tall, 512), BF16),
                   jax.ShapeDtypeStruct((b, tall, 256), BF16),
                   jax.ShapeDtypeStruct((b, nt, nv, TM), BF16)],
        compiler_params=_cparams(("parallel", "parallel")),
        name="proj_cd",
    )(x_all, mlat, mctx, w, wvt, cos, sin, bd, gq, gk)


def _fourier_kernel(nb, scale, uw_ref, ca_ref, sa_ref, cb_ref, sb_ref, o_ref):
    j = pl.program_id(0)
    ca = ca_ref[pl.ds(j, 1), :]
    sa = sa_ref[pl.ds(j, 1), :]
    cb = cb_ref[...]
    sb = sb_ref[...]
    ct = (ca * cb - sa * sb).astype(BF16)
    nst = (-(sa * cb + ca * sb)).astype(BF16)
    for bi in range(nb):
        acc = _dot(ct, uw_ref[bi, :, 0:256]) + _dot(nst, uw_ref[bi, :, 256:512])
        o_ref[bi] = (acc * scale).astype(BF16)


def _dft_tables(t, tmf):
    k = np.arange(t, dtype=np.int64)
    j1 = np.arange(t // tmf, dtype=np.int64) * tmf
    j0 = np.arange(tmf, dtype=np.int64)
    aa = (2.0 * np.pi / t) * ((j1[:, None] * k[None, :]) % t)
    ab = (2.0 * np.pi / t) * ((j0[:, None] * k[None, :]) % t)
    f = lambda a: jnp.asarray(a, F32)
    return f(np.cos(aa)), f(np.sin(aa)), f(np.cos(ab)), f(np.sin(ab))


def _fourier(uw):
    b, t, _ = uw.shape
    tmf = min(128, t)
    ca, sa, cb, sb = _dft_tables(t, tmf)
    scale = 1.0 / math.sqrt(t * HEAD_DIM)
    full2 = lambda j: (0, 0)
    return pl.pallas_call(
        functools.partial(_fourier_kernel, b, scale),
        grid=(t // tmf,),
        in_specs=[pl.BlockSpec((b, t, 512), lambda j: (0, 0, 0)),
                  pl.BlockSpec((t // tmf, t), full2),
                  pl.BlockSpec((t // tmf, t), full2),
                  pl.BlockSpec((tmf, t), full2),
                  pl.BlockSpec((tmf, t), full2)],
        out_specs=pl.BlockSpec((b, tmf, 256), lambda j: (0, j, 0)),
        out_shape=jax.ShapeDtypeStruct((b, t, 256), BF16),
        compiler_params=_cparams(("parallel",)),
        name="fourier_mix",
    )(uw, ca, sa, cb, sb)


def _win_attn_kernel(nct, nchunk, sink_ref, q_ref, k_ref, vt_ref, o_ref):
    i = pl.program_id(1)
    is_ctx = i < nct
    j = i - nct
    ctx_rows = nct * TM
    ctx_chunks = ctx_rows // LANES
    lw = TM + 2 * WINDOW
    lchunks = lw // LANES
    cs = jnp.clip(ctx_chunks + (TM // LANES) * j - WINDOW // LANES, 0, nchunk - lchunks)
    rs = pl.multiple_of(cs * LANES, LANES)
    qpos = j * TM + lax.broadcasted_iota(jnp.int32, (1, TM), 1)
    kpos = cs * LANES - ctx_rows + lax.broadcasted_iota(jnp.int32, (lw, 1), 0)
    allowed = (jnp.abs(qpos - kpos) <= WINDOW) & (kpos >= 0) & jnp.logical_not(is_ctx)
    lane = lax.broadcasted_iota(jnp.int32, (TM, LANES), 1)
    n_heads = q_ref.shape[-1] // HEAD_DIM
    group = n_heads // (k_ref.shape[-1] // LANES)
    for pair in range(n_heads // 2):
        qp = q_ref[0, :, pair * LANES:(pair + 1) * LANES]
        outs = []
        for half in range(2):
            hq = 2 * pair + half
            g = hq // group
            qm = jnp.where((lane >= HEAD_DIM) == (half == 1), qp, jnp.zeros_like(qp))
            kc = k_ref[0, 0:ctx_rows, g * LANES:(g + 1) * LANES]
            kl = k_ref[0, pl.ds(rs, lw), g * LANES:(g + 1) * LANES]
            s_c = _dot_nt(kc, qm)
            s_l = jnp.where(allowed, _dot_nt(kl, qm), NEG)
            sk = sink_ref[hq] * LOG2E
            m = jnp.maximum(jnp.maximum(jnp.max(s_c, axis=0, keepdims=True),
                                        jnp.max(s_l, axis=0, keepdims=True)), sk)
            p_c = jnp.exp2(s_c - m)
            p_l = jnp.exp2(s_l - m)
            l = (jnp.sum(p_c, axis=0, keepdims=True) + jnp.sum(p_l, axis=0, keepdims=True)
                 + jnp.exp2(sk - m))
            p_c = p_c.astype(BF16)
            p_l = p_l.astype(BF16)
            vrows = slice(g * HEAD_DIM, (g + 1) * HEAD_DIM)
            acc = jnp.zeros((HEAD_DIM, TM), F32)
            for c in range(ctx_chunks):
                acc = acc + _dot(vt_ref[0, c, vrows, :], p_c[c * LANES:(c + 1) * LANES, :])
            for c in range(lchunks):
                acc = acc + _dot(vt_ref[0, cs + c, vrows, :], p_l[c * LANES:(c + 1) * LANES, :])
            outs.append(acc / l)
        o2 = jnp.concatenate(outs, axis=0)
        o_ref[0, :, pair * LANES:(pair + 1) * LANES] = o2.T.astype(BF16)


def _win_attn(sink, q, k, vt, nct):
    b, tall, qw = q.shape
    nt = tall // TM
    nchunk = vt.shape[1]
    return pl.pallas_call(
        functools.partial(_win_attn_kernel, nct, nchunk),
        grid=(b, nt),
        in_specs=[pl.BlockSpec(memory_space=pltpu.SMEM),
                  pl.BlockSpec((1, TM, qw), lambda bi, i: (bi, i, 0)),
                  pl.BlockSpec((1, tall, k.shape[-1]), lambda bi, i: (bi, 0, 0)),
                  pl.BlockSpec((1, nchunk, vt.shape[2], LANES), lambda bi, i: (bi, 0, 0, 0))],
        out_specs=pl.BlockSpec((1, TM, qw), lambda bi, i: (bi, i, 0)),
        out_shape=jax.ShapeDtypeStruct((b, tall, qw), BF16),
        compiler_params=_cparams(("parallel", "parallel")),
        name="window_attention",
    )(sink, q, k, vt)


def _unit_operands(u):
    if u < 2 * N_C_HEADS:
        hc = u // 2
        dst, dv = _VT_LAYOUT[hc]
        return True, hc, dst, dv
    g = (u - 2 * N_C_HEADS) // (N_D_HEADS // N_D_KV)
    dst, dv = _VT_LAYOUT[N_C_HEADS + g]
    return False, g, dst, dv


def _attn_cd_kernel(nkt, lam_init, lam_ref, sg_ref, qc_ref, qd_ref, kc_ref, kd_ref, vt_ref, o_ref,
                    qm_s, m_s, acc_s, s_a, s_b):
    mq = qc_ref.shape[1]
    tk = vt_ref.shape[-1]
    lane = lax.broadcasted_iota(jnp.int32, (mq, LANES), 1)
    upper = lane >= HEAD_DIM
    for u in range(N_UNITS):
        src = qc_ref if u < 2 * N_C_HEADS else qd_ref
        pair = (u if u < 2 * N_C_HEADS else u - 2 * N_C_HEADS) // 2
        qp = src[0, :, pair * LANES:(pair + 1) * LANES]
        qm_s[u] = jnp.where(upper == (u % 2 == 1), qp, jnp.zeros_like(qp))
    m_s[...] = jnp.full(m_s.shape, NEG, F32)
    acc_s[...] = jnp.zeros(acc_s.shape, F32)

    def scores(tile, u, dst_s):
        is_c, slab, _, _ = _unit_operands(u)
        k_ref = kc_ref if is_c else kd_ref
        r0 = pl.multiple_of(tile * tk, tk)
        dst_s[u] = _dot_nt(k_ref[0, pl.ds(r0, tk), slab * LANES:(slab + 1) * LANES], qm_s[u])

    def consume(tile, u, src_s):
        _, _, row0, dv = _unit_operands(u)
        rows = dv + ONES_ROWS
        s = src_s[u]
        m_old = m_s[u]
        m_new = jnp.maximum(m_old, jnp.max(s, axis=0, keepdims=True))
        alpha = jnp.exp2(m_old - m_new)
        p = jnp.exp2(s - m_new).astype(BF16)
        pv = _dot(vt_ref[0, tile, row0:row0 + rows, :], p)
        acc_s[u, 0:rows, :] = acc_s[u, 0:rows, :] * alpha + pv
        m_s[u] = m_new

    def stage(tile, src_s, dst_s):
        nxt = jnp.minimum(tile + 1, nkt - 1)
        for u in range(N_UNITS):
            scores(nxt, u, dst_s)
            consume(tile, u, src_s)

    for u in range(N_UNITS):
        scores(0, u, s_a)
    stage(0, s_a, s_b)

    def body(pr, carry):
        stage(2 * pr + 1, s_b, s_a)
        stage(2 * pr + 2, s_a, s_b)
        return carry

    lax.fori_loop(0, (nkt - 1) // 2, body, 0)
    if (nkt - 1) % 2 == 1:
        stage(nkt - 1, s_b, s_a)

    lv = lam_ref[...]
    lam = (jnp.exp(jnp.sum(lv[0:1] * lv[1:2], axis=-1, keepdims=True))
           - jnp.exp(jnp.sum(lv[2:3] * lv[3:4], axis=-1, keepdims=True)) + lam_init)
    sg = sg_ref[...]
    dvc = 2 * HEAD_DIM
    for hc in range(N_C_HEADS):
        o1 = acc_s[2 * hc, 0:dvc, :] / acc_s[2 * hc, dvc:dvc + 1, :]
        o2 = acc_s[2 * hc + 1, 0:dvc, :] / acc_s[2 * hc + 1, dvc:dvc + 1, :]
        o = o1 - lam * o2
        ms = jnp.mean(o * o, axis=0, keepdims=True)
        o = o * lax.rsqrt(ms + RMS_EPS) * sg * (1.0 - lam_init)
        o_ref[0, :, hc * LANES:(hc + 1) * LANES] = o.T.astype(BF16)
    for pair in range(N_D_HEADS // 2):
        outs = []
        for half in range(2):
            u = 2 * N_C_HEADS + 2 * pair + half
            outs.append(acc_s[u, 0:HEAD_DIM, :] / acc_s[u, HEAD_DIM:HEAD_DIM + 1, :])
        o2 = jnp.concatenate(outs, axis=0)
        c0 = N_C_HEADS * LANES + pair * LANES
        o_ref[0, :, c0:c0 + LANES] = o2.T.astype(BF16)


def _attn_cd(lam_vec, sg, qc, qd, kc, kd, vt, nct, lam_init):
    b, tall, _ = qc.shape
    nkt, nv, tk = vt.shape[1], vt.shape[2], vt.shape[3]
    mq = TM
    nq = tall // mq - nct
    ow = N_C_HEADS * LANES + N_D_HEADS * HEAD_DIM
    qtile = lambda bi, i: (bi, i + nct, 0)
    return pl.pallas_call(
        functools.partial(_attn_cd_kernel, nkt, lam_init),
        grid=(b, nq),
        in_specs=[pl.BlockSpec((4, HEAD_DIM), lambda bi, i: (0, 0)),
                  pl.BlockSpec((LANES, 1), lambda bi, i: (0, 0)),
                  pl.BlockSpec((1, mq, qc.shape[-1]), qtile),
                  pl.BlockSpec((1, mq, qd.shape[-1]), qtile),
                  pl.BlockSpec((1, tall, kc.shape[-1]), lambda bi, i: (bi, 0, 0)),
                  pl.BlockSpec((1, tall, kd.shape[-1]), lambda bi, i: (bi, 0, 0)),
                  pl.BlockSpec((1, nkt, nv, tk), lambda bi, i: (bi, 0, 0, 0))],
        out_specs=pl.BlockSpec((1, mq, ow), lambda bi, i: (bi, i, 0)),
        out_shape=jax.ShapeDtypeStruct((b, nq * mq, ow), BF16),
        scratch_shapes=[pltpu.VMEM((N_UNITS, mq, LANES), BF16),
                        pltpu.VMEM((N_UNITS, 1, mq), F32),
                        pltpu.VMEM((N_UNITS, 2 * HEAD_DIM + ONES_ROWS, mq), F32),
                        pltpu.VMEM((N_UNITS, tk, mq), F32),
                        pltpu.VMEM((N_UNITS, tk, mq), F32)],
        compiler_params=_cparams(("parallel", "parallel")),
        name="attention_cd",
    )(lam_vec, sg, qc, qd, kc, kd, vt)


def _oproj_kernel(n_parts, widths, nct, alpha, n_exp, *refs):
    o_refs = refs[:n_parts]
    (w_ref, x_ref, mlat_ref, mctx_ref, lng_ref, lnb_ref, rwt_ref, rb_ref,
     x1_ref, h2_ref, te_ref, tg_ref) = refs[n_parts:]
    d = x_ref.shape[-1]
    is_ctx = pl.program_id(1) < nct
    y = None
    r0 = 0
    for o_ref, wd in zip(o_refs, widths):
        part = _dot(o_ref[0], w_ref[r0:r0 + wd, :])
        y = part if y is None else y + part
        r0 += wd
    m = jnp.where(is_ctx, mctx_ref[...], mlat_ref[0])
    x1 = _layernorm(alpha * x_ref[0] + m[:, 2 * d:3 * d] * y, lng_ref[...], lnb_ref[...])
    x1_ref[0] = x1
    h2 = (x1 * (1.0 + m[:, 4 * d:5 * d]) + m[:, 3 * d:4 * d]).astype(BF16)
    h2_ref[0] = h2
    logits = _dot_nt(rwt_ref[...], h2) + rb_ref[...]
    tm = logits.shape[1]
    row = lax.broadcasted_iota(jnp.int32, logits.shape, 0)
    vals, idxs = [], []
    for _ in range(TOP_K):
        mx = jnp.max(logits, axis=0, keepdims=True)
        idx = jnp.min(jnp.where(logits == mx, row, n_exp), axis=0, keepdims=True)
        vals.append(mx)
        idxs.append(idx)
        logits = jnp.where(row == idx, NEG, logits)
    es = [jnp.exp(v - vals[0]) for v in vals]
    tot = es[0] + es[1] + es[2] + es[3]
    pad_i = jnp.zeros((8 - TOP_K, tm), jnp.int32)
    pad_f = jnp.zeros((8 - TOP_K, tm), F32)
    te_ref[0] = jnp.concatenate(idxs + [pad_i], axis=0)
    tg_ref[0] = jnp.concatenate([e / tot for e in es] + [pad_f], axis=0)


def _oproj(parts, w_out, x_all, mlat, mctx, lng, lnb, rwt, rb, nct_out, x_tile_off, alpha):
    b, t, _ = parts[0].shape
    d = x_all.shape[-1]
    nt = t // TM
    n_exp = rwt.shape[0]
    widths = tuple(p.shape[-1] for p in parts)
    tile = lambda bi, i: (bi, i, 0)
    full = lambda bi, i: (0, 0)
    in_specs = [pl.BlockSpec((1, TM, wd), tile) for wd in widths] + [
        pl.BlockSpec(w_out.shape, full),
        pl.BlockSpec((1, TM, d), lambda bi, i: (bi, i + x_tile_off, 0)),
        pl.BlockSpec((1, 1, N_MOD * d), lambda bi, i: (bi, 0, 0)),
        pl.BlockSpec((1, N_MOD * d), full),
        pl.BlockSpec((1, d), full),
        pl.BlockSpec((1, d), full),
        pl.BlockSpec((n_exp, d), full),
        pl.BlockSpec((n_exp, 1), full)]
    return pl.pallas_call(
        functools.partial(_oproj_kernel, len(parts), widths, nct_out, alpha, n_exp),
        grid=(b, nt),
        in_specs=in_specs,
        out_specs=[pl.BlockSpec((1, TM, d), tile),
                   pl.BlockSpec((1, TM, d), tile),
                   pl.BlockSpec((1, 8, TM), lambda bi, i: (bi, 0, i)),
                   pl.BlockSpec((1, 8, TM), lambda bi, i: (bi, 0, i))],
        out_shape=[jax.ShapeDtypeStruct((b, t, d), F32),
                   jax.ShapeDtypeStruct((b, t, d), BF16),
                   jax.ShapeDtypeStruct((b, 8, t), jnp.int32),
                   jax.ShapeDtypeStruct((b, 8, t), F32)],
        compiler_params=_cparams(("parallel", "parallel")),
        name="out_proj_norm_router",
    )(*parts, w_out, x_all, mlat, mctx, lng, lnb, rwt, rb)


def _moe_kernel(be_ref, nu_ref, x_ref, wgu_ref, bgu_ref, wd_ref, bd_ref, y_ref, wgu_s, wd_s):
    blk = pl.program_id(0)
    ff = wd_ref.shape[1]

    @pl.when(blk < nu_ref[0])
    def _():
        prev = be_ref[jnp.maximum(blk - 1, 0)]

        @pl.when((blk == 0) | (be_ref[blk] != prev))
        def _():
            rows = 128

            def cast_gu(r, c):
                r0 = pl.multiple_of(r * rows, rows)
                wgu_s[pl.ds(r0, rows), :] = wgu_ref[0, pl.ds(r0, rows), :].astype(BF16)
                return c

            def cast_d(r, c):
                r0 = pl.multiple_of(r * rows, rows)
                wd_s[pl.ds(r0, rows), :] = wd_ref[0, pl.ds(r0, rows), :].astype(BF16)
                return c

            lax.fori_loop(0, wgu_s.shape[0] // rows, cast_gu, 0)
            lax.fori_loop(0, wd_s.shape[0] // rows, cast_d, 0)

        gu = _dot(x_ref[...], wgu_s[...]) + bgu_ref[0]
        g = jnp.minimum(gu[:, :ff], SWIGLU_LIMIT)
        u = jnp.clip(gu[:, ff:], -SWIGLU_LIMIT, SWIGLU_LIMIT)
        act = (u + 1.0) * (g / (1.0 + jnp.exp(-SWIGLU_ALPHA * g)))
        y_ref[...] = _dot(act.astype(BF16), wd_s[...]) + bd_ref[0]


def _moe_experts(block_e, n_used, xb, w_gu, b_gu, w_down, b_down):
    n_slots, d = xb.shape
    n_exp, _, ff2 = w_gu.shape
    ff = w_down.shape[1]
    nb = n_slots // MOE_BM
    rowblk = lambda i, be, nu: (jnp.minimum(i, nu[0] - 1), 0)
    wblk = lambda i, be, nu: (be[i], 0, 0)
    grid_spec = pltpu.PrefetchScalarGridSpec(
        num_scalar_prefetch=2,
        grid=(nb,),
        in_specs=[pl.BlockSpec((MOE_BM, d), rowblk),
                  pl.BlockSpec((1, d, ff2), wblk),
                  pl.BlockSpec((1, 1, ff2), wblk),
                  pl.BlockSpec((1, ff, d), wblk),
                  pl.BlockSpec((1, 1, d), wblk)],
        out_specs=pl.BlockSpec((MOE_BM, d), rowblk),
        scratch_shapes=[pltpu.VMEM((d, ff2), BF16), pltpu.VMEM((ff, d), BF16)])
    return pl.pallas_call(
        _moe_kernel,
        grid_spec=grid_spec,
        out_shape=jax.ShapeDtypeStruct((n_slots, d), F32),
        compiler_params=_cparams(("arbitrary",)),
        name="moe_experts",
    )(block_e, n_used, xb, w_gu, b_gu.reshape(n_exp, 1, ff2), w_down, b_down.reshape(n_exp, 1, d))


def _route(top_e, n_exp):
    n_tok, k = top_e.shape
    n_asg = n_tok * k
    e_flat = top_e.reshape(n_asg)
    onehot = (e_flat[:, None] == jnp.arange(n_exp, dtype=jnp.int32)[None, :]).astype(jnp.int32)
    csum = jnp.cumsum(onehot, axis=0)
    rank = jnp.take_along_axis(csum, e_flat[:, None], axis=1)[:, 0] - 1
    counts = csum[-1]
    padded = (counts + MOE_BM - 1) // MOE_BM * MOE_BM
    pend = jnp.cumsum(padded)
    pstart = pend - padded
    dest = (pstart[e_flat] + rank).astype(jnp.int32)
    n_blocks = -(-n_asg // MOE_BM) + n_exp
    n_used = (pend[-1] // MOE_BM).astype(jnp.int32)
    tok_of_slot = jnp.zeros((n_blocks * MOE_BM,), jnp.int32).at[dest].set(
        jnp.arange(n_asg, dtype=jnp.int32) // k)
    blk = jnp.minimum(jnp.arange(n_blocks, dtype=jnp.int32), n_used - 1) * MOE_BM
    block_e = jnp.minimum(jnp.searchsorted(pend, blk, side='right'), n_exp - 1).astype(jnp.int32)
    return dest, tok_of_slot, block_e, n_used.reshape(1)


def _postnorm2_kernel(nct, alpha, x1_ref, f_ref, mlat_ref, mctx_ref, lng_ref, lnb_ref, o_ref):
    d = x1_ref.shape[-1]
    is_ctx = pl.program_id(1) < nct
    m = jnp.where(is_ctx, mctx_ref[...], mlat_ref[0])
    o_ref[0] = _layernorm(alpha * x1_ref[0] + m[:, 5 * d:6 * d] * f_ref[0], lng_ref[...], lnb_ref[...])


def _postnorm2(x1, f, mlat, mctx, lng, lnb, nct, alpha):
    b, t, d = x1.shape
    tile = lambda bi, i: (bi, i, 0)
    full = lambda bi, i: (0, 0)
    return pl.pallas_call(
        functools.partial(_postnorm2_kernel, nct, alpha),
        grid=(b, t // TM),
        in_specs=[pl.BlockSpec((1, TM, d), tile),
                  pl.BlockSpec((1, TM, d), tile),
                  pl.BlockSpec((1, 1, N_MOD * d), lambda bi, i: (bi, 0, 0)),
                  pl.BlockSpec((1, N_MOD * d), full),
                  pl.BlockSpec((1, d), full),
                  pl.BlockSpec((1, d), full)],
        out_specs=pl.BlockSpec((1, TM, d), tile),
        out_shape=jax.ShapeDtypeStruct((b, t, d), F32),
        compiler_params=_cparams(("parallel", "parallel")),
        name="post_norm_ffn",
    )(x1, f, mlat, mctx, lng, lnb)


def _rope_tables(ctx_len, seq):
    t = np.arange(seq)
    row = (t // GRID_W).astype(np.float64)
    col = (t % GRID_W).astype(np.float64)
    nf = HEAD_DIM // 4
    inv = ROPE_THETA ** (-np.arange(nf, dtype=np.float64) / nf)
    ar = row[:, None] * inv[None, :]
    ac = col[:, None] * inv[None, :]
    ang = np.concatenate([ar, ar, ac, ac], axis=-1)
    cos = np.concatenate([np.ones((ctx_len, HEAD_DIM)), np.cos(ang)], axis=0)
    sin = np.concatenate([np.zeros((ctx_len, HEAD_DIM)), np.sin(ang)], axis=0)
    sign = np.where((np.arange(HEAD_DIM) % 32) < 16, -1.0, 1.0)[None, :]
    cos2 = np.tile(cos, (1, LANES // HEAD_DIM))
    sin2 = np.tile(sin * sign, (1, LANES // HEAD_DIM))
    return jnp.asarray(cos2, F32), jnp.asarray(sin2, F32)


def _channel_dft():
    c = np.arange(HEAD_DIM)
    ang = 2.0 * np.pi * ((c[:, None] * c[None, :]) % HEAD_DIM) / HEAD_DIM
    eye = np.eye(4)
    cs = np.concatenate([np.kron(eye, np.cos(ang)), np.kron(eye, np.sin(ang))], axis=1)
    return jnp.asarray(cs, BF16)


def _group_mean_matrix():
    bd = np.kron(np.eye(LANES // HEAD_DIM), np.full((HEAD_DIM, HEAD_DIM), 1.0 / HEAD_DIM))
    return jnp.asarray(bd, BF16)


def _dup_heads(wk, n_heads):
    d = wk.shape[0]
    return jnp.broadcast_to(wk.reshape(d, n_heads, 1, HEAD_DIM), (d, n_heads, 2, HEAD_DIM)).reshape(d, n_heads * LANES)


def _lambda_init(layer):
    return 0.8 - 0.6 * math.exp(-0.3 * layer)


def _moe_layer(h2, te, tg, router_n_exp, w_gu, b_gu, w_down, b_down):
    b, t, d = h2.shape
    top_e = jnp.transpose(te[:, :TOP_K, :], (0, 2, 1)).reshape(b * t, TOP_K)
    gates = jnp.transpose(tg[:, :TOP_K, :], (0, 2, 1)).reshape(b * t, TOP_K)
    dest, tok_of_slot, block_e, n_used = _route(top_e, router_n_exp)
    xb = h2.reshape(b * t, d)[tok_of_slot]
    yb = _moe_experts(block_e, n_used, xb, w_gu, b_gu, w_down, b_down)
    y = yb[dest].reshape(b * t, TOP_K, d)
    return jnp.sum(gates[:, :, None] * y, axis=1).reshape(b, t, d)


def kernel(x, c, ctx, c_ctx, mod_w, mod_b, ln_g, ln_b, ab_w_in, ab_sink, ab_w_out,
           cd_w_in, cd_lambda, cd_subln_g, cd_q_norm_g, cd_k_norm_g, cd_w_out,
           router_w, router_b, expert_w_gu, expert_b_gu, expert_w_down, expert_b_down):
    b, s, d = x.shape
    n_ctx = ctx.shape[1]
    depth = mod_w.shape[0]
    n_exp = router_w.shape[-1]
    assert d == 16 * HEAD_DIM and n_ctx % TM == 0 and s % TM == 0 and s % GRID_W == 0
    nct = n_ctx // TM
    alpha = (2 * depth) ** 0.25

    cos, sin = _rope_tables(n_ctx, s)
    cs_dft = _channel_dft()
    bd = _group_mean_matrix()
    mods = _mod_vectors(c, c_ctx, mod_w, mod_b)

    x_all = jnp.concatenate([ctx, x], axis=1)
    for l in range(depth):
        last = l == depth - 1
        i = l // 2
        mlat = mods[l, :b].reshape(b, 1, N_MOD * d)
        mctx = mods[l, b:b + 1]
        lng1, lnb1 = ln_g[l, 0].reshape(1, d), ln_b[l, 0].reshape(1, d)
        lng2, lnb2 = ln_g[l, 1].reshape(1, d), ln_b[l, 1].reshape(1, d)
        rwt = router_w[l].T.astype(BF16)
        rb = router_b[l].reshape(n_exp, 1)
        if l % 2 == 0:
            w = ab_w_in[i]
            w_n = jnp.concatenate([w[:, :1024], _dup_heads(w[:, 1024:1280], 4)], axis=1).astype(BF16)
            wvt = w[:, 1280:1536].T.astype(BF16)
            uw, q, k, vt = _proj_ab(x_all, mlat, mctx, w_n, wvt, cos, sin, cs_dft, nct)
            oa = jnp.concatenate([_fourier(uw[:, :n_ctx]), _fourier(uw[:, n_ctx:])], axis=1)
            ob = _win_attn(ab_sink[i], q, k, vt, nct)
            if last:
                parts = [oa[:, n_ctx:], ob[:, n_ctx:]]
            else:
                parts = [oa, ob]
            w_out = ab_w_out[i].astype(BF16)
        else:
            w = cd_w_in[i]
            w_n = jnp.concatenate([w[:, :1536], _dup_heads(w[:, 1536:1664], 2)], axis=1).astype(BF16)
            wvt = w[:, 1664:2304].T.astype(BF16)
            gq = jnp.tile(cd_q_norm_g[i], 2).reshape(1, LANES)
            gk = jnp.tile(cd_k_norm_g[i], 2).reshape(1, LANES)
            qc, qd, kc, kd, vt = _proj_cd(x_all, mlat, mctx, w_n, wvt, cos, sin, bd, gq, gk, nct)
            sg = cd_subln_g[i].reshape(LANES, 1)
            o_lat = _attn_cd(cd_lambda[i], sg, qc, qd, kc, kd, vt, nct, _lambda_init(l))
            if last:
                parts = [o_lat]
            else:
                raise NotImplementedError("context outputs of a differential/axial layer")
            w_out = cd_w_out[i].astype(BF16)
        if last:
            x1, h2, te, tg = _oproj(parts, w_out, x_all, mlat, mctx, lng1, lnb1, rwt, rb, 0, nct, alpha)
            f = _moe_layer(h2, te, tg, n_exp, expert_w_gu[l], expert_b_gu[l], expert_w_down[l], expert_b_down[l])
            return _postnorm2(x1, f, mlat, mctx, lng2, lnb2, 0, alpha)
        x1, h2, te, tg = _oproj(parts, w_out, x_all, mlat, mctx, lng1, lnb1, rwt, rb, nct, 0, alpha)
        f = _moe_layer(h2, te, tg, n_exp, expert_w_gu[l], expert_b_gu[l], expert_w_down[l], expert_b_down[l])
        x_all = _postnorm2(x1, f, mlat, mctx, lng2, lnb2, nct, alpha)
    return x_all[:, n_ctx:]
```

```python
import functools
import math

import jax
import jax.numpy as jnp
import numpy as np
from jax import lax
from jax.experimental import pallas as pl
from jax.experimental.pallas import tpu as pltpu

F32 = jnp.float32
BF16 = jnp.bfloat16

HEAD_DIM = 64
GRID_W = 64
WINDOW = 128
ROPE_THETA = 10000.0
LN_EPS = 1e-6
RMS_EPS = 1e-6
N_MOD = 6
TOP_K = 4
SWIGLU_LIMIT = 7.0
SWIGLU_ALPHA = 1.702
NEG = -1e30
LOG2E = 1.4426950408889634
QSCALE = HEAD_DIM ** -0.5 * LOG2E

N_C_HEADS = 4
N_D_HEADS = 8
N_D_KV = 2
N_UNITS = 2 * N_C_HEADS + N_D_HEADS
ONES_ROWS = 16
_VT_LAYOUT = tuple((h * (2 * HEAD_DIM + ONES_ROWS), 2 * HEAD_DIM) for h in range(N_C_HEADS)) + tuple(
    (N_C_HEADS * (2 * HEAD_DIM + ONES_ROWS) + g * (HEAD_DIM + ONES_ROWS), HEAD_DIM) for g in range(N_D_KV))
VT_ROWS = _VT_LAYOUT[-1][0] + HEAD_DIM + ONES_ROWS

LANES = 128
TM = 256
MOE_BM = 256
VMEM_LIMIT = 56 * 1024 * 1024


def _dot(a, b):
    return jnp.dot(a, b, preferred_element_type=F32)


def _dot_nt(a, b):
    return lax.dot_general(a, b, (((1,), (1,)), ((), ())), preferred_element_type=F32)


def _cparams(sem):
    return pltpu.CompilerParams(dimension_semantics=sem, vmem_limit_bytes=VMEM_LIMIT)


def _layernorm(z, g, b):
    mu = jnp.mean(z, axis=-1, keepdims=True)
    d = z - mu
    var = jnp.mean(d * d, axis=-1, keepdims=True)
    return d * lax.rsqrt(var + LN_EPS) * g + b


def _rope128(x, cos, sin_signed):
    lane = lax.broadcasted_iota(jnp.int32, x.shape, 1)
    lo = (lane % 32) < 16
    rot = jnp.where(lo, pltpu.roll(x, LANES - 16, 1), pltpu.roll(x, 16, 1))
    return x * cos + rot * sin_signed


def _rmsnorm128(x, g, bd):
    x2 = x * x
    hi = x2.astype(BF16)
    lo = (x2 - hi.astype(F32)).astype(BF16)
    ms = _dot(hi, bd) + _dot(lo, bd)
    return x * lax.rsqrt(ms + RMS_EPS) * g


def _modulated(x_ref, mlat_ref, mctx_ref, is_ctx, d):
    m = jnp.where(is_ctx, mctx_ref[...], mlat_ref[0])
    return (x_ref[0] * (1.0 + m[:, d:2 * d]) + m[:, 0:d]).astype(BF16)


def _mod_kernel(c_ref, w_ref, b_ref, o_ref):
    c = c_ref[...]
    s = (c / (1.0 + jnp.exp(-c))).astype(BF16)
    o_ref[0] = _dot(s, w_ref[0].astype(BF16)) + b_ref[0]


def _mod_vectors(c, c_ctx, mod_w, mod_b):
    depth, d, n = mod_w.shape
    b = c.shape[0]
    rows = 8 * (-(-(b + 1) // 8))
    cs = jnp.zeros((rows, d), F32).at[:b].set(c).at[b].set(c_ctx)
    tn = 1536
    out = pl.pallas_call(
        _mod_kernel,
        grid=(depth, n // tn),
        in_specs=[pl.BlockSpec((rows, d), lambda l, j: (0, 0)),
                  pl.BlockSpec((1, d, tn), lambda l, j: (l, 0, j)),
                  pl.BlockSpec((1, 1, tn), lambda l, j: (l, 0, j))],
        out_specs=pl.BlockSpec((1, rows, tn), lambda l, j: (l, 0, j)),
        out_shape=jax.ShapeDtypeStruct((depth, rows, n), F32),
        compiler_params=_cparams(("parallel", "parallel")),
        name="mod_vectors",
    )(cs, mod_w, mod_b.reshape(depth, 1, n))
    return out


def _proj_ab_kernel(nct, x_ref, mlat_ref, mctx_ref, w_ref, wvt_ref, cos_ref, sin_ref, cs_ref,
                    uw_ref, q_ref, k_ref, vt_ref):
    is_ctx = pl.program_id(1) < nct
    d = x_ref.shape[-1]
    h = _modulated(x_ref, mlat_ref, mctx_ref, is_ctx, d)
    p = _dot(h, w_ref[...])
    uw_ref[0] = _dot(p[:, 0:256].astype(BF16), cs_ref[...]).astype(BF16)
    cos = cos_ref[...]
    sin = sin_ref[...]
    for j in range(6):
        c0 = 256 + j * LANES
        q_ref[0, :, j * LANES:(j + 1) * LANES] = (_rope128(p[:, c0:c0 + LANES], cos, sin) * QSCALE).astype(BF16)
    for j in range(4):
        c0 = 1024 + j * LANES
        k_ref[0, :, j * LANES:(j + 1) * LANES] = _rope128(p[:, c0:c0 + LANES], cos, sin).astype(BF16)
    vt = _dot_nt(wvt_ref[...], h).astype(BF16)
    for c in range(TM // LANES):
        vt_ref[0, c] = vt[:, c * LANES:(c + 1) * LANES]


def _proj_ab(x_all, mlat, mctx, w, wvt, cos, sin, cs, nct):
    b, tall, d = x_all.shape
    nt = tall // TM
    nw = w.shape[1]
    nchunk = tall // LANES
    cpt = TM // LANES
    return pl.pallas_call(
        functools.partial(_proj_ab_kernel, nct),
        grid=(b, nt),
        in_specs=[pl.BlockSpec((1, TM, d), lambda bi, i: (bi, i, 0)),
                  pl.BlockSpec((1, 1, N_MOD * d), lambda bi, i: (bi, 0, 0)),
                  pl.BlockSpec((1, N_MOD * d), lambda bi, i: (0, 0)),
                  pl.BlockSpec((d, nw), lambda bi, i: (0, 0)),
                  pl.BlockSpec((256, d), lambda bi, i: (0, 0)),
                  pl.BlockSpec((TM, LANES), lambda bi, i: (i, 0)),
                  pl.BlockSpec((TM, LANES), lambda bi, i: (i, 0)),
                  pl.BlockSpec((256, 512), lambda bi, i: (0, 0))],
        out_specs=[pl.BlockSpec((1, TM, 512), lambda bi, i: (bi, i, 0)),
                   pl.BlockSpec((1, TM, 768), lambda bi, i: (bi, i, 0)),
                   pl.BlockSpec((1, TM, 512), lambda bi, i: (bi, i, 0)),
                   pl.BlockSpec((1, cpt, 256, LANES), lambda bi, i: (bi, i, 0, 0))],
        out_shape=[jax.ShapeDtypeStruct((b, tall, 512), BF16),
                   jax.ShapeDtypeStruct((b, tall, 768), BF16),
                   jax.ShapeDtypeStruct((b, tall, 512), BF16),
                   jax.ShapeDtypeStruct((b, nchunk, 256, LANES), BF16)],
        compiler_params=_cparams(("parallel", "parallel")),
        name="proj_ab",
    )(x_all, mlat, mctx, w, wvt, cos, sin, cs)


def _proj_cd_kernel(nct, x_ref, mlat_ref, mctx_ref, w_ref, wvt_ref, cos_ref, sin_ref, bd_ref,
                    gq_ref, gk_ref, qc_ref, qd_ref, kc_ref, kd_ref, vt_ref):
    is_ctx = pl.program_id(1) < nct
    d = x_ref.shape[-1]
    h = _modulated(x_ref, mlat_ref, mctx_ref, is_ctx, d)
    p = _dot(h, w_ref[...])
    cos = cos_ref[...]
    sin = sin_ref[...]
    bd = bd_ref[...]
    gq = gq_ref[...]
    gk = gk_ref[...]
    for j in range(4):
        sl = slice(j * LANES, (j + 1) * LANES)
        qc_ref[0, :, sl] = (_rope128(p[:, j * LANES:(j + 1) * LANES], cos, sin) * QSCALE).astype(BF16)
        c0 = 512 + j * LANES
        qd_ref[0, :, sl] = (_rope128(_rmsnorm128(p[:, c0:c0 + LANES], gq, bd), cos, sin) * QSCALE).astype(BF16)
        c0 = 1024 + j * LANES
        kc_ref[0, :, sl] = _rope128(p[:, c0:c0 + LANES], cos, sin).astype(BF16)
    for j in range(2):
        c0 = 1536 + j * LANES
        kd_ref[0, :, j * LANES:(j + 1) * LANES] = _rope128(
            _rmsnorm128(p[:, c0:c0 + LANES], gk, bd), cos, sin).astype(BF16)
    vt = _dot_nt(wvt_ref[...], h).astype(BF16)
    ones = jnp.ones((ONES_ROWS, vt.shape[1]), BF16)
    src = 0
    for dst, dv in _VT_LAYOUT:
        vt_ref[0, 0, dst:dst + dv, :] = vt[src:src + dv, :]
        vt_ref[0, 0, dst + dv:dst + dv + ONES_ROWS, :] = ones
        src += dv


def _proj_cd(x_all, mlat, mctx, w, wvt, cos, sin, bd, gq, gk, nct):
    b, tall, d = x_all.shape
    nt = tall // TM
    nw = w.shape[1]
    nv = VT_ROWS
    full = lambda bi, i: (0, 0)
    tile = lambda bi, i: (bi, i, 0)
    return pl.pallas_call(
        functools.partial(_proj_cd_kernel, nct),
        grid=(b, nt),
        in_specs=[pl.BlockSpec((1, TM, d), tile),
                  pl.BlockSpec((1, 1, N_MOD * d), lambda bi, i: (bi, 0, 0)),
                  pl.BlockSpec((1, N_MOD * d), full),
                  pl.BlockSpec((d, nw), full),
                  pl.BlockSpec(wvt.shape, full),
                  pl.BlockSpec((TM, LANES), lambda bi, i: (i, 0)),
                  pl.BlockSpec((TM, LANES), lambda bi, i: (i, 0)),
                  pl.BlockSpec((LANES, LANES), full),
                  pl.BlockSpec((1, LANES), full),
                  pl.BlockSpec((1, LANES), full)],
        out_specs=[pl.BlockSpec((1, TM, 512), tile),
                   pl.BlockSpec((1, TM, 512), tile),
                   pl.BlockSpec((1, TM, 512), tile),
                   pl.BlockSpec((1, TM, 256), tile),
                   pl.BlockSpec((1, 1, nv, TM), lambda bi, i: (bi, i, 0, 0))],
        out_shape=[jax.ShapeDtypeStruct((b, tall, 512), BF16),
                   jax.ShapeDtypeStruct((b, tall, 512), BF16),
                   jax.ShapeDtypeStruct((b, tall, 512), BF16),
                   jax.ShapeDtypeStruct((b, tall, 256), BF16),
                   jax.ShapeDtypeStruct((b, nt, nv, TM), BF16)],
        compiler_params=_cparams(("parallel", "parallel")),
        name="proj_cd",
    )(x_all, mlat, mctx, w, wvt, cos, sin, bd, gq, gk)


def _fourier_kernel(nb, scale, uw_ref, ca_ref, sa_ref, cb_ref, sb_ref, o_ref):
    j = pl.program_id(0)
    ca = ca_ref[pl.ds(j, 1), :]
    sa = sa_ref[pl.ds(j, 1), :]
    cb = cb_ref[...]
    sb = sb_ref[...]
    ct = (ca * cb - sa * sb).astype(BF16)
    nst = (-(sa * cb + ca * sb)).astype(BF16)
    for bi in range(nb):
        acc = _dot(ct, uw_ref[bi, :, 0:256]) + _dot(nst, uw_ref[bi, :, 256:512])
        o_ref[bi] = (acc * scale).astype(BF16)


def _dft_tables(t, tmf):
    k = np.arange(t, dtype=np.int64)
    j1 = np.arange(t // tmf, dtype=np.int64) * tmf
    j0 = np.arange(tmf, dtype=np.int64)
    aa = (2.0 * np.pi / t) * ((j1[:, None] * k[None, :]) % t)
    ab = (2.0 * np.pi / t) * ((j0[:, None] * k[None, :]) % t)
    f = lambda a: jnp.asarray(a, F32)
    return f(np.cos(aa)), f(np.sin(aa)), f(np.cos(ab)), f(np.sin(ab))


def _fourier(uw):
    b, t, _ = uw.shape
    tmf = min(128, t)
    ca, sa, cb, sb = _dft_tables(t, tmf)
    scale = 1.0 / math.sqrt(t * HEAD_DIM)
    full2 = lambda j: (0, 0)
    return pl.pallas_call(
        functools.partial(_fourier_kernel, b, scale),
        grid=(t // tmf,),
        in_specs=[pl.BlockSpec((b, t, 512), lambda j: (0, 0, 0)),
                  pl.BlockSpec((t // tmf, t), full2),
                  pl.BlockSpec((t // tmf, t), full2),
                  pl.BlockSpec((tmf, t), full2),
                  pl.BlockSpec((tmf, t), full2)],
        out_specs=pl.BlockSpec((b, tmf, 256), lambda j: (0, j, 0)),
        out_shape=jax.ShapeDtypeStruct((b, t, 256), BF16),
        compiler_params=_cparams(("parallel",)),
        name="fourier_mix",
    )(uw, ca, sa, cb, sb)


def _win_attn_kernel(nct, nchunk, sink_ref, q_ref, k_ref, vt_ref, o_ref):
    i = pl.program_id(1)
    is_ctx = i < nct
    j = i - nct
    ctx_rows = nct * TM
    ctx_chunks = ctx_rows // LANES
    lw = TM + 2 * WINDOW
    lchunks = lw // LANES
    cs = jnp.clip(ctx_chunks + (TM // LANES) * j - WINDOW // LANES, 0, nchunk - lchunks)
    rs = pl.multiple_of(cs * LANES, LANES)
    qpos = j * TM + lax.broadcasted_iota(jnp.int32, (1, TM), 1)
    kpos = cs * LANES - ctx_rows + lax.broadcasted_iota(jnp.int32, (lw, 1), 0)
    allowed = (jnp.abs(qpos - kpos) <= WINDOW) & (kpos >= 0) & jnp.logical_not(is_ctx)
    lane = lax.broadcasted_iota(jnp.int32, (TM, LANES), 1)
    n_heads = q_ref.shape[-1] // HEAD_DIM
    group = n_heads // (k_ref.shape[-1] // LANES)
    for pair in range(n_heads // 2):
        qp = q_ref[0, :, pair * LANES:(pair + 1) * LANES]
        outs = []
        for half in range(2):
            hq = 2 * pair + half
            g = hq // group
            qm = jnp.where((lane >= HEAD_DIM) == (half == 1), qp, jnp.zeros_like(qp))
            kc = k_ref[0, 0:ctx_rows, g * LANES:(g + 1) * LANES]
            kl = k_ref[0, pl.ds(rs, lw), g * LANES:(g + 1) * LANES]
            s_c = _dot_nt(kc, qm)
            s_l = jnp.where(allowed, _dot_nt(kl, qm), NEG)
            sk = sink_ref[hq] * LOG2E
            m = jnp.maximum(jnp.maximum(jnp.max(s_c, axis=0, keepdims=True),
                                        jnp.max(s_l, axis=0, keepdims=True)), sk)
            p_c = jnp.exp2(s_c - m)
            p_l = jnp.exp2(s_l - m)
            l = (jnp.sum(p_c, axis=0, keepdims=True) + jnp.sum(p_l, axis=0, keepdims=True)
                 + jnp.exp2(sk - m))
            p_c = p_c.astype(BF16)
            p_l = p_l.astype(BF16)
            vrows = slice(g * HEAD_DIM, (g + 1) * HEAD_DIM)
            acc = jnp.zeros((HEAD_DIM, TM), F32)
            for c in range(ctx_chunks):
                acc = acc + _dot(vt_ref[0, c, vrows, :], p_c[c * LANES:(c + 1) * LANES, :])
            for c in range(lchunks):
                acc = acc + _dot(vt_ref[0, cs + c, vrows, :], p_l[c * LANES:(c + 1) * LANES, :])
            outs.append(acc / l)
        o2 = jnp.concatenate(outs, axis=0)
        o_ref[0, :, pair * LANES:(pair + 1) * LANES] = o2.T.astype(BF16)


def _win_attn(sink, q, k, vt, nct):
    b, tall, qw = q.shape
    nt = tall // TM
    nchunk = vt.shape[1]
    return pl.pallas_call(
        functools.partial(_win_attn_kernel, nct, nchunk),
        grid=(b, nt),
        in_specs=[pl.BlockSpec(memory_space=pltpu.SMEM),
                  pl.BlockSpec((1, TM, qw), lambda bi, i: (bi, i, 0)),
                  pl.BlockSpec((1, tall, k.shape[-1]), lambda bi, i: (bi, 0, 0)),
                  pl.BlockSpec((1, nchunk, vt.shape[2], LANES), lambda bi, i: (bi, 0, 0, 0))],
        out_specs=pl.BlockSpec((1, TM, qw), lambda bi, i: (bi, i, 0)),
        out_shape=jax.ShapeDtypeStruct((b, tall, qw), BF16),
        compiler_params=_cparams(("parallel", "parallel")),
        name="window_attention",
    )(sink, q, k, vt)


def _unit_operands(u):
    if u < 2 * N_C_HEADS:
        hc = u // 2
        dst, dv = _VT_LAYOUT[hc]
        return True, hc, dst, dv
    g = (u - 2 * N_C_HEADS) // (N_D_HEADS // N_D_KV)
    dst, dv = _VT_LAYOUT[N_C_HEADS + g]
    return False, g, dst, dv


def _attn_cd_kernel(nkt, lam_init, lam_ref, sg_ref, qc_ref, qd_ref, kc_ref, kd_ref, vt_ref, o_ref,
                    qm_s, m_s, acc_s, s_a, s_b):
    mq = qc_ref.shape[1]
    tk = vt_ref.shape[-1]
    lane = lax.broadcasted_iota(jnp.int32, (mq, LANES), 1)
    upper = lane >= HEAD_DIM
    for u in range(N_UNITS):
        src = qc_ref if u < 2 * N_C_HEADS else qd_ref
        pair = (u if u < 2 * N_C_HEADS else u - 2 * N_C_HEADS) // 2
        qp = src[0, :, pair * LANES:(pair + 1) * LANES]
        qm_s[u] = jnp.where(upper == (u % 2 == 1), qp, jnp.zeros_like(qp))
    m_s[...] = jnp.full(m_s.shape, NEG, F32)
    acc_s[...] = jnp.zeros(acc_s.shape, F32)

    def scores(tile, u, dst_s):
        is_c, slab, _, _ = _unit_operands(u)
        k_ref = kc_ref if is_c else kd_ref
        r0 = pl.multiple_of(tile * tk, tk)
        dst_s[u] = _dot_nt(k_ref[0, pl.ds(r0, tk), slab * LANES:(slab + 1) * LANES], qm_s[u])

    def consume(tile, u, src_s):
        _, _, row0, dv = _unit_operands(u)
        rows = dv + ONES_ROWS
        s = src_s[u]
        m_old = m_s[u]
        m_new = jnp.maximum(m_old, jnp.max(s, axis=0, keepdims=True))
        alpha = jnp.exp2(m_old - m_new)
        p = jnp.exp2(s - m_new).astype(BF16)
        pv = _dot(vt_ref[0, tile, row0:row0 + rows, :], p)
        acc_s[u, 0:rows, :] = acc_s[u, 0:rows, :] * alpha + pv
        m_s[u] = m_new

    def stage(tile, src_s, dst_s):
        nxt = jnp.minimum(tile + 1, nkt - 1)
        for u in range(N_UNITS):
            scores(nxt, u, dst_s)
            consume(tile, u, src_s)

    for u in range(N_UNITS):
        scores(0, u, s_a)
    stage(0, s_a, s_b)

    def body(pr, carry):
        stage(2 * pr + 1, s_b, s_a)
        stage(2 * pr + 2, s_a, s_b)
        return carry

    lax.fori_loop(0, (nkt - 1) // 2, body, 0)
    if (nkt - 1) % 2 == 1:
        stage(nkt - 1, s_b, s_a)

    lv = lam_ref[...]
    lam = (jnp.exp(jnp.sum(lv[0:1] * lv[1:2], axis=-1, keepdims=True))
           - jnp.exp(jnp.sum(lv[2:3] * lv[3:4], axis=-1, keepdims=True)) + lam_init)
    sg = sg_ref[...]
    dvc = 2 * HEAD_DIM
    for hc in range(N_C_HEADS):
        o1 = acc_s[2 * hc, 0:dvc, :] / acc_s[2 * hc, dvc:dvc + 1, :]
        o2 = acc_s[2 * hc + 1, 0:dvc, :] / acc_s[2 * hc + 1, dvc:dvc + 1, :]
        o = o1 - lam * o2
        ms = jnp.mean(o * o, axis=0, keepdims=True)
        o = o * lax.rsqrt(ms + RMS_EPS) * sg * (1.0 - lam_init)
        o_ref[0, :, hc * LANES:(hc + 1) * LANES] = o.T.astype(BF16)
    for pair in range(N_D_HEADS // 2):
        outs = []
        for half in range(2):
            u = 2 * N_C_HEADS + 2 * pair + half
            outs.append(acc_s[u, 0:HEAD_DIM, :] / acc_s[u, HEAD_DIM:HEAD_DIM + 1, :])
        o2 = jnp.concatenate(outs, axis=0)
        c0 = N_C_HEADS * LANES + pair * LANES
        o_ref[0, :, c0:c0 + LANES] = o2.T.astype(BF16)


def _attn_cd(lam_vec, sg, qc, qd, kc, kd, vt, nct, lam_init):
    b, tall, _ = qc.shape
    nkt, nv, tk = vt.shape[1], vt.shape[2], vt.shape[3]
    mq = TM
    nq = tall // mq - nct
    ow = N_C_HEADS * LANES + N_D_HEADS * HEAD_DIM
    qtile = lambda bi, i: (bi, i + nct, 0)
    return pl.pallas_call(
        functools.partial(_attn_cd_kernel, nkt, lam_init),
        grid=(b, nq),
        in_specs=[pl.BlockSpec((4, HEAD_DIM), lambda bi, i: (0, 0)),
                  pl.BlockSpec((LANES, 1), lambda bi, i: (0, 0)),
                  pl.BlockSpec((1, mq, qc.shape[-1]), qtile),
                  pl.BlockSpec((1, mq, qd.shape[-1]), qtile),
                  pl.BlockSpec((1, tall, kc.shape[-1]), lambda bi, i: (bi, 0, 0)),
                  pl.BlockSpec((1, tall, kd.shape[-1]), lambda bi, i: (bi, 0, 0)),
                  pl.BlockSpec((1, nkt, nv, tk), lambda bi, i: (bi, 0, 0, 0))],
        out_specs=pl.BlockSpec((1, mq, ow), lambda bi, i: (bi, i, 0)),
        out_shape=jax.ShapeDtypeStruct((b, nq * mq, ow), BF16),
        scratch_shapes=[pltpu.VMEM((N_UNITS, mq, LANES), BF16),
                        pltpu.VMEM((N_UNITS, 1, mq), F32),
                        pltpu.VMEM((N_UNITS, 2 * HEAD_DIM + ONES_ROWS, mq), F32),
                        pltpu.VMEM((N_UNITS, tk, mq), F32),
                        pltpu.VMEM((N_UNITS, tk, mq), F32)],
        compiler_params=_cparams(("parallel", "parallel")),
        name="attention_cd",
    )(lam_vec, sg, qc, qd, kc, kd, vt)


def _oproj_kernel(n_parts, widths, nct, alpha, n_exp, *refs):
    o_refs = refs[:n_parts]
    (w_ref, x_ref, mlat_ref, mctx_ref, lng_ref, lnb_ref, rwt_ref, rb_ref,
     x1_ref, h2_ref, te_ref, tg_ref) = refs[n_parts:]
    d = x_ref.shape[-1]
    is_ctx = pl.program_id(1) < nct
    y = None
    r0 = 0
    for o_ref, wd in zip(o_refs, widths):
        part = _dot(o_ref[0], w_ref[r0:r0 + wd, :])
        y = part if y is None else y + part
        r0 += wd
    m = jnp.where(is_ctx, mctx_ref[...], mlat_ref[0])
    x1 = _layernorm(alpha * x_ref[0] + m[:, 2 * d:3 * d] * y, lng_ref[...], lnb_ref[...])
    x1_ref[0] = x1
    h2 = x1 * (1.0 + m[:, 4 * d:5 * d]) + m[:, 3 * d:4 * d]
    h2_ref[0] = h2
    logits = _dot_nt(rwt_ref[...], h2.astype(BF16)) + rb_ref[...]
    tm = logits.shape[1]
    row = lax.broadcasted_iota(jnp.int32, logits.shape, 0)
    vals, idxs = [], []
    for _ in range(TOP_K):
        mx = jnp.max(logits, axis=0, keepdims=True)
        idx = jnp.min(jnp.where(logits == mx, row, n_exp), axis=0, keepdims=True)
        vals.append(mx)
        idxs.append(idx)
        logits = jnp.where(row == idx, NEG, logits)
    es = [jnp.exp(v - vals[0]) for v in vals]
    tot = es[0] + es[1] + es[2] + es[3]
    pad_i = jnp.zeros((8 - TOP_K, tm), jnp.int32)
    pad_f = jnp.zeros((8 - TOP_K, tm), F32)
    te_ref[0] = jnp.concatenate(idxs + [pad_i], axis=0)
    tg_ref[0] = jnp.concatenate([e / tot for e in es] + [pad_f], axis=0)


def _oproj(parts, w_out, x_all, mlat, mctx, lng, lnb, rwt, rb, nct_out, x_tile_off, alpha):
    b, t, _ = parts[0].shape
    d = x_all.shape[-1]
    nt = t // TM
    n_exp = rwt.shape[0]
    widths = tuple(p.shape[-1] for p in parts)
    tile = lambda bi, i: (bi, i, 0)
    full = lambda bi, i: (0, 0)
    in_specs = [pl.BlockSpec((1, TM, wd), tile) for wd in widths] + [
        pl.BlockSpec(w_out.shape, full),
        pl.BlockSpec((1, TM, d), lambda bi, i: (bi, i + x_tile_off, 0)),
        pl.BlockSpec((1, 1, N_MOD * d), lambda bi, i: (bi, 0, 0)),
        pl.BlockSpec((1, N_MOD * d), full),
        pl.BlockSpec((1, d), full),
        pl.BlockSpec((1, d), full),
        pl.BlockSpec((n_exp, d), full),
        pl.BlockSpec((n_exp, 1), full)]
    return pl.pallas_call(
        functools.partial(_oproj_kernel, len(parts), widths, nct_out, alpha, n_exp),
        grid=(b, nt),
        in_specs=in_specs,
        out_specs=[pl.BlockSpec((1, TM, d), tile),
                   pl.BlockSpec((1, TM, d), tile),
                   pl.BlockSpec((1, 8, TM), lambda bi, i: (bi, 0, i)),
                   pl.BlockSpec((1, 8, TM), lambda bi, i: (bi, 0, i))],
        out_shape=[jax.ShapeDtypeStruct((b, t, d), F32),
                   jax.ShapeDtypeStruct((b, t, d), F32),
                   jax.ShapeDtypeStruct((b, 8, t), jnp.int32),
                   jax.ShapeDtypeStruct((b, 8, t), F32)],
        compiler_params=_cparams(("parallel", "parallel")),
        name="out_proj_norm_router",
    )(*parts, w_out, x_all, mlat, mctx, lng, lnb, rwt, rb)


def _moe_kernel(be_ref, nu_ref, x_ref, wgu_ref, bgu_ref, wd_ref, bd_ref, y_ref, wgu_s, wd_s):
    blk = pl.program_id(0)
    ff = wd_ref.shape[1]

    @pl.when(blk < nu_ref[0])
    def _():
        prev = be_ref[jnp.maximum(blk - 1, 0)]

        @pl.when((blk == 0) | (be_ref[blk] != prev))
        def _():
            rows = 128

            def cast_gu(r, c):
                r0 = pl.multiple_of(r * rows, rows)
                wgu_s[pl.ds(r0, rows), :] = wgu_ref[0, pl.ds(r0, rows), :].astype(BF16)
                return c

            def cast_d(r, c):
                r0 = pl.multiple_of(r * rows, rows)
                wd_s[pl.ds(r0, rows), :] = wd_ref[0, pl.ds(r0, rows), :].astype(BF16)
                return c

            lax.fori_loop(0, wgu_s.shape[0] // rows, cast_gu, 0)
            lax.fori_loop(0, wd_s.shape[0] // rows, cast_d, 0)

        gu = _dot(x_ref[...].astype(BF16), wgu_s[...]) + bgu_ref[0]
        g = jnp.minimum(gu[:, :ff], SWIGLU_LIMIT)
        u = jnp.clip(gu[:, ff:], -SWIGLU_LIMIT, SWIGLU_LIMIT)
        act = (u + 1.0) * (g / (1.0 + jnp.exp(-SWIGLU_ALPHA * g)))
        y_ref[...] = _dot(act.astype(BF16), wd_s[...]) + bd_ref[0]

    @pl.when(blk >= nu_ref[0])
    def _():
        y_ref[...] = jnp.zeros(y_ref.shape, F32)


def _moe_experts(block_e, n_used, xb, w_gu, b_gu, w_down, b_down):
    n_slots, d = xb.shape
    n_exp, _, ff2 = w_gu.shape
    ff = w_down.shape[1]
    nb = n_slots // MOE_BM
    xblk = lambda i, be, nu: (jnp.minimum(i, nu[0] - 1), 0)
    wblk = lambda i, be, nu: (be[i], 0, 0)
    grid_spec = pltpu.PrefetchScalarGridSpec(
        num_scalar_prefetch=2,
        grid=(nb,),
        in_specs=[pl.BlockSpec((MOE_BM, d), xblk),
                  pl.BlockSpec((1, d, ff2), wblk),
                  pl.BlockSpec((1, 1, ff2), wblk),
                  pl.BlockSpec((1, ff, d), wblk),
                  pl.BlockSpec((1, 1, d), wblk)],
        out_specs=pl.BlockSpec((MOE_BM, d), lambda i, be, nu: (i, 0)),
        scratch_shapes=[pltpu.VMEM((d, ff2), BF16), pltpu.VMEM((ff, d), BF16)])
    return pl.pallas_call(
        _moe_kernel,
        grid_spec=grid_spec,
        out_shape=jax.ShapeDtypeStruct((n_slots, d), F32),
        compiler_params=_cparams(("arbitrary",)),
        name="moe_experts",
    )(block_e, n_used, xb, w_gu, b_gu.reshape(n_exp, 1, ff2), w_down, b_down.reshape(n_exp, 1, d))


def _expert_onehot(te, n_exp):
    row = lax.broadcasted_iota(jnp.int32, (n_exp, te.shape[1]), 0)
    oh = jnp.zeros(row.shape, F32)
    for k in range(TOP_K):
        oh = oh + (row == te[k:k + 1, :]).astype(F32)
    return row, oh


def _route_count_kernel(n_exp, te_ref, cnt_ref):
    first = (pl.program_id(0) == 0) & (pl.program_id(1) == 0)

    @pl.when(first)
    def _():
        cnt_ref[...] = jnp.zeros(cnt_ref.shape, F32)

    _, oh = _expert_onehot(te_ref[0], n_exp)
    cnt_ref[...] += jnp.sum(oh, axis=1, keepdims=True)


def _route_slot_kernel(n_exp, te_ref, pstart_ref, tri_ref, slot_ref, carry_s):
    first = (pl.program_id(0) == 0) & (pl.program_id(1) == 0)

    @pl.when(first)
    def _():
        carry_s[...] = pstart_ref[...]

    te = te_ref[0]
    row, oh = _expert_onehot(te, n_exp)
    before = _dot(oh.astype(BF16), tri_ref[...]) + carry_s[...]
    slots = [jnp.sum(jnp.where(row == te[k:k + 1, :], before, 0.0), axis=0, keepdims=True)
             for k in range(TOP_K)]
    pad = jnp.zeros((8 - TOP_K, te.shape[1]), F32)
    slot_ref[0] = jnp.concatenate(slots + [pad], axis=0).astype(jnp.int32)
    carry_s[...] += jnp.sum(oh, axis=1, keepdims=True)


def _route(te, n_exp):
    b, _, t = te.shape
    tile = pl.BlockSpec((1, 8, TM), lambda bi, i: (bi, 0, i))
    vec = pl.BlockSpec((n_exp, 1), lambda bi, i: (0, 0))
    cnt = pl.pallas_call(
        functools.partial(_route_count_kernel, n_exp),
        grid=(b, t // TM),
        in_specs=[tile],
        out_specs=vec,
        out_shape=jax.ShapeDtypeStruct((n_exp, 1), F32),
        compiler_params=_cparams(("arbitrary", "arbitrary")),
        name="route_count",
    )(te)
    counts = cnt[:, 0].astype(jnp.int32)
    padded = (counts + MOE_BM - 1) // MOE_BM * MOE_BM
    pend = jnp.cumsum(padded)
    pstart = (pend - padded).astype(F32).reshape(n_exp, 1)
    tri = jnp.asarray(np.triu(np.ones((TM, TM)), 1), BF16)
    slot = pl.pallas_call(
        functools.partial(_route_slot_kernel, n_exp),
        grid=(b, t // TM),
        in_specs=[tile, vec, pl.BlockSpec((TM, TM), lambda bi, i: (0, 0))],
        out_specs=tile,
        out_shape=jax.ShapeDtypeStruct((b, 8, t), jnp.int32),
        scratch_shapes=[pltpu.VMEM((n_exp, 1), F32)],
        compiler_params=_cparams(("arbitrary", "arbitrary")),
        name="route_slot",
    )(te, pstart, tri)
    n_blocks = -(-(b * t * TOP_K) // MOE_BM) + n_exp
    n_used = (pend[-1] // MOE_BM).astype(jnp.int32)
    blk = jnp.minimum(jnp.arange(n_blocks, dtype=jnp.int32), n_used - 1) * MOE_BM
    block_e = jnp.minimum(jnp.searchsorted(pend, blk, side='right'), n_exp - 1).astype(jnp.int32)
    slot_flat = slot[:, :TOP_K].reshape(b, TOP_K, t // TM, TM).transpose(0, 2, 1, 3).reshape(-1)
    return slot_flat, pend.astype(jnp.int32), block_e, n_used.reshape(1), n_blocks


def _row_copy(src_ref, src_row, dst_ref, dst_row, sem):
    return pltpu.make_async_copy(src_ref.at[pl.ds(src_row, 1)], dst_ref.at[pl.ds(dst_row, 1)], sem)


def _moe_scatter_kernel(n_exp, pend_ref, slot_ref, h_ref, xb_ref, zero_s, sem):
    first = (pl.program_id(0) == 0) & (pl.program_id(1) == 0)
    tm = h_ref.shape[1]

    @pl.when(first)
    def _():
        zero_s[...] = jnp.zeros(zero_s.shape, zero_s.dtype)
        for phase in ("start", "wait"):
            for e in range(n_exp):
                lo = pend_ref[e - 1] if e else 0

                @pl.when(pend_ref[e] > lo)
                def _():
                    r0 = pl.multiple_of(pend_ref[e] - MOE_BM, MOE_BM)
                    cp = pltpu.make_async_copy(zero_s, xb_ref.at[pl.ds(r0, MOE_BM)], sem)
                    cp.start() if phase == "start" else cp.wait()

            def unused_block(blk, c):
                r0 = pl.multiple_of(blk * MOE_BM, MOE_BM)
                cp = pltpu.make_async_copy(zero_s, xb_ref.at[pl.ds(r0, MOE_BM)], sem)
                cp.start() if phase == "start" else cp.wait()
                return c

            lax.fori_loop(pend_ref[n_exp - 1] // MOE_BM, xb_ref.shape[0] // MOE_BM, unused_block, 0)

    def issue(j, c):
        for k in range(TOP_K):
            _row_copy(h_ref.at[0], j, xb_ref, slot_ref[k * tm + j], sem).start()
        return c

    def drain(j, c):
        for k in range(TOP_K):
            _row_copy(h_ref.at[0], j, xb_ref, slot_ref[k * tm + j], sem).wait()
        return c

    lax.fori_loop(0, tm, issue, 0)
    lax.fori_loop(0, tm, drain, 0)


def _moe_scatter(pend, slot_flat, h2, n_slots):
    b, t, d = h2.shape
    n_exp = pend.shape[0]
    grid_spec = pltpu.PrefetchScalarGridSpec(
        num_scalar_prefetch=1,
        grid=(b, t // TM),
        in_specs=[pl.BlockSpec((TOP_K * TM,), lambda bi, i, pe: (bi * (t // TM) + i,), memory_space=pltpu.SMEM),
                  pl.BlockSpec((1, TM, d), lambda bi, i, pe: (bi, i, 0))],
        out_specs=pl.BlockSpec(memory_space=pl.ANY),
        scratch_shapes=[pltpu.VMEM((MOE_BM, d), F32), pltpu.SemaphoreType.DMA])
    return pl.pallas_call(
        functools.partial(_moe_scatter_kernel, n_exp),
        grid_spec=grid_spec,
        out_shape=jax.ShapeDtypeStruct((n_slots, d), F32),
        compiler_params=_cparams(("arbitrary", "arbitrary")),
        name="moe_scatter_rows",
    )(pend, slot_flat, h2)


def _combine_kernel(nct, alpha, slot_ref, yb_ref, tg_ref, x1_ref, mlat_ref, mctx_ref, lng_ref, lnb_ref,
                    o_ref, buf_s, sem):
    d = x1_ref.shape[-1]
    tm = x1_ref.shape[1]

    def issue(j, c):
        for k in range(TOP_K):
            _row_copy(yb_ref, slot_ref[k * tm + j], buf_s.at[k], j, sem).start()
        return c

    def drain(j, c):
        for k in range(TOP_K):
            _row_copy(yb_ref, slot_ref[k * tm + j], buf_s.at[k], j, sem).wait()
        return c

    lax.fori_loop(0, tm, issue, 0)
    is_ctx = pl.program_id(1) < nct
    m = jnp.where(is_ctx, mctx_ref[...], mlat_ref[0])
    gates = tg_ref[0]
    lax.fori_loop(0, tm, drain, 0)
    f = gates[:, 0:1] * buf_s[0]
    for k in range(1, TOP_K):
        f = f + gates[:, k:k + 1] * buf_s[k]
    o_ref[0] = _layernorm(alpha * x1_ref[0] + m[:, 5 * d:6 * d] * f, lng_ref[...], lnb_ref[...])


def _combine_postnorm(slot_flat, yb, tg, x1, mlat, mctx, lng, lnb, nct, alpha):
    b, t, d = x1.shape
    tile = lambda bi, i: (bi, i, 0)
    full = lambda bi, i: (0, 0)
    return pl.pallas_call(
        functools.partial(_combine_kernel, nct, alpha),
        grid=(b, t // TM),
        in_specs=[pl.BlockSpec((TOP_K * TM,), lambda bi, i: (bi * (t // TM) + i,), memory_space=pltpu.SMEM),
                  pl.BlockSpec(memory_space=pl.ANY),
                  pl.BlockSpec((1, TM, 8), tile),
                  pl.BlockSpec((1, TM, d), tile),
                  pl.BlockSpec((1, 1, N_MOD * d), lambda bi, i: (bi, 0, 0)),
                  pl.BlockSpec((1, N_MOD * d), full),
                  pl.BlockSpec((1, d), full),
                  pl.BlockSpec((1, d), full)],
        out_specs=pl.BlockSpec((1, TM, d), tile),
        out_shape=jax.ShapeDtypeStruct((b, t, d), F32),
        scratch_shapes=[pltpu.VMEM((TOP_K, TM, d), F32), pltpu.SemaphoreType.DMA],
        compiler_params=_cparams(("parallel", "parallel")),
        name="combine_post_norm",
    )(slot_flat, yb, tg, x1, mlat, mctx, lng, lnb)


def _rope_tables(ctx_len, seq):
    t = np.arange(seq)
    row = (t // GRID_W).astype(np.float64)
    col = (t % GRID_W).astype(np.float64)
    nf = HEAD_DIM // 4
    inv = ROPE_THETA ** (-np.arange(nf, dtype=np.float64) / nf)
    ar = row[:, None] * inv[None, :]
    ac = col[:, None] * inv[None, :]
    ang = np.concatenate([ar, ar, ac, ac], axis=-1)
    cos = np.concatenate([np.ones((ctx_len, HEAD_DIM)), np.cos(ang)], axis=0)
    sin = np.concatenate([np.zeros((ctx_len, HEAD_DIM)), np.sin(ang)], axis=0)
    sign = np.where((np.arange(HEAD_DIM) % 32) < 16, -1.0, 1.0)[None, :]
    cos2 = np.tile(cos, (1, LANES // HEAD_DIM))
    sin2 = np.tile(sin * sign, (1, LANES // HEAD_DIM))
    return jnp.asarray(cos2, F32), jnp.asarray(sin2, F32)


def _channel_dft():
    c = np.arange(HEAD_DIM)
    ang = 2.0 * np.pi * ((c[:, None] * c[None, :]) % HEAD_DIM) / HEAD_DIM
    eye = np.eye(4)
    cs = np.concatenate([np.kron(eye, np.cos(ang)), np.kron(eye, np.sin(ang))], axis=1)
    return jnp.asarray(cs, BF16)


def _group_mean_matrix():
    bd = np.kron(np.eye(LANES // HEAD_DIM), np.full((HEAD_DIM, HEAD_DIM), 1.0 / HEAD_DIM))
    return jnp.asarray(bd, BF16)


def _dup_heads(wk, n_heads):
    d = wk.shape[0]
    return jnp.broadcast_to(wk.reshape(d, n_heads, 1, HEAD_DIM), (d, n_heads, 2, HEAD_DIM)).reshape(d, n_heads * LANES)


def _lambda_init(layer):
    return 0.8 - 0.6 * math.exp(-0.3 * layer)


def _moe_postnorm(h2, te, tg, x1, mlat, mctx, lng, lnb, nct, alpha, n_exp, w_gu, b_gu, w_down, b_down):
    slot_flat, pend, block_e, n_used, n_blocks = _route(te, n_exp)
    xb = _moe_scatter(pend, slot_flat, h2, n_blocks * MOE_BM)
    yb = _moe_experts(block_e, n_used, xb, w_gu, b_gu, w_down, b_down)
    return _combine_postnorm(slot_flat, yb, jnp.transpose(tg, (0, 2, 1)), x1, mlat, mctx, lng, lnb, nct, alpha)


def kernel(x, c, ctx, c_ctx, mod_w, mod_b, ln_g, ln_b, ab_w_in, ab_sink, ab_w_out,
           cd_w_in, cd_lambda, cd_subln_g, cd_q_norm_g, cd_k_norm_g, cd_w_out,
           router_w, router_b, expert_w_gu, expert_b_gu, expert_w_down, expert_b_down):
    b, s, d = x.shape
    n_ctx = ctx.shape[1]
    depth = mod_w.shape[0]
    n_exp = router_w.shape[-1]
    assert d == 16 * HEAD_DIM and n_ctx % TM == 0 and s % TM == 0 and s % GRID_W == 0
    nct = n_ctx // TM
    alpha = (2 * depth) ** 0.25

    cos, sin = _rope_tables(n_ctx, s)
    cs_dft = _channel_dft()
    bd = _group_mean_matrix()
    mods = _mod_vectors(c, c_ctx, mod_w, mod_b)

    x_all = jnp.concatenate([ctx, x], axis=1)
    for l in range(depth):
        last = l == depth - 1
        i = l // 2
        mlat = mods[l, :b].reshape(b, 1, N_MOD * d)
        mctx = mods[l, b:b + 1]
        lng1, lnb1 = ln_g[l, 0].reshape(1, d), ln_b[l, 0].reshape(1, d)
        lng2, lnb2 = ln_g[l, 1].reshape(1, d), ln_b[l, 1].reshape(1, d)
        rwt = router_w[l].T.astype(BF16)
        rb = router_b[l].reshape(n_exp, 1)
        if l % 2 == 0:
            w = ab_w_in[i]
            w_n = jnp.concatenate([w[:, :1024], _dup_heads(w[:, 1024:1280], 4)], axis=1).astype(BF16)
            wvt = w[:, 1280:1536].T.astype(BF16)
            uw, q, k, vt = _proj_ab(x_all, mlat, mctx, w_n, wvt, cos, sin, cs_dft, nct)
            oa = jnp.concatenate([_fourier(uw[:, :n_ctx]), _fourier(uw[:, n_ctx:])], axis=1)
            ob = _win_attn(ab_sink[i], q, k, vt, nct)
            if last:
                parts = [oa[:, n_ctx:], ob[:, n_ctx:]]
            else:
                parts = [oa, ob]
            w_out = ab_w_out[i].astype(BF16)
        else:
            w = cd_w_in[i]
            w_n = jnp.concatenate([w[:, :1536], _dup_heads(w[:, 1536:1664], 2)], axis=1).astype(BF16)
            wvt = w[:, 1664:2304].T.astype(BF16)
            gq = jnp.tile(cd_q_norm_g[i], 2).reshape(1, LANES)
            gk = jnp.tile(cd_k_norm_g[i], 2).reshape(1, LANES)
            qc, qd, kc, kd, vt = _proj_cd(x_all, mlat, mctx, w_n, wvt, cos, sin, bd, gq, gk, nct)
            sg = cd_subln_g[i].reshape(LANES, 1)
            o_lat = _attn_cd(cd_lambda[i], sg, qc, qd, kc, kd, vt, nct, _lambda_init(l))
            if last:
                parts = [o_lat]
            else:
                raise NotImplementedError("context outputs of a differential/axial layer")
            w_out = cd_w_out[i].astype(BF16)
        experts = (n_exp, expert_w_gu[l], expert_b_gu[l], expert_w_down[l], expert_b_down[l])
        if last:
            x1, h2, te, tg = _oproj(parts, w_out, x_all, mlat, mctx, lng1, lnb1, rwt, rb, 0, nct, alpha)
            return _moe_postnorm(h2, te, tg, x1, mlat, mctx, lng2, lnb2, 0, alpha, *experts)
        x1, h2, te, tg = _oproj(parts, w_out, x_all, mlat, mctx, lng1, lnb1, rwt, rb, nct, 0, alpha)
        x_all = _moe_postnorm(h2, te, tg, x1, mlat, mctx, lng2, lnb2, nct, alpha, *experts)
    return x_all[:, n_ctx:]
```

```python
import functools
import math

import jax
import jax.numpy as jnp
import numpy as np
from jax import lax
from jax.experimental import pallas as pl
from jax.experimental.pallas import tpu as pltpu

F32 = jnp.float32
BF16 = jnp.bfloat16

HEAD_DIM = 64
GRID_W = 64
WINDOW = 128
ROPE_THETA = 10000.0
LN_EPS = 1e-6
RMS_EPS = 1e-6
N_MOD = 6
TOP_K = 4
SWIGLU_LIMIT = 7.0
SWIGLU_ALPHA = 1.702
NEG = -1e30
LOG2E = 1.4426950408889634
QSCALE = HEAD_DIM ** -0.5 * LOG2E

N_C_HEADS = 4
N_D_HEADS = 8
N_D_KV = 2
N_UNITS = 2 * N_C_HEADS + N_D_HEADS
ONES_ROWS = 16
_VT_LAYOUT = tuple((h * (2 * HEAD_DIM + ONES_ROWS), 2 * HEAD_DIM) for h in range(N_C_HEADS)) + tuple(
    (N_C_HEADS * (2 * HEAD_DIM + ONES_ROWS) + g * (HEAD_DIM + ONES_ROWS), HEAD_DIM) for g in range(N_D_KV))
VT_ROWS = _VT_LAYOUT[-1][0] + HEAD_DIM + ONES_ROWS

LANES = 128
TM = 256
MOE_BM = 256
VMEM_LIMIT = 56 * 1024 * 1024


def _dot(a, b):
    return jnp.dot(a, b, preferred_element_type=F32)


def _dot_nt(a, b):
    return lax.dot_general(a, b, (((1,), (1,)), ((), ())), preferred_element_type=F32)


def _cparams(sem):
    return pltpu.CompilerParams(dimension_semantics=sem, vmem_limit_bytes=VMEM_LIMIT)


def _layernorm(z, g, b):
    mu = jnp.mean(z, axis=-1, keepdims=True)
    d = z - mu
    var = jnp.mean(d * d, axis=-1, keepdims=True)
    return d * lax.rsqrt(var + LN_EPS) * g + b


def _rope128(x, cos, sin_signed):
    lane = lax.broadcasted_iota(jnp.int32, x.shape, 1)
    lo = (lane % 32) < 16
    rot = jnp.where(lo, pltpu.roll(x, LANES - 16, 1), pltpu.roll(x, 16, 1))
    return x * cos + rot * sin_signed


def _rmsnorm128(x, g, bd):
    x2 = x * x
    hi = x2.astype(BF16)
    lo = (x2 - hi.astype(F32)).astype(BF16)
    ms = _dot(hi, bd) + _dot(lo, bd)
    return x * lax.rsqrt(ms + RMS_EPS) * g


def _modulated(x_ref, mlat_ref, mctx_ref, is_ctx, d):
    m = jnp.where(is_ctx, mctx_ref[...], mlat_ref[0])
    return (x_ref[0] * (1.0 + m[:, d:2 * d]) + m[:, 0:d]).astype(BF16)


def _mod_kernel(c_ref, w_ref, b_ref, o_ref):
    c = c_ref[...]
    s = (c / (1.0 + jnp.exp(-c))).astype(BF16)
    o_ref[0] = _dot(s, w_ref[0].astype(BF16)) + b_ref[0]


def _mod_vectors(c, c_ctx, mod_w, mod_b):
    depth, d, n = mod_w.shape
    b = c.shape[0]
    rows = 8 * (-(-(b + 1) // 8))
    cs = jnp.zeros((rows, d), F32).at[:b].set(c).at[b].set(c_ctx)
    tn = 1536
    out = pl.pallas_call(
        _mod_kernel,
        grid=(depth, n // tn),
        in_specs=[pl.BlockSpec((rows, d), lambda l, j: (0, 0)),
                  pl.BlockSpec((1, d, tn), lambda l, j: (l, 0, j)),
                  pl.BlockSpec((1, 1, tn), lambda l, j: (l, 0, j))],
        out_specs=pl.BlockSpec((1, rows, tn), lambda l, j: (l, 0, j)),
        out_shape=jax.ShapeDtypeStruct((depth, rows, n), F32),
        compiler_params=_cparams(("parallel", "parallel")),
        name="mod_vectors",
    )(cs, mod_w, mod_b.reshape(depth, 1, n))
    return out


def _proj_ab_kernel(nct, x_ref, mlat_ref, mctx_ref, w_ref, wvt_ref, cos_ref, sin_ref, cs_ref,
                    uw_ref, q_ref, k_ref, vt_ref):
    is_ctx = pl.program_id(1) < nct
    d = x_ref.shape[-1]
    h = _modulated(x_ref, mlat_ref, mctx_ref, is_ctx, d)
    p = _dot(h, w_ref[...])
    uw_ref[0] = _dot(p[:, 0:256].astype(BF16), cs_ref[...]).astype(BF16)
    cos = cos_ref[...]
    sin = sin_ref[...]
    for j in range(6):
        c0 = 256 + j * LANES
        q_ref[0, :, j * LANES:(j + 1) * LANES] = (_rope128(p[:, c0:c0 + LANES], cos, sin) * QSCALE).astype(BF16)
    for j in range(4):
        c0 = 1024 + j * LANES
        k_ref[0, :, j * LANES:(j + 1) * LANES] = _rope128(p[:, c0:c0 + LANES], cos, sin).astype(BF16)
    vt = _dot_nt(wvt_ref[...], h).astype(BF16)
    for c in range(TM // LANES):
        vt_ref[0, c] = vt[:, c * LANES:(c + 1) * LANES]


def _proj_ab(x_all, mlat, mctx, w, wvt, cos, sin, cs, nct):
    b, tall, d = x_all.shape
    nt = tall // TM
    nw = w.shape[1]
    nchunk = tall // LANES
    cpt = TM // LANES
    return pl.pallas_call(
        functools.partial(_proj_ab_kernel, nct),
        grid=(b, nt),
        in_specs=[pl.BlockSpec((1, TM, d), lambda bi, i: (bi, i, 0)),
                  pl.BlockSpec((1, 1, N_MOD * d), lambda bi, i: (bi, 0, 0)),
                  pl.BlockSpec((1, N_MOD * d), lambda bi, i: (0, 0)),
                  pl.BlockSpec((d, nw), lambda bi, i: (0, 0)),
                  pl.BlockSpec((256, d), lambda bi, i: (0, 0)),
                  pl.BlockSpec((TM, LANES), lambda bi, i: (i, 0)),
                  pl.BlockSpec((TM, LANES), lambda bi, i: (i, 0)),
                  pl.BlockSpec((256, 512), lambda bi, i: (0, 0))],
        out_specs=[pl.BlockSpec((1, TM, 512), lambda bi, i: (bi, i, 0)),
                   pl.BlockSpec((1, TM, 768), lambda bi, i: (bi, i, 0)),
                   pl.BlockSpec((1, TM, 512), lambda bi, i: (bi, i, 0)),
                   pl.BlockSpec((1, cpt, 256, LANES), lambda bi, i: (bi, i, 0, 0))],
        out_shape=[jax.ShapeDtypeStruct((b, tall, 512), BF16),
                   jax.ShapeDtypeStruct((b, tall, 768), BF16),
                   jax.ShapeDtypeStruct((b, tall, 512), BF16),
                   jax.ShapeDtypeStruct((b, nchunk, 256, LANES), BF16)],
        compiler_params=_cparams(("parallel", "parallel")),
        name="proj_ab",
    )(x_all, mlat, mctx, w, wvt, cos, sin, cs)


def _proj_cd_kernel(nct, x_ref, mlat_ref, mctx_ref, w_ref, wvt_ref, cos_ref, sin_ref, bd_ref,
                    gq_ref, gk_ref, qc_ref, qd_ref, kc_ref, kd_ref, vt_ref):
    is_ctx = pl.program_id(1) < nct
    d = x_ref.shape[-1]
    h = _modulated(x_ref, mlat_ref, mctx_ref, is_ctx, d)
    p = _dot(h, w_ref[...])
    cos = cos_ref[...]
    sin = sin_ref[...]
    bd = bd_ref[...]
    gq = gq_ref[...]
    gk = gk_ref[...]
    for j in range(4):
        sl = slice(j * LANES, (j + 1) * LANES)
        qc_ref[0, :, sl] = (_rope128(p[:, j * LANES:(j + 1) * LANES], cos, sin) * QSCALE).astype(BF16)
        c0 = 512 + j * LANES
        qd_ref[0, :, sl] = (_rope128(_rmsnorm128(p[:, c0:c0 + LANES], gq, bd), cos, sin) * QSCALE).astype(BF16)
        c0 = 1024 + j * LANES
        kc_ref[0, :, sl] = _rope128(p[:, c0:c0 + LANES], cos, sin).astype(BF16)
    for j in range(2):
        c0 = 1536 + j * LANES
        kd_ref[0, :, j * LANES:(j + 1) * LANES] = _rope128(
            _rmsnorm128(p[:, c0:c0 + LANES], gk, bd), cos, sin).astype(BF16)
    vt = _dot_nt(wvt_ref[...], h).astype(BF16)
    ones = jnp.ones((ONES_ROWS, vt.shape[1]), BF16)
    src = 0
    for dst, dv in _VT_LAYOUT:
        vt_ref[0, 0, dst:dst + dv, :] = vt[src:src + dv, :]
        vt_ref[0, 0, dst + dv:dst + dv + ONES_ROWS, :] = ones
        src += dv


def _proj_cd(x_all, mlat, mctx, w, wvt, cos, sin, bd, gq, gk, nct):
    b, tall, d = x_all.shape
    nt = tall // TM
    nw = w.shape[1]
    nv = VT_ROWS
    full = lambda bi, i: (0, 0)
    tile = lambda bi, i: (bi, i, 0)
    return pl.pallas_call(
        functools.partial(_proj_cd_kernel, nct),
        grid=(b, nt),
        in_specs=[pl.BlockSpec((1, TM, d), tile),
                  pl.BlockSpec((1, 1, N_MOD * d), lambda bi, i: (bi, 0, 0)),
                  pl.BlockSpec((1, N_MOD * d), full),
                  pl.BlockSpec((d, nw), full),
                  pl.BlockSpec(wvt.shape, full),
                  pl.BlockSpec((TM, LANES), lambda bi, i: (i, 0)),
                  pl.BlockSpec((TM, LANES), lambda bi, i: (i, 0)),
                  pl.BlockSpec((LANES, LANES), full),
                  pl.BlockSpec((1, LANES), full),
                  pl.BlockSpec((1, LANES), full)],
        out_specs=[pl.BlockSpec((1, TM, 512), tile),
                   pl.BlockSpec((1, TM, 512), tile),
                   pl.BlockSpec((1, TM, 512), tile),
                   pl.BlockSpec((1, TM, 256), tile),
                   pl.BlockSpec((1, 1, nv, TM), lambda bi, i: (bi, i, 0, 0))],
        out_shape=[jax.ShapeDtypeStruct((b, tall, 512), BF16),
                   jax.ShapeDtypeStruct((b, tall, 512), BF16),
                   jax.ShapeDtypeStruct((b, tall, 512), BF16),
                   jax.ShapeDtypeStruct((b, tall, 256), BF16),
                   jax.ShapeDtypeStruct((b, nt, nv, TM), BF16)],
        compiler_params=_cparams(("parallel", "parallel")),
        name="proj_cd",
    )(x_all, mlat, mctx, w, wvt, cos, sin, bd, gq, gk)


def _fourier_kernel(nb, scale, uw_ref, ca_ref, sa_ref, cb_ref, sb_ref, o_ref):
    j = pl.program_id(0)
    ca = ca_ref[pl.ds(j, 1), :]
    sa = sa_ref[pl.ds(j, 1), :]
    cb = cb_ref[...]
    sb = sb_ref[...]
    ct = (ca * cb - sa * sb).astype(BF16)
    nst = (-(sa * cb + ca * sb)).astype(BF16)
    for bi in range(nb):
        acc = _dot(ct, uw_ref[bi, :, 0:256]) + _dot(nst, uw_ref[bi, :, 256:512])
        o_ref[bi] = (acc * scale).astype(BF16)


def _dft_tables(t, tmf):
    k = np.arange(t, dtype=np.int64)
    j1 = np.arange(t // tmf, dtype=np.int64) * tmf
    j0 = np.arange(tmf, dtype=np.int64)
    aa = (2.0 * np.pi / t) * ((j1[:, None] * k[None, :]) % t)
    ab = (2.0 * np.pi / t) * ((j0[:, None] * k[None, :]) % t)
    f = lambda a: jnp.asarray(a, F32)
    return f(np.cos(aa)), f(np.sin(aa)), f(np.cos(ab)), f(np.sin(ab))


def _fourier(uw):
    b, t, _ = uw.shape
    tmf = min(128, t)
    ca, sa, cb, sb = _dft_tables(t, tmf)
    scale = 1.0 / math.sqrt(t * HEAD_DIM)
    full2 = lambda j: (0, 0)
    return pl.pallas_call(
        functools.partial(_fourier_kernel, b, scale),
        grid=(t // tmf,),
        in_specs=[pl.BlockSpec((b, t, 512), lambda j: (0, 0, 0)),
                  pl.BlockSpec((t // tmf, t), full2),
                  pl.BlockSpec((t // tmf, t), full2),
                  pl.BlockSpec((tmf, t), full2),
                  pl.BlockSpec((tmf, t), full2)],
        out_specs=pl.BlockSpec((b, tmf, 256), lambda j: (0, j, 0)),
        out_shape=jax.ShapeDtypeStruct((b, t, 256), BF16),
        compiler_params=_cparams(("parallel",)),
        name="fourier_mix",
    )(uw, ca, sa, cb, sb)


def _win_attn_kernel(nct, nchunk, sink_ref, q_ref, k_ref, vt_ref, o_ref):
    i = pl.program_id(1)
    is_ctx = i < nct
    j = i - nct
    ctx_rows = nct * TM
    ctx_chunks = ctx_rows // LANES
    lw = TM + 2 * WINDOW
    lchunks = lw // LANES
    cs = jnp.clip(ctx_chunks + (TM // LANES) * j - WINDOW // LANES, 0, nchunk - lchunks)
    rs = pl.multiple_of(cs * LANES, LANES)
    qpos = j * TM + lax.broadcasted_iota(jnp.int32, (1, TM), 1)
    kpos = cs * LANES - ctx_rows + lax.broadcasted_iota(jnp.int32, (lw, 1), 0)
    allowed = (jnp.abs(qpos - kpos) <= WINDOW) & (kpos >= 0) & jnp.logical_not(is_ctx)
    lane = lax.broadcasted_iota(jnp.int32, (TM, LANES), 1)
    n_heads = q_ref.shape[-1] // HEAD_DIM
    group = n_heads // (k_ref.shape[-1] // LANES)
    for pair in range(n_heads // 2):
        qp = q_ref[0, :, pair * LANES:(pair + 1) * LANES]
        outs = []
        for half in range(2):
            hq = 2 * pair + half
            g = hq // group
            qm = jnp.where((lane >= HEAD_DIM) == (half == 1), qp, jnp.zeros_like(qp))
            kc = k_ref[0, 0:ctx_rows, g * LANES:(g + 1) * LANES]
            kl = k_ref[0, pl.ds(rs, lw), g * LANES:(g + 1) * LANES]
            s_c = _dot_nt(kc, qm)
            s_l = jnp.where(allowed, _dot_nt(kl, qm), NEG)
            sk = sink_ref[hq] * LOG2E
            m = jnp.maximum(jnp.maximum(jnp.max(s_c, axis=0, keepdims=True),
                                        jnp.max(s_l, axis=0, keepdims=True)), sk)
            p_c = jnp.exp2(s_c - m)
            p_l = jnp.exp2(s_l - m)
            l = (jnp.sum(p_c, axis=0, keepdims=True) + jnp.sum(p_l, axis=0, keepdims=True)
                 + jnp.exp2(sk - m))
            p_c = p_c.astype(BF16)
            p_l = p_l.astype(BF16)
            vrows = slice(g * HEAD_DIM, (g + 1) * HEAD_DIM)
            acc = jnp.zeros((HEAD_DIM, TM), F32)
            for c in range(ctx_chunks):
                acc = acc + _dot(vt_ref[0, c, vrows, :], p_c[c * LANES:(c + 1) * LANES, :])
            for c in range(lchunks):
                acc = acc + _dot(vt_ref[0, cs + c, vrows, :], p_l[c * LANES:(c + 1) * LANES, :])
            outs.append(acc / l)
        o2 = jnp.concatenate(outs, axis=0)
        o_ref[0, :, pair * LANES:(pair + 1) * LANES] = o2.T.astype(BF16)


def _win_attn(sink, q, k, vt, nct):
    b, tall, qw = q.shape
    nt = tall // TM
    nchunk = vt.shape[1]
    return pl.pallas_call(
        functools.partial(_win_attn_kernel, nct, nchunk),
        grid=(b, nt),
        in_specs=[pl.BlockSpec(memory_space=pltpu.SMEM),
                  pl.BlockSpec((1, TM, qw), lambda bi, i: (bi, i, 0)),
                  pl.BlockSpec((1, tall, k.shape[-1]), lambda bi, i: (bi, 0, 0)),
                  pl.BlockSpec((1, nchunk, vt.shape[2], LANES), lambda bi, i: (bi, 0, 0, 0))],
        out_specs=pl.BlockSpec((1, TM, qw), lambda bi, i: (bi, i, 0)),
        out_shape=jax.ShapeDtypeStruct((b, tall, qw), BF16),
        compiler_params=_cparams(("parallel", "parallel")),
        name="window_attention",
    )(sink, q, k, vt)


def _unit_operands(u):
    if u < 2 * N_C_HEADS:
        hc = u // 2
        dst, dv = _VT_LAYOUT[hc]
        return True, hc, dst, dv
    g = (u - 2 * N_C_HEADS) // (N_D_HEADS // N_D_KV)
    dst, dv = _VT_LAYOUT[N_C_HEADS + g]
    return False, g, dst, dv


def _attn_cd_kernel(nkt, lam_init, lam_ref, sg_ref, qc_ref, qd_ref, kc_ref, kd_ref, vt_ref, o_ref,
                    qm_s, m_s, acc_s, s_a, s_b):
    mq = qc_ref.shape[1]
    tk = vt_ref.shape[-1]
    lane = lax.broadcasted_iota(jnp.int32, (mq, LANES), 1)
    upper = lane >= HEAD_DIM
    for u in range(N_UNITS):
        src = qc_ref if u < 2 * N_C_HEADS else qd_ref
        pair = (u if u < 2 * N_C_HEADS else u - 2 * N_C_HEADS) // 2
        qp = src[0, :, pair * LANES:(pair + 1) * LANES]
        qm_s[u] = jnp.where(upper == (u % 2 == 1), qp, jnp.zeros_like(qp))
    m_s[...] = jnp.full(m_s.shape, NEG, F32)
    acc_s[...] = jnp.zeros(acc_s.shape, F32)

    def scores(tile, u, dst_s):
        is_c, slab, _, _ = _unit_operands(u)
        k_ref = kc_ref if is_c else kd_ref
        r0 = pl.multiple_of(tile * tk, tk)
        dst_s[u] = _dot_nt(k_ref[0, pl.ds(r0, tk), slab * LANES:(slab + 1) * LANES], qm_s[u])

    def consume(tile, u, src_s):
        _, _, row0, dv = _unit_operands(u)
        rows = dv + ONES_ROWS
        s = src_s[u]
        m_old = m_s[u]
        m_new = jnp.maximum(m_old, jnp.max(s, axis=0, keepdims=True))
        alpha = jnp.exp2(m_old - m_new)
        p = jnp.exp2(s - m_new).astype(BF16)
        pv = _dot(vt_ref[0, tile, row0:row0 + rows, :], p)
        acc_s[u, 0:rows, :] = acc_s[u, 0:rows, :] * alpha + pv
        m_s[u] = m_new

    def stage(tile, src_s, dst_s):
        nxt = jnp.minimum(tile + 1, nkt - 1)
        for u in range(N_UNITS):
            scores(nxt, u, dst_s)
            consume(tile, u, src_s)

    for u in range(N_UNITS):
        scores(0, u, s_a)
    stage(0, s_a, s_b)

    def body(pr, carry):
        stage(2 * pr + 1, s_b, s_a)
        stage(2 * pr + 2, s_a, s_b)
        return carry

    lax.fori_loop(0, (nkt - 1) // 2, body, 0)
    if (nkt - 1) % 2 == 1:
        stage(nkt - 1, s_b, s_a)

    lv = lam_ref[...]
    lam = (jnp.exp(jnp.sum(lv[0:1] * lv[1:2], axis=-1, keepdims=True))
           - jnp.exp(jnp.sum(lv[2:3] * lv[3:4], axis=-1, keepdims=True)) + lam_init)
    sg = sg_ref[...]
    dvc = 2 * HEAD_DIM
    for hc in range(N_C_HEADS):
        o1 = acc_s[2 * hc, 0:dvc, :] / acc_s[2 * hc, dvc:dvc + 1, :]
        o2 = acc_s[2 * hc + 1, 0:dvc, :] / acc_s[2 * hc + 1, dvc:dvc + 1, :]
        o = o1 - lam * o2
        ms = jnp.mean(o * o, axis=0, keepdims=True)
        o = o * lax.rsqrt(ms + RMS_EPS) * sg * (1.0 - lam_init)
        o_ref[0, :, hc * LANES:(hc + 1) * LANES] = o.T.astype(BF16)
    for pair in range(N_D_HEADS // 2):
        outs = []
        for half in range(2):
            u = 2 * N_C_HEADS + 2 * pair + half
            outs.append(acc_s[u, 0:HEAD_DIM, :] / acc_s[u, HEAD_DIM:HEAD_DIM + 1, :])
        o2 = jnp.concatenate(outs, axis=0)
        c0 = N_C_HEADS * LANES + pair * LANES
        o_ref[0, :, c0:c0 + LANES] = o2.T.astype(BF16)


def _attn_cd(lam_vec, sg, qc, qd, kc, kd, vt, nct, lam_init):
    b, tall, _ = qc.shape
    nkt, nv, tk = vt.shape[1], vt.shape[2], vt.shape[3]
    mq = TM
    nq = tall // mq - nct
    ow = N_C_HEADS * LANES + N_D_HEADS * HEAD_DIM
    qtile = lambda bi, i: (bi, i + nct, 0)
    return pl.pallas_call(
        functools.partial(_attn_cd_kernel, nkt, lam_init),
        grid=(b, nq),
        in_specs=[pl.BlockSpec((4, HEAD_DIM), lambda bi, i: (0, 0)),
                  pl.BlockSpec((LANES, 1), lambda bi, i: (0, 0)),
                  pl.BlockSpec((1, mq, qc.shape[-1]), qtile),
                  pl.BlockSpec((1, mq, qd.shape[-1]), qtile),
                  pl.BlockSpec((1, tall, kc.shape[-1]), lambda bi, i: (bi, 0, 0)),
                  pl.BlockSpec((1, tall, kd.shape[-1]), lambda bi, i: (bi, 0, 0)),
                  pl.BlockSpec((1, nkt, nv, tk), lambda bi, i: (bi, 0, 0, 0))],
        out_specs=pl.BlockSpec((1, mq, ow), lambda bi, i: (bi, i, 0)),
        out_shape=jax.ShapeDtypeStruct((b, nq * mq, ow), BF16),
        scratch_shapes=[pltpu.VMEM((N_UNITS, mq, LANES), BF16),
                        pltpu.VMEM((N_UNITS, 1, mq), F32),
                        pltpu.VMEM((N_UNITS, 2 * HEAD_DIM + ONES_ROWS, mq), F32),
                        pltpu.VMEM((N_UNITS, tk, mq), F32),
                        pltpu.VMEM((N_UNITS, tk, mq), F32)],
        compiler_params=_cparams(("parallel", "parallel")),
        name="attention_cd",
    )(lam_vec, sg, qc, qd, kc, kd, vt)


def _oproj_kernel(n_parts, widths, nct, alpha, n_exp, *refs):
    o_refs = refs[:n_parts]
    (w_ref, x_ref, mlat_ref, mctx_ref, lng_ref, lnb_ref, rwt_ref, rb_ref,
     x1_ref, h2_ref, te_ref, tg_ref) = refs[n_parts:]
    d = x_ref.shape[-1]
    is_ctx = pl.program_id(1) < nct
    y = None
    r0 = 0
    for o_ref, wd in zip(o_refs, widths):
        part = _dot(o_ref[0], w_ref[r0:r0 + wd, :])
        y = part if y is None else y + part
        r0 += wd
    m = jnp.where(is_ctx, mctx_ref[...], mlat_ref[0])
    x1 = _layernorm(alpha * x_ref[0] + m[:, 2 * d:3 * d] * y, lng_ref[...], lnb_ref[...])
    x1_ref[0] = x1
    h2 = x1 * (1.0 + m[:, 4 * d:5 * d]) + m[:, 3 * d:4 * d]
    h2_ref[0] = h2
    logits = _dot_nt(rwt_ref[...], h2.astype(BF16)) + rb_ref[...]
    tm = logits.shape[1]
    row = lax.broadcasted_iota(jnp.int32, logits.shape, 0)
    vals, idxs = [], []
    for _ in range(TOP_K):
        mx = jnp.max(logits, axis=0, keepdims=True)
        idx = jnp.min(jnp.where(logits == mx, row, n_exp), axis=0, keepdims=True)
        vals.append(mx)
        idxs.append(idx)
        logits = jnp.where(row == idx, NEG, logits)
    es = [jnp.exp(v - vals[0]) for v in vals]
    tot = es[0] + es[1] + es[2] + es[3]
    pad_i = jnp.zeros((8 - TOP_K, tm), jnp.int32)
    pad_f = jnp.zeros((8 - TOP_K, tm), F32)
    te_ref[0] = jnp.concatenate(idxs + [pad_i], axis=0)
    tg_ref[0] = jnp.concatenate([e / tot for e in es] + [pad_f], axis=0)


def _oproj(parts, w_out, x_all, mlat, mctx, lng, lnb, rwt, rb, nct_out, x_tile_off, alpha):
    b, t, _ = parts[0].shape
    d = x_all.shape[-1]
    nt = t // TM
    n_exp = rwt.shape[0]
    widths = tuple(p.shape[-1] for p in parts)
    tile = lambda bi, i: (bi, i, 0)
    full = lambda bi, i: (0, 0)
    in_specs = [pl.BlockSpec((1, TM, wd), tile) for wd in widths] + [
        pl.BlockSpec(w_out.shape, full),
        pl.BlockSpec((1, TM, d), lambda bi, i: (bi, i + x_tile_off, 0)),
        pl.BlockSpec((1, 1, N_MOD * d), lambda bi, i: (bi, 0, 0)),
        pl.BlockSpec((1, N_MOD * d), full),
        pl.BlockSpec((1, d), full),
        pl.BlockSpec((1, d), full),
        pl.BlockSpec((n_exp, d), full),
        pl.BlockSpec((n_exp, 1), full)]
    return pl.pallas_call(
        functools.partial(_oproj_kernel, len(parts), widths, nct_out, alpha, n_exp),
        grid=(b, nt),
        in_specs=in_specs,
        out_specs=[pl.BlockSpec((1, TM, d), tile),
                   pl.BlockSpec((1, TM, d), tile),
                   pl.BlockSpec((1, 8, TM), lambda bi, i: (bi, 0, i)),
                   pl.BlockSpec((1, 8, TM), lambda bi, i: (bi, 0, i))],
        out_shape=[jax.ShapeDtypeStruct((b, t, d), F32),
                   jax.ShapeDtypeStruct((b, t, d), F32),
                   jax.ShapeDtypeStruct((b, 8, t), jnp.int32),
                   jax.ShapeDtypeStruct((b, 8, t), F32)],
        compiler_params=_cparams(("parallel", "parallel")),
        name="out_proj_norm_router",
    )(*parts, w_out, x_all, mlat, mctx, lng, lnb, rwt, rb)


def _moe_kernel(be_ref, nu_ref, x_ref, wgu_ref, bgu_ref, wd_ref, bd_ref, y_ref, wgu_s, wd_s):
    blk = pl.program_id(0)
    ff = wd_ref.shape[1]

    @pl.when(blk < nu_ref[0])
    def _():
        prev = be_ref[jnp.maximum(blk - 1, 0)]

        @pl.when((blk == 0) | (be_ref[blk] != prev))
        def _():
            rows = 128

            def cast_gu(r, c):
                r0 = pl.multiple_of(r * rows, rows)
                wgu_s[pl.ds(r0, rows), :] = wgu_ref[0, pl.ds(r0, rows), :].astype(BF16)
                return c

            def cast_d(r, c):
                r0 = pl.multiple_of(r * rows, rows)
                wd_s[pl.ds(r0, rows), :] = wd_ref[0, pl.ds(r0, rows), :].astype(BF16)
                return c

            lax.fori_loop(0, wgu_s.shape[0] // rows, cast_gu, 0)
            lax.fori_loop(0, wd_s.shape[0] // rows, cast_d, 0)

        gu = _dot(x_ref[...].astype(BF16), wgu_s[...]) + bgu_ref[0]
        g = jnp.minimum(gu[:, :ff], SWIGLU_LIMIT)
        u = jnp.clip(gu[:, ff:], -SWIGLU_LIMIT, SWIGLU_LIMIT)
        act = (u + 1.0) * (g / (1.0 + jnp.exp(-SWIGLU_ALPHA * g)))
        y_ref[...] = _dot(act.astype(BF16), wd_s[...]) + bd_ref[0]

    @pl.when(blk >= nu_ref[0])
    def _():
        y_ref[...] = jnp.zeros(y_ref.shape, F32)


def _moe_experts(block_e, n_used, xb, w_gu, b_gu, w_down, b_down):
    n_slots, d = xb.shape
    ff2 = w_gu.shape[-1]
    ff = w_down.shape[-2]
    n_exp = w_gu.shape[0] * w_gu.shape[1]
    w_gu = w_gu.reshape(n_exp, d, ff2)
    w_down = w_down.reshape(n_exp, ff, d)
    nb = n_slots // MOE_BM
    xblk = lambda i, be, nu: (jnp.minimum(i, nu[0] - 1), 0)
    wblk = lambda i, be, nu: (be[i], 0, 0)
    grid_spec = pltpu.PrefetchScalarGridSpec(
        num_scalar_prefetch=2,
        grid=(nb,),
        in_specs=[pl.BlockSpec((MOE_BM, d), xblk),
                  pl.BlockSpec((1, d, ff2), wblk),
                  pl.BlockSpec((1, 1, ff2), wblk),
                  pl.BlockSpec((1, ff, d), wblk),
                  pl.BlockSpec((1, 1, d), wblk)],
        out_specs=pl.BlockSpec((MOE_BM, d), lambda i, be, nu: (i, 0)),
        scratch_shapes=[pltpu.VMEM((d, ff2), BF16), pltpu.VMEM((ff, d), BF16)])
    return pl.pallas_call(
        _moe_kernel,
        grid_spec=grid_spec,
        out_shape=jax.ShapeDtypeStruct((n_slots, d), F32),
        compiler_params=_cparams(("arbitrary",)),
        name="moe_experts",
    )(block_e, n_used, xb, w_gu, b_gu.reshape(n_exp, 1, ff2), w_down, b_down.reshape(n_exp, 1, d))


def _expert_onehot(te, n_exp):
    row = lax.broadcasted_iota(jnp.int32, (n_exp, te.shape[1]), 0)
    oh = jnp.zeros(row.shape, F32)
    for k in range(TOP_K):
        oh = oh + (row == te[k:k + 1, :]).astype(F32)
    return row, oh


def _route_count_kernel(n_exp, te_ref, cnt_ref):
    first = (pl.program_id(0) == 0) & (pl.program_id(1) == 0)

    @pl.when(first)
    def _():
        cnt_ref[...] = jnp.zeros(cnt_ref.shape, F32)

    _, oh = _expert_onehot(te_ref[0], n_exp)
    cnt_ref[...] += jnp.sum(oh, axis=1, keepdims=True)


def _route_slot_kernel(n_exp, te_ref, pstart_ref, tri_ref, slot_ref, carry_s):
    first = (pl.program_id(0) == 0) & (pl.program_id(1) == 0)

    @pl.when(first)
    def _():
        carry_s[...] = pstart_ref[...]

    te = te_ref[0]
    row, oh = _expert_onehot(te, n_exp)
    before = _dot(oh.astype(BF16), tri_ref[...]) + carry_s[...]
    slots = [jnp.sum(jnp.where(row == te[k:k + 1, :], before, 0.0), axis=0, keepdims=True)
             for k in range(TOP_K)]
    pad = jnp.zeros((8 - TOP_K, te.shape[1]), F32)
    slot_ref[0] = jnp.concatenate(slots + [pad], axis=0).astype(jnp.int32)
    carry_s[...] += jnp.sum(oh, axis=1, keepdims=True)


def _route(te, n_exp):
    b, _, t = te.shape
    tile = pl.BlockSpec((1, 8, TM), lambda bi, i: (bi, 0, i))
    vec = pl.BlockSpec((n_exp, 1), lambda bi, i: (0, 0))
    cnt = pl.pallas_call(
        functools.partial(_route_count_kernel, n_exp),
        grid=(b, t // TM),
        in_specs=[tile],
        out_specs=vec,
        out_shape=jax.ShapeDtypeStruct((n_exp, 1), F32),
        compiler_params=_cparams(("arbitrary", "arbitrary")),
        name="route_count",
    )(te)
    counts = cnt[:, 0].astype(jnp.int32)
    padded = (counts + MOE_BM - 1) // MOE_BM * MOE_BM
    pend = jnp.cumsum(padded)
    pstart = (pend - padded).astype(F32).reshape(n_exp, 1)
    tri = jnp.asarray(np.triu(np.ones((TM, TM)), 1), BF16)
    slot = pl.pallas_call(
        functools.partial(_route_slot_kernel, n_exp),
        grid=(b, t // TM),
        in_specs=[tile, vec, pl.BlockSpec((TM, TM), lambda bi, i: (0, 0))],
        out_specs=tile,
        out_shape=jax.ShapeDtypeStruct((b, 8, t), jnp.int32),
        scratch_shapes=[pltpu.VMEM((n_exp, 1), F32)],
        compiler_params=_cparams(("arbitrary", "arbitrary")),
        name="route_slot",
    )(te, pstart, tri)
    n_blocks = -(-(b * t * TOP_K) // MOE_BM) + n_exp
    n_used = (pend[-1] // MOE_BM).astype(jnp.int32)
    blk = jnp.minimum(jnp.arange(n_blocks, dtype=jnp.int32), n_used - 1) * MOE_BM
    block_e = jnp.minimum(jnp.sum((pend[None, :] <= blk[:, None]).astype(jnp.int32), axis=1), n_exp - 1)
    slot_flat = slot[:, :TOP_K].reshape(b, TOP_K, t // TM, TM).transpose(0, 2, 1, 3).reshape(-1)
    return slot_flat, pend.astype(jnp.int32), block_e, n_used.reshape(1), n_blocks


def _row_copy(src_ref, src_row, dst_ref, dst_row, sem):
    return pltpu.make_async_copy(src_ref.at[pl.ds(src_row, 1)], dst_ref.at[pl.ds(dst_row, 1)], sem)


def _moe_scatter_kernel(n_exp, pend_ref, slot_ref, h_ref, xb_ref, zero_s, sem):
    first = (pl.program_id(0) == 0) & (pl.program_id(1) == 0)
    tm = h_ref.shape[1]

    @pl.when(first)
    def _():
        zero_s[...] = jnp.zeros(zero_s.shape, zero_s.dtype)
        for phase in ("start", "wait"):
            for e in range(n_exp):
                lo = pend_ref[e - 1] if e else 0

                @pl.when(pend_ref[e] > lo)
                def _():
                    r0 = pl.multiple_of(pend_ref[e] - MOE_BM, MOE_BM)
                    cp = pltpu.make_async_copy(zero_s, xb_ref.at[pl.ds(r0, MOE_BM)], sem)
                    cp.start() if phase == "start" else cp.wait()

            def unused_block(blk, c):
                r0 = pl.multiple_of(blk * MOE_BM, MOE_BM)
                cp = pltpu.make_async_copy(zero_s, xb_ref.at[pl.ds(r0, MOE_BM)], sem)
                cp.start() if phase == "start" else cp.wait()
                return c

            lax.fori_loop(pend_ref[n_exp - 1] // MOE_BM, xb_ref.shape[0] // MOE_BM, unused_block, 0)

    def issue(j, c):
        for k in range(TOP_K):
            _row_copy(h_ref.at[0], j, xb_ref, slot_ref[k * tm + j], sem).start(priority=k % 2)
        return c

    def drain(j, c):
        for k in range(TOP_K):
            _row_copy(h_ref.at[0], j, xb_ref, slot_ref[k * tm + j], sem).wait()
        return c

    lax.fori_loop(0, tm, issue, 0)
    lax.fori_loop(0, tm, drain, 0)


def _moe_scatter(pend, slot_flat, h2, n_slots):
    b, t, d = h2.shape
    n_exp = pend.shape[0]
    grid_spec = pltpu.PrefetchScalarGridSpec(
        num_scalar_prefetch=1,
        grid=(b, t // TM),
        in_specs=[pl.BlockSpec((TOP_K * TM,), lambda bi, i, pe: (bi * (t // TM) + i,), memory_space=pltpu.SMEM),
                  pl.BlockSpec((1, TM, d), lambda bi, i, pe: (bi, i, 0))],
        out_specs=pl.BlockSpec(memory_space=pl.ANY),
        scratch_shapes=[pltpu.VMEM((MOE_BM, d), F32), pltpu.SemaphoreType.DMA])
    return pl.pallas_call(
        functools.partial(_moe_scatter_kernel, n_exp),
        grid_spec=grid_spec,
        out_shape=jax.ShapeDtypeStruct((n_slots, d), F32),
        compiler_params=_cparams(("arbitrary", "arbitrary")),
        name="moe_scatter_rows",
    )(pend, slot_flat, h2)


def _combine_kernel(nct, alpha, slot_ref, yb_ref, tg_ref, x1_ref, mlat_ref, mctx_ref, lng_ref, lnb_ref,
                    o_ref, buf_s, sem):
    d = x1_ref.shape[-1]
    tm = x1_ref.shape[1]

    def issue(j, c):
        for k in range(TOP_K):
            _row_copy(yb_ref, slot_ref[k * tm + j], buf_s.at[k], j, sem).start(priority=k % 2)
        return c

    def drain(j, c):
        for k in range(TOP_K):
            _row_copy(yb_ref, slot_ref[k * tm + j], buf_s.at[k], j, sem).wait()
        return c

    lax.fori_loop(0, tm, issue, 0)
    is_ctx = pl.program_id(1) < nct
    m = jnp.where(is_ctx, mctx_ref[...], mlat_ref[0])
    gates = tg_ref[0]
    lax.fori_loop(0, tm, drain, 0)
    f = gates[:, 0:1] * buf_s[0]
    for k in range(1, TOP_K):
        f = f + gates[:, k:k + 1] * buf_s[k]
    o_ref[0] = _layernorm(alpha * x1_ref[0] + m[:, 5 * d:6 * d] * f, lng_ref[...], lnb_ref[...])


def _combine_postnorm(slot_flat, yb, tg, x1, mlat, mctx, lng, lnb, nct, alpha):
    b, t, d = x1.shape
    tile = lambda bi, i: (bi, i, 0)
    full = lambda bi, i: (0, 0)
    return pl.pallas_call(
        functools.partial(_combine_kernel, nct, alpha),
        grid=(b, t // TM),
        in_specs=[pl.BlockSpec((TOP_K * TM,), lambda bi, i: (bi * (t // TM) + i,), memory_space=pltpu.SMEM),
                  pl.BlockSpec(memory_space=pl.ANY),
                  pl.BlockSpec((1, TM, 8), tile),
                  pl.BlockSpec((1, TM, d), tile),
                  pl.BlockSpec((1, 1, N_MOD * d), lambda bi, i: (bi, 0, 0)),
                  pl.BlockSpec((1, N_MOD * d), full),
                  pl.BlockSpec((1, d), full),
                  pl.BlockSpec((1, d), full)],
        out_specs=pl.BlockSpec((1, TM, d), tile),
        out_shape=jax.ShapeDtypeStruct((b, t, d), F32),
        scratch_shapes=[pltpu.VMEM((TOP_K, TM, d), F32), pltpu.SemaphoreType.DMA],
        compiler_params=_cparams(("parallel", "parallel")),
        name="combine_post_norm",
    )(slot_flat, yb, tg, x1, mlat, mctx, lng, lnb)


def _rope_tables(ctx_len, seq):
    t = np.arange(seq)
    row = (t // GRID_W).astype(np.float64)
    col = (t % GRID_W).astype(np.float64)
    nf = HEAD_DIM // 4
    inv = ROPE_THETA ** (-np.arange(nf, dtype=np.float64) / nf)
    ar = row[:, None] * inv[None, :]
    ac = col[:, None] * inv[None, :]
    ang = np.concatenate([ar, ar, ac, ac], axis=-1)
    cos = np.concatenate([np.ones((ctx_len, HEAD_DIM)), np.cos(ang)], axis=0)
    sin = np.concatenate([np.zeros((ctx_len, HEAD_DIM)), np.sin(ang)], axis=0)
    sign = np.where((np.arange(HEAD_DIM) % 32) < 16, -1.0, 1.0)[None, :]
    cos2 = np.tile(cos, (1, LANES // HEAD_DIM))
    sin2 = np.tile(sin * sign, (1, LANES // HEAD_DIM))
    return jnp.asarray(cos2, F32), jnp.asarray(sin2, F32)


def _channel_dft():
    c = np.arange(HEAD_DIM)
    ang = 2.0 * np.pi * ((c[:, None] * c[None, :]) % HEAD_DIM) / HEAD_DIM
    eye = np.eye(4)
    cs = np.concatenate([np.kron(eye, np.cos(ang)), np.kron(eye, np.sin(ang))], axis=1)
    return jnp.asarray(cs, BF16)


def _group_mean_matrix():
    bd = np.kron(np.eye(LANES // HEAD_DIM), np.full((HEAD_DIM, HEAD_DIM), 1.0 / HEAD_DIM))
    return jnp.asarray(bd, BF16)


def _dup_heads(wk, n_heads):
    d = wk.shape[0]
    return jnp.broadcast_to(wk.reshape(d, n_heads, 1, HEAD_DIM), (d, n_heads, 2, HEAD_DIM)).reshape(d, n_heads * LANES)


def _lambda_init(layer):
    return 0.8 - 0.6 * math.exp(-0.3 * layer)


def _moe_postnorm(h2, te, tg, x1, mlat, mctx, lng, lnb, nct, alpha, layer, w_gu, b_gu, w_down, b_down):
    n_exp = w_gu.shape[1]
    slot_flat, pend, block_e, n_used, n_blocks = _route(te, n_exp)
    xb = _moe_scatter(pend, slot_flat, h2, n_blocks * MOE_BM)
    yb = _moe_experts(block_e + layer * n_exp, n_used, xb, w_gu, b_gu, w_down, b_down)
    return _combine_postnorm(slot_flat, yb, jnp.transpose(tg, (0, 2, 1)), x1, mlat, mctx, lng, lnb, nct, alpha)


def kernel(x, c, ctx, c_ctx, mod_w, mod_b, ln_g, ln_b, ab_w_in, ab_sink, ab_w_out,
           cd_w_in, cd_lambda, cd_subln_g, cd_q_norm_g, cd_k_norm_g, cd_w_out,
           router_w, router_b, expert_w_gu, expert_b_gu, expert_w_down, expert_b_down):
    b, s, d = x.shape
    n_ctx = ctx.shape[1]
    depth = mod_w.shape[0]
    n_exp = router_w.shape[-1]
    assert d == 16 * HEAD_DIM and n_ctx % TM == 0 and s % TM == 0 and s % GRID_W == 0
    nct = n_ctx // TM
    alpha = (2 * depth) ** 0.25

    cos, sin = _rope_tables(n_ctx, s)
    cs_dft = _channel_dft()
    bd = _group_mean_matrix()
    mods = _mod_vectors(c, c_ctx, mod_w, mod_b)

    x_all = jnp.concatenate([ctx, x], axis=1)
    for l in range(depth):
        last = l == depth - 1
        i = l // 2
        mlat = mods[l, :b].reshape(b, 1, N_MOD * d)
        mctx = mods[l, b:b + 1]
        lng1, lnb1 = ln_g[l, 0].reshape(1, d), ln_b[l, 0].reshape(1, d)
        lng2, lnb2 = ln_g[l, 1].reshape(1, d), ln_b[l, 1].reshape(1, d)
        rwt = router_w[l].T.astype(BF16)
        rb = router_b[l].reshape(n_exp, 1)
        if l % 2 == 0:
            w = ab_w_in[i]
            w_n = jnp.concatenate([w[:, :1024], _dup_heads(w[:, 1024:1280], 4)], axis=1).astype(BF16)
            wvt = w[:, 1280:1536].T.astype(BF16)
            uw, q, k, vt = _proj_ab(x_all, mlat, mctx, w_n, wvt, cos, sin, cs_dft, nct)
            oa = jnp.concatenate([_fourier(uw[:, :n_ctx]), _fourier(uw[:, n_ctx:])], axis=1)
            ob = _win_attn(ab_sink[i], q, k, vt, nct)
            if last:
                parts = [oa[:, n_ctx:], ob[:, n_ctx:]]
            else:
                parts = [oa, ob]
            w_out = ab_w_out[i].astype(BF16)
        else:
            w = cd_w_in[i]
            w_n = jnp.concatenate([w[:, :1536], _dup_heads(w[:, 1536:1664], 2)], axis=1).astype(BF16)
            wvt = w[:, 1664:2304].T.astype(BF16)
            gq = jnp.tile(cd_q_norm_g[i], 2).reshape(1, LANES)
            gk = jnp.tile(cd_k_norm_g[i], 2).reshape(1, LANES)
            qc, qd, kc, kd, vt = _proj_cd(x_all, mlat, mctx, w_n, wvt, cos, sin, bd, gq, gk, nct)
            sg = cd_subln_g[i].reshape(LANES, 1)
            o_lat = _attn_cd(cd_lambda[i], sg, qc, qd, kc, kd, vt, nct, _lambda_init(l))
            if last:
                parts = [o_lat]
            else:
                raise NotImplementedError("context outputs of a differential/axial layer")
            w_out = cd_w_out[i].astype(BF16)
        experts = (l, expert_w_gu, expert_b_gu, expert_w_down, expert_b_down)
        if last:
            x1, h2, te, tg = _oproj(parts, w_out, x_all, mlat, mctx, lng1, lnb1, rwt, rb, 0, nct, alpha)
            return _moe_postnorm(h2, te, tg, x1, mlat, mctx, lng2, lnb2, 0, alpha, *experts)
        x1, h2, te, tg = _oproj(parts, w_out, x_all, mlat, mctx, lng1, lnb1, rwt, rb, nct, 0, alpha)
        x_all = _moe_postnorm(h2, te, tg, x1, mlat, mctx, lng2, lnb2, nct, alpha, *experts)
    return x_all[:, n_ctx:]
```

```python
import functools
import math

import jax
import jax.numpy as jnp
import numpy as np
from jax import lax
from jax.experimental import pallas as pl
from jax.experimental.pallas import tpu as pltpu

F32 = jnp.float32
BF16 = jnp.bfloat16

HEAD_DIM = 64
GRID_W = 64
WINDOW = 128
ROPE_THETA = 10000.0
LN_EPS = 1e-6
RMS_EPS = 1e-6
N_MOD = 6
TOP_K = 4
SWIGLU_LIMIT = 7.0
SWIGLU_ALPHA = 1.702
NEG = -1e30
LOG2E = 1.4426950408889634
QSCALE = HEAD_DIM ** -0.5 * LOG2E

N_C_HEADS = 4
N_D_HEADS = 8
N_D_KV = 2
N_UNITS = 2 * N_C_HEADS + N_D_HEADS
ONES_ROWS = 16
_VT_LAYOUT = tuple((h * (2 * HEAD_DIM + ONES_ROWS), 2 * HEAD_DIM) for h in range(N_C_HEADS)) + tuple(
    (N_C_HEADS * (2 * HEAD_DIM + ONES_ROWS) + g * (HEAD_DIM + ONES_ROWS), HEAD_DIM) for g in range(N_D_KV))
VT_ROWS = _VT_LAYOUT[-1][0] + HEAD_DIM + ONES_ROWS

LANES = 128
TM = 256
MOE_BM = 256
VMEM_LIMIT = 56 * 1024 * 1024


def _dot(a, b):
    return jnp.dot(a, b, preferred_element_type=F32)


def _dot_nt(a, b):
    return lax.dot_general(a, b, (((1,), (1,)), ((), ())), preferred_element_type=F32)


def _cparams(sem):
    return pltpu.CompilerParams(dimension_semantics=sem, vmem_limit_bytes=VMEM_LIMIT)


def _layernorm(z, g, b):
    mu = jnp.mean(z, axis=-1, keepdims=True)
    d = z - mu
    var = jnp.mean(d * d, axis=-1, keepdims=True)
    return d * lax.rsqrt(var + LN_EPS) * g + b


def _rope128(x, cos, sin_signed):
    lane = lax.broadcasted_iota(jnp.int32, x.shape, 1)
    lo = (lane % 32) < 16
    rot = jnp.where(lo, pltpu.roll(x, LANES - 16, 1), pltpu.roll(x, 16, 1))
    return x * cos + rot * sin_signed


def _rmsnorm128(x, g, bd):
    x2 = x * x
    hi = x2.astype(BF16)
    lo = (x2 - hi.astype(F32)).astype(BF16)
    ms = _dot(hi, bd) + _dot(lo, bd)
    return x * lax.rsqrt(ms + RMS_EPS) * g


def _modulated(x_ref, mlat_ref, mctx_ref, is_ctx, d):
    m = jnp.where(is_ctx, mctx_ref[...], mlat_ref[0])
    return (x_ref[0] * (1.0 + m[:, d:2 * d]) + m[:, 0:d]).astype(BF16)


def _mod_kernel(c_ref, w_ref, b_ref, o_ref):
    c = c_ref[...]
    s = (c / (1.0 + jnp.exp(-c))).astype(BF16)
    o_ref[0] = _dot(s, w_ref[0].astype(BF16)) + b_ref[0]


def _mod_vectors(c, c_ctx, mod_w, mod_b):
    depth, d, n = mod_w.shape
    b = c.shape[0]
    rows = 8 * (-(-(b + 1) // 8))
    cs = jnp.zeros((rows, d), F32).at[:b].set(c).at[b].set(c_ctx)
    tn = 1536
    out = pl.pallas_call(
        _mod_kernel,
        grid=(depth, n // tn),
        in_specs=[pl.BlockSpec((rows, d), lambda l, j: (0, 0)),
                  pl.BlockSpec((1, d, tn), lambda l, j: (l, 0, j)),
                  pl.BlockSpec((1, 1, tn), lambda l, j: (l, 0, j))],
        out_specs=pl.BlockSpec((1, rows, tn), lambda l, j: (l, 0, j)),
        out_shape=jax.ShapeDtypeStruct((depth, rows, n), F32),
        compiler_params=_cparams(("parallel", "parallel")),
        name="mod_vectors",
    )(cs, mod_w, mod_b.reshape(depth, 1, n))
    return out


def _proj_ab_kernel(nct, x_ref, mlat_ref, mctx_ref, w_ref, wvt_ref, cos_ref, sin_ref, cs_ref,
                    uw_ref, q_ref, k_ref, vt_ref):
    is_ctx = pl.program_id(1) < nct
    d = x_ref.shape[-1]
    h = _modulated(x_ref, mlat_ref, mctx_ref, is_ctx, d)
    p = _dot(h, w_ref[...])
    uw_ref[0] = _dot(p[:, 0:256].astype(BF16), cs_ref[...]).astype(BF16)
    cos = cos_ref[...]
    sin = sin_ref[...]
    for j in range(6):
        c0 = 256 + j * LANES
        q_ref[0, :, j * LANES:(j + 1) * LANES] = (_rope128(p[:, c0:c0 + LANES], cos, sin) * QSCALE).astype(BF16)
    for j in range(4):
        c0 = 1024 + j * LANES
        k_ref[0, :, j * LANES:(j + 1) * LANES] = _rope128(p[:, c0:c0 + LANES], cos, sin).astype(BF16)
    vt = _dot_nt(wvt_ref[...], h).astype(BF16)
    for c in range(TM // LANES):
        vt_ref[0, c] = vt[:, c * LANES:(c + 1) * LANES]


def _proj_ab(x_all, mlat, mctx, w, wvt, cos, sin, cs, nct):
    b, tall, d = x_all.shape
    nt = tall // TM
    nw = w.shape[1]
    nchunk = tall // LANES
    cpt = TM // LANES
    return pl.pallas_call(
        functools.partial(_proj_ab_kernel, nct),
        grid=(b, nt),
        in_specs=[pl.BlockSpec((1, TM, d), lambda bi, i: (bi, i, 0)),
                  pl.BlockSpec((1, 1, N_MOD * d), lambda bi, i: (bi, 0, 0)),
                  pl.BlockSpec((1, N_MOD * d), lambda bi, i: (0, 0)),
                  pl.BlockSpec((d, nw), lambda bi, i: (0, 0)),
                  pl.BlockSpec((256, d), lambda bi, i: (0, 0)),
                  pl.BlockSpec((TM, LANES), lambda bi, i: (i, 0)),
                  pl.BlockSpec((TM, LANES), lambda bi, i: (i, 0)),
                  pl.BlockSpec((256, 512), lambda bi, i: (0, 0))],
        out_specs=[pl.BlockSpec((1, TM, 512), lambda bi, i: (bi, i, 0)),
                   pl.BlockSpec((1, TM, 768), lambda bi, i: (bi, i, 0)),
                   pl.BlockSpec((1, TM, 512), lambda bi, i: (bi, i, 0)),
                   pl.BlockSpec((1, cpt, 256, LANES), lambda bi, i: (bi, i, 0, 0))],
        out_shape=[jax.ShapeDtypeStruct((b, tall, 512), BF16),
                   jax.ShapeDtypeStruct((b, tall, 768), BF16),
                   jax.ShapeDtypeStruct((b, tall, 512), BF16),
                   jax.ShapeDtypeStruct((b, nchunk, 256, LANES), BF16)],
        compiler_params=_cparams(("parallel", "parallel")),
        name="proj_ab",
    )(x_all, mlat, mctx, w, wvt, cos, sin, cs)


def _proj_cd_kernel(nct, x_ref, mlat_ref, mctx_ref, w_ref, wvt_ref, cos_ref, sin_ref, bd_ref,
                    gq_ref, gk_ref, qc_ref, qd_ref, kc_ref, kd_ref, vt_ref):
    is_ctx = pl.program_id(1) < nct
    d = x_ref.shape[-1]
    h = _modulated(x_ref, mlat_ref, mctx_ref, is_ctx, d)
    p = _dot(h, w_ref[...])
    cos = cos_ref[...]
    sin = sin_ref[...]
    bd = bd_ref[...]
    gq = gq_ref[...]
    gk = gk_ref[...]
    for j in range(4):
        sl = slice(j * LANES, (j + 1) * LANES)
        qc_ref[0, :, sl] = (_rope128(p[:, j * LANES:(j + 1) * LANES], cos, sin) * QSCALE).astype(BF16)
        c0 = 512 + j * LANES
        qd_ref[0, :, sl] = (_rope128(_rmsnorm128(p[:, c0:c0 + LANES], gq, bd), cos, sin) * QSCALE).astype(BF16)
        c0 = 1024 + j * LANES
        kc_ref[0, :, sl] = _rope128(p[:, c0:c0 + LANES], cos, sin).astype(BF16)
    for j in range(2):
        c0 = 1536 + j * LANES
        kd_ref[0, :, j * LANES:(j + 1) * LANES] = _rope128(
            _rmsnorm128(p[:, c0:c0 + LANES], gk, bd), cos, sin).astype(BF16)
    vt = _dot_nt(wvt_ref[...], h).astype(BF16)
    ones = jnp.ones((ONES_ROWS, vt.shape[1]), BF16)
    src = 0
    for dst, dv in _VT_LAYOUT:
        vt_ref[0, 0, dst:dst + dv, :] = vt[src:src + dv, :]
        vt_ref[0, 0, dst + dv:dst + dv + ONES_ROWS, :] = ones
        src += dv


def _proj_cd(x_all, mlat, mctx, w, wvt, cos, sin, bd, gq, gk, nct):
    b, tall, d = x_all.shape
    nt = tall // TM
    nw = w.shape[1]
    nv = VT_ROWS
    full = lambda bi, i: (0, 0)
    tile = lambda bi, i: (bi, i, 0)
    return pl.pallas_call(
        functools.partial(_proj_cd_kernel, nct),
        grid=(b, nt),
        in_specs=[pl.BlockSpec((1, TM, d), tile),
                  pl.BlockSpec((1, 1, N_MOD * d), lambda bi, i: (bi, 0, 0)),
                  pl.BlockSpec((1, N_MOD * d), full),
                  pl.BlockSpec((d, nw), full),
                  pl.BlockSpec(wvt.shape, full),
                  pl.BlockSpec((TM, LANES), lambda bi, i: (i, 0)),
                  pl.BlockSpec((TM, LANES), lambda bi, i: (i, 0)),
                  pl.BlockSpec((LANES, LANES), full),
                  pl.BlockSpec((1, LANES), full),
                  pl.BlockSpec((1, LANES), full)],
        out_specs=[pl.BlockSpec((1, TM, 512), tile),
                   pl.BlockSpec((1, TM, 512), tile),
                   pl.BlockSpec((1, TM, 512), tile),
                   pl.BlockSpec((1, TM, 256), tile),
                   pl.BlockSpec((1, 1, nv, TM), lambda bi, i: (bi, i, 0, 0))],
        out_shape=[jax.ShapeDtypeStruct((b, tall, 512), BF16),
                   jax.ShapeDtypeStruct((b, tall, 512), BF16),
                   jax.ShapeDtypeStruct((b, tall, 512), BF16),
                   jax.ShapeDtypeStruct((b, tall, 256), BF16),
                   jax.ShapeDtypeStruct((b, nt, nv, TM), BF16)],
        compiler_params=_cparams(("parallel", "parallel")),
        name="proj_cd",
    )(x_all, mlat, mctx, w, wvt, cos, sin, bd, gq, gk)


def _fourier_kernel(nb, scale, uw_ref, ca_ref, sa_ref, cb_ref, sb_ref, o_ref):
    j = pl.program_id(0)
    ca = ca_ref[pl.ds(j, 1), :]
    sa = sa_ref[pl.ds(j, 1), :]
    cb = cb_ref[...]
    sb = sb_ref[...]
    ct = (ca * cb - sa * sb).astype(BF16)
    nst = (-(sa * cb + ca * sb)).astype(BF16)
    for bi in range(nb):
        acc = _dot(ct, uw_ref[bi, :, 0:256]) + _dot(nst, uw_ref[bi, :, 256:512])
        o_ref[bi] = (acc * scale).astype(BF16)


def _dft_tables(t, tmf):
    k = np.arange(t, dtype=np.int64)
    j1 = np.arange(t // tmf, dtype=np.int64) * tmf
    j0 = np.arange(tmf, dtype=np.int64)
    aa = (2.0 * np.pi / t) * ((j1[:, None] * k[None, :]) % t)
    ab = (2.0 * np.pi / t) * ((j0[:, None] * k[None, :]) % t)
    f = lambda a: jnp.asarray(a, F32)
    return f(np.cos(aa)), f(np.sin(aa)), f(np.cos(ab)), f(np.sin(ab))


def _fourier(uw):
    b, t, _ = uw.shape
    tmf = min(128, t)
    ca, sa, cb, sb = _dft_tables(t, tmf)
    scale = 1.0 / math.sqrt(t * HEAD_DIM)
    full2 = lambda j: (0, 0)
    return pl.pallas_call(
        functools.partial(_fourier_kernel, b, scale),
        grid=(t // tmf,),
        in_specs=[pl.BlockSpec((b, t, 512), lambda j: (0, 0, 0)),
                  pl.BlockSpec((t // tmf, t), full2),
                  pl.BlockSpec((t // tmf, t), full2),
                  pl.BlockSpec((tmf, t), full2),
                  pl.BlockSpec((tmf, t), full2)],
        out_specs=pl.BlockSpec((b, tmf, 256), lambda j: (0, j, 0)),
        out_shape=jax.ShapeDtypeStruct((b, t, 256), BF16),
        compiler_params=_cparams(("parallel",)),
        name="fourier_mix",
    )(uw, ca, sa, cb, sb)


def _win_attn_kernel(nct, nchunk, sink_ref, q_ref, k_ref, vt_ref, o_ref, s_a, s_b):
    i = pl.program_id(1)
    is_ctx = i < nct
    j = i - nct
    ctx_rows = nct * TM
    ctx_chunks = ctx_rows // LANES
    lw = TM + 2 * WINDOW
    lchunks = lw // LANES
    cs = jnp.clip(ctx_chunks + (TM // LANES) * j - WINDOW // LANES, 0, nchunk - lchunks)
    rs = pl.multiple_of(cs * LANES, LANES)
    qpos = j * TM + lax.broadcasted_iota(jnp.int32, (1, TM), 1)
    kpos = cs * LANES - ctx_rows + lax.broadcasted_iota(jnp.int32, (lw, 1), 0)
    allowed = (jnp.abs(qpos - kpos) <= WINDOW) & (kpos >= 0) & jnp.logical_not(is_ctx)
    lane = lax.broadcasted_iota(jnp.int32, (TM, LANES), 1)
    n_heads = q_ref.shape[-1] // HEAD_DIM
    group = n_heads // (k_ref.shape[-1] // LANES)

    def scores(hq, dst_s):
        pair, half = divmod(hq, 2)
        g = hq // group
        qp = q_ref[0, :, pair * LANES:(pair + 1) * LANES]
        qm = jnp.where((lane >= HEAD_DIM) == (half == 1), qp, jnp.zeros_like(qp))
        dst_s[0:ctx_rows, :] = _dot_nt(k_ref[0, 0:ctx_rows, g * LANES:(g + 1) * LANES], qm)
        s_l = _dot_nt(k_ref[0, pl.ds(rs, lw), g * LANES:(g + 1) * LANES], qm)
        dst_s[ctx_rows:ctx_rows + lw, :] = jnp.where(allowed, s_l, NEG)

    def consume(hq, src_s):
        g = hq // group
        s = src_s[...]
        sk = sink_ref[hq] * LOG2E
        m = jnp.maximum(jnp.max(s, axis=0, keepdims=True), sk)
        p = jnp.exp2(s - m)
        l = jnp.sum(p, axis=0, keepdims=True) + jnp.exp2(sk - m)
        p = p.astype(BF16)
        vrows = slice(g * HEAD_DIM, (g + 1) * HEAD_DIM)
        acc = jnp.zeros((HEAD_DIM, TM), F32)
        for c in range(ctx_chunks):
            acc = acc + _dot(vt_ref[0, c, vrows, :], p[c * LANES:(c + 1) * LANES, :])
        for c in range(lchunks):
            r0 = ctx_rows + c * LANES
            acc = acc + _dot(vt_ref[0, cs + c, vrows, :], p[r0:r0 + LANES, :])
        return acc / l

    bufs = (s_a, s_b)
    scores(0, s_a)
    outs = []
    for hq in range(n_heads):
        if hq + 1 < n_heads:
            scores(hq + 1, bufs[(hq + 1) % 2])
        outs.append(consume(hq, bufs[hq % 2]))
        if hq % 2 == 1:
            pair = hq // 2
            o2 = jnp.concatenate(outs, axis=0)
            o_ref[0, :, pair * LANES:(pair + 1) * LANES] = o2.T.astype(BF16)
            outs = []


def _win_attn(sink, q, k, vt, nct):
    b, tall, qw = q.shape
    nt = tall // TM
    nchunk = vt.shape[1]
    return pl.pallas_call(
        functools.partial(_win_attn_kernel, nct, nchunk),
        grid=(b, nt),
        in_specs=[pl.BlockSpec(memory_space=pltpu.SMEM),
                  pl.BlockSpec((1, TM, qw), lambda bi, i: (bi, i, 0)),
                  pl.BlockSpec((1, tall, k.shape[-1]), lambda bi, i: (bi, 0, 0)),
                  pl.BlockSpec((1, nchunk, vt.shape[2], LANES), lambda bi, i: (bi, 0, 0, 0))],
        out_specs=pl.BlockSpec((1, TM, qw), lambda bi, i: (bi, i, 0)),
        out_shape=jax.ShapeDtypeStruct((b, tall, qw), BF16),
        scratch_shapes=[pltpu.VMEM((nct * TM + TM + 2 * WINDOW, TM), F32)] * 2,
        compiler_params=_cparams(("parallel", "parallel")),
        name="window_attention",
    )(sink, q, k, vt)


def _unit_operands(u):
    if u < 2 * N_C_HEADS:
        hc = u // 2
        dst, dv = _VT_LAYOUT[hc]
        return True, hc, dst, dv
    g = (u - 2 * N_C_HEADS) // (N_D_HEADS // N_D_KV)
    dst, dv = _VT_LAYOUT[N_C_HEADS + g]
    return False, g, dst, dv


def _attn_cd_kernel(nkt, lam_init, lam_ref, sg_ref, qc_ref, qd_ref, kc_ref, kd_ref, vt_ref, o_ref,
                    qm_s, m_s, acc_s, s_a, s_b):
    mq = qc_ref.shape[1]
    tk = vt_ref.shape[-1]
    lane = lax.broadcasted_iota(jnp.int32, (mq, LANES), 1)
    upper = lane >= HEAD_DIM
    for u in range(N_UNITS):
        src = qc_ref if u < 2 * N_C_HEADS else qd_ref
        pair = (u if u < 2 * N_C_HEADS else u - 2 * N_C_HEADS) // 2
        qp = src[0, :, pair * LANES:(pair + 1) * LANES]
        qm_s[u] = jnp.where(upper == (u % 2 == 1), qp, jnp.zeros_like(qp))
    m_s[...] = jnp.full(m_s.shape, NEG, F32)
    acc_s[...] = jnp.zeros(acc_s.shape, F32)

    def scores(tile, u, dst_s):
        is_c, slab, _, _ = _unit_operands(u)
        k_ref = kc_ref if is_c else kd_ref
        r0 = pl.multiple_of(tile * tk, tk)
        dst_s[u] = _dot_nt(k_ref[0, pl.ds(r0, tk), slab * LANES:(slab + 1) * LANES], qm_s[u])

    def consume(tile, u, src_s):
        _, _, row0, dv = _unit_operands(u)
        rows = dv + ONES_ROWS
        s = src_s[u]
        m_old = m_s[u]
        m_new = jnp.maximum(m_old, jnp.max(s, axis=0, keepdims=True))
        alpha = jnp.exp2(m_old - m_new)
        p = jnp.exp2(s - m_new).astype(BF16)
        pv = _dot(vt_ref[0, tile, row0:row0 + rows, :], p)
        acc_s[u, 0:rows, :] = acc_s[u, 0:rows, :] * alpha + pv
        m_s[u] = m_new

    def stage(tile, src_s, dst_s):
        nxt = jnp.minimum(tile + 1, nkt - 1)
        for u in range(N_UNITS):
            scores(nxt, u, dst_s)
            consume(tile, u, src_s)

    for u in range(N_UNITS):
        scores(0, u, s_a)
    stage(0, s_a, s_b)

    def body(pr, carry):
        stage(2 * pr + 1, s_b, s_a)
        stage(2 * pr + 2, s_a, s_b)
        return carry

    lax.fori_loop(0, (nkt - 1) // 2, body, 0)
    if (nkt - 1) % 2 == 1:
        stage(nkt - 1, s_b, s_a)

    lv = lam_ref[...]
    lam = (jnp.exp(jnp.sum(lv[0:1] * lv[1:2], axis=-1, keepdims=True))
           - jnp.exp(jnp.sum(lv[2:3] * lv[3:4], axis=-1, keepdims=True)) + lam_init)
    sg = sg_ref[...]
    dvc = 2 * HEAD_DIM
    for hc in range(N_C_HEADS):
        o1 = acc_s[2 * hc, 0:dvc, :] / acc_s[2 * hc, dvc:dvc + 1, :]
        o2 = acc_s[2 * hc + 1, 0:dvc, :] / acc_s[2 * hc + 1, dvc:dvc + 1, :]
        o = o1 - lam * o2
        ms = jnp.mean(o * o, axis=0, keepdims=True)
        o = o * lax.rsqrt(ms + RMS_EPS) * sg * (1.0 - lam_init)
        o_ref[0, :, hc * LANES:(hc + 1) * LANES] = o.T.astype(BF16)
    for pair in range(N_D_HEADS // 2):
        outs = []
        for half in range(2):
            u = 2 * N_C_HEADS + 2 * pair + half
            outs.append(acc_s[u, 0:HEAD_DIM, :] / acc_s[u, HEAD_DIM:HEAD_DIM + 1, :])
        o2 = jnp.concatenate(outs, axis=0)
        c0 = N_C_HEADS * LANES + pair * LANES
        o_ref[0, :, c0:c0 + LANES] = o2.T.astype(BF16)


def _attn_cd(lam_vec, sg, qc, qd, kc, kd, vt, nct, lam_init):
    b, tall, _ = qc.shape
    nkt, nv, tk = vt.shape[1], vt.shape[2], vt.shape[3]
    mq = TM
    nq = tall // mq - nct
    ow = N_C_HEADS * LANES + N_D_HEADS * HEAD_DIM
    qtile = lambda bi, i: (bi, i + nct, 0)
    return pl.pallas_call(
        functools.partial(_attn_cd_kernel, nkt, lam_init),
        grid=(b, nq),
        in_specs=[pl.BlockSpec((4, HEAD_DIM), lambda bi, i: (0, 0)),
                  pl.BlockSpec((LANES, 1), lambda bi, i: (0, 0)),
                  pl.BlockSpec((1, mq, qc.shape[-1]), qtile),
                  pl.BlockSpec((1, mq, qd.shape[-1]), qtile),
                  pl.BlockSpec((1, tall, kc.shape[-1]), lambda bi, i: (bi, 0, 0)),
                  pl.BlockSpec((1, tall, kd.shape[-1]), lambda bi, i: (bi, 0, 0)),
                  pl.BlockSpec((1, nkt, nv, tk), lambda bi, i: (bi, 0, 0, 0))],
        out_specs=pl.BlockSpec((1, mq, ow), lambda bi, i: (bi, i, 0)),
        out_shape=jax.ShapeDtypeStruct((b, nq * mq, ow), BF16),
        scratch_shapes=[pltpu.VMEM((N_UNITS, mq, LANES), BF16),
                        pltpu.VMEM((N_UNITS, 1, mq), F32),
                        pltpu.VMEM((N_UNITS, 2 * HEAD_DIM + ONES_ROWS, mq), F32),
                        pltpu.VMEM((N_UNITS, tk, mq), F32),
                        pltpu.VMEM((N_UNITS, tk, mq), F32)],
        compiler_params=_cparams(("parallel", "parallel")),
        name="attention_cd",
    )(lam_vec, sg, qc, qd, kc, kd, vt)


def _oproj_kernel(n_parts, widths, nct, alpha, n_exp, *refs):
    o_refs = refs[:n_parts]
    (w_ref, x_ref, mlat_ref, mctx_ref, lng_ref, lnb_ref, rwt_ref, rb_ref,
     x1_ref, h2_ref, te_ref, tg_ref) = refs[n_parts:]
    d = x_ref.shape[-1]
    is_ctx = pl.program_id(1) < nct
    y = None
    r0 = 0
    for o_ref, wd in zip(o_refs, widths):
        part = _dot(o_ref[0], w_ref[r0:r0 + wd, :])
        y = part if y is None else y + part
        r0 += wd
    m = jnp.where(is_ctx, mctx_ref[...], mlat_ref[0])
    x1 = _layernorm(alpha * x_ref[0] + m[:, 2 * d:3 * d] * y, lng_ref[...], lnb_ref[...])
    x1_ref[0] = x1
    h2 = x1 * (1.0 + m[:, 4 * d:5 * d]) + m[:, 3 * d:4 * d]
    h2_ref[0] = h2
    logits = _dot_nt(rwt_ref[...], h2.astype(BF16)) + rb_ref[...]
    tm = logits.shape[1]
    row = lax.broadcasted_iota(jnp.int32, logits.shape, 0)
    vals, idxs = [], []
    for _ in range(TOP_K):
        mx = jnp.max(logits, axis=0, keepdims=True)
        idx = jnp.min(jnp.where(logits == mx, row, n_exp), axis=0, keepdims=True)
        vals.append(mx)
        idxs.append(idx)
        logits = jnp.where(row == idx, NEG, logits)
    es = [jnp.exp(v - vals[0]) for v in vals]
    tot = es[0] + es[1] + es[2] + es[3]
    pad_i = jnp.zeros((8 - TOP_K, tm), jnp.int32)
    pad_f = jnp.zeros((8 - TOP_K, tm), F32)
    te_ref[0] = jnp.concatenate(idxs + [pad_i], axis=0)
    tg_ref[0] = jnp.concatenate([e / tot for e in es] + [pad_f], axis=0)


def _oproj(parts, w_out, x_all, mlat, mctx, lng, lnb, rwt, rb, nct_out, x_tile_off, alpha):
    b, t, _ = parts[0].shape
    d = x_all.shape[-1]
    nt = t // TM
    n_exp = rwt.shape[0]
    widths = tuple(p.shape[-1] for p in parts)
    tile = lambda bi, i: (bi, i, 0)
    full = lambda bi, i: (0, 0)
    in_specs = [pl.BlockSpec((1, TM, wd), tile) for wd in widths] + [
        pl.BlockSpec(w_out.shape, full),
        pl.BlockSpec((1, TM, d), lambda bi, i: (bi, i + x_tile_off, 0)),
        pl.BlockSpec((1, 1, N_MOD * d), lambda bi, i: (bi, 0, 0)),
        pl.BlockSpec((1, N_MOD * d), full),
        pl.BlockSpec((1, d), full),
        pl.BlockSpec((1, d), full),
        pl.BlockSpec((n_exp, d), full),
        pl.BlockSpec((n_exp, 1), full)]
    return pl.pallas_call(
        functools.partial(_oproj_kernel, len(parts), widths, nct_out, alpha, n_exp),
        grid=(b, nt),
        in_specs=in_specs,
        out_specs=[pl.BlockSpec((1, TM, d), tile),
                   pl.BlockSpec((1, TM, d), tile),
                   pl.BlockSpec((1, 8, TM), lambda bi, i: (bi, 0, i)),
                   pl.BlockSpec((1, 8, TM), lambda bi, i: (bi, 0, i))],
        out_shape=[jax.ShapeDtypeStruct((b, t, d), F32),
                   jax.ShapeDtypeStruct((b, t, d), F32),
                   jax.ShapeDtypeStruct((b, 8, t), jnp.int32),
                   jax.ShapeDtypeStruct((b, 8, t), F32)],
        compiler_params=_cparams(("parallel", "parallel")),
        name="out_proj_norm_router",
    )(*parts, w_out, x_all, mlat, mctx, lng, lnb, rwt, rb)


def _moe_kernel(be_ref, nx_ref, nu_ref, x_ref, wgu_hbm, bgu_ref, wd_hbm, bd_ref, y_ref,
                wgu_f, wd_f, wgu_s, wd_s, sem):
    blk = pl.program_id(0)
    ff = wd_s.shape[0]

    def fetch(e):
        return (pltpu.make_async_copy(wgu_hbm.at[e], wgu_f, sem.at[0]),
                pltpu.make_async_copy(wd_hbm.at[e], wd_f, sem.at[1]))

    @pl.when(blk < nu_ref[0])
    def _():
        e = be_ref[blk]
        new_expert = (blk == 0) | (e != be_ref[jnp.maximum(blk - 1, 0)])

        @pl.when(blk == 0)
        def _():
            for cp in fetch(e):
                cp.start()

        @pl.when(new_expert)
        def _():
            for cp in fetch(e):
                cp.wait()
            rows = 128

            def cast_gu(r, c):
                r0 = pl.multiple_of(r * rows, rows)
                wgu_s[pl.ds(r0, rows), :] = wgu_f[pl.ds(r0, rows), :].astype(BF16)
                return c

            def cast_d(r, c):
                r0 = pl.multiple_of(r * rows, rows)
                wd_s[pl.ds(r0, rows), :] = wd_f[pl.ds(r0, rows), :].astype(BF16)
                return c

            lax.fori_loop(0, wgu_s.shape[0] // rows, cast_gu, 0)
            lax.fori_loop(0, wd_s.shape[0] // rows, cast_d, 0)

            @pl.when(nx_ref[blk] != e)
            def _():
                for cp in fetch(nx_ref[blk]):
                    cp.start()

        gu = _dot(x_ref[...].astype(BF16), wgu_s[...]) + bgu_ref[0]
        g = jnp.minimum(gu[:, :ff], SWIGLU_LIMIT)
        u = jnp.clip(gu[:, ff:], -SWIGLU_LIMIT, SWIGLU_LIMIT)
        act = (u + 1.0) * (g / (1.0 + jnp.exp(-SWIGLU_ALPHA * g)))
        y_ref[...] = _dot(act.astype(BF16), wd_s[...]) + bd_ref[0]

    @pl.when(blk >= nu_ref[0])
    def _():
        y_ref[...] = jnp.zeros(y_ref.shape, F32)


def _moe_experts(block_e, block_next_e, n_used, xb, w_gu, b_gu, w_down, b_down):
    n_slots, d = xb.shape
    ff2 = w_gu.shape[-1]
    ff = w_down.shape[-2]
    n_exp = w_gu.shape[0] * w_gu.shape[1]
    w_gu = w_gu.reshape(n_exp, d, ff2)
    w_down = w_down.reshape(n_exp, ff, d)
    nb = n_slots // MOE_BM
    xblk = lambda i, be, nx, nu: (jnp.minimum(i, nu[0] - 1), 0)
    bblk = lambda i, be, nx, nu: (be[i], 0, 0)
    grid_spec = pltpu.PrefetchScalarGridSpec(
        num_scalar_prefetch=3,
        grid=(nb,),
        in_specs=[pl.BlockSpec((MOE_BM, d), xblk),
                  pl.BlockSpec(memory_space=pl.ANY),
                  pl.BlockSpec((1, 1, ff2), bblk),
                  pl.BlockSpec(memory_space=pl.ANY),
                  pl.BlockSpec((1, 1, d), bblk)],
        out_specs=pl.BlockSpec((MOE_BM, d), lambda i, be, nx, nu: (i, 0)),
        scratch_shapes=[pltpu.VMEM((d, ff2), F32), pltpu.VMEM((ff, d), F32),
                        pltpu.VMEM((d, ff2), BF16), pltpu.VMEM((ff, d), BF16),
                        pltpu.SemaphoreType.DMA((2,))])
    return pl.pallas_call(
        _moe_kernel,
        grid_spec=grid_spec,
        out_shape=jax.ShapeDtypeStruct((n_slots, d), F32),
        compiler_params=_cparams(("arbitrary",)),
        name="moe_experts",
    )(block_e, block_next_e, n_used, xb, w_gu, b_gu.reshape(n_exp, 1, ff2), w_down, b_down.reshape(n_exp, 1, d))


def _expert_onehot(te, n_exp):
    row = lax.broadcasted_iota(jnp.int32, (n_exp, te.shape[1]), 0)
    oh = jnp.zeros(row.shape, F32)
    for k in range(TOP_K):
        oh = oh + (row == te[k:k + 1, :]).astype(F32)
    return row, oh


def _route_count_kernel(n_exp, te_ref, cnt_ref):
    first = (pl.program_id(0) == 0) & (pl.program_id(1) == 0)

    @pl.when(first)
    def _():
        cnt_ref[...] = jnp.zeros(cnt_ref.shape, F32)

    _, oh = _expert_onehot(te_ref[0], n_exp)
    cnt_ref[...] += jnp.sum(oh, axis=1, keepdims=True)


def _route_slot_kernel(n_exp, te_ref, pstart_ref, tri_ref, slot_ref, carry_s):
    first = (pl.program_id(0) == 0) & (pl.program_id(1) == 0)

    @pl.when(first)
    def _():
        carry_s[...] = pstart_ref[...]

    te = te_ref[0]
    row, oh = _expert_onehot(te, n_exp)
    before = _dot(oh.astype(BF16), tri_ref[...]) + carry_s[...]
    slots = [jnp.sum(jnp.where(row == te[k:k + 1, :], before, 0.0), axis=0, keepdims=True)
             for k in range(TOP_K)]
    pad = jnp.zeros((8 - TOP_K, te.shape[1]), F32)
    slot_ref[0] = jnp.concatenate(slots + [pad], axis=0).astype(jnp.int32)
    carry_s[...] += jnp.sum(oh, axis=1, keepdims=True)


def _route(te, n_exp):
    b, _, t = te.shape
    tile = pl.BlockSpec((1, 8, TM), lambda bi, i: (bi, 0, i))
    vec = pl.BlockSpec((n_exp, 1), lambda bi, i: (0, 0))
    cnt = pl.pallas_call(
        functools.partial(_route_count_kernel, n_exp),
        grid=(b, t // TM),
        in_specs=[tile],
        out_specs=vec,
        out_shape=jax.ShapeDtypeStruct((n_exp, 1), F32),
        compiler_params=_cparams(("arbitrary", "arbitrary")),
        name="route_count",
    )(te)
    counts = cnt[:, 0].astype(jnp.int32)
    padded = (counts + MOE_BM - 1) // MOE_BM * MOE_BM
    pend = jnp.cumsum(padded)
    pstart = (pend - padded).astype(F32).reshape(n_exp, 1)
    tri = jnp.asarray(np.triu(np.ones((TM, TM)), 1), BF16)
    slot = pl.pallas_call(
        functools.partial(_route_slot_kernel, n_exp),
        grid=(b, t // TM),
        in_specs=[tile, vec, pl.BlockSpec((TM, TM), lambda bi, i: (0, 0))],
        out_specs=tile,
        out_shape=jax.ShapeDtypeStruct((b, 8, t), jnp.int32),
        scratch_shapes=[pltpu.VMEM((n_exp, 1), F32)],
        compiler_params=_cparams(("arbitrary", "arbitrary")),
        name="route_slot",
    )(te, pstart, tri)
    n_blocks = -(-(b * t * TOP_K) // MOE_BM) + n_exp
    n_used = (pend[-1] // MOE_BM).astype(jnp.int32)
    blk = jnp.minimum(jnp.arange(n_blocks, dtype=jnp.int32), n_used - 1) * MOE_BM
    block_e = jnp.minimum(jnp.sum((pend[None, :] <= blk[:, None]).astype(jnp.int32), axis=1), n_exp - 1)
    ids = jnp.arange(n_exp, dtype=jnp.int32)
    later = (padded[None, :] > 0) & (ids[None, :] > block_e[:, None])
    block_next_e = jnp.min(jnp.where(later, ids[None, :], n_exp), axis=1)
    block_next_e = jnp.where(block_next_e == n_exp, block_e, block_next_e).astype(jnp.int32)
    slot_flat = slot[:, :TOP_K].reshape(b, TOP_K, t // TM, TM).transpose(0, 2, 1, 3).reshape(-1)
    return slot_flat, pend.astype(jnp.int32), block_e, block_next_e, n_used.reshape(1), n_blocks


def _row_copy(src_ref, src_row, dst_ref, dst_row, sem):
    return pltpu.make_async_copy(src_ref.at[pl.ds(src_row, 1)], dst_ref.at[pl.ds(dst_row, 1)], sem)


def _moe_scatter_kernel(n_exp, pend_ref, slot_ref, h_ref, xb_ref, zero_s, sem):
    first = (pl.program_id(0) == 0) & (pl.program_id(1) == 0)
    tm = h_ref.shape[1]

    @pl.when(first)
    def _():
        zero_s[...] = jnp.zeros(zero_s.shape, zero_s.dtype)
        for phase in ("start", "wait"):
            for e in range(n_exp):
                lo = pend_ref[e - 1] if e else 0

                @pl.when(pend_ref[e] > lo)
                def _():
                    r0 = pl.multiple_of(pend_ref[e] - MOE_BM, MOE_BM)
                    cp = pltpu.make_async_copy(zero_s, xb_ref.at[pl.ds(r0, MOE_BM)], sem)
                    cp.start() if phase == "start" else cp.wait()

            def unused_block(blk, c):
                r0 = pl.multiple_of(blk * MOE_BM, MOE_BM)
                cp = pltpu.make_async_copy(zero_s, xb_ref.at[pl.ds(r0, MOE_BM)], sem)
                cp.start() if phase == "start" else cp.wait()
                return c

            lax.fori_loop(pend_ref[n_exp - 1] // MOE_BM, xb_ref.shape[0] // MOE_BM, unused_block, 0)

    def issue(j, c):
        for k in range(TOP_K):
            _row_copy(h_ref.at[0], j, xb_ref, slot_ref[k * tm + j], sem).start(priority=k % 2)
        return c

    def drain(j, c):
        for k in range(TOP_K):
            _row_copy(h_ref.at[0], j, xb_ref, slot_ref[k * tm + j], sem).wait()
        return c

    lax.fori_loop(0, tm, issue, 0)
    lax.fori_loop(0, tm, drain, 0)


def _moe_scatter(pend, slot_flat, h2, n_slots):
    b, t, d = h2.shape
    n_exp = pend.shape[0]
    grid_spec = pltpu.PrefetchScalarGridSpec(
        num_scalar_prefetch=1,
        grid=(b, t // TM),
        in_specs=[pl.BlockSpec((TOP_K * TM,), lambda bi, i, pe: (bi * (t // TM) + i,), memory_space=pltpu.SMEM),
                  pl.BlockSpec((1, TM, d), lambda bi, i, pe: (bi, i, 0))],
        out_specs=pl.BlockSpec(memory_space=pl.ANY),
        scratch_shapes=[pltpu.VMEM((MOE_BM, d), F32), pltpu.SemaphoreType.DMA])
    return pl.pallas_call(
        functools.partial(_moe_scatter_kernel, n_exp),
        grid_spec=grid_spec,
        out_shape=jax.ShapeDtypeStruct((n_slots, d), F32),
        compiler_params=_cparams(("arbitrary", "arbitrary")),
        name="moe_scatter_rows",
    )(pend, slot_flat, h2)


def _combine_kernel(nct, alpha, slot_ref, slot_next_ref, yb_ref, tg_ref, x1_ref, mlat_ref, mctx_ref,
                    lng_ref, lnb_ref, o_ref, buf_s, sem):
    d = x1_ref.shape[-1]
    tm = x1_ref.shape[1]
    step = pl.program_id(0) * pl.num_programs(1) + pl.program_id(1)
    n_steps = pl.num_programs(0) * pl.num_programs(1)
    par = step % 2

    def rows(slots, buf, wait):
        def body(j, c):
            for k in range(TOP_K):
                cp = _row_copy(yb_ref, slots[k * tm + j], buf_s.at[buf, k], j, sem.at[buf])
                cp.wait() if wait else cp.start(priority=k % 2)
            return c
        lax.fori_loop(0, tm, body, 0)

    @pl.when(step == 0)
    def _():
        rows(slot_ref, 0, False)

    @pl.when(step + 1 < n_steps)
    def _():
        rows(slot_next_ref, 1 - par, False)

    rows(slot_ref, par, True)
    is_ctx = pl.program_id(1) < nct
    m = jnp.where(is_ctx, mctx_ref[...], mlat_ref[0])
    gates = tg_ref[0]
    f = gates[:, 0:1] * buf_s[par, 0]
    for k in range(1, TOP_K):
        f = f + gates[:, k:k + 1] * buf_s[par, k]
    o_ref[0] = _layernorm(alpha * x1_ref[0] + m[:, 5 * d:6 * d] * f, lng_ref[...], lnb_ref[...])


def _combine_postnorm(slot_flat, yb, tg, x1, mlat, mctx, lng, lnb, nct, alpha):
    b, t, d = x1.shape
    tile = lambda bi, i: (bi, i, 0)
    full = lambda bi, i: (0, 0)
    nt = t // TM
    last = b * nt - 1
    return pl.pallas_call(
        functools.partial(_combine_kernel, nct, alpha),
        grid=(b, nt),
        in_specs=[pl.BlockSpec((TOP_K * TM,), lambda bi, i: (bi * nt + i,), memory_space=pltpu.SMEM),
                  pl.BlockSpec((TOP_K * TM,), lambda bi, i: (jnp.minimum(bi * nt + i + 1, last),),
                               memory_space=pltpu.SMEM),
                  pl.BlockSpec(memory_space=pl.ANY),
                  pl.BlockSpec((1, TM, 8), tile),
                  pl.BlockSpec((1, TM, d), tile),
                  pl.BlockSpec((1, 1, N_MOD * d), lambda bi, i: (bi, 0, 0)),
                  pl.BlockSpec((1, N_MOD * d), full),
                  pl.BlockSpec((1, d), full),
                  pl.BlockSpec((1, d), full)],
        out_specs=pl.BlockSpec((1, TM, d), tile),
        out_shape=jax.ShapeDtypeStruct((b, t, d), F32),
        scratch_shapes=[pltpu.VMEM((2, TOP_K, TM, d), F32), pltpu.SemaphoreType.DMA((2,))],
        compiler_params=_cparams(("arbitrary", "arbitrary")),
        name="combine_post_norm",
    )(slot_flat, slot_flat, yb, tg, x1, mlat, mctx, lng, lnb)


def _rope_tables(ctx_len, seq):
    t = np.arange(seq)
    row = (t // GRID_W).astype(np.float64)
    col = (t % GRID_W).astype(np.float64)
    nf = HEAD_DIM // 4
    inv = ROPE_THETA ** (-np.arange(nf, dtype=np.float64) / nf)
    ar = row[:, None] * inv[None, :]
    ac = col[:, None] * inv[None, :]
    ang = np.concatenate([ar, ar, ac, ac], axis=-1)
    cos = np.concatenate([np.ones((ctx_len, HEAD_DIM)), np.cos(ang)], axis=0)
    sin = np.concatenate([np.zeros((ctx_len, HEAD_DIM)), np.sin(ang)], axis=0)
    sign = np.where((np.arange(HEAD_DIM) % 32) < 16, -1.0, 1.0)[None, :]
    cos2 = np.tile(cos, (1, LANES // HEAD_DIM))
    sin2 = np.tile(sin * sign, (1, LANES // HEAD_DIM))
    return jnp.asarray(cos2, F32), jnp.asarray(sin2, F32)


def _channel_dft():
    c = np.arange(HEAD_DIM)
    ang = 2.0 * np.pi * ((c[:, None] * c[None, :]) % HEAD_DIM) / HEAD_DIM
    eye = np.eye(4)
    cs = np.concatenate([np.kron(eye, np.cos(ang)), np.kron(eye, np.sin(ang))], axis=1)
    return jnp.asarray(cs, BF16)


def _group_mean_matrix():
    bd = np.kron(np.eye(LANES // HEAD_DIM), np.full((HEAD_DIM, HEAD_DIM), 1.0 / HEAD_DIM))
    return jnp.asarray(bd, BF16)


def _dup_heads(wk, n_heads):
    d = wk.shape[0]
    return jnp.broadcast_to(wk.reshape(d, n_heads, 1, HEAD_DIM), (d, n_heads, 2, HEAD_DIM)).reshape(d, n_heads * LANES)


def _lambda_init(layer):
    return 0.8 - 0.6 * math.exp(-0.3 * layer)


def _moe_postnorm(h2, te, tg, x1, mlat, mctx, lng, lnb, nct, alpha, layer, w_gu, b_gu, w_down, b_down):
    n_exp = w_gu.shape[1]
    slot_flat, pend, block_e, block_next_e, n_used, n_blocks = _route(te, n_exp)
    xb = _moe_scatter(pend, slot_flat, h2, n_blocks * MOE_BM)
    off = layer * n_exp
    yb = _moe_experts(block_e + off, block_next_e + off, n_used, xb, w_gu, b_gu, w_down, b_down)
    return _combine_postnorm(slot_flat, yb, jnp.transpose(tg, (0, 2, 1)), x1, mlat, mctx, lng, lnb, nct, alpha)


def kernel(x, c, ctx, c_ctx, mod_w, mod_b, ln_g, ln_b, ab_w_in, ab_sink, ab_w_out,
           cd_w_in, cd_lambda, cd_subln_g, cd_q_norm_g, cd_k_norm_g, cd_w_out,
           router_w, router_b, expert_w_gu, expert_b_gu, expert_w_down, expert_b_down):
    b, s, d = x.shape
    n_ctx = ctx.shape[1]
    depth = mod_w.shape[0]
    n_exp = router_w.shape[-1]
    assert d == 16 * HEAD_DIM and n_ctx % TM == 0 and s % TM == 0 and s % GRID_W == 0
    nct = n_ctx // TM
    alpha = (2 * depth) ** 0.25

    cos, sin = _rope_tables(n_ctx, s)
    cs_dft = _channel_dft()
    bd = _group_mean_matrix()
    mods = _mod_vectors(c, c_ctx, mod_w, mod_b)

    x_all = jnp.concatenate([ctx, x], axis=1)
    for l in range(depth):
        last = l == depth - 1
        i = l // 2
        mlat = mods[l, :b].reshape(b, 1, N_MOD * d)
        mctx = mods[l, b:b + 1]
        lng1, lnb1 = ln_g[l, 0].reshape(1, d), ln_b[l, 0].reshape(1, d)
        lng2, lnb2 = ln_g[l, 1].reshape(1, d), ln_b[l, 1].reshape(1, d)
        rwt = router_w[l].T.astype(BF16)
        rb = router_b[l].reshape(n_exp, 1)
        if l % 2 == 0:
            w = ab_w_in[i]
            w_n = jnp.concatenate([w[:, :1024], _dup_heads(w[:, 1024:1280], 4)], axis=1).astype(BF16)
            wvt = w[:, 1280:1536].T.astype(BF16)
            uw, q, k, vt = _proj_ab(x_all, mlat, mctx, w_n, wvt, cos, sin, cs_dft, nct)
            oa = jnp.concatenate([_fourier(uw[:, :n_ctx]), _fourier(uw[:, n_ctx:])], axis=1)
            ob = _win_attn(ab_sink[i], q, k, vt, nct)
            if last:
                parts = [oa[:, n_ctx:], ob[:, n_ctx:]]
            else:
                parts = [oa, ob]
            w_out = ab_w_out[i].astype(BF16)
        else:
            w = cd_w_in[i]
            w_n = jnp.concatenate([w[:, :1536], _dup_heads(w[:, 1536:1664], 2)], axis=1).astype(BF16)
            wvt = w[:, 1664:2304].T.astype(BF16)
            gq = jnp.tile(cd_q_norm_g[i], 2).reshape(1, LANES)
            gk = jnp.tile(cd_k_norm_g[i], 2).reshape(1, LANES)
            qc, qd, kc, kd, vt = _proj_cd(x_all, mlat, mctx, w_n, wvt, cos, sin, bd, gq, gk, nct)
            sg = cd_subln_g[i].reshape(LANES, 1)
            o_lat = _attn_cd(cd_lambda[i], sg, qc, qd, kc, kd, vt, nct, _lambda_init(l))
            if last:
                parts = [o_lat]
            else:
                raise NotImplementedError("context outputs of a differential/axial layer")
            w_out = cd_w_out[i].astype(BF16)
        experts = (l, expert_w_gu, expert_b_gu, expert_w_down, expert_b_down)
        if last:
            x1, h2, te, tg = _oproj(parts, w_out, x_all, mlat, mctx, lng1, lnb1, rwt, rb, 0, nct, alpha)
            return _moe_postnorm(h2, te, tg, x1, mlat, mctx, lng2, lnb2, 0, alpha, *experts)
        x1, h2, te, tg = _oproj(parts, w_out, x_all, mlat, mctx, lng1, lnb1, rwt, rb, nct, 0, alpha)
        x_all = _moe_postnorm(h2, te, tg, x1, mlat, mctx, lng2, lnb2, nct, alpha, *experts)
    return x_all[:, n_ctx:]
```

```python
import functools
import math

import jax
import jax.numpy as jnp
import numpy as np
from jax import lax
from jax.experimental import pallas as pl
from jax.experimental.pallas import tpu as pltpu

F32 = jnp.float32
BF16 = jnp.bfloat16

HEAD_DIM = 64
GRID_W = 64
WINDOW = 128
ROPE_THETA = 10000.0
LN_EPS = 1e-6
RMS_EPS = 1e-6
N_MOD = 6
TOP_K = 4
SWIGLU_LIMIT = 7.0
SWIGLU_ALPHA = 1.702
NEG = -1e30
LOG2E = 1.4426950408889634
QSCALE = HEAD_DIM ** -0.5 * LOG2E

N_C_HEADS = 4
N_D_HEADS = 8
N_D_KV = 2
N_UNITS = 2 * N_C_HEADS + N_D_HEADS
ONES_ROWS = 16
_VT_LAYOUT = tuple((h * (2 * HEAD_DIM + ONES_ROWS), 2 * HEAD_DIM) for h in range(N_C_HEADS)) + tuple(
    (N_C_HEADS * (2 * HEAD_DIM + ONES_ROWS) + g * (HEAD_DIM + ONES_ROWS), HEAD_DIM) for g in range(N_D_KV))
VT_ROWS = _VT_LAYOUT[-1][0] + HEAD_DIM + ONES_ROWS

LANES = 128
TM = 256
MOE_BM = 256
ROW_GROUP = 8
SMEM_BLOCK = 1024
VMEM_LIMIT = 56 * 1024 * 1024


def _dot(a, b):
    return jnp.dot(a, b, preferred_element_type=F32)


def _dot_nt(a, b):
    return lax.dot_general(a, b, (((1,), (1,)), ((), ())), preferred_element_type=F32)


def _cparams(sem):
    return pltpu.CompilerParams(dimension_semantics=sem, vmem_limit_bytes=VMEM_LIMIT)


def _layernorm(z, g, b):
    mu = jnp.mean(z, axis=-1, keepdims=True)
    d = z - mu
    var = jnp.mean(d * d, axis=-1, keepdims=True)
    return d * lax.rsqrt(var + LN_EPS) * g + b


def _rope128(x, cos, sin_signed):
    lane = lax.broadcasted_iota(jnp.int32, x.shape, 1)
    lo = (lane % 32) < 16
    rot = jnp.where(lo, pltpu.roll(x, LANES - 16, 1), pltpu.roll(x, 16, 1))
    return x * cos + rot * sin_signed


def _rmsnorm128(x, g, bd):
    x2 = x * x
    hi = x2.astype(BF16)
    lo = (x2 - hi.astype(F32)).astype(BF16)
    ms = _dot(hi, bd) + _dot(lo, bd)
    return x * lax.rsqrt(ms + RMS_EPS) * g


def _modulated(x_ref, mlat_ref, mctx_ref, is_ctx, d):
    m = jnp.where(is_ctx, mctx_ref[...], mlat_ref[0])
    return (x_ref[0] * (1.0 + m[:, d:2 * d]) + m[:, 0:d]).astype(BF16)


def _mod_kernel(c_ref, w_ref, b_ref, o_ref):
    c = c_ref[...]
    s = (c / (1.0 + jnp.exp(-c))).astype(BF16)
    o_ref[0] = _dot(s, w_ref[0].astype(BF16)) + b_ref[0]


def _mod_vectors(c, c_ctx, mod_w, mod_b):
    depth, d, n = mod_w.shape
    b = c.shape[0]
    rows = 8 * (-(-(b + 1) // 8))
    cs = jnp.zeros((rows, d), F32).at[:b].set(c).at[b].set(c_ctx)
    tn = 1536
    out = pl.pallas_call(
        _mod_kernel,
        grid=(depth, n // tn),
        in_specs=[pl.BlockSpec((rows, d), lambda l, j: (0, 0)),
                  pl.BlockSpec((1, d, tn), lambda l, j: (l, 0, j)),
                  pl.BlockSpec((1, 1, tn), lambda l, j: (l, 0, j))],
        out_specs=pl.BlockSpec((1, rows, tn), lambda l, j: (l, 0, j)),
        out_shape=jax.ShapeDtypeStruct((depth, rows, n), F32),
        compiler_params=_cparams(("parallel", "parallel")),
        name="mod_vectors",
    )(cs, mod_w, mod_b.reshape(depth, 1, n))
    return out


def _proj_ab_kernel(nct, x_ref, mlat_ref, mctx_ref, w_ref, wvt_ref, cos_ref, sin_ref, cs_ref,
                    uw_ref, q_ref, k_ref, vt_ref):
    is_ctx = pl.program_id(1) < nct
    d = x_ref.shape[-1]
    h = _modulated(x_ref, mlat_ref, mctx_ref, is_ctx, d)
    p = _dot(h, w_ref[...])
    uw_ref[0] = _dot(p[:, 0:256].astype(BF16), cs_ref[...]).astype(BF16)
    cos = cos_ref[...]
    sin = sin_ref[...]
    for j in range(6):
        c0 = 256 + j * LANES
        q_ref[0, :, j * LANES:(j + 1) * LANES] = (_rope128(p[:, c0:c0 + LANES], cos, sin) * QSCALE).astype(BF16)
    for j in range(4):
        c0 = 1024 + j * LANES
        k_ref[0, :, j * LANES:(j + 1) * LANES] = _rope128(p[:, c0:c0 + LANES], cos, sin).astype(BF16)
    vt = _dot_nt(wvt_ref[...], h).astype(BF16)
    for c in range(TM // LANES):
        vt_ref[0, c] = vt[:, c * LANES:(c + 1) * LANES]


def _proj_ab(x_all, mlat, mctx, w, wvt, cos, sin, cs, nct):
    b, tall, d = x_all.shape
    nt = tall // TM
    nw = w.shape[1]
    nchunk = tall // LANES
    cpt = TM // LANES
    return pl.pallas_call(
        functools.partial(_proj_ab_kernel, nct),
        grid=(b, nt),
        in_specs=[pl.BlockSpec((1, TM, d), lambda bi, i: (bi, i, 0)),
                  pl.BlockSpec((1, 1, N_MOD * d), lambda bi, i: (bi, 0, 0)),
                  pl.BlockSpec((1, N_MOD * d), lambda bi, i: (0, 0)),
                  pl.BlockSpec((d, nw), lambda bi, i: (0, 0)),
                  pl.BlockSpec((256, d), lambda bi, i: (0, 0)),
                  pl.BlockSpec((TM, LANES), lambda bi, i: (i, 0)),
                  pl.BlockSpec((TM, LANES), lambda bi, i: (i, 0)),
                  pl.BlockSpec((256, 512), lambda bi, i: (0, 0))],
        out_specs=[pl.BlockSpec((1, TM, 512), lambda bi, i: (bi, i, 0)),
                   pl.BlockSpec((1, TM, 768), lambda bi, i: (bi, i, 0)),
                   pl.BlockSpec((1, TM, 512), lambda bi, i: (bi, i, 0)),
                   pl.BlockSpec((1, cpt, 256, LANES), lambda bi, i: (bi, i, 0, 0))],
        out_shape=[jax.ShapeDtypeStruct((b, tall, 512), BF16),
                   jax.ShapeDtypeStruct((b, tall, 768), BF16),
                   jax.ShapeDtypeStruct((b, tall, 512), BF16),
                   jax.ShapeDtypeStruct((b, nchunk, 256, LANES), BF16)],
        compiler_params=_cparams(("parallel", "parallel")),
        name="proj_ab",
    )(x_all, mlat, mctx, w, wvt, cos, sin, cs)


def _proj_cd_kernel(nct, x_ref, mlat_ref, mctx_ref, w_ref, wvt_ref, cos_ref, sin_ref, bd_ref,
                    gq_ref, gk_ref, qc_ref, qd_ref, kc_ref, kd_ref, vt_ref):
    is_ctx = pl.program_id(1) < nct
    d = x_ref.shape[-1]
    h = _modulated(x_ref, mlat_ref, mctx_ref, is_ctx, d)
    p = _dot(h, w_ref[...])
    cos = cos_ref[...]
    sin = sin_ref[...]
    bd = bd_ref[...]
    gq = gq_ref[...]
    gk = gk_ref[...]
    for j in range(4):
        sl = slice(j * LANES, (j + 1) * LANES)
        qc_ref[0, :, sl] = (_rope128(p[:, j * LANES:(j + 1) * LANES], cos, sin) * QSCALE).astype(BF16)
        c0 = 512 + j * LANES
        qd_ref[0, :, sl] = (_rope128(_rmsnorm128(p[:, c0:c0 + LANES], gq, bd), cos, sin) * QSCALE).astype(BF16)
        c0 = 1024 + j * LANES
        kc_ref[0, :, sl] = _rope128(p[:, c0:c0 + LANES], cos, sin).astype(BF16)
    for j in range(2):
        c0 = 1536 + j * LANES
        kd_ref[0, :, j * LANES:(j + 1) * LANES] = _rope128(
            _rmsnorm128(p[:, c0:c0 + LANES], gk, bd), cos, sin).astype(BF16)
    vt = _dot_nt(wvt_ref[...], h).astype(BF16)
    ones = jnp.ones((ONES_ROWS, vt.shape[1]), BF16)
    src = 0
    for dst, dv in _VT_LAYOUT:
        vt_ref[0, 0, dst:dst + dv, :] = vt[src:src + dv, :]
        vt_ref[0, 0, dst + dv:dst + dv + ONES_ROWS, :] = ones
        src += dv


def _proj_cd(x_all, mlat, mctx, w, wvt, cos, sin, bd, gq, gk, nct):
    b, tall, d = x_all.shape
    nt = tall // TM
    nw = w.shape[1]
    nv = VT_ROWS
    full = lambda bi, i: (0, 0)
    tile = lambda bi, i: (bi, i, 0)
    return pl.pallas_call(
        functools.partial(_proj_cd_kernel, nct),
        grid=(b, nt),
        in_specs=[pl.BlockSpec((1, TM, d), tile),
                  pl.BlockSpec((1, 1, N_MOD * d), lambda bi, i: (bi, 0, 0)),
                  pl.BlockSpec((1, N_MOD * d), full),
                  pl.BlockSpec((d, nw), full),
                  pl.BlockSpec(wvt.shape, full),
                  pl.BlockSpec((TM, LANES), lambda bi, i: (i, 0)),
                  pl.BlockSpec((TM, LANES), lambda bi, i: (i, 0)),
                  pl.BlockSpec((LANES, LANES), full),
                  pl.BlockSpec((1, LANES), full),
                  pl.BlockSpec((1, LANES), full)],
        out_specs=[pl.BlockSpec((1, TM, 512), tile),
                   pl.BlockSpec((1, TM, 512), tile),
                   pl.BlockSpec((1, TM, 512), tile),
                   pl.BlockSpec((1, TM, 256), tile),
                   pl.BlockSpec((1, 1, nv, TM), lambda bi, i: (bi, i, 0, 0))],
        out_shape=[jax.ShapeDtypeStruct((b, tall, 512), BF16),
                   jax.ShapeDtypeStruct((b, tall, 512), BF16),
                   jax.ShapeDtypeStruct((b, tall, 512), BF16),
                   jax.ShapeDtypeStruct((b, tall, 256), BF16),
                   jax.ShapeDtypeStruct((b, nt, nv, TM), BF16)],
        compiler_params=_cparams(("parallel", "parallel")),
        name="proj_cd",
    )(x_all, mlat, mctx, w, wvt, cos, sin, bd, gq, gk)


def _fourier_kernel(nb, scale, uw_ref, ca_ref, sa_ref, cb_ref, sb_ref, o_ref):
    j = pl.program_id(0)
    ca = ca_ref[pl.ds(j, 1), :]
    sa = sa_ref[pl.ds(j, 1), :]
    cb = cb_ref[...]
    sb = sb_ref[...]
    ct = (ca * cb - sa * sb).astype(BF16)
    nst = (-(sa * cb + ca * sb)).astype(BF16)
    for bi in range(nb):
        acc = _dot(ct, uw_ref[bi, :, 0:256]) + _dot(nst, uw_ref[bi, :, 256:512])
        o_ref[bi] = (acc * scale).astype(BF16)


def _dft_tables(t, tmf):
    k = np.arange(t, dtype=np.int64)
    j1 = np.arange(t // tmf, dtype=np.int64) * tmf
    j0 = np.arange(tmf, dtype=np.int64)
    aa = (2.0 * np.pi / t) * ((j1[:, None] * k[None, :]) % t)
    ab = (2.0 * np.pi / t) * ((j0[:, None] * k[None, :]) % t)
    f = lambda a: jnp.asarray(a, F32)
    return f(np.cos(aa)), f(np.sin(aa)), f(np.cos(ab)), f(np.sin(ab))


def _fourier(uw):
    b, t, _ = uw.shape
    tmf = min(128, t)
    ca, sa, cb, sb = _dft_tables(t, tmf)
    scale = 1.0 / math.sqrt(t * HEAD_DIM)
    full2 = lambda j: (0, 0)
    return pl.pallas_call(
        functools.partial(_fourier_kernel, b, scale),
        grid=(t // tmf,),
        in_specs=[pl.BlockSpec((b, t, 512), lambda j: (0, 0, 0)),
                  pl.BlockSpec((t // tmf, t), full2),
                  pl.BlockSpec((t // tmf, t), full2),
                  pl.BlockSpec((tmf, t), full2),
                  pl.BlockSpec((tmf, t), full2)],
        out_specs=pl.BlockSpec((b, tmf, 256), lambda j: (0, j, 0)),
        out_shape=jax.ShapeDtypeStruct((b, t, 256), BF16),
        compiler_params=_cparams(("parallel",)),
        name="fourier_mix",
    )(uw, ca, sa, cb, sb)


def _win_attn_kernel(nct, nchunk, sink_ref, q_ref, k_ref, vt_ref, o_ref, s_a, s_b):
    i = pl.program_id(1)
    is_ctx = i < nct
    j = i - nct
    ctx_rows = nct * TM
    ctx_chunks = ctx_rows // LANES
    lw = TM + 2 * WINDOW
    lchunks = lw // LANES
    cs = jnp.clip(ctx_chunks + (TM // LANES) * j - WINDOW // LANES, 0, nchunk - lchunks)
    rs = pl.multiple_of(cs * LANES, LANES)
    qpos = j * TM + lax.broadcasted_iota(jnp.int32, (1, TM), 1)
    kpos = cs * LANES - ctx_rows + lax.broadcasted_iota(jnp.int32, (lw, 1), 0)
    allowed = (jnp.abs(qpos - kpos) <= WINDOW) & (kpos >= 0) & jnp.logical_not(is_ctx)
    lane = lax.broadcasted_iota(jnp.int32, (TM, LANES), 1)
    n_heads = q_ref.shape[-1] // HEAD_DIM
    group = n_heads // (k_ref.shape[-1] // LANES)

    def scores(hq, dst_s):
        pair, half = divmod(hq, 2)
        g = hq // group
        qp = q_ref[0, :, pair * LANES:(pair + 1) * LANES]
        qm = jnp.where((lane >= HEAD_DIM) == (half == 1), qp, jnp.zeros_like(qp))
        dst_s[0:ctx_rows, :] = _dot_nt(k_ref[0, 0:ctx_rows, g * LANES:(g + 1) * LANES], qm)
        s_l = _dot_nt(k_ref[0, pl.ds(rs, lw), g * LANES:(g + 1) * LANES], qm)
        dst_s[ctx_rows:ctx_rows + lw, :] = jnp.where(allowed, s_l, NEG)

    def consume(hq, src_s):
        g = hq // group
        s = src_s[...]
        sk = sink_ref[hq] * LOG2E
        m = jnp.maximum(jnp.max(s, axis=0, keepdims=True), sk)
        p = jnp.exp2(s - m)
        l = jnp.sum(p, axis=0, keepdims=True) + jnp.exp2(sk - m)
        p = p.astype(BF16)
        vrows = slice(g * HEAD_DIM, (g + 1) * HEAD_DIM)
        acc = jnp.zeros((HEAD_DIM, TM), F32)
        for c in range(ctx_chunks):
            acc = acc + _dot(vt_ref[0, c, vrows, :], p[c * LANES:(c + 1) * LANES, :])
        for c in range(lchunks):
            r0 = ctx_rows + c * LANES
            acc = acc + _dot(vt_ref[0, cs + c, vrows, :], p[r0:r0 + LANES, :])
        return acc / l

    bufs = (s_a, s_b)
    scores(0, s_a)
    outs = []
    for hq in range(n_heads):
        if hq + 1 < n_heads:
            scores(hq + 1, bufs[(hq + 1) % 2])
        outs.append(consume(hq, bufs[hq % 2]))
        if hq % 2 == 1:
            pair = hq // 2
            o2 = jnp.concatenate(outs, axis=0)
            o_ref[0, :, pair * LANES:(pair + 1) * LANES] = o2.T.astype(BF16)
            outs = []


def _win_attn(sink, q, k, vt, nct):
    b, tall, qw = q.shape
    nt = tall // TM
    nchunk = vt.shape[1]
    return pl.pallas_call(
        functools.partial(_win_attn_kernel, nct, nchunk),
        grid=(b, nt),
        in_specs=[pl.BlockSpec(memory_space=pltpu.SMEM),
                  pl.BlockSpec((1, TM, qw), lambda bi, i: (bi, i, 0)),
                  pl.BlockSpec((1, tall, k.shape[-1]), lambda bi, i: (bi, 0, 0)),
                  pl.BlockSpec((1, nchunk, vt.shape[2], LANES), lambda bi, i: (bi, 0, 0, 0))],
        out_specs=pl.BlockSpec((1, TM, qw), lambda bi, i: (bi, i, 0)),
        out_shape=jax.ShapeDtypeStruct((b, tall, qw), BF16),
        scratch_shapes=[pltpu.VMEM((nct * TM + TM + 2 * WINDOW, TM), F32)] * 2,
        compiler_params=_cparams(("parallel", "parallel")),
        name="window_attention",
    )(sink, q, k, vt)


def _unit_operands(u):
    if u < 2 * N_C_HEADS:
        hc = u // 2
        dst, dv = _VT_LAYOUT[hc]
        return True, hc, dst, dv
    g = (u - 2 * N_C_HEADS) // (N_D_HEADS // N_D_KV)
    dst, dv = _VT_LAYOUT[N_C_HEADS + g]
    return False, g, dst, dv


def _attn_cd_kernel(nkt, lam_init, lam_ref, sg_ref, qc_ref, qd_ref, kc_ref, kd_ref, vt_ref, o_ref,
                    qm_s, m_s, acc_s, s_a, s_b):
    mq = qc_ref.shape[1]
    tk = vt_ref.shape[-1]
    lane = lax.broadcasted_iota(jnp.int32, (mq, LANES), 1)
    upper = lane >= HEAD_DIM
    for u in range(N_UNITS):
        src = qc_ref if u < 2 * N_C_HEADS else qd_ref
        pair = (u if u < 2 * N_C_HEADS else u - 2 * N_C_HEADS) // 2
        qp = src[0, :, pair * LANES:(pair + 1) * LANES]
        qm_s[u] = jnp.where(upper == (u % 2 == 1), qp, jnp.zeros_like(qp))
    m_s[...] = jnp.full(m_s.shape, NEG, F32)
    acc_s[...] = jnp.zeros(acc_s.shape, F32)

    def scores(tile, u, dst_s):
        is_c, slab, _, _ = _unit_operands(u)
        k_ref = kc_ref if is_c else kd_ref
        r0 = pl.multiple_of(tile * tk, tk)
        dst_s[u] = _dot_nt(k_ref[0, pl.ds(r0, tk), slab * LANES:(slab + 1) * LANES], qm_s[u])

    def consume(tile, u, src_s):
        _, _, row0, dv = _unit_operands(u)
        rows = dv + ONES_ROWS
        s = src_s[u]
        m_old = m_s[u]
        m_new = jnp.maximum(m_old, jnp.max(s, axis=0, keepdims=True))
        alpha = jnp.exp2(m_old - m_new)
        p = jnp.exp2(s - m_new).astype(BF16)
        pv = _dot(vt_ref[0, tile, row0:row0 + rows, :], p)
        acc_s[u, 0:rows, :] = acc_s[u, 0:rows, :] * alpha + pv
        m_s[u] = m_new

    def stage(tile, src_s, dst_s):
        nxt = jnp.minimum(tile + 1, nkt - 1)
        for u in range(N_UNITS):
            scores(nxt, u, dst_s)
            consume(tile, u, src_s)

    for u in range(N_UNITS):
        scores(0, u, s_a)
    stage(0, s_a, s_b)

    def body(pr, carry):
        stage(2 * pr + 1, s_b, s_a)
        stage(2 * pr + 2, s_a, s_b)
        return carry

    lax.fori_loop(0, (nkt - 1) // 2, body, 0)
    if (nkt - 1) % 2 == 1:
        stage(nkt - 1, s_b, s_a)

    lv = lam_ref[...]
    lam = (jnp.exp(jnp.sum(lv[0:1] * lv[1:2], axis=-1, keepdims=True))
           - jnp.exp(jnp.sum(lv[2:3] * lv[3:4], axis=-1, keepdims=True)) + lam_init)
    sg = sg_ref[...]
    dvc = 2 * HEAD_DIM
    for hc in range(N_C_HEADS):
        o1 = acc_s[2 * hc, 0:dvc, :] / acc_s[2 * hc, dvc:dvc + 1, :]
        o2 = acc_s[2 * hc + 1, 0:dvc, :] / acc_s[2 * hc + 1, dvc:dvc + 1, :]
        o = o1 - lam * o2
        ms = jnp.mean(o * o, axis=0, keepdims=True)
        o = o * lax.rsqrt(ms + RMS_EPS) * sg * (1.0 - lam_init)
        o_ref[0, :, hc * LANES:(hc + 1) * LANES] = o.T.astype(BF16)
    for pair in range(N_D_HEADS // 2):
        outs = []
        for half in range(2):
            u = 2 * N_C_HEADS + 2 * pair + half
            outs.append(acc_s[u, 0:HEAD_DIM, :] / acc_s[u, HEAD_DIM:HEAD_DIM + 1, :])
        o2 = jnp.concatenate(outs, axis=0)
        c0 = N_C_HEADS * LANES + pair * LANES
        o_ref[0, :, c0:c0 + LANES] = o2.T.astype(BF16)


def _attn_cd(lam_vec, sg, qc, qd, kc, kd, vt, nct, lam_init):
    b, tall, _ = qc.shape
    nkt, nv, tk = vt.shape[1], vt.shape[2], vt.shape[3]
    mq = TM
    nq = tall // mq - nct
    ow = N_C_HEADS * LANES + N_D_HEADS * HEAD_DIM
    qtile = lambda bi, i: (bi, i + nct, 0)
    return pl.pallas_call(
        functools.partial(_attn_cd_kernel, nkt, lam_init),
        grid=(b, nq),
        in_specs=[pl.BlockSpec((4, HEAD_DIM), lambda bi, i: (0, 0)),
                  pl.BlockSpec((LANES, 1), lambda bi, i: (0, 0)),
                  pl.BlockSpec((1, mq, qc.shape[-1]), qtile),
                  pl.BlockSpec((1, mq, qd.shape[-1]), qtile),
                  pl.BlockSpec((1, tall, kc.shape[-1]), lambda bi, i: (bi, 0, 0)),
                  pl.BlockSpec((1, tall, kd.shape[-1]), lambda bi, i: (bi, 0, 0)),
                  pl.BlockSpec((1, nkt, nv, tk), lambda bi, i: (bi, 0, 0, 0))],
        out_specs=pl.BlockSpec((1, mq, ow), lambda bi, i: (bi, i, 0)),
        out_shape=jax.ShapeDtypeStruct((b, nq * mq, ow), BF16),
        scratch_shapes=[pltpu.VMEM((N_UNITS, mq, LANES), BF16),
                        pltpu.VMEM((N_UNITS, 1, mq), F32),
                        pltpu.VMEM((N_UNITS, 2 * HEAD_DIM + ONES_ROWS, mq), F32),
                        pltpu.VMEM((N_UNITS, tk, mq), F32),
                        pltpu.VMEM((N_UNITS, tk, mq), F32)],
        compiler_params=_cparams(("parallel", "parallel")),
        name="attention_cd",
    )(lam_vec, sg, qc, qd, kc, kd, vt)


def _oproj_kernel(n_parts, widths, nct, alpha, n_exp, *refs):
    o_refs = refs[:n_parts]
    (w_ref, x_ref, mlat_ref, mctx_ref, lng_ref, lnb_ref, rwt_ref, rb_ref,
     x1_ref, h2_ref, te_ref, tg_ref) = refs[n_parts:]
    d = x_ref.shape[-1]
    is_ctx = pl.program_id(1) < nct
    y = None
    r0 = 0
    for o_ref, wd in zip(o_refs, widths):
        part = _dot(o_ref[0], w_ref[r0:r0 + wd, :])
        y = part if y is None else y + part
        r0 += wd
    m = jnp.where(is_ctx, mctx_ref[...], mlat_ref[0])
    x1 = _layernorm(alpha * x_ref[0] + m[:, 2 * d:3 * d] * y, lng_ref[...], lnb_ref[...])
    x1_ref[0] = x1
    h2 = x1 * (1.0 + m[:, 4 * d:5 * d]) + m[:, 3 * d:4 * d]
    h2_ref[0] = h2
    logits = _dot_nt(rwt_ref[...], h2.astype(BF16)) + rb_ref[...]
    tm = logits.shape[1]
    row = lax.broadcasted_iota(jnp.int32, logits.shape, 0)
    vals, idxs = [], []
    for _ in range(TOP_K):
        mx = jnp.max(logits, axis=0, keepdims=True)
        idx = jnp.min(jnp.where(logits == mx, row, n_exp), axis=0, keepdims=True)
        vals.append(mx)
        idxs.append(idx)
        logits = jnp.where(row == idx, NEG, logits)
    es = [jnp.exp(v - vals[0]) for v in vals]
    tot = es[0] + es[1] + es[2] + es[3]
    pad_i = jnp.zeros((8 - TOP_K, tm), jnp.int32)
    pad_f = jnp.zeros((8 - TOP_K, tm), F32)
    te_ref[0] = jnp.concatenate(idxs + [pad_i], axis=0)
    tg_ref[0] = jnp.concatenate([e / tot for e in es] + [pad_f], axis=0)


def _oproj(parts, w_out, x_all, mlat, mctx, lng, lnb, rwt, rb, nct_out, x_tile_off, alpha):
    b, t, _ = parts[0].shape
    d = x_all.shape[-1]
    nt = t // TM
    n_exp = rwt.shape[0]
    widths = tuple(p.shape[-1] for p in parts)
    tile = lambda bi, i: (bi, i, 0)
    full = lambda bi, i: (0, 0)
    in_specs = [pl.BlockSpec((1, TM, wd), tile) for wd in widths] + [
        pl.BlockSpec(w_out.shape, full),
        pl.BlockSpec((1, TM, d), lambda bi, i: (bi, i + x_tile_off, 0)),
        pl.BlockSpec((1, 1, N_MOD * d), lambda bi, i: (bi, 0, 0)),
        pl.BlockSpec((1, N_MOD * d), full),
        pl.BlockSpec((1, d), full),
        pl.BlockSpec((1, d), full),
        pl.BlockSpec((n_exp, d), full),
        pl.BlockSpec((n_exp, 1), full)]
    return pl.pallas_call(
        functools.partial(_oproj_kernel, len(parts), widths, nct_out, alpha, n_exp),
        grid=(b, nt),
        in_specs=in_specs,
        out_specs=[pl.BlockSpec((1, TM, d), tile),
                   pl.BlockSpec((1, TM, d), tile),
                   pl.BlockSpec((1, 8, TM), lambda bi, i: (bi, 0, i)),
                   pl.BlockSpec((1, 8, TM), lambda bi, i: (bi, 0, i))],
        out_shape=[jax.ShapeDtypeStruct((b, t, d), F32),
                   jax.ShapeDtypeStruct((b, t, d), F32),
                   jax.ShapeDtypeStruct((b, 8, t), jnp.int32),
                   jax.ShapeDtypeStruct((b, 8, t), F32)],
        compiler_params=_cparams(("parallel", "parallel")),
        name="out_proj_norm_router",
    )(*parts, w_out, x_all, mlat, mctx, lng, lnb, rwt, rb)


def _moe_kernel(be_ref, nx_ref, nu_ref, x_ref, wgu_hbm, bgu_ref, wd_hbm, bd_ref, y_ref,
                wgu_f, wd_f, wgu_s, wd_s, sem):
    blk = pl.program_id(0)
    ff = wd_s.shape[0]

    def fetch(e):
        return (pltpu.make_async_copy(wgu_hbm.at[e], wgu_f, sem.at[0]),
                pltpu.make_async_copy(wd_hbm.at[e], wd_f, sem.at[1]))

    @pl.when(blk < nu_ref[0])
    def _():
        e = be_ref[blk]
        new_expert = (blk == 0) | (e != be_ref[jnp.maximum(blk - 1, 0)])

        @pl.when(blk == 0)
        def _():
            for cp in fetch(e):
                cp.start()

        @pl.when(new_expert)
        def _():
            for cp in fetch(e):
                cp.wait()
            rows = 128

            def cast_gu(r, c):
                r0 = pl.multiple_of(r * rows, rows)
                wgu_s[pl.ds(r0, rows), :] = wgu_f[pl.ds(r0, rows), :].astype(BF16)
                return c

            def cast_d(r, c):
                r0 = pl.multiple_of(r * rows, rows)
                wd_s[pl.ds(r0, rows), :] = wd_f[pl.ds(r0, rows), :].astype(BF16)
                return c

            lax.fori_loop(0, wgu_s.shape[0] // rows, cast_gu, 0)
            lax.fori_loop(0, wd_s.shape[0] // rows, cast_d, 0)

            @pl.when(nx_ref[blk] != e)
            def _():
                for cp in fetch(nx_ref[blk]):
                    cp.start()

        gu = _dot(x_ref[...].astype(BF16), wgu_s[...]) + bgu_ref[0]
        g = jnp.minimum(gu[:, :ff], SWIGLU_LIMIT)
        u = jnp.clip(gu[:, ff:], -SWIGLU_LIMIT, SWIGLU_LIMIT)
        act = (u + 1.0) * (g / (1.0 + jnp.exp(-SWIGLU_ALPHA * g)))
        y_ref[...] = _dot(act.astype(BF16), wd_s[...]) + bd_ref[0]

    @pl.when(blk >= nu_ref[0])
    def _():
        y_ref[...] = jnp.zeros(y_ref.shape, F32)


def _moe_experts(block_e, block_next_e, n_used, xb, w_gu, b_gu, w_down, b_down):
    n_slots, d = xb.shape
    ff2 = w_gu.shape[-1]
    ff = w_down.shape[-2]
    n_exp = w_gu.shape[0] * w_gu.shape[1]
    w_gu = w_gu.reshape(n_exp, d, ff2)
    w_down = w_down.reshape(n_exp, ff, d)
    nb = n_slots // MOE_BM
    xblk = lambda i, be, nx, nu: (jnp.minimum(i, nu[0] - 1), 0)
    bblk = lambda i, be, nx, nu: (be[i], 0, 0)
    grid_spec = pltpu.PrefetchScalarGridSpec(
        num_scalar_prefetch=3,
        grid=(nb,),
        in_specs=[pl.BlockSpec((MOE_BM, d), xblk),
                  pl.BlockSpec(memory_space=pl.ANY),
                  pl.BlockSpec((1, 1, ff2), bblk),
                  pl.BlockSpec(memory_space=pl.ANY),
                  pl.BlockSpec((1, 1, d), bblk)],
        out_specs=pl.BlockSpec((MOE_BM, d), lambda i, be, nx, nu: (i, 0)),
        scratch_shapes=[pltpu.VMEM((d, ff2), F32), pltpu.VMEM((ff, d), F32),
                        pltpu.VMEM((d, ff2), BF16), pltpu.VMEM((ff, d), BF16),
                        pltpu.SemaphoreType.DMA((2,))])
    return pl.pallas_call(
        _moe_kernel,
        grid_spec=grid_spec,
        out_shape=jax.ShapeDtypeStruct((n_slots, d), F32),
        compiler_params=_cparams(("arbitrary",)),
        name="moe_experts",
    )(block_e, block_next_e, n_used, xb, w_gu, b_gu.reshape(n_exp, 1, ff2), w_down, b_down.reshape(n_exp, 1, d))


def _expert_onehot(te, n_exp):
    row = lax.broadcasted_iota(jnp.int32, (n_exp, te.shape[1]), 0)
    oh = jnp.zeros(row.shape, F32)
    for k in range(TOP_K):
        oh = oh + (row == te[k:k + 1, :]).astype(F32)
    return row, oh


def _groups(cnt):
    return jnp.floor((cnt + (ROW_GROUP - 1)) / ROW_GROUP)


def _route_count_kernel(n_exp, te_ref, n8_ref):
    _, oh = _expert_onehot(te_ref[0], n_exp)
    n8 = _groups(jnp.sum(oh, axis=1, keepdims=True))
    n8_ref[0] = jnp.broadcast_to(n8, (n_exp, LANES)).astype(jnp.int32)


def _route_pos_kernel(n_exp, te_ref, g0_ref, tri_ref, ltri_ref, pos_ref, tab_ref):
    te = te_ref[0]
    tm = te.shape[1]
    row, oh = _expert_onehot(te, n_exp)
    n8 = _groups(jnp.sum(oh, axis=1, keepdims=True))
    lbase = _dot(ltri_ref[...], jnp.broadcast_to(n8, (n_exp, LANES)).astype(BF16))[:, 0:1]
    before = _dot(oh.astype(BF16), tri_ref[...])
    where_to = ROW_GROUP * lbase + before
    pos = [jnp.sum(jnp.where(row == te[k:k + 1, :], where_to, 0.0), axis=0, keepdims=True)
           for k in range(TOP_K)]
    pad = jnp.zeros((8 - TOP_K, tm), F32)
    pos_ref[0] = jnp.concatenate(pos + [pad], axis=0).astype(jnp.int32)
    g = lax.broadcasted_iota(jnp.int32, (n_exp, tm), 1).astype(F32)
    inside = (g >= lbase) & (g < lbase + n8)
    gdst = jnp.sum(jnp.where(inside, g0_ref[0] + g - lbase, 0.0), axis=0, keepdims=True)
    ngrp = jnp.broadcast_to(jnp.sum(n8, axis=0, keepdims=True), (1, tm))
    tab_ref[0] = jnp.concatenate([gdst, ngrp, jnp.zeros((6, tm), F32)], axis=0).astype(jnp.int32)


def _route(te, n_exp):
    b, _, t = te.shape
    nt = t // TM
    n_tiles = b * nt
    assert TOP_K * TM // ROW_GROUP + n_exp <= TM
    tile = pl.BlockSpec((1, 8, TM), lambda bi, i: (bi, 0, i))
    per_tile = lambda bi, i: (bi * nt + i, 0, 0)
    n8 = pl.pallas_call(
        functools.partial(_route_count_kernel, n_exp),
        grid=(b, nt),
        in_specs=[tile],
        out_specs=pl.BlockSpec((1, n_exp, LANES), per_tile),
        out_shape=jax.ShapeDtypeStruct((n_tiles, n_exp, LANES), jnp.int32),
        compiler_params=_cparams(("parallel", "parallel")),
        name="route_count",
    )(te)[:, :, 0]
    rows = ROW_GROUP * jnp.sum(n8, axis=0)
    padded = (rows + MOE_BM - 1) // MOE_BM * MOE_BM
    pend = jnp.cumsum(padded)
    g0 = (pend - padded)[None, :] // ROW_GROUP + jnp.cumsum(n8, axis=0) - n8
    tri = jnp.asarray(np.triu(np.ones((TM, TM)), 1), BF16)
    ltri = jnp.asarray(np.tril(np.ones((n_exp, n_exp)), -1), BF16)
    pos, tab = pl.pallas_call(
        functools.partial(_route_pos_kernel, n_exp),
        grid=(b, nt),
        in_specs=[tile,
                  pl.BlockSpec((1, n_exp, 1), per_tile),
                  pl.BlockSpec((TM, TM), lambda bi, i: (0, 0)),
                  pl.BlockSpec((n_exp, n_exp), lambda bi, i: (0, 0))],
        out_specs=[tile, pl.BlockSpec((1, 8, TM), per_tile)],
        out_shape=[jax.ShapeDtypeStruct((b, 8, t), jnp.int32),
                   jax.ShapeDtypeStruct((n_tiles, 8, TM), jnp.int32)],
        compiler_params=_cparams(("parallel", "parallel")),
        name="route_positions",
    )(te, g0.astype(F32).reshape(n_tiles, n_exp, 1), tri, ltri)
    gdst = jnp.pad(tab[:, 0, :], ((0, 0), (0, SMEM_BLOCK - TM))).reshape(-1)
    ngrp = tab[:, 1, 0]
    max_rows = b * t * TOP_K + n_tiles * n_exp * (ROW_GROUP - 1)
    n_blocks = -(-max_rows // MOE_BM) + n_exp
    n_used = (pend[-1] // MOE_BM).astype(jnp.int32)
    blk = jnp.minimum(jnp.arange(n_blocks, dtype=jnp.int32), n_used - 1) * MOE_BM
    block_e = jnp.minimum(jnp.sum((pend[None, :] <= blk[:, None]).astype(jnp.int32), axis=1), n_exp - 1)
    ids = jnp.arange(n_exp, dtype=jnp.int32)
    later = (padded[None, :] > 0) & (ids[None, :] > block_e[:, None])
    block_next_e = jnp.min(jnp.where(later, ids[None, :], n_exp), axis=1)
    block_next_e = jnp.where(block_next_e == n_exp, block_e, block_next_e).astype(jnp.int32)
    return pos, gdst, ngrp, pend.astype(jnp.int32), block_e, block_next_e, n_used.reshape(1), n_blocks


def _group_copy(src_ref, src_group, dst_ref, dst_group, sem):
    s0 = pl.multiple_of(src_group * ROW_GROUP, ROW_GROUP)
    d0 = pl.multiple_of(dst_group * ROW_GROUP, ROW_GROUP)
    return pltpu.make_async_copy(src_ref.at[pl.ds(s0, ROW_GROUP)], dst_ref.at[pl.ds(d0, ROW_GROUP)], sem)


def _tile_step():
    return pl.program_id(0) * pl.num_programs(1) + pl.program_id(1)


def _moe_scatter_kernel(n_exp, pend_ref, ngrp_ref, gdst_ref, pos_ref, h_ref, xb_ref, xs, zero_s, sem):
    step = _tile_step()
    tm = h_ref.shape[1]

    @pl.when(step == 0)
    def _():
        zero_s[...] = jnp.zeros(zero_s.shape, zero_s.dtype)
        for phase in ("start", "wait"):
            for e in range(n_exp):
                lo = pend_ref[e - 1] if e else 0

                @pl.when(pend_ref[e] > lo)
                def _():
                    r0 = pl.multiple_of(pend_ref[e] - MOE_BM, MOE_BM)
                    cp = pltpu.make_async_copy(zero_s, xb_ref.at[pl.ds(r0, MOE_BM)], sem)
                    cp.start() if phase == "start" else cp.wait()

            def unused_block(blk, c):
                r0 = pl.multiple_of(blk * MOE_BM, MOE_BM)
                cp = pltpu.make_async_copy(zero_s, xb_ref.at[pl.ds(r0, MOE_BM)], sem)
                cp.start() if phase == "start" else cp.wait()
                return c

            lax.fori_loop(pend_ref[n_exp - 1] // MOE_BM, xb_ref.shape[0] // MOE_BM, unused_block, 0)

    pos = pos_ref[0]
    hb = h_ref[0].astype(BF16)
    for c in range(xs.shape[0] // tm):
        r = lax.broadcasted_iota(jnp.int32, (tm, tm), 0) + c * tm
        hit = r == pos[0:1, :]
        for k in range(1, TOP_K):
            hit = hit | (r == pos[k:k + 1, :])
        xs[c * tm:(c + 1) * tm, :] = _dot(jnp.where(hit, 1.0, 0.0).astype(BF16), hb)

    n = ngrp_ref[step]

    def issue(g, c):
        _group_copy(xs, g, xb_ref, gdst_ref[g], sem).start()
        return c

    def drain(g, c):
        _group_copy(xs, g, xb_ref, gdst_ref[g], sem).wait()
        return c

    lax.fori_loop(0, n, issue, 0)
    lax.fori_loop(0, n, drain, 0)


def _sort_rows(n_exp):
    return -(-(TOP_K * TM + n_exp * ROW_GROUP) // TM) * TM


def _moe_scatter(pend, ngrp, gdst, pos, h2, n_slots):
    b, t, d = h2.shape
    nt = t // TM
    n_exp = pend.shape[0]
    grid_spec = pltpu.PrefetchScalarGridSpec(
        num_scalar_prefetch=2,
        grid=(b, nt),
        in_specs=[pl.BlockSpec((SMEM_BLOCK,), lambda bi, i, pe, ng: (bi * nt + i,), memory_space=pltpu.SMEM),
                  pl.BlockSpec((1, 8, TM), lambda bi, i, pe, ng: (bi, 0, i)),
                  pl.BlockSpec((1, TM, d), lambda bi, i, pe, ng: (bi, i, 0))],
        out_specs=pl.BlockSpec(memory_space=pl.ANY),
        scratch_shapes=[pltpu.VMEM((_sort_rows(n_exp), d), F32), pltpu.VMEM((MOE_BM, d), F32),
                        pltpu.SemaphoreType.DMA])
    return pl.pallas_call(
        functools.partial(_moe_scatter_kernel, n_exp),
        grid_spec=grid_spec,
        out_shape=jax.ShapeDtypeStruct((n_slots, d), F32),
        compiler_params=_cparams(("arbitrary", "arbitrary")),
        name="moe_scatter_rows",
    )(pend, ngrp, gdst, pos, h2)


def _combine_kernel(nct, alpha, ngrp_ref, gdst_ref, yb_ref, pos_ref, tg_ref, x1_ref, mlat_ref, mctx_ref,
                    lng_ref, lnb_ref, o_ref, ys, sem):
    d = x1_ref.shape[-1]
    tm = x1_ref.shape[1]
    step = _tile_step()

    @pl.when(step == 0)
    def _():
        ys[...] = jnp.zeros(ys.shape, F32)

    n = ngrp_ref[step]

    def issue(g, c):
        _group_copy(yb_ref, gdst_ref[g], ys, g, sem).start()
        return c

    def drain(g, c):
        _group_copy(yb_ref, gdst_ref[g], ys, g, sem).wait()
        return c

    lax.fori_loop(0, n, issue, 0)
    is_ctx = pl.program_id(1) < nct
    m = jnp.where(is_ctx, mctx_ref[...], mlat_ref[0])
    pos = pos_ref[0]
    gates = tg_ref[0]
    lax.fori_loop(0, n, drain, 0)
    f = None
    for c in range(ys.shape[0] // tm):
        col = lax.broadcasted_iota(jnp.int32, (tm, tm), 1) + c * tm
        w = jnp.where(col == pos[:, 0:1], gates[:, 0:1], 0.0)
        for k in range(1, TOP_K):
            w = w + jnp.where(col == pos[:, k:k + 1], gates[:, k:k + 1], 0.0)
        part = _dot(w.astype(BF16), ys[c * tm:(c + 1) * tm, :].astype(BF16))
        f = part if f is None else f + part
    o_ref[0] = _layernorm(alpha * x1_ref[0] + m[:, 5 * d:6 * d] * f, lng_ref[...], lnb_ref[...])


def _combine_postnorm(ngrp, gdst, yb, pos_t, tg_t, x1, mlat, mctx, lng, lnb, nct, alpha, n_exp):
    b, t, d = x1.shape
    nt = t // TM
    tile = lambda bi, i, ng: (bi, i, 0)
    full = lambda bi, i, ng: (0, 0)
    grid_spec = pltpu.PrefetchScalarGridSpec(
        num_scalar_prefetch=1,
        grid=(b, nt),
        in_specs=[pl.BlockSpec((SMEM_BLOCK,), lambda bi, i, ng: (bi * nt + i,), memory_space=pltpu.SMEM),
                  pl.BlockSpec(memory_space=pl.ANY),
                  pl.BlockSpec((1, TM, 8), tile),
                  pl.BlockSpec((1, TM, 8), tile),
                  pl.BlockSpec((1, TM, d), tile),
                  pl.BlockSpec((1, 1, N_MOD * d), lambda bi, i, ng: (bi, 0, 0)),
                  pl.BlockSpec((1, N_MOD * d), full),
                  pl.BlockSpec((1, d), full),
                  pl.BlockSpec((1, d), full)],
        out_specs=pl.BlockSpec((1, TM, d), tile),
        scratch_shapes=[pltpu.VMEM((_sort_rows(n_exp), d), F32), pltpu.SemaphoreType.DMA])
    return pl.pallas_call(
        functools.partial(_combine_kernel, nct, alpha),
        grid_spec=grid_spec,
        out_shape=jax.ShapeDtypeStruct((b, t, d), F32),
        compiler_params=_cparams(("arbitrary", "arbitrary")),
        name="combine_post_norm",
    )(ngrp, gdst, yb, pos_t, tg_t, x1, mlat, mctx, lng, lnb)


def _rope_tables(ctx_len, seq):
    t = np.arange(seq)
    row = (t // GRID_W).astype(np.float64)
    col = (t % GRID_W).astype(np.float64)
    nf = HEAD_DIM // 4
    inv = ROPE_THETA ** (-np.arange(nf, dtype=np.float64) / nf)
    ar = row[:, None] * inv[None, :]
    ac = col[:, None] * inv[None, :]
    ang = np.concatenate([ar, ar, ac, ac], axis=-1)
    cos = np.concatenate([np.ones((ctx_len, HEAD_DIM)), np.cos(ang)], axis=0)
    sin = np.concatenate([np.zeros((ctx_len, HEAD_DIM)), np.sin(ang)], axis=0)
    sign = np.where((np.arange(HEAD_DIM) % 32) < 16, -1.0, 1.0)[None, :]
    cos2 = np.tile(cos, (1, LANES // HEAD_DIM))
    sin2 = np.tile(sin * sign, (1, LANES // HEAD_DIM))
    return jnp.asarray(cos2, F32), jnp.asarray(sin2, F32)


def _channel_dft():
    c = np.arange(HEAD_DIM)
    ang = 2.0 * np.pi * ((c[:, None] * c[None, :]) % HEAD_DIM) / HEAD_DIM
    eye = np.eye(4)
    cs = np.concatenate([np.kron(eye, np.cos(ang)), np.kron(eye, np.sin(ang))], axis=1)
    return jnp.asarray(cs, BF16)


def _group_mean_matrix():
    bd = np.kron(np.eye(LANES // HEAD_DIM), np.full((HEAD_DIM, HEAD_DIM), 1.0 / HEAD_DIM))
    return jnp.asarray(bd, BF16)


def _dup_heads(wk, n_heads):
    d = wk.shape[0]
    return jnp.broadcast_to(wk.reshape(d, n_heads, 1, HEAD_DIM), (d, n_heads, 2, HEAD_DIM)).reshape(d, n_heads * LANES)


def _lambda_init(layer):
    return 0.8 - 0.6 * math.exp(-0.3 * layer)


def _moe_postnorm(h2, te, tg, x1, mlat, mctx, lng, lnb, nct, alpha, layer, w_gu, b_gu, w_down, b_down):
    n_exp = w_gu.shape[1]
    pos, gdst, ngrp, pend, block_e, block_next_e, n_used, n_blocks = _route(te, n_exp)
    xb = _moe_scatter(pend, ngrp, gdst, pos, h2, n_blocks * MOE_BM)
    off = layer * n_exp
    yb = _moe_experts(block_e + off, block_next_e + off, n_used, xb, w_gu, b_gu, w_down, b_down)
    pos_t, tg_t = jnp.transpose(pos, (0, 2, 1)), jnp.transpose(tg, (0, 2, 1))
    return _combine_postnorm(ngrp, gdst, yb, pos_t, tg_t, x1, mlat, mctx, lng, lnb, nct, alpha, n_exp)


def kernel(x, c, ctx, c_ctx, mod_w, mod_b, ln_g, ln_b, ab_w_in, ab_sink, ab_w_out,
           cd_w_in, cd_lambda, cd_subln_g, cd_q_norm_g, cd_k_norm_g, cd_w_out,
           router_w, router_b, expert_w_gu, expert_b_gu, expert_w_down, expert_b_down):
    b, s, d = x.shape
    n_ctx = ctx.shape[1]
    depth = mod_w.shape[0]
    n_exp = router_w.shape[-1]
    assert d == 16 * HEAD_DIM and n_ctx % TM == 0 and s % TM == 0 and s % GRID_W == 0
    nct = n_ctx // TM
    alpha = (2 * depth) ** 0.25

    cos, sin = _rope_tables(n_ctx, s)
    cs_dft = _channel_dft()
    bd = _group_mean_matrix()
    mods = _mod_vectors(c, c_ctx, mod_w, mod_b)

    x_all = jnp.concatenate([ctx, x], axis=1)
    for l in range(depth):
        last = l == depth - 1
        i = l // 2
        mlat = mods[l, :b].reshape(b, 1, N_MOD * d)
        mctx = mods[l, b:b + 1]
        lng1, lnb1 = ln_g[l, 0].reshape(1, d), ln_b[l, 0].reshape(1, d)
        lng2, lnb2 = ln_g[l, 1].reshape(1, d), ln_b[l, 1].reshape(1, d)
        rwt = router_w[l].T.astype(BF16)
        rb = router_b[l].reshape(n_exp, 1)
        if l % 2 == 0:
            w = ab_w_in[i]
            w_n = jnp.concatenate([w[:, :1024], _dup_heads(w[:, 1024:1280], 4)], axis=1).astype(BF16)
            wvt = w[:, 1280:1536].T.astype(BF16)
            uw, q, k, vt = _proj_ab(x_all, mlat, mctx, w_n, wvt, cos, sin, cs_dft, nct)
            oa = jnp.concatenate([_fourier(uw[:, :n_ctx]), _fourier(uw[:, n_ctx:])], axis=1)
            ob = _win_attn(ab_sink[i], q, k, vt, nct)
            if last:
                parts = [oa[:, n_ctx:], ob[:, n_ctx:]]
            else:
                parts = [oa, ob]
            w_out = ab_w_out[i].astype(BF16)
        else:
            w = cd_w_in[i]
            w_n = jnp.concatenate([w[:, :1536], _dup_heads(w[:, 1536:1664], 2)], axis=1).astype(BF16)
            wvt = w[:, 1664:2304].T.astype(BF16)
            gq = jnp.tile(cd_q_norm_g[i], 2).reshape(1, LANES)
            gk = jnp.tile(cd_k_norm_g[i], 2).reshape(1, LANES)
            qc, qd, kc, kd, vt = _proj_cd(x_all, mlat, mctx, w_n, wvt, cos, sin, bd, gq, gk, nct)
            sg = cd_subln_g[i].reshape(LANES, 1)
            o_lat = _attn_cd(cd_lambda[i], sg, qc, qd, kc, kd, vt, nct, _lambda_init(l))
            if last:
                parts = [o_lat]
            else:
                raise NotImplementedError("context outputs of a differential/axial layer")
            w_out = cd_w_out[i].astype(BF16)
        experts = (l, expert_w_gu, expert_b_gu, expert_w_down, expert_b_down)
        if last:
            x1, h2, te, tg = _oproj(parts, w_out, x_all, mlat, mctx, lng1, lnb1, rwt, rb, 0, nct, alpha)
            return _moe_postnorm(h2, te, tg, x1, mlat, mctx, lng2, lnb2, 0, alpha, *experts)
        x1, h2, te, tg = _oproj(parts, w_out, x_all, mlat, mctx, lng1, lnb1, rwt, rb, nct, 0, alpha)
        x_all = _moe_postnorm(h2, te, tg, x1, mlat, mctx, lng2, lnb2, nct, alpha, *experts)
    return x_all[:, n_ctx:]
```

```python
import functools
import math

import jax
import jax.numpy as jnp
import numpy as np
from jax import lax
from jax.experimental import pallas as pl
from jax.experimental.pallas import tpu as pltpu

F32 = jnp.float32
BF16 = jnp.bfloat16

HEAD_DIM = 64
GRID_W = 64
WINDOW = 128
ROPE_THETA = 10000.0
LN_EPS = 1e-6
RMS_EPS = 1e-6
N_MOD = 6
TOP_K = 4
SWIGLU_LIMIT = 7.0
SWIGLU_ALPHA = 1.702
NEG = -1e30
LOG2E = 1.4426950408889634
QSCALE = HEAD_DIM ** -0.5 * LOG2E

N_C_HEADS = 4
N_D_HEADS = 8
N_D_KV = 2
N_UNITS = 2 * N_C_HEADS + N_D_HEADS
ONES_ROWS = 16
_VT_LAYOUT = tuple((h * (2 * HEAD_DIM + ONES_ROWS), 2 * HEAD_DIM) for h in range(N_C_HEADS)) + tuple(
    (N_C_HEADS * (2 * HEAD_DIM + ONES_ROWS) + g * (HEAD_DIM + ONES_ROWS), HEAD_DIM) for g in range(N_D_KV))
VT_ROWS = _VT_LAYOUT[-1][0] + HEAD_DIM + ONES_ROWS

LANES = 128
TM = 256
MOE_BM = 256
ROW_GROUP = 8
SMEM_BLOCK = 1024
VMEM_LIMIT = 56 * 1024 * 1024


def _dot(a, b):
    return jnp.dot(a, b, preferred_element_type=F32)


def _dot_nt(a, b):
    return lax.dot_general(a, b, (((1,), (1,)), ((), ())), preferred_element_type=F32)


def _cparams(sem):
    return pltpu.CompilerParams(dimension_semantics=sem, vmem_limit_bytes=VMEM_LIMIT)


def _layernorm(z, g, b):
    mu = jnp.mean(z, axis=-1, keepdims=True)
    d = z - mu
    var = jnp.mean(d * d, axis=-1, keepdims=True)
    return d * lax.rsqrt(var + LN_EPS) * g + b


def _rope128(x, cos, sin_signed):
    lane = lax.broadcasted_iota(jnp.int32, x.shape, 1)
    lo = (lane % 32) < 16
    rot = jnp.where(lo, pltpu.roll(x, LANES - 16, 1), pltpu.roll(x, 16, 1))
    return x * cos + rot * sin_signed


def _rmsnorm128(x, g, bd):
    x2 = x * x
    hi = x2.astype(BF16)
    lo = (x2 - hi.astype(F32)).astype(BF16)
    ms = _dot(hi, bd) + _dot(lo, bd)
    return x * lax.rsqrt(ms + RMS_EPS) * g


def _modulated(x_ref, mlat_ref, mctx_ref, is_ctx, d):
    m = jnp.where(is_ctx, mctx_ref[...], mlat_ref[0])
    return (x_ref[0] * (1.0 + m[:, d:2 * d]) + m[:, 0:d]).astype(BF16)


def _mod_kernel(c_ref, w_ref, b_ref, o_ref):
    c = c_ref[...]
    s = (c / (1.0 + jnp.exp(-c))).astype(BF16)
    o_ref[0] = _dot(s, w_ref[0].astype(BF16)) + b_ref[0]


def _mod_vectors(c, c_ctx, mod_w, mod_b):
    depth, d, n = mod_w.shape
    b = c.shape[0]
    rows = 8 * (-(-(b + 1) // 8))
    cs = jnp.zeros((rows, d), F32).at[:b].set(c).at[b].set(c_ctx)
    tn = 1536
    out = pl.pallas_call(
        _mod_kernel,
        grid=(depth, n // tn),
        in_specs=[pl.BlockSpec((rows, d), lambda l, j: (0, 0)),
                  pl.BlockSpec((1, d, tn), lambda l, j: (l, 0, j)),
                  pl.BlockSpec((1, 1, tn), lambda l, j: (l, 0, j))],
        out_specs=pl.BlockSpec((1, rows, tn), lambda l, j: (l, 0, j)),
        out_shape=jax.ShapeDtypeStruct((depth, rows, n), F32),
        compiler_params=_cparams(("parallel", "parallel")),
        name="mod_vectors",
    )(cs, mod_w, mod_b.reshape(depth, 1, n))
    return out


def _proj_ab_kernel(nct, x_ref, mlat_ref, mctx_ref, w_ref, wvt_ref, cos_ref, sin_ref, cs_ref,
                    uw_ref, q_ref, k_ref, vt_ref):
    is_ctx = pl.program_id(1) < nct
    d = x_ref.shape[-1]
    h = _modulated(x_ref, mlat_ref, mctx_ref, is_ctx, d)
    p = _dot(h, w_ref[...])
    uw_ref[0] = _dot(p[:, 0:256].astype(BF16), cs_ref[...]).astype(BF16)
    cos = cos_ref[...]
    sin = sin_ref[...]
    for j in range(6):
        c0 = 256 + j * LANES
        q_ref[0, :, j * LANES:(j + 1) * LANES] = (_rope128(p[:, c0:c0 + LANES], cos, sin) * QSCALE).astype(BF16)
    for j in range(4):
        c0 = 1024 + j * LANES
        k_ref[0, :, j * LANES:(j + 1) * LANES] = _rope128(p[:, c0:c0 + LANES], cos, sin).astype(BF16)
    vt = _dot_nt(wvt_ref[...], h).astype(BF16)
    for c in range(TM // LANES):
        vt_ref[0, c] = vt[:, c * LANES:(c + 1) * LANES]


def _proj_ab(x_all, mlat, mctx, w, wvt, cos, sin, cs, nct):
    b, tall, d = x_all.shape
    nt = tall // TM
    nw = w.shape[1]
    nchunk = tall // LANES
    cpt = TM // LANES
    return pl.pallas_call(
        functools.partial(_proj_ab_kernel, nct),
        grid=(b, nt),
        in_specs=[pl.BlockSpec((1, TM, d), lambda bi, i: (bi, i, 0)),
                  pl.BlockSpec((1, 1, N_MOD * d), lambda bi, i: (bi, 0, 0)),
                  pl.BlockSpec((1, N_MOD * d), lambda bi, i: (0, 0)),
                  pl.BlockSpec((d, nw), lambda bi, i: (0, 0)),
                  pl.BlockSpec((256, d), lambda bi, i: (0, 0)),
                  pl.BlockSpec((TM, LANES), lambda bi, i: (i, 0)),
                  pl.BlockSpec((TM, LANES), lambda bi, i: (i, 0)),
                  pl.BlockSpec((256, 512), lambda bi, i: (0, 0))],
        out_specs=[pl.BlockSpec((1, TM, 512), lambda bi, i: (bi, i, 0)),
                   pl.BlockSpec((1, TM, 768), lambda bi, i: (bi, i, 0)),
                   pl.BlockSpec((1, TM, 512), lambda bi, i: (bi, i, 0)),
                   pl.BlockSpec((1, cpt, 256, LANES), lambda bi, i: (bi, i, 0, 0))],
        out_shape=[jax.ShapeDtypeStruct((b, tall, 512), BF16),
                   jax.ShapeDtypeStruct((b, tall, 768), BF16),
                   jax.ShapeDtypeStruct((b, tall, 512), BF16),
                   jax.ShapeDtypeStruct((b, nchunk, 256, LANES), BF16)],
        compiler_params=_cparams(("parallel", "parallel")),
        name="proj_ab",
    )(x_all, mlat, mctx, w, wvt, cos, sin, cs)


def _proj_cd_kernel(nct, x_ref, mlat_ref, mctx_ref, w_ref, wvt_ref, cos_ref, sin_ref, bd_ref,
                    gq_ref, gk_ref, qc_ref, qd_ref, kc_ref, kd_ref, vt_ref):
    is_ctx = pl.program_id(1) < nct
    d = x_ref.shape[-1]
    h = _modulated(x_ref, mlat_ref, mctx_ref, is_ctx, d)
    p = _dot(h, w_ref[...])
    cos = cos_ref[...]
    sin = sin_ref[...]
    bd = bd_ref[...]
    gq = gq_ref[...]
    gk = gk_ref[...]
    for j in range(4):
        sl = slice(j * LANES, (j + 1) * LANES)
        qc_ref[0, :, sl] = (_rope128(p[:, j * LANES:(j + 1) * LANES], cos, sin) * QSCALE).astype(BF16)
        c0 = 512 + j * LANES
        qd_ref[0, :, sl] = (_rope128(_rmsnorm128(p[:, c0:c0 + LANES], gq, bd), cos, sin) * QSCALE).astype(BF16)
        c0 = 1024 + j * LANES
        kc_ref[0, :, sl] = _rope128(p[:, c0:c0 + LANES], cos, sin).astype(BF16)
    for j in range(2):
        c0 = 1536 + j * LANES
        kd_ref[0, :, j * LANES:(j + 1) * LANES] = _rope128(
            _rmsnorm128(p[:, c0:c0 + LANES], gk, bd), cos, sin).astype(BF16)
    vt = _dot_nt(wvt_ref[...], h).astype(BF16)
    ones = jnp.ones((ONES_ROWS, vt.shape[1]), BF16)
    src = 0
    for dst, dv in _VT_LAYOUT:
        vt_ref[0, 0, dst:dst + dv, :] = vt[src:src + dv, :]
        vt_ref[0, 0, dst + dv:dst + dv + ONES_ROWS, :] = ones
        src += dv


def _proj_cd(x_all, mlat, mctx, w, wvt, cos, sin, bd, gq, gk, nct):
    b, tall, d = x_all.shape
    nt = tall // TM
    nw = w.shape[1]
    nv = VT_ROWS
    full = lambda bi, i: (0, 0)
    tile = lambda bi, i: (bi, i, 0)
    return pl.pallas_call(
        functools.partial(_proj_cd_kernel, nct),
        grid=(b, nt),
        in_specs=[pl.BlockSpec((1, TM, d), tile),
                  pl.BlockSpec((1, 1, N_MOD * d), lambda bi, i: (bi, 0, 0)),
                  pl.BlockSpec((1, N_MOD * d), full),
                  pl.BlockSpec((d, nw), full),
                  pl.BlockSpec(wvt.shape, full),
                  pl.BlockSpec((TM, LANES), lambda bi, i: (i, 0)),
                  pl.BlockSpec((TM, LANES), lambda bi, i: (i, 0)),
                  pl.BlockSpec((LANES, LANES), full),
                  pl.BlockSpec((1, LANES), full),
                  pl.BlockSpec((1, LANES), full)],
        out_specs=[pl.BlockSpec((1, TM, 512), tile),
                   pl.BlockSpec((1, TM, 512), tile),
                   pl.BlockSpec((1, TM, 512), tile),
                   pl.BlockSpec((1, TM, 256), tile),
                   pl.BlockSpec((1, 1, nv, TM), lambda bi, i: (bi, i, 0, 0))],
        out_shape=[jax.ShapeDtypeStruct((b, tall, 512), BF16),
                   jax.ShapeDtypeStruct((b, tall, 512), BF16),
                   jax.ShapeDtypeStruct((b, tall, 512), BF16),
                   jax.ShapeDtypeStruct((b, tall, 256), BF16),
                   jax.ShapeDtypeStruct((b, nt, nv, TM), BF16)],
        compiler_params=_cparams(("parallel", "parallel")),
        name="proj_cd",
    )(x_all, mlat, mctx, w, wvt, cos, sin, bd, gq, gk)


def _fourier_kernel(nb, scale, uw_ref, ca_ref, sa_ref, cb_ref, sb_ref, o_ref):
    j = pl.program_id(0)
    ca = ca_ref[pl.ds(j, 1), :]
    sa = sa_ref[pl.ds(j, 1), :]
    cb = cb_ref[...]
    sb = sb_ref[...]
    ct = (ca * cb - sa * sb).astype(BF16)
    nst = (-(sa * cb + ca * sb)).astype(BF16)
    for bi in range(nb):
        acc = _dot(ct, uw_ref[bi, :, 0:256]) + _dot(nst, uw_ref[bi, :, 256:512])
        o_ref[bi] = (acc * scale).astype(BF16)


def _dft_tables(t, tmf):
    k = np.arange(t, dtype=np.int64)
    j1 = np.arange(t // tmf, dtype=np.int64) * tmf
    j0 = np.arange(tmf, dtype=np.int64)
    aa = (2.0 * np.pi / t) * ((j1[:, None] * k[None, :]) % t)
    ab = (2.0 * np.pi / t) * ((j0[:, None] * k[None, :]) % t)
    f = lambda a: jnp.asarray(a, F32)
    return f(np.cos(aa)), f(np.sin(aa)), f(np.cos(ab)), f(np.sin(ab))


def _fourier(uw):
    b, t, _ = uw.shape
    tmf = min(128, t)
    ca, sa, cb, sb = _dft_tables(t, tmf)
    scale = 1.0 / math.sqrt(t * HEAD_DIM)
    full2 = lambda j: (0, 0)
    return pl.pallas_call(
        functools.partial(_fourier_kernel, b, scale),
        grid=(t // tmf,),
        in_specs=[pl.BlockSpec((b, t, 512), lambda j: (0, 0, 0)),
                  pl.BlockSpec((t // tmf, t), full2),
                  pl.BlockSpec((t // tmf, t), full2),
                  pl.BlockSpec((tmf, t), full2),
                  pl.BlockSpec((tmf, t), full2)],
        out_specs=pl.BlockSpec((b, tmf, 256), lambda j: (0, j, 0)),
        out_shape=jax.ShapeDtypeStruct((b, t, 256), BF16),
        compiler_params=_cparams(("parallel",)),
        name="fourier_mix",
    )(uw, ca, sa, cb, sb)


def _win_attn_kernel(nct, nchunk, sink_ref, q_ref, k_ref, vt_ref, o_ref, s_a, s_b):
    i = pl.program_id(1)
    is_ctx = i < nct
    j = i - nct
    ctx_rows = nct * TM
    ctx_chunks = ctx_rows // LANES
    lw = TM + 2 * WINDOW
    lchunks = lw // LANES
    cs = jnp.clip(ctx_chunks + (TM // LANES) * j - WINDOW // LANES, 0, nchunk - lchunks)
    rs = pl.multiple_of(cs * LANES, LANES)
    qpos = j * TM + lax.broadcasted_iota(jnp.int32, (1, TM), 1)
    kpos = cs * LANES - ctx_rows + lax.broadcasted_iota(jnp.int32, (lw, 1), 0)
    allowed = (jnp.abs(qpos - kpos) <= WINDOW) & (kpos >= 0) & jnp.logical_not(is_ctx)
    lane = lax.broadcasted_iota(jnp.int32, (TM, LANES), 1)
    n_heads = q_ref.shape[-1] // HEAD_DIM
    group = n_heads // (k_ref.shape[-1] // LANES)

    def scores(hq, dst_s):
        pair, half = divmod(hq, 2)
        g = hq // group
        qp = q_ref[0, :, pair * LANES:(pair + 1) * LANES]
        qm = jnp.where((lane >= HEAD_DIM) == (half == 1), qp, jnp.zeros_like(qp))
        dst_s[0:ctx_rows, :] = _dot_nt(k_ref[0, 0:ctx_rows, g * LANES:(g + 1) * LANES], qm)
        s_l = _dot_nt(k_ref[0, pl.ds(rs, lw), g * LANES:(g + 1) * LANES], qm)
        dst_s[ctx_rows:ctx_rows + lw, :] = jnp.where(allowed, s_l, NEG)

    def consume(hq, src_s):
        g = hq // group
        s = src_s[...]
        sk = sink_ref[hq] * LOG2E
        m = jnp.maximum(jnp.max(s, axis=0, keepdims=True), sk)
        p = jnp.exp2(s - m)
        l = jnp.sum(p, axis=0, keepdims=True) + jnp.exp2(sk - m)
        p = p.astype(BF16)
        vrows = slice(g * HEAD_DIM, (g + 1) * HEAD_DIM)
        acc = jnp.zeros((HEAD_DIM, TM), F32)
        for c in range(ctx_chunks):
            acc = acc + _dot(vt_ref[0, c, vrows, :], p[c * LANES:(c + 1) * LANES, :])
        for c in range(lchunks):
            r0 = ctx_rows + c * LANES
            acc = acc + _dot(vt_ref[0, cs + c, vrows, :], p[r0:r0 + LANES, :])
        return acc / l

    bufs = (s_a, s_b)
    scores(0, s_a)
    outs = []
    for hq in range(n_heads):
        if hq + 1 < n_heads:
            scores(hq + 1, bufs[(hq + 1) % 2])
        outs.append(consume(hq, bufs[hq % 2]))
        if hq % 2 == 1:
            pair = hq // 2
            o2 = jnp.concatenate(outs, axis=0)
            o_ref[0, :, pair * LANES:(pair + 1) * LANES] = o2.T.astype(BF16)
            outs = []


def _win_attn(sink, q, k, vt, nct):
    b, tall, qw = q.shape
    nt = tall // TM
    nchunk = vt.shape[1]
    return pl.pallas_call(
        functools.partial(_win_attn_kernel, nct, nchunk),
        grid=(b, nt),
        in_specs=[pl.BlockSpec(memory_space=pltpu.SMEM),
                  pl.BlockSpec((1, TM, qw), lambda bi, i: (bi, i, 0)),
                  pl.BlockSpec((1, tall, k.shape[-1]), lambda bi, i: (bi, 0, 0)),
                  pl.BlockSpec((1, nchunk, vt.shape[2], LANES), lambda bi, i: (bi, 0, 0, 0))],
        out_specs=pl.BlockSpec((1, TM, qw), lambda bi, i: (bi, i, 0)),
        out_shape=jax.ShapeDtypeStruct((b, tall, qw), BF16),
        scratch_shapes=[pltpu.VMEM((nct * TM + TM + 2 * WINDOW, TM), F32)] * 2,
        compiler_params=_cparams(("parallel", "parallel")),
        name="window_attention",
    )(sink, q, k, vt)


def _unit_operands(u):
    if u < 2 * N_C_HEADS:
        hc = u // 2
        dst, dv = _VT_LAYOUT[hc]
        return True, hc, dst, dv
    g = (u - 2 * N_C_HEADS) // (N_D_HEADS // N_D_KV)
    dst, dv = _VT_LAYOUT[N_C_HEADS + g]
    return False, g, dst, dv


def _attn_cd_kernel(nkt, lam_init, lam_ref, sg_ref, qc_ref, qd_ref, kc_ref, kd_ref, vt_ref, o_ref,
                    qm_s, m_s, acc_s, s_a, s_b):
    mq = qc_ref.shape[1]
    tk = vt_ref.shape[-1]
    lane = lax.broadcasted_iota(jnp.int32, (mq, LANES), 1)
    upper = lane >= HEAD_DIM
    for u in range(N_UNITS):
        src = qc_ref if u < 2 * N_C_HEADS else qd_ref
        pair = (u if u < 2 * N_C_HEADS else u - 2 * N_C_HEADS) // 2
        qp = src[0, :, pair * LANES:(pair + 1) * LANES]
        qm_s[u] = jnp.where(upper == (u % 2 == 1), qp, jnp.zeros_like(qp))
    m_s[...] = jnp.full(m_s.shape, NEG, F32)
    acc_s[...] = jnp.zeros(acc_s.shape, F32)

    def scores(tile, u, dst_s):
        is_c, slab, _, _ = _unit_operands(u)
        k_ref = kc_ref if is_c else kd_ref
        r0 = pl.multiple_of(tile * tk, tk)
        dst_s[u] = _dot_nt(k_ref[0, pl.ds(r0, tk), slab * LANES:(slab + 1) * LANES], qm_s[u])

    def consume(tile, u, src_s):
        _, _, row0, dv = _unit_operands(u)
        rows = dv + ONES_ROWS
        s = src_s[u]
        m_old = m_s[u]
        m_new = jnp.maximum(m_old, jnp.max(s, axis=0, keepdims=True))
        alpha = jnp.exp2(m_old - m_new)
        p = jnp.exp2(s - m_new).astype(BF16)
        pv = _dot(vt_ref[0, tile, row0:row0 + rows, :], p)
        acc_s[u, 0:rows, :] = acc_s[u, 0:rows, :] * alpha + pv
        m_s[u] = m_new

    def stage(tile, src_s, dst_s):
        nxt = jnp.minimum(tile + 1, nkt - 1)
        for u in range(N_UNITS):
            scores(nxt, u, dst_s)
            consume(tile, u, src_s)

    for u in range(N_UNITS):
        scores(0, u, s_a)
    stage(0, s_a, s_b)

    def body(pr, carry):
        stage(2 * pr + 1, s_b, s_a)
        stage(2 * pr + 2, s_a, s_b)
        return carry

    lax.fori_loop(0, (nkt - 1) // 2, body, 0)
    if (nkt - 1) % 2 == 1:
        stage(nkt - 1, s_b, s_a)

    lv = lam_ref[...]
    lam = (jnp.exp(jnp.sum(lv[0:1] * lv[1:2], axis=-1, keepdims=True))
           - jnp.exp(jnp.sum(lv[2:3] * lv[3:4], axis=-1, keepdims=True)) + lam_init)
    sg = sg_ref[...]
    dvc = 2 * HEAD_DIM
    for hc in range(N_C_HEADS):
        o1 = acc_s[2 * hc, 0:dvc, :] / acc_s[2 * hc, dvc:dvc + 1, :]
        o2 = acc_s[2 * hc + 1, 0:dvc, :] / acc_s[2 * hc + 1, dvc:dvc + 1, :]
        o = o1 - lam * o2
        ms = jnp.mean(o * o, axis=0, keepdims=True)
        o = o * lax.rsqrt(ms + RMS_EPS) * sg * (1.0 - lam_init)
        o_ref[0, :, hc * LANES:(hc + 1) * LANES] = o.T.astype(BF16)
    for pair in range(N_D_HEADS // 2):
        outs = []
        for half in range(2):
            u = 2 * N_C_HEADS + 2 * pair + half
            outs.append(acc_s[u, 0:HEAD_DIM, :] / acc_s[u, HEAD_DIM:HEAD_DIM + 1, :])
        o2 = jnp.concatenate(outs, axis=0)
        c0 = N_C_HEADS * LANES + pair * LANES
        o_ref[0, :, c0:c0 + LANES] = o2.T.astype(BF16)


def _attn_cd(lam_vec, sg, qc, qd, kc, kd, vt, nct, lam_init):
    b, tall, _ = qc.shape
    nkt, nv, tk = vt.shape[1], vt.shape[2], vt.shape[3]
    mq = TM
    nq = tall // mq - nct
    ow = N_C_HEADS * LANES + N_D_HEADS * HEAD_DIM
    qtile = lambda bi, i: (bi, i + nct, 0)
    return pl.pallas_call(
        functools.partial(_attn_cd_kernel, nkt, lam_init),
        grid=(b, nq),
        in_specs=[pl.BlockSpec((4, HEAD_DIM), lambda bi, i: (0, 0)),
                  pl.BlockSpec((LANES, 1), lambda bi, i: (0, 0)),
                  pl.BlockSpec((1, mq, qc.shape[-1]), qtile),
                  pl.BlockSpec((1, mq, qd.shape[-1]), qtile),
                  pl.BlockSpec((1, tall, kc.shape[-1]), lambda bi, i: (bi, 0, 0)),
                  pl.BlockSpec((1, tall, kd.shape[-1]), lambda bi, i: (bi, 0, 0)),
                  pl.BlockSpec((1, nkt, nv, tk), lambda bi, i: (bi, 0, 0, 0))],
        out_specs=pl.BlockSpec((1, mq, ow), lambda bi, i: (bi, i, 0)),
        out_shape=jax.ShapeDtypeStruct((b, nq * mq, ow), BF16),
        scratch_shapes=[pltpu.VMEM((N_UNITS, mq, LANES), BF16),
                        pltpu.VMEM((N_UNITS, 1, mq), F32),
                        pltpu.VMEM((N_UNITS, 2 * HEAD_DIM + ONES_ROWS, mq), F32),
                        pltpu.VMEM((N_UNITS, tk, mq), F32),
                        pltpu.VMEM((N_UNITS, tk, mq), F32)],
        compiler_params=_cparams(("parallel", "parallel")),
        name="attention_cd",
    )(lam_vec, sg, qc, qd, kc, kd, vt)


def _oproj_kernel(n_parts, widths, nct, alpha, n_exp, *refs):
    o_refs = refs[:n_parts]
    (w_ref, x_ref, mlat_ref, mctx_ref, lng_ref, lnb_ref, rwt_ref, rb_ref,
     x1_ref, h2_ref, te_ref, tg_ref) = refs[n_parts:]
    d = x_ref.shape[-1]
    is_ctx = pl.program_id(1) < nct
    y = None
    r0 = 0
    for o_ref, wd in zip(o_refs, widths):
        part = _dot(o_ref[0], w_ref[r0:r0 + wd, :])
        y = part if y is None else y + part
        r0 += wd
    m = jnp.where(is_ctx, mctx_ref[...], mlat_ref[0])
    x1 = _layernorm(alpha * x_ref[0] + m[:, 2 * d:3 * d] * y, lng_ref[...], lnb_ref[...])
    x1_ref[0] = x1
    h2 = x1 * (1.0 + m[:, 4 * d:5 * d]) + m[:, 3 * d:4 * d]
    h2_ref[0] = h2
    logits = _dot_nt(rwt_ref[...], h2.astype(BF16)) + rb_ref[...]
    tm = logits.shape[1]
    row = lax.broadcasted_iota(jnp.int32, logits.shape, 0)
    vals, idxs = [], []
    for _ in range(TOP_K):
        mx = jnp.max(logits, axis=0, keepdims=True)
        idx = jnp.min(jnp.where(logits == mx, row, n_exp), axis=0, keepdims=True)
        vals.append(mx)
        idxs.append(idx)
        logits = jnp.where(row == idx, NEG, logits)
    es = [jnp.exp(v - vals[0]) for v in vals]
    tot = es[0] + es[1] + es[2] + es[3]
    pad_i = jnp.zeros((8 - TOP_K, tm), jnp.int32)
    pad_f = jnp.zeros((8 - TOP_K, tm), F32)
    te_ref[0] = jnp.concatenate(idxs + [pad_i], axis=0)
    tg_ref[0] = jnp.concatenate([e / tot for e in es] + [pad_f], axis=0)


def _oproj(parts, w_out, x_all, mlat, mctx, lng, lnb, rwt, rb, nct_out, x_tile_off, alpha):
    b, t, _ = parts[0].shape
    d = x_all.shape[-1]
    nt = t // TM
    n_exp = rwt.shape[0]
    widths = tuple(p.shape[-1] for p in parts)
    tile = lambda bi, i: (bi, i, 0)
    full = lambda bi, i: (0, 0)
    in_specs = [pl.BlockSpec((1, TM, wd), tile) for wd in widths] + [
        pl.BlockSpec(w_out.shape, full),
        pl.BlockSpec((1, TM, d), lambda bi, i: (bi, i + x_tile_off, 0)),
        pl.BlockSpec((1, 1, N_MOD * d), lambda bi, i: (bi, 0, 0)),
        pl.BlockSpec((1, N_MOD * d), full),
        pl.BlockSpec((1, d), full),
        pl.BlockSpec((1, d), full),
        pl.BlockSpec((n_exp, d), full),
        pl.BlockSpec((n_exp, 1), full)]
    return pl.pallas_call(
        functools.partial(_oproj_kernel, len(parts), widths, nct_out, alpha, n_exp),
        grid=(b, nt),
        in_specs=in_specs,
        out_specs=[pl.BlockSpec((1, TM, d), tile),
                   pl.BlockSpec((1, TM, d), tile),
                   pl.BlockSpec((1, 8, TM), lambda bi, i: (bi, 0, i)),
                   pl.BlockSpec((1, 8, TM), lambda bi, i: (bi, 0, i))],
        out_shape=[jax.ShapeDtypeStruct((b, t, d), F32),
                   jax.ShapeDtypeStruct((b, t, d), F32),
                   jax.ShapeDtypeStruct((b, 8, t), jnp.int32),
                   jax.ShapeDtypeStruct((b, 8, t), F32)],
        compiler_params=_cparams(("parallel", "parallel")),
        name="out_proj_norm_router",
    )(*parts, w_out, x_all, mlat, mctx, lng, lnb, rwt, rb)


def _moe_kernel(be_ref, nx_ref, nu_ref, x_ref, wgu_hbm, bgu_ref, wd_hbm, bd_ref, y_ref,
                wgu_f, wd_f, wgu_s, wd_s, sem):
    blk = pl.program_id(0)
    ff = wd_s.shape[0]

    def fetch(e):
        return (pltpu.make_async_copy(wgu_hbm.at[e], wgu_f, sem.at[0]),
                pltpu.make_async_copy(wd_hbm.at[e], wd_f, sem.at[1]))

    @pl.when(blk < nu_ref[0])
    def _():
        e = be_ref[blk]
        new_expert = (blk == 0) | (e != be_ref[jnp.maximum(blk - 1, 0)])

        @pl.when(blk == 0)
        def _():
            for cp in fetch(e):
                cp.start()

        @pl.when(new_expert)
        def _():
            for cp in fetch(e):
                cp.wait()
            rows = 128

            def cast_gu(r, c):
                r0 = pl.multiple_of(r * rows, rows)
                wgu_s[pl.ds(r0, rows), :] = wgu_f[pl.ds(r0, rows), :].astype(BF16)
                return c

            def cast_d(r, c):
                r0 = pl.multiple_of(r * rows, rows)
                wd_s[pl.ds(r0, rows), :] = wd_f[pl.ds(r0, rows), :].astype(BF16)
                return c

            lax.fori_loop(0, wgu_s.shape[0] // rows, cast_gu, 0)
            lax.fori_loop(0, wd_s.shape[0] // rows, cast_d, 0)

            @pl.when(nx_ref[blk] != e)
            def _():
                for cp in fetch(nx_ref[blk]):
                    cp.start()

        gu = _dot(x_ref[...].astype(BF16), wgu_s[...]) + bgu_ref[0]
        g = jnp.minimum(gu[:, :ff], SWIGLU_LIMIT)
        u = jnp.clip(gu[:, ff:], -SWIGLU_LIMIT, SWIGLU_LIMIT)
        act = (u + 1.0) * (g / (1.0 + jnp.exp(-SWIGLU_ALPHA * g)))
        y_ref[...] = _dot(act.astype(BF16), wd_s[...]) + bd_ref[0]

    @pl.when(blk >= nu_ref[0])
    def _():
        y_ref[...] = jnp.zeros(y_ref.shape, F32)


def _moe_experts(block_e, block_next_e, n_used, xb, w_gu, b_gu, w_down, b_down):
    n_slots, d = xb.shape
    ff2 = w_gu.shape[-1]
    ff = w_down.shape[-2]
    n_exp = w_gu.shape[0] * w_gu.shape[1]
    w_gu = w_gu.reshape(n_exp, d, ff2)
    w_down = w_down.reshape(n_exp, ff, d)
    nb = n_slots // MOE_BM
    xblk = lambda i, be, nx, nu: (jnp.minimum(i, nu[0] - 1), 0)
    bblk = lambda i, be, nx, nu: (be[i], 0, 0)
    grid_spec = pltpu.PrefetchScalarGridSpec(
        num_scalar_prefetch=3,
        grid=(nb,),
        in_specs=[pl.BlockSpec((MOE_BM, d), xblk),
                  pl.BlockSpec(memory_space=pl.ANY),
                  pl.BlockSpec((1, 1, ff2), bblk),
                  pl.BlockSpec(memory_space=pl.ANY),
                  pl.BlockSpec((1, 1, d), bblk)],
        out_specs=pl.BlockSpec((MOE_BM, d), lambda i, be, nx, nu: (i, 0)),
        scratch_shapes=[pltpu.VMEM((d, ff2), F32), pltpu.VMEM((ff, d), F32),
                        pltpu.VMEM((d, ff2), BF16), pltpu.VMEM((ff, d), BF16),
                        pltpu.SemaphoreType.DMA((2,))])
    return pl.pallas_call(
        _moe_kernel,
        grid_spec=grid_spec,
        out_shape=jax.ShapeDtypeStruct((n_slots, d), F32),
        compiler_params=_cparams(("arbitrary",)),
        name="moe_experts",
    )(block_e, block_next_e, n_used, xb, w_gu, b_gu.reshape(n_exp, 1, ff2), w_down, b_down.reshape(n_exp, 1, d))


def _expert_onehot(te, n_exp):
    row = lax.broadcasted_iota(jnp.int32, (n_exp, te.shape[1]), 0)
    oh = jnp.zeros(row.shape, F32)
    for k in range(TOP_K):
        oh = oh + (row == te[k:k + 1, :]).astype(F32)
    return row, oh


def _groups(cnt):
    return jnp.floor((cnt + (ROW_GROUP - 1)) / ROW_GROUP)


def _route_count_kernel(n_exp, te_ref, n8_ref):
    _, oh = _expert_onehot(te_ref[0], n_exp)
    n8 = _groups(jnp.sum(oh, axis=1, keepdims=True))
    n8_ref[0] = jnp.broadcast_to(n8, (n_exp, LANES)).astype(jnp.int32)


def _route_pos_kernel(n_exp, te_ref, g0_ref, tri_ref, ltri_ref, pos_ref, tab_ref):
    te = te_ref[0]
    tm = te.shape[1]
    row, oh = _expert_onehot(te, n_exp)
    n8 = _groups(jnp.sum(oh, axis=1, keepdims=True))
    lbase = _dot(ltri_ref[...], jnp.broadcast_to(n8, (n_exp, LANES)).astype(BF16))[:, 0:1]
    before = _dot(oh.astype(BF16), tri_ref[...])
    where_to = ROW_GROUP * lbase + before
    pos = [jnp.sum(jnp.where(row == te[k:k + 1, :], where_to, 0.0), axis=0, keepdims=True)
           for k in range(TOP_K)]
    pad = jnp.zeros((8 - TOP_K, tm), F32)
    pos_ref[0] = jnp.concatenate(pos + [pad], axis=0).astype(jnp.int32)
    g = lax.broadcasted_iota(jnp.int32, (n_exp, tm), 1).astype(F32)
    inside = (g >= lbase) & (g < lbase + n8)
    gdst = jnp.sum(jnp.where(inside, g0_ref[0] + g - lbase, 0.0), axis=0, keepdims=True)
    ngrp = jnp.broadcast_to(jnp.sum(n8, axis=0, keepdims=True), (1, tm))
    tab_ref[0] = jnp.concatenate([gdst, ngrp, jnp.zeros((6, tm), F32)], axis=0).astype(jnp.int32)


def _route(te, n_exp):
    b, _, t = te.shape
    nt = t // TM
    n_tiles = b * nt
    assert TOP_K * TM // ROW_GROUP + n_exp <= TM
    tile = pl.BlockSpec((1, 8, TM), lambda bi, i: (bi, 0, i))
    per_tile = lambda bi, i: (bi * nt + i, 0, 0)
    n8 = pl.pallas_call(
        functools.partial(_route_count_kernel, n_exp),
        grid=(b, nt),
        in_specs=[tile],
        out_specs=pl.BlockSpec((1, n_exp, LANES), per_tile),
        out_shape=jax.ShapeDtypeStruct((n_tiles, n_exp, LANES), jnp.int32),
        compiler_params=_cparams(("parallel", "parallel")),
        name="route_count",
    )(te)[:, :, 0]
    rows = ROW_GROUP * jnp.sum(n8, axis=0)
    padded = (rows + MOE_BM - 1) // MOE_BM * MOE_BM
    pend = jnp.cumsum(padded)
    g0 = (pend - padded)[None, :] // ROW_GROUP + jnp.cumsum(n8, axis=0) - n8
    tri = jnp.asarray(np.triu(np.ones((TM, TM)), 1), BF16)
    ltri = jnp.asarray(np.tril(np.ones((n_exp, n_exp)), -1), BF16)
    pos, tab = pl.pallas_call(
        functools.partial(_route_pos_kernel, n_exp),
        grid=(b, nt),
        in_specs=[tile,
                  pl.BlockSpec((1, n_exp, 1), per_tile),
                  pl.BlockSpec((TM, TM), lambda bi, i: (0, 0)),
                  pl.BlockSpec((n_exp, n_exp), lambda bi, i: (0, 0))],
        out_specs=[tile, pl.BlockSpec((1, 8, TM), per_tile)],
        out_shape=[jax.ShapeDtypeStruct((b, 8, t), jnp.int32),
                   jax.ShapeDtypeStruct((n_tiles, 8, TM), jnp.int32)],
        compiler_params=_cparams(("parallel", "parallel")),
        name="route_positions",
    )(te, g0.astype(F32).reshape(n_tiles, n_exp, 1), tri, ltri)
    gdst = jnp.pad(tab[:, 0, :], ((0, 0), (0, SMEM_BLOCK - TM))).reshape(-1)
    ngrp = tab[:, 1, 0]
    max_rows = b * t * TOP_K + n_tiles * n_exp * (ROW_GROUP - 1)
    n_blocks = -(-max_rows // MOE_BM) + n_exp
    n_used = (pend[-1] // MOE_BM).astype(jnp.int32)
    blk = jnp.minimum(jnp.arange(n_blocks, dtype=jnp.int32), n_used - 1) * MOE_BM
    block_e = jnp.minimum(jnp.sum((pend[None, :] <= blk[:, None]).astype(jnp.int32), axis=1), n_exp - 1)
    ids = jnp.arange(n_exp, dtype=jnp.int32)
    later = (padded[None, :] > 0) & (ids[None, :] > block_e[:, None])
    block_next_e = jnp.min(jnp.where(later, ids[None, :], n_exp), axis=1)
    block_next_e = jnp.where(block_next_e == n_exp, block_e, block_next_e).astype(jnp.int32)
    return pos, gdst, ngrp, pend.astype(jnp.int32), block_e, block_next_e, n_used.reshape(1), n_blocks


def _group_copy(src_ref, src_group, dst_ref, dst_group, sem):
    s0 = pl.multiple_of(src_group * ROW_GROUP, ROW_GROUP)
    d0 = pl.multiple_of(dst_group * ROW_GROUP, ROW_GROUP)
    return pltpu.make_async_copy(src_ref.at[pl.ds(s0, ROW_GROUP)], dst_ref.at[pl.ds(d0, ROW_GROUP)], sem)


def _tile_step():
    return pl.program_id(0) * pl.num_programs(1) + pl.program_id(1)


def _moe_scatter_kernel(n_exp, pend_ref, ngrp_ref, gdst_ref, gdst_prev_ref, pos_ref, h_ref, xb_ref,
                        xs, zero_s, sem, zsem):
    step = _tile_step()
    tm = h_ref.shape[1]

    @pl.when(step == 0)
    def _():
        zero_s[...] = jnp.zeros(zero_s.shape, zero_s.dtype)
        for phase in ("start", "wait"):
            for e in range(n_exp):
                lo = pend_ref[e - 1] if e else 0

                @pl.when(pend_ref[e] > lo)
                def _():
                    r0 = pl.multiple_of(pend_ref[e] - MOE_BM, MOE_BM)
                    cp = pltpu.make_async_copy(zero_s, xb_ref.at[pl.ds(r0, MOE_BM)], zsem)
                    cp.start() if phase == "start" else cp.wait()

            def unused_block(blk, c):
                r0 = pl.multiple_of(blk * MOE_BM, MOE_BM)
                cp = pltpu.make_async_copy(zero_s, xb_ref.at[pl.ds(r0, MOE_BM)], zsem)
                cp.start() if phase == "start" else cp.wait()
                return c

            lax.fori_loop(pend_ref[n_exp - 1] // MOE_BM, xb_ref.shape[0] // MOE_BM, unused_block, 0)

    par = step % 2
    pos = pos_ref[0]
    hb = h_ref[0].astype(BF16)
    for c in range(xs.shape[1] // tm):
        r = lax.broadcasted_iota(jnp.int32, (tm, tm), 0) + c * tm
        hit = r == pos[0:1, :]
        for k in range(1, TOP_K):
            hit = hit | (r == pos[k:k + 1, :])
        xs[par, c * tm:(c + 1) * tm, :] = _dot(jnp.where(hit, 1.0, 0.0).astype(BF16), hb)

    def groups(table, buf, count, wait):
        def body(g, c):
            cp = _group_copy(xs.at[buf], g, xb_ref, table[g], sem.at[buf])
            cp.wait() if wait else cp.start()
            return c
        lax.fori_loop(0, count, body, 0)

    groups(gdst_ref, par, ngrp_ref[step], False)

    @pl.when(step > 0)
    def _():
        groups(gdst_prev_ref, 1 - par, ngrp_ref[jnp.maximum(step - 1, 0)], True)

    @pl.when(step == pl.num_programs(0) * pl.num_programs(1) - 1)
    def _():
        groups(gdst_ref, par, ngrp_ref[step], True)


def _sort_rows(n_exp):
    return -(-(TOP_K * TM + n_exp * ROW_GROUP) // TM) * TM


def _moe_scatter(pend, ngrp, gdst, pos, h2, n_slots):
    b, t, d = h2.shape
    nt = t // TM
    n_exp = pend.shape[0]
    grid_spec = pltpu.PrefetchScalarGridSpec(
        num_scalar_prefetch=2,
        grid=(b, nt),
        in_specs=[pl.BlockSpec((SMEM_BLOCK,), lambda bi, i, pe, ng: (bi * nt + i,), memory_space=pltpu.SMEM),
                  pl.BlockSpec((SMEM_BLOCK,), lambda bi, i, pe, ng: (jnp.maximum(bi * nt + i - 1, 0),),
                               memory_space=pltpu.SMEM),
                  pl.BlockSpec((1, 8, TM), lambda bi, i, pe, ng: (bi, 0, i)),
                  pl.BlockSpec((1, TM, d), lambda bi, i, pe, ng: (bi, i, 0))],
        out_specs=pl.BlockSpec(memory_space=pl.ANY),
        scratch_shapes=[pltpu.VMEM((2, _sort_rows(n_exp), d), F32), pltpu.VMEM((MOE_BM, d), F32),
                        pltpu.SemaphoreType.DMA((2,)), pltpu.SemaphoreType.DMA])
    return pl.pallas_call(
        functools.partial(_moe_scatter_kernel, n_exp),
        grid_spec=grid_spec,
        out_shape=jax.ShapeDtypeStruct((n_slots, d), F32),
        compiler_params=_cparams(("arbitrary", "arbitrary")),
        name="moe_scatter_rows",
    )(pend, ngrp, gdst, gdst, pos, h2)


def _combine_kernel(nct, alpha, ngrp_ref, gdst_ref, gdst_next_ref, yb_ref, pos_ref, tg_ref, x1_ref,
                    mlat_ref, mctx_ref, lng_ref, lnb_ref, o_ref, ys, sem):
    d = x1_ref.shape[-1]
    tm = x1_ref.shape[1]
    step = _tile_step()
    n_steps = pl.num_programs(0) * pl.num_programs(1)
    par = step % 2

    def groups(table, buf, count, wait):
        def body(g, c):
            cp = _group_copy(yb_ref, table[g], ys.at[buf], g, sem.at[buf])
            cp.wait() if wait else cp.start()
            return c
        lax.fori_loop(0, count, body, 0)

    @pl.when(step == 0)
    def _():
        ys[...] = jnp.zeros(ys.shape, F32)
        groups(gdst_ref, 0, ngrp_ref[0], False)

    @pl.when(step + 1 < n_steps)
    def _():
        groups(gdst_next_ref, 1 - par, ngrp_ref[jnp.minimum(step + 1, n_steps - 1)], False)

    groups(gdst_ref, par, ngrp_ref[step], True)
    is_ctx = pl.program_id(1) < nct
    m = jnp.where(is_ctx, mctx_ref[...], mlat_ref[0])
    pos = pos_ref[0]
    gates = tg_ref[0]
    f = None
    for c in range(ys.shape[1] // tm):
        col = lax.broadcasted_iota(jnp.int32, (tm, tm), 1) + c * tm
        w = jnp.where(col == pos[:, 0:1], gates[:, 0:1], 0.0)
        for k in range(1, TOP_K):
            w = w + jnp.where(col == pos[:, k:k + 1], gates[:, k:k + 1], 0.0)
        part = _dot(w.astype(BF16), ys[par, c * tm:(c + 1) * tm, :].astype(BF16))
        f = part if f is None else f + part
    o_ref[0] = _layernorm(alpha * x1_ref[0] + m[:, 5 * d:6 * d] * f, lng_ref[...], lnb_ref[...])


def _combine_postnorm(ngrp, gdst, yb, pos_t, tg_t, x1, mlat, mctx, lng, lnb, nct, alpha, n_exp):
    b, t, d = x1.shape
    nt = t // TM
    tile = lambda bi, i, ng: (bi, i, 0)
    full = lambda bi, i, ng: (0, 0)
    grid_spec = pltpu.PrefetchScalarGridSpec(
        num_scalar_prefetch=1,
        grid=(b, nt),
        in_specs=[pl.BlockSpec((SMEM_BLOCK,), lambda bi, i, ng: (bi * nt + i,), memory_space=pltpu.SMEM),
                  pl.BlockSpec((SMEM_BLOCK,), lambda bi, i, ng: (jnp.minimum(bi * nt + i + 1, b * nt - 1),),
                               memory_space=pltpu.SMEM),
                  pl.BlockSpec(memory_space=pl.ANY),
                  pl.BlockSpec((1, TM, 8), tile),
                  pl.BlockSpec((1, TM, 8), tile),
                  pl.BlockSpec((1, TM, d), tile),
                  pl.BlockSpec((1, 1, N_MOD * d), lambda bi, i, ng: (bi, 0, 0)),
                  pl.BlockSpec((1, N_MOD * d), full),
                  pl.BlockSpec((1, d), full),
                  pl.BlockSpec((1, d), full)],
        out_specs=pl.BlockSpec((1, TM, d), tile),
        scratch_shapes=[pltpu.VMEM((2, _sort_rows(n_exp), d), F32), pltpu.SemaphoreType.DMA((2,))])
    return pl.pallas_call(
        functools.partial(_combine_kernel, nct, alpha),
        grid_spec=grid_spec,
        out_shape=jax.ShapeDtypeStruct((b, t, d), F32),
        compiler_params=_cparams(("arbitrary", "arbitrary")),
        name="combine_post_norm",
    )(ngrp, gdst, gdst, yb, pos_t, tg_t, x1, mlat, mctx, lng, lnb)


def _rope_tables(ctx_len, seq):
    t = np.arange(seq)
    row = (t // GRID_W).astype(np.float64)
    col = (t % GRID_W).astype(np.float64)
    nf = HEAD_DIM // 4
    inv = ROPE_THETA ** (-np.arange(nf, dtype=np.float64) / nf)
    ar = row[:, None] * inv[None, :]
    ac = col[:, None] * inv[None, :]
    ang = np.concatenate([ar, ar, ac, ac], axis=-1)
    cos = np.concatenate([np.ones((ctx_len, HEAD_DIM)), np.cos(ang)], axis=0)
    sin = np.concatenate([np.zeros((ctx_len, HEAD_DIM)), np.sin(ang)], axis=0)
    sign = np.where((np.arange(HEAD_DIM) % 32) < 16, -1.0, 1.0)[None, :]
    cos2 = np.tile(cos, (1, LANES // HEAD_DIM))
    sin2 = np.tile(sin * sign, (1, LANES // HEAD_DIM))
    return jnp.asarray(cos2, F32), jnp.asarray(sin2, F32)


def _channel_dft():
    c = np.arange(HEAD_DIM)
    ang = 2.0 * np.pi * ((c[:, None] * c[None, :]) % HEAD_DIM) / HEAD_DIM
    eye = np.eye(4)
    cs = np.concatenate([np.kron(eye, np.cos(ang)), np.kron(eye, np.sin(ang))], axis=1)
    return jnp.asarray(cs, BF16)


def _group_mean_matrix():
    bd = np.kron(np.eye(LANES // HEAD_DIM), np.full((HEAD_DIM, HEAD_DIM), 1.0 / HEAD_DIM))
    return jnp.asarray(bd, BF16)


def _dup_heads(wk, n_heads):
    d = wk.shape[0]
    return jnp.broadcast_to(wk.reshape(d, n_heads, 1, HEAD_DIM), (d, n_heads, 2, HEAD_DIM)).reshape(d, n_heads * LANES)


def _lambda_init(layer):
    return 0.8 - 0.6 * math.exp(-0.3 * layer)


def _moe_postnorm(h2, te, tg, x1, mlat, mctx, lng, lnb, nct, alpha, layer, w_gu, b_gu, w_down, b_down):
    n_exp = w_gu.shape[1]
    pos, gdst, ngrp, pend, block_e, block_next_e, n_used, n_blocks = _route(te, n_exp)
    xb = _moe_scatter(pend, ngrp, gdst, pos, h2, n_blocks * MOE_BM)
    off = layer * n_exp
    yb = _moe_experts(block_e + off, block_next_e + off, n_used, xb, w_gu, b_gu, w_down, b_down)
    pos_t, tg_t = jnp.transpose(pos, (0, 2, 1)), jnp.transpose(tg, (0, 2, 1))
    return _combine_postnorm(ngrp, gdst, yb, pos_t, tg_t, x1, mlat, mctx, lng, lnb, nct, alpha, n_exp)


def kernel(x, c, ctx, c_ctx, mod_w, mod_b, ln_g, ln_b, ab_w_in, ab_sink, ab_w_out,
           cd_w_in, cd_lambda, cd_subln_g, cd_q_norm_g, cd_k_norm_g, cd_w_out,
           router_w, router_b, expert_w_gu, expert_b_gu, expert_w_down, expert_b_down):
    b, s, d = x.shape
    n_ctx = ctx.shape[1]
    depth = mod_w.shape[0]
    n_exp = router_w.shape[-1]
    assert d == 16 * HEAD_DIM and n_ctx % TM == 0 and s % TM == 0 and s % GRID_W == 0
    nct = n_ctx // TM
    alpha = (2 * depth) ** 0.25

    cos, sin = _rope_tables(n_ctx, s)
    cs_dft = _channel_dft()
    bd = _group_mean_matrix()
    mods = _mod_vectors(c, c_ctx, mod_w, mod_b)

    x_all = jnp.concatenate([ctx, x], axis=1)
    for l in range(depth):
        last = l == depth - 1
        i = l // 2
        mlat = mods[l, :b].reshape(b, 1, N_MOD * d)
        mctx = mods[l, b:b + 1]
        lng1, lnb1 = ln_g[l, 0].reshape(1, d), ln_b[l, 0].reshape(1, d)
        lng2, lnb2 = ln_g[l, 1].reshape(1, d), ln_b[l, 1].reshape(1, d)
        rwt = router_w[l].T.astype(BF16)
        rb = router_b[l].reshape(n_exp, 1)
        if l % 2 == 0:
            w = ab_w_in[i]
            w_n = jnp.concatenate([w[:, :1024], _dup_heads(w[:, 1024:1280], 4)], axis=1).astype(BF16)
            wvt = w[:, 1280:1536].T.astype(BF16)
            uw, q, k, vt = _proj_ab(x_all, mlat, mctx, w_n, wvt, cos, sin, cs_dft, nct)
            oa = jnp.concatenate([_fourier(uw[:, :n_ctx]), _fourier(uw[:, n_ctx:])], axis=1)
            ob = _win_attn(ab_sink[i], q, k, vt, nct)
            if last:
                parts = [oa[:, n_ctx:], ob[:, n_ctx:]]
            else:
                parts = [oa, ob]
            w_out = ab_w_out[i].astype(BF16)
        else:
            w = cd_w_in[i]
            w_n = jnp.concatenate([w[:, :1536], _dup_heads(w[:, 1536:1664], 2)], axis=1).astype(BF16)
            wvt = w[:, 1664:2304].T.astype(BF16)
            gq = jnp.tile(cd_q_norm_g[i], 2).reshape(1, LANES)
            gk = jnp.tile(cd_k_norm_g[i], 2).reshape(1, LANES)
            qc, qd, kc, kd, vt = _proj_cd(x_all, mlat, mctx, w_n, wvt, cos, sin, bd, gq, gk, nct)
            sg = cd_subln_g[i].reshape(LANES, 1)
            o_lat = _attn_cd(cd_lambda[i], sg, qc, qd, kc, kd, vt, nct, _lambda_init(l))
            if last:
                parts = [o_lat]
            else:
                raise NotImplementedError("context outputs of a differential/axial layer")
            w_out = cd_w_out[i].astype(BF16)
        experts = (l, expert_w_gu, expert_b_gu, expert_w_down, expert_b_down)
        if last:
            x1, h2, te, tg = _oproj(parts, w_out, x_all, mlat, mctx, lng1, lnb1, rwt, rb, 0, nct, alpha)
            return _moe_postnorm(h2, te, tg, x1, mlat, mctx, lng2, lnb2, 0, alpha, *experts)
        x1, h2, te, tg = _oproj(parts, w_out, x_all, mlat, mctx, lng1, lnb1, rwt, rb, nct, 0, alpha)
        x_all = _moe_postnorm(h2, te, tg, x1, mlat, mctx, lng2, lnb2, nct, alpha, *experts)
    return x_all[:, n_ctx:]
```

```python
import functools
import math

import jax
import jax.numpy as jnp
import numpy as np
from jax import lax
from jax.experimental import pallas as pl
from jax.experimental.pallas import tpu as pltpu

F32 = jnp.float32
BF16 = jnp.bfloat16

HEAD_DIM = 64
GRID_W = 64
WINDOW = 128
ROPE_THETA = 10000.0
LN_EPS = 1e-6
RMS_EPS = 1e-6
N_MOD = 6
TOP_K = 4
SWIGLU_LIMIT = 7.0
SWIGLU_ALPHA = 1.702
NEG = -1e30
LOG2E = 1.4426950408889634
QSCALE = HEAD_DIM ** -0.5 * LOG2E

N_C_HEADS = 4
N_D_HEADS = 8
N_D_KV = 2
N_UNITS = 2 * N_C_HEADS + N_D_HEADS
ONES_ROWS = 16
_VT_LAYOUT = tuple((h * (2 * HEAD_DIM + ONES_ROWS), 2 * HEAD_DIM) for h in range(N_C_HEADS)) + tuple(
    (N_C_HEADS * (2 * HEAD_DIM + ONES_ROWS) + g * (HEAD_DIM + ONES_ROWS), HEAD_DIM) for g in range(N_D_KV))
VT_ROWS = _VT_LAYOUT[-1][0] + HEAD_DIM + ONES_ROWS

LANES = 128
TM = 256
MOE_BM = 512
ROW_GROUP = 8
SMEM_BLOCK = 1024
VMEM_LIMIT = 56 * 1024 * 1024


def _dot(a, b):
    return jnp.dot(a, b, preferred_element_type=F32)


def _dot_nt(a, b):
    return lax.dot_general(a, b, (((1,), (1,)), ((), ())), preferred_element_type=F32)


def _cparams(sem):
    return pltpu.CompilerParams(dimension_semantics=sem, vmem_limit_bytes=VMEM_LIMIT)


def _layernorm(z, g, b):
    mu = jnp.mean(z, axis=-1, keepdims=True)
    d = z - mu
    var = jnp.mean(d * d, axis=-1, keepdims=True)
    return d * lax.rsqrt(var + LN_EPS) * g + b


def _rope128(x, cos, sin_signed):
    lane = lax.broadcasted_iota(jnp.int32, x.shape, 1)
    lo = (lane % 32) < 16
    rot = jnp.where(lo, pltpu.roll(x, LANES - 16, 1), pltpu.roll(x, 16, 1))
    return x * cos + rot * sin_signed


def _rmsnorm128(x, g, bd):
    x2 = x * x
    hi = x2.astype(BF16)
    lo = (x2 - hi.astype(F32)).astype(BF16)
    ms = _dot(hi, bd) + _dot(lo, bd)
    return x * lax.rsqrt(ms + RMS_EPS) * g


def _modulated(x_ref, mlat_ref, mctx_ref, is_ctx, d):
    m = jnp.where(is_ctx, mctx_ref[...], mlat_ref[0])
    return (x_ref[0] * (1.0 + m[:, d:2 * d]) + m[:, 0:d]).astype(BF16)


def _mod_kernel(c_ref, w_ref, b_ref, o_ref):
    c = c_ref[...]
    s = (c / (1.0 + jnp.exp(-c))).astype(BF16)
    o_ref[0] = _dot(s, w_ref[0].astype(BF16)) + b_ref[0]


def _mod_vectors(c, c_ctx, mod_w, mod_b):
    depth, d, n = mod_w.shape
    b = c.shape[0]
    rows = 8 * (-(-(b + 1) // 8))
    cs = jnp.zeros((rows, d), F32).at[:b].set(c).at[b].set(c_ctx)
    tn = 1536
    out = pl.pallas_call(
        _mod_kernel,
        grid=(depth, n // tn),
        in_specs=[pl.BlockSpec((rows, d), lambda l, j: (0, 0)),
                  pl.BlockSpec((1, d, tn), lambda l, j: (l, 0, j)),
                  pl.BlockSpec((1, 1, tn), lambda l, j: (l, 0, j))],
        out_specs=pl.BlockSpec((1, rows, tn), lambda l, j: (l, 0, j)),
        out_shape=jax.ShapeDtypeStruct((depth, rows, n), F32),
        compiler_params=_cparams(("parallel", "parallel")),
        name="mod_vectors",
    )(cs, mod_w, mod_b.reshape(depth, 1, n))
    return out


def _proj_ab_kernel(nct, x_ref, mlat_ref, mctx_ref, w_ref, wvt_ref, cos_ref, sin_ref, cs_ref,
                    uw_ref, q_ref, k_ref, vt_ref):
    is_ctx = pl.program_id(1) < nct
    d = x_ref.shape[-1]
    h = _modulated(x_ref, mlat_ref, mctx_ref, is_ctx, d)
    p = _dot(h, w_ref[...])
    uw_ref[0] = _dot(p[:, 0:256].astype(BF16), cs_ref[...]).astype(BF16)
    cos = cos_ref[...]
    sin = sin_ref[...]
    for j in range(6):
        c0 = 256 + j * LANES
        q_ref[0, :, j * LANES:(j + 1) * LANES] = (_rope128(p[:, c0:c0 + LANES], cos, sin) * QSCALE).astype(BF16)
    for j in range(4):
        c0 = 1024 + j * LANES
        k_ref[0, :, j * LANES:(j + 1) * LANES] = _rope128(p[:, c0:c0 + LANES], cos, sin).astype(BF16)
    vt = _dot_nt(wvt_ref[...], h).astype(BF16)
    for c in range(TM // LANES):
        vt_ref[0, c] = vt[:, c * LANES:(c + 1) * LANES]


def _proj_ab(x_all, mlat, mctx, w, wvt, cos, sin, cs, nct):
    b, tall, d = x_all.shape
    nt = tall // TM
    nw = w.shape[1]
    nchunk = tall // LANES
    cpt = TM // LANES
    return pl.pallas_call(
        functools.partial(_proj_ab_kernel, nct),
        grid=(b, nt),
        in_specs=[pl.BlockSpec((1, TM, d), lambda bi, i: (bi, i, 0)),
                  pl.BlockSpec((1, 1, N_MOD * d), lambda bi, i: (bi, 0, 0)),
                  pl.BlockSpec((1, N_MOD * d), lambda bi, i: (0, 0)),
                  pl.BlockSpec((d, nw), lambda bi, i: (0, 0)),
                  pl.BlockSpec((256, d), lambda bi, i: (0, 0)),
                  pl.BlockSpec((TM, LANES), lambda bi, i: (i, 0)),
                  pl.BlockSpec((TM, LANES), lambda bi, i: (i, 0)),
                  pl.BlockSpec((256, 512), lambda bi, i: (0, 0))],
        out_specs=[pl.BlockSpec((1, TM, 512), lambda bi, i: (bi, i, 0)),
                   pl.BlockSpec((1, TM, 768), lambda bi, i: (bi, i, 0)),
                   pl.BlockSpec((1, TM, 512), lambda bi, i: (bi, i, 0)),
                   pl.BlockSpec((1, cpt, 256, LANES), lambda bi, i: (bi, i, 0, 0))],
        out_shape=[jax.ShapeDtypeStruct((b, tall, 512), BF16),
                   jax.ShapeDtypeStruct((b, tall, 768), BF16),
                   jax.ShapeDtypeStruct((b, tall, 512), BF16),
                   jax.ShapeDtypeStruct((b, nchunk, 256, LANES), BF16)],
        compiler_params=_cparams(("parallel", "parallel")),
        name="proj_ab",
    )(x_all, mlat, mctx, w, wvt, cos, sin, cs)


def _proj_cd_kernel(nct, x_ref, mlat_ref, mctx_ref, w_ref, wvt_ref, cos_ref, sin_ref, bd_ref,
                    gq_ref, gk_ref, qc_ref, qd_ref, kc_ref, kd_ref, vt_ref):
    is_ctx = pl.program_id(1) < nct
    d = x_ref.shape[-1]
    h = _modulated(x_ref, mlat_ref, mctx_ref, is_ctx, d)
    p = _dot(h, w_ref[...])
    cos = cos_ref[...]
    sin = sin_ref[...]
    bd = bd_ref[...]
    gq = gq_ref[...]
    gk = gk_ref[...]
    for j in range(4):
        sl = slice(j * LANES, (j + 1) * LANES)
        qc_ref[0, :, sl] = (_rope128(p[:, j * LANES:(j + 1) * LANES], cos, sin) * QSCALE).astype(BF16)
        c0 = 512 + j * LANES
        qd_ref[0, :, sl] = (_rope128(_rmsnorm128(p[:, c0:c0 + LANES], gq, bd), cos, sin) * QSCALE).astype(BF16)
        c0 = 1024 + j * LANES
        kc_ref[0, :, sl] = _rope128(p[:, c0:c0 + LANES], cos, sin).astype(BF16)
    for j in range(2):
        c0 = 1536 + j * LANES
        kd_ref[0, :, j * LANES:(j + 1) * LANES] = _rope128(
            _rmsnorm128(p[:, c0:c0 + LANES], gk, bd), cos, sin).astype(BF16)
    vt = _dot_nt(wvt_ref[...], h).astype(BF16)
    ones = jnp.ones((ONES_ROWS, vt.shape[1]), BF16)
    src = 0
    for dst, dv in _VT_LAYOUT:
        vt_ref[0, 0, dst:dst + dv, :] = vt[src:src + dv, :]
        vt_ref[0, 0, dst + dv:dst + dv + ONES_ROWS, :] = ones
        src += dv


def _proj_cd(x_all, mlat, mctx, w, wvt, cos, sin, bd, gq, gk, nct):
    b, tall, d = x_all.shape
    nt = tall // TM
    nw = w.shape[1]
    nv = VT_ROWS
    full = lambda bi, i: (0, 0)
    tile = lambda bi, i: (bi, i, 0)
    return pl.pallas_call(
        functools.partial(_proj_cd_kernel, nct),
        grid=(b, nt),
        in_specs=[pl.BlockSpec((1, TM, d), tile),
                  pl.BlockSpec((1, 1, N_MOD * d), lambda bi, i: (bi, 0, 0)),
                  pl.BlockSpec((1, N_MOD * d), full),
                  pl.BlockSpec((d, nw), full),
                  pl.BlockSpec(wvt.shape, full),
                  pl.BlockSpec((TM, LANES), lambda bi, i: (i, 0)),
                  pl.BlockSpec((TM, LANES), lambda bi, i: (i, 0)),
                  pl.BlockSpec((LANES, LANES), full),
                  pl.BlockSpec((1, LANES), full),
                  pl.BlockSpec((1, LANES), full)],
        out_specs=[pl.BlockSpec((1, TM, 512), tile),
                   pl.BlockSpec((1, TM, 512), tile),
                   pl.BlockSpec((1, TM, 512), tile),
                   pl.BlockSpec((1, TM, 256), tile),
                   pl.BlockSpec((1, 1, nv, TM), lambda bi, i: (bi, i, 0, 0))],
        out_shape=[jax.ShapeDtypeStruct((b, tall, 512), BF16),
                   jax.ShapeDtypeStruct((b, tall, 512), BF16),
                   jax.ShapeDtypeStruct((b, tall, 512), BF16),
                   jax.ShapeDtypeStruct((b, tall, 256), BF16),
                   jax.ShapeDtypeStruct((b, nt, nv, TM), BF16)],
        compiler_params=_cparams(("parallel", "parallel")),
        name="proj_cd",
    )(x_all, mlat, mctx, w, wvt, cos, sin, bd, gq, gk)


def _fourier_kernel(nb, scale, uw_ref, ca_ref, sa_ref, cb_ref, sb_ref, o_ref):
    j = pl.program_id(0)
    ca = ca_ref[pl.ds(j, 1), :]
    sa = sa_ref[pl.ds(j, 1), :]
    cb = cb_ref[...]
    sb = sb_ref[...]
    ct = (ca * cb - sa * sb).astype(BF16)
    nst = (-(sa * cb + ca * sb)).astype(BF16)
    for bi in range(nb):
        acc = _dot(ct, uw_ref[bi, :, 0:256]) + _dot(nst, uw_ref[bi, :, 256:512])
        o_ref[bi] = (acc * scale).astype(BF16)


def _dft_tables(t, tmf):
    k = np.arange(t, dtype=np.int64)
    j1 = np.arange(t // tmf, dtype=np.int64) * tmf
    j0 = np.arange(tmf, dtype=np.int64)
    aa = (2.0 * np.pi / t) * ((j1[:, None] * k[None, :]) % t)
    ab = (2.0 * np.pi / t) * ((j0[:, None] * k[None, :]) % t)
    f = lambda a: jnp.asarray(a, F32)
    return f(np.cos(aa)), f(np.sin(aa)), f(np.cos(ab)), f(np.sin(ab))


def _fourier(uw):
    b, t, _ = uw.shape
    tmf = min(128, t)
    ca, sa, cb, sb = _dft_tables(t, tmf)
    scale = 1.0 / math.sqrt(t * HEAD_DIM)
    full2 = lambda j: (0, 0)
    return pl.pallas_call(
        functools.partial(_fourier_kernel, b, scale),
        grid=(t // tmf,),
        in_specs=[pl.BlockSpec((b, t, 512), lambda j: (0, 0, 0)),
                  pl.BlockSpec((t // tmf, t), full2),
                  pl.BlockSpec((t // tmf, t), full2),
                  pl.BlockSpec((tmf, t), full2),
                  pl.BlockSpec((tmf, t), full2)],
        out_specs=pl.BlockSpec((b, tmf, 256), lambda j: (0, j, 0)),
        out_shape=jax.ShapeDtypeStruct((b, t, 256), BF16),
        compiler_params=_cparams(("parallel",)),
        name="fourier_mix",
    )(uw, ca, sa, cb, sb)


def _win_attn_kernel(nct, nchunk, sink_ref, q_ref, k_ref, vt_ref, o_ref, s_a, s_b):
    i = pl.program_id(1)
    is_ctx = i < nct
    j = i - nct
    ctx_rows = nct * TM
    ctx_chunks = ctx_rows // LANES
    lw = TM + 2 * WINDOW
    lchunks = lw // LANES
    cs = jnp.clip(ctx_chunks + (TM // LANES) * j - WINDOW // LANES, 0, nchunk - lchunks)
    rs = pl.multiple_of(cs * LANES, LANES)
    qpos = j * TM + lax.broadcasted_iota(jnp.int32, (1, TM), 1)
    kpos = cs * LANES - ctx_rows + lax.broadcasted_iota(jnp.int32, (lw, 1), 0)
    allowed = (jnp.abs(qpos - kpos) <= WINDOW) & (kpos >= 0) & jnp.logical_not(is_ctx)
    lane = lax.broadcasted_iota(jnp.int32, (TM, LANES), 1)
    n_heads = q_ref.shape[-1] // HEAD_DIM
    group = n_heads // (k_ref.shape[-1] // LANES)

    def scores(hq, dst_s):
        pair, half = divmod(hq, 2)
        g = hq // group
        qp = q_ref[0, :, pair * LANES:(pair + 1) * LANES]
        qm = jnp.where((lane >= HEAD_DIM) == (half == 1), qp, jnp.zeros_like(qp))
        dst_s[0:ctx_rows, :] = _dot_nt(k_ref[0, 0:ctx_rows, g * LANES:(g + 1) * LANES], qm)
        s_l = _dot_nt(k_ref[0, pl.ds(rs, lw), g * LANES:(g + 1) * LANES], qm)
        dst_s[ctx_rows:ctx_rows + lw, :] = jnp.where(allowed, s_l, NEG)

    def consume(hq, src_s):
        g = hq // group
        s = src_s[...]
        sk = sink_ref[hq] * LOG2E
        m = jnp.maximum(jnp.max(s, axis=0, keepdims=True), sk)
        p = jnp.exp2(s - m)
        l = jnp.sum(p, axis=0, keepdims=True) + jnp.exp2(sk - m)
        p = p.astype(BF16)
        vrows = slice(g * HEAD_DIM, (g + 1) * HEAD_DIM)
        acc = jnp.zeros((HEAD_DIM, TM), F32)
        for c in range(ctx_chunks):
            acc = acc + _dot(vt_ref[0, c, vrows, :], p[c * LANES:(c + 1) * LANES, :])
        for c in range(lchunks):
            r0 = ctx_rows + c * LANES
            acc = acc + _dot(vt_ref[0, cs + c, vrows, :], p[r0:r0 + LANES, :])
        return acc / l

    bufs = (s_a, s_b)
    scores(0, s_a)
    outs = []
    for hq in range(n_heads):
        if hq + 1 < n_heads:
            scores(hq + 1, bufs[(hq + 1) % 2])
        outs.append(consume(hq, bufs[hq % 2]))
        if hq % 2 == 1:
            pair = hq // 2
            o2 = jnp.concatenate(outs, axis=0)
            o_ref[0, :, pair * LANES:(pair + 1) * LANES] = o2.T.astype(BF16)
            outs = []


def _win_attn(sink, q, k, vt, nct):
    b, tall, qw = q.shape
    nt = tall // TM
    nchunk = vt.shape[1]
    return pl.pallas_call(
        functools.partial(_win_attn_kernel, nct, nchunk),
        grid=(b, nt),
        in_specs=[pl.BlockSpec(memory_space=pltpu.SMEM),
                  pl.BlockSpec((1, TM, qw), lambda bi, i: (bi, i, 0)),
                  pl.BlockSpec((1, tall, k.shape[-1]), lambda bi, i: (bi, 0, 0)),
                  pl.BlockSpec((1, nchunk, vt.shape[2], LANES), lambda bi, i: (bi, 0, 0, 0))],
        out_specs=pl.BlockSpec((1, TM, qw), lambda bi, i: (bi, i, 0)),
        out_shape=jax.ShapeDtypeStruct((b, tall, qw), BF16),
        scratch_shapes=[pltpu.VMEM((nct * TM + TM + 2 * WINDOW, TM), F32)] * 2,
        compiler_params=_cparams(("parallel", "parallel")),
        name="window_attention",
    )(sink, q, k, vt)


def _unit_operands(u):
    if u < 2 * N_C_HEADS:
        hc = u // 2
        dst, dv = _VT_LAYOUT[hc]
        return True, hc, dst, dv
    g = (u - 2 * N_C_HEADS) // (N_D_HEADS // N_D_KV)
    dst, dv = _VT_LAYOUT[N_C_HEADS + g]
    return False, g, dst, dv


def _attn_cd_kernel(nkt, lam_init, lam_ref, sg_ref, qc_ref, qd_ref, kc_ref, kd_ref, vt_ref, o_ref,
                    qm_s, m_s, acc_s, s_a, s_b):
    mq = qc_ref.shape[1]
    tk = vt_ref.shape[-1]
    lane = lax.broadcasted_iota(jnp.int32, (mq, LANES), 1)
    upper = lane >= HEAD_DIM
    for u in range(N_UNITS):
        src = qc_ref if u < 2 * N_C_HEADS else qd_ref
        pair = (u if u < 2 * N_C_HEADS else u - 2 * N_C_HEADS) // 2
        qp = src[0, :, pair * LANES:(pair + 1) * LANES]
        qm_s[u] = jnp.where(upper == (u % 2 == 1), qp, jnp.zeros_like(qp))
    m_s[...] = jnp.full(m_s.shape, NEG, F32)
    acc_s[...] = jnp.zeros(acc_s.shape, F32)

    def scores(tile, u, dst_s):
        is_c, slab, _, _ = _unit_operands(u)
        k_ref = kc_ref if is_c else kd_ref
        r0 = pl.multiple_of(tile * tk, tk)
        dst_s[u] = _dot_nt(k_ref[0, pl.ds(r0, tk), slab * LANES:(slab + 1) * LANES], qm_s[u])

    def consume(tile, u, src_s):
        _, _, row0, dv = _unit_operands(u)
        rows = dv + ONES_ROWS
        s = src_s[u]
        m_old = m_s[u]
        m_new = jnp.maximum(m_old, jnp.max(s, axis=0, keepdims=True))
        alpha = jnp.exp2(m_old - m_new)
        p = jnp.exp2(s - m_new).astype(BF16)
        pv = _dot(vt_ref[0, tile, row0:row0 + rows, :], p)
        acc_s[u, 0:rows, :] = acc_s[u, 0:rows, :] * alpha + pv
        m_s[u] = m_new

    def stage(tile, src_s, dst_s):
        nxt = jnp.minimum(tile + 1, nkt - 1)
        for u in range(N_UNITS):
            scores(nxt, u, dst_s)
            consume(tile, u, src_s)

    for u in range(N_UNITS):
        scores(0, u, s_a)
    stage(0, s_a, s_b)

    def body(pr, carry):
        stage(2 * pr + 1, s_b, s_a)
        stage(2 * pr + 2, s_a, s_b)
        return carry

    lax.fori_loop(0, (nkt - 1) // 2, body, 0)
    if (nkt - 1) % 2 == 1:
        stage(nkt - 1, s_b, s_a)

    lv = lam_ref[...]
    lam = (jnp.exp(jnp.sum(lv[0:1] * lv[1:2], axis=-1, keepdims=True))
           - jnp.exp(jnp.sum(lv[2:3] * lv[3:4], axis=-1, keepdims=True)) + lam_init)
    sg = sg_ref[...]
    dvc = 2 * HEAD_DIM
    for hc in range(N_C_HEADS):
        o1 = acc_s[2 * hc, 0:dvc, :] / acc_s[2 * hc, dvc:dvc + 1, :]
        o2 = acc_s[2 * hc + 1, 0:dvc, :] / acc_s[2 * hc + 1, dvc:dvc + 1, :]
        o = o1 - lam * o2
        ms = jnp.mean(o * o, axis=0, keepdims=True)
        o = o * lax.rsqrt(ms + RMS_EPS) * sg * (1.0 - lam_init)
        o_ref[0, :, hc * LANES:(hc + 1) * LANES] = o.T.astype(BF16)
    for pair in range(N_D_HEADS // 2):
        outs = []
        for half in range(2):
            u = 2 * N_C_HEADS + 2 * pair + half
            outs.append(acc_s[u, 0:HEAD_DIM, :] / acc_s[u, HEAD_DIM:HEAD_DIM + 1, :])
        o2 = jnp.concatenate(outs, axis=0)
        c0 = N_C_HEADS * LANES + pair * LANES
        o_ref[0, :, c0:c0 + LANES] = o2.T.astype(BF16)


def _attn_cd(lam_vec, sg, qc, qd, kc, kd, vt, nct, lam_init):
    b, tall, _ = qc.shape
    nkt, nv, tk = vt.shape[1], vt.shape[2], vt.shape[3]
    mq = TM
    nq = tall // mq - nct
    ow = N_C_HEADS * LANES + N_D_HEADS * HEAD_DIM
    qtile = lambda bi, i: (bi, i + nct, 0)
    return pl.pallas_call(
        functools.partial(_attn_cd_kernel, nkt, lam_init),
        grid=(b, nq),
        in_specs=[pl.BlockSpec((4, HEAD_DIM), lambda bi, i: (0, 0)),
                  pl.BlockSpec((LANES, 1), lambda bi, i: (0, 0)),
                  pl.BlockSpec((1, mq, qc.shape[-1]), qtile),
                  pl.BlockSpec((1, mq, qd.shape[-1]), qtile),
                  pl.BlockSpec((1, tall, kc.shape[-1]), lambda bi, i: (bi, 0, 0)),
                  pl.BlockSpec((1, tall, kd.shape[-1]), lambda bi, i: (bi, 0, 0)),
                  pl.BlockSpec((1, nkt, nv, tk), lambda bi, i: (bi, 0, 0, 0))],
        out_specs=pl.BlockSpec((1, mq, ow), lambda bi, i: (bi, i, 0)),
        out_shape=jax.ShapeDtypeStruct((b, nq * mq, ow), BF16),
        scratch_shapes=[pltpu.VMEM((N_UNITS, mq, LANES), BF16),
                        pltpu.VMEM((N_UNITS, 1, mq), F32),
                        pltpu.VMEM((N_UNITS, 2 * HEAD_DIM + ONES_ROWS, mq), F32),
                        pltpu.VMEM((N_UNITS, tk, mq), F32),
                        pltpu.VMEM((N_UNITS, tk, mq), F32)],
        compiler_params=_cparams(("parallel", "parallel")),
        name="attention_cd",
    )(lam_vec, sg, qc, qd, kc, kd, vt)


def _oproj_kernel(n_parts, widths, nct, alpha, n_exp, *refs):
    o_refs = refs[:n_parts]
    (w_ref, x_ref, mlat_ref, mctx_ref, lng_ref, lnb_ref, rwt_ref, rb_ref,
     x1_ref, h2_ref, te_ref, tg_ref) = refs[n_parts:]
    d = x_ref.shape[-1]
    is_ctx = pl.program_id(1) < nct
    y = None
    r0 = 0
    for o_ref, wd in zip(o_refs, widths):
        part = _dot(o_ref[0], w_ref[r0:r0 + wd, :])
        y = part if y is None else y + part
        r0 += wd
    m = jnp.where(is_ctx, mctx_ref[...], mlat_ref[0])
    x1 = _layernorm(alpha * x_ref[0] + m[:, 2 * d:3 * d] * y, lng_ref[...], lnb_ref[...])
    x1_ref[0] = x1
    h2 = x1 * (1.0 + m[:, 4 * d:5 * d]) + m[:, 3 * d:4 * d]
    h2_ref[0] = h2
    logits = _dot_nt(rwt_ref[...], h2.astype(BF16)) + rb_ref[...]
    tm = logits.shape[1]
    row = lax.broadcasted_iota(jnp.int32, logits.shape, 0)
    vals, idxs = [], []
    for _ in range(TOP_K):
        mx = jnp.max(logits, axis=0, keepdims=True)
        idx = jnp.min(jnp.where(logits == mx, row, n_exp), axis=0, keepdims=True)
        vals.append(mx)
        idxs.append(idx)
        logits = jnp.where(row == idx, NEG, logits)
    es = [jnp.exp(v - vals[0]) for v in vals]
    tot = es[0] + es[1] + es[2] + es[3]
    pad_i = jnp.zeros((8 - TOP_K, tm), jnp.int32)
    pad_f = jnp.zeros((8 - TOP_K, tm), F32)
    te_ref[0] = jnp.concatenate(idxs + [pad_i], axis=0)
    tg_ref[0] = jnp.concatenate([e / tot for e in es] + [pad_f], axis=0)


def _oproj(parts, w_out, x_all, mlat, mctx, lng, lnb, rwt, rb, nct_out, x_tile_off, alpha):
    b, t, _ = parts[0].shape
    d = x_all.shape[-1]
    nt = t // TM
    n_exp = rwt.shape[0]
    widths = tuple(p.shape[-1] for p in parts)
    tile = lambda bi, i: (bi, i, 0)
    full = lambda bi, i: (0, 0)
    in_specs = [pl.BlockSpec((1, TM, wd), tile) for wd in widths] + [
        pl.BlockSpec(w_out.shape, full),
        pl.BlockSpec((1, TM, d), lambda bi, i: (bi, i + x_tile_off, 0)),
        pl.BlockSpec((1, 1, N_MOD * d), lambda bi, i: (bi, 0, 0)),
        pl.BlockSpec((1, N_MOD * d), full),
        pl.BlockSpec((1, d), full),
        pl.BlockSpec((1, d), full),
        pl.BlockSpec((n_exp, d), full),
        pl.BlockSpec((n_exp, 1), full)]
    return pl.pallas_call(
        functools.partial(_oproj_kernel, len(parts), widths, nct_out, alpha, n_exp),
        grid=(b, nt),
        in_specs=in_specs,
        out_specs=[pl.BlockSpec((1, TM, d), tile),
                   pl.BlockSpec((1, TM, d), tile),
                   pl.BlockSpec((1, 8, TM), lambda bi, i: (bi, 0, i)),
                   pl.BlockSpec((1, 8, TM), lambda bi, i: (bi, 0, i))],
        out_shape=[jax.ShapeDtypeStruct((b, t, d), F32),
                   jax.ShapeDtypeStruct((b, t, d), F32),
                   jax.ShapeDtypeStruct((b, 8, t), jnp.int32),
                   jax.ShapeDtypeStruct((b, 8, t), F32)],
        compiler_params=_cparams(("parallel", "parallel")),
        name="out_proj_norm_router",
    )(*parts, w_out, x_all, mlat, mctx, lng, lnb, rwt, rb)


def _moe_kernel(be_ref, nx_ref, bv_ref, nu_ref, x_ref, wgu_hbm, bgu_ref, wd_hbm, bd_ref, y_ref,
                wgu_f, wd_f, wgu_s, wd_s, sem):
    blk = pl.program_id(0)
    ff = wd_s.shape[0]

    def fetch(e):
        return (pltpu.make_async_copy(wgu_hbm.at[e], wgu_f, sem.at[0]),
                pltpu.make_async_copy(wd_hbm.at[e], wd_f, sem.at[1]))

    @pl.when(blk < nu_ref[0])
    def _():
        e = be_ref[blk]
        new_expert = (blk == 0) | (e != be_ref[jnp.maximum(blk - 1, 0)])

        @pl.when(blk == 0)
        def _():
            for cp in fetch(e):
                cp.start()

        @pl.when(new_expert)
        def _():
            for cp in fetch(e):
                cp.wait()
            rows = 128

            def cast_gu(r, c):
                r0 = pl.multiple_of(r * rows, rows)
                wgu_s[pl.ds(r0, rows), :] = wgu_f[pl.ds(r0, rows), :].astype(BF16)
                return c

            def cast_d(r, c):
                r0 = pl.multiple_of(r * rows, rows)
                wd_s[pl.ds(r0, rows), :] = wd_f[pl.ds(r0, rows), :].astype(BF16)
                return c

            lax.fori_loop(0, wgu_s.shape[0] // rows, cast_gu, 0)
            lax.fori_loop(0, wd_s.shape[0] // rows, cast_d, 0)

            @pl.when(nx_ref[blk] != e)
            def _():
                for cp in fetch(nx_ref[blk]):
                    cp.start()

        def ffn(rows):
            gu = _dot(x_ref[rows, :].astype(BF16), wgu_s[...]) + bgu_ref[0]
            g = jnp.minimum(gu[:, :ff], SWIGLU_LIMIT)
            u = jnp.clip(gu[:, ff:], -SWIGLU_LIMIT, SWIGLU_LIMIT)
            act = (u + 1.0) * (g / (1.0 + jnp.exp(-SWIGLU_ALPHA * g)))
            y_ref[rows, :] = _dot(act.astype(BF16), wd_s[...]) + bd_ref[0]

        half = x_ref.shape[0] // 2

        @pl.when(bv_ref[blk] > half)
        def _():
            ffn(slice(None))

        @pl.when(bv_ref[blk] <= half)
        def _():
            ffn(slice(0, half))
            y_ref[half:, :] = jnp.zeros((half, y_ref.shape[1]), F32)

    @pl.when(blk >= nu_ref[0])
    def _():
        y_ref[...] = jnp.zeros(y_ref.shape, F32)


def _moe_experts(block_e, block_next_e, block_rows, n_used, xb, w_gu, b_gu, w_down, b_down):
    n_slots, d = xb.shape
    ff2 = w_gu.shape[-1]
    ff = w_down.shape[-2]
    n_exp = w_gu.shape[0] * w_gu.shape[1]
    w_gu = w_gu.reshape(n_exp, d, ff2)
    w_down = w_down.reshape(n_exp, ff, d)
    nb = n_slots // MOE_BM
    xblk = lambda i, be, nx, bv, nu: (jnp.minimum(i, nu[0] - 1), 0)
    bblk = lambda i, be, nx, bv, nu: (be[i], 0, 0)
    grid_spec = pltpu.PrefetchScalarGridSpec(
        num_scalar_prefetch=4,
        grid=(nb,),
        in_specs=[pl.BlockSpec((MOE_BM, d), xblk),
                  pl.BlockSpec(memory_space=pl.ANY),
                  pl.BlockSpec((1, 1, ff2), bblk),
                  pl.BlockSpec(memory_space=pl.ANY),
                  pl.BlockSpec((1, 1, d), bblk)],
        out_specs=pl.BlockSpec((MOE_BM, d), lambda i, be, nx, bv, nu: (i, 0)),
        scratch_shapes=[pltpu.VMEM((d, ff2), F32), pltpu.VMEM((ff, d), F32),
                        pltpu.VMEM((d, ff2), BF16), pltpu.VMEM((ff, d), BF16),
                        pltpu.SemaphoreType.DMA((2,))])
    return pl.pallas_call(
        _moe_kernel,
        grid_spec=grid_spec,
        out_shape=jax.ShapeDtypeStruct((n_slots, d), F32),
        compiler_params=_cparams(("arbitrary",)),
        name="moe_experts",
    )(block_e, block_next_e, block_rows, n_used, xb, w_gu, b_gu.reshape(n_exp, 1, ff2), w_down,
      b_down.reshape(n_exp, 1, d))


def _expert_onehot(te, n_exp):
    row = lax.broadcasted_iota(jnp.int32, (n_exp, te.shape[1]), 0)
    oh = jnp.zeros(row.shape, F32)
    for k in range(TOP_K):
        oh = oh + (row == te[k:k + 1, :]).astype(F32)
    return row, oh


def _groups(cnt):
    return jnp.floor((cnt + (ROW_GROUP - 1)) / ROW_GROUP)


def _route_count_kernel(n_exp, te_ref, n8_ref):
    _, oh = _expert_onehot(te_ref[0], n_exp)
    n8 = _groups(jnp.sum(oh, axis=1, keepdims=True))
    n8_ref[0] = jnp.broadcast_to(n8, (n_exp, LANES)).astype(jnp.int32)


def _route_pos_kernel(n_exp, te_ref, g0_ref, tri_ref, ltri_ref, pos_ref, tab_ref):
    te = te_ref[0]
    tm = te.shape[1]
    row, oh = _expert_onehot(te, n_exp)
    n8 = _groups(jnp.sum(oh, axis=1, keepdims=True))
    lbase = _dot(ltri_ref[...], jnp.broadcast_to(n8, (n_exp, LANES)).astype(BF16))[:, 0:1]
    before = _dot(oh.astype(BF16), tri_ref[...])
    where_to = ROW_GROUP * lbase + before
    pos = [jnp.sum(jnp.where(row == te[k:k + 1, :], where_to, 0.0), axis=0, keepdims=True)
           for k in range(TOP_K)]
    pad = jnp.zeros((8 - TOP_K, tm), F32)
    pos_ref[0] = jnp.concatenate(pos + [pad], axis=0).astype(jnp.int32)
    g = lax.broadcasted_iota(jnp.int32, (n_exp, tm), 1).astype(F32)
    inside = (g >= lbase) & (g < lbase + n8)
    gdst = jnp.sum(jnp.where(inside, g0_ref[0] + g - lbase, 0.0), axis=0, keepdims=True)
    ngrp = jnp.broadcast_to(jnp.sum(n8, axis=0, keepdims=True), (1, tm))
    tab_ref[0] = jnp.concatenate([gdst, ngrp, jnp.zeros((6, tm), F32)], axis=0).astype(jnp.int32)


def _route(te, n_exp):
    b, _, t = te.shape
    nt = t // TM
    n_tiles = b * nt
    assert TOP_K * TM // ROW_GROUP + n_exp <= TM
    tile = pl.BlockSpec((1, 8, TM), lambda bi, i: (bi, 0, i))
    per_tile = lambda bi, i: (bi * nt + i, 0, 0)
    n8 = pl.pallas_call(
        functools.partial(_route_count_kernel, n_exp),
        grid=(b, nt),
        in_specs=[tile],
        out_specs=pl.BlockSpec((1, n_exp, LANES), per_tile),
        out_shape=jax.ShapeDtypeStruct((n_tiles, n_exp, LANES), jnp.int32),
        compiler_params=_cparams(("parallel", "parallel")),
        name="route_count",
    )(te)[:, :, 0]
    rows = ROW_GROUP * jnp.sum(n8, axis=0)
    padded = (rows + MOE_BM - 1) // MOE_BM * MOE_BM
    pend = jnp.cumsum(padded)
    g0 = (pend - padded)[None, :] // ROW_GROUP + jnp.cumsum(n8, axis=0) - n8
    tri = jnp.asarray(np.triu(np.ones((TM, TM)), 1), BF16)
    ltri = jnp.asarray(np.tril(np.ones((n_exp, n_exp)), -1), BF16)
    pos, tab = pl.pallas_call(
        functools.partial(_route_pos_kernel, n_exp),
        grid=(b, nt),
        in_specs=[tile,
                  pl.BlockSpec((1, n_exp, 1), per_tile),
                  pl.BlockSpec((TM, TM), lambda bi, i: (0, 0)),
                  pl.BlockSpec((n_exp, n_exp), lambda bi, i: (0, 0))],
        out_specs=[tile, pl.BlockSpec((1, 8, TM), per_tile)],
        out_shape=[jax.ShapeDtypeStruct((b, 8, t), jnp.int32),
                   jax.ShapeDtypeStruct((n_tiles, 8, TM), jnp.int32)],
        compiler_params=_cparams(("parallel", "parallel")),
        name="route_positions",
    )(te, g0.astype(F32).reshape(n_tiles, n_exp, 1), tri, ltri)
    gdst = jnp.pad(tab[:, 0, :], ((0, 0), (0, SMEM_BLOCK - TM))).reshape(-1)
    ngrp = tab[:, 1, 0]
    max_rows = b * t * TOP_K + n_tiles * n_exp * (ROW_GROUP - 1)
    n_blocks = -(-max_rows // MOE_BM) + n_exp
    n_used = (pend[-1] // MOE_BM).astype(jnp.int32)
    blk = jnp.minimum(jnp.arange(n_blocks, dtype=jnp.int32), n_used - 1) * MOE_BM
    block_e = jnp.minimum(jnp.sum((pend[None, :] <= blk[:, None]).astype(jnp.int32), axis=1), n_exp - 1)
    ids = jnp.arange(n_exp, dtype=jnp.int32)
    later = (padded[None, :] > 0) & (ids[None, :] > block_e[:, None])
    block_next_e = jnp.min(jnp.where(later, ids[None, :], n_exp), axis=1)
    block_next_e = jnp.where(block_next_e == n_exp, block_e, block_next_e).astype(jnp.int32)
    mine = (block_e[:, None] == ids[None, :]).astype(jnp.int32)
    first_blk = jnp.sum(mine * ((pend - padded) // MOE_BM)[None, :], axis=1)
    left = jnp.sum(mine * rows[None, :], axis=1) - (jnp.arange(n_blocks, dtype=jnp.int32) - first_blk) * MOE_BM
    block_rows = jnp.clip(left, 0, MOE_BM).astype(jnp.int32)
    return (pos, gdst, ngrp, pend.astype(jnp.int32), block_e, block_next_e, block_rows, n_used.reshape(1),
            n_blocks)


def _group_copy(src_ref, src_group, dst_ref, dst_group, sem):
    s0 = pl.multiple_of(src_group * ROW_GROUP, ROW_GROUP)
    d0 = pl.multiple_of(dst_group * ROW_GROUP, ROW_GROUP)
    return pltpu.make_async_copy(src_ref.at[pl.ds(s0, ROW_GROUP)], dst_ref.at[pl.ds(d0, ROW_GROUP)], sem)


def _tile_step():
    return pl.program_id(0) * pl.num_programs(1) + pl.program_id(1)


def _moe_scatter_kernel(n_exp, pend_ref, ngrp_ref, gdst_ref, gdst_prev_ref, pos_ref, h_ref, xb_ref,
                        xs, zero_s, sem, zsem):
    step = _tile_step()
    tm = h_ref.shape[1]

    @pl.when(step == 0)
    def _():
        zero_s[...] = jnp.zeros(zero_s.shape, zero_s.dtype)
        for phase in ("start", "wait"):
            for e in range(n_exp):
                lo = pend_ref[e - 1] if e else 0

                @pl.when(pend_ref[e] > lo)
                def _():
                    r0 = pl.multiple_of(pend_ref[e] - MOE_BM, MOE_BM)
                    cp = pltpu.make_async_copy(zero_s, xb_ref.at[pl.ds(r0, MOE_BM)], zsem)
                    cp.start() if phase == "start" else cp.wait()

            def unused_block(blk, c):
                r0 = pl.multiple_of(blk * MOE_BM, MOE_BM)
                cp = pltpu.make_async_copy(zero_s, xb_ref.at[pl.ds(r0, MOE_BM)], zsem)
                cp.start() if phase == "start" else cp.wait()
                return c

            lax.fori_loop(pend_ref[n_exp - 1] // MOE_BM, xb_ref.shape[0] // MOE_BM, unused_block, 0)

    par = step % 2
    pos = pos_ref[0]
    hb = h_ref[0].astype(BF16)
    for c in range(xs.shape[1] // tm):
        r = lax.broadcasted_iota(jnp.int32, (tm, tm), 0) + c * tm
        hit = r == pos[0:1, :]
        for k in range(1, TOP_K):
            hit = hit | (r == pos[k:k + 1, :])
        xs[par, c * tm:(c + 1) * tm, :] = _dot(jnp.where(hit, 1.0, 0.0).astype(BF16), hb)

    def groups(table, buf, count, wait):
        def body(g, c):
            cp = _group_copy(xs.at[buf], g, xb_ref, table[g], sem.at[buf])
            cp.wait() if wait else cp.start()
            return c
        lax.fori_loop(0, count, body, 0)

    groups(gdst_ref, par, ngrp_ref[step], False)

    @pl.when(step > 0)
    def _():
        groups(gdst_prev_ref, 1 - par, ngrp_ref[jnp.maximum(step - 1, 0)], True)

    @pl.when(step == pl.num_programs(0) * pl.num_programs(1) - 1)
    def _():
        groups(gdst_ref, par, ngrp_ref[step], True)


def _sort_rows(n_exp):
    return -(-(TOP_K * TM + n_exp * ROW_GROUP) // TM) * TM


def _moe_scatter(pend, ngrp, gdst, pos, h2, n_slots):
    b, t, d = h2.shape
    nt = t // TM
    n_exp = pend.shape[0]
    grid_spec = pltpu.PrefetchScalarGridSpec(
        num_scalar_prefetch=2,
        grid=(b, nt),
        in_specs=[pl.BlockSpec((SMEM_BLOCK,), lambda bi, i, pe, ng: (bi * nt + i,), memory_space=pltpu.SMEM),
                  pl.BlockSpec((SMEM_BLOCK,), lambda bi, i, pe, ng: (jnp.maximum(bi * nt + i - 1, 0),),
                               memory_space=pltpu.SMEM),
                  pl.BlockSpec((1, 8, TM), lambda bi, i, pe, ng: (bi, 0, i)),
                  pl.BlockSpec((1, TM, d), lambda bi, i, pe, ng: (bi, i, 0))],
        out_specs=pl.BlockSpec(memory_space=pl.ANY),
        scratch_shapes=[pltpu.VMEM((2, _sort_rows(n_exp), d), F32), pltpu.VMEM((MOE_BM, d), F32),
                        pltpu.SemaphoreType.DMA((2,)), pltpu.SemaphoreType.DMA])
    return pl.pallas_call(
        functools.partial(_moe_scatter_kernel, n_exp),
        grid_spec=grid_spec,
        out_shape=jax.ShapeDtypeStruct((n_slots, d), F32),
        compiler_params=_cparams(("arbitrary", "arbitrary")),
        name="moe_scatter_rows",
    )(pend, ngrp, gdst, gdst, pos, h2)


def _combine_kernel(nct, alpha, ngrp_ref, gdst_ref, gdst_next_ref, yb_ref, pos_ref, tg_ref, x1_ref,
                    mlat_ref, mctx_ref, lng_ref, lnb_ref, o_ref, ys, sem):
    d = x1_ref.shape[-1]
    tm = x1_ref.shape[1]
    step = _tile_step()
    n_steps = pl.num_programs(0) * pl.num_programs(1)
    par = step % 2

    def groups(table, buf, count, wait):
        def body(g, c):
            cp = _group_copy(yb_ref, table[g], ys.at[buf], g, sem.at[buf])
            cp.wait() if wait else cp.start()
            return c
        lax.fori_loop(0, count, body, 0)

    @pl.when(step == 0)
    def _():
        ys[...] = jnp.zeros(ys.shape, F32)
        groups(gdst_ref, 0, ngrp_ref[0], False)

    @pl.when(step + 1 < n_steps)
    def _():
        groups(gdst_next_ref, 1 - par, ngrp_ref[jnp.minimum(step + 1, n_steps - 1)], False)

    groups(gdst_ref, par, ngrp_ref[step], True)
    is_ctx = pl.program_id(1) < nct
    m = jnp.where(is_ctx, mctx_ref[...], mlat_ref[0])
    pos = pos_ref[0]
    gates = tg_ref[0]
    f = None
    for c in range(ys.shape[1] // tm):
        col = lax.broadcasted_iota(jnp.int32, (tm, tm), 1) + c * tm
        w = jnp.where(col == pos[:, 0:1], gates[:, 0:1], 0.0)
        for k in range(1, TOP_K):
            w = w + jnp.where(col == pos[:, k:k + 1], gates[:, k:k + 1], 0.0)
        part = _dot(w.astype(BF16), ys[par, c * tm:(c + 1) * tm, :].astype(BF16))
        f = part if f is None else f + part
    o_ref[0] = _layernorm(alpha * x1_ref[0] + m[:, 5 * d:6 * d] * f, lng_ref[...], lnb_ref[...])


def _combine_postnorm(ngrp, gdst, yb, pos_t, tg_t, x1, mlat, mctx, lng, lnb, nct, alpha, n_exp):
    b, t, d = x1.shape
    nt = t // TM
    tile = lambda bi, i, ng: (bi, i, 0)
    full = lambda bi, i, ng: (0, 0)
    grid_spec = pltpu.PrefetchScalarGridSpec(
        num_scalar_prefetch=1,
        grid=(b, nt),
        in_specs=[pl.BlockSpec((SMEM_BLOCK,), lambda bi, i, ng: (bi * nt + i,), memory_space=pltpu.SMEM),
                  pl.BlockSpec((SMEM_BLOCK,), lambda bi, i, ng: (jnp.minimum(bi * nt + i + 1, b * nt - 1),),
                               memory_space=pltpu.SMEM),
                  pl.BlockSpec(memory_space=pl.ANY),
                  pl.BlockSpec((1, TM, 8), tile),
                  pl.BlockSpec((1, TM, 8), tile),
                  pl.BlockSpec((1, TM, d), tile),
                  pl.BlockSpec((1, 1, N_MOD * d), lambda bi, i, ng: (bi, 0, 0)),
                  pl.BlockSpec((1, N_MOD * d), full),
                  pl.BlockSpec((1, d), full),
                  pl.BlockSpec((1, d), full)],
        out_specs=pl.BlockSpec((1, TM, d), tile),
        scratch_shapes=[pltpu.VMEM((2, _sort_rows(n_exp), d), F32), pltpu.SemaphoreType.DMA((2,))])
    return pl.pallas_call(
        functools.partial(_combine_kernel, nct, alpha),
        grid_spec=grid_spec,
        out_shape=jax.ShapeDtypeStruct((b, t, d), F32),
        compiler_params=_cparams(("arbitrary", "arbitrary")),
        name="combine_post_norm",
    )(ngrp, gdst, gdst, yb, pos_t, tg_t, x1, mlat, mctx, lng, lnb)


def _rope_tables(ctx_len, seq):
    t = np.arange(seq)
    row = (t // GRID_W).astype(np.float64)
    col = (t % GRID_W).astype(np.float64)
    nf = HEAD_DIM // 4
    inv = ROPE_THETA ** (-np.arange(nf, dtype=np.float64) / nf)
    ar = row[:, None] * inv[None, :]
    ac = col[:, None] * inv[None, :]
    ang = np.concatenate([ar, ar, ac, ac], axis=-1)
    cos = np.concatenate([np.ones((ctx_len, HEAD_DIM)), np.cos(ang)], axis=0)
    sin = np.concatenate([np.zeros((ctx_len, HEAD_DIM)), np.sin(ang)], axis=0)
    sign = np.where((np.arange(HEAD_DIM) % 32) < 16, -1.0, 1.0)[None, :]
    cos2 = np.tile(cos, (1, LANES // HEAD_DIM))
    sin2 = np.tile(sin * sign, (1, LANES // HEAD_DIM))
    return jnp.asarray(cos2, F32), jnp.asarray(sin2, F32)


def _channel_dft():
    c = np.arange(HEAD_DIM)
    ang = 2.0 * np.pi * ((c[:, None] * c[None, :]) % HEAD_DIM) / HEAD_DIM
    eye = np.eye(4)
    cs = np.concatenate([np.kron(eye, np.cos(ang)), np.kron(eye, np.sin(ang))], axis=1)
    return jnp.asarray(cs, BF16)


def _group_mean_matrix():
    bd = np.kron(np.eye(LANES // HEAD_DIM), np.full((HEAD_DIM, HEAD_DIM), 1.0 / HEAD_DIM))
    return jnp.asarray(bd, BF16)


def _dup_heads(wk, n_heads):
    d = wk.shape[0]
    return jnp.broadcast_to(wk.reshape(d, n_heads, 1, HEAD_DIM), (d, n_heads, 2, HEAD_DIM)).reshape(d, n_heads * LANES)


def _lambda_init(layer):
    return 0.8 - 0.6 * math.exp(-0.3 * layer)


def _moe_postnorm(h2, te, tg, x1, mlat, mctx, lng, lnb, nct, alpha, layer, w_gu, b_gu, w_down, b_down):
    n_exp = w_gu.shape[1]
    pos, gdst, ngrp, pend, block_e, block_next_e, block_rows, n_used, n_blocks = _route(te, n_exp)
    xb = _moe_scatter(pend, ngrp, gdst, pos, h2, n_blocks * MOE_BM)
    off = layer * n_exp
    yb = _moe_experts(block_e + off, block_next_e + off, block_rows, n_used, xb, w_gu, b_gu, w_down, b_down)
    pos_t, tg_t = jnp.transpose(pos, (0, 2, 1)), jnp.transpose(tg, (0, 2, 1))
    return _combine_postnorm(ngrp, gdst, yb, pos_t, tg_t, x1, mlat, mctx, lng, lnb, nct, alpha, n_exp)


def kernel(x, c, ctx, c_ctx, mod_w, mod_b, ln_g, ln_b, ab_w_in, ab_sink, ab_w_out,
           cd_w_in, cd_lambda, cd_subln_g, cd_q_norm_g, cd_k_norm_g, cd_w_out,
           router_w, router_b, expert_w_gu, expert_b_gu, expert_w_down, expert_b_down):
    b, s, d = x.shape
    n_ctx = ctx.shape[1]
    depth = mod_w.shape[0]
    n_exp = router_w.shape[-1]
    assert d == 16 * HEAD_DIM and n_ctx % TM == 0 and s % TM == 0 and s % GRID_W == 0
    nct = n_ctx // TM
    alpha = (2 * depth) ** 0.25

    cos, sin = _rope_tables(n_ctx, s)
    cs_dft = _channel_dft()
    bd = _group_mean_matrix()
    mods = _mod_vectors(c, c_ctx, mod_w, mod_b)

    x_all = jnp.concatenate([ctx, x], axis=1)
    for l in range(depth):
        last = l == depth - 1
        i = l // 2
        mlat = mods[l, :b].reshape(b, 1, N_MOD * d)
        mctx = mods[l, b:b + 1]
        lng1, lnb1 = ln_g[l, 0].reshape(1, d), ln_b[l, 0].reshape(1, d)
        lng2, lnb2 = ln_g[l, 1].reshape(1, d), ln_b[l, 1].reshape(1, d)
        rwt = router_w[l].T.astype(BF16)
        rb = router_b[l].reshape(n_exp, 1)
        if l % 2 == 0:
            w = ab_w_in[i]
            w_n = jnp.concatenate([w[:, :1024], _dup_heads(w[:, 1024:1280], 4)], axis=1).astype(BF16)
            wvt = w[:, 1280:1536].T.astype(BF16)
            uw, q, k, vt = _proj_ab(x_all, mlat, mctx, w_n, wvt, cos, sin, cs_dft, nct)
            oa = jnp.concatenate([_fourier(uw[:, :n_ctx]), _fourier(uw[:, n_ctx:])], axis=1)
            ob = _win_attn(ab_sink[i], q, k, vt, nct)
            if last:
                parts = [oa[:, n_ctx:], ob[:, n_ctx:]]
            else:
                parts = [oa, ob]
            w_out = ab_w_out[i].astype(BF16)
        else:
            w = cd_w_in[i]
            w_n = jnp.concatenate([w[:, :1536], _dup_heads(w[:, 1536:1664], 2)], axis=1).astype(BF16)
            wvt = w[:, 1664:2304].T.astype(BF16)
            gq = jnp.tile(cd_q_norm_g[i], 2).reshape(1, LANES)
            gk = jnp.tile(cd_k_norm_g[i], 2).reshape(1, LANES)
            qc, qd, kc, kd, vt = _proj_cd(x_all, mlat, mctx, w_n, wvt, cos, sin, bd, gq, gk, nct)
            sg = cd_subln_g[i].reshape(LANES, 1)
            o_lat = _attn_cd(cd_lambda[i], sg, qc, qd, kc, kd, vt, nct, _lambda_init(l))
            if last:
                parts = [o_lat]
            else:
                raise NotImplementedError("context outputs of a differential/axial layer")
            w_out = cd_w_out[i].astype(BF16)
        experts = (l, expert_w_gu, expert_b_gu, expert_w_down, expert_b_down)
        if last:
            x1, h2, te, tg = _oproj(parts, w_out, x_all, mlat, mctx, lng1, lnb1, rwt, rb, 0, nct, alpha)
            return _moe_postnorm(h2, te, tg, x1, mlat, mctx, lng2, lnb2, 0, alpha, *experts)
        x1, h2, te, tg = _oproj(parts, w_out, x_all, mlat, mctx, lng1, lnb1, rwt, rb, nct, 0, alpha)
        x_all = _moe_postnorm(h2, te, tg, x1, mlat, mctx, lng2, lnb2, nct, alpha, *experts)
    return x_all[:, n_ctx:]
```

```python
import functools
import math

import jax
import jax.numpy as jnp
import numpy as np
from jax import lax
from jax.experimental import pallas as pl
from jax.experimental.pallas import tpu as pltpu

F32 = jnp.float32
BF16 = jnp.bfloat16

HEAD_DIM = 64
GRID_W = 64
WINDOW = 128
ROPE_THETA = 10000.0
LN_EPS = 1e-6
RMS_EPS = 1e-6
N_MOD = 6
TOP_K = 4
SWIGLU_LIMIT = 7.0
SWIGLU_ALPHA = 1.702
NEG = -1e30
LOG2E = 1.4426950408889634
QSCALE = HEAD_DIM ** -0.5 * LOG2E

N_C_HEADS = 4
N_D_HEADS = 8
N_D_KV = 2
N_UNITS = 2 * N_C_HEADS + N_D_HEADS
ONES_ROWS = 16
_VT_LAYOUT = tuple((h * (2 * HEAD_DIM + ONES_ROWS), 2 * HEAD_DIM) for h in range(N_C_HEADS)) + tuple(
    (N_C_HEADS * (2 * HEAD_DIM + ONES_ROWS) + g * (HEAD_DIM + ONES_ROWS), HEAD_DIM) for g in range(N_D_KV))
VT_ROWS = _VT_LAYOUT[-1][0] + HEAD_DIM + ONES_ROWS

LANES = 128
TM = 256
MOE_BM = 512
ROW_GROUP = 8
SMEM_BLOCK = 1024
VMEM_LIMIT = 56 * 1024 * 1024


def _dot(a, b):
    return jnp.dot(a, b, preferred_element_type=F32)


def _dot_nt(a, b):
    return lax.dot_general(a, b, (((1,), (1,)), ((), ())), preferred_element_type=F32)


def _cparams(sem):
    return pltpu.CompilerParams(dimension_semantics=sem, vmem_limit_bytes=VMEM_LIMIT)


def _layernorm(z, g, b):
    mu = jnp.mean(z, axis=-1, keepdims=True)
    d = z - mu
    var = jnp.mean(d * d, axis=-1, keepdims=True)
    return d * lax.rsqrt(var + LN_EPS) * g + b


def _rope128(x, cos, sin_signed):
    lane = lax.broadcasted_iota(jnp.int32, x.shape, 1)
    lo = (lane % 32) < 16
    rot = jnp.where(lo, pltpu.roll(x, LANES - 16, 1), pltpu.roll(x, 16, 1))
    return x * cos + rot * sin_signed


def _rmsnorm128(x, g, bd):
    x2 = x * x
    hi = x2.astype(BF16)
    lo = (x2 - hi.astype(F32)).astype(BF16)
    ms = _dot(hi, bd) + _dot(lo, bd)
    return x * lax.rsqrt(ms + RMS_EPS) * g


def _modulated(x_ref, ctx_ref, mlat_ref, mctx_ref, is_ctx, d):
    m = jnp.where(is_ctx, mctx_ref[...], mlat_ref[0])
    x = jnp.where(is_ctx, ctx_ref[0], x_ref[0])
    return (x * (1.0 + m[:, d:2 * d]) + m[:, 0:d]).astype(BF16)


def _stream_specs(x, ctx, nct, d, off=0):
    if ctx is None:
        return (x, x), [pl.BlockSpec((1, TM, d), lambda bi, i: (bi, i + off, 0)),
                        pl.BlockSpec((1, TM, d), lambda bi, i: (bi, jnp.minimum(i, max(nct - 1, 0)), 0))]
    return (x, ctx), [pl.BlockSpec((1, TM, d), lambda bi, i: (bi, jnp.maximum(i - nct, 0), 0)),
                      pl.BlockSpec((1, TM, d), lambda bi, i: (bi, jnp.minimum(i, nct - 1), 0))]


def _mod_kernel(c_ref, w_ref, b_ref, o_ref):
    c = c_ref[...]
    s = (c / (1.0 + jnp.exp(-c))).astype(BF16)
    o_ref[0] = _dot(s, w_ref[0].astype(BF16)) + b_ref[0]


def _mod_vectors(c, c_ctx, mod_w, mod_b):
    depth, d, n = mod_w.shape
    b = c.shape[0]
    rows = 8 * (-(-(b + 1) // 8))
    cs = jnp.zeros((rows, d), F32).at[:b].set(c).at[b].set(c_ctx)
    tn = 1536
    out = pl.pallas_call(
        _mod_kernel,
        grid=(depth, n // tn),
        in_specs=[pl.BlockSpec((rows, d), lambda l, j: (0, 0)),
                  pl.BlockSpec((1, d, tn), lambda l, j: (l, 0, j)),
                  pl.BlockSpec((1, 1, tn), lambda l, j: (l, 0, j))],
        out_specs=pl.BlockSpec((1, rows, tn), lambda l, j: (l, 0, j)),
        out_shape=jax.ShapeDtypeStruct((depth, rows, n), F32),
        compiler_params=_cparams(("parallel", "parallel")),
        name="mod_vectors",
    )(cs, mod_w, mod_b.reshape(depth, 1, n))
    return out


def _proj_ab_kernel(nct, x_ref, ctx_ref, mlat_ref, mctx_ref, w_ref, wvt_ref, cos_ref, sin_ref, cs_ref,
                    uw_ref, q_ref, k_ref, vt_ref):
    is_ctx = pl.program_id(1) < nct
    d = x_ref.shape[-1]
    h = _modulated(x_ref, ctx_ref, mlat_ref, mctx_ref, is_ctx, d)
    p = _dot(h, w_ref[...])
    uw_ref[0] = _dot(p[:, 0:256].astype(BF16), cs_ref[...]).astype(BF16)
    cos = cos_ref[...]
    sin = sin_ref[...]
    for j in range(6):
        c0 = 256 + j * LANES
        q_ref[0, :, j * LANES:(j + 1) * LANES] = (_rope128(p[:, c0:c0 + LANES], cos, sin) * QSCALE).astype(BF16)
    for j in range(4):
        c0 = 1024 + j * LANES
        k_ref[0, :, j * LANES:(j + 1) * LANES] = _rope128(p[:, c0:c0 + LANES], cos, sin).astype(BF16)
    vt = _dot_nt(wvt_ref[...], h).astype(BF16)
    for c in range(TM // LANES):
        vt_ref[0, c] = vt[:, c * LANES:(c + 1) * LANES]


def _proj_ab(x, ctx, mlat, mctx, w, wvt, cos, sin, cs, nct):
    b, d = x.shape[0], x.shape[2]
    tall = x.shape[1] + (0 if ctx is None else ctx.shape[1])
    stream, stream_specs = _stream_specs(x, ctx, nct, d)
    nt = tall // TM
    nw = w.shape[1]
    nchunk = tall // LANES
    cpt = TM // LANES
    return pl.pallas_call(
        functools.partial(_proj_ab_kernel, nct),
        grid=(b, nt),
        in_specs=stream_specs + [
                  pl.BlockSpec((1, 1, N_MOD * d), lambda bi, i: (bi, 0, 0)),
                  pl.BlockSpec((1, N_MOD * d), lambda bi, i: (0, 0)),
                  pl.BlockSpec((d, nw), lambda bi, i: (0, 0)),
                  pl.BlockSpec((256, d), lambda bi, i: (0, 0)),
                  pl.BlockSpec((TM, LANES), lambda bi, i: (i, 0)),
                  pl.BlockSpec((TM, LANES), lambda bi, i: (i, 0)),
                  pl.BlockSpec((256, 512), lambda bi, i: (0, 0))],
        out_specs=[pl.BlockSpec((1, TM, 512), lambda bi, i: (bi, i, 0)),
                   pl.BlockSpec((1, TM, 768), lambda bi, i: (bi, i, 0)),
                   pl.BlockSpec((1, TM, 512), lambda bi, i: (bi, i, 0)),
                   pl.BlockSpec((1, cpt, 256, LANES), lambda bi, i: (bi, i, 0, 0))],
        out_shape=[jax.ShapeDtypeStruct((b, tall, 512), BF16),
                   jax.ShapeDtypeStruct((b, tall, 768), BF16),
                   jax.ShapeDtypeStruct((b, tall, 512), BF16),
                   jax.ShapeDtypeStruct((b, nchunk, 256, LANES), BF16)],
        compiler_params=_cparams(("parallel", "parallel")),
        name="proj_ab",
    )(*stream, mlat, mctx, w, wvt, cos, sin, cs)


def _proj_cd_kernel(nct, x_ref, ctx_ref, mlat_ref, mctx_ref, w_ref, wvt_ref, cos_ref, sin_ref, bd_ref,
                    gq_ref, gk_ref, qc_ref, qd_ref, kc_ref, kd_ref, vt_ref):
    is_ctx = pl.program_id(1) < nct
    d = x_ref.shape[-1]
    h = _modulated(x_ref, ctx_ref, mlat_ref, mctx_ref, is_ctx, d)
    p = _dot(h, w_ref[...])
    cos = cos_ref[...]
    sin = sin_ref[...]
    bd = bd_ref[...]
    gq = gq_ref[...]
    gk = gk_ref[...]
    for j in range(4):
        sl = slice(j * LANES, (j + 1) * LANES)
        qc_ref[0, :, sl] = (_rope128(p[:, j * LANES:(j + 1) * LANES], cos, sin) * QSCALE).astype(BF16)
        c0 = 512 + j * LANES
        qd_ref[0, :, sl] = (_rope128(_rmsnorm128(p[:, c0:c0 + LANES], gq, bd), cos, sin) * QSCALE).astype(BF16)
        c0 = 1024 + j * LANES
        kc_ref[0, :, sl] = _rope128(p[:, c0:c0 + LANES], cos, sin).astype(BF16)
    for j in range(2):
        c0 = 1536 + j * LANES
        kd_ref[0, :, j * LANES:(j + 1) * LANES] = _rope128(
            _rmsnorm128(p[:, c0:c0 + LANES], gk, bd), cos, sin).astype(BF16)
    vt = _dot_nt(wvt_ref[...], h).astype(BF16)
    ones = jnp.ones((ONES_ROWS, vt.shape[1]), BF16)
    src = 0
    for dst, dv in _VT_LAYOUT:
        vt_ref[0, 0, dst:dst + dv, :] = vt[src:src + dv, :]
        vt_ref[0, 0, dst + dv:dst + dv + ONES_ROWS, :] = ones
        src += dv


def _proj_cd(x, ctx, mlat, mctx, w, wvt, cos, sin, bd, gq, gk, nct):
    b, d = x.shape[0], x.shape[2]
    tall = x.shape[1] + (0 if ctx is None else ctx.shape[1])
    stream, stream_specs = _stream_specs(x, ctx, nct, d)
    nt = tall // TM
    nw = w.shape[1]
    nv = VT_ROWS
    full = lambda bi, i: (0, 0)
    tile = lambda bi, i: (bi, i, 0)
    return pl.pallas_call(
        functools.partial(_proj_cd_kernel, nct),
        grid=(b, nt),
        in_specs=stream_specs + [
                  pl.BlockSpec((1, 1, N_MOD * d), lambda bi, i: (bi, 0, 0)),
                  pl.BlockSpec((1, N_MOD * d), full),
                  pl.BlockSpec((d, nw), full),
                  pl.BlockSpec(wvt.shape, full),
                  pl.BlockSpec((TM, LANES), lambda bi, i: (i, 0)),
                  pl.BlockSpec((TM, LANES), lambda bi, i: (i, 0)),
                  pl.BlockSpec((LANES, LANES), full),
                  pl.BlockSpec((1, LANES), full),
                  pl.BlockSpec((1, LANES), full)],
        out_specs=[pl.BlockSpec((1, TM, 512), tile),
                   pl.BlockSpec((1, TM, 512), tile),
                   pl.BlockSpec((1, TM, 512), tile),
                   pl.BlockSpec((1, TM, 256), tile),
                   pl.BlockSpec((1, 1, nv, TM), lambda bi, i: (bi, i, 0, 0))],
        out_shape=[jax.ShapeDtypeStruct((b, tall, 512), BF16),
                   jax.ShapeDtypeStruct((b, tall, 512), BF16),
                   jax.ShapeDtypeStruct((b, tall, 512), BF16),
                   jax.ShapeDtypeStruct((b, tall, 256), BF16),
                   jax.ShapeDtypeStruct((b, nt, nv, TM), BF16)],
        compiler_params=_cparams(("parallel", "parallel")),
        name="proj_cd",
    )(*stream, mlat, mctx, w, wvt, cos, sin, bd, gq, gk)


def _fourier_kernel(nb, scale, uw_ref, ca_ref, sa_ref, cb_ref, sb_ref, o_ref):
    j = pl.program_id(0)
    ca = ca_ref[pl.ds(j, 1), :]
    sa = sa_ref[pl.ds(j, 1), :]
    cb = cb_ref[...]
    sb = sb_ref[...]
    ct = (ca * cb - sa * sb).astype(BF16)
    nst = (-(sa * cb + ca * sb)).astype(BF16)
    for bi in range(nb):
        acc = _dot(ct, uw_ref[bi, :, 0:256]) + _dot(nst, uw_ref[bi, :, 256:512])
        o_ref[bi] = (acc * scale).astype(BF16)


def _dft_tables(t, tmf):
    k = np.arange(t, dtype=np.int64)
    j1 = np.arange(t // tmf, dtype=np.int64) * tmf
    j0 = np.arange(tmf, dtype=np.int64)
    aa = (2.0 * np.pi / t) * ((j1[:, None] * k[None, :]) % t)
    ab = (2.0 * np.pi / t) * ((j0[:, None] * k[None, :]) % t)
    f = lambda a: jnp.asarray(a, F32)
    return f(np.cos(aa)), f(np.sin(aa)), f(np.cos(ab)), f(np.sin(ab))


def _fourier(uw):
    b, t, _ = uw.shape
    tmf = min(128, t)
    ca, sa, cb, sb = _dft_tables(t, tmf)
    scale = 1.0 / math.sqrt(t * HEAD_DIM)
    full2 = lambda j: (0, 0)
    return pl.pallas_call(
        functools.partial(_fourier_kernel, b, scale),
        grid=(t // tmf,),
        in_specs=[pl.BlockSpec((b, t, 512), lambda j: (0, 0, 0)),
                  pl.BlockSpec((t // tmf, t), full2),
                  pl.BlockSpec((t // tmf, t), full2),
                  pl.BlockSpec((tmf, t), full2),
                  pl.BlockSpec((tmf, t), full2)],
        out_specs=pl.BlockSpec((b, tmf, 256), lambda j: (0, j, 0)),
        out_shape=jax.ShapeDtypeStruct((b, t, 256), BF16),
        compiler_params=_cparams(("parallel",)),
        name="fourier_mix",
    )(uw, ca, sa, cb, sb)


def _win_attn_kernel(nct, nchunk, sink_ref, q_ref, k_ref, vt_ref, o_ref, s_a, s_b):
    i = pl.program_id(1)
    is_ctx = i < nct
    j = i - nct
    ctx_rows = nct * TM
    ctx_chunks = ctx_rows // LANES
    lw = TM + 2 * WINDOW
    lchunks = lw // LANES
    cs = jnp.clip(ctx_chunks + (TM // LANES) * j - WINDOW // LANES, 0, nchunk - lchunks)
    rs = pl.multiple_of(cs * LANES, LANES)
    qpos = j * TM + lax.broadcasted_iota(jnp.int32, (1, TM), 1)
    kpos = cs * LANES - ctx_rows + lax.broadcasted_iota(jnp.int32, (lw, 1), 0)
    allowed = (jnp.abs(qpos - kpos) <= WINDOW) & (kpos >= 0) & jnp.logical_not(is_ctx)
    lane = lax.broadcasted_iota(jnp.int32, (TM, LANES), 1)
    n_heads = q_ref.shape[-1] // HEAD_DIM
    group = n_heads // (k_ref.shape[-1] // LANES)

    def scores(hq, dst_s):
        pair, half = divmod(hq, 2)
        g = hq // group
        qp = q_ref[0, :, pair * LANES:(pair + 1) * LANES]
        qm = jnp.where((lane >= HEAD_DIM) == (half == 1), qp, jnp.zeros_like(qp))
        dst_s[0:ctx_rows, :] = _dot_nt(k_ref[0, 0:ctx_rows, g * LANES:(g + 1) * LANES], qm)
        s_l = _dot_nt(k_ref[0, pl.ds(rs, lw), g * LANES:(g + 1) * LANES], qm)
        dst_s[ctx_rows:ctx_rows + lw, :] = jnp.where(allowed, s_l, NEG)

    def consume(hq, src_s):
        g = hq // group
        s = src_s[...]
        sk = sink_ref[hq] * LOG2E
        m = jnp.maximum(jnp.max(s, axis=0, keepdims=True), sk)
        p = jnp.exp2(s - m)
        l = jnp.sum(p, axis=0, keepdims=True) + jnp.exp2(sk - m)
        p = p.astype(BF16)
        vrows = slice(g * HEAD_DIM, (g + 1) * HEAD_DIM)
        acc = jnp.zeros((HEAD_DIM, TM), F32)
        for c in range(ctx_chunks):
            acc = acc + _dot(vt_ref[0, c, vrows, :], p[c * LANES:(c + 1) * LANES, :])
        for c in range(lchunks):
            r0 = ctx_rows + c * LANES
            acc = acc + _dot(vt_ref[0, cs + c, vrows, :], p[r0:r0 + LANES, :])
        return acc / l

    bufs = (s_a, s_b)
    scores(0, s_a)
    outs = []
    for hq in range(n_heads):
        if hq + 1 < n_heads:
            scores(hq + 1, bufs[(hq + 1) % 2])
        outs.append(consume(hq, bufs[hq % 2]))
        if hq % 2 == 1:
            pair = hq // 2
            o2 = jnp.concatenate(outs, axis=0)
            o_ref[0, :, pair * LANES:(pair + 1) * LANES] = o2.T.astype(BF16)
            outs = []


def _win_attn(sink, q, k, vt, nct):
    b, tall, qw = q.shape
    nt = tall // TM
    nchunk = vt.shape[1]
    return pl.pallas_call(
        functools.partial(_win_attn_kernel, nct, nchunk),
        grid=(b, nt),
        in_specs=[pl.BlockSpec(memory_space=pltpu.SMEM),
                  pl.BlockSpec((1, TM, qw), lambda bi, i: (bi, i, 0)),
                  pl.BlockSpec((1, tall, k.shape[-1]), lambda bi, i: (bi, 0, 0)),
                  pl.BlockSpec((1, nchunk, vt.shape[2], LANES), lambda bi, i: (bi, 0, 0, 0))],
        out_specs=pl.BlockSpec((1, TM, qw), lambda bi, i: (bi, i, 0)),
        out_shape=jax.ShapeDtypeStruct((b, tall, qw), BF16),
        scratch_shapes=[pltpu.VMEM((nct * TM + TM + 2 * WINDOW, TM), F32)] * 2,
        compiler_params=_cparams(("parallel", "parallel")),
        name="window_attention",
    )(sink, q, k, vt)


def _unit_operands(u):
    if u < 2 * N_C_HEADS:
        hc = u // 2
        dst, dv = _VT_LAYOUT[hc]
        return True, hc, dst, dv
    g = (u - 2 * N_C_HEADS) // (N_D_HEADS // N_D_KV)
    dst, dv = _VT_LAYOUT[N_C_HEADS + g]
    return False, g, dst, dv


def _attn_cd_kernel(nkt, lam_init, lam_ref, sg_ref, qc_ref, qd_ref, kc_ref, kd_ref, vt_ref, o_ref,
                    qm_s, m_s, acc_s, s_a, s_b):
    mq = qc_ref.shape[1]
    tk = vt_ref.shape[-1]
    lane = lax.broadcasted_iota(jnp.int32, (mq, LANES), 1)
    upper = lane >= HEAD_DIM
    for u in range(N_UNITS):
        src = qc_ref if u < 2 * N_C_HEADS else qd_ref
        pair = (u if u < 2 * N_C_HEADS else u - 2 * N_C_HEADS) // 2
        qp = src[0, :, pair * LANES:(pair + 1) * LANES]
        qm_s[u] = jnp.where(upper == (u % 2 == 1), qp, jnp.zeros_like(qp))
    m_s[...] = jnp.full(m_s.shape, NEG, F32)
    acc_s[...] = jnp.zeros(acc_s.shape, F32)

    def scores(tile, u, dst_s):
        is_c, slab, _, _ = _unit_operands(u)
        k_ref = kc_ref if is_c else kd_ref
        r0 = pl.multiple_of(tile * tk, tk)
        dst_s[u] = _dot_nt(k_ref[0, pl.ds(r0, tk), slab * LANES:(slab + 1) * LANES], qm_s[u])

    def consume(tile, u, src_s):
        _, _, row0, dv = _unit_operands(u)
        rows = dv + ONES_ROWS
        s = src_s[u]
        m_old = m_s[u]
        m_new = jnp.maximum(m_old, jnp.max(s, axis=0, keepdims=True))
        alpha = jnp.exp2(m_old - m_new)
        p = jnp.exp2(s - m_new).astype(BF16)
        pv = _dot(vt_ref[0, tile, row0:row0 + rows, :], p)
        acc_s[u, 0:rows, :] = acc_s[u, 0:rows, :] * alpha + pv
        m_s[u] = m_new

    def stage(tile, src_s, dst_s):
        nxt = jnp.minimum(tile + 1, nkt - 1)
        for u in range(N_UNITS):
            scores(nxt, u, dst_s)
            consume(tile, u, src_s)

    for u in range(N_UNITS):
        scores(0, u, s_a)
    stage(0, s_a, s_b)

    def body(pr, carry):
        stage(2 * pr + 1, s_b, s_a)
        stage(2 * pr + 2, s_a, s_b)
        return carry

    lax.fori_loop(0, (nkt - 1) // 2, body, 0)
    if (nkt - 1) % 2 == 1:
        stage(nkt - 1, s_b, s_a)

    lv = lam_ref[...]
    lam = (jnp.exp(jnp.sum(lv[0:1] * lv[1:2], axis=-1, keepdims=True))
           - jnp.exp(jnp.sum(lv[2:3] * lv[3:4], axis=-1, keepdims=True)) + lam_init)
    sg = sg_ref[...]
    dvc = 2 * HEAD_DIM
    for hc in range(N_C_HEADS):
        o1 = acc_s[2 * hc, 0:dvc, :] / acc_s[2 * hc, dvc:dvc + 1, :]
        o2 = acc_s[2 * hc + 1, 0:dvc, :] / acc_s[2 * hc + 1, dvc:dvc + 1, :]
        o = o1 - lam * o2
        ms = jnp.mean(o * o, axis=0, keepdims=True)
        o = o * lax.rsqrt(ms + RMS_EPS) * sg * (1.0 - lam_init)
        o_ref[0, :, hc * LANES:(hc + 1) * LANES] = o.T.astype(BF16)
    for pair in range(N_D_HEADS // 2):
        outs = []
        for half in range(2):
            u = 2 * N_C_HEADS + 2 * pair + half
            outs.append(acc_s[u, 0:HEAD_DIM, :] / acc_s[u, HEAD_DIM:HEAD_DIM + 1, :])
        o2 = jnp.concatenate(outs, axis=0)
        c0 = N_C_HEADS * LANES + pair * LANES
        o_ref[0, :, c0:c0 + LANES] = o2.T.astype(BF16)


def _attn_cd(lam_vec, sg, qc, qd, kc, kd, vt, nct, lam_init):
    b, tall, _ = qc.shape
    nkt, nv, tk = vt.shape[1], vt.shape[2], vt.shape[3]
    mq = TM
    nq = tall // mq - nct
    ow = N_C_HEADS * LANES + N_D_HEADS * HEAD_DIM
    qtile = lambda bi, i: (bi, i + nct, 0)
    return pl.pallas_call(
        functools.partial(_attn_cd_kernel, nkt, lam_init),
        grid=(b, nq),
        in_specs=[pl.BlockSpec((4, HEAD_DIM), lambda bi, i: (0, 0)),
                  pl.BlockSpec((LANES, 1), lambda bi, i: (0, 0)),
                  pl.BlockSpec((1, mq, qc.shape[-1]), qtile),
                  pl.BlockSpec((1, mq, qd.shape[-1]), qtile),
                  pl.BlockSpec((1, tall, kc.shape[-1]), lambda bi, i: (bi, 0, 0)),
                  pl.BlockSpec((1, tall, kd.shape[-1]), lambda bi, i: (bi, 0, 0)),
                  pl.BlockSpec((1, nkt, nv, tk), lambda bi, i: (bi, 0, 0, 0))],
        out_specs=pl.BlockSpec((1, mq, ow), lambda bi, i: (bi, i, 0)),
        out_shape=jax.ShapeDtypeStruct((b, nq * mq, ow), BF16),
        scratch_shapes=[pltpu.VMEM((N_UNITS, mq, LANES), BF16),
                        pltpu.VMEM((N_UNITS, 1, mq), F32),
                        pltpu.VMEM((N_UNITS, 2 * HEAD_DIM + ONES_ROWS, mq), F32),
                        pltpu.VMEM((N_UNITS, tk, mq), F32),
                        pltpu.VMEM((N_UNITS, tk, mq), F32)],
        compiler_params=_cparams(("parallel", "parallel")),
        name="attention_cd",
    )(lam_vec, sg, qc, qd, kc, kd, vt)


def _oproj_kernel(n_parts, widths, nct, alpha, n_exp, *refs):
    o_refs = refs[:n_parts]
    (w_ref, x_ref, ctx_ref, mlat_ref, mctx_ref, lng_ref, lnb_ref, rwt_ref, rb_ref,
     x1_ref, h2_ref, te_ref, tg_ref, n8_ref) = refs[n_parts:]
    d = x_ref.shape[-1]
    is_ctx = pl.program_id(1) < nct
    y = None
    r0 = 0
    for o_ref, wd in zip(o_refs, widths):
        part = _dot(o_ref[0], w_ref[r0:r0 + wd, :])
        y = part if y is None else y + part
        r0 += wd
    m = jnp.where(is_ctx, mctx_ref[...], mlat_ref[0])
    x = jnp.where(is_ctx, ctx_ref[0], x_ref[0])
    x1 = _layernorm(alpha * x + m[:, 2 * d:3 * d] * y, lng_ref[...], lnb_ref[...])
    x1_ref[0] = x1
    h2 = x1 * (1.0 + m[:, 4 * d:5 * d]) + m[:, 3 * d:4 * d]
    h2_ref[0] = h2
    logits = _dot_nt(rwt_ref[...], h2.astype(BF16)) + rb_ref[...]
    tm = logits.shape[1]
    row = lax.broadcasted_iota(jnp.int32, logits.shape, 0)
    vals, idxs = [], []
    for _ in range(TOP_K):
        mx = jnp.max(logits, axis=0, keepdims=True)
        idx = jnp.min(jnp.where(logits == mx, row, n_exp), axis=0, keepdims=True)
        vals.append(mx)
        idxs.append(idx)
        logits = jnp.where(row == idx, NEG, logits)
    es = [jnp.exp(v - vals[0]) for v in vals]
    tot = es[0] + es[1] + es[2] + es[3]
    pad_i = jnp.zeros((8 - TOP_K, tm), jnp.int32)
    pad_f = jnp.zeros((8 - TOP_K, tm), F32)
    te = jnp.concatenate(idxs + [pad_i], axis=0)
    te_ref[0] = te
    tg_ref[0] = jnp.concatenate([e / tot for e in es] + [pad_f], axis=0)
    _, oh = _expert_onehot(te, n_exp)
    n8 = _groups(jnp.sum(oh, axis=1, keepdims=True))
    n8_ref[0] = jnp.broadcast_to(n8, (n_exp, LANES)).astype(jnp.int32)


def _oproj(parts, w_out, x, ctx, mlat, mctx, lng, lnb, rwt, rb, nct_out, x_tile_off, alpha):
    b, t, _ = parts[0].shape
    d = x.shape[-1]
    nt = t // TM
    n_exp = rwt.shape[0]
    widths = tuple(p.shape[-1] for p in parts)
    tile = lambda bi, i: (bi, i, 0)
    full = lambda bi, i: (0, 0)
    stream, stream_specs = _stream_specs(x, ctx, nct_out, d, x_tile_off)
    in_specs = [pl.BlockSpec((1, TM, wd), tile) for wd in widths] + [
        pl.BlockSpec(w_out.shape, full)] + stream_specs + [
        pl.BlockSpec((1, 1, N_MOD * d), lambda bi, i: (bi, 0, 0)),
        pl.BlockSpec((1, N_MOD * d), full),
        pl.BlockSpec((1, d), full),
        pl.BlockSpec((1, d), full),
        pl.BlockSpec((n_exp, d), full),
        pl.BlockSpec((n_exp, 1), full)]
    return pl.pallas_call(
        functools.partial(_oproj_kernel, len(parts), widths, nct_out, alpha, n_exp),
        grid=(b, nt),
        in_specs=in_specs,
        out_specs=[pl.BlockSpec((1, TM, d), tile),
                   pl.BlockSpec((1, TM, d), tile),
                   pl.BlockSpec((1, 8, TM), lambda bi, i: (bi, 0, i)),
                   pl.BlockSpec((1, 8, TM), lambda bi, i: (bi, 0, i)),
                   pl.BlockSpec((1, n_exp, LANES), lambda bi, i: (bi * nt + i, 0, 0))],
        out_shape=[jax.ShapeDtypeStruct((b, t, d), F32),
                   jax.ShapeDtypeStruct((b, t, d), F32),
                   jax.ShapeDtypeStruct((b, 8, t), jnp.int32),
                   jax.ShapeDtypeStruct((b, 8, t), F32),
                   jax.ShapeDtypeStruct((b * nt, n_exp, LANES), jnp.int32)],
        compiler_params=_cparams(("parallel", "parallel")),
        name="out_proj_norm_router",
    )(*parts, w_out, *stream, mlat, mctx, lng, lnb, rwt, rb)


def _moe_kernel(be_ref, nx_ref, bv_ref, nu_ref, x_ref, wgu_hbm, bgu_ref, wd_hbm, bd_ref, y_ref,
                wgu_f, wd_f, wgu_s, wd_s, sem):
    blk = pl.program_id(0)
    ff = wd_s.shape[0]

    def fetch(e):
        return (pltpu.make_async_copy(wgu_hbm.at[e], wgu_f, sem.at[0]),
                pltpu.make_async_copy(wd_hbm.at[e], wd_f, sem.at[1]))

    @pl.when(blk < nu_ref[0])
    def _():
        e = be_ref[blk]
        new_expert = (blk == 0) | (e != be_ref[jnp.maximum(blk - 1, 0)])

        @pl.when(blk == 0)
        def _():
            for cp in fetch(e):
                cp.start()

        @pl.when(new_expert)
        def _():
            for cp in fetch(e):
                cp.wait()
            rows = 128

            def cast_gu(r, c):
                r0 = pl.multiple_of(r * rows, rows)
                wgu_s[pl.ds(r0, rows), :] = wgu_f[pl.ds(r0, rows), :].astype(BF16)
                return c

            def cast_d(r, c):
                r0 = pl.multiple_of(r * rows, rows)
                wd_s[pl.ds(r0, rows), :] = wd_f[pl.ds(r0, rows), :].astype(BF16)
                return c

            lax.fori_loop(0, wgu_s.shape[0] // rows, cast_gu, 0)
            lax.fori_loop(0, wd_s.shape[0] // rows, cast_d, 0)

            @pl.when(nx_ref[blk] != e)
            def _():
                for cp in fetch(nx_ref[blk]):
                    cp.start()

        def ffn(rows):
            gu = _dot(x_ref[rows, :].astype(BF16), wgu_s[...]) + bgu_ref[0]
            g = jnp.minimum(gu[:, :ff], SWIGLU_LIMIT)
            u = jnp.clip(gu[:, ff:], -SWIGLU_LIMIT, SWIGLU_LIMIT)
            act = (u + 1.0) * (g / (1.0 + jnp.exp(-SWIGLU_ALPHA * g)))
            y_ref[rows, :] = _dot(act.astype(BF16), wd_s[...]) + bd_ref[0]

        half = x_ref.shape[0] // 2

        @pl.when(bv_ref[blk] > half)
        def _():
            ffn(slice(None))

        @pl.when(bv_ref[blk] <= half)
        def _():
            ffn(slice(0, half))
            y_ref[half:, :] = jnp.zeros((half, y_ref.shape[1]), F32)

    @pl.when(blk >= nu_ref[0])
    def _():
        y_ref[...] = jnp.zeros(y_ref.shape, F32)


def _moe_experts(block_e, block_next_e, block_rows, n_used, xb, w_gu, b_gu, w_down, b_down):
    n_slots, d = xb.shape
    ff2 = w_gu.shape[-1]
    ff = w_down.shape[-2]
    n_exp = w_gu.shape[0] * w_gu.shape[1]
    w_gu = w_gu.reshape(n_exp, d, ff2)
    w_down = w_down.reshape(n_exp, ff, d)
    nb = n_slots // MOE_BM
    xblk = lambda i, be, nx, bv, nu: (jnp.minimum(i, nu[0] - 1), 0)
    bblk = lambda i, be, nx, bv, nu: (be[i], 0, 0)
    grid_spec = pltpu.PrefetchScalarGridSpec(
        num_scalar_prefetch=4,
        grid=(nb,),
        in_specs=[pl.BlockSpec((MOE_BM, d), xblk),
                  pl.BlockSpec(memory_space=pl.ANY),
                  pl.BlockSpec((1, 1, ff2), bblk),
                  pl.BlockSpec(memory_space=pl.ANY),
                  pl.BlockSpec((1, 1, d), bblk)],
        out_specs=pl.BlockSpec((MOE_BM, d), lambda i, be, nx, bv, nu: (i, 0)),
        scratch_shapes=[pltpu.VMEM((d, ff2), F32), pltpu.VMEM((ff, d), F32),
                        pltpu.VMEM((d, ff2), BF16), pltpu.VMEM((ff, d), BF16),
                        pltpu.SemaphoreType.DMA((2,))])
    return pl.pallas_call(
        _moe_kernel,
        grid_spec=grid_spec,
        out_shape=jax.ShapeDtypeStruct((n_slots, d), F32),
        compiler_params=_cparams(("arbitrary",)),
        name="moe_experts",
    )(block_e, block_next_e, block_rows, n_used, xb, w_gu, b_gu.reshape(n_exp, 1, ff2), w_down,
      b_down.reshape(n_exp, 1, d))


def _expert_onehot(te, n_exp):
    row = lax.broadcasted_iota(jnp.int32, (n_exp, te.shape[1]), 0)
    oh = jnp.zeros(row.shape, F32)
    for k in range(TOP_K):
        oh = oh + (row == te[k:k + 1, :]).astype(F32)
    return row, oh


def _groups(cnt):
    return jnp.floor((cnt + (ROW_GROUP - 1)) / ROW_GROUP)


def _route_pos_kernel(n_exp, te_ref, g0_ref, tri_ref, ltri_ref, pos_ref, tab_ref):
    te = te_ref[0]
    tm = te.shape[1]
    row, oh = _expert_onehot(te, n_exp)
    n8 = _groups(jnp.sum(oh, axis=1, keepdims=True))
    lbase = _dot(ltri_ref[...], jnp.broadcast_to(n8, (n_exp, LANES)).astype(BF16))[:, 0:1]
    before = _dot(oh.astype(BF16), tri_ref[...])
    where_to = ROW_GROUP * lbase + before
    pos = [jnp.sum(jnp.where(row == te[k:k + 1, :], where_to, 0.0), axis=0, keepdims=True)
           for k in range(TOP_K)]
    pad = jnp.zeros((8 - TOP_K, tm), F32)
    pos_ref[0] = jnp.concatenate(pos + [pad], axis=0).astype(jnp.int32)
    g = lax.broadcasted_iota(jnp.int32, (n_exp, tm), 1).astype(F32)
    inside = (g >= lbase) & (g < lbase + n8)
    gdst = jnp.sum(jnp.where(inside, g0_ref[0] + g - lbase, 0.0), axis=0, keepdims=True)
    ngrp = jnp.broadcast_to(jnp.sum(n8, axis=0, keepdims=True), (1, tm))
    tab_ref[0] = jnp.concatenate([gdst, ngrp, jnp.zeros((6, tm), F32)], axis=0).astype(jnp.int32)


def _route(te, n8, n_exp):
    b, _, t = te.shape
    nt = t // TM
    n_tiles = b * nt
    assert TOP_K * TM // ROW_GROUP + n_exp <= TM
    tile = pl.BlockSpec((1, 8, TM), lambda bi, i: (bi, 0, i))
    per_tile = lambda bi, i: (bi * nt + i, 0, 0)
    rows = ROW_GROUP * jnp.sum(n8, axis=0)
    padded = (rows + MOE_BM - 1) // MOE_BM * MOE_BM
    pend = jnp.cumsum(padded)
    g0 = (pend - padded)[None, :] // ROW_GROUP + jnp.cumsum(n8, axis=0) - n8
    tri = jnp.asarray(np.triu(np.ones((TM, TM)), 1), BF16)
    ltri = jnp.asarray(np.tril(np.ones((n_exp, n_exp)), -1), BF16)
    pos, tab = pl.pallas_call(
        functools.partial(_route_pos_kernel, n_exp),
        grid=(b, nt),
        in_specs=[tile,
                  pl.BlockSpec((1, n_exp, 1), per_tile),
                  pl.BlockSpec((TM, TM), lambda bi, i: (0, 0)),
                  pl.BlockSpec((n_exp, n_exp), lambda bi, i: (0, 0))],
        out_specs=[tile, pl.BlockSpec((1, 8, TM), per_tile)],
        out_shape=[jax.ShapeDtypeStruct((b, 8, t), jnp.int32),
                   jax.ShapeDtypeStruct((n_tiles, 8, TM), jnp.int32)],
        compiler_params=_cparams(("parallel", "parallel")),
        name="route_positions",
    )(te, g0.astype(F32).reshape(n_tiles, n_exp, 1), tri, ltri)
    gdst = jnp.pad(tab[:, 0, :], ((0, 0), (0, SMEM_BLOCK - TM))).reshape(-1)
    ngrp = tab[:, 1, 0]
    max_rows = b * t * TOP_K + n_tiles * n_exp * (ROW_GROUP - 1)
    n_blocks = -(-max_rows // MOE_BM) + n_exp
    n_used = (pend[-1] // MOE_BM).astype(jnp.int32)
    blk = jnp.minimum(jnp.arange(n_blocks, dtype=jnp.int32), n_used - 1) * MOE_BM
    block_e = jnp.minimum(jnp.sum((pend[None, :] <= blk[:, None]).astype(jnp.int32), axis=1), n_exp - 1)
    ids = jnp.arange(n_exp, dtype=jnp.int32)
    later = (padded[None, :] > 0) & (ids[None, :] > block_e[:, None])
    block_next_e = jnp.min(jnp.where(later, ids[None, :], n_exp), axis=1)
    block_next_e = jnp.where(block_next_e == n_exp, block_e, block_next_e).astype(jnp.int32)
    mine = (block_e[:, None] == ids[None, :]).astype(jnp.int32)
    first_blk = jnp.sum(mine * ((pend - padded) // MOE_BM)[None, :], axis=1)
    left = jnp.sum(mine * rows[None, :], axis=1) - (jnp.arange(n_blocks, dtype=jnp.int32) - first_blk) * MOE_BM
    block_rows = jnp.clip(left, 0, MOE_BM).astype(jnp.int32)
    return (pos, gdst, ngrp, pend.astype(jnp.int32), block_e, block_next_e, block_rows, n_used.reshape(1),
            n_blocks)


def _group_copy(src_ref, src_group, dst_ref, dst_group, sem):
    s0 = pl.multiple_of(src_group * ROW_GROUP, ROW_GROUP)
    d0 = pl.multiple_of(dst_group * ROW_GROUP, ROW_GROUP)
    return pltpu.make_async_copy(src_ref.at[pl.ds(s0, ROW_GROUP)], dst_ref.at[pl.ds(d0, ROW_GROUP)], sem)


def _tile_step():
    return pl.program_id(0) * pl.num_programs(1) + pl.program_id(1)


def _moe_scatter_kernel(n_exp, pend_ref, ngrp_ref, gdst_ref, gdst_prev_ref, pos_ref, h_ref, xb_ref,
                        xs, zero_s, sem, zsem):
    step = _tile_step()
    tm = h_ref.shape[1]

    @pl.when(step == 0)
    def _():
        zero_s[...] = jnp.zeros(zero_s.shape, zero_s.dtype)
        for phase in ("start", "wait"):
            for e in range(n_exp):
                lo = pend_ref[e - 1] if e else 0

                @pl.when(pend_ref[e] > lo)
                def _():
                    r0 = pl.multiple_of(pend_ref[e] - MOE_BM, MOE_BM)
                    cp = pltpu.make_async_copy(zero_s, xb_ref.at[pl.ds(r0, MOE_BM)], zsem)
                    cp.start() if phase == "start" else cp.wait()

            def unused_block(blk, c):
                r0 = pl.multiple_of(blk * MOE_BM, MOE_BM)
                cp = pltpu.make_async_copy(zero_s, xb_ref.at[pl.ds(r0, MOE_BM)], zsem)
                cp.start() if phase == "start" else cp.wait()
                return c

            lax.fori_loop(pend_ref[n_exp - 1] // MOE_BM, xb_ref.shape[0] // MOE_BM, unused_block, 0)

    par = step % 2
    pos = pos_ref[0]
    hb = h_ref[0].astype(BF16)
    for c in range(xs.shape[1] // tm):
        r = lax.broadcasted_iota(jnp.int32, (tm, tm), 0) + c * tm
        hit = r == pos[0:1, :]
        for k in range(1, TOP_K):
            hit = hit | (r == pos[k:k + 1, :])
        xs[par, c * tm:(c + 1) * tm, :] = _dot(jnp.where(hit, 1.0, 0.0).astype(BF16), hb)

    def groups(table, buf, count, wait):
        def body(g, c):
            cp = _group_copy(xs.at[buf], g, xb_ref, table[g], sem.at[buf])
            cp.wait() if wait else cp.start()
            return c
        lax.fori_loop(0, count, body, 0)

    groups(gdst_ref, par, ngrp_ref[step], False)

    @pl.when(step > 0)
    def _():
        groups(gdst_prev_ref, 1 - par, ngrp_ref[jnp.maximum(step - 1, 0)], True)

    @pl.when(step == pl.num_programs(0) * pl.num_programs(1) - 1)
    def _():
        groups(gdst_ref, par, ngrp_ref[step], True)


def _sort_rows(n_exp):
    return -(-(TOP_K * TM + n_exp * ROW_GROUP) // TM) * TM


def _moe_scatter(pend, ngrp, gdst, pos, h2, n_slots):
    b, t, d = h2.shape
    nt = t // TM
    n_exp = pend.shape[0]
    grid_spec = pltpu.PrefetchScalarGridSpec(
        num_scalar_prefetch=2,
        grid=(b, nt),
        in_specs=[pl.BlockSpec((SMEM_BLOCK,), lambda bi, i, pe, ng: (bi * nt + i,), memory_space=pltpu.SMEM),
                  pl.BlockSpec((SMEM_BLOCK,), lambda bi, i, pe, ng: (jnp.maximum(bi * nt + i - 1, 0),),
                               memory_space=pltpu.SMEM),
                  pl.BlockSpec((1, 8, TM), lambda bi, i, pe, ng: (bi, 0, i)),
                  pl.BlockSpec((1, TM, d), lambda bi, i, pe, ng: (bi, i, 0))],
        out_specs=pl.BlockSpec(memory_space=pl.ANY),
        scratch_shapes=[pltpu.VMEM((2, _sort_rows(n_exp), d), F32), pltpu.VMEM((MOE_BM, d), F32),
                        pltpu.SemaphoreType.DMA((2,)), pltpu.SemaphoreType.DMA])
    return pl.pallas_call(
        functools.partial(_moe_scatter_kernel, n_exp),
        grid_spec=grid_spec,
        out_shape=jax.ShapeDtypeStruct((n_slots, d), F32),
        compiler_params=_cparams(("arbitrary", "arbitrary")),
        name="moe_scatter_rows",
    )(pend, ngrp, gdst, gdst, pos, h2)


def _combine_kernel(nct, alpha, ngrp_ref, gdst_ref, gdst_next_ref, yb_ref, pos_ref, tg_ref, x1_ref,
                    mlat_ref, mctx_ref, lng_ref, lnb_ref, o_ref, ys, sem):
    d = x1_ref.shape[-1]
    tm = x1_ref.shape[1]
    step = _tile_step()
    n_steps = pl.num_programs(0) * pl.num_programs(1)
    par = step % 2

    def groups(table, buf, count, wait):
        def body(g, c):
            cp = _group_copy(yb_ref, table[g], ys.at[buf], g, sem.at[buf])
            cp.wait() if wait else cp.start()
            return c
        lax.fori_loop(0, count, body, 0)

    @pl.when(step == 0)
    def _():
        ys[...] = jnp.zeros(ys.shape, F32)
        groups(gdst_ref, 0, ngrp_ref[0], False)

    @pl.when(step + 1 < n_steps)
    def _():
        groups(gdst_next_ref, 1 - par, ngrp_ref[jnp.minimum(step + 1, n_steps - 1)], False)

    groups(gdst_ref, par, ngrp_ref[step], True)
    is_ctx = pl.program_id(1) < nct
    m = jnp.where(is_ctx, mctx_ref[...], mlat_ref[0])
    pos = pos_ref[0]
    gates = tg_ref[0]
    f = None
    for c in range(ys.shape[1] // tm):
        col = lax.broadcasted_iota(jnp.int32, (tm, tm), 1) + c * tm
        w = jnp.where(col == pos[:, 0:1], gates[:, 0:1], 0.0)
        for k in range(1, TOP_K):
            w = w + jnp.where(col == pos[:, k:k + 1], gates[:, k:k + 1], 0.0)
        part = _dot(w.astype(BF16), ys[par, c * tm:(c + 1) * tm, :].astype(BF16))
        f = part if f is None else f + part
    o_ref[0] = _layernorm(alpha * x1_ref[0] + m[:, 5 * d:6 * d] * f, lng_ref[...], lnb_ref[...])


def _combine_postnorm(ngrp, gdst, yb, pos_t, tg_t, x1, mlat, mctx, lng, lnb, nct, alpha, n_exp):
    b, t, d = x1.shape
    nt = t // TM
    tile = lambda bi, i, ng: (bi, i, 0)
    full = lambda bi, i, ng: (0, 0)
    grid_spec = pltpu.PrefetchScalarGridSpec(
        num_scalar_prefetch=1,
        grid=(b, nt),
        in_specs=[pl.BlockSpec((SMEM_BLOCK,), lambda bi, i, ng: (bi * nt + i,), memory_space=pltpu.SMEM),
                  pl.BlockSpec((SMEM_BLOCK,), lambda bi, i, ng: (jnp.minimum(bi * nt + i + 1, b * nt - 1),),
                               memory_space=pltpu.SMEM),
                  pl.BlockSpec(memory_space=pl.ANY),
                  pl.BlockSpec((1, TM, 8), tile),
                  pl.BlockSpec((1, TM, 8), tile),
                  pl.BlockSpec((1, TM, d), tile),
                  pl.BlockSpec((1, 1, N_MOD * d), lambda bi, i, ng: (bi, 0, 0)),
                  pl.BlockSpec((1, N_MOD * d), full),
                  pl.BlockSpec((1, d), full),
                  pl.BlockSpec((1, d), full)],
        out_specs=pl.BlockSpec((1, TM, d), tile),
        scratch_shapes=[pltpu.VMEM((2, _sort_rows(n_exp), d), F32), pltpu.SemaphoreType.DMA((2,))])
    return pl.pallas_call(
        functools.partial(_combine_kernel, nct, alpha),
        grid_spec=grid_spec,
        out_shape=jax.ShapeDtypeStruct((b, t, d), F32),
        compiler_params=_cparams(("arbitrary", "arbitrary")),
        name="combine_post_norm",
    )(ngrp, gdst, gdst, yb, pos_t, tg_t, x1, mlat, mctx, lng, lnb)


def _rope_tables(ctx_len, seq):
    t = np.arange(seq)
    row = (t // GRID_W).astype(np.float64)
    col = (t % GRID_W).astype(np.float64)
    nf = HEAD_DIM // 4
    inv = ROPE_THETA ** (-np.arange(nf, dtype=np.float64) / nf)
    ar = row[:, None] * inv[None, :]
    ac = col[:, None] * inv[None, :]
    ang = np.concatenate([ar, ar, ac, ac], axis=-1)
    cos = np.concatenate([np.ones((ctx_len, HEAD_DIM)), np.cos(ang)], axis=0)
    sin = np.concatenate([np.zeros((ctx_len, HEAD_DIM)), np.sin(ang)], axis=0)
    sign = np.where((np.arange(HEAD_DIM) % 32) < 16, -1.0, 1.0)[None, :]
    cos2 = np.tile(cos, (1, LANES // HEAD_DIM))
    sin2 = np.tile(sin * sign, (1, LANES // HEAD_DIM))
    return jnp.asarray(cos2, F32), jnp.asarray(sin2, F32)


def _channel_dft():
    c = np.arange(HEAD_DIM)
    ang = 2.0 * np.pi * ((c[:, None] * c[None, :]) % HEAD_DIM) / HEAD_DIM
    eye = np.eye(4)
    cs = np.concatenate([np.kron(eye, np.cos(ang)), np.kron(eye, np.sin(ang))], axis=1)
    return jnp.asarray(cs, BF16)


def _group_mean_matrix():
    bd = np.kron(np.eye(LANES // HEAD_DIM), np.full((HEAD_DIM, HEAD_DIM), 1.0 / HEAD_DIM))
    return jnp.asarray(bd, BF16)


def _dup_heads(wk, n_heads):
    d = wk.shape[0]
    return jnp.broadcast_to(wk.reshape(d, n_heads, 1, HEAD_DIM), (d, n_heads, 2, HEAD_DIM)).reshape(d, n_heads * LANES)


def _lambda_init(layer):
    return 0.8 - 0.6 * math.exp(-0.3 * layer)


def _moe_postnorm(h2, te, tg, n8, x1, mlat, mctx, lng, lnb, nct, alpha, layer, w_gu, b_gu, w_down, b_down):
    n_exp = w_gu.shape[1]
    pos, gdst, ngrp, pend, block_e, block_next_e, block_rows, n_used, n_blocks = _route(te, n8[:, :, 0], n_exp)
    xb = _moe_scatter(pend, ngrp, gdst, pos, h2, n_blocks * MOE_BM)
    off = layer * n_exp
    yb = _moe_experts(block_e + off, block_next_e + off, block_rows, n_used, xb, w_gu, b_gu, w_down, b_down)
    pos_t, tg_t = jnp.transpose(pos, (0, 2, 1)), jnp.transpose(tg, (0, 2, 1))
    return _combine_postnorm(ngrp, gdst, yb, pos_t, tg_t, x1, mlat, mctx, lng, lnb, nct, alpha, n_exp)


def kernel(x, c, ctx, c_ctx, mod_w, mod_b, ln_g, ln_b, ab_w_in, ab_sink, ab_w_out,
           cd_w_in, cd_lambda, cd_subln_g, cd_q_norm_g, cd_k_norm_g, cd_w_out,
           router_w, router_b, expert_w_gu, expert_b_gu, expert_w_down, expert_b_down):
    b, s, d = x.shape
    n_ctx = ctx.shape[1]
    depth = mod_w.shape[0]
    n_exp = router_w.shape[-1]
    assert d == 16 * HEAD_DIM and n_ctx % TM == 0 and s % TM == 0 and s % GRID_W == 0
    nct = n_ctx // TM
    alpha = (2 * depth) ** 0.25

    cos, sin = _rope_tables(n_ctx, s)
    cs_dft = _channel_dft()
    bd = _group_mean_matrix()
    mods = _mod_vectors(c, c_ctx, mod_w, mod_b)

    xs, xc = x, ctx
    for l in range(depth):
        last = l == depth - 1
        i = l // 2
        mlat = mods[l, :b].reshape(b, 1, N_MOD * d)
        mctx = mods[l, b:b + 1]
        lng1, lnb1 = ln_g[l, 0].reshape(1, d), ln_b[l, 0].reshape(1, d)
        lng2, lnb2 = ln_g[l, 1].reshape(1, d), ln_b[l, 1].reshape(1, d)
        rwt = router_w[l].T.astype(BF16)
        rb = router_b[l].reshape(n_exp, 1)
        if l % 2 == 0:
            w = ab_w_in[i]
            w_n = jnp.concatenate([w[:, :1024], _dup_heads(w[:, 1024:1280], 4)], axis=1).astype(BF16)
            wvt = w[:, 1280:1536].T.astype(BF16)
            uw, q, k, vt = _proj_ab(xs, xc, mlat, mctx, w_n, wvt, cos, sin, cs_dft, nct)
            oa = jnp.concatenate([_fourier(uw[:, :n_ctx]), _fourier(uw[:, n_ctx:])], axis=1)
            ob = _win_attn(ab_sink[i], q, k, vt, nct)
            if last:
                parts = [oa[:, n_ctx:], ob[:, n_ctx:]]
            else:
                parts = [oa, ob]
            w_out = ab_w_out[i].astype(BF16)
        else:
            w = cd_w_in[i]
            w_n = jnp.concatenate([w[:, :1536], _dup_heads(w[:, 1536:1664], 2)], axis=1).astype(BF16)
            wvt = w[:, 1664:2304].T.astype(BF16)
            gq = jnp.tile(cd_q_norm_g[i], 2).reshape(1, LANES)
            gk = jnp.tile(cd_k_norm_g[i], 2).reshape(1, LANES)
            qc, qd, kc, kd, vt = _proj_cd(xs, xc, mlat, mctx, w_n, wvt, cos, sin, bd, gq, gk, nct)
            sg = cd_subln_g[i].reshape(LANES, 1)
            o_lat = _attn_cd(cd_lambda[i], sg, qc, qd, kc, kd, vt, nct, _lambda_init(l))
            if last:
                parts = [o_lat]
            else:
                raise NotImplementedError("context outputs of a differential/axial layer")
            w_out = cd_w_out[i].astype(BF16)
        experts = (l, expert_w_gu, expert_b_gu, expert_w_down, expert_b_down)
        if last:
            x1, h2, te, tg, n8 = _oproj(parts, w_out, xs, xc, mlat, mctx, lng1, lnb1, rwt, rb, 0, nct, alpha)
            return _moe_postnorm(h2, te, tg, n8, x1, mlat, mctx, lng2, lnb2, 0, alpha, *experts)
        x1, h2, te, tg, n8 = _oproj(parts, w_out, xs, xc, mlat, mctx, lng1, lnb1, rwt, rb, nct, 0, alpha)
        xs, xc = _moe_postnorm(h2, te, tg, n8, x1, mlat, mctx, lng2, lnb2, nct, alpha, *experts), None
    return xs[:, n_ctx:]
```

```python
import functools
import math

import jax
import jax.numpy as jnp
import numpy as np
from jax import lax
from jax.experimental import pallas as pl
from jax.experimental.pallas import tpu as pltpu

F32 = jnp.float32
BF16 = jnp.bfloat16

HEAD_DIM = 64
GRID_W = 64
WINDOW = 128
ROPE_THETA = 10000.0
LN_EPS = 1e-6
RMS_EPS = 1e-6
N_MOD = 6
TOP_K = 4
SWIGLU_LIMIT = 7.0
SWIGLU_ALPHA = 1.702
NEG = -1e30
LOG2E = 1.4426950408889634
QSCALE = HEAD_DIM ** -0.5 * LOG2E

N_C_HEADS = 4
N_D_HEADS = 8
N_D_KV = 2
N_UNITS = 2 * N_C_HEADS + N_D_HEADS
ONES_ROWS = 16
_VT_LAYOUT = tuple((h * (2 * HEAD_DIM + ONES_ROWS), 2 * HEAD_DIM) for h in range(N_C_HEADS)) + tuple(
    (N_C_HEADS * (2 * HEAD_DIM + ONES_ROWS) + g * (HEAD_DIM + ONES_ROWS), HEAD_DIM) for g in range(N_D_KV))
VT_ROWS = _VT_LAYOUT[-1][0] + HEAD_DIM + ONES_ROWS

LANES = 128
TM = 256
MOE_BM = 512
ROW_GROUP = 8
SMEM_BLOCK = 1024
VMEM_LIMIT = 56 * 1024 * 1024


def _dot(a, b):
    return jnp.dot(a, b, preferred_element_type=F32)


def _dot_nt(a, b):
    return lax.dot_general(a, b, (((1,), (1,)), ((), ())), preferred_element_type=F32)


def _cparams(sem):
    return pltpu.CompilerParams(dimension_semantics=sem, vmem_limit_bytes=VMEM_LIMIT)


def _layernorm(z, g, b):
    mu = jnp.mean(z, axis=-1, keepdims=True)
    d = z - mu
    var = jnp.mean(d * d, axis=-1, keepdims=True)
    return d * lax.rsqrt(var + LN_EPS) * g + b


def _rope128(x, cos, sin_signed):
    lane = lax.broadcasted_iota(jnp.int32, x.shape, 1)
    lo = (lane % 32) < 16
    rot = jnp.where(lo, pltpu.roll(x, LANES - 16, 1), pltpu.roll(x, 16, 1))
    return x * cos + rot * sin_signed


def _rmsnorm128(x, g, bd):
    x2 = x * x
    hi = x2.astype(BF16)
    lo = (x2 - hi.astype(F32)).astype(BF16)
    ms = _dot(hi, bd) + _dot(lo, bd)
    return x * lax.rsqrt(ms + RMS_EPS) * g


def _modulated(x_ref, ctx_ref, mlat_ref, mctx_ref, is_ctx, d):
    m = jnp.where(is_ctx, mctx_ref[...], mlat_ref[0])
    x = jnp.where(is_ctx, ctx_ref[0], x_ref[0])
    return (x * (1.0 + m[:, d:2 * d]) + m[:, 0:d]).astype(BF16)


def _stream_specs(x, ctx, nct, d, off=0):
    if ctx is None:
        return (x, x), [pl.BlockSpec((1, TM, d), lambda bi, i: (bi, i + off, 0)),
                        pl.BlockSpec((1, TM, d), lambda bi, i: (bi, jnp.minimum(i, max(nct - 1, 0)), 0))]
    return (x, ctx), [pl.BlockSpec((1, TM, d), lambda bi, i: (bi, jnp.maximum(i - nct, 0), 0)),
                      pl.BlockSpec((1, TM, d), lambda bi, i: (bi, jnp.minimum(i, nct - 1), 0))]


def _mod_kernel(c_ref, w_ref, b_ref, o_ref):
    c = c_ref[...]
    s = (c / (1.0 + jnp.exp(-c))).astype(BF16)
    o_ref[0] = _dot(s, w_ref[0].astype(BF16)) + b_ref[0]


def _mod_vectors(c, c_ctx, mod_w, mod_b):
    depth, d, n = mod_w.shape
    b = c.shape[0]
    rows = 8 * (-(-(b + 1) // 8))
    cs = jnp.zeros((rows, d), F32).at[:b].set(c).at[b].set(c_ctx)
    tn = 1536
    out = pl.pallas_call(
        _mod_kernel,
        grid=(depth, n // tn),
        in_specs=[pl.BlockSpec((rows, d), lambda l, j: (0, 0)),
                  pl.BlockSpec((1, d, tn), lambda l, j: (l, 0, j)),
                  pl.BlockSpec((1, 1, tn), lambda l, j: (l, 0, j))],
        out_specs=pl.BlockSpec((1, rows, tn), lambda l, j: (l, 0, j)),
        out_shape=jax.ShapeDtypeStruct((depth, rows, n), F32),
        compiler_params=_cparams(("parallel", "parallel")),
        name="mod_vectors",
    )(cs, mod_w, mod_b.reshape(depth, 1, n))
    return out


def _proj_ab_kernel(nct, x_ref, ctx_ref, mlat_ref, mctx_ref, w_ref, wvt_ref, cos_ref, sin_ref, cs_ref,
                    uw_ref, q_ref, k_ref, vt_ref):
    is_ctx = pl.program_id(1) < nct
    d = x_ref.shape[-1]
    h = _modulated(x_ref, ctx_ref, mlat_ref, mctx_ref, is_ctx, d)
    p = _dot(h, w_ref[...])
    uw_ref[0] = _dot(p[:, 0:256].astype(BF16), cs_ref[...]).astype(BF16)
    cos = cos_ref[...]
    sin = sin_ref[...]
    for j in range(6):
        c0 = 256 + j * LANES
        q_ref[0, :, j * LANES:(j + 1) * LANES] = (_rope128(p[:, c0:c0 + LANES], cos, sin) * QSCALE).astype(BF16)
    for j in range(4):
        c0 = 1024 + j * LANES
        k_ref[0, :, j * LANES:(j + 1) * LANES] = _rope128(p[:, c0:c0 + LANES], cos, sin).astype(BF16)
    vt = _dot_nt(wvt_ref[...], h).astype(BF16)
    for c in range(TM // LANES):
        vt_ref[0, c] = vt[:, c * LANES:(c + 1) * LANES]


def _proj_ab(x, ctx, mlat, mctx, w, wvt, cos, sin, cs, nct):
    b, d = x.shape[0], x.shape[2]
    tall = x.shape[1] + (0 if ctx is None else ctx.shape[1])
    stream, stream_specs = _stream_specs(x, ctx, nct, d)
    nt = tall // TM
    nw = w.shape[1]
    nchunk = tall // LANES
    cpt = TM // LANES
    return pl.pallas_call(
        functools.partial(_proj_ab_kernel, nct),
        grid=(b, nt),
        in_specs=stream_specs + [
                  pl.BlockSpec((1, 1, N_MOD * d), lambda bi, i: (bi, 0, 0)),
                  pl.BlockSpec((1, N_MOD * d), lambda bi, i: (0, 0)),
                  pl.BlockSpec((d, nw), lambda bi, i: (0, 0)),
                  pl.BlockSpec((256, d), lambda bi, i: (0, 0)),
                  pl.BlockSpec((TM, LANES), lambda bi, i: (i, 0)),
                  pl.BlockSpec((TM, LANES), lambda bi, i: (i, 0)),
                  pl.BlockSpec((256, 512), lambda bi, i: (0, 0))],
        out_specs=[pl.BlockSpec((1, TM, 512), lambda bi, i: (bi, i, 0)),
                   pl.BlockSpec((1, TM, 768), lambda bi, i: (bi, i, 0)),
                   pl.BlockSpec((1, TM, 512), lambda bi, i: (bi, i, 0)),
                   pl.BlockSpec((1, cpt, 256, LANES), lambda bi, i: (bi, i, 0, 0))],
        out_shape=[jax.ShapeDtypeStruct((b, tall, 512), BF16),
                   jax.ShapeDtypeStruct((b, tall, 768), BF16),
                   jax.ShapeDtypeStruct((b, tall, 512), BF16),
                   jax.ShapeDtypeStruct((b, nchunk, 256, LANES), BF16)],
        compiler_params=_cparams(("parallel", "parallel")),
        name="proj_ab",
    )(*stream, mlat, mctx, w, wvt, cos, sin, cs)


def _proj_cd_kernel(nct, x_ref, ctx_ref, mlat_ref, mctx_ref, w_ref, wvt_ref, cos_ref, sin_ref, bd_ref,
                    gq_ref, gk_ref, qc_ref, qd_ref, kc_ref, kd_ref, vt_ref):
    is_ctx = pl.program_id(1) < nct
    d = x_ref.shape[-1]
    h = _modulated(x_ref, ctx_ref, mlat_ref, mctx_ref, is_ctx, d)
    p = _dot(h, w_ref[...])
    cos = cos_ref[...]
    sin = sin_ref[...]
    bd = bd_ref[...]
    gq = gq_ref[...]
    gk = gk_ref[...]
    for j in range(4):
        sl = slice(j * LANES, (j + 1) * LANES)
        qc_ref[0, :, sl] = (_rope128(p[:, j * LANES:(j + 1) * LANES], cos, sin) * QSCALE).astype(BF16)
        c0 = 512 + j * LANES
        qd_ref[0, :, sl] = (_rope128(_rmsnorm128(p[:, c0:c0 + LANES], gq, bd), cos, sin) * QSCALE).astype(BF16)
        c0 = 1024 + j * LANES
        kc_ref[0, :, sl] = _rope128(p[:, c0:c0 + LANES], cos, sin).astype(BF16)
    for j in range(2):
        c0 = 1536 + j * LANES
        kd_ref[0, :, j * LANES:(j + 1) * LANES] = _rope128(
            _rmsnorm128(p[:, c0:c0 + LANES], gk, bd), cos, sin).astype(BF16)
    vt = _dot_nt(wvt_ref[...], h).astype(BF16)
    ones = jnp.ones((ONES_ROWS, vt.shape[1]), BF16)
    src = 0
    for dst, dv in _VT_LAYOUT:
        vt_ref[0, 0, dst:dst + dv, :] = vt[src:src + dv, :]
        vt_ref[0, 0, dst + dv:dst + dv + ONES_ROWS, :] = ones
        src += dv


def _proj_cd(x, ctx, mlat, mctx, w, wvt, cos, sin, bd, gq, gk, nct):
    b, d = x.shape[0], x.shape[2]
    tall = x.shape[1] + (0 if ctx is None else ctx.shape[1])
    stream, stream_specs = _stream_specs(x, ctx, nct, d)
    nt = tall // TM
    nw = w.shape[1]
    nv = VT_ROWS
    full = lambda bi, i: (0, 0)
    tile = lambda bi, i: (bi, i, 0)
    return pl.pallas_call(
        functools.partial(_proj_cd_kernel, nct),
        grid=(b, nt),
        in_specs=stream_specs + [
                  pl.BlockSpec((1, 1, N_MOD * d), lambda bi, i: (bi, 0, 0)),
                  pl.BlockSpec((1, N_MOD * d), full),
                  pl.BlockSpec((d, nw), full),
                  pl.BlockSpec(wvt.shape, full),
                  pl.BlockSpec((TM, LANES), lambda bi, i: (i, 0)),
                  pl.BlockSpec((TM, LANES), lambda bi, i: (i, 0)),
                  pl.BlockSpec((LANES, LANES), full),
                  pl.BlockSpec((1, LANES), full),
                  pl.BlockSpec((1, LANES), full)],
        out_specs=[pl.BlockSpec((1, TM, 512), tile),
                   pl.BlockSpec((1, TM, 512), tile),
                   pl.BlockSpec((1, TM, 512), tile),
                   pl.BlockSpec((1, TM, 256), tile),
                   pl.BlockSpec((1, 1, nv, TM), lambda bi, i: (bi, i, 0, 0))],
        out_shape=[jax.ShapeDtypeStruct((b, tall, 512), BF16),
                   jax.ShapeDtypeStruct((b, tall, 512), BF16),
                   jax.ShapeDtypeStruct((b, tall, 512), BF16),
                   jax.ShapeDtypeStruct((b, tall, 256), BF16),
                   jax.ShapeDtypeStruct((b, nt, nv, TM), BF16)],
        compiler_params=_cparams(("parallel", "parallel")),
        name="proj_cd",
    )(*stream, mlat, mctx, w, wvt, cos, sin, bd, gq, gk)


def _fourier_kernel(nb, scale, uw_ref, ca_ref, sa_ref, cb_ref, sb_ref, o_ref):
    j = pl.program_id(0)
    ca = ca_ref[pl.ds(j, 1), :]
    sa = sa_ref[pl.ds(j, 1), :]
    cb = cb_ref[...]
    sb = sb_ref[...]
    ct = (ca * cb - sa * sb).astype(BF16)
    nst = (-(sa * cb + ca * sb)).astype(BF16)
    for bi in range(nb):
        acc = _dot(ct, uw_ref[bi, :, 0:256]) + _dot(nst, uw_ref[bi, :, 256:512])
        o_ref[bi] = (acc * scale).astype(BF16)


def _dft_tables(t, tmf):
    k = np.arange(t, dtype=np.int64)
    j1 = np.arange(t // tmf, dtype=np.int64) * tmf
    j0 = np.arange(tmf, dtype=np.int64)
    aa = (2.0 * np.pi / t) * ((j1[:, None] * k[None, :]) % t)
    ab = (2.0 * np.pi / t) * ((j0[:, None] * k[None, :]) % t)
    f = lambda a: jnp.asarray(a, F32)
    return f(np.cos(aa)), f(np.sin(aa)), f(np.cos(ab)), f(np.sin(ab))


def _fourier(uw):
    b, t, _ = uw.shape
    tmf = min(128, t)
    ca, sa, cb, sb = _dft_tables(t, tmf)
    scale = 1.0 / math.sqrt(t * HEAD_DIM)
    full2 = lambda j: (0, 0)
    return pl.pallas_call(
        functools.partial(_fourier_kernel, b, scale),
        grid=(t // tmf,),
        in_specs=[pl.BlockSpec((b, t, 512), lambda j: (0, 0, 0)),
                  pl.BlockSpec((t // tmf, t), full2),
                  pl.BlockSpec((t // tmf, t), full2),
                  pl.BlockSpec((tmf, t), full2),
                  pl.BlockSpec((tmf, t), full2)],
        out_specs=pl.BlockSpec((b, tmf, 256), lambda j: (0, j, 0)),
        out_shape=jax.ShapeDtypeStruct((b, t, 256), BF16),
        compiler_params=_cparams(("parallel",)),
        name="fourier_mix",
    )(uw, ca, sa, cb, sb)


def _win_attn_kernel(nct, nchunk, sink_ref, q_ref, k_ref, vt_ref, o_ref, s_a, s_b):
    i = pl.program_id(1)
    is_ctx = i < nct
    j = i - nct
    ctx_rows = nct * TM
    ctx_chunks = ctx_rows // LANES
    lw = TM + 2 * WINDOW
    lchunks = lw // LANES
    cs = jnp.clip(ctx_chunks + (TM // LANES) * j - WINDOW // LANES, 0, nchunk - lchunks)
    rs = pl.multiple_of(cs * LANES, LANES)
    qpos = j * TM + lax.broadcasted_iota(jnp.int32, (1, TM), 1)
    kpos = cs * LANES - ctx_rows + lax.broadcasted_iota(jnp.int32, (lw, 1), 0)
    allowed = (jnp.abs(qpos - kpos) <= WINDOW) & (kpos >= 0) & jnp.logical_not(is_ctx)
    lane = lax.broadcasted_iota(jnp.int32, (TM, LANES), 1)
    n_heads = q_ref.shape[-1] // HEAD_DIM
    group = n_heads // (k_ref.shape[-1] // LANES)

    def scores(hq, dst_s):
        pair, half = divmod(hq, 2)
        g = hq // group
        qp = q_ref[0, :, pair * LANES:(pair + 1) * LANES]
        qm = jnp.where((lane >= HEAD_DIM) == (half == 1), qp, jnp.zeros_like(qp))
        dst_s[0:ctx_rows, :] = _dot_nt(k_ref[0, 0:ctx_rows, g * LANES:(g + 1) * LANES], qm)
        s_l = _dot_nt(k_ref[0, pl.ds(rs, lw), g * LANES:(g + 1) * LANES], qm)
        dst_s[ctx_rows:ctx_rows + lw, :] = jnp.where(allowed, s_l, NEG)

    def consume(hq, src_s):
        g = hq // group
        s = src_s[...]
        sk = sink_ref[hq] * LOG2E
        m = jnp.maximum(jnp.max(s, axis=0, keepdims=True), sk)
        p = jnp.exp2(s - m)
        l = jnp.sum(p, axis=0, keepdims=True) + jnp.exp2(sk - m)
        p = p.astype(BF16)
        vrows = slice(g * HEAD_DIM, (g + 1) * HEAD_DIM)
        acc = jnp.zeros((HEAD_DIM, TM), F32)
        for c in range(ctx_chunks):
            acc = acc + _dot(vt_ref[0, c, vrows, :], p[c * LANES:(c + 1) * LANES, :])
        for c in range(lchunks):
            r0 = ctx_rows + c * LANES
            acc = acc + _dot(vt_ref[0, cs + c, vrows, :], p[r0:r0 + LANES, :])
        return acc / l

    bufs = (s_a, s_b)
    scores(0, s_a)
    outs = []
    for hq in range(n_heads):
        if hq + 1 < n_heads:
            scores(hq + 1, bufs[(hq + 1) % 2])
        outs.append(consume(hq, bufs[hq % 2]))
        if hq % 2 == 1:
            pair = hq // 2
            o2 = jnp.concatenate(outs, axis=0)
            o_ref[0, :, pair * LANES:(pair + 1) * LANES] = o2.T.astype(BF16)
            outs = []


def _win_attn(sink, q, k, vt, nct):
    b, tall, qw = q.shape
    nt = tall // TM
    nchunk = vt.shape[1]
    return pl.pallas_call(
        functools.partial(_win_attn_kernel, nct, nchunk),
        grid=(b, nt),
        in_specs=[pl.BlockSpec(memory_space=pltpu.SMEM),
                  pl.BlockSpec((1, TM, qw), lambda bi, i: (bi, i, 0)),
                  pl.BlockSpec((1, tall, k.shape[-1]), lambda bi, i: (bi, 0, 0)),
                  pl.BlockSpec((1, nchunk, vt.shape[2], LANES), lambda bi, i: (bi, 0, 0, 0))],
        out_specs=pl.BlockSpec((1, TM, qw), lambda bi, i: (bi, i, 0)),
        out_shape=jax.ShapeDtypeStruct((b, tall, qw), BF16),
        scratch_shapes=[pltpu.VMEM((nct * TM + TM + 2 * WINDOW, TM), F32)] * 2,
        compiler_params=_cparams(("parallel", "parallel")),
        name="window_attention",
    )(sink, q, k, vt)


def _unit_operands(u):
    if u < 2 * N_C_HEADS:
        hc = u // 2
        dst, dv = _VT_LAYOUT[hc]
        return True, hc, dst, dv
    g = (u - 2 * N_C_HEADS) // (N_D_HEADS // N_D_KV)
    dst, dv = _VT_LAYOUT[N_C_HEADS + g]
    return False, g, dst, dv


def _attn_cd_kernel(nkt, lam_init, lam_ref, sg_ref, qc_ref, qd_ref, kc_ref, kd_ref, vt_ref, o_ref,
                    qm_s, m_s, acc_s, s_a, s_b):
    mq = qc_ref.shape[1]
    tk = vt_ref.shape[-1]
    lane = lax.broadcasted_iota(jnp.int32, (mq, LANES), 1)
    upper = lane >= HEAD_DIM
    for u in range(N_UNITS):
        src = qc_ref if u < 2 * N_C_HEADS else qd_ref
        pair = (u if u < 2 * N_C_HEADS else u - 2 * N_C_HEADS) // 2
        qp = src[0, :, pair * LANES:(pair + 1) * LANES]
        qm_s[u] = jnp.where(upper == (u % 2 == 1), qp, jnp.zeros_like(qp))
    m_s[...] = jnp.full(m_s.shape, NEG, F32)
    acc_s[...] = jnp.zeros(acc_s.shape, F32)

    def scores(tile, u, dst_s):
        is_c, slab, _, _ = _unit_operands(u)
        k_ref = kc_ref if is_c else kd_ref
        r0 = pl.multiple_of(tile * tk, tk)
        dst_s[u] = _dot_nt(k_ref[0, pl.ds(r0, tk), slab * LANES:(slab + 1) * LANES], qm_s[u])

    def consume(tile, u, src_s):
        _, _, row0, dv = _unit_operands(u)
        rows = dv + ONES_ROWS
        m_old = m_s[u]
        m_new = jnp.maximum(m_old, jnp.max(src_s[u], axis=0, keepdims=True))
        alpha = jnp.exp2(m_old - m_new)
        p = jnp.exp2(src_s[u] - m_new).astype(BF16)
        pv = _dot(vt_ref[0, tile, row0:row0 + rows, :], p)
        acc_s[u, 0:rows, :] = acc_s[u, 0:rows, :] * alpha + pv
        m_s[u] = m_new

    def stage(tile, src_s, dst_s):
        nxt = jnp.minimum(tile + 1, nkt - 1)
        for u in range(N_UNITS):
            scores(nxt, u, dst_s)
            consume(tile, u, src_s)

    for u in range(N_UNITS):
        scores(0, u, s_a)
    stage(0, s_a, s_b)

    def body(pr, carry):
        stage(2 * pr + 1, s_b, s_a)
        stage(2 * pr + 2, s_a, s_b)
        return carry

    lax.fori_loop(0, (nkt - 1) // 2, body, 0)
    if (nkt - 1) % 2 == 1:
        stage(nkt - 1, s_b, s_a)

    lv = lam_ref[...]
    lam = (jnp.exp(jnp.sum(lv[0:1] * lv[1:2], axis=-1, keepdims=True))
           - jnp.exp(jnp.sum(lv[2:3] * lv[3:4], axis=-1, keepdims=True)) + lam_init)
    sg = sg_ref[...]
    dvc = 2 * HEAD_DIM
    for hc in range(N_C_HEADS):
        o1 = acc_s[2 * hc, 0:dvc, :] / acc_s[2 * hc, dvc:dvc + 1, :]
        o2 = acc_s[2 * hc + 1, 0:dvc, :] / acc_s[2 * hc + 1, dvc:dvc + 1, :]
        o = o1 - lam * o2
        ms = jnp.mean(o * o, axis=0, keepdims=True)
        o = o * lax.rsqrt(ms + RMS_EPS) * sg * (1.0 - lam_init)
        o_ref[0, :, hc * LANES:(hc + 1) * LANES] = o.T.astype(BF16)
    for pair in range(N_D_HEADS // 2):
        outs = []
        for half in range(2):
            u = 2 * N_C_HEADS + 2 * pair + half
            outs.append(acc_s[u, 0:HEAD_DIM, :] / acc_s[u, HEAD_DIM:HEAD_DIM + 1, :])
        o2 = jnp.concatenate(outs, axis=0)
        c0 = N_C_HEADS * LANES + pair * LANES
        o_ref[0, :, c0:c0 + LANES] = o2.T.astype(BF16)


def _attn_cd(lam_vec, sg, qc, qd, kc, kd, vt, nct, lam_init):
    b, tall, _ = qc.shape
    nkt, nv, tk = vt.shape[1], vt.shape[2], vt.shape[3]
    mq = TM
    nq = tall // mq - nct
    ow = N_C_HEADS * LANES + N_D_HEADS * HEAD_DIM
    qtile = lambda bi, i: (bi, i + nct, 0)
    return pl.pallas_call(
        functools.partial(_attn_cd_kernel, nkt, lam_init),
        grid=(b, nq),
        in_specs=[pl.BlockSpec((4, HEAD_DIM), lambda bi, i: (0, 0)),
                  pl.BlockSpec((LANES, 1), lambda bi, i: (0, 0)),
                  pl.BlockSpec((1, mq, qc.shape[-1]), qtile),
                  pl.BlockSpec((1, mq, qd.shape[-1]), qtile),
                  pl.BlockSpec((1, tall, kc.shape[-1]), lambda bi, i: (bi, 0, 0)),
                  pl.BlockSpec((1, tall, kd.shape[-1]), lambda bi, i: (bi, 0, 0)),
                  pl.BlockSpec((1, nkt, nv, tk), lambda bi, i: (bi, 0, 0, 0))],
        out_specs=pl.BlockSpec((1, mq, ow), lambda bi, i: (bi, i, 0)),
        out_shape=jax.ShapeDtypeStruct((b, nq * mq, ow), BF16),
        scratch_shapes=[pltpu.VMEM((N_UNITS, mq, LANES), BF16),
                        pltpu.VMEM((N_UNITS, 1, mq), F32),
                        pltpu.VMEM((N_UNITS, 2 * HEAD_DIM + ONES_ROWS, mq), F32),
                        pltpu.VMEM((N_UNITS, tk, mq), F32),
                        pltpu.VMEM((N_UNITS, tk, mq), F32)],
        compiler_params=_cparams(("parallel", "parallel")),
        name="attention_cd",
    )(lam_vec, sg, qc, qd, kc, kd, vt)


def _oproj_kernel(n_parts, widths, nct, alpha, n_exp, *refs):
    o_refs = refs[:n_parts]
    (w_ref, x_ref, ctx_ref, mlat_ref, mctx_ref, lng_ref, lnb_ref, rwt_ref, rb_ref,
     x1_ref, h2_ref, te_ref, tg_ref, n8_ref) = refs[n_parts:]
    d = x_ref.shape[-1]
    is_ctx = pl.program_id(1) < nct
    y = None
    r0 = 0
    for o_ref, wd in zip(o_refs, widths):
        part = _dot(o_ref[0], w_ref[r0:r0 + wd, :])
        y = part if y is None else y + part
        r0 += wd
    m = jnp.where(is_ctx, mctx_ref[...], mlat_ref[0])
    x = jnp.where(is_ctx, ctx_ref[0], x_ref[0])
    x1 = _layernorm(alpha * x + m[:, 2 * d:3 * d] * y, lng_ref[...], lnb_ref[...])
    x1_ref[0] = x1
    h2 = x1 * (1.0 + m[:, 4 * d:5 * d]) + m[:, 3 * d:4 * d]
    h2_ref[0] = h2
    logits = _dot_nt(rwt_ref[...], h2.astype(BF16)) + rb_ref[...]
    tm = logits.shape[1]
    row = lax.broadcasted_iota(jnp.int32, logits.shape, 0)
    vals, idxs = [], []
    for _ in range(TOP_K):
        mx = jnp.max(logits, axis=0, keepdims=True)
        idx = jnp.min(jnp.where(logits == mx, row, n_exp), axis=0, keepdims=True)
        vals.append(mx)
        idxs.append(idx)
        logits = jnp.where(row == idx, NEG, logits)
    es = [jnp.exp(v - vals[0]) for v in vals]
    tot = es[0] + es[1] + es[2] + es[3]
    pad_i = jnp.zeros((8 - TOP_K, tm), jnp.int32)
    pad_f = jnp.zeros((8 - TOP_K, tm), F32)
    te = jnp.concatenate(idxs + [pad_i], axis=0)
    te_ref[0] = te
    tg_ref[0] = jnp.concatenate([e / tot for e in es] + [pad_f], axis=0)
    _, oh = _expert_onehot(te, n_exp)
    n8 = _groups(jnp.sum(oh, axis=1, keepdims=True))
    n8_ref[0] = jnp.broadcast_to(n8, (n_exp, LANES)).astype(jnp.int32)


def _oproj(parts, w_out, x, ctx, mlat, mctx, lng, lnb, rwt, rb, nct_out, x_tile_off, alpha):
    b, t, _ = parts[0].shape
    d = x.shape[-1]
    nt = t // TM
    n_exp = rwt.shape[0]
    widths = tuple(p.shape[-1] for p in parts)
    tile = lambda bi, i: (bi, i, 0)
    full = lambda bi, i: (0, 0)
    stream, stream_specs = _stream_specs(x, ctx, nct_out, d, x_tile_off)
    in_specs = [pl.BlockSpec((1, TM, wd), tile) for wd in widths] + [
        pl.BlockSpec(w_out.shape, full)] + stream_specs + [
        pl.BlockSpec((1, 1, N_MOD * d), lambda bi, i: (bi, 0, 0)),
        pl.BlockSpec((1, N_MOD * d), full),
        pl.BlockSpec((1, d), full),
        pl.BlockSpec((1, d), full),
        pl.BlockSpec((n_exp, d), full),
        pl.BlockSpec((n_exp, 1), full)]
    return pl.pallas_call(
        functools.partial(_oproj_kernel, len(parts), widths, nct_out, alpha, n_exp),
        grid=(b, nt),
        in_specs=in_specs,
        out_specs=[pl.BlockSpec((1, TM, d), tile),
                   pl.BlockSpec((1, TM, d), tile),
                   pl.BlockSpec((1, 8, TM), lambda bi, i: (bi, 0, i)),
                   pl.BlockSpec((1, 8, TM), lambda bi, i: (bi, 0, i)),
                   pl.BlockSpec((1, n_exp, LANES), lambda bi, i: (bi * nt + i, 0, 0))],
        out_shape=[jax.ShapeDtypeStruct((b, t, d), F32),
                   jax.ShapeDtypeStruct((b, t, d), F32),
                   jax.ShapeDtypeStruct((b, 8, t), jnp.int32),
                   jax.ShapeDtypeStruct((b, 8, t), F32),
                   jax.ShapeDtypeStruct((b * nt, n_exp, LANES), jnp.int32)],
        compiler_params=_cparams(("parallel", "parallel")),
        name="out_proj_norm_router",
    )(*parts, w_out, *stream, mlat, mctx, lng, lnb, rwt, rb)


def _moe_kernel(be_ref, nx_ref, bv_ref, nu_ref, rprev_ref, rcur_ref, rnext_ref, xs_hbm, wgu_hbm, bgu_ref,
                wd_hbm, bd_ref, ys_in_hbm, ys_hbm, wgu_f, wd_f, wgu_s, wd_s, xbuf, ybuf, sem, gsem, ssem):
    del ys_in_hbm
    blk = pl.program_id(0)
    ff = wd_s.shape[0]
    gpb = xbuf.shape[1] // ROW_GROUP
    par = blk % 2

    def gathers(rec, buf, wait):
        for g in range(gpb):
            cp = _group_copy(xs_hbm, rec[g], xbuf.at[buf], g, gsem.at[buf])
            cp.wait() if wait else cp.start()

    def scatters(rec, buf, wait):
        for g in range(gpb):
            cp = _group_copy(ybuf.at[buf], g, ys_hbm, rec[gpb + g], ssem.at[buf])
            cp.wait() if wait else cp.start()

    def fetch(e):
        return (pltpu.make_async_copy(wgu_hbm.at[e], wgu_f, sem.at[0]),
                pltpu.make_async_copy(wd_hbm.at[e], wd_f, sem.at[1]))

    @pl.when(blk < nu_ref[0])
    def _():
        e = be_ref[blk]
        new_expert = (blk == 0) | (e != be_ref[jnp.maximum(blk - 1, 0)])

        @pl.when(blk == 0)
        def _():
            for cp in fetch(e):
                cp.start()

        @pl.when(new_expert)
        def _():
            for cp in fetch(e):
                cp.wait()
            rows = 128

            def cast_gu(r, c):
                r0 = pl.multiple_of(r * rows, rows)
                wgu_s[pl.ds(r0, rows), :] = wgu_f[pl.ds(r0, rows), :].astype(BF16)
                return c

            def cast_d(r, c):
                r0 = pl.multiple_of(r * rows, rows)
                wd_s[pl.ds(r0, rows), :] = wd_f[pl.ds(r0, rows), :].astype(BF16)
                return c

            lax.fori_loop(0, wgu_s.shape[0] // rows, cast_gu, 0)
            lax.fori_loop(0, wd_s.shape[0] // rows, cast_d, 0)

            @pl.when(nx_ref[blk] != e)
            def _():
                for cp in fetch(nx_ref[blk]):
                    cp.start()

        @pl.when(blk == 0)
        def _():
            gathers(rcur_ref, 0, False)

        gathers(rcur_ref, par, True)
        half = xbuf.shape[1] // 2

        def ffn(rows):
            gathers(rnext_ref, 1 - par, False)
            gu = _dot(xbuf[par, rows, :].astype(BF16), wgu_s[...]) + bgu_ref[0]
            g = jnp.minimum(gu[:, :ff], SWIGLU_LIMIT)
            u = jnp.clip(gu[:, ff:], -SWIGLU_LIMIT, SWIGLU_LIMIT)
            act = (u + 1.0) * (g / (1.0 + jnp.exp(-SWIGLU_ALPHA * g)))
            ybuf[par, rows, :] = _dot(act.astype(BF16), wd_s[...]) + bd_ref[0]
            if rows != slice(None):
                ybuf[par, half:, :] = jnp.zeros((half, ybuf.shape[2]), F32)
            scatters(rcur_ref, par, False)

        @pl.when(bv_ref[blk] > half)
        def _():
            ffn(slice(None))

        @pl.when(bv_ref[blk] <= half)
        def _():
            ffn(slice(0, half))

        @pl.when(blk > 0)
        def _():
            scatters(rprev_ref, 1 - par, True)

        @pl.when(blk == nu_ref[0] - 1)
        def _():
            gathers(rnext_ref, 1 - par, True)
            scatters(rcur_ref, par, True)


def _moe_experts(place, layer, xs, ys, w_gu, b_gu, w_down, b_down):
    d = xs.shape[1]
    ff2 = w_gu.shape[-1]
    ff = w_down.shape[-2]
    off = layer * w_gu.shape[1]
    n_exp = w_gu.shape[0] * w_gu.shape[1]
    w_gu = w_gu.reshape(n_exp, d, ff2)
    w_down = w_down.reshape(n_exp, ff, d)
    nb = place["n_blocks"]
    last = lambda nu: nu[0] - 1
    bblk = lambda i, be, nx, bv, nu: (be[i], 0, 0)
    rec = lambda f: pl.BlockSpec((SMEM_BLOCK,), f, memory_space=pltpu.SMEM)
    grid_spec = pltpu.PrefetchScalarGridSpec(
        num_scalar_prefetch=4,
        grid=(nb,),
        in_specs=[rec(lambda i, be, nx, bv, nu: (jnp.clip(i - 1, 0, last(nu)),)),
                  rec(lambda i, be, nx, bv, nu: (jnp.minimum(i, last(nu)),)),
                  rec(lambda i, be, nx, bv, nu: (jnp.minimum(i + 1, last(nu)),)),
                  pl.BlockSpec(memory_space=pl.ANY),
                  pl.BlockSpec(memory_space=pl.ANY),
                  pl.BlockSpec((1, 1, ff2), bblk),
                  pl.BlockSpec(memory_space=pl.ANY),
                  pl.BlockSpec((1, 1, d), bblk),
                  pl.BlockSpec(memory_space=pl.ANY)],
        out_specs=pl.BlockSpec(memory_space=pl.ANY),
        scratch_shapes=[pltpu.VMEM((d, ff2), F32), pltpu.VMEM((ff, d), F32),
                        pltpu.VMEM((d, ff2), BF16), pltpu.VMEM((ff, d), BF16),
                        pltpu.VMEM((2, MOE_BM, d), F32), pltpu.VMEM((2, MOE_BM, d), F32),
                        pltpu.SemaphoreType.DMA((2,)), pltpu.SemaphoreType.DMA((2,)),
                        pltpu.SemaphoreType.DMA((2,))])
    records = place["rec"]
    return pl.pallas_call(
        _moe_kernel,
        grid_spec=grid_spec,
        out_shape=jax.ShapeDtypeStruct(ys.shape, F32),
        input_output_aliases={12: 0},
        compiler_params=_cparams(("arbitrary",)),
        name="moe_experts",
    )(place["block_e"] + off, place["block_next_e"] + off, place["block_rows"], place["n_used"],
      records, records, records, xs, w_gu, b_gu.reshape(n_exp, 1, ff2), w_down, b_down.reshape(n_exp, 1, d), ys)


def _expert_onehot(te, n_exp):
    row = lax.broadcasted_iota(jnp.int32, (n_exp, te.shape[1]), 0)
    oh = jnp.zeros(row.shape, F32)
    for k in range(TOP_K):
        oh = oh + (row == te[k:k + 1, :]).astype(F32)
    return row, oh


def _groups(cnt):
    return jnp.floor((cnt + (ROW_GROUP - 1)) / ROW_GROUP)


def _route_pos_kernel(n_exp, te_ref, tri_ref, ltri_ref, pos_ref):
    te = te_ref[0]
    tm = te.shape[1]
    row, oh = _expert_onehot(te, n_exp)
    n8 = _groups(jnp.sum(oh, axis=1, keepdims=True))
    lbase = _dot(ltri_ref[...], jnp.broadcast_to(n8, (n_exp, LANES)).astype(BF16))[:, 0:1]
    before = _dot(oh.astype(BF16), tri_ref[...])
    where_to = ROW_GROUP * lbase + before
    pos = [jnp.sum(jnp.where(row == te[k:k + 1, :], where_to, 0.0), axis=0, keepdims=True)
           for k in range(TOP_K)]
    pad = jnp.zeros((8 - TOP_K, tm), F32)
    pos_ref[0] = jnp.concatenate(pos + [pad], axis=0).astype(jnp.int32)


def _route_positions(te, n_exp):
    b, _, t = te.shape
    tile = pl.BlockSpec((1, 8, TM), lambda bi, i: (bi, 0, i))
    tri = jnp.asarray(np.triu(np.ones((TM, TM)), 1), BF16)
    ltri = jnp.asarray(np.tril(np.ones((n_exp, n_exp)), -1), BF16)
    return pl.pallas_call(
        functools.partial(_route_pos_kernel, n_exp),
        grid=(b, t // TM),
        in_specs=[tile,
                  pl.BlockSpec((TM, TM), lambda bi, i: (0, 0)),
                  pl.BlockSpec((n_exp, n_exp), lambda bi, i: (0, 0))],
        out_specs=tile,
        out_shape=jax.ShapeDtypeStruct((b, 8, t), jnp.int32),
        compiler_params=_cparams(("parallel", "parallel")),
        name="route_positions",
    )(te, tri, ltri)


def _placement(n8, n_asg):
    n_tiles, n_exp = n8.shape
    gpt = _sort_rows(n_exp) // ROW_GROUP
    gpb = MOE_BM // ROW_GROUP
    assert (TOP_K * TM + n_exp * (ROW_GROUP - 1)) // ROW_GROUP < gpt
    ids = jnp.arange(n_exp, dtype=jnp.int32)
    lbase = jnp.cumsum(n8, axis=1) - n8
    cum_incl = jnp.cumsum(n8, axis=0)
    cum_excl = cum_incl - n8
    groups = cum_incl[-1]
    padded = (groups + gpb - 1) // gpb * gpb
    pend = jnp.cumsum(padded)
    pstart = pend - padded
    n_blocks = -(-(n_asg + n_tiles * n_exp * (ROW_GROUP - 1)) // MOE_BM) + n_exp
    n_used = (pend[-1] // gpb).astype(jnp.int32)
    blocks = jnp.arange(n_blocks, dtype=jnp.int32)
    blk = jnp.minimum(blocks, n_used - 1) * gpb
    block_e = jnp.minimum(jnp.sum((pend[None, :] <= blk[:, None]).astype(jnp.int32), axis=1), n_exp - 1)
    later = (padded[None, :] > 0) & (ids[None, :] > block_e[:, None])
    block_next_e = jnp.min(jnp.where(later, ids[None, :], n_exp), axis=1)
    block_next_e = jnp.where(block_next_e == n_exp, block_e, block_next_e).astype(jnp.int32)
    j = jnp.arange(n_blocks * gpb, dtype=jnp.int32)
    e_j = block_e[j // gpb]
    j_e = j - pstart[e_j]
    valid = (j_e < groups[e_j]) & (j // gpb < n_used)
    tile = jnp.sum((cum_incl.T[e_j] <= j_e[:, None]).astype(jnp.int32), axis=1)
    tile = jnp.minimum(tile, n_tiles - 1)
    at = tile * n_exp + e_j
    held = tile * gpt + lbase.reshape(-1)[at] + j_e - cum_excl.reshape(-1)[at]
    src = jnp.where(valid, held, gpt - 1)
    dst = jnp.where(valid, held, n_tiles * gpt + j % gpb)
    rec = jnp.zeros((n_blocks, SMEM_BLOCK), jnp.int32)
    rec = rec.at[:, 0:gpb].set(src.reshape(n_blocks, gpb)).at[:, gpb:2 * gpb].set(dst.reshape(n_blocks, gpb))
    left = groups[block_e] - (blocks - pstart[block_e] // gpb) * gpb
    block_rows = (jnp.clip(left, 0, gpb) * ROW_GROUP).astype(jnp.int32)
    ngrp = jnp.sum(n8, axis=1).astype(jnp.int32)
    return dict(rec=rec.reshape(-1), block_e=block_e, block_next_e=block_next_e, block_rows=block_rows,
                n_used=n_used.reshape(1), n_blocks=n_blocks, ngrp=ngrp)


def _group_copy(src_ref, src_group, dst_ref, dst_group, sem):
    s0 = pl.multiple_of(src_group * ROW_GROUP, ROW_GROUP)
    d0 = pl.multiple_of(dst_group * ROW_GROUP, ROW_GROUP)
    return pltpu.make_async_copy(src_ref.at[pl.ds(s0, ROW_GROUP)], dst_ref.at[pl.ds(d0, ROW_GROUP)], sem)


def _tile_step():
    return pl.program_id(0) * pl.num_programs(1) + pl.program_id(1)


def _moe_sort_kernel(ngrp_ref, pos_ref, h_ref, xs_ref, ys_hbm, zero_s, sem):
    step = _tile_step()
    tm = h_ref.shape[1]
    gpt = xs_ref.shape[0] // ROW_GROUP
    pos = pos_ref[0]
    hb = h_ref[0].astype(BF16)
    for c in range(xs_ref.shape[0] // tm):
        r = lax.broadcasted_iota(jnp.int32, (tm, tm), 0) + c * tm
        hit = r == pos[0:1, :]
        for k in range(1, TOP_K):
            hit = hit | (r == pos[k:k + 1, :])
        xs_ref[c * tm:(c + 1) * tm, :] = _dot(jnp.where(hit, 1.0, 0.0).astype(BF16), hb)

    @pl.when(step == 0)
    def _():
        zero_s[...] = jnp.zeros(zero_s.shape, F32)

    def clear(wait):
        def body(g, c):
            r0 = pl.multiple_of((step * gpt + g) * ROW_GROUP, ROW_GROUP)
            cp = pltpu.make_async_copy(zero_s, ys_hbm.at[pl.ds(r0, ROW_GROUP)], sem)
            cp.wait() if wait else cp.start()
            return c
        lax.fori_loop(ngrp_ref[step], gpt, body, 0)

    clear(False)
    clear(True)


def _sort_rows(n_exp):
    return -(-(TOP_K * TM + n_exp * ROW_GROUP) // TM) * TM


def _moe_sort(ngrp, pos, h2, n_exp):
    b, t, d = h2.shape
    nt = t // TM
    sr = _sort_rows(n_exp)
    grid_spec = pltpu.PrefetchScalarGridSpec(
        num_scalar_prefetch=1,
        grid=(b, nt),
        in_specs=[pl.BlockSpec((1, 8, TM), lambda bi, i, ng: (bi, 0, i)),
                  pl.BlockSpec((1, TM, d), lambda bi, i, ng: (bi, i, 0))],
        out_specs=[pl.BlockSpec((sr, d), lambda bi, i, ng: (bi * nt + i, 0)),
                   pl.BlockSpec(memory_space=pl.ANY)],
        scratch_shapes=[pltpu.VMEM((ROW_GROUP, d), F32), pltpu.SemaphoreType.DMA])
    return pl.pallas_call(
        _moe_sort_kernel,
        grid_spec=grid_spec,
        out_shape=[jax.ShapeDtypeStruct((b * nt * sr, d), F32),
                   jax.ShapeDtypeStruct(((b * nt + 1) * sr, d), F32)],
        compiler_params=_cparams(("arbitrary", "arbitrary")),
        name="moe_sort_rows",
    )(ngrp, pos, h2)


def _combine_kernel(nct, alpha, ys_ref, pos_ref, tg_ref, x1_ref, mlat_ref, mctx_ref, lng_ref, lnb_ref, o_ref):
    d = x1_ref.shape[-1]
    tm = x1_ref.shape[1]
    is_ctx = pl.program_id(1) < nct
    m = jnp.where(is_ctx, mctx_ref[...], mlat_ref[0])
    pos = pos_ref[0]
    gates = tg_ref[0]
    f = None
    for c in range(ys_ref.shape[0] // tm):
        col = lax.broadcasted_iota(jnp.int32, (tm, tm), 1) + c * tm
        w = jnp.where(col == pos[:, 0:1], gates[:, 0:1], 0.0)
        for k in range(1, TOP_K):
            w = w + jnp.where(col == pos[:, k:k + 1], gates[:, k:k + 1], 0.0)
        part = _dot(w.astype(BF16), ys_ref[c * tm:(c + 1) * tm, :].astype(BF16))
        f = part if f is None else f + part
    o_ref[0] = _layernorm(alpha * x1_ref[0] + m[:, 5 * d:6 * d] * f, lng_ref[...], lnb_ref[...])


def _combine_postnorm(ys, pos_t, tg_t, x1, mlat, mctx, lng, lnb, nct, alpha, n_exp):
    b, t, d = x1.shape
    nt = t // TM
    tile = lambda bi, i: (bi, i, 0)
    full = lambda bi, i: (0, 0)
    return pl.pallas_call(
        functools.partial(_combine_kernel, nct, alpha),
        grid=(b, nt),
        in_specs=[pl.BlockSpec((_sort_rows(n_exp), d), lambda bi, i: (bi * nt + i, 0)),
                  pl.BlockSpec((1, TM, 8), tile),
                  pl.BlockSpec((1, TM, 8), tile),
                  pl.BlockSpec((1, TM, d), tile),
                  pl.BlockSpec((1, 1, N_MOD * d), lambda bi, i: (bi, 0, 0)),
                  pl.BlockSpec((1, N_MOD * d), full),
                  pl.BlockSpec((1, d), full),
                  pl.BlockSpec((1, d), full)],
        out_specs=pl.BlockSpec((1, TM, d), tile),
        out_shape=jax.ShapeDtypeStruct((b, t, d), F32),
        compiler_params=_cparams(("parallel", "parallel")),
        name="combine_post_norm",
    )(ys, pos_t, tg_t, x1, mlat, mctx, lng, lnb)


def _rope_tables(ctx_len, seq):
    t = np.arange(seq)
    row = (t // GRID_W).astype(np.float64)
    col = (t % GRID_W).astype(np.float64)
    nf = HEAD_DIM // 4
    inv = ROPE_THETA ** (-np.arange(nf, dtype=np.float64) / nf)
    ar = row[:, None] * inv[None, :]
    ac = col[:, None] * inv[None, :]
    ang = np.concatenate([ar, ar, ac, ac], axis=-1)
    cos = np.concatenate([np.ones((ctx_len, HEAD_DIM)), np.cos(ang)], axis=0)
    sin = np.concatenate([np.zeros((ctx_len, HEAD_DIM)), np.sin(ang)], axis=0)
    sign = np.where((np.arange(HEAD_DIM) % 32) < 16, -1.0, 1.0)[None, :]
    cos2 = np.tile(cos, (1, LANES // HEAD_DIM))
    sin2 = np.tile(sin * sign, (1, LANES // HEAD_DIM))
    return jnp.asarray(cos2, F32), jnp.asarray(sin2, F32)


def _channel_dft():
    c = np.arange(HEAD_DIM)
    ang = 2.0 * np.pi * ((c[:, None] * c[None, :]) % HEAD_DIM) / HEAD_DIM
    eye = np.eye(4)
    cs = np.concatenate([np.kron(eye, np.cos(ang)), np.kron(eye, np.sin(ang))], axis=1)
    return jnp.asarray(cs, BF16)


def _group_mean_matrix():
    bd = np.kron(np.eye(LANES // HEAD_DIM), np.full((HEAD_DIM, HEAD_DIM), 1.0 / HEAD_DIM))
    return jnp.asarray(bd, BF16)


def _dup_heads(wk, n_heads):
    d = wk.shape[0]
    return jnp.broadcast_to(wk.reshape(d, n_heads, 1, HEAD_DIM), (d, n_heads, 2, HEAD_DIM)).reshape(d, n_heads * LANES)


def _lambda_init(layer):
    return 0.8 - 0.6 * math.exp(-0.3 * layer)


def _moe_postnorm(h2, te, tg, n8, x1, mlat, mctx, lng, lnb, nct, alpha, layer, w_gu, b_gu, w_down, b_down):
    n_exp = w_gu.shape[1]
    b, t, _ = h2.shape
    place = _placement(n8[:, :, 0], b * t * TOP_K)
    pos = _route_positions(te, n_exp)
    xs, ys = _moe_sort(place["ngrp"], pos, h2, n_exp)
    ys = _moe_experts(place, layer, xs, ys, w_gu, b_gu, w_down, b_down)
    pos_t, tg_t = jnp.transpose(pos, (0, 2, 1)), jnp.transpose(tg, (0, 2, 1))
    return _combine_postnorm(ys, pos_t, tg_t, x1, mlat, mctx, lng, lnb, nct, alpha, n_exp)


def kernel(x, c, ctx, c_ctx, mod_w, mod_b, ln_g, ln_b, ab_w_in, ab_sink, ab_w_out,
           cd_w_in, cd_lambda, cd_subln_g, cd_q_norm_g, cd_k_norm_g, cd_w_out,
           router_w, router_b, expert_w_gu, expert_b_gu, expert_w_down, expert_b_down):
    b, s, d = x.shape
    n_ctx = ctx.shape[1]
    depth = mod_w.shape[0]
    n_exp = router_w.shape[-1]
    assert d == 16 * HEAD_DIM and n_ctx % TM == 0 and s % TM == 0 and s % GRID_W == 0
    nct = n_ctx // TM
    alpha = (2 * depth) ** 0.25

    cos, sin = _rope_tables(n_ctx, s)
    cs_dft = _channel_dft()
    bd = _group_mean_matrix()
    mods = _mod_vectors(c, c_ctx, mod_w, mod_b)

    xs, xc = x, ctx
    for l in range(depth):
        last = l == depth - 1
        i = l // 2
        mlat = mods[l, :b].reshape(b, 1, N_MOD * d)
        mctx = mods[l, b:b + 1]
        lng1, lnb1 = ln_g[l, 0].reshape(1, d), ln_b[l, 0].reshape(1, d)
        lng2, lnb2 = ln_g[l, 1].reshape(1, d), ln_b[l, 1].reshape(1, d)
        rwt = router_w[l].T.astype(BF16)
        rb = router_b[l].reshape(n_exp, 1)
        if l % 2 == 0:
            w = ab_w_in[i]
            w_n = jnp.concatenate([w[:, :1024], _dup_heads(w[:, 1024:1280], 4)], axis=1).astype(BF16)
            wvt = w[:, 1280:1536].T.astype(BF16)
            uw, q, k, vt = _proj_ab(xs, xc, mlat, mctx, w_n, wvt, cos, sin, cs_dft, nct)
            oa = jnp.concatenate([_fourier(uw[:, :n_ctx]), _fourier(uw[:, n_ctx:])], axis=1)
            ob = _win_attn(ab_sink[i], q, k, vt, nct)
            if last:
                parts = [oa[:, n_ctx:], ob[:, n_ctx:]]
            else:
                parts = [oa, ob]
            w_out = ab_w_out[i].astype(BF16)
        else:
            w = cd_w_in[i]
            w_n = jnp.concatenate([w[:, :1536], _dup_heads(w[:, 1536:1664], 2)], axis=1).astype(BF16)
            wvt = w[:, 1664:2304].T.astype(BF16)
            gq = jnp.tile(cd_q_norm_g[i], 2).reshape(1, LANES)
            gk = jnp.tile(cd_k_norm_g[i], 2).reshape(1, LANES)
            qc, qd, kc, kd, vt = _proj_cd(xs, xc, mlat, mctx, w_n, wvt, cos, sin, bd, gq, gk, nct)
            sg = cd_subln_g[i].reshape(LANES, 1)
            o_lat = _attn_cd(cd_lambda[i], sg, qc, qd, kc, kd, vt, nct, _lambda_init(l))
            if last:
                parts = [o_lat]
            else:
                raise NotImplementedError("context outputs of a differential/axial layer")
            w_out = cd_w_out[i].astype(BF16)
        experts = (l, expert_w_gu, expert_b_gu, expert_w_down, expert_b_down)
        if last:
            x1, h2, te, tg, n8 = _oproj(parts, w_out, xs, xc, mlat, mctx, lng1, lnb1, rwt, rb, 0, nct, alpha)
            return _moe_postnorm(h2, te, tg, n8, x1, mlat, mctx, lng2, lnb2, 0, alpha, *experts)
        x1, h2, te, tg, n8 = _oproj(parts, w_out, xs, xc, mlat, mctx, lng1, lnb1, rwt, rb, nct, 0, alpha)
        xs, xc = _moe_postnorm(h2, te, tg, n8, x1, mlat, mctx, lng2, lnb2, nct, alpha, *experts), None
    return xs[:, n_ctx:]
```

```python
import functools
import math

import jax
import jax.numpy as jnp
import numpy as np
from jax import lax
from jax.experimental import pallas as pl
from jax.experimental.pallas import tpu as pltpu

F32 = jnp.float32
BF16 = jnp.bfloat16

HEAD_DIM = 64
GRID_W = 64
WINDOW = 128
ROPE_THETA = 10000.0
LN_EPS = 1e-6
RMS_EPS = 1e-6
N_MOD = 6
TOP_K = 4
SWIGLU_LIMIT = 7.0
SWIGLU_ALPHA = 1.702
NEG = -1e30
LOG2E = 1.4426950408889634
QSCALE = HEAD_DIM ** -0.5 * LOG2E

N_C_HEADS = 4
N_D_HEADS = 8
N_D_KV = 2
N_UNITS = 2 * N_C_HEADS + N_D_HEADS
ONES_ROWS = 16
_VT_LAYOUT = tuple((h * (2 * HEAD_DIM + ONES_ROWS), 2 * HEAD_DIM) for h in range(N_C_HEADS)) + tuple(
    (N_C_HEADS * (2 * HEAD_DIM + ONES_ROWS) + g * (HEAD_DIM + ONES_ROWS), HEAD_DIM) for g in range(N_D_KV))
VT_ROWS = _VT_LAYOUT[-1][0] + HEAD_DIM + ONES_ROWS

LANES = 128
TM = 256
MOE_BM = 512
ROW_GROUP = 8
SMEM_BLOCK = 1024
VMEM_LIMIT = 56 * 1024 * 1024


def _dot(a, b):
    return jnp.dot(a, b, preferred_element_type=F32)


def _dot_nt(a, b):
    return lax.dot_general(a, b, (((1,), (1,)), ((), ())), preferred_element_type=F32)


def _cparams(sem):
    return pltpu.CompilerParams(dimension_semantics=sem, vmem_limit_bytes=VMEM_LIMIT)


def _layernorm(z, g, b):
    mu = jnp.mean(z, axis=-1, keepdims=True)
    d = z - mu
    var = jnp.mean(d * d, axis=-1, keepdims=True)
    return d * lax.rsqrt(var + LN_EPS) * g + b


def _rope128(x, cos, sin_signed):
    lane = lax.broadcasted_iota(jnp.int32, x.shape, 1)
    lo = (lane % 32) < 16
    rot = jnp.where(lo, pltpu.roll(x, LANES - 16, 1), pltpu.roll(x, 16, 1))
    return x * cos + rot * sin_signed


def _rmsnorm128(x, g, bd):
    x2 = x * x
    hi = x2.astype(BF16)
    lo = (x2 - hi.astype(F32)).astype(BF16)
    ms = _dot(hi, bd) + _dot(lo, bd)
    return x * lax.rsqrt(ms + RMS_EPS) * g


def _modulated(x_ref, ctx_ref, mlat_ref, mctx_ref, is_ctx, d):
    m = jnp.where(is_ctx, mctx_ref[...], mlat_ref[0])
    x = jnp.where(is_ctx, ctx_ref[0], x_ref[0])
    return (x * (1.0 + m[:, d:2 * d]) + m[:, 0:d]).astype(BF16)


def _stream_specs(x, ctx, nct, d, off=0):
    if ctx is None:
        return (x, x), [pl.BlockSpec((1, TM, d), lambda bi, i: (bi, i + off, 0)),
                        pl.BlockSpec((1, TM, d), lambda bi, i: (bi, jnp.minimum(i, max(nct - 1, 0)), 0))]
    return (x, ctx), [pl.BlockSpec((1, TM, d), lambda bi, i: (bi, jnp.maximum(i - nct, 0), 0)),
                      pl.BlockSpec((1, TM, d), lambda bi, i: (bi, jnp.minimum(i, nct - 1), 0))]


def _mod_kernel(c_ref, w_ref, b_ref, o_ref):
    c = c_ref[...]
    s = (c / (1.0 + jnp.exp(-c))).astype(BF16)
    o_ref[0] = _dot(s, w_ref[0].astype(BF16)) + b_ref[0]


def _mod_vectors(c, c_ctx, mod_w, mod_b):
    depth, d, n = mod_w.shape
    b = c.shape[0]
    rows = 8 * (-(-(b + 1) // 8))
    cs = jnp.zeros((rows, d), F32).at[:b].set(c).at[b].set(c_ctx)
    tn = 1536
    out = pl.pallas_call(
        _mod_kernel,
        grid=(depth, n // tn),
        in_specs=[pl.BlockSpec((rows, d), lambda l, j: (0, 0)),
                  pl.BlockSpec((1, d, tn), lambda l, j: (l, 0, j)),
                  pl.BlockSpec((1, 1, tn), lambda l, j: (l, 0, j))],
        out_specs=pl.BlockSpec((1, rows, tn), lambda l, j: (l, 0, j)),
        out_shape=jax.ShapeDtypeStruct((depth, rows, n), F32),
        compiler_params=_cparams(("parallel", "parallel")),
        name="mod_vectors",
    )(cs, mod_w, mod_b.reshape(depth, 1, n))
    return out


def _proj_ab_kernel(nct, x_ref, ctx_ref, mlat_ref, mctx_ref, w_ref, wvt_ref, cos_ref, sin_ref, cs_ref,
                    uw_ref, q_ref, k_ref, vt_ref):
    is_ctx = pl.program_id(1) < nct
    d = x_ref.shape[-1]
    h = _modulated(x_ref, ctx_ref, mlat_ref, mctx_ref, is_ctx, d)
    p = _dot(h, w_ref[...])
    uw_ref[0] = _dot(p[:, 0:256].astype(BF16), cs_ref[...]).astype(BF16)
    cos = cos_ref[...]
    sin = sin_ref[...]
    for j in range(6):
        c0 = 256 + j * LANES
        q_ref[0, :, j * LANES:(j + 1) * LANES] = (_rope128(p[:, c0:c0 + LANES], cos, sin) * QSCALE).astype(BF16)
    for j in range(4):
        c0 = 1024 + j * LANES
        k_ref[0, :, j * LANES:(j + 1) * LANES] = _rope128(p[:, c0:c0 + LANES], cos, sin).astype(BF16)
    vt = _dot_nt(wvt_ref[...], h).astype(BF16)
    for c in range(TM // LANES):
        vt_ref[0, c] = vt[:, c * LANES:(c + 1) * LANES]


def _proj_ab(x, ctx, mlat, mctx, w, wvt, cos, sin, cs, nct):
    b, d = x.shape[0], x.shape[2]
    tall = x.shape[1] + (0 if ctx is None else ctx.shape[1])
    stream, stream_specs = _stream_specs(x, ctx, nct, d)
    nt = tall // TM
    nw = w.shape[1]
    nchunk = tall // LANES
    cpt = TM // LANES
    return pl.pallas_call(
        functools.partial(_proj_ab_kernel, nct),
        grid=(b, nt),
        in_specs=stream_specs + [
                  pl.BlockSpec((1, 1, N_MOD * d), lambda bi, i: (bi, 0, 0)),
                  pl.BlockSpec((1, N_MOD * d), lambda bi, i: (0, 0)),
                  pl.BlockSpec((d, nw), lambda bi, i: (0, 0)),
                  pl.BlockSpec((256, d), lambda bi, i: (0, 0)),
                  pl.BlockSpec((TM, LANES), lambda bi, i: (i, 0)),
                  pl.BlockSpec((TM, LANES), lambda bi, i: (i, 0)),
                  pl.BlockSpec((256, 512), lambda bi, i: (0, 0))],
        out_specs=[pl.BlockSpec((1, TM, 512), lambda bi, i: (bi, i, 0)),
                   pl.BlockSpec((1, TM, 768), lambda bi, i: (bi, i, 0)),
                   pl.BlockSpec((1, TM, 512), lambda bi, i: (bi, i, 0)),
                   pl.BlockSpec((1, cpt, 256, LANES), lambda bi, i: (bi, i, 0, 0))],
        out_shape=[jax.ShapeDtypeStruct((b, tall, 512), BF16),
                   jax.ShapeDtypeStruct((b, tall, 768), BF16),
                   jax.ShapeDtypeStruct((b, tall, 512), BF16),
                   jax.ShapeDtypeStruct((b, nchunk, 256, LANES), BF16)],
        compiler_params=_cparams(("parallel", "parallel")),
        name="proj_ab",
    )(*stream, mlat, mctx, w, wvt, cos, sin, cs)


def _proj_cd_kernel(nct, x_ref, ctx_ref, mlat_ref, mctx_ref, w_ref, wvt_ref, cos_ref, sin_ref, bd_ref,
                    gq_ref, gk_ref, qc_ref, qd_ref, kc_ref, kd_ref, vt_ref):
    is_ctx = pl.program_id(1) < nct
    d = x_ref.shape[-1]
    h = _modulated(x_ref, ctx_ref, mlat_ref, mctx_ref, is_ctx, d)
    p = _dot(h, w_ref[...])
    cos = cos_ref[...]
    sin = sin_ref[...]
    bd = bd_ref[...]
    gq = gq_ref[...]
    gk = gk_ref[...]
    for j in range(4):
        sl = slice(j * LANES, (j + 1) * LANES)
        qc_ref[0, :, sl] = (_rope128(p[:, j * LANES:(j + 1) * LANES], cos, sin) * QSCALE).astype(BF16)
        c0 = 512 + j * LANES
        qd_ref[0, :, sl] = (_rope128(_rmsnorm128(p[:, c0:c0 + LANES], gq, bd), cos, sin) * QSCALE).astype(BF16)
        c0 = 1024 + j * LANES
        kc_ref[0, :, sl] = _rope128(p[:, c0:c0 + LANES], cos, sin).astype(BF16)
    for j in range(2):
        c0 = 1536 + j * LANES
        kd_ref[0, :, j * LANES:(j + 1) * LANES] = _rope128(
            _rmsnorm128(p[:, c0:c0 + LANES], gk, bd), cos, sin).astype(BF16)
    vt = _dot_nt(wvt_ref[...], h).astype(BF16)
    ones = jnp.ones((ONES_ROWS, vt.shape[1]), BF16)
    src = 0
    for dst, dv in _VT_LAYOUT:
        vt_ref[0, 0, dst:dst + dv, :] = vt[src:src + dv, :]
        vt_ref[0, 0, dst + dv:dst + dv + ONES_ROWS, :] = ones
        src += dv


def _proj_cd(x, ctx, mlat, mctx, w, wvt, cos, sin, bd, gq, gk, nct):
    b, d = x.shape[0], x.shape[2]
    tall = x.shape[1] + (0 if ctx is None else ctx.shape[1])
    stream, stream_specs = _stream_specs(x, ctx, nct, d)
    nt = tall // TM
    nw = w.shape[1]
    nv = VT_ROWS
    full = lambda bi, i: (0, 0)
    tile = lambda bi, i: (bi, i, 0)
    return pl.pallas_call(
        functools.partial(_proj_cd_kernel, nct),
        grid=(b, nt),
        in_specs=stream_specs + [
                  pl.BlockSpec((1, 1, N_MOD * d), lambda bi, i: (bi, 0, 0)),
                  pl.BlockSpec((1, N_MOD * d), full),
                  pl.BlockSpec((d, nw), full),
                  pl.BlockSpec(wvt.shape, full),
                  pl.BlockSpec((TM, LANES), lambda bi, i: (i, 0)),
                  pl.BlockSpec((TM, LANES), lambda bi, i: (i, 0)),
                  pl.BlockSpec((LANES, LANES), full),
                  pl.BlockSpec((1, LANES), full),
                  pl.BlockSpec((1, LANES), full)],
        out_specs=[pl.BlockSpec((1, TM, 512), tile),
                   pl.BlockSpec((1, TM, 512), tile),
                   pl.BlockSpec((1, TM, 512), tile),
                   pl.BlockSpec((1, TM, 256), tile),
                   pl.BlockSpec((1, 1, nv, TM), lambda bi, i: (bi, i, 0, 0))],
        out_shape=[jax.ShapeDtypeStruct((b, tall, 512), BF16),
                   jax.ShapeDtypeStruct((b, tall, 512), BF16),
                   jax.ShapeDtypeStruct((b, tall, 512), BF16),
                   jax.ShapeDtypeStruct((b, tall, 256), BF16),
                   jax.ShapeDtypeStruct((b, nt, nv, TM), BF16)],
        compiler_params=_cparams(("parallel", "parallel")),
        name="proj_cd",
    )(*stream, mlat, mctx, w, wvt, cos, sin, bd, gq, gk)


def _fourier_kernel(nb, scale, uw_ref, ca_ref, sa_ref, cb_ref, sb_ref, o_ref):
    j = pl.program_id(0)
    ca = ca_ref[pl.ds(j, 1), :]
    sa = sa_ref[pl.ds(j, 1), :]
    cb = cb_ref[...]
    sb = sb_ref[...]
    ct = (ca * cb - sa * sb).astype(BF16)
    nst = (-(sa * cb + ca * sb)).astype(BF16)
    for bi in range(nb):
        acc = _dot(ct, uw_ref[bi, :, 0:256]) + _dot(nst, uw_ref[bi, :, 256:512])
        o_ref[bi] = (acc * scale).astype(BF16)


def _dft_tables(t, tmf):
    k = np.arange(t, dtype=np.int64)
    j1 = np.arange(t // tmf, dtype=np.int64) * tmf
    j0 = np.arange(tmf, dtype=np.int64)
    aa = (2.0 * np.pi / t) * ((j1[:, None] * k[None, :]) % t)
    ab = (2.0 * np.pi / t) * ((j0[:, None] * k[None, :]) % t)
    f = lambda a: jnp.asarray(a, F32)
    return f(np.cos(aa)), f(np.sin(aa)), f(np.cos(ab)), f(np.sin(ab))


def _fourier(uw):
    b, t, _ = uw.shape
    tmf = min(128, t)
    ca, sa, cb, sb = _dft_tables(t, tmf)
    scale = 1.0 / math.sqrt(t * HEAD_DIM)
    full2 = lambda j: (0, 0)
    return pl.pallas_call(
        functools.partial(_fourier_kernel, b, scale),
        grid=(t // tmf,),
        in_specs=[pl.BlockSpec((b, t, 512), lambda j: (0, 0, 0)),
                  pl.BlockSpec((t // tmf, t), full2),
                  pl.BlockSpec((t // tmf, t), full2),
                  pl.BlockSpec((tmf, t), full2),
                  pl.BlockSpec((tmf, t), full2)],
        out_specs=pl.BlockSpec((b, tmf, 256), lambda j: (0, j, 0)),
        out_shape=jax.ShapeDtypeStruct((b, t, 256), BF16),
        compiler_params=_cparams(("parallel",)),
        name="fourier_mix",
    )(uw, ca, sa, cb, sb)


def _win_attn_kernel(nct, nchunk, sink_ref, q_ref, k_ref, vt_ref, o_ref, s_a, s_b):
    i = pl.program_id(1)
    is_ctx = i < nct
    j = i - nct
    ctx_rows = nct * TM
    ctx_chunks = ctx_rows // LANES
    lw = TM + 2 * WINDOW
    lchunks = lw // LANES
    cs = jnp.clip(ctx_chunks + (TM // LANES) * j - WINDOW // LANES, 0, nchunk - lchunks)
    rs = pl.multiple_of(cs * LANES, LANES)
    qpos = j * TM + lax.broadcasted_iota(jnp.int32, (1, TM), 1)
    kpos = cs * LANES - ctx_rows + lax.broadcasted_iota(jnp.int32, (lw, 1), 0)
    allowed = (jnp.abs(qpos - kpos) <= WINDOW) & (kpos >= 0) & jnp.logical_not(is_ctx)
    lane = lax.broadcasted_iota(jnp.int32, (TM, LANES), 1)
    n_heads = q_ref.shape[-1] // HEAD_DIM
    group = n_heads // (k_ref.shape[-1] // LANES)

    def scores(hq, dst_s):
        pair, half = divmod(hq, 2)
        g = hq // group
        qp = q_ref[0, :, pair * LANES:(pair + 1) * LANES]
        qm = jnp.where((lane >= HEAD_DIM) == (half == 1), qp, jnp.zeros_like(qp))
        dst_s[0:ctx_rows, :] = _dot_nt(k_ref[0, 0:ctx_rows, g * LANES:(g + 1) * LANES], qm)
        s_l = _dot_nt(k_ref[0, pl.ds(rs, lw), g * LANES:(g + 1) * LANES], qm)
        dst_s[ctx_rows:ctx_rows + lw, :] = jnp.where(allowed, s_l, NEG)

    def consume(hq, src_s):
        g = hq // group
        s = src_s[...]
        sk = sink_ref[hq] * LOG2E
        m = jnp.maximum(jnp.max(s, axis=0, keepdims=True), sk)
        p = jnp.exp2(s - m)
        l = jnp.sum(p, axis=0, keepdims=True) + jnp.exp2(sk - m)
        p = p.astype(BF16)
        vrows = slice(g * HEAD_DIM, (g + 1) * HEAD_DIM)
        acc = jnp.zeros((HEAD_DIM, TM), F32)
        for c in range(ctx_chunks):
            acc = acc + _dot(vt_ref[0, c, vrows, :], p[c * LANES:(c + 1) * LANES, :])
        for c in range(lchunks):
            r0 = ctx_rows + c * LANES
            acc = acc + _dot(vt_ref[0, cs + c, vrows, :], p[r0:r0 + LANES, :])
        return acc / l

    bufs = (s_a, s_b)
    scores(0, s_a)
    outs = []
    for hq in range(n_heads):
        if hq + 1 < n_heads:
            scores(hq + 1, bufs[(hq + 1) % 2])
        outs.append(consume(hq, bufs[hq % 2]))
        if hq % 2 == 1:
            pair = hq // 2
            o2 = jnp.concatenate(outs, axis=0)
            o_ref[0, :, pair * LANES:(pair + 1) * LANES] = o2.T.astype(BF16)
            outs = []


def _win_attn(sink, q, k, vt, nct):
    b, tall, qw = q.shape
    nt = tall // TM
    nchunk = vt.shape[1]
    return pl.pallas_call(
        functools.partial(_win_attn_kernel, nct, nchunk),
        grid=(b, nt),
        in_specs=[pl.BlockSpec(memory_space=pltpu.SMEM),
                  pl.BlockSpec((1, TM, qw), lambda bi, i: (bi, i, 0)),
                  pl.BlockSpec((1, tall, k.shape[-1]), lambda bi, i: (bi, 0, 0)),
                  pl.BlockSpec((1, nchunk, vt.shape[2], LANES), lambda bi, i: (bi, 0, 0, 0))],
        out_specs=pl.BlockSpec((1, TM, qw), lambda bi, i: (bi, i, 0)),
        out_shape=jax.ShapeDtypeStruct((b, tall, qw), BF16),
        scratch_shapes=[pltpu.VMEM((nct * TM + TM + 2 * WINDOW, TM), F32)] * 2,
        compiler_params=_cparams(("parallel", "parallel")),
        name="window_attention",
    )(sink, q, k, vt)


def _unit_operands(u):
    if u < 2 * N_C_HEADS:
        hc = u // 2
        dst, dv = _VT_LAYOUT[hc]
        return True, hc, dst, dv
    g = (u - 2 * N_C_HEADS) // (N_D_HEADS // N_D_KV)
    dst, dv = _VT_LAYOUT[N_C_HEADS + g]
    return False, g, dst, dv


def _attn_cd_kernel(nkt, lam_init, lam_ref, sg_ref, qc_ref, qd_ref, kc_ref, kd_ref, vt_ref, o_ref,
                    qm_s, m_s, acc_s, s_a, s_b):
    mq = qc_ref.shape[1]
    tk = vt_ref.shape[-1]
    lane = lax.broadcasted_iota(jnp.int32, (mq, LANES), 1)
    upper = lane >= HEAD_DIM
    for u in range(N_UNITS):
        src = qc_ref if u < 2 * N_C_HEADS else qd_ref
        pair = (u if u < 2 * N_C_HEADS else u - 2 * N_C_HEADS) // 2
        qp = src[0, :, pair * LANES:(pair + 1) * LANES]
        qm_s[u] = jnp.where(upper == (u % 2 == 1), qp, jnp.zeros_like(qp))
    m_s[...] = jnp.full(m_s.shape, NEG, F32)
    acc_s[...] = jnp.zeros(acc_s.shape, F32)

    def scores(tile, u, dst_s):
        is_c, slab, _, _ = _unit_operands(u)
        k_ref = kc_ref if is_c else kd_ref
        r0 = pl.multiple_of(tile * tk, tk)
        dst_s[u] = _dot_nt(k_ref[0, pl.ds(r0, tk), slab * LANES:(slab + 1) * LANES], qm_s[u])

    def consume(tile, u, src_s):
        _, _, row0, dv = _unit_operands(u)
        rows = dv + ONES_ROWS
        m_old = m_s[u]
        m_new = jnp.maximum(m_old, jnp.max(src_s[u], axis=0, keepdims=True))
        alpha = jnp.exp2(m_old - m_new)
        p = jnp.exp2(src_s[u] - m_new).astype(BF16)
        pv = _dot(vt_ref[0, tile, row0:row0 + rows, :], p)
        acc_s[u, 0:rows, :] = acc_s[u, 0:rows, :] * alpha + pv
        m_s[u] = m_new

    def stage(tile, src_s, dst_s):
        nxt = jnp.minimum(tile + 1, nkt - 1)
        for u in range(N_UNITS):
            scores(nxt, u, dst_s)
            consume(tile, u, src_s)

    for u in range(N_UNITS):
        scores(0, u, s_a)
    stage(0, s_a, s_b)

    def body(pr, carry):
        stage(2 * pr + 1, s_b, s_a)
        stage(2 * pr + 2, s_a, s_b)
        return carry

    lax.fori_loop(0, (nkt - 1) // 2, body, 0)
    if (nkt - 1) % 2 == 1:
        stage(nkt - 1, s_b, s_a)

    lv = lam_ref[...]
    lam = (jnp.exp(jnp.sum(lv[0:1] * lv[1:2], axis=-1, keepdims=True))
           - jnp.exp(jnp.sum(lv[2:3] * lv[3:4], axis=-1, keepdims=True)) + lam_init)
    sg = sg_ref[...]
    dvc = 2 * HEAD_DIM
    for hc in range(N_C_HEADS):
        o1 = acc_s[2 * hc, 0:dvc, :] / acc_s[2 * hc, dvc:dvc + 1, :]
        o2 = acc_s[2 * hc + 1, 0:dvc, :] / acc_s[2 * hc + 1, dvc:dvc + 1, :]
        o = o1 - lam * o2
        ms = jnp.mean(o * o, axis=0, keepdims=True)
        o = o * lax.rsqrt(ms + RMS_EPS) * sg * (1.0 - lam_init)
        o_ref[0, :, hc * LANES:(hc + 1) * LANES] = o.T.astype(BF16)
    for pair in range(N_D_HEADS // 2):
        outs = []
        for half in range(2):
            u = 2 * N_C_HEADS + 2 * pair + half
            outs.append(acc_s[u, 0:HEAD_DIM, :] / acc_s[u, HEAD_DIM:HEAD_DIM + 1, :])
        o2 = jnp.concatenate(outs, axis=0)
        c0 = N_C_HEADS * LANES + pair * LANES
        o_ref[0, :, c0:c0 + LANES] = o2.T.astype(BF16)


def _attn_cd(lam_vec, sg, qc, qd, kc, kd, vt, nct, lam_init):
    b, tall, _ = qc.shape
    nkt, nv, tk = vt.shape[1], vt.shape[2], vt.shape[3]
    mq = TM
    nq = tall // mq - nct
    ow = N_C_HEADS * LANES + N_D_HEADS * HEAD_DIM
    qtile = lambda bi, i: (bi, i + nct, 0)
    return pl.pallas_call(
        functools.partial(_attn_cd_kernel, nkt, lam_init),
        grid=(b, nq),
        in_specs=[pl.BlockSpec((4, HEAD_DIM), lambda bi, i: (0, 0)),
                  pl.BlockSpec((LANES, 1), lambda bi, i: (0, 0)),
                  pl.BlockSpec((1, mq, qc.shape[-1]), qtile),
                  pl.BlockSpec((1, mq, qd.shape[-1]), qtile),
                  pl.BlockSpec((1, tall, kc.shape[-1]), lambda bi, i: (bi, 0, 0)),
                  pl.BlockSpec((1, tall, kd.shape[-1]), lambda bi, i: (bi, 0, 0)),
                  pl.BlockSpec((1, nkt, nv, tk), lambda bi, i: (bi, 0, 0, 0))],
        out_specs=pl.BlockSpec((1, mq, ow), lambda bi, i: (bi, i, 0)),
        out_shape=jax.ShapeDtypeStruct((b, nq * mq, ow), BF16),
        scratch_shapes=[pltpu.VMEM((N_UNITS, mq, LANES), BF16),
                        pltpu.VMEM((N_UNITS, 1, mq), F32),
                        pltpu.VMEM((N_UNITS, 2 * HEAD_DIM + ONES_ROWS, mq), F32),
                        pltpu.VMEM((N_UNITS, tk, mq), F32),
                        pltpu.VMEM((N_UNITS, tk, mq), F32)],
        compiler_params=_cparams(("parallel", "parallel")),
        name="attention_cd",
    )(lam_vec, sg, qc, qd, kc, kd, vt)


def _oproj_kernel(n_parts, widths, nct, alpha, n_exp, *refs):
    o_refs = refs[:n_parts]
    (w_ref, x_ref, ctx_ref, mlat_ref, mctx_ref, lng_ref, lnb_ref, rwt_ref, rb_ref,
     x1_ref, h2_ref, te_ref, tg_ref, n8_ref) = refs[n_parts:]
    d = x_ref.shape[-1]
    is_ctx = pl.program_id(1) < nct
    y = None
    r0 = 0
    for o_ref, wd in zip(o_refs, widths):
        part = _dot(o_ref[0], w_ref[r0:r0 + wd, :])
        y = part if y is None else y + part
        r0 += wd
    m = jnp.where(is_ctx, mctx_ref[...], mlat_ref[0])
    x = jnp.where(is_ctx, ctx_ref[0], x_ref[0])
    x1 = _layernorm(alpha * x + m[:, 2 * d:3 * d] * y, lng_ref[...], lnb_ref[...])
    x1_ref[0] = x1
    h2 = x1 * (1.0 + m[:, 4 * d:5 * d]) + m[:, 3 * d:4 * d]
    h2_ref[0] = h2
    logits = _dot_nt(rwt_ref[...], h2.astype(BF16)) + rb_ref[...]
    tm = logits.shape[1]
    row = lax.broadcasted_iota(jnp.int32, logits.shape, 0)
    vals, idxs = [], []
    for _ in range(TOP_K):
        mx = jnp.max(logits, axis=0, keepdims=True)
        idx = jnp.min(jnp.where(logits == mx, row, n_exp), axis=0, keepdims=True)
        vals.append(mx)
        idxs.append(idx)
        logits = jnp.where(row == idx, NEG, logits)
    es = [jnp.exp(v - vals[0]) for v in vals]
    tot = es[0] + es[1] + es[2] + es[3]
    pad_i = jnp.zeros((8 - TOP_K, tm), jnp.int32)
    pad_f = jnp.zeros((8 - TOP_K, tm), F32)
    te = jnp.concatenate(idxs + [pad_i], axis=0)
    te_ref[0] = te
    tg_ref[0] = jnp.concatenate([e / tot for e in es] + [pad_f], axis=0)
    _, oh = _expert_onehot(te, n_exp)
    n8 = _groups(jnp.sum(oh, axis=1, keepdims=True))
    n8_ref[0] = jnp.broadcast_to(n8, (n_exp, LANES)).astype(jnp.int32)


def _oproj(parts, w_out, x, ctx, mlat, mctx, lng, lnb, rwt, rb, nct_out, x_tile_off, alpha):
    b, t, _ = parts[0].shape
    d = x.shape[-1]
    nt = t // TM
    n_exp = rwt.shape[0]
    widths = tuple(p.shape[-1] for p in parts)
    tile = lambda bi, i: (bi, i, 0)
    full = lambda bi, i: (0, 0)
    stream, stream_specs = _stream_specs(x, ctx, nct_out, d, x_tile_off)
    in_specs = [pl.BlockSpec((1, TM, wd), tile) for wd in widths] + [
        pl.BlockSpec(w_out.shape, full)] + stream_specs + [
        pl.BlockSpec((1, 1, N_MOD * d), lambda bi, i: (bi, 0, 0)),
        pl.BlockSpec((1, N_MOD * d), full),
        pl.BlockSpec((1, d), full),
        pl.BlockSpec((1, d), full),
        pl.BlockSpec((n_exp, d), full),
        pl.BlockSpec((n_exp, 1), full)]
    return pl.pallas_call(
        functools.partial(_oproj_kernel, len(parts), widths, nct_out, alpha, n_exp),
        grid=(b, nt),
        in_specs=in_specs,
        out_specs=[pl.BlockSpec((1, TM, d), tile),
                   pl.BlockSpec((1, TM, d), tile),
                   pl.BlockSpec((1, 8, TM), lambda bi, i: (bi, 0, i)),
                   pl.BlockSpec((1, 8, TM), lambda bi, i: (bi, 0, i)),
                   pl.BlockSpec((1, n_exp, LANES), lambda bi, i: (bi * nt + i, 0, 0))],
        out_shape=[jax.ShapeDtypeStruct((b, t, d), F32),
                   jax.ShapeDtypeStruct((b, t, d), F32),
                   jax.ShapeDtypeStruct((b, 8, t), jnp.int32),
                   jax.ShapeDtypeStruct((b, 8, t), F32),
                   jax.ShapeDtypeStruct((b * nt, n_exp, LANES), jnp.int32)],
        compiler_params=_cparams(("parallel", "parallel")),
        name="out_proj_norm_router",
    )(*parts, w_out, *stream, mlat, mctx, lng, lnb, rwt, rb)


def _moe_kernel(be_ref, nx_ref, bv_ref, nu_ref, rprev_ref, rcur_ref, rnext_ref, xs_hbm, wgu_hbm, bgu_ref,
                wd_hbm, bd_ref, ys_in_hbm, ys_hbm, wgu_f, wd_f, wgu_s, wd_s, xbuf, ybuf, sem, gsem, ssem):
    del ys_in_hbm
    blk = pl.program_id(0)
    ff = wd_s.shape[0]
    gpb = xbuf.shape[1] // ROW_GROUP
    par = blk % 2

    def gathers(rec, buf, wait):
        if wait:
            pltpu.make_async_copy(xs_hbm.at[pl.ds(0, xbuf.shape[1])], xbuf.at[buf], gsem.at[buf]).wait()
            return
        for g in range(gpb):
            _group_copy(xs_hbm, rec[g], xbuf.at[buf], g, gsem.at[buf]).start()

    def scatters(rec, buf, wait):
        if wait:
            pltpu.make_async_copy(ybuf.at[buf], ys_hbm.at[pl.ds(0, ybuf.shape[1])], ssem.at[buf]).wait()
            return
        for g in range(gpb):
            _group_copy(ybuf.at[buf], g, ys_hbm, rec[gpb + g], ssem.at[buf]).start()

    def fetch(e):
        return (pltpu.make_async_copy(wgu_hbm.at[e], wgu_f, sem.at[0]),
                pltpu.make_async_copy(wd_hbm.at[e], wd_f, sem.at[1]))

    @pl.when(blk < nu_ref[0])
    def _():
        e = be_ref[blk]
        new_expert = (blk == 0) | (e != be_ref[jnp.maximum(blk - 1, 0)])

        @pl.when(blk == 0)
        def _():
            for cp in fetch(e):
                cp.start()

        @pl.when(new_expert)
        def _():
            for cp in fetch(e):
                cp.wait()
            rows = 128

            def cast_gu(r, c):
                r0 = pl.multiple_of(r * rows, rows)
                wgu_s[pl.ds(r0, rows), :] = wgu_f[pl.ds(r0, rows), :].astype(BF16)
                return c

            def cast_d(r, c):
                r0 = pl.multiple_of(r * rows, rows)
                wd_s[pl.ds(r0, rows), :] = wd_f[pl.ds(r0, rows), :].astype(BF16)
                return c

            lax.fori_loop(0, wgu_s.shape[0] // rows, cast_gu, 0)
            lax.fori_loop(0, wd_s.shape[0] // rows, cast_d, 0)

            @pl.when(nx_ref[blk] != e)
            def _():
                for cp in fetch(nx_ref[blk]):
                    cp.start()

        @pl.when(blk == 0)
        def _():
            gathers(rcur_ref, 0, False)

        gathers(rcur_ref, par, True)
        half = xbuf.shape[1] // 2

        def ffn(rows):
            gathers(rnext_ref, 1 - par, False)
            gu = _dot(xbuf[par, rows, :].astype(BF16), wgu_s[...]) + bgu_ref[0]
            g = jnp.minimum(gu[:, :ff], SWIGLU_LIMIT)
            u = jnp.clip(gu[:, ff:], -SWIGLU_LIMIT, SWIGLU_LIMIT)
            act = (u + 1.0) * (g / (1.0 + jnp.exp(-SWIGLU_ALPHA * g)))
            ybuf[par, rows, :] = _dot(act.astype(BF16), wd_s[...]) + bd_ref[0]
            if rows != slice(None):
                ybuf[par, half:, :] = jnp.zeros((half, ybuf.shape[2]), F32)
            scatters(rcur_ref, par, False)

        @pl.when(bv_ref[blk] > half)
        def _():
            ffn(slice(None))

        @pl.when(bv_ref[blk] <= half)
        def _():
            ffn(slice(0, half))

        @pl.when(blk > 0)
        def _():
            scatters(rprev_ref, 1 - par, True)

        @pl.when(blk == nu_ref[0] - 1)
        def _():
            gathers(rnext_ref, 1 - par, True)
            scatters(rcur_ref, par, True)


def _moe_experts(place, layer, xs, ys, w_gu, b_gu, w_down, b_down):
    d = xs.shape[1]
    ff2 = w_gu.shape[-1]
    ff = w_down.shape[-2]
    off = layer * w_gu.shape[1]
    n_exp = w_gu.shape[0] * w_gu.shape[1]
    w_gu = w_gu.reshape(n_exp, d, ff2)
    w_down = w_down.reshape(n_exp, ff, d)
    nb = place["n_blocks"]
    last = lambda nu: nu[0] - 1
    bblk = lambda i, be, nx, bv, nu: (be[i], 0, 0)
    rec = lambda f: pl.BlockSpec((SMEM_BLOCK,), f, memory_space=pltpu.SMEM)
    grid_spec = pltpu.PrefetchScalarGridSpec(
        num_scalar_prefetch=4,
        grid=(nb,),
        in_specs=[rec(lambda i, be, nx, bv, nu: (jnp.clip(i - 1, 0, last(nu)),)),
                  rec(lambda i, be, nx, bv, nu: (jnp.minimum(i, last(nu)),)),
                  rec(lambda i, be, nx, bv, nu: (jnp.minimum(i + 1, last(nu)),)),
                  pl.BlockSpec(memory_space=pl.ANY),
                  pl.BlockSpec(memory_space=pl.ANY),
                  pl.BlockSpec((1, 1, ff2), bblk),
                  pl.BlockSpec(memory_space=pl.ANY),
                  pl.BlockSpec((1, 1, d), bblk),
                  pl.BlockSpec(memory_space=pl.ANY)],
        out_specs=pl.BlockSpec(memory_space=pl.ANY),
        scratch_shapes=[pltpu.VMEM((d, ff2), F32), pltpu.VMEM((ff, d), F32),
                        pltpu.VMEM((d, ff2), BF16), pltpu.VMEM((ff, d), BF16),
                        pltpu.VMEM((2, MOE_BM, d), F32), pltpu.VMEM((2, MOE_BM, d), F32),
                        pltpu.SemaphoreType.DMA((2,)), pltpu.SemaphoreType.DMA((2,)),
                        pltpu.SemaphoreType.DMA((2,))])
    records = place["rec"]
    return pl.pallas_call(
        _moe_kernel,
        grid_spec=grid_spec,
        out_shape=jax.ShapeDtypeStruct(ys.shape, F32),
        input_output_aliases={12: 0},
        compiler_params=_cparams(("arbitrary",)),
        name="moe_experts",
    )(place["block_e"] + off, place["block_next_e"] + off, place["block_rows"], place["n_used"],
      records, records, records, xs, w_gu, b_gu.reshape(n_exp, 1, ff2), w_down, b_down.reshape(n_exp, 1, d), ys)


def _expert_onehot(te, n_exp):
    row = lax.broadcasted_iota(jnp.int32, (n_exp, te.shape[1]), 0)
    oh = jnp.zeros(row.shape, F32)
    for k in range(TOP_K):
        oh = oh + (row == te[k:k + 1, :]).astype(F32)
    return row, oh


def _groups(cnt):
    return jnp.floor((cnt + (ROW_GROUP - 1)) / ROW_GROUP)


def _route_pos_kernel(n_exp, te_ref, tri_ref, ltri_ref, pos_ref):
    te = te_ref[0]
    tm = te.shape[1]
    row, oh = _expert_onehot(te, n_exp)
    n8 = _groups(jnp.sum(oh, axis=1, keepdims=True))
    lbase = _dot(ltri_ref[...], jnp.broadcast_to(n8, (n_exp, LANES)).astype(BF16))[:, 0:1]
    before = _dot(oh.astype(BF16), tri_ref[...])
    where_to = ROW_GROUP * lbase + before
    pos = [jnp.sum(jnp.where(row == te[k:k + 1, :], where_to, 0.0), axis=0, keepdims=True)
           for k in range(TOP_K)]
    pad = jnp.zeros((8 - TOP_K, tm), F32)
    pos_ref[0] = jnp.concatenate(pos + [pad], axis=0).astype(jnp.int32)


def _route_positions(te, n_exp):
    b, _, t = te.shape
    tile = pl.BlockSpec((1, 8, TM), lambda bi, i: (bi, 0, i))
    tri = jnp.asarray(np.triu(np.ones((TM, TM)), 1), BF16)
    ltri = jnp.asarray(np.tril(np.ones((n_exp, n_exp)), -1), BF16)
    return pl.pallas_call(
        functools.partial(_route_pos_kernel, n_exp),
        grid=(b, t // TM),
        in_specs=[tile,
                  pl.BlockSpec((TM, TM), lambda bi, i: (0, 0)),
                  pl.BlockSpec((n_exp, n_exp), lambda bi, i: (0, 0))],
        out_specs=tile,
        out_shape=jax.ShapeDtypeStruct((b, 8, t), jnp.int32),
        compiler_params=_cparams(("parallel", "parallel")),
        name="route_positions",
    )(te, tri, ltri)


def _placement(n8, n_asg):
    n_tiles, n_exp = n8.shape
    gpt = _sort_rows(n_exp) // ROW_GROUP
    gpb = MOE_BM // ROW_GROUP
    assert (TOP_K * TM + n_exp * (ROW_GROUP - 1)) // ROW_GROUP < gpt
    ids = jnp.arange(n_exp, dtype=jnp.int32)
    lbase = jnp.cumsum(n8, axis=1) - n8
    cum_incl = jnp.cumsum(n8, axis=0)
    cum_excl = cum_incl - n8
    groups = cum_incl[-1]
    padded = (groups + gpb - 1) // gpb * gpb
    pend = jnp.cumsum(padded)
    pstart = pend - padded
    n_blocks = -(-(n_asg + n_tiles * n_exp * (ROW_GROUP - 1)) // MOE_BM) + n_exp
    n_used = (pend[-1] // gpb).astype(jnp.int32)
    blocks = jnp.arange(n_blocks, dtype=jnp.int32)
    blk = jnp.minimum(blocks, n_used - 1) * gpb
    block_e = jnp.minimum(jnp.sum((pend[None, :] <= blk[:, None]).astype(jnp.int32), axis=1), n_exp - 1)
    later = (padded[None, :] > 0) & (ids[None, :] > block_e[:, None])
    block_next_e = jnp.min(jnp.where(later, ids[None, :], n_exp), axis=1)
    block_next_e = jnp.where(block_next_e == n_exp, block_e, block_next_e).astype(jnp.int32)
    mine = (block_e[:, None] == ids[None, :]).astype(jnp.int32)
    pick = lambda table: jnp.sum(mine[:, :, None] * table.T[None, :, :], axis=1)
    b_start = jnp.sum(mine * pstart[None, :], axis=1)
    b_groups = jnp.sum(mine * groups[None, :], axis=1)
    j_e = (blocks * gpb - b_start)[:, None] + jnp.arange(gpb, dtype=jnp.int32)[None, :]
    valid = (j_e < b_groups[:, None]) & (blocks < n_used)[:, None]
    tile = jnp.sum((pick(cum_incl)[:, None, :] <= j_e[:, :, None]).astype(jnp.int32), axis=2)
    tile = jnp.minimum(tile, n_tiles - 1)
    in_tile = (tile[:, :, None] == jnp.arange(n_tiles, dtype=jnp.int32)[None, None, :]).astype(jnp.int32)
    shift = jnp.sum(in_tile * (pick(lbase) - pick(cum_excl))[:, None, :], axis=2)
    held = tile * gpt + shift + j_e
    pad_dst = n_tiles * gpt + jnp.arange(gpb, dtype=jnp.int32)[None, :]
    src = jnp.where(valid, held, gpt - 1)
    dst = jnp.where(valid, held, pad_dst)
    rec = jnp.concatenate([src, dst, jnp.zeros((n_blocks, SMEM_BLOCK - 2 * gpb), jnp.int32)], axis=1)
    left = b_groups - (blocks - b_start // gpb) * gpb
    block_rows = (jnp.clip(left, 0, gpb) * ROW_GROUP).astype(jnp.int32)
    ngrp = jnp.sum(n8, axis=1).astype(jnp.int32)
    return dict(rec=rec.reshape(-1), block_e=block_e, block_next_e=block_next_e, block_rows=block_rows,
                n_used=n_used.reshape(1), n_blocks=n_blocks, ngrp=ngrp)


def _group_copy(src_ref, src_group, dst_ref, dst_group, sem):
    s0 = pl.multiple_of(src_group * ROW_GROUP, ROW_GROUP)
    d0 = pl.multiple_of(dst_group * ROW_GROUP, ROW_GROUP)
    return pltpu.make_async_copy(src_ref.at[pl.ds(s0, ROW_GROUP)], dst_ref.at[pl.ds(d0, ROW_GROUP)], sem)


def _tile_step():
    return pl.program_id(0) * pl.num_programs(1) + pl.program_id(1)


def _moe_sort_kernel(ngrp_ref, pos_ref, h_ref, xs_ref, ys_hbm, zero_s, sem):
    step = _tile_step()
    tm = h_ref.shape[1]
    gpt = xs_ref.shape[0] // ROW_GROUP
    pos = pos_ref[0]
    hb = h_ref[0].astype(BF16)
    for c in range(xs_ref.shape[0] // tm):
        r = lax.broadcasted_iota(jnp.int32, (tm, tm), 0) + c * tm
        hit = r == pos[0:1, :]
        for k in range(1, TOP_K):
            hit = hit | (r == pos[k:k + 1, :])
        xs_ref[c * tm:(c + 1) * tm, :] = _dot(jnp.where(hit, 1.0, 0.0).astype(BF16), hb)

    @pl.when(step == 0)
    def _():
        zero_s[...] = jnp.zeros(zero_s.shape, F32)

    def clear(wait):
        def body(g, c):
            r0 = pl.multiple_of((step * gpt + g) * ROW_GROUP, ROW_GROUP)
            cp = pltpu.make_async_copy(zero_s, ys_hbm.at[pl.ds(r0, ROW_GROUP)], sem)
            cp.wait() if wait else cp.start()
            return c
        lax.fori_loop(ngrp_ref[step], gpt, body, 0)

    clear(False)
    clear(True)


def _sort_rows(n_exp):
    return -(-(TOP_K * TM + n_exp * ROW_GROUP) // TM) * TM


def _moe_sort(ngrp, pos, h2, n_exp):
    b, t, d = h2.shape
    nt = t // TM
    sr = _sort_rows(n_exp)
    grid_spec = pltpu.PrefetchScalarGridSpec(
        num_scalar_prefetch=1,
        grid=(b, nt),
        in_specs=[pl.BlockSpec((1, 8, TM), lambda bi, i, ng: (bi, 0, i)),
                  pl.BlockSpec((1, TM, d), lambda bi, i, ng: (bi, i, 0))],
        out_specs=[pl.BlockSpec((sr, d), lambda bi, i, ng: (bi * nt + i, 0)),
                   pl.BlockSpec(memory_space=pl.ANY)],
        scratch_shapes=[pltpu.VMEM((ROW_GROUP, d), F32), pltpu.SemaphoreType.DMA])
    return pl.pallas_call(
        _moe_sort_kernel,
        grid_spec=grid_spec,
        out_shape=[jax.ShapeDtypeStruct((b * nt * sr, d), F32),
                   jax.ShapeDtypeStruct(((b * nt + 1) * sr, d), F32)],
        compiler_params=_cparams(("arbitrary", "arbitrary")),
        name="moe_sort_rows",
    )(ngrp, pos, h2)


def _combine_kernel(nct, alpha, ys_ref, pos_ref, tg_ref, x1_ref, mlat_ref, mctx_ref, lng_ref, lnb_ref, o_ref):
    d = x1_ref.shape[-1]
    tm = x1_ref.shape[1]
    is_ctx = pl.program_id(1) < nct
    m = jnp.where(is_ctx, mctx_ref[...], mlat_ref[0])
    pos = pos_ref[0]
    gates = tg_ref[0]
    f = None
    for c in range(ys_ref.shape[0] // tm):
        col = lax.broadcasted_iota(jnp.int32, (tm, tm), 1) + c * tm
        w = jnp.where(col == pos[:, 0:1], gates[:, 0:1], 0.0)
        for k in range(1, TOP_K):
            w = w + jnp.where(col == pos[:, k:k + 1], gates[:, k:k + 1], 0.0)
        part = _dot(w.astype(BF16), ys_ref[c * tm:(c + 1) * tm, :].astype(BF16))
        f = part if f is None else f + part
    o_ref[0] = _layernorm(alpha * x1_ref[0] + m[:, 5 * d:6 * d] * f, lng_ref[...], lnb_ref[...])


def _combine_postnorm(ys, pos_t, tg_t, x1, mlat, mctx, lng, lnb, nct, alpha, n_exp):
    b, t, d = x1.shape
    nt = t // TM
    tile = lambda bi, i: (bi, i, 0)
    full = lambda bi, i: (0, 0)
    return pl.pallas_call(
        functools.partial(_combine_kernel, nct, alpha),
        grid=(b, nt),
        in_specs=[pl.BlockSpec((_sort_rows(n_exp), d), lambda bi, i: (bi * nt + i, 0)),
                  pl.BlockSpec((1, TM, 8), tile),
                  pl.BlockSpec((1, TM, 8), tile),
                  pl.BlockSpec((1, TM, d), tile),
                  pl.BlockSpec((1, 1, N_MOD * d), lambda bi, i: (bi, 0, 0)),
                  pl.BlockSpec((1, N_MOD * d), full),
                  pl.BlockSpec((1, d), full),
                  pl.BlockSpec((1, d), full)],
        out_specs=pl.BlockSpec((1, TM, d), tile),
        out_shape=jax.ShapeDtypeStruct((b, t, d), F32),
        compiler_params=_cparams(("parallel", "parallel")),
        name="combine_post_norm",
    )(ys, pos_t, tg_t, x1, mlat, mctx, lng, lnb)


def _rope_tables(ctx_len, seq):
    t = np.arange(seq)
    row = (t // GRID_W).astype(np.float64)
    col = (t % GRID_W).astype(np.float64)
    nf = HEAD_DIM // 4
    inv = ROPE_THETA ** (-np.arange(nf, dtype=np.float64) / nf)
    ar = row[:, None] * inv[None, :]
    ac = col[:, None] * inv[None, :]
    ang = np.concatenate([ar, ar, ac, ac], axis=-1)
    cos = np.concatenate([np.ones((ctx_len, HEAD_DIM)), np.cos(ang)], axis=0)
    sin = np.concatenate([np.zeros((ctx_len, HEAD_DIM)), np.sin(ang)], axis=0)
    sign = np.where((np.arange(HEAD_DIM) % 32) < 16, -1.0, 1.0)[None, :]
    cos2 = np.tile(cos, (1, LANES // HEAD_DIM))
    sin2 = np.tile(sin * sign, (1, LANES // HEAD_DIM))
    return jnp.asarray(cos2, F32), jnp.asarray(sin2, F32)


def _channel_dft():
    c = np.arange(HEAD_DIM)
    ang = 2.0 * np.pi * ((c[:, None] * c[None, :]) % HEAD_DIM) / HEAD_DIM
    eye = np.eye(4)
    cs = np.concatenate([np.kron(eye, np.cos(ang)), np.kron(eye, np.sin(ang))], axis=1)
    return jnp.asarray(cs, BF16)


def _group_mean_matrix():
    bd = np.kron(np.eye(LANES // HEAD_DIM), np.full((HEAD_DIM, HEAD_DIM), 1.0 / HEAD_DIM))
    return jnp.asarray(bd, BF16)


def _dup_heads(wk, n_heads):
    d = wk.shape[0]
    return jnp.broadcast_to(wk.reshape(d, n_heads, 1, HEAD_DIM), (d, n_heads, 2, HEAD_DIM)).reshape(d, n_heads * LANES)


def _lambda_init(layer):
    return 0.8 - 0.6 * math.exp(-0.3 * layer)


def _moe_postnorm(h2, te, tg, n8, x1, mlat, mctx, lng, lnb, nct, alpha, layer, w_gu, b_gu, w_down, b_down):
    n_exp = w_gu.shape[1]
    b, t, _ = h2.shape
    place = _placement(n8[:, :, 0], b * t * TOP_K)
    pos = _route_positions(te, n_exp)
    xs, ys = _moe_sort(place["ngrp"], pos, h2, n_exp)
    ys = _moe_experts(place, layer, xs, ys, w_gu, b_gu, w_down, b_down)
    pos_t, tg_t = jnp.transpose(pos, (0, 2, 1)), jnp.transpose(tg, (0, 2, 1))
    return _combine_postnorm(ys, pos_t, tg_t, x1, mlat, mctx, lng, lnb, nct, alpha, n_exp)


def kernel(x, c, ctx, c_ctx, mod_w, mod_b, ln_g, ln_b, ab_w_in, ab_sink, ab_w_out,
           cd_w_in, cd_lambda, cd_subln_g, cd_q_norm_g, cd_k_norm_g, cd_w_out,
           router_w, router_b, expert_w_gu, expert_b_gu, expert_w_down, expert_b_down):
    b, s, d = x.shape
    n_ctx = ctx.shape[1]
    depth = mod_w.shape[0]
    n_exp = router_w.shape[-1]
    assert d == 16 * HEAD_DIM and n_ctx % TM == 0 and s % TM == 0 and s % GRID_W == 0
    nct = n_ctx // TM
    alpha = (2 * depth) ** 0.25

    cos, sin = _rope_tables(n_ctx, s)
    cs_dft = _channel_dft()
    bd = _group_mean_matrix()
    mods = _mod_vectors(c, c_ctx, mod_w, mod_b)

    xs, xc = x, ctx
    for l in range(depth):
        last = l == depth - 1
        i = l // 2
        mlat = mods[l, :b].reshape(b, 1, N_MOD * d)
        mctx = mods[l, b:b + 1]
        lng1, lnb1 = ln_g[l, 0].reshape(1, d), ln_b[l, 0].reshape(1, d)
        lng2, lnb2 = ln_g[l, 1].reshape(1, d), ln_b[l, 1].reshape(1, d)
        rwt = router_w[l].T.astype(BF16)
        rb = router_b[l].reshape(n_exp, 1)
        if l % 2 == 0:
            w = ab_w_in[i]
            w_n = jnp.concatenate([w[:, :1024], _dup_heads(w[:, 1024:1280], 4)], axis=1).astype(BF16)
            wvt = w[:, 1280:1536].T.astype(BF16)
            uw, q, k, vt = _proj_ab(xs, xc, mlat, mctx, w_n, wvt, cos, sin, cs_dft, nct)
            oa = jnp.concatenate([_fourier(uw[:, :n_ctx]), _fourier(uw[:, n_ctx:])], axis=1)
            ob = _win_attn(ab_sink[i], q, k, vt, nct)
            if last:
                parts = [oa[:, n_ctx:], ob[:, n_ctx:]]
            else:
                parts = [oa, ob]
            w_out = ab_w_out[i].astype(BF16)
        else:
            w = cd_w_in[i]
            w_n = jnp.concatenate([w[:, :1536], _dup_heads(w[:, 1536:1664], 2)], axis=1).astype(BF16)
            wvt = w[:, 1664:2304].T.astype(BF16)
            gq = jnp.tile(cd_q_norm_g[i], 2).reshape(1, LANES)
            gk = jnp.tile(cd_k_norm_g[i], 2).reshape(1, LANES)
            qc, qd, kc, kd, vt = _proj_cd(xs, xc, mlat, mctx, w_n, wvt, cos, sin, bd, gq, gk, nct)
            sg = cd_subln_g[i].reshape(LANES, 1)
            o_lat = _attn_cd(cd_lambda[i], sg, qc, qd, kc, kd, vt, nct, _lambda_init(l))
            if last:
                parts = [o_lat]
            else:
                raise NotImplementedError("context outputs of a differential/axial layer")
            w_out = cd_w_out[i].astype(BF16)
        experts = (l, expert_w_gu, expert_b_gu, expert_w_down, expert_b_down)
        if last:
            x1, h2, te, tg, n8 = _oproj(parts, w_out, xs, xc, mlat, mctx, lng1, lnb1, rwt, rb, 0, nct, alpha)
            return _moe_postnorm(h2, te, tg, n8, x1, mlat, mctx, lng2, lnb2, 0, alpha, *experts)
        x1, h2, te, tg, n8 = _oproj(parts, w_out, xs, xc, mlat, mctx, lng1, lnb1, rwt, rb, nct, 0, alpha)
        xs, xc = _moe_postnorm(h2, te, tg, n8, x1, mlat, mctx, lng2, lnb2, nct, alpha, *experts), None
    return xs[:, n_ctx:]
```

```python
import functools
import math

import jax
import jax.numpy as jnp
import numpy as np
from jax import lax
from jax.experimental import pallas as pl
from jax.experimental.pallas import tpu as pltpu

F32 = jnp.float32
BF16 = jnp.bfloat16

HEAD_DIM = 64
GRID_W = 64
WINDOW = 128
ROPE_THETA = 10000.0
LN_EPS = 1e-6
RMS_EPS = 1e-6
N_MOD = 6
TOP_K = 4
SWIGLU_LIMIT = 7.0
SWIGLU_ALPHA = 1.702
NEG = -1e30
LOG2E = 1.4426950408889634
QSCALE = HEAD_DIM ** -0.5 * LOG2E

N_C_HEADS = 4
N_D_HEADS = 8
N_D_KV = 2
N_UNITS = 2 * N_C_HEADS + N_D_HEADS
ONES_ROWS = 16
_VT_LAYOUT = tuple((h * (2 * HEAD_DIM + ONES_ROWS), 2 * HEAD_DIM) for h in range(N_C_HEADS)) + tuple(
    (N_C_HEADS * (2 * HEAD_DIM + ONES_ROWS) + g * (HEAD_DIM + ONES_ROWS), HEAD_DIM) for g in range(N_D_KV))
VT_ROWS = _VT_LAYOUT[-1][0] + HEAD_DIM + ONES_ROWS

LANES = 128
TM = 256
MOE_BM = 512
ROW_GROUP = 8
SMEM_BLOCK = 1024
VMEM_LIMIT = 56 * 1024 * 1024


def _dot(a, b):
    return jnp.dot(a, b, preferred_element_type=F32)


def _dot_nt(a, b):
    return lax.dot_general(a, b, (((1,), (1,)), ((), ())), preferred_element_type=F32)


def _cparams(sem):
    return pltpu.CompilerParams(dimension_semantics=sem, vmem_limit_bytes=VMEM_LIMIT)


def _layernorm(z, g, b):
    mu = jnp.mean(z, axis=-1, keepdims=True)
    d = z - mu
    var = jnp.mean(d * d, axis=-1, keepdims=True)
    return d * lax.rsqrt(var + LN_EPS) * g + b


def _rope128(x, cos, sin_signed):
    lane = lax.broadcasted_iota(jnp.int32, x.shape, 1)
    lo = (lane % 32) < 16
    rot = jnp.where(lo, pltpu.roll(x, LANES - 16, 1), pltpu.roll(x, 16, 1))
    return x * cos + rot * sin_signed


def _rmsnorm128(x, g, bd):
    x2 = x * x
    hi = x2.astype(BF16)
    lo = (x2 - hi.astype(F32)).astype(BF16)
    ms = _dot(hi, bd) + _dot(lo, bd)
    return x * lax.rsqrt(ms + RMS_EPS) * g


def _modulated(x_ref, ctx_ref, mlat_ref, mctx_ref, is_ctx, d):
    m = jnp.where(is_ctx, mctx_ref[...], mlat_ref[0])
    x = jnp.where(is_ctx, ctx_ref[0], x_ref[0])
    return (x * (1.0 + m[:, d:2 * d]) + m[:, 0:d]).astype(BF16)


def _stream_specs(x, ctx, nct, d, off=0):
    if ctx is None:
        return (x, x), [pl.BlockSpec((1, TM, d), lambda bi, i: (bi, i + off, 0)),
                        pl.BlockSpec((1, TM, d), lambda bi, i: (bi, jnp.minimum(i, max(nct - 1, 0)), 0))]
    return (x, ctx), [pl.BlockSpec((1, TM, d), lambda bi, i: (bi, jnp.maximum(i - nct, 0), 0)),
                      pl.BlockSpec((1, TM, d), lambda bi, i: (bi, jnp.minimum(i, nct - 1), 0))]


def _mod_kernel(c_ref, w_ref, b_ref, o_ref):
    c = c_ref[...]
    s = (c / (1.0 + jnp.exp(-c))).astype(BF16)
    o_ref[0] = _dot(s, w_ref[0].astype(BF16)) + b_ref[0]


def _mod_vectors(c, c_ctx, mod_w, mod_b):
    depth, d, n = mod_w.shape
    b = c.shape[0]
    rows = 8 * (-(-(b + 1) // 8))
    cs = jnp.zeros((rows, d), F32).at[:b].set(c).at[b].set(c_ctx)
    tn = 1536
    out = pl.pallas_call(
        _mod_kernel,
        grid=(depth, n // tn),
        in_specs=[pl.BlockSpec((rows, d), lambda l, j: (0, 0)),
                  pl.BlockSpec((1, d, tn), lambda l, j: (l, 0, j)),
                  pl.BlockSpec((1, 1, tn), lambda l, j: (l, 0, j))],
        out_specs=pl.BlockSpec((1, rows, tn), lambda l, j: (l, 0, j)),
        out_shape=jax.ShapeDtypeStruct((depth, rows, n), F32),
        compiler_params=_cparams(("parallel", "parallel")),
        name="mod_vectors",
    )(cs, mod_w, mod_b.reshape(depth, 1, n))
    return out


def _proj_ab_kernel(nct, x_ref, ctx_ref, mlat_ref, mctx_ref, w_ref, wvt_ref, cos_ref, sin_ref, cs_ref,
                    uw_ref, q_ref, k_ref, vt_ref):
    is_ctx = pl.program_id(1) < nct
    d = x_ref.shape[-1]
    h = _modulated(x_ref, ctx_ref, mlat_ref, mctx_ref, is_ctx, d)
    p = _dot(h, w_ref[...])
    uw_ref[0] = _dot(p[:, 0:256].astype(BF16), cs_ref[...]).astype(BF16)
    cos = cos_ref[...]
    sin = sin_ref[...]
    for j in range(6):
        c0 = 256 + j * LANES
        q_ref[0, :, j * LANES:(j + 1) * LANES] = (_rope128(p[:, c0:c0 + LANES], cos, sin) * QSCALE).astype(BF16)
    for j in range(4):
        c0 = 1024 + j * LANES
        k_ref[0, :, j * LANES:(j + 1) * LANES] = _rope128(p[:, c0:c0 + LANES], cos, sin).astype(BF16)
    vt = _dot_nt(wvt_ref[...], h).astype(BF16)
    for c in range(TM // LANES):
        vt_ref[0, c] = vt[:, c * LANES:(c + 1) * LANES]


def _proj_ab(x, ctx, mlat, mctx, w, wvt, cos, sin, cs, nct):
    b, d = x.shape[0], x.shape[2]
    tall = x.shape[1] + (0 if ctx is None else ctx.shape[1])
    stream, stream_specs = _stream_specs(x, ctx, nct, d)
    nt = tall // TM
    nw = w.shape[1]
    nchunk = tall // LANES
    cpt = TM // LANES
    return pl.pallas_call(
        functools.partial(_proj_ab_kernel, nct),
        grid=(b, nt),
        in_specs=stream_specs + [
                  pl.BlockSpec((1, 1, N_MOD * d), lambda bi, i: (bi, 0, 0)),
                  pl.BlockSpec((1, N_MOD * d), lambda bi, i: (0, 0)),
                  pl.BlockSpec((d, nw), lambda bi, i: (0, 0)),
                  pl.BlockSpec((256, d), lambda bi, i: (0, 0)),
                  pl.BlockSpec((TM, LANES), lambda bi, i: (i, 0)),
                  pl.BlockSpec((TM, LANES), lambda bi, i: (i, 0)),
                  pl.BlockSpec((256, 512), lambda bi, i: (0, 0))],
        out_specs=[pl.BlockSpec((1, TM, 512), lambda bi, i: (bi, i, 0)),
                   pl.BlockSpec((1, TM, 768), lambda bi, i: (bi, i, 0)),
                   pl.BlockSpec((1, TM, 512), lambda bi, i: (bi, i, 0)),
                   pl.BlockSpec((1, cpt, 256, LANES), lambda bi, i: (bi, i, 0, 0))],
        out_shape=[jax.ShapeDtypeStruct((b, tall, 512), BF16),
                   jax.ShapeDtypeStruct((b, tall, 768), BF16),
                   jax.ShapeDtypeStruct((b, tall, 512), BF16),
                   jax.ShapeDtypeStruct((b, nchunk, 256, LANES), BF16)],
        compiler_params=_cparams(("parallel", "parallel")),
        name="proj_ab",
    )(*stream, mlat, mctx, w, wvt, cos, sin, cs)


def _proj_cd_kernel(nct, x_ref, ctx_ref, mlat_ref, mctx_ref, w_ref, wvt_ref, cos_ref, sin_ref, bd_ref,
                    gq_ref, gk_ref, qc_ref, qd_ref, kc_ref, kd_ref, vt_ref):
    is_ctx = pl.program_id(1) < nct
    d = x_ref.shape[-1]
    h = _modulated(x_ref, ctx_ref, mlat_ref, mctx_ref, is_ctx, d)
    p = _dot(h, w_ref[...])
    cos = cos_ref[...]
    sin = sin_ref[...]
    bd = bd_ref[...]
    gq = gq_ref[...]
    gk = gk_ref[...]
    for j in range(4):
        sl = slice(j * LANES, (j + 1) * LANES)
        qc_ref[0, :, sl] = (_rope128(p[:, j * LANES:(j + 1) * LANES], cos, sin) * QSCALE).astype(BF16)
        c0 = 512 + j * LANES
        qd_ref[0, :, sl] = (_rope128(_rmsnorm128(p[:, c0:c0 + LANES], gq, bd), cos, sin) * QSCALE).astype(BF16)
        c0 = 1024 + j * LANES
        kc_ref[0, :, sl] = _rope128(p[:, c0:c0 + LANES], cos, sin).astype(BF16)
    for j in range(2):
        c0 = 1536 + j * LANES
        kd_ref[0, :, j * LANES:(j + 1) * LANES] = _rope128(
            _rmsnorm128(p[:, c0:c0 + LANES], gk, bd), cos, sin).astype(BF16)
    vt = _dot_nt(wvt_ref[...], h).astype(BF16)
    ones = jnp.ones((ONES_ROWS, vt.shape[1]), BF16)
    src = 0
    for dst, dv in _VT_LAYOUT:
        vt_ref[0, 0, dst:dst + dv, :] = vt[src:src + dv, :]
        vt_ref[0, 0, dst + dv:dst + dv + ONES_ROWS, :] = ones
        src += dv


def _proj_cd(x, ctx, mlat, mctx, w, wvt, cos, sin, bd, gq, gk, nct):
    b, d = x.shape[0], x.shape[2]
    tall = x.shape[1] + (0 if ctx is None else ctx.shape[1])
    stream, stream_specs = _stream_specs(x, ctx, nct, d)
    nt = tall // TM
    nw = w.shape[1]
    nv = VT_ROWS
    full = lambda bi, i: (0, 0)
    tile = lambda bi, i: (bi, i, 0)
    return pl.pallas_call(
        functools.partial(_proj_cd_kernel, nct),
        grid=(b, nt),
        in_specs=stream_specs + [
                  pl.BlockSpec((1, 1, N_MOD * d), lambda bi, i: (bi, 0, 0)),
                  pl.BlockSpec((1, N_MOD * d), full),
                  pl.BlockSpec((d, nw), full),
                  pl.BlockSpec(wvt.shape, full),
                  pl.BlockSpec((TM, LANES), lambda bi, i: (i, 0)),
                  pl.BlockSpec((TM, LANES), lambda bi, i: (i, 0)),
                  pl.BlockSpec((LANES, LANES), full),
                  pl.BlockSpec((1, LANES), full),
                  pl.BlockSpec((1, LANES), full)],
        out_specs=[pl.BlockSpec((1, TM, 512), tile),
                   pl.BlockSpec((1, TM, 512), tile),
                   pl.BlockSpec((1, TM, 512), tile),
                   pl.BlockSpec((1, TM, 256), tile),
                   pl.BlockSpec((1, 1, nv, TM), lambda bi, i: (bi, i, 0, 0))],
        out_shape=[jax.ShapeDtypeStruct((b, tall, 512), BF16),
                   jax.ShapeDtypeStruct((b, tall, 512), BF16),
                   jax.ShapeDtypeStruct((b, tall, 512), BF16),
                   jax.ShapeDtypeStruct((b, tall, 256), BF16),
                   jax.ShapeDtypeStruct((b, nt, nv, TM), BF16)],
        compiler_params=_cparams(("parallel", "parallel")),
        name="proj_cd",
    )(*stream, mlat, mctx, w, wvt, cos, sin, bd, gq, gk)


def _fourier_kernel(nb, scale, uw_ref, ca_ref, sa_ref, cb_ref, sb_ref, o_ref):
    j = pl.program_id(0)
    ca = ca_ref[pl.ds(j, 1), :]
    sa = sa_ref[pl.ds(j, 1), :]
    cb = cb_ref[...]
    sb = sb_ref[...]
    ct = (ca * cb - sa * sb).astype(BF16)
    nst = (-(sa * cb + ca * sb)).astype(BF16)
    for bi in range(nb):
        acc = _dot(ct, uw_ref[bi, :, 0:256]) + _dot(nst, uw_ref[bi, :, 256:512])
        o_ref[bi] = (acc * scale).astype(BF16)


def _dft_tables(t, tmf):
    k = np.arange(t, dtype=np.int64)
    j1 = np.arange(t // tmf, dtype=np.int64) * tmf
    j0 = np.arange(tmf, dtype=np.int64)
    aa = (2.0 * np.pi / t) * ((j1[:, None] * k[None, :]) % t)
    ab = (2.0 * np.pi / t) * ((j0[:, None] * k[None, :]) % t)
    f = lambda a: jnp.asarray(a, F32)
    return f(np.cos(aa)), f(np.sin(aa)), f(np.cos(ab)), f(np.sin(ab))


def _fourier(uw):
    b, t, _ = uw.shape
    tmf = min(128, t)
    ca, sa, cb, sb = _dft_tables(t, tmf)
    scale = 1.0 / math.sqrt(t * HEAD_DIM)
    full2 = lambda j: (0, 0)
    return pl.pallas_call(
        functools.partial(_fourier_kernel, b, scale),
        grid=(t // tmf,),
        in_specs=[pl.BlockSpec((b, t, 512), lambda j: (0, 0, 0)),
                  pl.BlockSpec((t // tmf, t), full2),
                  pl.BlockSpec((t // tmf, t), full2),
                  pl.BlockSpec((tmf, t), full2),
                  pl.BlockSpec((tmf, t), full2)],
        out_specs=pl.BlockSpec((b, tmf, 256), lambda j: (0, j, 0)),
        out_shape=jax.ShapeDtypeStruct((b, t, 256), BF16),
        compiler_params=_cparams(("parallel",)),
        name="fourier_mix",
    )(uw, ca, sa, cb, sb)


def _win_attn_kernel(nct, nchunk, sink_ref, q_ref, k_ref, vt_ref, o_ref, s_a, s_b):
    i = pl.program_id(1)
    is_ctx = i < nct
    j = i - nct
    ctx_rows = nct * TM
    ctx_chunks = ctx_rows // LANES
    lw = TM + 2 * WINDOW
    lchunks = lw // LANES
    cs = jnp.clip(ctx_chunks + (TM // LANES) * j - WINDOW // LANES, 0, nchunk - lchunks)
    rs = pl.multiple_of(cs * LANES, LANES)
    qpos = j * TM + lax.broadcasted_iota(jnp.int32, (1, TM), 1)
    kpos = cs * LANES - ctx_rows + lax.broadcasted_iota(jnp.int32, (lw, 1), 0)
    allowed = (jnp.abs(qpos - kpos) <= WINDOW) & (kpos >= 0) & jnp.logical_not(is_ctx)
    lane = lax.broadcasted_iota(jnp.int32, (TM, LANES), 1)
    n_heads = q_ref.shape[-1] // HEAD_DIM
    group = n_heads // (k_ref.shape[-1] // LANES)

    def scores(hq, dst_s):
        pair, half = divmod(hq, 2)
        g = hq // group
        qp = q_ref[0, :, pair * LANES:(pair + 1) * LANES]
        qm = jnp.where((lane >= HEAD_DIM) == (half == 1), qp, jnp.zeros_like(qp))
        dst_s[0:ctx_rows, :] = _dot_nt(k_ref[0, 0:ctx_rows, g * LANES:(g + 1) * LANES], qm)
        s_l = _dot_nt(k_ref[0, pl.ds(rs, lw), g * LANES:(g + 1) * LANES], qm)
        dst_s[ctx_rows:ctx_rows + lw, :] = jnp.where(allowed, s_l, NEG)

    def consume(hq, src_s):
        g = hq // group
        s = src_s[...]
        sk = sink_ref[hq] * LOG2E
        m = jnp.maximum(jnp.max(s, axis=0, keepdims=True), sk)
        p = jnp.exp2(s - m)
        l = jnp.sum(p, axis=0, keepdims=True) + jnp.exp2(sk - m)
        p = p.astype(BF16)
        vrows = slice(g * HEAD_DIM, (g + 1) * HEAD_DIM)
        acc = jnp.zeros((HEAD_DIM, TM), F32)
        for c in range(ctx_chunks):
            acc = acc + _dot(vt_ref[0, c, vrows, :], p[c * LANES:(c + 1) * LANES, :])
        for c in range(lchunks):
            r0 = ctx_rows + c * LANES
            acc = acc + _dot(vt_ref[0, cs + c, vrows, :], p[r0:r0 + LANES, :])
        return acc / l

    bufs = (s_a, s_b)
    scores(0, s_a)
    outs = []
    for hq in range(n_heads):
        if hq + 1 < n_heads:
            scores(hq + 1, bufs[(hq + 1) % 2])
        outs.append(consume(hq, bufs[hq % 2]))
        if hq % 2 == 1:
            pair = hq // 2
            o2 = jnp.concatenate(outs, axis=0)
            o_ref[0, :, pair * LANES:(pair + 1) * LANES] = o2.T.astype(BF16)
            outs = []


def _win_attn(sink, q, k, vt, nct):
    b, tall, qw = q.shape
    nt = tall // TM
    nchunk = vt.shape[1]
    return pl.pallas_call(
        functools.partial(_win_attn_kernel, nct, nchunk),
        grid=(b, nt),
        in_specs=[pl.BlockSpec(memory_space=pltpu.SMEM),
                  pl.BlockSpec((1, TM, qw), lambda bi, i: (bi, i, 0)),
                  pl.BlockSpec((1, tall, k.shape[-1]), lambda bi, i: (bi, 0, 0)),
                  pl.BlockSpec((1, nchunk, vt.shape[2], LANES), lambda bi, i: (bi, 0, 0, 0))],
        out_specs=pl.BlockSpec((1, TM, qw), lambda bi, i: (bi, i, 0)),
        out_shape=jax.ShapeDtypeStruct((b, tall, qw), BF16),
        scratch_shapes=[pltpu.VMEM((nct * TM + TM + 2 * WINDOW, TM), F32)] * 2,
        compiler_params=_cparams(("parallel", "parallel")),
        name="window_attention",
    )(sink, q, k, vt)


def _unit_operands(u):
    if u < 2 * N_C_HEADS:
        hc = u // 2
        dst, dv = _VT_LAYOUT[hc]
        return True, hc, dst, dv
    g = (u - 2 * N_C_HEADS) // (N_D_HEADS // N_D_KV)
    dst, dv = _VT_LAYOUT[N_C_HEADS + g]
    return False, g, dst, dv


def _attn_cd_kernel(nkt, lam_init, lam_ref, sg_ref, qc_ref, qd_ref, kc_ref, kd_ref, vt_ref, o_ref,
                    qm_s, m_s, acc_s, s_a, s_b):
    mq = qc_ref.shape[1]
    tk = vt_ref.shape[-1]
    lane = lax.broadcasted_iota(jnp.int32, (mq, LANES), 1)
    upper = lane >= HEAD_DIM
    for u in range(N_UNITS):
        src = qc_ref if u < 2 * N_C_HEADS else qd_ref
        pair = (u if u < 2 * N_C_HEADS else u - 2 * N_C_HEADS) // 2
        qp = src[0, :, pair * LANES:(pair + 1) * LANES]
        qm_s[u] = jnp.where(upper == (u % 2 == 1), qp, jnp.zeros_like(qp))
    m_s[...] = jnp.full(m_s.shape, NEG, F32)
    acc_s[...] = jnp.zeros(acc_s.shape, F32)

    def scores(tile, u, dst_s):
        is_c, slab, _, _ = _unit_operands(u)
        k_ref = kc_ref if is_c else kd_ref
        r0 = pl.multiple_of(tile * tk, tk)
        dst_s[u] = _dot_nt(k_ref[0, pl.ds(r0, tk), slab * LANES:(slab + 1) * LANES], qm_s[u])

    def consume(tile, u, src_s):
        _, _, row0, dv = _unit_operands(u)
        rows = dv + ONES_ROWS
        m_old = m_s[u]
        m_new = jnp.maximum(m_old, jnp.max(src_s[u], axis=0, keepdims=True))
        alpha = jnp.exp2(m_old - m_new)
        p = jnp.exp2(src_s[u] - m_new).astype(BF16)
        pv = _dot(vt_ref[0, tile, row0:row0 + rows, :], p)
        acc_s[u, 0:rows, :] = acc_s[u, 0:rows, :] * alpha + pv
        m_s[u] = m_new

    def stage(tile, src_s, dst_s):
        nxt = jnp.minimum(tile + 1, nkt - 1)
        for u in range(N_UNITS):
            scores(nxt, u, dst_s)
            consume(tile, u, src_s)

    for u in range(N_UNITS):
        scores(0, u, s_a)
    stage(0, s_a, s_b)

    def body(pr, carry):
        stage(2 * pr + 1, s_b, s_a)
        stage(2 * pr + 2, s_a, s_b)
        return carry

    lax.fori_loop(0, (nkt - 1) // 2, body, 0)
    if (nkt - 1) % 2 == 1:
        stage(nkt - 1, s_b, s_a)

    lv = lam_ref[...]
    lam = (jnp.exp(jnp.sum(lv[0:1] * lv[1:2], axis=-1, keepdims=True))
           - jnp.exp(jnp.sum(lv[2:3] * lv[3:4], axis=-1, keepdims=True)) + lam_init)
    sg = sg_ref[...]
    dvc = 2 * HEAD_DIM
    for hc in range(N_C_HEADS):
        o1 = acc_s[2 * hc, 0:dvc, :] / acc_s[2 * hc, dvc:dvc + 1, :]
        o2 = acc_s[2 * hc + 1, 0:dvc, :] / acc_s[2 * hc + 1, dvc:dvc + 1, :]
        o = o1 - lam * o2
        ms = jnp.mean(o * o, axis=0, keepdims=True)
        o = o * lax.rsqrt(ms + RMS_EPS) * sg * (1.0 - lam_init)
        o_ref[0, :, hc * LANES:(hc + 1) * LANES] = o.T.astype(BF16)
    for pair in range(N_D_HEADS // 2):
        outs = []
        for half in range(2):
            u = 2 * N_C_HEADS + 2 * pair + half
            outs.append(acc_s[u, 0:HEAD_DIM, :] / acc_s[u, HEAD_DIM:HEAD_DIM + 1, :])
        o2 = jnp.concatenate(outs, axis=0)
        c0 = N_C_HEADS * LANES + pair * LANES
        o_ref[0, :, c0:c0 + LANES] = o2.T.astype(BF16)


def _attn_cd(lam_vec, sg, qc, qd, kc, kd, vt, nct, lam_init):
    b, tall, _ = qc.shape
    nkt, nv, tk = vt.shape[1], vt.shape[2], vt.shape[3]
    mq = TM
    nq = tall // mq - nct
    ow = N_C_HEADS * LANES + N_D_HEADS * HEAD_DIM
    qtile = lambda bi, i: (bi, i + nct, 0)
    return pl.pallas_call(
        functools.partial(_attn_cd_kernel, nkt, lam_init),
        grid=(b, nq),
        in_specs=[pl.BlockSpec((4, HEAD_DIM), lambda bi, i: (0, 0)),
                  pl.BlockSpec((LANES, 1), lambda bi, i: (0, 0)),
                  pl.BlockSpec((1, mq, qc.shape[-1]), qtile),
                  pl.BlockSpec((1, mq, qd.shape[-1]), qtile),
                  pl.BlockSpec((1, tall, kc.shape[-1]), lambda bi, i: (bi, 0, 0)),
                  pl.BlockSpec((1, tall, kd.shape[-1]), lambda bi, i: (bi, 0, 0)),
                  pl.BlockSpec((1, nkt, nv, tk), lambda bi, i: (bi, 0, 0, 0))],
        out_specs=pl.BlockSpec((1, mq, ow), lambda bi, i: (bi, i, 0)),
        out_shape=jax.ShapeDtypeStruct((b, nq * mq, ow), BF16),
        scratch_shapes=[pltpu.VMEM((N_UNITS, mq, LANES), BF16),
                        pltpu.VMEM((N_UNITS, 1, mq), F32),
                        pltpu.VMEM((N_UNITS, 2 * HEAD_DIM + ONES_ROWS, mq), F32),
                        pltpu.VMEM((N_UNITS, tk, mq), F32),
                        pltpu.VMEM((N_UNITS, tk, mq), F32)],
        compiler_params=_cparams(("parallel", "parallel")),
        name="attention_cd",
    )(lam_vec, sg, qc, qd, kc, kd, vt)


def _oproj_kernel(n_parts, widths, nct, alpha, n_exp, *refs):
    o_refs = refs[:n_parts]
    (w_ref, x_ref, ctx_ref, mlat_ref, mctx_ref, lng_ref, lnb_ref, rwt_ref, rb_ref, tri_ref, ltri_ref,
     x1_ref, tg_ref, n8_ref, pos_ref, xs_ref, ys_hbm, zero_s, sem) = refs[n_parts:]
    d = x_ref.shape[-1]
    is_ctx = pl.program_id(1) < nct
    zero_s[...] = jnp.zeros(zero_s.shape, F32)
    tail0 = _tile_step() * xs_ref.shape[0] + TOP_K * x_ref.shape[1]
    clear = pltpu.make_async_copy(zero_s, ys_hbm.at[pl.ds(pl.multiple_of(tail0, ROW_GROUP), zero_s.shape[0])], sem)
    clear.start()
    y = None
    r0 = 0
    for o_ref, wd in zip(o_refs, widths):
        part = _dot(o_ref[0], w_ref[r0:r0 + wd, :])
        y = part if y is None else y + part
        r0 += wd
    m = jnp.where(is_ctx, mctx_ref[...], mlat_ref[0])
    x = jnp.where(is_ctx, ctx_ref[0], x_ref[0])
    x1 = _layernorm(alpha * x + m[:, 2 * d:3 * d] * y, lng_ref[...], lnb_ref[...])
    x1_ref[0] = x1
    hb = (x1 * (1.0 + m[:, 4 * d:5 * d]) + m[:, 3 * d:4 * d]).astype(BF16)
    logits = _dot_nt(rwt_ref[...], hb) + rb_ref[...]
    tm = logits.shape[1]
    row = lax.broadcasted_iota(jnp.int32, logits.shape, 0)
    vals, idxs = [], []
    for _ in range(TOP_K):
        mx = jnp.max(logits, axis=0, keepdims=True)
        idx = jnp.min(jnp.where(logits == mx, row, n_exp), axis=0, keepdims=True)
        vals.append(mx)
        idxs.append(idx)
        logits = jnp.where(row == idx, NEG, logits)
    es = [jnp.exp(v - vals[0]) for v in vals]
    tot = es[0] + es[1] + es[2] + es[3]
    pad_i = jnp.zeros((8 - TOP_K, tm), jnp.int32)
    pad_f = jnp.zeros((8 - TOP_K, tm), F32)
    te = jnp.concatenate(idxs + [pad_i], axis=0)
    tg_ref[0] = jnp.concatenate([e / tot for e in es] + [pad_f], axis=0)
    _, oh = _expert_onehot(te, n_exp)
    n8 = _groups(jnp.sum(oh, axis=1, keepdims=True))
    n8_ref[0] = jnp.broadcast_to(n8, (n_exp, LANES)).astype(jnp.int32)
    lbase = _dot(ltri_ref[...], jnp.broadcast_to(n8, (n_exp, LANES)).astype(BF16))[:, 0:1]
    before = _dot(oh.astype(BF16), tri_ref[...])
    where_to = ROW_GROUP * lbase + before
    pos = [jnp.sum(jnp.where(row == te[k:k + 1, :], where_to, 0.0), axis=0, keepdims=True).astype(jnp.int32)
           for k in range(TOP_K)]
    pos_ref[0] = jnp.concatenate(pos + [pad_i], axis=0)
    for c in range(xs_ref.shape[0] // tm):
        r = lax.broadcasted_iota(jnp.int32, (tm, tm), 0) + c * tm
        hit = r == pos[0]
        for k in range(1, TOP_K):
            hit = hit | (r == pos[k])
        xs_ref[c * tm:(c + 1) * tm, :] = _dot(jnp.where(hit, 1.0, 0.0).astype(BF16), hb)
    clear.wait()


def _oproj(parts, w_out, x, ctx, mlat, mctx, lng, lnb, rwt, rb, nct_out, x_tile_off, alpha):
    b, t, _ = parts[0].shape
    d = x.shape[-1]
    nt = t // TM
    n_exp = rwt.shape[0]
    widths = tuple(p.shape[-1] for p in parts)
    tile = lambda bi, i: (bi, i, 0)
    full = lambda bi, i: (0, 0)
    stream, stream_specs = _stream_specs(x, ctx, nct_out, d, x_tile_off)
    in_specs = [pl.BlockSpec((1, TM, wd), tile) for wd in widths] + [
        pl.BlockSpec(w_out.shape, full)] + stream_specs + [
        pl.BlockSpec((1, 1, N_MOD * d), lambda bi, i: (bi, 0, 0)),
        pl.BlockSpec((1, N_MOD * d), full),
        pl.BlockSpec((1, d), full),
        pl.BlockSpec((1, d), full),
        pl.BlockSpec((n_exp, d), full),
        pl.BlockSpec((n_exp, 1), full),
        pl.BlockSpec((TM, TM), full),
        pl.BlockSpec((n_exp, n_exp), full)]
    sr = _sort_rows(n_exp)
    tri = jnp.asarray(np.triu(np.ones((TM, TM)), 1), BF16)
    ltri = jnp.asarray(np.tril(np.ones((n_exp, n_exp)), -1), BF16)
    small = pl.BlockSpec((1, 8, TM), lambda bi, i: (bi, 0, i))
    return pl.pallas_call(
        functools.partial(_oproj_kernel, len(parts), widths, nct_out, alpha, n_exp),
        grid=(b, nt),
        in_specs=in_specs,
        out_specs=[pl.BlockSpec((1, TM, d), tile),
                   small,
                   pl.BlockSpec((1, n_exp, LANES), lambda bi, i: (bi * nt + i, 0, 0)),
                   small,
                   pl.BlockSpec((sr, d), lambda bi, i: (bi * nt + i, 0)),
                   pl.BlockSpec(memory_space=pl.ANY)],
        out_shape=[jax.ShapeDtypeStruct((b, t, d), F32),
                   jax.ShapeDtypeStruct((b, 8, t), F32),
                   jax.ShapeDtypeStruct((b * nt, n_exp, LANES), jnp.int32),
                   jax.ShapeDtypeStruct((b, 8, t), jnp.int32),
                   jax.ShapeDtypeStruct((b * nt * sr, d), F32),
                   jax.ShapeDtypeStruct(((b * nt + 1) * sr, d), F32)],
        scratch_shapes=[pltpu.VMEM((sr - TOP_K * TM, d), F32), pltpu.SemaphoreType.DMA],
        compiler_params=_cparams(("parallel", "parallel")),
        name="out_proj_norm_router",
    )(*parts, w_out, *stream, mlat, mctx, lng, lnb, rwt, rb, tri, ltri)


def _moe_kernel(be_ref, nx_ref, bv_ref, nu_ref, rprev_ref, rcur_ref, rnext_ref, xs_hbm, wgu_hbm, bgu_ref,
                wd_hbm, bd_ref, ys_in_hbm, ys_hbm, wgu_f, wd_f, wgu_s, wd_s, xbuf, ybuf, sem, gsem, ssem):
    del ys_in_hbm
    blk = pl.program_id(0)
    ff = wd_s.shape[0]
    gpb = xbuf.shape[1] // ROW_GROUP
    par = blk % 2

    def gathers(rec, buf, wait):
        if wait:
            pltpu.make_async_copy(xs_hbm.at[pl.ds(0, xbuf.shape[1])], xbuf.at[buf], gsem.at[buf]).wait()
            return
        for g in range(gpb):
            _group_copy(xs_hbm, rec[g], xbuf.at[buf], g, gsem.at[buf]).start()

    def scatters(rec, buf, wait):
        if wait:
            pltpu.make_async_copy(ybuf.at[buf], ys_hbm.at[pl.ds(0, ybuf.shape[1])], ssem.at[buf]).wait()
            return
        for g in range(gpb):
            _group_copy(ybuf.at[buf], g, ys_hbm, rec[gpb + g], ssem.at[buf]).start()

    def fetch(e):
        return (pltpu.make_async_copy(wgu_hbm.at[e], wgu_f, sem.at[0]),
                pltpu.make_async_copy(wd_hbm.at[e], wd_f, sem.at[1]))

    @pl.when(blk < nu_ref[0])
    def _():
        e = be_ref[blk]
        new_expert = (blk == 0) | (e != be_ref[jnp.maximum(blk - 1, 0)])

        @pl.when(blk == 0)
        def _():
            for cp in fetch(e):
                cp.start()

        @pl.when(new_expert)
        def _():
            for cp in fetch(e):
                cp.wait()
            rows = 128

            def cast_gu(r, c):
                r0 = pl.multiple_of(r * rows, rows)
                wgu_s[pl.ds(r0, rows), :] = wgu_f[pl.ds(r0, rows), :].astype(BF16)
                return c

            def cast_d(r, c):
                r0 = pl.multiple_of(r * rows, rows)
                wd_s[pl.ds(r0, rows), :] = wd_f[pl.ds(r0, rows), :].astype(BF16)
                return c

            lax.fori_loop(0, wgu_s.shape[0] // rows, cast_gu, 0)
            lax.fori_loop(0, wd_s.shape[0] // rows, cast_d, 0)

            @pl.when(nx_ref[blk] != e)
            def _():
                for cp in fetch(nx_ref[blk]):
                    cp.start()

        @pl.when(blk == 0)
        def _():
            gathers(rcur_ref, 0, False)

        gathers(rcur_ref, par, True)
        half = xbuf.shape[1] // 2

        def ffn(rows):
            gathers(rnext_ref, 1 - par, False)
            gu = _dot(xbuf[par, rows, :].astype(BF16), wgu_s[...]) + bgu_ref[0]
            g = jnp.minimum(gu[:, :ff], SWIGLU_LIMIT)
            u = jnp.clip(gu[:, ff:], -SWIGLU_LIMIT, SWIGLU_LIMIT)
            act = (u + 1.0) * (g / (1.0 + jnp.exp(-SWIGLU_ALPHA * g)))
            ybuf[par, rows, :] = _dot(act.astype(BF16), wd_s[...]) + bd_ref[0]
            if rows != slice(None):
                ybuf[par, half:, :] = jnp.zeros((half, ybuf.shape[2]), F32)
            scatters(rcur_ref, par, False)

        @pl.when(bv_ref[blk] > half)
        def _():
            ffn(slice(None))

        @pl.when(bv_ref[blk] <= half)
        def _():
            ffn(slice(0, half))

        @pl.when(blk > 0)
        def _():
            scatters(rprev_ref, 1 - par, True)

        @pl.when(blk == nu_ref[0] - 1)
        def _():
            gathers(rnext_ref, 1 - par, True)
            scatters(rcur_ref, par, True)


def _moe_experts(place, layer, xs, ys, w_gu, b_gu, w_down, b_down):
    d = xs.shape[1]
    ff2 = w_gu.shape[-1]
    ff = w_down.shape[-2]
    off = layer * w_gu.shape[1]
    n_exp = w_gu.shape[0] * w_gu.shape[1]
    w_gu = w_gu.reshape(n_exp, d, ff2)
    w_down = w_down.reshape(n_exp, ff, d)
    nb = place["n_blocks"]
    last = lambda nu: nu[0] - 1
    bblk = lambda i, be, nx, bv, nu: (be[i], 0, 0)
    rec = lambda f: pl.BlockSpec((SMEM_BLOCK,), f, memory_space=pltpu.SMEM)
    grid_spec = pltpu.PrefetchScalarGridSpec(
        num_scalar_prefetch=4,
        grid=(nb,),
        in_specs=[rec(lambda i, be, nx, bv, nu: (jnp.clip(i - 1, 0, last(nu)),)),
                  rec(lambda i, be, nx, bv, nu: (jnp.minimum(i, last(nu)),)),
                  rec(lambda i, be, nx, bv, nu: (jnp.minimum(i + 1, last(nu)),)),
                  pl.BlockSpec(memory_space=pl.ANY),
                  pl.BlockSpec(memory_space=pl.ANY),
                  pl.BlockSpec((1, 1, ff2), bblk),
                  pl.BlockSpec(memory_space=pl.ANY),
                  pl.BlockSpec((1, 1, d), bblk),
                  pl.BlockSpec(memory_space=pl.ANY)],
        out_specs=pl.BlockSpec(memory_space=pl.ANY),
        scratch_shapes=[pltpu.VMEM((d, ff2), F32), pltpu.VMEM((ff, d), F32),
                        pltpu.VMEM((d, ff2), BF16), pltpu.VMEM((ff, d), BF16),
                        pltpu.VMEM((2, MOE_BM, d), F32), pltpu.VMEM((2, MOE_BM, d), F32),
                        pltpu.SemaphoreType.DMA((2,)), pltpu.SemaphoreType.DMA((2,)),
                        pltpu.SemaphoreType.DMA((2,))])
    records = place["rec"]
    return pl.pallas_call(
        _moe_kernel,
        grid_spec=grid_spec,
        out_shape=jax.ShapeDtypeStruct(ys.shape, F32),
        input_output_aliases={12: 0},
        compiler_params=_cparams(("arbitrary",)),
        name="moe_experts",
    )(place["block_e"] + off, place["block_next_e"] + off, place["block_rows"], place["n_used"],
      records, records, records, xs, w_gu, b_gu.reshape(n_exp, 1, ff2), w_down, b_down.reshape(n_exp, 1, d), ys)


def _expert_onehot(te, n_exp):
    row = lax.broadcasted_iota(jnp.int32, (n_exp, te.shape[1]), 0)
    oh = jnp.zeros(row.shape, F32)
    for k in range(TOP_K):
        oh = oh + (row == te[k:k + 1, :]).astype(F32)
    return row, oh


def _groups(cnt):
    return jnp.floor((cnt + (ROW_GROUP - 1)) / ROW_GROUP)


def _placement(n8, n_asg):
    n_tiles, n_exp = n8.shape
    gpt = _sort_rows(n_exp) // ROW_GROUP
    gpb = MOE_BM // ROW_GROUP
    assert (TOP_K * TM + n_exp * (ROW_GROUP - 1)) // ROW_GROUP < gpt
    ids = jnp.arange(n_exp, dtype=jnp.int32)
    lbase = jnp.cumsum(n8, axis=1) - n8
    cum_incl = jnp.cumsum(n8, axis=0)
    cum_excl = cum_incl - n8
    groups = cum_incl[-1]
    padded = (groups + gpb - 1) // gpb * gpb
    pend = jnp.cumsum(padded)
    pstart = pend - padded
    n_blocks = -(-(n_asg + n_tiles * n_exp * (ROW_GROUP - 1)) // MOE_BM) + n_exp
    n_used = (pend[-1] // gpb).astype(jnp.int32)
    blocks = jnp.arange(n_blocks, dtype=jnp.int32)
    blk = jnp.minimum(blocks, n_used - 1) * gpb
    block_e = jnp.minimum(jnp.sum((pend[None, :] <= blk[:, None]).astype(jnp.int32), axis=1), n_exp - 1)
    later = (padded[None, :] > 0) & (ids[None, :] > block_e[:, None])
    block_next_e = jnp.min(jnp.where(later, ids[None, :], n_exp), axis=1)
    block_next_e = jnp.where(block_next_e == n_exp, block_e, block_next_e).astype(jnp.int32)
    mine = (block_e[:, None] == ids[None, :]).astype(jnp.int32)
    pick = lambda table: jnp.sum(mine[:, :, None] * table.T[None, :, :], axis=1)
    b_start = jnp.sum(mine * pstart[None, :], axis=1)
    b_groups = jnp.sum(mine * groups[None, :], axis=1)
    j_e = (blocks * gpb - b_start)[:, None] + jnp.arange(gpb, dtype=jnp.int32)[None, :]
    valid = (j_e < b_groups[:, None]) & (blocks < n_used)[:, None]
    tile = jnp.sum((pick(cum_incl)[:, None, :] <= j_e[:, :, None]).astype(jnp.int32), axis=2)
    tile = jnp.minimum(tile, n_tiles - 1)
    in_tile = (tile[:, :, None] == jnp.arange(n_tiles, dtype=jnp.int32)[None, None, :]).astype(jnp.int32)
    shift = jnp.sum(in_tile * (pick(lbase) - pick(cum_excl))[:, None, :], axis=2)
    held = tile * gpt + shift + j_e
    pad_dst = n_tiles * gpt + jnp.arange(gpb, dtype=jnp.int32)[None, :]
    src = jnp.where(valid, held, gpt - 1)
    dst = jnp.where(valid, held, pad_dst)
    rec = jnp.concatenate([src, dst, jnp.zeros((n_blocks, SMEM_BLOCK - 2 * gpb), jnp.int32)], axis=1)
    left = b_groups - (blocks - b_start // gpb) * gpb
    block_rows = (jnp.clip(left, 0, gpb) * ROW_GROUP).astype(jnp.int32)
    return dict(rec=rec.reshape(-1), block_e=block_e, block_next_e=block_next_e, block_rows=block_rows,
                n_used=n_used.reshape(1), n_blocks=n_blocks)


def _group_copy(src_ref, src_group, dst_ref, dst_group, sem):
    s0 = pl.multiple_of(src_group * ROW_GROUP, ROW_GROUP)
    d0 = pl.multiple_of(dst_group * ROW_GROUP, ROW_GROUP)
    return pltpu.make_async_copy(src_ref.at[pl.ds(s0, ROW_GROUP)], dst_ref.at[pl.ds(d0, ROW_GROUP)], sem)


def _tile_step():
    return pl.program_id(0) * pl.num_programs(1) + pl.program_id(1)


def _sort_rows(n_exp):
    return -(-(TOP_K * TM + n_exp * ROW_GROUP) // TM) * TM


def _combine_kernel(nct, alpha, ys_ref, pos_ref, tg_ref, x1_ref, mlat_ref, mctx_ref, lng_ref, lnb_ref, o_ref):
    d = x1_ref.shape[-1]
    tm = x1_ref.shape[1]
    is_ctx = pl.program_id(1) < nct
    m = jnp.where(is_ctx, mctx_ref[...], mlat_ref[0])
    pos = pos_ref[0]
    gates = tg_ref[0]
    f = None
    for c in range(ys_ref.shape[0] // tm):
        col = lax.broadcasted_iota(jnp.int32, (tm, tm), 1) + c * tm
        w = jnp.where(col == pos[:, 0:1], gates[:, 0:1], 0.0)
        for k in range(1, TOP_K):
            w = w + jnp.where(col == pos[:, k:k + 1], gates[:, k:k + 1], 0.0)
        part = _dot(w.astype(BF16), ys_ref[c * tm:(c + 1) * tm, :].astype(BF16))
        f = part if f is None else f + part
    o_ref[0] = _layernorm(alpha * x1_ref[0] + m[:, 5 * d:6 * d] * f, lng_ref[...], lnb_ref[...])


def _combine_postnorm(ys, pos_t, tg_t, x1, mlat, mctx, lng, lnb, nct, alpha, n_exp):
    b, t, d = x1.shape
    nt = t // TM
    tile = lambda bi, i: (bi, i, 0)
    full = lambda bi, i: (0, 0)
    return pl.pallas_call(
        functools.partial(_combine_kernel, nct, alpha),
        grid=(b, nt),
        in_specs=[pl.BlockSpec((_sort_rows(n_exp), d), lambda bi, i: (bi * nt + i, 0)),
                  pl.BlockSpec((1, TM, 8), tile),
                  pl.BlockSpec((1, TM, 8), tile),
                  pl.BlockSpec((1, TM, d), tile),
                  pl.BlockSpec((1, 1, N_MOD * d), lambda bi, i: (bi, 0, 0)),
                  pl.BlockSpec((1, N_MOD * d), full),
                  pl.BlockSpec((1, d), full),
                  pl.BlockSpec((1, d), full)],
        out_specs=pl.BlockSpec((1, TM, d), tile),
        out_shape=jax.ShapeDtypeStruct((b, t, d), F32),
        compiler_params=_cparams(("parallel", "parallel")),
        name="combine_post_norm",
    )(ys, pos_t, tg_t, x1, mlat, mctx, lng, lnb)


def _rope_tables(ctx_len, seq):
    t = np.arange(seq)
    row = (t // GRID_W).astype(np.float64)
    col = (t % GRID_W).astype(np.float64)
    nf = HEAD_DIM // 4
    inv = ROPE_THETA ** (-np.arange(nf, dtype=np.float64) / nf)
    ar = row[:, None] * inv[None, :]
    ac = col[:, None] * inv[None, :]
    ang = np.concatenate([ar, ar, ac, ac], axis=-1)
    cos = np.concatenate([np.ones((ctx_len, HEAD_DIM)), np.cos(ang)], axis=0)
    sin = np.concatenate([np.zeros((ctx_len, HEAD_DIM)), np.sin(ang)], axis=0)
    sign = np.where((np.arange(HEAD_DIM) % 32) < 16, -1.0, 1.0)[None, :]
    cos2 = np.tile(cos, (1, LANES // HEAD_DIM))
    sin2 = np.tile(sin * sign, (1, LANES // HEAD_DIM))
    return jnp.asarray(cos2, F32), jnp.asarray(sin2, F32)


def _channel_dft():
    c = np.arange(HEAD_DIM)
    ang = 2.0 * np.pi * ((c[:, None] * c[None, :]) % HEAD_DIM) / HEAD_DIM
    eye = np.eye(4)
    cs = np.concatenate([np.kron(eye, np.cos(ang)), np.kron(eye, np.sin(ang))], axis=1)
    return jnp.asarray(cs, BF16)


def _group_mean_matrix():
    bd = np.kron(np.eye(LANES // HEAD_DIM), np.full((HEAD_DIM, HEAD_DIM), 1.0 / HEAD_DIM))
    return jnp.asarray(bd, BF16)


def _dup_heads(wk, n_heads):
    d = wk.shape[0]
    return jnp.broadcast_to(wk.reshape(d, n_heads, 1, HEAD_DIM), (d, n_heads, 2, HEAD_DIM)).reshape(d, n_heads * LANES)


def _lambda_init(layer):
    return 0.8 - 0.6 * math.exp(-0.3 * layer)


def _moe_postnorm(routed, mlat, mctx, lng, lnb, nct, alpha, layer, w_gu, b_gu, w_down, b_down):
    x1, tg, n8, pos, xs, ys = routed
    n_exp = w_gu.shape[1]
    b, t, _ = x1.shape
    place = _placement(n8[:, :, 0], b * t * TOP_K)
    ys = _moe_experts(place, layer, xs, ys, w_gu, b_gu, w_down, b_down)
    pos_t, tg_t = jnp.transpose(pos, (0, 2, 1)), jnp.transpose(tg, (0, 2, 1))
    return _combine_postnorm(ys, pos_t, tg_t, x1, mlat, mctx, lng, lnb, nct, alpha, n_exp)


def kernel(x, c, ctx, c_ctx, mod_w, mod_b, ln_g, ln_b, ab_w_in, ab_sink, ab_w_out,
           cd_w_in, cd_lambda, cd_subln_g, cd_q_norm_g, cd_k_norm_g, cd_w_out,
           router_w, router_b, expert_w_gu, expert_b_gu, expert_w_down, expert_b_down):
    b, s, d = x.shape
    n_ctx = ctx.shape[1]
    depth = mod_w.shape[0]
    n_exp = router_w.shape[-1]
    assert d == 16 * HEAD_DIM and n_ctx % TM == 0 and s % TM == 0 and s % GRID_W == 0
    nct = n_ctx // TM
    alpha = (2 * depth) ** 0.25

    cos, sin = _rope_tables(n_ctx, s)
    cs_dft = _channel_dft()
    bd = _group_mean_matrix()
    mods = _mod_vectors(c, c_ctx, mod_w, mod_b)

    xs, xc = x, ctx
    for l in range(depth):
        last = l == depth - 1
        i = l // 2
        mlat = mods[l, :b].reshape(b, 1, N_MOD * d)
        mctx = mods[l, b:b + 1]
        lng1, lnb1 = ln_g[l, 0].reshape(1, d), ln_b[l, 0].reshape(1, d)
        lng2, lnb2 = ln_g[l, 1].reshape(1, d), ln_b[l, 1].reshape(1, d)
        rwt = router_w[l].T.astype(BF16)
        rb = router_b[l].reshape(n_exp, 1)
        if l % 2 == 0:
            w = ab_w_in[i]
            w_n = jnp.concatenate([w[:, :1024], _dup_heads(w[:, 1024:1280], 4)], axis=1).astype(BF16)
            wvt = w[:, 1280:1536].T.astype(BF16)
            uw, q, k, vt = _proj_ab(xs, xc, mlat, mctx, w_n, wvt, cos, sin, cs_dft, nct)
            oa = jnp.concatenate([_fourier(uw[:, :n_ctx]), _fourier(uw[:, n_ctx:])], axis=1)
            ob = _win_attn(ab_sink[i], q, k, vt, nct)
            if last:
                parts = [oa[:, n_ctx:], ob[:, n_ctx:]]
            else:
                parts = [oa, ob]
            w_out = ab_w_out[i].astype(BF16)
        else:
            w = cd_w_in[i]
            w_n = jnp.concatenate([w[:, :1536], _dup_heads(w[:, 1536:1664], 2)], axis=1).astype(BF16)
            wvt = w[:, 1664:2304].T.astype(BF16)
            gq = jnp.tile(cd_q_norm_g[i], 2).reshape(1, LANES)
            gk = jnp.tile(cd_k_norm_g[i], 2).reshape(1, LANES)
            qc, qd, kc, kd, vt = _proj_cd(xs, xc, mlat, mctx, w_n, wvt, cos, sin, bd, gq, gk, nct)
            sg = cd_subln_g[i].reshape(LANES, 1)
            o_lat = _attn_cd(cd_lambda[i], sg, qc, qd, kc, kd, vt, nct, _lambda_init(l))
            if last:
                parts = [o_lat]
            else:
                raise NotImplementedError("context outputs of a differential/axial layer")
            w_out = cd_w_out[i].astype(BF16)
        experts = (l, expert_w_gu, expert_b_gu, expert_w_down, expert_b_down)
        if last:
            routed = _oproj(parts, w_out, xs, xc, mlat, mctx, lng1, lnb1, rwt, rb, 0, nct, alpha)
            return _moe_postnorm(routed, mlat, mctx, lng2, lnb2, 0, alpha, *experts)
        routed = _oproj(parts, w_out, xs, xc, mlat, mctx, lng1, lnb1, rwt, rb, nct, 0, alpha)
        xs, xc = _moe_postnorm(routed, mlat, mctx, lng2, lnb2, nct, alpha, *experts), None
    return xs[:, n_ctx:]
```

```python
import functools
import math

import jax
import jax.numpy as jnp
import numpy as np
from jax import lax
from jax.experimental import pallas as pl
from jax.experimental.pallas import tpu as pltpu

F32 = jnp.float32
BF16 = jnp.bfloat16

HEAD_DIM = 64
GRID_W = 64
WINDOW = 128
ROPE_THETA = 10000.0
LN_EPS = 1e-6
RMS_EPS = 1e-6
N_MOD = 6
TOP_K = 4
SWIGLU_LIMIT = 7.0
SWIGLU_ALPHA = 1.702
NEG = -1e30
LOG2E = 1.4426950408889634
QSCALE = HEAD_DIM ** -0.5 * LOG2E

N_C_HEADS = 4
N_D_HEADS = 8
N_D_KV = 2
N_UNITS = 2 * N_C_HEADS + N_D_HEADS
ONES_ROWS = 16
_VT_LAYOUT = tuple((h * (2 * HEAD_DIM + ONES_ROWS), 2 * HEAD_DIM) for h in range(N_C_HEADS)) + tuple(
    (N_C_HEADS * (2 * HEAD_DIM + ONES_ROWS) + g * (HEAD_DIM + ONES_ROWS), HEAD_DIM) for g in range(N_D_KV))
VT_ROWS = _VT_LAYOUT[-1][0] + HEAD_DIM + ONES_ROWS

LANES = 128
TM = 256
MOE_BM = 512
ROW_GROUP = 8
SMEM_BLOCK = 1024
VMEM_LIMIT = 56 * 1024 * 1024


def _dot(a, b):
    return jnp.dot(a, b, preferred_element_type=F32)


def _dot_nt(a, b):
    return lax.dot_general(a, b, (((1,), (1,)), ((), ())), preferred_element_type=F32)


def _cparams(sem):
    return pltpu.CompilerParams(dimension_semantics=sem, vmem_limit_bytes=VMEM_LIMIT)


def _layernorm(z, g, b):
    mu = jnp.mean(z, axis=-1, keepdims=True)
    d = z - mu
    var = jnp.mean(d * d, axis=-1, keepdims=True)
    return d * lax.rsqrt(var + LN_EPS) * g + b


def _rope128(x, cos, sin_signed):
    lane = lax.broadcasted_iota(jnp.int32, x.shape, 1)
    lo = (lane % 32) < 16
    rot = jnp.where(lo, pltpu.roll(x, LANES - 16, 1), pltpu.roll(x, 16, 1))
    return x * cos + rot * sin_signed


def _rmsnorm128(x, g, bd):
    x2 = x * x
    hi = x2.astype(BF16)
    lo = (x2 - hi.astype(F32)).astype(BF16)
    ms = _dot(hi, bd) + _dot(lo, bd)
    return x * lax.rsqrt(ms + RMS_EPS) * g


def _modulated(x_ref, ctx_ref, mlat_ref, mctx_ref, is_ctx, d):
    m = jnp.where(is_ctx, mctx_ref[...], mlat_ref[0])
    x = jnp.where(is_ctx, ctx_ref[0], x_ref[0])
    return (x * (1.0 + m[:, d:2 * d]) + m[:, 0:d]).astype(BF16)


def _stream_specs(x, ctx, nct, d, off=0):
    if ctx is None:
        return (x, x), [pl.BlockSpec((1, TM, d), lambda bi, i: (bi, i + off, 0)),
                        pl.BlockSpec((1, TM, d), lambda bi, i: (bi, jnp.minimum(i, max(nct - 1, 0)), 0))]
    return (x, ctx), [pl.BlockSpec((1, TM, d), lambda bi, i: (bi, jnp.maximum(i - nct, 0), 0)),
                      pl.BlockSpec((1, TM, d), lambda bi, i: (bi, jnp.minimum(i, nct - 1), 0))]


def _mod_kernel(c_ref, w_ref, b_ref, o_ref):
    c = c_ref[...]
    s = (c / (1.0 + jnp.exp(-c))).astype(BF16)
    o_ref[0] = _dot(s, w_ref[0].astype(BF16)) + b_ref[0]


def _mod_vectors(c, c_ctx, mod_w, mod_b):
    depth, d, n = mod_w.shape
    b = c.shape[0]
    rows = 8 * (-(-(b + 1) // 8))
    cs = jnp.zeros((rows, d), F32).at[:b].set(c).at[b].set(c_ctx)
    tn = 1536
    out = pl.pallas_call(
        _mod_kernel,
        grid=(depth, n // tn),
        in_specs=[pl.BlockSpec((rows, d), lambda l, j: (0, 0)),
                  pl.BlockSpec((1, d, tn), lambda l, j: (l, 0, j)),
                  pl.BlockSpec((1, 1, tn), lambda l, j: (l, 0, j))],
        out_specs=pl.BlockSpec((1, rows, tn), lambda l, j: (l, 0, j)),
        out_shape=jax.ShapeDtypeStruct((depth, rows, n), F32),
        compiler_params=_cparams(("parallel", "parallel")),
        name="mod_vectors",
    )(cs, mod_w, mod_b.reshape(depth, 1, n))
    return out


def _proj_ab_kernel(nct, x_ref, ctx_ref, mlat_ref, mctx_ref, w_ref, wvt_ref, cos_ref, sin_ref, cs_ref,
                    uw_ref, q_ref, k_ref, vt_ref):
    is_ctx = pl.program_id(1) < nct
    d = x_ref.shape[-1]
    h = _modulated(x_ref, ctx_ref, mlat_ref, mctx_ref, is_ctx, d)
    p = _dot(h, w_ref[...])
    uw_ref[0] = _dot(p[:, 0:256].astype(BF16), cs_ref[...]).astype(BF16)
    cos = cos_ref[...]
    sin = sin_ref[...]
    for j in range(6):
        c0 = 256 + j * LANES
        q_ref[0, :, j * LANES:(j + 1) * LANES] = (_rope128(p[:, c0:c0 + LANES], cos, sin) * QSCALE).astype(BF16)
    for j in range(4):
        c0 = 1024 + j * LANES
        k_ref[0, :, j * LANES:(j + 1) * LANES] = _rope128(p[:, c0:c0 + LANES], cos, sin).astype(BF16)
    vt = _dot_nt(wvt_ref[...], h).astype(BF16)
    for c in range(TM // LANES):
        vt_ref[0, c] = vt[:, c * LANES:(c + 1) * LANES]


def _proj_ab(x, ctx, mlat, mctx, w, wvt, cos, sin, cs, nct):
    b, d = x.shape[0], x.shape[2]
    tall = x.shape[1] + (0 if ctx is None else ctx.shape[1])
    stream, stream_specs = _stream_specs(x, ctx, nct, d)
    nt = tall // TM
    nw = w.shape[1]
    nchunk = tall // LANES
    cpt = TM // LANES
    return pl.pallas_call(
        functools.partial(_proj_ab_kernel, nct),
        grid=(b, nt),
        in_specs=stream_specs + [
                  pl.BlockSpec((1, 1, N_MOD * d), lambda bi, i: (bi, 0, 0)),
                  pl.BlockSpec((1, N_MOD * d), lambda bi, i: (0, 0)),
                  pl.BlockSpec((d, nw), lambda bi, i: (0, 0)),
                  pl.BlockSpec((256, d), lambda bi, i: (0, 0)),
                  pl.BlockSpec((TM, LANES), lambda bi, i: (i, 0)),
                  pl.BlockSpec((TM, LANES), lambda bi, i: (i, 0)),
                  pl.BlockSpec((256, 512), lambda bi, i: (0, 0))],
        out_specs=[pl.BlockSpec((1, TM, 512), lambda bi, i: (bi, i, 0)),
                   pl.BlockSpec((1, TM, 768), lambda bi, i: (bi, i, 0)),
                   pl.BlockSpec((1, TM, 512), lambda bi, i: (bi, i, 0)),
                   pl.BlockSpec((1, cpt, 256, LANES), lambda bi, i: (bi, i, 0, 0))],
        out_shape=[jax.ShapeDtypeStruct((b, tall, 512), BF16),
                   jax.ShapeDtypeStruct((b, tall, 768), BF16),
                   jax.ShapeDtypeStruct((b, tall, 512), BF16),
                   jax.ShapeDtypeStruct((b, nchunk, 256, LANES), BF16)],
        compiler_params=_cparams(("parallel", "parallel")),
        name="proj_ab",
    )(*stream, mlat, mctx, w, wvt, cos, sin, cs)


def _proj_cd_kernel(nct, x_ref, ctx_ref, mlat_ref, mctx_ref, w_ref, wvt_ref, cos_ref, sin_ref, bd_ref,
                    gq_ref, gk_ref, qc_ref, qd_ref, kc_ref, kd_ref, vt_ref):
    is_ctx = pl.program_id(1) < nct
    d = x_ref.shape[-1]
    h = _modulated(x_ref, ctx_ref, mlat_ref, mctx_ref, is_ctx, d)
    p = _dot(h, w_ref[...])
    cos = cos_ref[...]
    sin = sin_ref[...]
    bd = bd_ref[...]
    gq = gq_ref[...]
    gk = gk_ref[...]
    for j in range(4):
        sl = slice(j * LANES, (j + 1) * LANES)
        qc_ref[0, :, sl] = (_rope128(p[:, j * LANES:(j + 1) * LANES], cos, sin) * QSCALE).astype(BF16)
        c0 = 512 + j * LANES
        qd_ref[0, :, sl] = (_rope128(_rmsnorm128(p[:, c0:c0 + LANES], gq, bd), cos, sin) * QSCALE).astype(BF16)
        c0 = 1024 + j * LANES
        kc_ref[0, :, sl] = _rope128(p[:, c0:c0 + LANES], cos, sin).astype(BF16)
    for j in range(2):
        c0 = 1536 + j * LANES
        kd_ref[0, :, j * LANES:(j + 1) * LANES] = _rope128(
            _rmsnorm128(p[:, c0:c0 + LANES], gk, bd), cos, sin).astype(BF16)
    vt = _dot_nt(wvt_ref[...], h).astype(BF16)
    ones = jnp.ones((ONES_ROWS, vt.shape[1]), BF16)
    src = 0
    for dst, dv in _VT_LAYOUT:
        vt_ref[0, 0, dst:dst + dv, :] = vt[src:src + dv, :]
        vt_ref[0, 0, dst + dv:dst + dv + ONES_ROWS, :] = ones
        src += dv


def _proj_cd(x, ctx, mlat, mctx, w, wvt, cos, sin, bd, gq, gk, nct):
    b, d = x.shape[0], x.shape[2]
    tall = x.shape[1] + (0 if ctx is None else ctx.shape[1])
    stream, stream_specs = _stream_specs(x, ctx, nct, d)
    nt = tall // TM
    nw = w.shape[1]
    nv = VT_ROWS
    full = lambda bi, i: (0, 0)
    tile = lambda bi, i: (bi, i, 0)
    return pl.pallas_call(
        functools.partial(_proj_cd_kernel, nct),
        grid=(b, nt),
        in_specs=stream_specs + [
                  pl.BlockSpec((1, 1, N_MOD * d), lambda bi, i: (bi, 0, 0)),
                  pl.BlockSpec((1, N_MOD * d), full),
                  pl.BlockSpec((d, nw), full),
                  pl.BlockSpec(wvt.shape, full),
                  pl.BlockSpec((TM, LANES), lambda bi, i: (i, 0)),
                  pl.BlockSpec((TM, LANES), lambda bi, i: (i, 0)),
                  pl.BlockSpec((LANES, LANES), full),
                  pl.BlockSpec((1, LANES), full),
                  pl.BlockSpec((1, LANES), full)],
        out_specs=[pl.BlockSpec((1, TM, 512), tile),
                   pl.BlockSpec((1, TM, 512), tile),
                   pl.BlockSpec((1, TM, 512), tile),
                   pl.BlockSpec((1, TM, 256), tile),
                   pl.BlockSpec((1, 1, nv, TM), lambda bi, i: (bi, i, 0, 0))],
        out_shape=[jax.ShapeDtypeStruct((b, tall, 512), BF16),
                   jax.ShapeDtypeStruct((b, tall, 512), BF16),
                   jax.ShapeDtypeStruct((b, tall, 512), BF16),
                   jax.ShapeDtypeStruct((b, tall, 256), BF16),
                   jax.ShapeDtypeStruct((b, nt, nv, TM), BF16)],
        compiler_params=_cparams(("parallel", "parallel")),
        name="proj_cd",
    )(*stream, mlat, mctx, w, wvt, cos, sin, bd, gq, gk)


def _fourier_kernel(nb, scale, uw_ref, ca_ref, sa_ref, cb_ref, sb_ref, o_ref):
    j = pl.program_id(0)
    ca = ca_ref[pl.ds(j, 1), :]
    sa = sa_ref[pl.ds(j, 1), :]
    cb = cb_ref[...]
    sb = sb_ref[...]
    ct = (ca * cb - sa * sb).astype(BF16)
    nst = (-(sa * cb + ca * sb)).astype(BF16)
    for bi in range(nb):
        acc = _dot(ct, uw_ref[bi, :, 0:256]) + _dot(nst, uw_ref[bi, :, 256:512])
        o_ref[bi] = (acc * scale).astype(BF16)


def _dft_tables(t, tmf):
    k = np.arange(t, dtype=np.int64)
    j1 = np.arange(t // tmf, dtype=np.int64) * tmf
    j0 = np.arange(tmf, dtype=np.int64)
    aa = (2.0 * np.pi / t) * ((j1[:, None] * k[None, :]) % t)
    ab = (2.0 * np.pi / t) * ((j0[:, None] * k[None, :]) % t)
    f = lambda a: jnp.asarray(a, F32)
    return f(np.cos(aa)), f(np.sin(aa)), f(np.cos(ab)), f(np.sin(ab))


def _fourier(uw):
    b, t, _ = uw.shape
    tmf = min(128, t)
    ca, sa, cb, sb = _dft_tables(t, tmf)
    scale = 1.0 / math.sqrt(t * HEAD_DIM)
    full2 = lambda j: (0, 0)
    return pl.pallas_call(
        functools.partial(_fourier_kernel, b, scale),
        grid=(t // tmf,),
        in_specs=[pl.BlockSpec((b, t, 512), lambda j: (0, 0, 0)),
                  pl.BlockSpec((t // tmf, t), full2),
                  pl.BlockSpec((t // tmf, t), full2),
                  pl.BlockSpec((tmf, t), full2),
                  pl.BlockSpec((tmf, t), full2)],
        out_specs=pl.BlockSpec((b, tmf, 256), lambda j: (0, j, 0)),
        out_shape=jax.ShapeDtypeStruct((b, t, 256), BF16),
        compiler_params=_cparams(("parallel",)),
        name="fourier_mix",
    )(uw, ca, sa, cb, sb)


def _win_attn_kernel(nct, nchunk, sink_ref, q_ref, k_ref, vt_ref, o_ref, s_a, s_b):
    i = pl.program_id(1)
    is_ctx = i < nct
    j = i - nct
    ctx_rows = nct * TM
    ctx_chunks = ctx_rows // LANES
    lw = TM + 2 * WINDOW
    lchunks = lw // LANES
    cs = jnp.clip(ctx_chunks + (TM // LANES) * j - WINDOW // LANES, 0, nchunk - lchunks)
    rs = pl.multiple_of(cs * LANES, LANES)
    qpos = j * TM + lax.broadcasted_iota(jnp.int32, (1, TM), 1)
    kpos = cs * LANES - ctx_rows + lax.broadcasted_iota(jnp.int32, (lw, 1), 0)
    allowed = (jnp.abs(qpos - kpos) <= WINDOW) & (kpos >= 0) & jnp.logical_not(is_ctx)
    lane = lax.broadcasted_iota(jnp.int32, (TM, LANES), 1)
    n_heads = q_ref.shape[-1] // HEAD_DIM
    group = n_heads // (k_ref.shape[-1] // LANES)

    def scores(hq, dst_s):
        pair, half = divmod(hq, 2)
        g = hq // group
        qp = q_ref[0, :, pair * LANES:(pair + 1) * LANES]
        qm = jnp.where((lane >= HEAD_DIM) == (half == 1), qp, jnp.zeros_like(qp))
        dst_s[0:ctx_rows, :] = _dot_nt(k_ref[0, 0:ctx_rows, g * LANES:(g + 1) * LANES], qm)
        s_l = _dot_nt(k_ref[0, pl.ds(rs, lw), g * LANES:(g + 1) * LANES], qm)
        dst_s[ctx_rows:ctx_rows + lw, :] = jnp.where(allowed, s_l, NEG)

    def consume(hq, src_s):
        g = hq // group
        s = src_s[...]
        sk = sink_ref[hq] * LOG2E
        m = jnp.maximum(jnp.max(s, axis=0, keepdims=True), sk)
        p = jnp.exp2(s - m)
        l = jnp.sum(p, axis=0, keepdims=True) + jnp.exp2(sk - m)
        p = p.astype(BF16)
        vrows = slice(g * HEAD_DIM, (g + 1) * HEAD_DIM)
        acc = jnp.zeros((HEAD_DIM, TM), F32)
        for c in range(ctx_chunks):
            acc = acc + _dot(vt_ref[0, c, vrows, :], p[c * LANES:(c + 1) * LANES, :])
        for c in range(lchunks):
            r0 = ctx_rows + c * LANES
            acc = acc + _dot(vt_ref[0, cs + c, vrows, :], p[r0:r0 + LANES, :])
        return acc / l

    bufs = (s_a, s_b)
    scores(0, s_a)
    outs = []
    for hq in range(n_heads):
        if hq + 1 < n_heads:
            scores(hq + 1, bufs[(hq + 1) % 2])
        outs.append(consume(hq, bufs[hq % 2]))
        if hq % 2 == 1:
            pair = hq // 2
            o2 = jnp.concatenate(outs, axis=0)
            o_ref[0, :, pair * LANES:(pair + 1) * LANES] = o2.T.astype(BF16)
            outs = []


def _win_attn(sink, q, k, vt, nct):
    b, tall, qw = q.shape
    nt = tall // TM
    nchunk = vt.shape[1]
    return pl.pallas_call(
        functools.partial(_win_attn_kernel, nct, nchunk),
        grid=(b, nt),
        in_specs=[pl.BlockSpec(memory_space=pltpu.SMEM),
                  pl.BlockSpec((1, TM, qw), lambda bi, i: (bi, i, 0)),
                  pl.BlockSpec((1, tall, k.shape[-1]), lambda bi, i: (bi, 0, 0)),
                  pl.BlockSpec((1, nchunk, vt.shape[2], LANES), lambda bi, i: (bi, 0, 0, 0))],
        out_specs=pl.BlockSpec((1, TM, qw), lambda bi, i: (bi, i, 0)),
        out_shape=jax.ShapeDtypeStruct((b, tall, qw), BF16),
        scratch_shapes=[pltpu.VMEM((nct * TM + TM + 2 * WINDOW, TM), F32)] * 2,
        compiler_params=_cparams(("parallel", "parallel")),
        name="window_attention",
    )(sink, q, k, vt)


def _unit_operands(u):
    if u < 2 * N_C_HEADS:
        hc = u // 2
        dst, dv = _VT_LAYOUT[hc]
        return True, hc, dst, dv
    g = (u - 2 * N_C_HEADS) // (N_D_HEADS // N_D_KV)
    dst, dv = _VT_LAYOUT[N_C_HEADS + g]
    return False, g, dst, dv


def _attn_cd_kernel(nkt, lam_init, lam_ref, sg_ref, qc_ref, qd_ref, kc_ref, kd_ref, vt_ref, o_ref,
                    qm_s, m_s, acc_s, s_a, s_b):
    mq = qc_ref.shape[1]
    tk = vt_ref.shape[-1]
    lane = lax.broadcasted_iota(jnp.int32, (mq, LANES), 1)
    upper = lane >= HEAD_DIM
    for u in range(N_UNITS):
        src = qc_ref if u < 2 * N_C_HEADS else qd_ref
        pair = (u if u < 2 * N_C_HEADS else u - 2 * N_C_HEADS) // 2
        qp = src[0, :, pair * LANES:(pair + 1) * LANES]
        qm_s[u] = jnp.where(upper == (u % 2 == 1), qp, jnp.zeros_like(qp))
    m_s[...] = jnp.full(m_s.shape, NEG, F32)
    acc_s[...] = jnp.zeros(acc_s.shape, F32)

    def scores(tile, u, dst_s):
        is_c, slab, _, _ = _unit_operands(u)
        k_ref = kc_ref if is_c else kd_ref
        r0 = pl.multiple_of(tile * tk, tk)
        dst_s[u] = _dot_nt(k_ref[0, pl.ds(r0, tk), slab * LANES:(slab + 1) * LANES], qm_s[u])

    def consume(tile, u, src_s):
        _, _, row0, dv = _unit_operands(u)
        rows = dv + ONES_ROWS
        m_old = m_s[u]
        m_new = jnp.maximum(m_old, jnp.max(src_s[u], axis=0, keepdims=True))
        alpha = jnp.exp2(m_old - m_new)
        p = jnp.exp2(src_s[u] - m_new).astype(BF16)
        pv = _dot(vt_ref[0, tile, row0:row0 + rows, :], p)
        acc_s[u, 0:rows, :] = acc_s[u, 0:rows, :] * alpha + pv
        m_s[u] = m_new

    def stage(tile, src_s, dst_s):
        nxt = jnp.minimum(tile + 1, nkt - 1)
        for u in range(N_UNITS):
            scores(nxt, u, dst_s)
            consume(tile, u, src_s)

    for u in range(N_UNITS):
        scores(0, u, s_a)
    stage(0, s_a, s_b)

    def body(pr, carry):
        stage(2 * pr + 1, s_b, s_a)
        stage(2 * pr + 2, s_a, s_b)
        return carry

    lax.fori_loop(0, (nkt - 1) // 2, body, 0)
    if (nkt - 1) % 2 == 1:
        stage(nkt - 1, s_b, s_a)

    lv = lam_ref[...]
    lam = (jnp.exp(jnp.sum(lv[0:1] * lv[1:2], axis=-1, keepdims=True))
           - jnp.exp(jnp.sum(lv[2:3] * lv[3:4], axis=-1, keepdims=True)) + lam_init)
    sg = sg_ref[...]
    dvc = 2 * HEAD_DIM
    for hc in range(N_C_HEADS):
        o1 = acc_s[2 * hc, 0:dvc, :] / acc_s[2 * hc, dvc:dvc + 1, :]
        o2 = acc_s[2 * hc + 1, 0:dvc, :] / acc_s[2 * hc + 1, dvc:dvc + 1, :]
        o = o1 - lam * o2
        ms = jnp.mean(o * o, axis=0, keepdims=True)
        o = o * lax.rsqrt(ms + RMS_EPS) * sg * (1.0 - lam_init)
        o_ref[0, :, hc * LANES:(hc + 1) * LANES] = o.T.astype(BF16)
    for pair in range(N_D_HEADS // 2):
        outs = []
        for half in range(2):
            u = 2 * N_C_HEADS + 2 * pair + half
            outs.append(acc_s[u, 0:HEAD_DIM, :] / acc_s[u, HEAD_DIM:HEAD_DIM + 1, :])
        o2 = jnp.concatenate(outs, axis=0)
        c0 = N_C_HEADS * LANES + pair * LANES
        o_ref[0, :, c0:c0 + LANES] = o2.T.astype(BF16)


def _attn_cd(lam_vec, sg, qc, qd, kc, kd, vt, nct, lam_init):
    b, tall, _ = qc.shape
    nkt, nv, tk = vt.shape[1], vt.shape[2], vt.shape[3]
    mq = TM
    nq = tall // mq - nct
    ow = N_C_HEADS * LANES + N_D_HEADS * HEAD_DIM
    qtile = lambda bi, i: (bi, i + nct, 0)
    return pl.pallas_call(
        functools.partial(_attn_cd_kernel, nkt, lam_init),
        grid=(b, nq),
        in_specs=[pl.BlockSpec((4, HEAD_DIM), lambda bi, i: (0, 0)),
                  pl.BlockSpec((LANES, 1), lambda bi, i: (0, 0)),
                  pl.BlockSpec((1, mq, qc.shape[-1]), qtile),
                  pl.BlockSpec((1, mq, qd.shape[-1]), qtile),
                  pl.BlockSpec((1, tall, kc.shape[-1]), lambda bi, i: (bi, 0, 0)),
                  pl.BlockSpec((1, tall, kd.shape[-1]), lambda bi, i: (bi, 0, 0)),
                  pl.BlockSpec((1, nkt, nv, tk), lambda bi, i: (bi, 0, 0, 0))],
        out_specs=pl.BlockSpec((1, mq, ow), lambda bi, i: (bi, i, 0)),
        out_shape=jax.ShapeDtypeStruct((b, nq * mq, ow), BF16),
        scratch_shapes=[pltpu.VMEM((N_UNITS, mq, LANES), BF16),
                        pltpu.VMEM((N_UNITS, 1, mq), F32),
                        pltpu.VMEM((N_UNITS, 2 * HEAD_DIM + ONES_ROWS, mq), F32),
                        pltpu.VMEM((N_UNITS, tk, mq), F32),
                        pltpu.VMEM((N_UNITS, tk, mq), F32)],
        compiler_params=_cparams(("parallel", "parallel")),
        name="attention_cd",
    )(lam_vec, sg, qc, qd, kc, kd, vt)


def _oproj_kernel(n_parts, widths, nct, alpha, n_exp, *refs):
    o_refs = refs[:n_parts]
    (w_ref, x_ref, ctx_ref, mlat_ref, mctx_ref, lng_ref, lnb_ref, rwt_ref, rb_ref, tri_ref, ltri_ref,
     x1_ref, tg_ref, n8_ref, pos_ref, xs_ref, ys_hbm, zero_s, sem) = refs[n_parts:]
    d = x_ref.shape[-1]
    is_ctx = pl.program_id(1) < nct
    zero_s[...] = jnp.zeros(zero_s.shape, F32)
    tail0 = _tile_step() * xs_ref.shape[0] + TOP_K * x_ref.shape[1]
    clear = pltpu.make_async_copy(zero_s, ys_hbm.at[pl.ds(pl.multiple_of(tail0, ROW_GROUP), zero_s.shape[0])], sem)
    clear.start()
    y = None
    r0 = 0
    for o_ref, wd in zip(o_refs, widths):
        part = _dot(o_ref[0], w_ref[r0:r0 + wd, :])
        y = part if y is None else y + part
        r0 += wd
    m = jnp.where(is_ctx, mctx_ref[...], mlat_ref[0])
    x = jnp.where(is_ctx, ctx_ref[0], x_ref[0])
    x1 = _layernorm(alpha * x + m[:, 2 * d:3 * d] * y, lng_ref[...], lnb_ref[...])
    x1_ref[0] = x1
    hb = (x1 * (1.0 + m[:, 4 * d:5 * d]) + m[:, 3 * d:4 * d]).astype(BF16)
    logits = _dot_nt(rwt_ref[...], hb) + rb_ref[...]
    tm = logits.shape[1]
    row = lax.broadcasted_iota(jnp.int32, logits.shape, 0)
    vals, idxs = [], []
    for _ in range(TOP_K):
        mx = jnp.max(logits, axis=0, keepdims=True)
        idx = jnp.min(jnp.where(logits == mx, row, n_exp), axis=0, keepdims=True)
        vals.append(mx)
        idxs.append(idx)
        logits = jnp.where(row == idx, NEG, logits)
    es = [jnp.exp(v - vals[0]) for v in vals]
    tot = es[0] + es[1] + es[2] + es[3]
    pad_i = jnp.zeros((8 - TOP_K, tm), jnp.int32)
    pad_f = jnp.zeros((8 - TOP_K, tm), F32)
    te = jnp.concatenate(idxs + [pad_i], axis=0)
    tg_ref[0] = jnp.concatenate([e / tot for e in es] + [pad_f], axis=0)
    _, oh = _expert_onehot(te, n_exp)
    n8 = _groups(jnp.sum(oh, axis=1, keepdims=True))
    n8_ref[0] = jnp.broadcast_to(n8, (n_exp, LANES)).astype(jnp.int32)
    lbase = _dot(ltri_ref[...], jnp.broadcast_to(n8, (n_exp, LANES)).astype(BF16))[:, 0:1]
    before = _dot(oh.astype(BF16), tri_ref[...])
    where_to = ROW_GROUP * lbase + before
    pos = [jnp.sum(jnp.where(row == te[k:k + 1, :], where_to, 0.0), axis=0, keepdims=True).astype(jnp.int32)
           for k in range(TOP_K)]
    pos_ref[0] = jnp.concatenate(pos + [pad_i], axis=0)
    for c in range(xs_ref.shape[0] // tm):
        r = lax.broadcasted_iota(jnp.int32, (tm, tm), 0) + c * tm
        hit = r == pos[0]
        for k in range(1, TOP_K):
            hit = hit | (r == pos[k])
        xs_ref[c * tm:(c + 1) * tm, :] = _dot(jnp.where(hit, 1.0, 0.0).astype(BF16), hb)
    clear.wait()


def _oproj(parts, w_out, x, ctx, mlat, mctx, lng, lnb, rwt, rb, nct_out, x_tile_off, alpha):
    b, t, _ = parts[0].shape
    d = x.shape[-1]
    nt = t // TM
    n_exp = rwt.shape[0]
    widths = tuple(p.shape[-1] for p in parts)
    tile = lambda bi, i: (bi, i, 0)
    full = lambda bi, i: (0, 0)
    stream, stream_specs = _stream_specs(x, ctx, nct_out, d, x_tile_off)
    in_specs = [pl.BlockSpec((1, TM, wd), tile) for wd in widths] + [
        pl.BlockSpec(w_out.shape, full)] + stream_specs + [
        pl.BlockSpec((1, 1, N_MOD * d), lambda bi, i: (bi, 0, 0)),
        pl.BlockSpec((1, N_MOD * d), full),
        pl.BlockSpec((1, d), full),
        pl.BlockSpec((1, d), full),
        pl.BlockSpec((n_exp, d), full),
        pl.BlockSpec((n_exp, 1), full),
        pl.BlockSpec((TM, TM), full),
        pl.BlockSpec((n_exp, n_exp), full)]
    sr = _sort_rows(n_exp)
    tri = jnp.asarray(np.triu(np.ones((TM, TM)), 1), BF16)
    ltri = jnp.asarray(np.tril(np.ones((n_exp, n_exp)), -1), BF16)
    small = pl.BlockSpec((1, 8, TM), lambda bi, i: (bi, 0, i))
    return pl.pallas_call(
        functools.partial(_oproj_kernel, len(parts), widths, nct_out, alpha, n_exp),
        grid=(b, nt),
        in_specs=in_specs,
        out_specs=[pl.BlockSpec((1, TM, d), tile),
                   small,
                   pl.BlockSpec((1, n_exp, LANES), lambda bi, i: (bi * nt + i, 0, 0)),
                   small,
                   pl.BlockSpec((sr, d), lambda bi, i: (bi * nt + i, 0)),
                   pl.BlockSpec(memory_space=pl.ANY)],
        out_shape=[jax.ShapeDtypeStruct((b, t, d), F32),
                   jax.ShapeDtypeStruct((b, 8, t), F32),
                   jax.ShapeDtypeStruct((b * nt, n_exp, LANES), jnp.int32),
                   jax.ShapeDtypeStruct((b, 8, t), jnp.int32),
                   jax.ShapeDtypeStruct((b * nt * sr, d), F32),
                   jax.ShapeDtypeStruct(((b * nt + 1) * sr, d), F32)],
        scratch_shapes=[pltpu.VMEM((sr - TOP_K * TM, d), F32), pltpu.SemaphoreType.DMA],
        compiler_params=_cparams(("parallel", "parallel")),
        name="out_proj_norm_router",
    )(*parts, w_out, *stream, mlat, mctx, lng, lnb, rwt, rb, tri, ltri)


def _moe_kernel(be_ref, nx_ref, bv_ref, nu_ref, rprev_ref, rcur_ref, rnext_ref, xs_hbm, wgu_hbm, bgu_ref,
                wd_hbm, bd_ref, ys_in_hbm, ys_hbm, wgu_f, wd_f, wgu_s, wd_s, xbuf, ybuf, sem, gsem, ssem):
    del ys_in_hbm
    blk = pl.program_id(0)
    ff = wd_s.shape[0]
    gpb = xbuf.shape[1] // ROW_GROUP
    par = blk % 2

    def gathers(rec, buf, wait):
        if wait:
            pltpu.make_async_copy(xs_hbm.at[pl.ds(0, xbuf.shape[1])], xbuf.at[buf], gsem.at[buf]).wait()
            return
        for g in range(gpb):
            _group_copy(xs_hbm, rec[g], xbuf.at[buf], g, gsem.at[buf]).start(priority=g % 2)

    def scatters(rec, buf, wait):
        if wait:
            pltpu.make_async_copy(ybuf.at[buf], ys_hbm.at[pl.ds(0, ybuf.shape[1])], ssem.at[buf]).wait()
            return
        for g in range(gpb):
            _group_copy(ybuf.at[buf], g, ys_hbm, rec[gpb + g], ssem.at[buf]).start(priority=g % 2)

    def fetch(e):
        return (pltpu.make_async_copy(wgu_hbm.at[e], wgu_f, sem.at[0]),
                pltpu.make_async_copy(wd_hbm.at[e], wd_f, sem.at[1]))

    @pl.when(blk < nu_ref[0])
    def _():
        e = be_ref[blk]
        new_expert = (blk == 0) | (e != be_ref[jnp.maximum(blk - 1, 0)])

        @pl.when(blk == 0)
        def _():
            for cp in fetch(e):
                cp.start()

        @pl.when(new_expert)
        def _():
            for cp in fetch(e):
                cp.wait()
            rows = 128

            def cast_gu(r, c):
                r0 = pl.multiple_of(r * rows, rows)
                wgu_s[pl.ds(r0, rows), :] = wgu_f[pl.ds(r0, rows), :].astype(BF16)
                return c

            def cast_d(r, c):
                r0 = pl.multiple_of(r * rows, rows)
                wd_s[pl.ds(r0, rows), :] = wd_f[pl.ds(r0, rows), :].astype(BF16)
                return c

            lax.fori_loop(0, wgu_s.shape[0] // rows, cast_gu, 0)
            lax.fori_loop(0, wd_s.shape[0] // rows, cast_d, 0)

            @pl.when(nx_ref[blk] != e)
            def _():
                for cp in fetch(nx_ref[blk]):
                    cp.start()

        @pl.when(blk == 0)
        def _():
            gathers(rcur_ref, 0, False)

        gathers(rcur_ref, par, True)
        half = xbuf.shape[1] // 2

        def ffn(rows):
            gathers(rnext_ref, 1 - par, False)
            gu = _dot(xbuf[par, rows, :].astype(BF16), wgu_s[...]) + bgu_ref[0]
            g = jnp.minimum(gu[:, :ff], SWIGLU_LIMIT)
            u = jnp.clip(gu[:, ff:], -SWIGLU_LIMIT, SWIGLU_LIMIT)
            act = (u + 1.0) * (g / (1.0 + jnp.exp(-SWIGLU_ALPHA * g)))
            ybuf[par, rows, :] = _dot(act.astype(BF16), wd_s[...]) + bd_ref[0]
            if rows != slice(None):
                ybuf[par, half:, :] = jnp.zeros((half, ybuf.shape[2]), F32)
            scatters(rcur_ref, par, False)

        @pl.when(bv_ref[blk] > half)
        def _():
            ffn(slice(None))

        @pl.when(bv_ref[blk] <= half)
        def _():
            ffn(slice(0, half))

        @pl.when(blk > 0)
        def _():
            scatters(rprev_ref, 1 - par, True)

        @pl.when(blk == nu_ref[0] - 1)
        def _():
            gathers(rnext_ref, 1 - par, True)
            scatters(rcur_ref, par, True)


def _moe_experts(place, layer, xs, ys, w_gu, b_gu, w_down, b_down):
    d = xs.shape[1]
    ff2 = w_gu.shape[-1]
    ff = w_down.shape[-2]
    off = layer * w_gu.shape[1]
    n_exp = w_gu.shape[0] * w_gu.shape[1]
    w_gu = w_gu.reshape(n_exp, d, ff2)
    w_down = w_down.reshape(n_exp, ff, d)
    nb = place["n_blocks"]
    last = lambda nu: nu[0] - 1
    bblk = lambda i, be, nx, bv, nu: (be[i], 0, 0)
    rec = lambda f: pl.BlockSpec((SMEM_BLOCK,), f, memory_space=pltpu.SMEM)
    grid_spec = pltpu.PrefetchScalarGridSpec(
        num_scalar_prefetch=4,
        grid=(nb,),
        in_specs=[rec(lambda i, be, nx, bv, nu: (jnp.clip(i - 1, 0, last(nu)),)),
                  rec(lambda i, be, nx, bv, nu: (jnp.minimum(i, last(nu)),)),
                  rec(lambda i, be, nx, bv, nu: (jnp.minimum(i + 1, last(nu)),)),
                  pl.BlockSpec(memory_space=pl.ANY),
                  pl.BlockSpec(memory_space=pl.ANY),
                  pl.BlockSpec((1, 1, ff2), bblk),
                  pl.BlockSpec(memory_space=pl.ANY),
                  pl.BlockSpec((1, 1, d), bblk),
                  pl.BlockSpec(memory_space=pl.ANY)],
        out_specs=pl.BlockSpec(memory_space=pl.ANY),
        scratch_shapes=[pltpu.VMEM((d, ff2), F32), pltpu.VMEM((ff, d), F32),
                        pltpu.VMEM((d, ff2), BF16), pltpu.VMEM((ff, d), BF16),
                        pltpu.VMEM((2, MOE_BM, d), F32), pltpu.VMEM((2, MOE_BM, d), F32),
                        pltpu.SemaphoreType.DMA((2,)), pltpu.SemaphoreType.DMA((2,)),
                        pltpu.SemaphoreType.DMA((2,))])
    records = place["rec"]
    return pl.pallas_call(
        _moe_kernel,
        grid_spec=grid_spec,
        out_shape=jax.ShapeDtypeStruct(ys.shape, F32),
        input_output_aliases={12: 0},
        compiler_params=_cparams(("arbitrary",)),
        name="moe_experts",
    )(place["block_e"] + off, place["block_next_e"] + off, place["block_rows"], place["n_used"],
      records, records, records, xs, w_gu, b_gu.reshape(n_exp, 1, ff2), w_down, b_down.reshape(n_exp, 1, d), ys)


def _expert_onehot(te, n_exp):
    row = lax.broadcasted_iota(jnp.int32, (n_exp, te.shape[1]), 0)
    oh = jnp.zeros(row.shape, F32)
    for k in range(TOP_K):
        oh = oh + (row == te[k:k + 1, :]).astype(F32)
    return row, oh


def _groups(cnt):
    return jnp.floor((cnt + (ROW_GROUP - 1)) / ROW_GROUP)


def _placement(n8, n_asg):
    n_tiles, n_exp = n8.shape
    gpt = _sort_rows(n_exp) // ROW_GROUP
    gpb = MOE_BM // ROW_GROUP
    assert (TOP_K * TM + n_exp * (ROW_GROUP - 1)) // ROW_GROUP < gpt
    ids = jnp.arange(n_exp, dtype=jnp.int32)
    lbase = jnp.cumsum(n8, axis=1) - n8
    cum_incl = jnp.cumsum(n8, axis=0)
    cum_excl = cum_incl - n8
    groups = cum_incl[-1]
    padded = (groups + gpb - 1) // gpb * gpb
    pend = jnp.cumsum(padded)
    pstart = pend - padded
    n_blocks = -(-(n_asg + n_tiles * n_exp * (ROW_GROUP - 1)) // MOE_BM) + n_exp
    n_used = (pend[-1] // gpb).astype(jnp.int32)
    blocks = jnp.arange(n_blocks, dtype=jnp.int32)
    blk = jnp.minimum(blocks, n_used - 1) * gpb
    block_e = jnp.minimum(jnp.sum((pend[None, :] <= blk[:, None]).astype(jnp.int32), axis=1), n_exp - 1)
    later = (padded[None, :] > 0) & (ids[None, :] > block_e[:, None])
    block_next_e = jnp.min(jnp.where(later, ids[None, :], n_exp), axis=1)
    block_next_e = jnp.where(block_next_e == n_exp, block_e, block_next_e).astype(jnp.int32)
    mine = (block_e[:, None] == ids[None, :]).astype(jnp.int32)
    pick = lambda table: jnp.sum(mine[:, :, None] * table.T[None, :, :], axis=1)
    b_start = jnp.sum(mine * pstart[None, :], axis=1)
    b_groups = jnp.sum(mine * groups[None, :], axis=1)
    j_e = (blocks * gpb - b_start)[:, None] + jnp.arange(gpb, dtype=jnp.int32)[None, :]
    valid = (j_e < b_groups[:, None]) & (blocks < n_used)[:, None]
    tile = jnp.sum((pick(cum_incl)[:, None, :] <= j_e[:, :, None]).astype(jnp.int32), axis=2)
    tile = jnp.minimum(tile, n_tiles - 1)
    in_tile = (tile[:, :, None] == jnp.arange(n_tiles, dtype=jnp.int32)[None, None, :]).astype(jnp.int32)
    shift = jnp.sum(in_tile * (pick(lbase) - pick(cum_excl))[:, None, :], axis=2)
    held = tile * gpt + shift + j_e
    pad_dst = n_tiles * gpt + jnp.arange(gpb, dtype=jnp.int32)[None, :]
    src = jnp.where(valid, held, gpt - 1)
    dst = jnp.where(valid, held, pad_dst)
    rec = jnp.concatenate([src, dst, jnp.zeros((n_blocks, SMEM_BLOCK - 2 * gpb), jnp.int32)], axis=1)
    left = b_groups - (blocks - b_start // gpb) * gpb
    block_rows = (jnp.clip(left, 0, gpb) * ROW_GROUP).astype(jnp.int32)
    return dict(rec=rec.reshape(-1), block_e=block_e, block_next_e=block_next_e, block_rows=block_rows,
                n_used=n_used.reshape(1), n_blocks=n_blocks)


def _group_copy(src_ref, src_group, dst_ref, dst_group, sem):
    s0 = pl.multiple_of(src_group * ROW_GROUP, ROW_GROUP)
    d0 = pl.multiple_of(dst_group * ROW_GROUP, ROW_GROUP)
    return pltpu.make_async_copy(src_ref.at[pl.ds(s0, ROW_GROUP)], dst_ref.at[pl.ds(d0, ROW_GROUP)], sem)


def _tile_step():
    return pl.program_id(0) * pl.num_programs(1) + pl.program_id(1)


def _sort_rows(n_exp):
    return -(-(TOP_K * TM + n_exp * ROW_GROUP) // TM) * TM


def _combine_kernel(nct, alpha, ys_ref, pos_ref, tg_ref, x1_ref, mlat_ref, mctx_ref, lng_ref, lnb_ref, o_ref):
    d = x1_ref.shape[-1]
    tm = x1_ref.shape[1]
    is_ctx = pl.program_id(1) < nct
    m = jnp.where(is_ctx, mctx_ref[...], mlat_ref[0])
    pos = pos_ref[0]
    gates = tg_ref[0]
    f = None
    for c in range(ys_ref.shape[0] // tm):
        col = lax.broadcasted_iota(jnp.int32, (tm, tm), 1) + c * tm
        w = jnp.where(col == pos[:, 0:1], gates[:, 0:1], 0.0)
        for k in range(1, TOP_K):
            w = w + jnp.where(col == pos[:, k:k + 1], gates[:, k:k + 1], 0.0)
        part = _dot(w.astype(BF16), ys_ref[c * tm:(c + 1) * tm, :].astype(BF16))
        f = part if f is None else f + part
    o_ref[0] = _layernorm(alpha * x1_ref[0] + m[:, 5 * d:6 * d] * f, lng_ref[...], lnb_ref[...])


def _combine_postnorm(ys, pos_t, tg_t, x1, mlat, mctx, lng, lnb, nct, alpha, n_exp):
    b, t, d = x1.shape
    nt = t // TM
    tile = lambda bi, i: (bi, i, 0)
    full = lambda bi, i: (0, 0)
    return pl.pallas_call(
        functools.partial(_combine_kernel, nct, alpha),
        grid=(b, nt),
        in_specs=[pl.BlockSpec((_sort_rows(n_exp), d), lambda bi, i: (bi * nt + i, 0)),
                  pl.BlockSpec((1, TM, 8), tile),
                  pl.BlockSpec((1, TM, 8), tile),
                  pl.BlockSpec((1, TM, d), tile),
                  pl.BlockSpec((1, 1, N_MOD * d), lambda bi, i: (bi, 0, 0)),
                  pl.BlockSpec((1, N_MOD * d), full),
                  pl.BlockSpec((1, d), full),
                  pl.BlockSpec((1, d), full)],
        out_specs=pl.BlockSpec((1, TM, d), tile),
        out_shape=jax.ShapeDtypeStruct((b, t, d), F32),
        compiler_params=_cparams(("parallel", "parallel")),
        name="combine_post_norm",
    )(ys, pos_t, tg_t, x1, mlat, mctx, lng, lnb)


def _rope_tables(ctx_len, seq):
    t = np.arange(seq)
    row = (t // GRID_W).astype(np.float64)
    col = (t % GRID_W).astype(np.float64)
    nf = HEAD_DIM // 4
    inv = ROPE_THETA ** (-np.arange(nf, dtype=np.float64) / nf)
    ar = row[:, None] * inv[None, :]
    ac = col[:, None] * inv[None, :]
    ang = np.concatenate([ar, ar, ac, ac], axis=-1)
    cos = np.concatenate([np.ones((ctx_len, HEAD_DIM)), np.cos(ang)], axis=0)
    sin = np.concatenate([np.zeros((ctx_len, HEAD_DIM)), np.sin(ang)], axis=0)
    sign = np.where((np.arange(HEAD_DIM) % 32) < 16, -1.0, 1.0)[None, :]
    cos2 = np.tile(cos, (1, LANES // HEAD_DIM))
    sin2 = np.tile(sin * sign, (1, LANES // HEAD_DIM))
    return jnp.asarray(cos2, F32), jnp.asarray(sin2, F32)


def _channel_dft():
    c = np.arange(HEAD_DIM)
    ang = 2.0 * np.pi * ((c[:, None] * c[None, :]) % HEAD_DIM) / HEAD_DIM
    eye = np.eye(4)
    cs = np.concatenate([np.kron(eye, np.cos(ang)), np.kron(eye, np.sin(ang))], axis=1)
    return jnp.asarray(cs, BF16)


def _group_mean_matrix():
    bd = np.kron(np.eye(LANES // HEAD_DIM), np.full((HEAD_DIM, HEAD_DIM), 1.0 / HEAD_DIM))
    return jnp.asarray(bd, BF16)


def _dup_heads(wk, n_heads):
    d = wk.shape[0]
    return jnp.broadcast_to(wk.reshape(d, n_heads, 1, HEAD_DIM), (d, n_heads, 2, HEAD_DIM)).reshape(d, n_heads * LANES)


def _lambda_init(layer):
    return 0.8 - 0.6 * math.exp(-0.3 * layer)


def _moe_postnorm(routed, mlat, mctx, lng, lnb, nct, alpha, layer, w_gu, b_gu, w_down, b_down):
    x1, tg, n8, pos, xs, ys = routed
    n_exp = w_gu.shape[1]
    b, t, _ = x1.shape
    place = _placement(n8[:, :, 0], b * t * TOP_K)
    ys = _moe_experts(place, layer, xs, ys, w_gu, b_gu, w_down, b_down)
    pos_t, tg_t = jnp.transpose(pos, (0, 2, 1)), jnp.transpose(tg, (0, 2, 1))
    return _combine_postnorm(ys, pos_t, tg_t, x1, mlat, mctx, lng, lnb, nct, alpha, n_exp)


def kernel(x, c, ctx, c_ctx, mod_w, mod_b, ln_g, ln_b, ab_w_in, ab_sink, ab_w_out,
           cd_w_in, cd_lambda, cd_subln_g, cd_q_norm_g, cd_k_norm_g, cd_w_out,
           router_w, router_b, expert_w_gu, expert_b_gu, expert_w_down, expert_b_down):
    b, s, d = x.shape
    n_ctx = ctx.shape[1]
    depth = mod_w.shape[0]
    n_exp = router_w.shape[-1]
    assert d == 16 * HEAD_DIM and n_ctx % TM == 0 and s % TM == 0 and s % GRID_W == 0
    nct = n_ctx // TM
    alpha = (2 * depth) ** 0.25

    cos, sin = _rope_tables(n_ctx, s)
    cs_dft = _channel_dft()
    bd = _group_mean_matrix()
    mods = _mod_vectors(c, c_ctx, mod_w, mod_b)

    xs, xc = x, ctx
    for l in range(depth):
        last = l == depth - 1
        i = l // 2
        mlat = mods[l, :b].reshape(b, 1, N_MOD * d)
        mctx = mods[l, b:b + 1]
        lng1, lnb1 = ln_g[l, 0].reshape(1, d), ln_b[l, 0].reshape(1, d)
        lng2, lnb2 = ln_g[l, 1].reshape(1, d), ln_b[l, 1].reshape(1, d)
        rwt = router_w[l].T.astype(BF16)
        rb = router_b[l].reshape(n_exp, 1)
        if l % 2 == 0:
            w = ab_w_in[i]
            w_n = jnp.concatenate([w[:, :1024], _dup_heads(w[:, 1024:1280], 4)], axis=1).astype(BF16)
            wvt = w[:, 1280:1536].T.astype(BF16)
            uw, q, k, vt = _proj_ab(xs, xc, mlat, mctx, w_n, wvt, cos, sin, cs_dft, nct)
            oa = jnp.concatenate([_fourier(uw[:, :n_ctx]), _fourier(uw[:, n_ctx:])], axis=1)
            ob = _win_attn(ab_sink[i], q, k, vt, nct)
            if last:
                parts = [oa[:, n_ctx:], ob[:, n_ctx:]]
            else:
                parts = [oa, ob]
            w_out = ab_w_out[i].astype(BF16)
        else:
            w = cd_w_in[i]
            w_n = jnp.concatenate([w[:, :1536], _dup_heads(w[:, 1536:1664], 2)], axis=1).astype(BF16)
            wvt = w[:, 1664:2304].T.astype(BF16)
            gq = jnp.tile(cd_q_norm_g[i], 2).reshape(1, LANES)
            gk = jnp.tile(cd_k_norm_g[i], 2).reshape(1, LANES)
            qc, qd, kc, kd, vt = _proj_cd(xs, xc, mlat, mctx, w_n, wvt, cos, sin, bd, gq, gk, nct)
            sg = cd_subln_g[i].reshape(LANES, 1)
            o_lat = _attn_cd(cd_lambda[i], sg, qc, qd, kc, kd, vt, nct, _lambda_init(l))
            if last:
                parts = [o_lat]
            else:
                raise NotImplementedError("context outputs of a differential/axial layer")
            w_out = cd_w_out[i].astype(BF16)
        experts = (l, expert_w_gu, expert_b_gu, expert_w_down, expert_b_down)
        if last:
            routed = _oproj(parts, w_out, xs, xc, mlat, mctx, lng1, lnb1, rwt, rb, 0, nct, alpha)
            return _moe_postnorm(routed, mlat, mctx, lng2, lnb2, 0, alpha, *experts)
        routed = _oproj(parts, w_out, xs, xc, mlat, mctx, lng1, lnb1, rwt, rb, nct, 0, alpha)
        xs, xc = _moe_postnorm(routed, mlat, mctx, lng2, lnb2, nct, alpha, *experts), None
    return xs[:, n_ctx:]
```

```python
import functools
import math

import jax
import jax.numpy as jnp
import numpy as np
from jax import lax
from jax.experimental import pallas as pl
from jax.experimental.pallas import tpu as pltpu

F32 = jnp.float32
BF16 = jnp.bfloat16

HEAD_DIM = 64
GRID_W = 64
WINDOW = 128
ROPE_THETA = 10000.0
LN_EPS = 1e-6
RMS_EPS = 1e-6
N_MOD = 6
TOP_K = 4
SWIGLU_LIMIT = 7.0
SWIGLU_ALPHA = 1.702
NEG = -1e30
LOG2E = 1.4426950408889634
QSCALE = HEAD_DIM ** -0.5 * LOG2E

N_C_HEADS = 4
N_D_HEADS = 8
N_D_KV = 2
N_UNITS = 2 * N_C_HEADS + N_D_HEADS
ONES_ROWS = 16
_VT_LAYOUT = tuple((h * (2 * HEAD_DIM + ONES_ROWS), 2 * HEAD_DIM) for h in range(N_C_HEADS)) + tuple(
    (N_C_HEADS * (2 * HEAD_DIM + ONES_ROWS) + g * (HEAD_DIM + ONES_ROWS), HEAD_DIM) for g in range(N_D_KV))
VT_ROWS = _VT_LAYOUT[-1][0] + HEAD_DIM + ONES_ROWS

LANES = 128
TM = 256
MOE_BM = 512
ROW_GROUP = 8
SMEM_BLOCK = 1024
VMEM_LIMIT = 56 * 1024 * 1024


def _dot(a, b):
    return jnp.dot(a, b, preferred_element_type=F32)


def _dot_nt(a, b):
    return lax.dot_general(a, b, (((1,), (1,)), ((), ())), preferred_element_type=F32)


def _cparams(sem):
    return pltpu.CompilerParams(dimension_semantics=sem, vmem_limit_bytes=VMEM_LIMIT)


def _layernorm(z, g, b):
    mu = jnp.mean(z, axis=-1, keepdims=True)
    d = z - mu
    var = jnp.mean(d * d, axis=-1, keepdims=True)
    return d * lax.rsqrt(var + LN_EPS) * g + b


def _rope128(x, cos, sin_signed):
    lane = lax.broadcasted_iota(jnp.int32, x.shape, 1)
    lo = (lane % 32) < 16
    rot = jnp.where(lo, pltpu.roll(x, LANES - 16, 1), pltpu.roll(x, 16, 1))
    return x * cos + rot * sin_signed


def _rmsnorm128(x, g, bd):
    x2 = x * x
    hi = x2.astype(BF16)
    lo = (x2 - hi.astype(F32)).astype(BF16)
    ms = _dot(hi, bd) + _dot(lo, bd)
    return x * lax.rsqrt(ms + RMS_EPS) * g


U32 = jnp.uint32
_HI16 = 0xFFFF0000


def _pack_bf16_pairs(x):
    n = x.shape[1] // 2
    bits = lax.bitcast_convert_type(x, U32)
    return (bits[:, :n] >> 16) | (bits[:, n:] & U32(_HI16))


def _unpack_bf16_pairs(p):
    lo = lax.bitcast_convert_type(p << 16, F32).astype(BF16)
    hi = lax.bitcast_convert_type(p & U32(_HI16), F32).astype(BF16)
    return lo, hi


def _modulated(x_ref, ctx_ref, mlat_ref, mctx_ref, is_ctx, d):
    m = jnp.where(is_ctx, mctx_ref[...], mlat_ref[0])
    x = jnp.where(is_ctx, ctx_ref[0], x_ref[0])
    return (x * (1.0 + m[:, d:2 * d]) + m[:, 0:d]).astype(BF16)


def _stream_specs(x, ctx, nct, d, off=0):
    if ctx is None:
        return (x, x), [pl.BlockSpec((1, TM, d), lambda bi, i: (bi, i + off, 0)),
                        pl.BlockSpec((1, TM, d), lambda bi, i: (bi, jnp.minimum(i, max(nct - 1, 0)), 0))]
    return (x, ctx), [pl.BlockSpec((1, TM, d), lambda bi, i: (bi, jnp.maximum(i - nct, 0), 0)),
                      pl.BlockSpec((1, TM, d), lambda bi, i: (bi, jnp.minimum(i, nct - 1), 0))]


def _mod_kernel(c_ref, w_ref, b_ref, o_ref):
    c = c_ref[...]
    s = (c / (1.0 + jnp.exp(-c))).astype(BF16)
    o_ref[0] = _dot(s, w_ref[0].astype(BF16)) + b_ref[0]


def _mod_vectors(c, c_ctx, mod_w, mod_b):
    depth, d, n = mod_w.shape
    b = c.shape[0]
    rows = 8 * (-(-(b + 1) // 8))
    cs = jnp.zeros((rows, d), F32).at[:b].set(c).at[b].set(c_ctx)
    tn = 1536
    out = pl.pallas_call(
        _mod_kernel,
        grid=(depth, n // tn),
        in_specs=[pl.BlockSpec((rows, d), lambda l, j: (0, 0)),
                  pl.BlockSpec((1, d, tn), lambda l, j: (l, 0, j)),
                  pl.BlockSpec((1, 1, tn), lambda l, j: (l, 0, j))],
        out_specs=pl.BlockSpec((1, rows, tn), lambda l, j: (l, 0, j)),
        out_shape=jax.ShapeDtypeStruct((depth, rows, n), F32),
        compiler_params=_cparams(("parallel", "parallel")),
        name="mod_vectors",
    )(cs, mod_w, mod_b.reshape(depth, 1, n))
    return out


def _proj_ab_kernel(nct, x_ref, ctx_ref, mlat_ref, mctx_ref, w_ref, wvt_ref, cos_ref, sin_ref, cs_ref,
                    uw_ref, q_ref, k_ref, vt_ref):
    is_ctx = pl.program_id(1) < nct
    d = x_ref.shape[-1]
    h = _modulated(x_ref, ctx_ref, mlat_ref, mctx_ref, is_ctx, d)
    p = _dot(h, w_ref[...])
    uw_ref[0] = _dot(p[:, 0:256].astype(BF16), cs_ref[...]).astype(BF16)
    cos = cos_ref[...]
    sin = sin_ref[...]
    for j in range(6):
        c0 = 256 + j * LANES
        q_ref[0, :, j * LANES:(j + 1) * LANES] = (_rope128(p[:, c0:c0 + LANES], cos, sin) * QSCALE).astype(BF16)
    for j in range(4):
        c0 = 1024 + j * LANES
        k_ref[0, :, j * LANES:(j + 1) * LANES] = _rope128(p[:, c0:c0 + LANES], cos, sin).astype(BF16)
    vt = _dot_nt(wvt_ref[...], h).astype(BF16)
    for c in range(TM // LANES):
        vt_ref[0, c] = vt[:, c * LANES:(c + 1) * LANES]


def _proj_ab(x, ctx, mlat, mctx, w, wvt, cos, sin, cs, nct):
    b, d = x.shape[0], x.shape[2]
    tall = x.shape[1] + (0 if ctx is None else ctx.shape[1])
    stream, stream_specs = _stream_specs(x, ctx, nct, d)
    nt = tall // TM
    nw = w.shape[1]
    nchunk = tall // LANES
    cpt = TM // LANES
    return pl.pallas_call(
        functools.partial(_proj_ab_kernel, nct),
        grid=(b, nt),
        in_specs=stream_specs + [
                  pl.BlockSpec((1, 1, N_MOD * d), lambda bi, i: (bi, 0, 0)),
                  pl.BlockSpec((1, N_MOD * d), lambda bi, i: (0, 0)),
                  pl.BlockSpec((d, nw), lambda bi, i: (0, 0)),
                  pl.BlockSpec((256, d), lambda bi, i: (0, 0)),
                  pl.BlockSpec((TM, LANES), lambda bi, i: (i, 0)),
                  pl.BlockSpec((TM, LANES), lambda bi, i: (i, 0)),
                  pl.BlockSpec((256, 512), lambda bi, i: (0, 0))],
        out_specs=[pl.BlockSpec((1, TM, 512), lambda bi, i: (bi, i, 0)),
                   pl.BlockSpec((1, TM, 768), lambda bi, i: (bi, i, 0)),
                   pl.BlockSpec((1, TM, 512), lambda bi, i: (bi, i, 0)),
                   pl.BlockSpec((1, cpt, 256, LANES), lambda bi, i: (bi, i, 0, 0))],
        out_shape=[jax.ShapeDtypeStruct((b, tall, 512), BF16),
                   jax.ShapeDtypeStruct((b, tall, 768), BF16),
                   jax.ShapeDtypeStruct((b, tall, 512), BF16),
                   jax.ShapeDtypeStruct((b, nchunk, 256, LANES), BF16)],
        compiler_params=_cparams(("parallel", "parallel")),
        name="proj_ab",
    )(*stream, mlat, mctx, w, wvt, cos, sin, cs)


def _proj_cd_kernel(nct, x_ref, ctx_ref, mlat_ref, mctx_ref, w_ref, wvt_ref, cos_ref, sin_ref, bd_ref,
                    gq_ref, gk_ref, qc_ref, qd_ref, kc_ref, kd_ref, vt_ref):
    is_ctx = pl.program_id(1) < nct
    d = x_ref.shape[-1]
    h = _modulated(x_ref, ctx_ref, mlat_ref, mctx_ref, is_ctx, d)
    p = _dot(h, w_ref[...])
    cos = cos_ref[...]
    sin = sin_ref[...]
    bd = bd_ref[...]
    gq = gq_ref[...]
    gk = gk_ref[...]
    for j in range(4):
        sl = slice(j * LANES, (j + 1) * LANES)
        qc_ref[0, :, sl] = (_rope128(p[:, j * LANES:(j + 1) * LANES], cos, sin) * QSCALE).astype(BF16)
        c0 = 512 + j * LANES
        qd_ref[0, :, sl] = (_rope128(_rmsnorm128(p[:, c0:c0 + LANES], gq, bd), cos, sin) * QSCALE).astype(BF16)
        c0 = 1024 + j * LANES
        kc_ref[0, :, sl] = _rope128(p[:, c0:c0 + LANES], cos, sin).astype(BF16)
    for j in range(2):
        c0 = 1536 + j * LANES
        kd_ref[0, :, j * LANES:(j + 1) * LANES] = _rope128(
            _rmsnorm128(p[:, c0:c0 + LANES], gk, bd), cos, sin).astype(BF16)
    vt = _dot_nt(wvt_ref[...], h).astype(BF16)
    ones = jnp.ones((ONES_ROWS, vt.shape[1]), BF16)
    src = 0
    for dst, dv in _VT_LAYOUT:
        vt_ref[0, 0, dst:dst + dv, :] = vt[src:src + dv, :]
        vt_ref[0, 0, dst + dv:dst + dv + ONES_ROWS, :] = ones
        src += dv


def _proj_cd(x, ctx, mlat, mctx, w, wvt, cos, sin, bd, gq, gk, nct):
    b, d = x.shape[0], x.shape[2]
    tall = x.shape[1] + (0 if ctx is None else ctx.shape[1])
    stream, stream_specs = _stream_specs(x, ctx, nct, d)
    nt = tall // TM
    nw = w.shape[1]
    nv = VT_ROWS
    full = lambda bi, i: (0, 0)
    tile = lambda bi, i: (bi, i, 0)
    return pl.pallas_call(
        functools.partial(_proj_cd_kernel, nct),
        grid=(b, nt),
        in_specs=stream_specs + [
                  pl.BlockSpec((1, 1, N_MOD * d), lambda bi, i: (bi, 0, 0)),
                  pl.BlockSpec((1, N_MOD * d), full),
                  pl.BlockSpec((d, nw), full),
                  pl.BlockSpec(wvt.shape, full),
                  pl.BlockSpec((TM, LANES), lambda bi, i: (i, 0)),
                  pl.BlockSpec((TM, LANES), lambda bi, i: (i, 0)),
                  pl.BlockSpec((LANES, LANES), full),
                  pl.BlockSpec((1, LANES), full),
                  pl.BlockSpec((1, LANES), full)],
        out_specs=[pl.BlockSpec((1, TM, 512), tile),
                   pl.BlockSpec((1, TM, 512), tile),
                   pl.BlockSpec((1, TM, 512), tile),
                   pl.BlockSpec((1, TM, 256), tile),
                   pl.BlockSpec((1, 1, nv, TM), lambda bi, i: (bi, i, 0, 0))],
        out_shape=[jax.ShapeDtypeStruct((b, tall, 512), BF16),
                   jax.ShapeDtypeStruct((b, tall, 512), BF16),
                   jax.ShapeDtypeStruct((b, tall, 512), BF16),
                   jax.ShapeDtypeStruct((b, tall, 256), BF16),
                   jax.ShapeDtypeStruct((b, nt, nv, TM), BF16)],
        compiler_params=_cparams(("parallel", "parallel")),
        name="proj_cd",
    )(*stream, mlat, mctx, w, wvt, cos, sin, bd, gq, gk)


def _fourier_kernel(nb, scale, uw_ref, ca_ref, sa_ref, cb_ref, sb_ref, o_ref):
    j = pl.program_id(0)
    ca = ca_ref[pl.ds(j, 1), :]
    sa = sa_ref[pl.ds(j, 1), :]
    cb = cb_ref[...]
    sb = sb_ref[...]
    ct = (ca * cb - sa * sb).astype(BF16)
    nst = (-(sa * cb + ca * sb)).astype(BF16)
    for bi in range(nb):
        acc = _dot(ct, uw_ref[bi, :, 0:256]) + _dot(nst, uw_ref[bi, :, 256:512])
        o_ref[bi] = (acc * scale).astype(BF16)


def _dft_tables(t, tmf):
    k = np.arange(t, dtype=np.int64)
    j1 = np.arange(t // tmf, dtype=np.int64) * tmf
    j0 = np.arange(tmf, dtype=np.int64)
    aa = (2.0 * np.pi / t) * ((j1[:, None] * k[None, :]) % t)
    ab = (2.0 * np.pi / t) * ((j0[:, None] * k[None, :]) % t)
    f = lambda a: jnp.asarray(a, F32)
    return f(np.cos(aa)), f(np.sin(aa)), f(np.cos(ab)), f(np.sin(ab))


def _fourier(uw):
    b, t, _ = uw.shape
    tmf = min(128, t)
    ca, sa, cb, sb = _dft_tables(t, tmf)
    scale = 1.0 / math.sqrt(t * HEAD_DIM)
    full2 = lambda j: (0, 0)
    return pl.pallas_call(
        functools.partial(_fourier_kernel, b, scale),
        grid=(t // tmf,),
        in_specs=[pl.BlockSpec((b, t, 512), lambda j: (0, 0, 0)),
                  pl.BlockSpec((t // tmf, t), full2),
                  pl.BlockSpec((t // tmf, t), full2),
                  pl.BlockSpec((tmf, t), full2),
                  pl.BlockSpec((tmf, t), full2)],
        out_specs=pl.BlockSpec((b, tmf, 256), lambda j: (0, j, 0)),
        out_shape=jax.ShapeDtypeStruct((b, t, 256), BF16),
        compiler_params=_cparams(("parallel",)),
        name="fourier_mix",
    )(uw, ca, sa, cb, sb)


def _win_attn_kernel(nct, nchunk, sink_ref, q_ref, k_ref, vt_ref, o_ref, s_a, s_b):
    i = pl.program_id(1)
    is_ctx = i < nct
    j = i - nct
    ctx_rows = nct * TM
    ctx_chunks = ctx_rows // LANES
    lw = TM + 2 * WINDOW
    lchunks = lw // LANES
    cs = jnp.clip(ctx_chunks + (TM // LANES) * j - WINDOW // LANES, 0, nchunk - lchunks)
    rs = pl.multiple_of(cs * LANES, LANES)
    qpos = j * TM + lax.broadcasted_iota(jnp.int32, (1, TM), 1)
    kpos = cs * LANES - ctx_rows + lax.broadcasted_iota(jnp.int32, (lw, 1), 0)
    allowed = (jnp.abs(qpos - kpos) <= WINDOW) & (kpos >= 0) & jnp.logical_not(is_ctx)
    lane = lax.broadcasted_iota(jnp.int32, (TM, LANES), 1)
    n_heads = q_ref.shape[-1] // HEAD_DIM
    group = n_heads // (k_ref.shape[-1] // LANES)

    def scores(hq, dst_s):
        pair, half = divmod(hq, 2)
        g = hq // group
        qp = q_ref[0, :, pair * LANES:(pair + 1) * LANES]
        qm = jnp.where((lane >= HEAD_DIM) == (half == 1), qp, jnp.zeros_like(qp))
        dst_s[0:ctx_rows, :] = _dot_nt(k_ref[0, 0:ctx_rows, g * LANES:(g + 1) * LANES], qm)
        s_l = _dot_nt(k_ref[0, pl.ds(rs, lw), g * LANES:(g + 1) * LANES], qm)
        dst_s[ctx_rows:ctx_rows + lw, :] = jnp.where(allowed, s_l, NEG)

    def consume(hq, src_s):
        g = hq // group
        s = src_s[...]
        sk = sink_ref[hq] * LOG2E
        m = jnp.maximum(jnp.max(s, axis=0, keepdims=True), sk)
        p = jnp.exp2(s - m)
        l = jnp.sum(p, axis=0, keepdims=True) + jnp.exp2(sk - m)
        p = p.astype(BF16)
        vrows = slice(g * HEAD_DIM, (g + 1) * HEAD_DIM)
        acc = jnp.zeros((HEAD_DIM, TM), F32)
        for c in range(ctx_chunks):
            acc = acc + _dot(vt_ref[0, c, vrows, :], p[c * LANES:(c + 1) * LANES, :])
        for c in range(lchunks):
            r0 = ctx_rows + c * LANES
            acc = acc + _dot(vt_ref[0, cs + c, vrows, :], p[r0:r0 + LANES, :])
        return acc / l

    bufs = (s_a, s_b)
    scores(0, s_a)
    outs = []
    for hq in range(n_heads):
        if hq + 1 < n_heads:
            scores(hq + 1, bufs[(hq + 1) % 2])
        outs.append(consume(hq, bufs[hq % 2]))
        if hq % 2 == 1:
            pair = hq // 2
            o2 = jnp.concatenate(outs, axis=0)
            o_ref[0, :, pair * LANES:(pair + 1) * LANES] = o2.T.astype(BF16)
            outs = []


def _win_attn(sink, q, k, vt, nct):
    b, tall, qw = q.shape
    nt = tall // TM
    nchunk = vt.shape[1]
    return pl.pallas_call(
        functools.partial(_win_attn_kernel, nct, nchunk),
        grid=(b, nt),
        in_specs=[pl.BlockSpec(memory_space=pltpu.SMEM),
                  pl.BlockSpec((1, TM, qw), lambda bi, i: (bi, i, 0)),
                  pl.BlockSpec((1, tall, k.shape[-1]), lambda bi, i: (bi, 0, 0)),
                  pl.BlockSpec((1, nchunk, vt.shape[2], LANES), lambda bi, i: (bi, 0, 0, 0))],
        out_specs=pl.BlockSpec((1, TM, qw), lambda bi, i: (bi, i, 0)),
        out_shape=jax.ShapeDtypeStruct((b, tall, qw), BF16),
        scratch_shapes=[pltpu.VMEM((nct * TM + TM + 2 * WINDOW, TM), F32)] * 2,
        compiler_params=_cparams(("parallel", "parallel")),
        name="window_attention",
    )(sink, q, k, vt)


def _unit_operands(u):
    if u < 2 * N_C_HEADS:
        hc = u // 2
        dst, dv = _VT_LAYOUT[hc]
        return True, hc, dst, dv
    g = (u - 2 * N_C_HEADS) // (N_D_HEADS // N_D_KV)
    dst, dv = _VT_LAYOUT[N_C_HEADS + g]
    return False, g, dst, dv


def _attn_cd_kernel(nkt, lam_init, lam_ref, sg_ref, qc_ref, qd_ref, kc_ref, kd_ref, vt_ref, o_ref,
                    qm_s, m_s, acc_s, s_a, s_b):
    mq = qc_ref.shape[1]
    tk = vt_ref.shape[-1]
    lane = lax.broadcasted_iota(jnp.int32, (mq, LANES), 1)
    upper = lane >= HEAD_DIM
    for u in range(N_UNITS):
        src = qc_ref if u < 2 * N_C_HEADS else qd_ref
        pair = (u if u < 2 * N_C_HEADS else u - 2 * N_C_HEADS) // 2
        qp = src[0, :, pair * LANES:(pair + 1) * LANES]
        qm_s[u] = jnp.where(upper == (u % 2 == 1), qp, jnp.zeros_like(qp))
    m_s[...] = jnp.full(m_s.shape, NEG, F32)
    acc_s[...] = jnp.zeros(acc_s.shape, F32)

    def scores(tile, u, dst_s):
        is_c, slab, _, _ = _unit_operands(u)
        k_ref = kc_ref if is_c else kd_ref
        r0 = pl.multiple_of(tile * tk, tk)
        dst_s[u] = _dot_nt(k_ref[0, pl.ds(r0, tk), slab * LANES:(slab + 1) * LANES], qm_s[u])

    def consume(tile, u, src_s):
        _, _, row0, dv = _unit_operands(u)
        rows = dv + ONES_ROWS
        m_old = m_s[u]
        m_new = jnp.maximum(m_old, jnp.max(src_s[u], axis=0, keepdims=True))
        alpha = jnp.exp2(m_old - m_new)
        p = jnp.exp2(src_s[u] - m_new).astype(BF16)
        pv = _dot(vt_ref[0, tile, row0:row0 + rows, :], p)
        acc_s[u, 0:rows, :] = acc_s[u, 0:rows, :] * alpha + pv
        m_s[u] = m_new

    def stage(tile, src_s, dst_s):
        nxt = jnp.minimum(tile + 1, nkt - 1)
        for u in range(N_UNITS):
            scores(nxt, u, dst_s)
            consume(tile, u, src_s)

    for u in range(N_UNITS):
        scores(0, u, s_a)
    stage(0, s_a, s_b)

    def body(pr, carry):
        stage(2 * pr + 1, s_b, s_a)
        stage(2 * pr + 2, s_a, s_b)
        return carry

    lax.fori_loop(0, (nkt - 1) // 2, body, 0)
    if (nkt - 1) % 2 == 1:
        stage(nkt - 1, s_b, s_a)

    lv = lam_ref[...]
    lam = (jnp.exp(jnp.sum(lv[0:1] * lv[1:2], axis=-1, keepdims=True))
           - jnp.exp(jnp.sum(lv[2:3] * lv[3:4], axis=-1, keepdims=True)) + lam_init)
    sg = sg_ref[...]
    dvc = 2 * HEAD_DIM
    for hc in range(N_C_HEADS):
        o1 = acc_s[2 * hc, 0:dvc, :] / acc_s[2 * hc, dvc:dvc + 1, :]
        o2 = acc_s[2 * hc + 1, 0:dvc, :] / acc_s[2 * hc + 1, dvc:dvc + 1, :]
        o = o1 - lam * o2
        ms = jnp.mean(o * o, axis=0, keepdims=True)
        o = o * lax.rsqrt(ms + RMS_EPS) * sg * (1.0 - lam_init)
        o_ref[0, :, hc * LANES:(hc + 1) * LANES] = o.T.astype(BF16)
    for pair in range(N_D_HEADS // 2):
        outs = []
        for half in range(2):
            u = 2 * N_C_HEADS + 2 * pair + half
            outs.append(acc_s[u, 0:HEAD_DIM, :] / acc_s[u, HEAD_DIM:HEAD_DIM + 1, :])
        o2 = jnp.concatenate(outs, axis=0)
        c0 = N_C_HEADS * LANES + pair * LANES
        o_ref[0, :, c0:c0 + LANES] = o2.T.astype(BF16)


def _attn_cd(lam_vec, sg, qc, qd, kc, kd, vt, nct, lam_init):
    b, tall, _ = qc.shape
    nkt, nv, tk = vt.shape[1], vt.shape[2], vt.shape[3]
    mq = TM
    nq = tall // mq - nct
    ow = N_C_HEADS * LANES + N_D_HEADS * HEAD_DIM
    qtile = lambda bi, i: (bi, i + nct, 0)
    return pl.pallas_call(
        functools.partial(_attn_cd_kernel, nkt, lam_init),
        grid=(b, nq),
        in_specs=[pl.BlockSpec((4, HEAD_DIM), lambda bi, i: (0, 0)),
                  pl.BlockSpec((LANES, 1), lambda bi, i: (0, 0)),
                  pl.BlockSpec((1, mq, qc.shape[-1]), qtile),
                  pl.BlockSpec((1, mq, qd.shape[-1]), qtile),
                  pl.BlockSpec((1, tall, kc.shape[-1]), lambda bi, i: (bi, 0, 0)),
                  pl.BlockSpec((1, tall, kd.shape[-1]), lambda bi, i: (bi, 0, 0)),
                  pl.BlockSpec((1, nkt, nv, tk), lambda bi, i: (bi, 0, 0, 0))],
        out_specs=pl.BlockSpec((1, mq, ow), lambda bi, i: (bi, i, 0)),
        out_shape=jax.ShapeDtypeStruct((b, nq * mq, ow), BF16),
        scratch_shapes=[pltpu.VMEM((N_UNITS, mq, LANES), BF16),
                        pltpu.VMEM((N_UNITS, 1, mq), F32),
                        pltpu.VMEM((N_UNITS, 2 * HEAD_DIM + ONES_ROWS, mq), F32),
                        pltpu.VMEM((N_UNITS, tk, mq), F32),
                        pltpu.VMEM((N_UNITS, tk, mq), F32)],
        compiler_params=_cparams(("parallel", "parallel")),
        name="attention_cd",
    )(lam_vec, sg, qc, qd, kc, kd, vt)


def _oproj_kernel(n_parts, widths, nct, alpha, n_exp, *refs):
    o_refs = refs[:n_parts]
    (w_ref, x_ref, ctx_ref, mlat_ref, mctx_ref, lng_ref, lnb_ref, rwt_ref, rb_ref, tri_ref, ltri_ref,
     x1_ref, tg_ref, n8_ref, pos_ref, xs_ref, ys_hbm, zero_s, sem) = refs[n_parts:]
    d = x_ref.shape[-1]
    is_ctx = pl.program_id(1) < nct
    zero_s[...] = jnp.zeros(zero_s.shape, zero_s.dtype)
    tail0 =_tile_step() * xs_ref.shape[0] + TOP_K * x_ref.shape[1]
    clear = pltpu.make_async_copy(zero_s, ys_hbm.at[pl.ds(pl.multiple_of(tail0, ROW_GROUP), zero_s.shape[0])], sem)
    clear.start()
    y = None
    r0 = 0
    for o_ref, wd in zip(o_refs, widths):
        part = _dot(o_ref[0], w_ref[r0:r0 + wd, :])
        y = part if y is None else y + part
        r0 += wd
    m = jnp.where(is_ctx, mctx_ref[...], mlat_ref[0])
    x = jnp.where(is_ctx, ctx_ref[0], x_ref[0])
    x1 = _layernorm(alpha * x + m[:, 2 * d:3 * d] * y, lng_ref[...], lnb_ref[...])
    x1_ref[0] = x1
    hb = (x1 * (1.0 + m[:, 4 * d:5 * d]) + m[:, 3 * d:4 * d]).astype(BF16)
    logits = _dot_nt(rwt_ref[...], hb) + rb_ref[...]
    tm = logits.shape[1]
    row = lax.broadcasted_iota(jnp.int32, logits.shape, 0)
    vals, idxs = [], []
    for _ in range(TOP_K):
        mx = jnp.max(logits, axis=0, keepdims=True)
        idx = jnp.min(jnp.where(logits == mx, row, n_exp), axis=0, keepdims=True)
        vals.append(mx)
        idxs.append(idx)
        logits = jnp.where(row == idx, NEG, logits)
    es = [jnp.exp(v - vals[0]) for v in vals]
    tot = es[0] + es[1] + es[2] + es[3]
    pad_i = jnp.zeros((8 - TOP_K, tm), jnp.int32)
    pad_f = jnp.zeros((8 - TOP_K, tm), F32)
    te = jnp.concatenate(idxs + [pad_i], axis=0)
    tg_ref[0] = jnp.concatenate([e / tot for e in es] + [pad_f], axis=0)
    _, oh = _expert_onehot(te, n_exp)
    n8 = _groups(jnp.sum(oh, axis=1, keepdims=True))
    n8_ref[0] = jnp.broadcast_to(n8, (n_exp, LANES)).astype(jnp.int32)
    lbase = _dot(ltri_ref[...], jnp.broadcast_to(n8, (n_exp, LANES)).astype(BF16))[:, 0:1]
    before = _dot(oh.astype(BF16), tri_ref[...])
    where_to = ROW_GROUP * lbase + before
    pos = [jnp.sum(jnp.where(row == te[k:k + 1, :], where_to, 0.0), axis=0, keepdims=True).astype(jnp.int32)
           for k in range(TOP_K)]
    pos_ref[0] = jnp.concatenate(pos + [pad_i], axis=0)
    for c in range(xs_ref.shape[0] // tm):
        r = lax.broadcasted_iota(jnp.int32, (tm, tm), 0) + c * tm
        hit = r == pos[0]
        for k in range(1, TOP_K):
            hit = hit | (r == pos[k])
        xs_ref[c * tm:(c + 1) * tm, :] = _pack_bf16_pairs(_dot(jnp.where(hit, 1.0, 0.0).astype(BF16), hb))
    clear.wait()


def _oproj(parts, w_out, x, ctx, mlat, mctx, lng, lnb, rwt, rb, nct_out, x_tile_off, alpha):
    b, t, _ = parts[0].shape
    d = x.shape[-1]
    nt = t // TM
    n_exp = rwt.shape[0]
    widths = tuple(p.shape[-1] for p in parts)
    tile = lambda bi, i: (bi, i, 0)
    full = lambda bi, i: (0, 0)
    stream, stream_specs = _stream_specs(x, ctx, nct_out, d, x_tile_off)
    in_specs = [pl.BlockSpec((1, TM, wd), tile) for wd in widths] + [
        pl.BlockSpec(w_out.shape, full)] + stream_specs + [
        pl.BlockSpec((1, 1, N_MOD * d), lambda bi, i: (bi, 0, 0)),
        pl.BlockSpec((1, N_MOD * d), full),
        pl.BlockSpec((1, d), full),
        pl.BlockSpec((1, d), full),
        pl.BlockSpec((n_exp, d), full),
        pl.BlockSpec((n_exp, 1), full),
        pl.BlockSpec((TM, TM), full),
        pl.BlockSpec((n_exp, n_exp), full)]
    sr = _sort_rows(n_exp)
    tri = jnp.asarray(np.triu(np.ones((TM, TM)), 1), BF16)
    ltri = jnp.asarray(np.tril(np.ones((n_exp, n_exp)), -1), BF16)
    small = pl.BlockSpec((1, 8, TM), lambda bi, i: (bi, 0, i))
    return pl.pallas_call(
        functools.partial(_oproj_kernel, len(parts), widths, nct_out, alpha, n_exp),
        grid=(b, nt),
        in_specs=in_specs,
        out_specs=[pl.BlockSpec((1, TM, d), tile),
                   small,
                   pl.BlockSpec((1, n_exp, LANES), lambda bi, i: (bi * nt + i, 0, 0)),
                   small,
                   pl.BlockSpec((sr, d // 2), lambda bi, i: (bi * nt + i, 0)),
                   pl.BlockSpec(memory_space=pl.ANY)],
        out_shape=[jax.ShapeDtypeStruct((b, t, d), F32),
                   jax.ShapeDtypeStruct((b, 8, t), F32),
                   jax.ShapeDtypeStruct((b * nt, n_exp, LANES), jnp.int32),
                   jax.ShapeDtypeStruct((b, 8, t), jnp.int32),
                   jax.ShapeDtypeStruct((b * nt * sr, d // 2), U32),
                   jax.ShapeDtypeStruct(((b * nt + 1) * sr, d // 2), U32)],
        scratch_shapes=[pltpu.VMEM((sr - TOP_K * TM, d // 2), U32), pltpu.SemaphoreType.DMA],
        compiler_params=_cparams(("parallel", "parallel")),
        name="out_proj_norm_router",
    )(*parts, w_out, *stream, mlat, mctx, lng, lnb, rwt, rb, tri, ltri)


def _moe_kernel(be_ref, nx_ref, bv_ref, nu_ref, rprev_ref, rcur_ref, rnext_ref, xs_hbm, wgu_hbm, bgu_ref,
                wd_hbm, bd_ref, ys_in_hbm, ys_hbm, wgu_f, wd_f, wgu_s, wd_s, xbuf, ybuf, sem, gsem, ssem):
    del ys_in_hbm
    blk = pl.program_id(0)
    ff = wd_s.shape[0]
    gpb = xbuf.shape[1] // ROW_GROUP
    par = blk % 2

    def gathers(rec, buf, wait):
        if wait:
            pltpu.make_async_copy(xs_hbm.at[pl.ds(0, xbuf.shape[1])], xbuf.at[buf], gsem.at[buf]).wait()
            return
        for g in range(gpb):
            _group_copy(xs_hbm, rec[g], xbuf.at[buf], g, gsem.at[buf]).start(priority=g % 2)

    def scatters(rec, buf, wait):
        if wait:
            pltpu.make_async_copy(ybuf.at[buf], ys_hbm.at[pl.ds(0, ybuf.shape[1])], ssem.at[buf]).wait()
            return
        for g in range(gpb):
            _group_copy(ybuf.at[buf], g, ys_hbm, rec[gpb + g], ssem.at[buf]).start(priority=g % 2)

    def fetch(e):
        return (pltpu.make_async_copy(wgu_hbm.at[e], wgu_f, sem.at[0]),
                pltpu.make_async_copy(wd_hbm.at[e], wd_f, sem.at[1]))

    @pl.when(blk < nu_ref[0])
    def _():
        e = be_ref[blk]
        new_expert = (blk == 0) | (e != be_ref[jnp.maximum(blk - 1, 0)])

        @pl.when(blk == 0)
        def _():
            for cp in fetch(e):
                cp.start()

        @pl.when(new_expert)
        def _():
            for cp in fetch(e):
                cp.wait()
            rows = 128

            def cast_gu(r, c):
                r0 = pl.multiple_of(r * rows, rows)
                wgu_s[pl.ds(r0, rows), :] = wgu_f[pl.ds(r0, rows), :].astype(BF16)
                return c

            def cast_d(r, c):
                r0 = pl.multiple_of(r * rows, rows)
                wd_s[pl.ds(r0, rows), :] = wd_f[pl.ds(r0, rows), :].astype(BF16)
                return c

            lax.fori_loop(0, wgu_s.shape[0] // rows, cast_gu, 0)
            lax.fori_loop(0, wd_s.shape[0] // rows, cast_d, 0)

            @pl.when(nx_ref[blk] != e)
            def _():
                for cp in fetch(nx_ref[blk]):
                    cp.start()

        @pl.when(blk == 0)
        def _():
            gathers(rcur_ref, 0, False)

        gathers(rcur_ref, par, True)
        half = xbuf.shape[1] // 2

        def ffn(rows):
            gathers(rnext_ref, 1 - par, False)
            x_lo, x_hi = _unpack_bf16_pairs(xbuf[par, rows, :])
            dh = x_lo.shape[1]
            gu = _dot(x_lo, wgu_s[0:dh, :]) + _dot(x_hi, wgu_s[dh:, :]) + bgu_ref[0]
            g = jnp.minimum(gu[:, :ff], SWIGLU_LIMIT)
            u = jnp.clip(gu[:, ff:], -SWIGLU_LIMIT, SWIGLU_LIMIT)
            act = (u + 1.0) * (g / (1.0 + jnp.exp(-SWIGLU_ALPHA * g)))
            y = _dot(act.astype(BF16), wd_s[...]) + bd_ref[0]
            ybuf[par, rows, :] = _pack_bf16_pairs(y.astype(BF16).astype(F32))
            if rows != slice(None):
                ybuf[par, half:, :] = jnp.zeros((half, ybuf.shape[2]), ybuf.dtype)
            scatters(rcur_ref, par, False)

        @pl.when(bv_ref[blk] > half)
        def _():
            ffn(slice(None))

        @pl.when(bv_ref[blk] <= half)
        def _():
            ffn(slice(0, half))

        @pl.when(blk > 0)
        def _():
            scatters(rprev_ref, 1 - par, True)

        @pl.when(blk == nu_ref[0] - 1)
        def _():
            gathers(rnext_ref, 1 - par, True)
            scatters(rcur_ref, par, True)


def _moe_experts(place, layer, xs, ys, w_gu, b_gu, w_down, b_down):
    d = 2 * xs.shape[1]
    ff2 = w_gu.shape[-1]
    ff = w_down.shape[-2]
    off = layer * w_gu.shape[1]
    n_exp = w_gu.shape[0] * w_gu.shape[1]
    w_gu = w_gu.reshape(n_exp, d, ff2)
    w_down = w_down.reshape(n_exp, ff, d)
    nb = place["n_blocks"]
    last = lambda nu: nu[0] - 1
    bblk = lambda i, be, nx, bv, nu: (be[i], 0, 0)
    rec = lambda f: pl.BlockSpec((SMEM_BLOCK,), f, memory_space=pltpu.SMEM)
    grid_spec = pltpu.PrefetchScalarGridSpec(
        num_scalar_prefetch=4,
        grid=(nb,),
        in_specs=[rec(lambda i, be, nx, bv, nu: (jnp.clip(i - 1, 0, last(nu)),)),
                  rec(lambda i, be, nx, bv, nu: (jnp.minimum(i, last(nu)),)),
                  rec(lambda i, be, nx, bv, nu: (jnp.minimum(i + 1, last(nu)),)),
                  pl.BlockSpec(memory_space=pl.ANY),
                  pl.BlockSpec(memory_space=pl.ANY),
                  pl.BlockSpec((1, 1, ff2), bblk),
                  pl.BlockSpec(memory_space=pl.ANY),
                  pl.BlockSpec((1, 1, d), bblk),
                  pl.BlockSpec(memory_space=pl.ANY)],
        out_specs=pl.BlockSpec(memory_space=pl.ANY),
        scratch_shapes=[pltpu.VMEM((d, ff2), F32), pltpu.VMEM((ff, d), F32),
                        pltpu.VMEM((d, ff2), BF16), pltpu.VMEM((ff, d), BF16),
                        pltpu.VMEM((2, MOE_BM, d // 2), U32), pltpu.VMEM((2, MOE_BM, d // 2), U32),
                        pltpu.SemaphoreType.DMA((2,)), pltpu.SemaphoreType.DMA((2,)),
                        pltpu.SemaphoreType.DMA((2,))])
    records = place["rec"]
    return pl.pallas_call(
        _moe_kernel,
        grid_spec=grid_spec,
        out_shape=jax.ShapeDtypeStruct(ys.shape, ys.dtype),
        input_output_aliases={12: 0},
        compiler_params=_cparams(("arbitrary",)),
        name="moe_experts",
    )(place["block_e"] + off, place["block_next_e"] + off, place["block_rows"], place["n_used"],
      records, records, records, xs, w_gu, b_gu.reshape(n_exp, 1, ff2), w_down, b_down.reshape(n_exp, 1, d), ys)


def _expert_onehot(te, n_exp):
    row = lax.broadcasted_iota(jnp.int32, (n_exp, te.shape[1]), 0)
    oh = jnp.zeros(row.shape, F32)
    for k in range(TOP_K):
        oh = oh + (row == te[k:k + 1, :]).astype(F32)
    return row, oh


def _groups(cnt):
    return jnp.floor((cnt + (ROW_GROUP - 1)) / ROW_GROUP)


def _placement(n8, n_asg):
    n_tiles, n_exp = n8.shape
    gpt = _sort_rows(n_exp) // ROW_GROUP
    gpb = MOE_BM // ROW_GROUP
    assert (TOP_K * TM + n_exp * (ROW_GROUP - 1)) // ROW_GROUP < gpt
    ids = jnp.arange(n_exp, dtype=jnp.int32)
    lbase = jnp.cumsum(n8, axis=1) - n8
    cum_incl = jnp.cumsum(n8, axis=0)
    cum_excl = cum_incl - n8
    groups = cum_incl[-1]
    padded = (groups + gpb - 1) // gpb * gpb
    pend = jnp.cumsum(padded)
    pstart = pend - padded
    n_blocks = -(-(n_asg + n_tiles * n_exp * (ROW_GROUP - 1)) // MOE_BM) + n_exp
    n_used = (pend[-1] // gpb).astype(jnp.int32)
    blocks = jnp.arange(n_blocks, dtype=jnp.int32)
    blk = jnp.minimum(blocks, n_used - 1) * gpb
    block_e = jnp.minimum(jnp.sum((pend[None, :] <= blk[:, None]).astype(jnp.int32), axis=1), n_exp - 1)
    later = (padded[None, :] > 0) & (ids[None, :] > block_e[:, None])
    block_next_e = jnp.min(jnp.where(later, ids[None, :], n_exp), axis=1)
    block_next_e = jnp.where(block_next_e == n_exp, block_e, block_next_e).astype(jnp.int32)
    mine = (block_e[:, None] == ids[None, :]).astype(jnp.int32)
    pick = lambda table: jnp.sum(mine[:, :, None] * table.T[None, :, :], axis=1)
    b_start = jnp.sum(mine * pstart[None, :], axis=1)
    b_groups = jnp.sum(mine * groups[None, :], axis=1)
    j_e = (blocks * gpb - b_start)[:, None] + jnp.arange(gpb, dtype=jnp.int32)[None, :]
    valid = (j_e < b_groups[:, None]) & (blocks < n_used)[:, None]
    tile = jnp.sum((pick(cum_incl)[:, None, :] <= j_e[:, :, None]).astype(jnp.int32), axis=2)
    tile = jnp.minimum(tile, n_tiles - 1)
    in_tile = (tile[:, :, None] == jnp.arange(n_tiles, dtype=jnp.int32)[None, None, :]).astype(jnp.int32)
    shift = jnp.sum(in_tile * (pick(lbase) - pick(cum_excl))[:, None, :], axis=2)
    held = tile * gpt + shift + j_e
    pad_dst = n_tiles * gpt + jnp.arange(gpb, dtype=jnp.int32)[None, :]
    src = jnp.where(valid, held, gpt - 1)
    dst = jnp.where(valid, held, pad_dst)
    rec = jnp.concatenate([src, dst, jnp.zeros((n_blocks, SMEM_BLOCK - 2 * gpb), jnp.int32)], axis=1)
    left = b_groups - (blocks - b_start // gpb) * gpb
    block_rows = (jnp.clip(left, 0, gpb) * ROW_GROUP).astype(jnp.int32)
    return dict(rec=rec.reshape(-1), block_e=block_e, block_next_e=block_next_e, block_rows=block_rows,
                n_used=n_used.reshape(1), n_blocks=n_blocks)


def _group_copy(src_ref, src_group, dst_ref, dst_group, sem):
    s0 = pl.multiple_of(src_group * ROW_GROUP, ROW_GROUP)
    d0 = pl.multiple_of(dst_group * ROW_GROUP, ROW_GROUP)
    return pltpu.make_async_copy(src_ref.at[pl.ds(s0, ROW_GROUP)], dst_ref.at[pl.ds(d0, ROW_GROUP)], sem)


def _tile_step():
    return pl.program_id(0) * pl.num_programs(1) + pl.program_id(1)


def _sort_rows(n_exp):
    return -(-(TOP_K * TM + n_exp * ROW_GROUP) // TM) * TM


def _combine_kernel(nct, alpha, ys_ref, pos_ref, tg_ref, x1_ref, mlat_ref, mctx_ref, lng_ref, lnb_ref, o_ref):
    d = x1_ref.shape[-1]
    tm = x1_ref.shape[1]
    is_ctx = pl.program_id(1) < nct
    m = jnp.where(is_ctx, mctx_ref[...], mlat_ref[0])
    pos = pos_ref[0]
    gates = tg_ref[0]
    f_lo = f_hi = None
    for c in range(ys_ref.shape[0] // tm):
        col = lax.broadcasted_iota(jnp.int32, (tm, tm), 1) + c * tm
        w = jnp.where(col == pos[:, 0:1], gates[:, 0:1], 0.0)
        for k in range(1, TOP_K):
            w = w + jnp.where(col == pos[:, k:k + 1], gates[:, k:k + 1], 0.0)
        w = w.astype(BF16)
        y_lo, y_hi = _unpack_bf16_pairs(ys_ref[c * tm:(c + 1) * tm, :])
        f_lo = _dot(w, y_lo) if f_lo is None else f_lo + _dot(w, y_lo)
        f_hi = _dot(w, y_hi) if f_hi is None else f_hi + _dot(w, y_hi)
    f = jnp.concatenate([f_lo, f_hi], axis=1)
    o_ref[0] = _layernorm(alpha * x1_ref[0] + m[:, 5 * d:6 * d] * f, lng_ref[...], lnb_ref[...])


def _combine_postnorm(ys, pos_t, tg_t, x1, mlat, mctx, lng, lnb, nct, alpha, n_exp):
    b, t, d = x1.shape
    nt = t // TM
    tile = lambda bi, i: (bi, i, 0)
    full = lambda bi, i: (0, 0)
    return pl.pallas_call(
        functools.partial(_combine_kernel, nct, alpha),
        grid=(b, nt),
        in_specs=[pl.BlockSpec((_sort_rows(n_exp), d // 2), lambda bi, i: (bi * nt + i, 0)),
                  pl.BlockSpec((1, TM, 8), tile),
                  pl.BlockSpec((1, TM, 8), tile),
                  pl.BlockSpec((1, TM, d), tile),
                  pl.BlockSpec((1, 1, N_MOD * d), lambda bi, i: (bi, 0, 0)),
                  pl.BlockSpec((1, N_MOD * d), full),
                  pl.BlockSpec((1, d), full),
                  pl.BlockSpec((1, d), full)],
        out_specs=pl.BlockSpec((1, TM, d), tile),
        out_shape=jax.ShapeDtypeStruct((b, t, d), F32),
        compiler_params=_cparams(("parallel", "parallel")),
        name="combine_post_norm",
    )(ys, pos_t, tg_t, x1, mlat, mctx, lng, lnb)


def _rope_tables(ctx_len, seq):
    t = np.arange(seq)
    row = (t // GRID_W).astype(np.float64)
    col = (t % GRID_W).astype(np.float64)
    nf = HEAD_DIM // 4
    inv = ROPE_THETA ** (-np.arange(nf, dtype=np.float64) / nf)
    ar = row[:, None] * inv[None, :]
    ac = col[:, None] * inv[None, :]
    ang = np.concatenate([ar, ar, ac, ac], axis=-1)
    cos = np.concatenate([np.ones((ctx_len, HEAD_DIM)), np.cos(ang)], axis=0)
    sin = np.concatenate([np.zeros((ctx_len, HEAD_DIM)), np.sin(ang)], axis=0)
    sign = np.where((np.arange(HEAD_DIM) % 32) < 16, -1.0, 1.0)[None, :]
    cos2 = np.tile(cos, (1, LANES // HEAD_DIM))
    sin2 = np.tile(sin * sign, (1, LANES // HEAD_DIM))
    return jnp.asarray(cos2, F32), jnp.asarray(sin2, F32)


def _channel_dft():
    c = np.arange(HEAD_DIM)
    ang = 2.0 * np.pi * ((c[:, None] * c[None, :]) % HEAD_DIM) / HEAD_DIM
    eye = np.eye(4)
    cs = np.concatenate([np.kron(eye, np.cos(ang)), np.kron(eye, np.sin(ang))], axis=1)
    return jnp.asarray(cs, BF16)


def _group_mean_matrix():
    bd = np.kron(np.eye(LANES // HEAD_DIM), np.full((HEAD_DIM, HEAD_DIM), 1.0 / HEAD_DIM))
    return jnp.asarray(bd, BF16)


def _dup_heads(wk, n_heads):
    d = wk.shape[0]
    return jnp.broadcast_to(wk.reshape(d, n_heads, 1, HEAD_DIM), (d, n_heads, 2, HEAD_DIM)).reshape(d, n_heads * LANES)


def _lambda_init(layer):
    return 0.8 - 0.6 * math.exp(-0.3 * layer)


def _moe_postnorm(routed, mlat, mctx, lng, lnb, nct, alpha, layer, w_gu, b_gu, w_down, b_down):
    x1, tg, n8, pos, xs, ys = routed
    n_exp = w_gu.shape[1]
    b, t, _ = x1.shape
    place = _placement(n8[:, :, 0], b * t * TOP_K)
    ys = _moe_experts(place, layer, xs, ys, w_gu, b_gu, w_down, b_down)
    pos_t, tg_t = jnp.transpose(pos, (0, 2, 1)), jnp.transpose(tg, (0, 2, 1))
    return _combine_postnorm(ys, pos_t, tg_t, x1, mlat, mctx, lng, lnb, nct, alpha, n_exp)


def kernel(x, c, ctx, c_ctx, mod_w, mod_b, ln_g, ln_b, ab_w_in, ab_sink, ab_w_out,
           cd_w_in, cd_lambda, cd_subln_g, cd_q_norm_g, cd_k_norm_g, cd_w_out,
           router_w, router_b, expert_w_gu, expert_b_gu, expert_w_down, expert_b_down):
    b, s, d = x.shape
    n_ctx = ctx.shape[1]
    depth = mod_w.shape[0]
    n_exp = router_w.shape[-1]
    assert d == 16 * HEAD_DIM and n_ctx % TM == 0 and s % TM == 0 and s % GRID_W == 0
    nct = n_ctx // TM
    alpha = (2 * depth) ** 0.25

    cos, sin = _rope_tables(n_ctx, s)
    cs_dft = _channel_dft()
    bd = _group_mean_matrix()
    mods = _mod_vectors(c, c_ctx, mod_w, mod_b)

    xs, xc = x, ctx
    for l in range(depth):
        last = l == depth - 1
        i = l // 2
        mlat = mods[l, :b].reshape(b, 1, N_MOD * d)
        mctx = mods[l, b:b + 1]
        lng1, lnb1 = ln_g[l, 0].reshape(1, d), ln_b[l, 0].reshape(1, d)
        lng2, lnb2 = ln_g[l, 1].reshape(1, d), ln_b[l, 1].reshape(1, d)
        rwt = router_w[l].T.astype(BF16)
        rb = router_b[l].reshape(n_exp, 1)
        if l % 2 == 0:
            w = ab_w_in[i]
            w_n = jnp.concatenate([w[:, :1024], _dup_heads(w[:, 1024:1280], 4)], axis=1).astype(BF16)
            wvt = w[:, 1280:1536].T.astype(BF16)
            uw, q, k, vt = _proj_ab(xs, xc, mlat, mctx, w_n, wvt, cos, sin, cs_dft, nct)
            oa = jnp.concatenate([_fourier(uw[:, :n_ctx]), _fourier(uw[:, n_ctx:])], axis=1)
            ob = _win_attn(ab_sink[i], q, k, vt, nct)
            if last:
                parts = [oa[:, n_ctx:], ob[:, n_ctx:]]
            else:
                parts = [oa, ob]
            w_out = ab_w_out[i].astype(BF16)
        else:
            w = cd_w_in[i]
            w_n = jnp.concatenate([w[:, :1536], _dup_heads(w[:, 1536:1664], 2)], axis=1).astype(BF16)
            wvt = w[:, 1664:2304].T.astype(BF16)
            gq = jnp.tile(cd_q_norm_g[i], 2).reshape(1, LANES)
            gk = jnp.tile(cd_k_norm_g[i], 2).reshape(1, LANES)
            qc, qd, kc, kd, vt = _proj_cd(xs, xc, mlat, mctx, w_n, wvt, cos, sin, bd, gq, gk, nct)
            sg = cd_subln_g[i].reshape(LANES, 1)
            o_lat = _attn_cd(cd_lambda[i], sg, qc, qd, kc, kd, vt, nct, _lambda_init(l))
            if last:
                parts = [o_lat]
            else:
                raise NotImplementedError("context outputs of a differential/axial layer")
            w_out = cd_w_out[i].astype(BF16)
        experts = (l, expert_w_gu, expert_b_gu, expert_w_down, expert_b_down)
        if last:
            routed = _oproj(parts, w_out, xs, xc, mlat, mctx, lng1, lnb1, rwt, rb, 0, nct, alpha)
            return _moe_postnorm(routed, mlat, mctx, lng2, lnb2, 0, alpha, *experts)
        routed = _oproj(parts, w_out, xs, xc, mlat, mctx, lng1, lnb1, rwt, rb, nct, 0, alpha)
        xs, xc = _moe_postnorm(routed, mlat, mctx, lng2, lnb2, nct, alpha, *experts), None
    return xs[:, n_ctx:]
```

```python
import functools
import math

import jax
import jax.numpy as jnp
import numpy as np
from jax import lax
from jax.experimental import pallas as pl
from jax.experimental.pallas import tpu as pltpu

F32 = jnp.float32
BF16 = jnp.bfloat16

HEAD_DIM = 64
GRID_W = 64
WINDOW = 128
ROPE_THETA = 10000.0
LN_EPS = 1e-6
RMS_EPS = 1e-6
N_MOD = 6
TOP_K = 4
SWIGLU_LIMIT = 7.0
SWIGLU_ALPHA = 1.702
NEG = -1e30
LOG2E = 1.4426950408889634
QSCALE = HEAD_DIM ** -0.5 * LOG2E

N_C_HEADS = 4
N_D_HEADS = 8
N_D_KV = 2
N_UNITS = 2 * N_C_HEADS + N_D_HEADS
ONES_ROWS = 16
_VT_LAYOUT = tuple((h * (2 * HEAD_DIM + ONES_ROWS), 2 * HEAD_DIM) for h in range(N_C_HEADS)) + tuple(
    (N_C_HEADS * (2 * HEAD_DIM + ONES_ROWS) + g * (HEAD_DIM + ONES_ROWS), HEAD_DIM) for g in range(N_D_KV))
VT_ROWS = _VT_LAYOUT[-1][0] + HEAD_DIM + ONES_ROWS

LANES = 128
TM = 256
MOE_BM = 512
ROW_GROUP = 8
SMEM_BLOCK = 1024
VMEM_LIMIT = 56 * 1024 * 1024


def _dot(a, b):
    return jnp.dot(a, b, preferred_element_type=F32)


def _dot_nt(a, b):
    return lax.dot_general(a, b, (((1,), (1,)), ((), ())), preferred_element_type=F32)


def _cparams(sem):
    return pltpu.CompilerParams(dimension_semantics=sem, vmem_limit_bytes=VMEM_LIMIT)


def _layernorm(z, g, b):
    mu = jnp.mean(z, axis=-1, keepdims=True)
    d = z - mu
    var = jnp.mean(d * d, axis=-1, keepdims=True)
    return d * lax.rsqrt(var + LN_EPS) * g + b


def _rope128(x, cos, sin_signed):
    lane = lax.broadcasted_iota(jnp.int32, x.shape, 1)
    lo = (lane % 32) < 16
    rot = jnp.where(lo, pltpu.roll(x, LANES - 16, 1), pltpu.roll(x, 16, 1))
    return x * cos + rot * sin_signed


def _rmsnorm128(x, g, bd):
    x2 = x * x
    hi = x2.astype(BF16)
    lo = (x2 - hi.astype(F32)).astype(BF16)
    ms = _dot(hi, bd) + _dot(lo, bd)
    return x * lax.rsqrt(ms + RMS_EPS) * g


U32 = jnp.uint32
_HI16 = 0xFFFF0000


def _pack_bf16_pairs(x):
    n = x.shape[1] // 2
    bits = lax.bitcast_convert_type(x, U32)
    return (bits[:, :n] >> 16) | (bits[:, n:] & U32(_HI16))


def _unpack_bf16_pairs(p):
    lo = lax.bitcast_convert_type(p << 16, F32).astype(BF16)
    hi = lax.bitcast_convert_type(p & U32(_HI16), F32).astype(BF16)
    return lo, hi


def _modulated(x_ref, ctx_ref, mlat_ref, mctx_ref, is_ctx, d):
    m = jnp.where(is_ctx, mctx_ref[...], mlat_ref[0])
    x = jnp.where(is_ctx, ctx_ref[0], x_ref[0])
    return (x * (1.0 + m[:, d:2 * d]) + m[:, 0:d]).astype(BF16)


def _stream_specs(x, ctx, nct, d, off=0):
    if ctx is None:
        return (x, x), [pl.BlockSpec((1, TM, d), lambda bi, i: (bi, i + off, 0)),
                        pl.BlockSpec((1, TM, d), lambda bi, i: (bi, jnp.minimum(i, max(nct - 1, 0)), 0))]
    return (x, ctx), [pl.BlockSpec((1, TM, d), lambda bi, i: (bi, jnp.maximum(i - nct, 0), 0)),
                      pl.BlockSpec((1, TM, d), lambda bi, i: (bi, jnp.minimum(i, nct - 1), 0))]


def _mod_kernel(c_ref, w_ref, b_ref, o_ref):
    c = c_ref[...]
    s = (c / (1.0 + jnp.exp(-c))).astype(BF16)
    o_ref[0] = _dot(s, w_ref[0].astype(BF16)) + b_ref[0]


def _mod_vectors(c, c_ctx, mod_w, mod_b):
    depth, d, n = mod_w.shape
    b = c.shape[0]
    rows = 8 * (-(-(b + 1) // 8))
    cs = jnp.zeros((rows, d), F32).at[:b].set(c).at[b].set(c_ctx)
    tn = 1536
    out = pl.pallas_call(
        _mod_kernel,
        grid=(depth, n // tn),
        in_specs=[pl.BlockSpec((rows, d), lambda l, j: (0, 0)),
                  pl.BlockSpec((1, d, tn), lambda l, j: (l, 0, j)),
                  pl.BlockSpec((1, 1, tn), lambda l, j: (l, 0, j))],
        out_specs=pl.BlockSpec((1, rows, tn), lambda l, j: (l, 0, j)),
        out_shape=jax.ShapeDtypeStruct((depth, rows, n), F32),
        compiler_params=_cparams(("parallel", "parallel")),
        name="mod_vectors",
    )(cs, mod_w, mod_b.reshape(depth, 1, n))
    return out


def _proj_ab_kernel(nct, x_ref, ctx_ref, mlat_ref, mctx_ref, w_ref, wvt_ref, cos_ref, sin_ref, cs_ref,
                    uw_ref, q_ref, k_ref, vt_ref):
    is_ctx = pl.program_id(1) < nct
    d = x_ref.shape[-1]
    h = _modulated(x_ref, ctx_ref, mlat_ref, mctx_ref, is_ctx, d)
    p = _dot(h, w_ref[...])
    uw_ref[0] = _dot(p[:, 0:256].astype(BF16), cs_ref[...]).astype(BF16)
    cos = cos_ref[...]
    sin = sin_ref[...]
    for j in range(6):
        c0 = 256 + j * LANES
        q_ref[0, :, j * LANES:(j + 1) * LANES] = (_rope128(p[:, c0:c0 + LANES], cos, sin) * QSCALE).astype(BF16)
    for j in range(4):
        c0 = 1024 + j * LANES
        k_ref[0, :, j * LANES:(j + 1) * LANES] = _rope128(p[:, c0:c0 + LANES], cos, sin).astype(BF16)
    vt = _dot_nt(wvt_ref[...], h).astype(BF16)
    for c in range(TM // LANES):
        vt_ref[0, c] = vt[:, c * LANES:(c + 1) * LANES]


def _proj_ab(x, ctx, mlat, mctx, w, wvt, cos, sin, cs, nct):
    b, d = x.shape[0], x.shape[2]
    tall = x.shape[1] + (0 if ctx is None else ctx.shape[1])
    stream, stream_specs = _stream_specs(x, ctx, nct, d)
    nt = tall // TM
    nw = w.shape[1]
    nchunk = tall // LANES
    cpt = TM // LANES
    return pl.pallas_call(
        functools.partial(_proj_ab_kernel, nct),
        grid=(b, nt),
        in_specs=stream_specs + [
                  pl.BlockSpec((1, 1, N_MOD * d), lambda bi, i: (bi, 0, 0)),
                  pl.BlockSpec((1, N_MOD * d), lambda bi, i: (0, 0)),
                  pl.BlockSpec((d, nw), lambda bi, i: (0, 0)),
                  pl.BlockSpec((256, d), lambda bi, i: (0, 0)),
                  pl.BlockSpec((TM, LANES), lambda bi, i: (i, 0)),
                  pl.BlockSpec((TM, LANES), lambda bi, i: (i, 0)),
                  pl.BlockSpec((256, 512), lambda bi, i: (0, 0))],
        out_specs=[pl.BlockSpec((1, TM, 512), lambda bi, i: (bi, i, 0)),
                   pl.BlockSpec((1, TM, 768), lambda bi, i: (bi, i, 0)),
                   pl.BlockSpec((1, TM, 512), lambda bi, i: (bi, i, 0)),
                   pl.BlockSpec((1, cpt, 256, LANES), lambda bi, i: (bi, i, 0, 0))],
        out_shape=[jax.ShapeDtypeStruct((b, tall, 512), BF16),
                   jax.ShapeDtypeStruct((b, tall, 768), BF16),
                   jax.ShapeDtypeStruct((b, tall, 512), BF16),
                   jax.ShapeDtypeStruct((b, nchunk, 256, LANES), BF16)],
        compiler_params=_cparams(("parallel", "parallel")),
        name="proj_ab",
    )(*stream, mlat, mctx, w, wvt, cos, sin, cs)


def _proj_cd_kernel(nct, x_ref, ctx_ref, mlat_ref, mctx_ref, w_ref, wvt_ref, cos_ref, sin_ref, bd_ref,
                    gq_ref, gk_ref, qc_ref, qd_ref, kc_ref, kd_ref, vt_ref):
    is_ctx = pl.program_id(1) < nct
    d = x_ref.shape[-1]
    h = _modulated(x_ref, ctx_ref, mlat_ref, mctx_ref, is_ctx, d)
    p = _dot(h, w_ref[...])
    cos = cos_ref[...]
    sin = sin_ref[...]
    bd = bd_ref[...]
    gq = gq_ref[...]
    gk = gk_ref[...]
    for j in range(4):
        sl = slice(j * LANES, (j + 1) * LANES)
        qc_ref[0, :, sl] = (_rope128(p[:, j * LANES:(j + 1) * LANES], cos, sin) * QSCALE).astype(BF16)
        c0 = 512 + j * LANES
        qd_ref[0, :, sl] = (_rope128(_rmsnorm128(p[:, c0:c0 + LANES], gq, bd), cos, sin) * QSCALE).astype(BF16)
        c0 = 1024 + j * LANES
        kc_ref[0, :, sl] = _rope128(p[:, c0:c0 + LANES], cos, sin).astype(BF16)
    for j in range(2):
        c0 = 1536 + j * LANES
        kd_ref[0, :, j * LANES:(j + 1) * LANES] = _rope128(
            _rmsnorm128(p[:, c0:c0 + LANES], gk, bd), cos, sin).astype(BF16)
    vt = _dot_nt(wvt_ref[...], h).astype(BF16)
    ones = jnp.ones((ONES_ROWS, vt.shape[1]), BF16)
    src = 0
    for dst, dv in _VT_LAYOUT:
        vt_ref[0, 0, dst:dst + dv, :] = vt[src:src + dv, :]
        vt_ref[0, 0, dst + dv:dst + dv + ONES_ROWS, :] = ones
        src += dv


def _proj_cd(x, ctx, mlat, mctx, w, wvt, cos, sin, bd, gq, gk, nct):
    b, d = x.shape[0], x.shape[2]
    tall = x.shape[1] + (0 if ctx is None else ctx.shape[1])
    stream, stream_specs = _stream_specs(x, ctx, nct, d)
    nt = tall // TM
    nw = w.shape[1]
    nv = VT_ROWS
    full = lambda bi, i: (0, 0)
    tile = lambda bi, i: (bi, i, 0)
    return pl.pallas_call(
        functools.partial(_proj_cd_kernel, nct),
        grid=(b, nt),
        in_specs=stream_specs + [
                  pl.BlockSpec((1, 1, N_MOD * d), lambda bi, i: (bi, 0, 0)),
                  pl.BlockSpec((1, N_MOD * d), full),
                  pl.BlockSpec((d, nw), full),
                  pl.BlockSpec(wvt.shape, full),
                  pl.BlockSpec((TM, LANES), lambda bi, i: (i, 0)),
                  pl.BlockSpec((TM, LANES), lambda bi, i: (i, 0)),
                  pl.BlockSpec((LANES, LANES), full),
                  pl.BlockSpec((1, LANES), full),
                  pl.BlockSpec((1, LANES), full)],
        out_specs=[pl.BlockSpec((1, TM, 512), tile),
                   pl.BlockSpec((1, TM, 512), tile),
                   pl.BlockSpec((1, TM, 512), tile),
                   pl.BlockSpec((1, TM, 256), tile),
                   pl.BlockSpec((1, 1, nv, TM), lambda bi, i: (bi, i, 0, 0))],
        out_shape=[jax.ShapeDtypeStruct((b, tall, 512), BF16),
                   jax.ShapeDtypeStruct((b, tall, 512), BF16),
                   jax.ShapeDtypeStruct((b, tall, 512), BF16),
                   jax.ShapeDtypeStruct((b, tall, 256), BF16),
                   jax.ShapeDtypeStruct((b, nt, nv, TM), BF16)],
        compiler_params=_cparams(("parallel", "parallel")),
        name="proj_cd",
    )(*stream, mlat, mctx, w, wvt, cos, sin, bd, gq, gk)


def _fourier_kernel(nb, scale, uw_ref, ca_ref, sa_ref, cb_ref, sb_ref, o_ref):
    j = pl.program_id(0)
    ca = ca_ref[pl.ds(j, 1), :]
    sa = sa_ref[pl.ds(j, 1), :]
    cb = cb_ref[...]
    sb = sb_ref[...]
    ct = (ca * cb - sa * sb).astype(BF16)
    nst = (-(sa * cb + ca * sb)).astype(BF16)
    for bi in range(nb):
        acc = _dot(ct, uw_ref[bi, :, 0:256]) + _dot(nst, uw_ref[bi, :, 256:512])
        o_ref[bi] = (acc * scale).astype(BF16)


def _dft_tables(t, tmf):
    k = np.arange(t, dtype=np.int64)
    j1 = np.arange(t // tmf, dtype=np.int64) * tmf
    j0 = np.arange(tmf, dtype=np.int64)
    aa = (2.0 * np.pi / t) * ((j1[:, None] * k[None, :]) % t)
    ab = (2.0 * np.pi / t) * ((j0[:, None] * k[None, :]) % t)
    f = lambda a: jnp.asarray(a, F32)
    return f(np.cos(aa)), f(np.sin(aa)), f(np.cos(ab)), f(np.sin(ab))


def _fourier(uw):
    b, t, _ = uw.shape
    tmf = min(128, t)
    ca, sa, cb, sb = _dft_tables(t, tmf)
    scale = 1.0 / math.sqrt(t * HEAD_DIM)
    full2 = lambda j: (0, 0)
    return pl.pallas_call(
        functools.partial(_fourier_kernel, b, scale),
        grid=(t // tmf,),
        in_specs=[pl.BlockSpec((b, t, 512), lambda j: (0, 0, 0)),
                  pl.BlockSpec((t // tmf, t), full2),
                  pl.BlockSpec((t // tmf, t), full2),
                  pl.BlockSpec((tmf, t), full2),
                  pl.BlockSpec((tmf, t), full2)],
        out_specs=pl.BlockSpec((b, tmf, 256), lambda j: (0, j, 0)),
        out_shape=jax.ShapeDtypeStruct((b, t, 256), BF16),
        compiler_params=_cparams(("parallel",)),
        name="fourier_mix",
    )(uw, ca, sa, cb, sb)


def _win_attn_kernel(nct, nchunk, sink_ref, q_ref, k_ref, vt_ref, o_ref, s_a, s_b):
    i = pl.program_id(1)
    is_ctx = i < nct
    j = i - nct
    ctx_rows = nct * TM
    ctx_chunks = ctx_rows // LANES
    lw = TM + 2 * WINDOW
    lchunks = lw // LANES
    cs = jnp.clip(ctx_chunks + (TM // LANES) * j - WINDOW // LANES, 0, nchunk - lchunks)
    rs = pl.multiple_of(cs * LANES, LANES)
    qpos = j * TM + lax.broadcasted_iota(jnp.int32, (1, TM), 1)
    kpos = cs * LANES - ctx_rows + lax.broadcasted_iota(jnp.int32, (lw, 1), 0)
    allowed = (jnp.abs(qpos - kpos) <= WINDOW) & (kpos >= 0) & jnp.logical_not(is_ctx)
    lane = lax.broadcasted_iota(jnp.int32, (TM, LANES), 1)
    n_heads = q_ref.shape[-1] // HEAD_DIM
    group = n_heads // (k_ref.shape[-1] // LANES)

    def scores(hq, dst_s):
        pair, half = divmod(hq, 2)
        g = hq // group
        qp = q_ref[0, :, pair * LANES:(pair + 1) * LANES]
        qm = jnp.where((lane >= HEAD_DIM) == (half == 1), qp, jnp.zeros_like(qp))
        dst_s[0:ctx_rows, :] = _dot_nt(k_ref[0, 0:ctx_rows, g * LANES:(g + 1) * LANES], qm)
        s_l = _dot_nt(k_ref[0, pl.ds(rs, lw), g * LANES:(g + 1) * LANES], qm)
        dst_s[ctx_rows:ctx_rows + lw, :] = jnp.where(allowed, s_l, NEG)

    def consume(hq, src_s):
        g = hq // group
        s = src_s[...]
        sk = sink_ref[hq] * LOG2E
        m = jnp.maximum(jnp.max(s, axis=0, keepdims=True), sk)
        p = jnp.exp2(s - m)
        l = jnp.sum(p, axis=0, keepdims=True) + jnp.exp2(sk - m)
        p = p.astype(BF16)
        vrows = slice(g * HEAD_DIM, (g + 1) * HEAD_DIM)
        acc = jnp.zeros((HEAD_DIM, TM), F32)
        for c in range(ctx_chunks):
            acc = acc + _dot(vt_ref[0, c, vrows, :], p[c * LANES:(c + 1) * LANES, :])
        for c in range(lchunks):
            r0 = ctx_rows + c * LANES
            acc = acc + _dot(vt_ref[0, cs + c, vrows, :], p[r0:r0 + LANES, :])
        return acc / l

    bufs = (s_a, s_b)
    scores(0, s_a)
    outs = []
    for hq in range(n_heads):
        if hq + 1 < n_heads:
            scores(hq + 1, bufs[(hq + 1) % 2])
        outs.append(consume(hq, bufs[hq % 2]))
        if hq % 2 == 1:
            pair = hq // 2
            o2 = jnp.concatenate(outs, axis=0)
            o_ref[0, :, pair * LANES:(pair + 1) * LANES] = o2.T.astype(BF16)
            outs = []


def _win_attn(sink, q, k, vt, nct):
    b, tall, qw = q.shape
    nt = tall // TM
    nchunk = vt.shape[1]
    return pl.pallas_call(
        functools.partial(_win_attn_kernel, nct, nchunk),
        grid=(b, nt),
        in_specs=[pl.BlockSpec(memory_space=pltpu.SMEM),
                  pl.BlockSpec((1, TM, qw), lambda bi, i: (bi, i, 0)),
                  pl.BlockSpec((1, tall, k.shape[-1]), lambda bi, i: (bi, 0, 0)),
                  pl.BlockSpec((1, nchunk, vt.shape[2], LANES), lambda bi, i: (bi, 0, 0, 0))],
        out_specs=pl.BlockSpec((1, TM, qw), lambda bi, i: (bi, i, 0)),
        out_shape=jax.ShapeDtypeStruct((b, tall, qw), BF16),
        scratch_shapes=[pltpu.VMEM((nct * TM + TM + 2 * WINDOW, TM), F32)] * 2,
        compiler_params=_cparams(("parallel", "parallel")),
        name="window_attention",
    )(sink, q, k, vt)


def _unit_operands(u):
    if u < 2 * N_C_HEADS:
        hc = u // 2
        dst, dv = _VT_LAYOUT[hc]
        return True, hc, dst, dv
    g = (u - 2 * N_C_HEADS) // (N_D_HEADS // N_D_KV)
    dst, dv = _VT_LAYOUT[N_C_HEADS + g]
    return False, g, dst, dv


def _attn_cd_kernel(nkt, lam_init, lam_ref, sg_ref, qc_ref, qd_ref, kc_ref, kd_ref, vt_ref, o_ref,
                    qm_s, m_s, acc_s, s_a, s_b):
    mq = qc_ref.shape[1]
    tk = vt_ref.shape[-1]
    lane = lax.broadcasted_iota(jnp.int32, (mq, LANES), 1)
    upper = lane >= HEAD_DIM
    for u in range(N_UNITS):
        src = qc_ref if u < 2 * N_C_HEADS else qd_ref
        pair = (u if u < 2 * N_C_HEADS else u - 2 * N_C_HEADS) // 2
        qp = src[0, :, pair * LANES:(pair + 1) * LANES]
        qm_s[u] = jnp.where(upper == (u % 2 == 1), qp, jnp.zeros_like(qp))
    m_s[...] = jnp.full(m_s.shape, NEG, F32)
    acc_s[...] = jnp.zeros(acc_s.shape, F32)

    def scores(tile, u, dst_s):
        is_c, slab, _, _ = _unit_operands(u)
        k_ref = kc_ref if is_c else kd_ref
        r0 = pl.multiple_of(tile * tk, tk)
        dst_s[u] = _dot_nt(k_ref[0, pl.ds(r0, tk), slab * LANES:(slab + 1) * LANES], qm_s[u])

    def consume(tile, u, src_s):
        _, _, row0, dv = _unit_operands(u)
        rows = dv + ONES_ROWS
        m_old = m_s[u]
        m_new = jnp.maximum(m_old, jnp.max(src_s[u], axis=0, keepdims=True))
        alpha = jnp.exp2(m_old - m_new)
        p = jnp.exp2(src_s[u] - m_new).astype(BF16)
        pv = _dot(vt_ref[0, tile, row0:row0 + rows, :], p)
        acc_s[u, 0:rows, :] = acc_s[u, 0:rows, :] * alpha + pv
        m_s[u] = m_new

    def stage(tile, src_s, dst_s):
        nxt = jnp.minimum(tile + 1, nkt - 1)
        for u in range(N_UNITS):
            scores(nxt, u, dst_s)
            consume(tile, u, src_s)

    for u in range(N_UNITS):
        scores(0, u, s_a)
    stage(0, s_a, s_b)

    def body(pr, carry):
        stage(2 * pr + 1, s_b, s_a)
        stage(2 * pr + 2, s_a, s_b)
        return carry

    lax.fori_loop(0, (nkt - 1) // 2, body, 0)
    if (nkt - 1) % 2 == 1:
        stage(nkt - 1, s_b, s_a)

    lv = lam_ref[...]
    lam = (jnp.exp(jnp.sum(lv[0:1] * lv[1:2], axis=-1, keepdims=True))
           - jnp.exp(jnp.sum(lv[2:3] * lv[3:4], axis=-1, keepdims=True)) + lam_init)
    sg = sg_ref[...]
    dvc = 2 * HEAD_DIM
    for hc in range(N_C_HEADS):
        o1 = acc_s[2 * hc, 0:dvc, :] / acc_s[2 * hc, dvc:dvc + 1, :]
        o2 = acc_s[2 * hc + 1, 0:dvc, :] / acc_s[2 * hc + 1, dvc:dvc + 1, :]
        o = o1 - lam * o2
        ms = jnp.mean(o * o, axis=0, keepdims=True)
        o = o * lax.rsqrt(ms + RMS_EPS) * sg * (1.0 - lam_init)
        o_ref[0, :, hc * LANES:(hc + 1) * LANES] = o.T.astype(BF16)
    for pair in range(N_D_HEADS // 2):
        outs = []
        for half in range(2):
            u = 2 * N_C_HEADS + 2 * pair + half
            outs.append(acc_s[u, 0:HEAD_DIM, :] / acc_s[u, HEAD_DIM:HEAD_DIM + 1, :])
        o2 = jnp.concatenate(outs, axis=0)
        c0 = N_C_HEADS * LANES + pair * LANES
        o_ref[0, :, c0:c0 + LANES] = o2.T.astype(BF16)


def _attn_cd(lam_vec, sg, qc, qd, kc, kd, vt, nct, lam_init):
    b, tall, _ = qc.shape
    nkt, nv, tk = vt.shape[1], vt.shape[2], vt.shape[3]
    mq = TM
    nq = tall // mq - nct
    ow = N_C_HEADS * LANES + N_D_HEADS * HEAD_DIM
    qtile = lambda bi, i: (bi, i + nct, 0)
    return pl.pallas_call(
        functools.partial(_attn_cd_kernel, nkt, lam_init),
        grid=(b, nq),
        in_specs=[pl.BlockSpec((4, HEAD_DIM), lambda bi, i: (0, 0)),
                  pl.BlockSpec((LANES, 1), lambda bi, i: (0, 0)),
                  pl.BlockSpec((1, mq, qc.shape[-1]), qtile),
                  pl.BlockSpec((1, mq, qd.shape[-1]), qtile),
                  pl.BlockSpec((1, tall, kc.shape[-1]), lambda bi, i: (bi, 0, 0)),
                  pl.BlockSpec((1, tall, kd.shape[-1]), lambda bi, i: (bi, 0, 0)),
                  pl.BlockSpec((1, nkt, nv, tk), lambda bi, i: (bi, 0, 0, 0))],
        out_specs=pl.BlockSpec((1, mq, ow), lambda bi, i: (bi, i, 0)),
        out_shape=jax.ShapeDtypeStruct((b, nq * mq, ow), BF16),
        scratch_shapes=[pltpu.VMEM((N_UNITS, mq, LANES), BF16),
                        pltpu.VMEM((N_UNITS, 1, mq), F32),
                        pltpu.VMEM((N_UNITS, 2 * HEAD_DIM + ONES_ROWS, mq), F32),
                        pltpu.VMEM((N_UNITS, tk, mq), F32),
                        pltpu.VMEM((N_UNITS, tk, mq), F32)],
        compiler_params=_cparams(("parallel", "parallel")),
        name="attention_cd",
    )(lam_vec, sg, qc, qd, kc, kd, vt)


def _oproj_kernel(n_parts, widths, nct, alpha, n_exp, *refs):
    o_refs = refs[:n_parts]
    (w_ref, x_ref, ctx_ref, mlat_ref, mctx_ref, lng_ref, lnb_ref, rwt_ref, rb_ref, tri_ref, ltri_ref,
     x1_ref, tg_ref, n8_ref, pos_ref, xs_ref, ys_hbm, zero_s, sem) = refs[n_parts:]
    d = x_ref.shape[-1]
    is_ctx = pl.program_id(1) < nct
    zero_s[...] = jnp.zeros(zero_s.shape, zero_s.dtype)
    tail0 =_tile_step() * xs_ref.shape[0] + TOP_K * x_ref.shape[1]
    clear = pltpu.make_async_copy(zero_s, ys_hbm.at[pl.ds(pl.multiple_of(tail0, ROW_GROUP), zero_s.shape[0])], sem)
    clear.start()
    y = None
    r0 = 0
    for o_ref, wd in zip(o_refs, widths):
        part = _dot(o_ref[0], w_ref[r0:r0 + wd, :])
        y = part if y is None else y + part
        r0 += wd
    m = jnp.where(is_ctx, mctx_ref[...], mlat_ref[0])
    x = jnp.where(is_ctx, ctx_ref[0], x_ref[0])
    x1 = _layernorm(alpha * x + m[:, 2 * d:3 * d] * y, lng_ref[...], lnb_ref[...])
    x1_ref[0] = x1
    hb = (x1 * (1.0 + m[:, 4 * d:5 * d]) + m[:, 3 * d:4 * d]).astype(BF16)
    logits = _dot_nt(rwt_ref[...], hb) + rb_ref[...]
    tm = logits.shape[1]
    row = lax.broadcasted_iota(jnp.int32, logits.shape, 0)
    vals, idxs = [], []
    for _ in range(TOP_K):
        mx = jnp.max(logits, axis=0, keepdims=True)
        idx = jnp.min(jnp.where(logits == mx, row, n_exp), axis=0, keepdims=True)
        vals.append(mx)
        idxs.append(idx)
        logits = jnp.where(row == idx, NEG, logits)
    es = [jnp.exp(v - vals[0]) for v in vals]
    tot = es[0] + es[1] + es[2] + es[3]
    pad_i = jnp.zeros((8 - TOP_K, tm), jnp.int32)
    pad_f = jnp.zeros((8 - TOP_K, tm), F32)
    te = jnp.concatenate(idxs + [pad_i], axis=0)
    tg_ref[0] = jnp.concatenate([e / tot for e in es] + [pad_f], axis=0)
    _, oh = _expert_onehot(te, n_exp)
    n8 = _groups(jnp.sum(oh, axis=1, keepdims=True))
    n8_ref[0] = jnp.broadcast_to(n8, (n_exp, LANES)).astype(jnp.int32)
    lbase = _dot(ltri_ref[...], jnp.broadcast_to(n8, (n_exp, LANES)).astype(BF16))[:, 0:1]
    before = _dot(oh.astype(BF16), tri_ref[...])
    where_to = ROW_GROUP * lbase + before
    pos = [jnp.sum(jnp.where(row == te[k:k + 1, :], where_to, 0.0), axis=0, keepdims=True).astype(jnp.int32)
           for k in range(TOP_K)]
    pos_ref[0] = jnp.concatenate(pos + [pad_i], axis=0)
    for c in range(xs_ref.shape[0] // tm):
        r = lax.broadcasted_iota(jnp.int32, (tm, tm), 0) + c * tm
        hit = r == pos[0]
        for k in range(1, TOP_K):
            hit = hit | (r == pos[k])
        xs_ref[c * tm:(c + 1) * tm, :] = _pack_bf16_pairs(_dot(jnp.where(hit, 1.0, 0.0).astype(BF16), hb))
    clear.wait()


def _oproj(parts, w_out, x, ctx, mlat, mctx, lng, lnb, rwt, rb, nct_out, x_tile_off, alpha):
    b, t, _ = parts[0].shape
    d = x.shape[-1]
    nt = t // TM
    n_exp = rwt.shape[0]
    widths = tuple(p.shape[-1] for p in parts)
    tile = lambda bi, i: (bi, i, 0)
    full = lambda bi, i: (0, 0)
    stream, stream_specs = _stream_specs(x, ctx, nct_out, d, x_tile_off)
    in_specs = [pl.BlockSpec((1, TM, wd), tile) for wd in widths] + [
        pl.BlockSpec(w_out.shape, full)] + stream_specs + [
        pl.BlockSpec((1, 1, N_MOD * d), lambda bi, i: (bi, 0, 0)),
        pl.BlockSpec((1, N_MOD * d), full),
        pl.BlockSpec((1, d), full),
        pl.BlockSpec((1, d), full),
        pl.BlockSpec((n_exp, d), full),
        pl.BlockSpec((n_exp, 1), full),
        pl.BlockSpec((TM, TM), full),
        pl.BlockSpec((n_exp, n_exp), full)]
    sr = _sort_rows(n_exp)
    tri = jnp.asarray(np.triu(np.ones((TM, TM)), 1), BF16)
    ltri = jnp.asarray(np.tril(np.ones((n_exp, n_exp)), -1), BF16)
    small = pl.BlockSpec((1, 8, TM), lambda bi, i: (bi, 0, i))
    return pl.pallas_call(
        functools.partial(_oproj_kernel, len(parts), widths, nct_out, alpha, n_exp),
        grid=(b, nt),
        in_specs=in_specs,
        out_specs=[pl.BlockSpec((1, TM, d), tile),
                   small,
                   pl.BlockSpec((1, n_exp, LANES), lambda bi, i: (bi * nt + i, 0, 0)),
                   small,
                   pl.BlockSpec((sr, d // 2), lambda bi, i: (bi * nt + i, 0)),
                   pl.BlockSpec(memory_space=pl.ANY)],
        out_shape=[jax.ShapeDtypeStruct((b, t, d), F32),
                   jax.ShapeDtypeStruct((b, 8, t), F32),
                   jax.ShapeDtypeStruct((b * nt, n_exp, LANES), jnp.int32),
                   jax.ShapeDtypeStruct((b, 8, t), jnp.int32),
                   jax.ShapeDtypeStruct((b * nt * sr, d // 2), U32),
                   jax.ShapeDtypeStruct(((b * nt + 1) * sr, d // 2), U32)],
        scratch_shapes=[pltpu.VMEM((sr - TOP_K * TM, d // 2), U32), pltpu.SemaphoreType.DMA],
        compiler_params=_cparams(("parallel", "parallel")),
        name="out_proj_norm_router",
    )(*parts, w_out, *stream, mlat, mctx, lng, lnb, rwt, rb, tri, ltri)


def _moe_kernel(be_ref, nx_ref, bv_ref, nu_ref, rprev_ref, rcur_ref, rnext_ref, xs_hbm, wgu_hbm, bgu_ref,
                wd_hbm, bd_ref, ys_in_hbm, ys_hbm, wgu_f, wd_f, wgu_s, wd_s, xbuf, ybuf, sem, gsem, ssem):
    del ys_in_hbm
    blk = pl.program_id(0)
    ff = wd_s.shape[0]
    gpb = xbuf.shape[1] // ROW_GROUP
    par = blk % 2

    def gathers(rec, buf, wait):
        if wait:
            pltpu.make_async_copy(xs_hbm.at[pl.ds(0, xbuf.shape[1])], xbuf.at[buf], gsem.at[buf]).wait()
            return
        for g in range(gpb):
            _group_copy(xs_hbm, rec[g], xbuf.at[buf], g, gsem.at[buf]).start(priority=g % 2)

    def scatters(rec, buf, wait):
        if wait:
            pltpu.make_async_copy(ybuf.at[buf], ys_hbm.at[pl.ds(0, ybuf.shape[1])], ssem.at[buf]).wait()
            return
        for g in range(gpb):
            _group_copy(ybuf.at[buf], g, ys_hbm, rec[gpb + g], ssem.at[buf]).start(priority=g % 2)

    def fetch(e):
        return (pltpu.make_async_copy(wgu_hbm.at[e], wgu_f, sem.at[0]),
                pltpu.make_async_copy(wd_hbm.at[e], wd_f, sem.at[1]))

    @pl.when(blk < nu_ref[0])
    def _():
        e = be_ref[blk]
        new_expert = (blk == 0) | (e != be_ref[jnp.maximum(blk - 1, 0)])

        @pl.when(blk == 0)
        def _():
            for cp in fetch(e):
                cp.start()

        @pl.when(new_expert)
        def _():
            for cp in fetch(e):
                cp.wait()
            rows = 128

            def cast_gu(r, c):
                r0 = pl.multiple_of(r * rows, rows)
                wgu_s[pl.ds(r0, rows), :] = wgu_f[pl.ds(r0, rows), :].astype(BF16)
                return c

            def cast_d(r, c):
                r0 = pl.multiple_of(r * rows, rows)
                wd_s[pl.ds(r0, rows), :] = wd_f[pl.ds(r0, rows), :].astype(BF16)
                return c

            lax.fori_loop(0, wgu_s.shape[0] // rows, cast_gu, 0)
            lax.fori_loop(0, wd_s.shape[0] // rows, cast_d, 0)

            @pl.when(nx_ref[blk] != e)
            def _():
                for cp in fetch(nx_ref[blk]):
                    cp.start()

        @pl.when(blk == 0)
        def _():
            gathers(rcur_ref, 0, False)

        gathers(rcur_ref, par, True)
        half = xbuf.shape[1] // 2

        def ffn(rows):
            gathers(rnext_ref, 1 - par, False)
            x = jnp.concatenate(_unpack_bf16_pairs(xbuf[par, rows, :]), axis=1)
            gu = _dot(x, wgu_s[...]) + bgu_ref[0]
            g = jnp.minimum(gu[:, :ff], SWIGLU_LIMIT)
            u = jnp.clip(gu[:, ff:], -SWIGLU_LIMIT, SWIGLU_LIMIT)
            act = (u + 1.0) * (g / (1.0 + jnp.exp(-SWIGLU_ALPHA * g)))
            y = _dot(act.astype(BF16), wd_s[...]) + bd_ref[0]
            ybuf[par, rows, :] = _pack_bf16_pairs(y.astype(BF16).astype(F32))
            if rows != slice(None):
                ybuf[par, half:, :] = jnp.zeros((half, ybuf.shape[2]), ybuf.dtype)
            scatters(rcur_ref, par, False)

        @pl.when(bv_ref[blk] > half)
        def _():
            ffn(slice(None))

        @pl.when(bv_ref[blk] <= half)
        def _():
            ffn(slice(0, half))

        @pl.when(blk > 0)
        def _():
            scatters(rprev_ref, 1 - par, True)

        @pl.when(blk == nu_ref[0] - 1)
        def _():
            gathers(rnext_ref, 1 - par, True)
            scatters(rcur_ref, par, True)


def _moe_experts(place, layer, xs, ys, w_gu, b_gu, w_down, b_down):
    d = 2 * xs.shape[1]
    ff2 = w_gu.shape[-1]
    ff = w_down.shape[-2]
    off = layer * w_gu.shape[1]
    n_exp = w_gu.shape[0] * w_gu.shape[1]
    w_gu = w_gu.reshape(n_exp, d, ff2)
    w_down = w_down.reshape(n_exp, ff, d)
    nb = place["n_blocks"]
    last = lambda nu: nu[0] - 1
    bblk = lambda i, be, nx, bv, nu: (be[i], 0, 0)
    rec = lambda f: pl.BlockSpec((SMEM_BLOCK,), f, memory_space=pltpu.SMEM)
    grid_spec = pltpu.PrefetchScalarGridSpec(
        num_scalar_prefetch=4,
        grid=(nb,),
        in_specs=[rec(lambda i, be, nx, bv, nu: (jnp.clip(i - 1, 0, last(nu)),)),
                  rec(lambda i, be, nx, bv, nu: (jnp.minimum(i, last(nu)),)),
                  rec(lambda i, be, nx, bv, nu: (jnp.minimum(i + 1, last(nu)),)),
                  pl.BlockSpec(memory_space=pl.ANY),
                  pl.BlockSpec(memory_space=pl.ANY),
                  pl.BlockSpec((1, 1, ff2), bblk),
                  pl.BlockSpec(memory_space=pl.ANY),
                  pl.BlockSpec((1, 1, d), bblk),
                  pl.BlockSpec(memory_space=pl.ANY)],
        out_specs=pl.BlockSpec(memory_space=pl.ANY),
        scratch_shapes=[pltpu.VMEM((d, ff2), F32), pltpu.VMEM((ff, d), F32),
                        pltpu.VMEM((d, ff2), BF16), pltpu.VMEM((ff, d), BF16),
                        pltpu.VMEM((2, MOE_BM, d // 2), U32), pltpu.VMEM((2, MOE_BM, d // 2), U32),
                        pltpu.SemaphoreType.DMA((2,)), pltpu.SemaphoreType.DMA((2,)),
                        pltpu.SemaphoreType.DMA((2,))])
    records = place["rec"]
    return pl.pallas_call(
        _moe_kernel,
        grid_spec=grid_spec,
        out_shape=jax.ShapeDtypeStruct(ys.shape, ys.dtype),
        input_output_aliases={12: 0},
        compiler_params=_cparams(("arbitrary",)),
        name="moe_experts",
    )(place["block_e"] + off, place["block_next_e"] + off, place["block_rows"], place["n_used"],
      records, records, records, xs, w_gu, b_gu.reshape(n_exp, 1, ff2), w_down, b_down.reshape(n_exp, 1, d), ys)


def _expert_onehot(te, n_exp):
    row = lax.broadcasted_iota(jnp.int32, (n_exp, te.shape[1]), 0)
    oh = jnp.zeros(row.shape, F32)
    for k in range(TOP_K):
        oh = oh + (row == te[k:k + 1, :]).astype(F32)
    return row, oh


def _groups(cnt):
    return jnp.floor((cnt + (ROW_GROUP - 1)) / ROW_GROUP)


def _placement(n8, n_asg):
    n_tiles, n_exp = n8.shape
    gpt = _sort_rows(n_exp) // ROW_GROUP
    gpb = MOE_BM // ROW_GROUP
    assert (TOP_K * TM + n_exp * (ROW_GROUP - 1)) // ROW_GROUP < gpt
    ids = jnp.arange(n_exp, dtype=jnp.int32)
    lbase = jnp.cumsum(n8, axis=1) - n8
    cum_incl = jnp.cumsum(n8, axis=0)
    cum_excl = cum_incl - n8
    groups = cum_incl[-1]
    padded = (groups + gpb - 1) // gpb * gpb
    pend = jnp.cumsum(padded)
    pstart = pend - padded
    n_blocks = -(-(n_asg + n_tiles * n_exp * (ROW_GROUP - 1)) // MOE_BM) + n_exp
    n_used = (pend[-1] // gpb).astype(jnp.int32)
    blocks = jnp.arange(n_blocks, dtype=jnp.int32)
    blk = jnp.minimum(blocks, n_used - 1) * gpb
    block_e = jnp.minimum(jnp.sum((pend[None, :] <= blk[:, None]).astype(jnp.int32), axis=1), n_exp - 1)
    later = (padded[None, :] > 0) & (ids[None, :] > block_e[:, None])
    block_next_e = jnp.min(jnp.where(later, ids[None, :], n_exp), axis=1)
    block_next_e = jnp.where(block_next_e == n_exp, block_e, block_next_e).astype(jnp.int32)
    mine = (block_e[:, None] == ids[None, :]).astype(jnp.int32)
    pick = lambda table: jnp.sum(mine[:, :, None] * table.T[None, :, :], axis=1)
    b_start = jnp.sum(mine * pstart[None, :], axis=1)
    b_groups = jnp.sum(mine * groups[None, :], axis=1)
    j_e = (blocks * gpb - b_start)[:, None] + jnp.arange(gpb, dtype=jnp.int32)[None, :]
    valid = (j_e < b_groups[:, None]) & (blocks < n_used)[:, None]
    tile = jnp.sum((pick(cum_incl)[:, None, :] <= j_e[:, :, None]).astype(jnp.int32), axis=2)
    tile = jnp.minimum(tile, n_tiles - 1)
    in_tile = (tile[:, :, None] == jnp.arange(n_tiles, dtype=jnp.int32)[None, None, :]).astype(jnp.int32)
    shift = jnp.sum(in_tile * (pick(lbase) - pick(cum_excl))[:, None, :], axis=2)
    held = tile * gpt + shift + j_e
    pad_dst = n_tiles * gpt + jnp.arange(gpb, dtype=jnp.int32)[None, :]
    src = jnp.where(valid, held, gpt - 1)
    dst = jnp.where(valid, held, pad_dst)
    rec = jnp.concatenate([src, dst, jnp.zeros((n_blocks, SMEM_BLOCK - 2 * gpb), jnp.int32)], axis=1)
    left = b_groups - (blocks - b_start // gpb) * gpb
    block_rows = (jnp.clip(left, 0, gpb) * ROW_GROUP).astype(jnp.int32)
    return dict(rec=rec.reshape(-1), block_e=block_e, block_next_e=block_next_e, block_rows=block_rows,
                n_used=n_used.reshape(1), n_blocks=n_blocks)


def _group_copy(src_ref, src_group, dst_ref, dst_group, sem):
    s0 = pl.multiple_of(src_group * ROW_GROUP, ROW_GROUP)
    d0 = pl.multiple_of(dst_group * ROW_GROUP, ROW_GROUP)
    return pltpu.make_async_copy(src_ref.at[pl.ds(s0, ROW_GROUP)], dst_ref.at[pl.ds(d0, ROW_GROUP)], sem)


def _tile_step():
    return pl.program_id(0) * pl.num_programs(1) + pl.program_id(1)


def _sort_rows(n_exp):
    return -(-(TOP_K * TM + n_exp * ROW_GROUP) // TM) * TM


def _combine_kernel(nct, alpha, ys_ref, pos_ref, tg_ref, x1_ref, mlat_ref, mctx_ref, lng_ref, lnb_ref, o_ref):
    d = x1_ref.shape[-1]
    tm = x1_ref.shape[1]
    is_ctx = pl.program_id(1) < nct
    m = jnp.where(is_ctx, mctx_ref[...], mlat_ref[0])
    pos = pos_ref[0]
    gates = tg_ref[0]
    f_lo = f_hi = None
    for c in range(ys_ref.shape[0] // tm):
        col = lax.broadcasted_iota(jnp.int32, (tm, tm), 1) + c * tm
        w = jnp.where(col == pos[:, 0:1], gates[:, 0:1], 0.0)
        for k in range(1, TOP_K):
            w = w + jnp.where(col == pos[:, k:k + 1], gates[:, k:k + 1], 0.0)
        w = w.astype(BF16)
        y_lo, y_hi = _unpack_bf16_pairs(ys_ref[c * tm:(c + 1) * tm, :])
        f_lo = _dot(w, y_lo) if f_lo is None else f_lo + _dot(w, y_lo)
        f_hi = _dot(w, y_hi) if f_hi is None else f_hi + _dot(w, y_hi)
    f = jnp.concatenate([f_lo, f_hi], axis=1)
    o_ref[0] = _layernorm(alpha * x1_ref[0] + m[:, 5 * d:6 * d] * f, lng_ref[...], lnb_ref[...])


def _combine_postnorm(ys, pos_t, tg_t, x1, mlat, mctx, lng, lnb, nct, alpha, n_exp):
    b, t, d = x1.shape
    nt = t // TM
    tile = lambda bi, i: (bi, i, 0)
    full = lambda bi, i: (0, 0)
    return pl.pallas_call(
        functools.partial(_combine_kernel, nct, alpha),
        grid=(b, nt),
        in_specs=[pl.BlockSpec((_sort_rows(n_exp), d // 2), lambda bi, i: (bi * nt + i, 0)),
                  pl.BlockSpec((1, TM, 8), tile),
                  pl.BlockSpec((1, TM, 8), tile),
                  pl.BlockSpec((1, TM, d), tile),
                  pl.BlockSpec((1, 1, N_MOD * d), lambda bi, i: (bi, 0, 0)),
                  pl.BlockSpec((1, N_MOD * d), full),
                  pl.BlockSpec((1, d), full),
                  pl.BlockSpec((1, d), full)],
        out_specs=pl.BlockSpec((1, TM, d), tile),
        out_shape=jax.ShapeDtypeStruct((b, t, d), F32),
        compiler_params=_cparams(("parallel", "parallel")),
        name="combine_post_norm",
    )(ys, pos_t, tg_t, x1, mlat, mctx, lng, lnb)


def _rope_tables(ctx_len, seq):
    t = np.arange(seq)
    row = (t // GRID_W).astype(np.float64)
    col = (t % GRID_W).astype(np.float64)
    nf = HEAD_DIM // 4
    inv = ROPE_THETA ** (-np.arange(nf, dtype=np.float64) / nf)
    ar = row[:, None] * inv[None, :]
    ac = col[:, None] * inv[None, :]
    ang = np.concatenate([ar, ar, ac, ac], axis=-1)
    cos = np.concatenate([np.ones((ctx_len, HEAD_DIM)), np.cos(ang)], axis=0)
    sin = np.concatenate([np.zeros((ctx_len, HEAD_DIM)), np.sin(ang)], axis=0)
    sign = np.where((np.arange(HEAD_DIM) % 32) < 16, -1.0, 1.0)[None, :]
    cos2 = np.tile(cos, (1, LANES // HEAD_DIM))
    sin2 = np.tile(sin * sign, (1, LANES // HEAD_DIM))
    return jnp.asarray(cos2, F32), jnp.asarray(sin2, F32)


def _channel_dft():
    c = np.arange(HEAD_DIM)
    ang = 2.0 * np.pi * ((c[:, None] * c[None, :]) % HEAD_DIM) / HEAD_DIM
    eye = np.eye(4)
    cs = np.concatenate([np.kron(eye, np.cos(ang)), np.kron(eye, np.sin(ang))], axis=1)
    return jnp.asarray(cs, BF16)


def _group_mean_matrix():
    bd = np.kron(np.eye(LANES // HEAD_DIM), np.full((HEAD_DIM, HEAD_DIM), 1.0 / HEAD_DIM))
    return jnp.asarray(bd, BF16)


def _dup_heads(wk, n_heads):
    d = wk.shape[0]
    return jnp.broadcast_to(wk.reshape(d, n_heads, 1, HEAD_DIM), (d, n_heads, 2, HEAD_DIM)).reshape(d, n_heads * LANES)


def _lambda_init(layer):
    return 0.8 - 0.6 * math.exp(-0.3 * layer)


def _moe_postnorm(routed, mlat, mctx, lng, lnb, nct, alpha, layer, w_gu, b_gu, w_down, b_down):
    x1, tg, n8, pos, xs, ys = routed
    n_exp = w_gu.shape[1]
    b, t, _ = x1.shape
    place = _placement(n8[:, :, 0], b * t * TOP_K)
    ys = _moe_experts(place, layer, xs, ys, w_gu, b_gu, w_down, b_down)
    pos_t, tg_t = jnp.transpose(pos, (0, 2, 1)), jnp.transpose(tg, (0, 2, 1))
    return _combine_postnorm(ys, pos_t, tg_t, x1, mlat, mctx, lng, lnb, nct, alpha, n_exp)


def kernel(x, c, ctx, c_ctx, mod_w, mod_b, ln_g, ln_b, ab_w_in, ab_sink, ab_w_out,
           cd_w_in, cd_lambda, cd_subln_g, cd_q_norm_g, cd_k_norm_g, cd_w_out,
           router_w, router_b, expert_w_gu, expert_b_gu, expert_w_down, expert_b_down):
    b, s, d = x.shape
    n_ctx = ctx.shape[1]
    depth = mod_w.shape[0]
    n_exp = router_w.shape[-1]
    assert d == 16 * HEAD_DIM and n_ctx % TM == 0 and s % TM == 0 and s % GRID_W == 0
    nct = n_ctx // TM
    alpha = (2 * depth) ** 0.25

    cos, sin = _rope_tables(n_ctx, s)
    cs_dft = _channel_dft()
    bd = _group_mean_matrix()
    mods = _mod_vectors(c, c_ctx, mod_w, mod_b)

    xs, xc = x, ctx
    for l in range(depth):
        last = l == depth - 1
        i = l // 2
        mlat = mods[l, :b].reshape(b, 1, N_MOD * d)
        mctx = mods[l, b:b + 1]
        lng1, lnb1 = ln_g[l, 0].reshape(1, d), ln_b[l, 0].reshape(1, d)
        lng2, lnb2 = ln_g[l, 1].reshape(1, d), ln_b[l, 1].reshape(1, d)
        rwt = router_w[l].T.astype(BF16)
        rb = router_b[l].reshape(n_exp, 1)
        if l % 2 == 0:
            w = ab_w_in[i]
            w_n = jnp.concatenate([w[:, :1024], _dup_heads(w[:, 1024:1280], 4)], axis=1).astype(BF16)
            wvt = w[:, 1280:1536].T.astype(BF16)
            uw, q, k, vt = _proj_ab(xs, xc, mlat, mctx, w_n, wvt, cos, sin, cs_dft, nct)
            oa = jnp.concatenate([_fourier(uw[:, :n_ctx]), _fourier(uw[:, n_ctx:])], axis=1)
            ob = _win_attn(ab_sink[i], q, k, vt, nct)
            if last:
                parts = [oa[:, n_ctx:], ob[:, n_ctx:]]
            else:
                parts = [oa, ob]
            w_out = ab_w_out[i].astype(BF16)
        else:
            w = cd_w_in[i]
            w_n = jnp.concatenate([w[:, :1536], _dup_heads(w[:, 1536:1664], 2)], axis=1).astype(BF16)
            wvt = w[:, 1664:2304].T.astype(BF16)
            gq = jnp.tile(cd_q_norm_g[i], 2).reshape(1, LANES)
            gk = jnp.tile(cd_k_norm_g[i], 2).reshape(1, LANES)
            qc, qd, kc, kd, vt = _proj_cd(xs, xc, mlat, mctx, w_n, wvt, cos, sin, bd, gq, gk, nct)
            sg = cd_subln_g[i].reshape(LANES, 1)
            o_lat = _attn_cd(cd_lambda[i], sg, qc, qd, kc, kd, vt, nct, _lambda_init(l))
            if last:
                parts = [o_lat]
            else:
                raise NotImplementedError("context outputs of a differential/axial layer")
            w_out = cd_w_out[i].astype(BF16)
        experts = (l, expert_w_gu, expert_b_gu, expert_w_down, expert_b_down)
        if last:
            routed = _oproj(parts, w_out, xs, xc, mlat, mctx, lng1, lnb1, rwt, rb, 0, nct, alpha)
            return _moe_postnorm(routed, mlat, mctx, lng2, lnb2, 0, alpha, *experts)
        routed = _oproj(parts, w_out, xs, xc, mlat, mctx, lng1, lnb1, rwt, rb, nct, 0, alpha)
        xs, xc = _moe_postnorm(routed, mlat, mctx, lng2, lnb2, nct, alpha, *experts), None
    return xs[:, n_ctx:]
```

```python
import functools
import math

import jax
import jax.numpy as jnp
import numpy as np
from jax import lax
from jax.experimental import pallas as pl
from jax.experimental.pallas import tpu as pltpu

F32 = jnp.float32
BF16 = jnp.bfloat16

HEAD_DIM = 64
GRID_W = 64
WINDOW = 128
ROPE_THETA = 10000.0
LN_EPS = 1e-6
RMS_EPS = 1e-6
N_MOD = 6
TOP_K = 4
SWIGLU_LIMIT = 7.0
SWIGLU_ALPHA = 1.702
NEG = -1e30
LOG2E = 1.4426950408889634
QSCALE = HEAD_DIM ** -0.5 * LOG2E

N_C_HEADS = 4
N_D_HEADS = 8
N_D_KV = 2
N_UNITS = 2 * N_C_HEADS + N_D_HEADS
ONES_ROWS = 16
_VT_LAYOUT = tuple((h * (2 * HEAD_DIM + ONES_ROWS), 2 * HEAD_DIM) for h in range(N_C_HEADS)) + tuple(
    (N_C_HEADS * (2 * HEAD_DIM + ONES_ROWS) + g * (HEAD_DIM + ONES_ROWS), HEAD_DIM) for g in range(N_D_KV))
VT_ROWS = _VT_LAYOUT[-1][0] + HEAD_DIM + ONES_ROWS

LANES = 128
TM = 256
MOE_BM = 512
ROW_GROUP = 8
SMEM_BLOCK = 1024
VMEM_LIMIT = 56 * 1024 * 1024


def _dot(a, b):
    return jnp.dot(a, b, preferred_element_type=F32)


def _dot_nt(a, b):
    return lax.dot_general(a, b, (((1,), (1,)), ((), ())), preferred_element_type=F32)


def _cparams(sem):
    return pltpu.CompilerParams(dimension_semantics=sem, vmem_limit_bytes=VMEM_LIMIT)


def _layernorm(z, g, b):
    mu = jnp.mean(z, axis=-1, keepdims=True)
    d = z - mu
    var = jnp.mean(d * d, axis=-1, keepdims=True)
    return d * lax.rsqrt(var + LN_EPS) * g + b


def _rope128(x, cos, sin_signed):
    lane = lax.broadcasted_iota(jnp.int32, x.shape, 1)
    lo = (lane % 32) < 16
    rot = jnp.where(lo, pltpu.roll(x, LANES - 16, 1), pltpu.roll(x, 16, 1))
    return x * cos + rot * sin_signed


def _rmsnorm128(x, g, bd):
    x2 = x * x
    hi = x2.astype(BF16)
    lo = (x2 - hi.astype(F32)).astype(BF16)
    ms = _dot(hi, bd) + _dot(lo, bd)
    return x * lax.rsqrt(ms + RMS_EPS) * g


U32 = jnp.uint32
_HI16 = 0xFFFF0000


def _pack_bf16_pairs(x):
    n = x.shape[1] // 2
    bits = lax.bitcast_convert_type(x, U32)
    return (bits[:, :n] >> 16) | (bits[:, n:] & U32(_HI16))


def _unpack_bf16_pairs(p):
    lo = lax.bitcast_convert_type(p << 16, F32).astype(BF16)
    hi = lax.bitcast_convert_type(p & U32(_HI16), F32).astype(BF16)
    return lo, hi


def _modulated(x_ref, ctx_ref, mlat_ref, mctx_ref, is_ctx, d):
    m = jnp.where(is_ctx, mctx_ref[...], mlat_ref[0])
    x = jnp.where(is_ctx, ctx_ref[0], x_ref[0])
    return (x * (1.0 + m[:, d:2 * d]) + m[:, 0:d]).astype(BF16)


def _stream_specs(x, ctx, nct, d, off=0):
    if ctx is None:
        return (x, x), [pl.BlockSpec((1, TM, d), lambda bi, i: (bi, i + off, 0)),
                        pl.BlockSpec((1, TM, d), lambda bi, i: (bi, jnp.minimum(i, max(nct - 1, 0)), 0))]
    return (x, ctx), [pl.BlockSpec((1, TM, d), lambda bi, i: (bi, jnp.maximum(i - nct, 0), 0)),
                      pl.BlockSpec((1, TM, d), lambda bi, i: (bi, jnp.minimum(i, nct - 1), 0))]


def _mod_kernel(c_ref, w_ref, b_ref, o_ref):
    c = c_ref[...]
    s = (c / (1.0 + jnp.exp(-c))).astype(BF16)
    o_ref[0] = _dot(s, w_ref[0].astype(BF16)) + b_ref[0]


def _mod_vectors(c, c_ctx, mod_w, mod_b):
    depth, d, n = mod_w.shape
    b = c.shape[0]
    rows = 8 * (-(-(b + 1) // 8))
    cs = jnp.zeros((rows, d), F32).at[:b].set(c).at[b].set(c_ctx)
    tn = 1536
    out = pl.pallas_call(
        _mod_kernel,
        grid=(depth, n // tn),
        in_specs=[pl.BlockSpec((rows, d), lambda l, j: (0, 0)),
                  pl.BlockSpec((1, d, tn), lambda l, j: (l, 0, j)),
                  pl.BlockSpec((1, 1, tn), lambda l, j: (l, 0, j))],
        out_specs=pl.BlockSpec((1, rows, tn), lambda l, j: (l, 0, j)),
        out_shape=jax.ShapeDtypeStruct((depth, rows, n), F32),
        compiler_params=_cparams(("parallel", "parallel")),
        name="mod_vectors",
    )(cs, mod_w, mod_b.reshape(depth, 1, n))
    return out


def _proj_ab_kernel(nct, x_ref, ctx_ref, mlat_ref, mctx_ref, w_ref, wvt_ref, cos_ref, sin_ref, cs_ref,
                    uw_ref, q_ref, k_ref, vt_ref):
    is_ctx = pl.program_id(1) < nct
    d = x_ref.shape[-1]
    h = _modulated(x_ref, ctx_ref, mlat_ref, mctx_ref, is_ctx, d)
    p = _dot(h, w_ref[...])
    uw_ref[0] = _dot(p[:, 0:256].astype(BF16), cs_ref[...]).astype(BF16)
    cos = cos_ref[...]
    sin = sin_ref[...]
    for j in range(6):
        c0 = 256 + j * LANES
        q_ref[0, :, j * LANES:(j + 1) * LANES] = (_rope128(p[:, c0:c0 + LANES], cos, sin) * QSCALE).astype(BF16)
    for j in range(4):
        c0 = 1024 + j * LANES
        k_ref[0, :, j * LANES:(j + 1) * LANES] = _rope128(p[:, c0:c0 + LANES], cos, sin).astype(BF16)
    vt = _dot_nt(wvt_ref[...], h).astype(BF16)
    for c in range(TM // LANES):
        vt_ref[0, c] = vt[:, c * LANES:(c + 1) * LANES]


def _proj_ab(x, ctx, mlat, mctx, w, wvt, cos, sin, cs, nct):
    b, d = x.shape[0], x.shape[2]
    tall = x.shape[1] + (0 if ctx is None else ctx.shape[1])
    stream, stream_specs = _stream_specs(x, ctx, nct, d)
    nt = tall // TM
    nw = w.shape[1]
    nchunk = tall // LANES
    cpt = TM // LANES
    return pl.pallas_call(
        functools.partial(_proj_ab_kernel, nct),
        grid=(b, nt),
        in_specs=stream_specs + [
                  pl.BlockSpec((1, 1, N_MOD * d), lambda bi, i: (bi, 0, 0)),
                  pl.BlockSpec((1, N_MOD * d), lambda bi, i: (0, 0)),
                  pl.BlockSpec((d, nw), lambda bi, i: (0, 0)),
                  pl.BlockSpec((256, d), lambda bi, i: (0, 0)),
                  pl.BlockSpec((TM, LANES), lambda bi, i: (i, 0)),
                  pl.BlockSpec((TM, LANES), lambda bi, i: (i, 0)),
                  pl.BlockSpec((256, 512), lambda bi, i: (0, 0))],
        out_specs=[pl.BlockSpec((1, TM, 512), lambda bi, i: (bi, i, 0)),
                   pl.BlockSpec((1, TM, 768), lambda bi, i: (bi, i, 0)),
                   pl.BlockSpec((1, TM, 512), lambda bi, i: (bi, i, 0)),
                   pl.BlockSpec((1, cpt, 256, LANES), lambda bi, i: (bi, i, 0, 0))],
        out_shape=[jax.ShapeDtypeStruct((b, tall, 512), BF16),
                   jax.ShapeDtypeStruct((b, tall, 768), BF16),
                   jax.ShapeDtypeStruct((b, tall, 512), BF16),
                   jax.ShapeDtypeStruct((b, nchunk, 256, LANES), BF16)],
        compiler_params=_cparams(("parallel", "parallel")),
        name="proj_ab",
    )(*stream, mlat, mctx, w, wvt, cos, sin, cs)


def _proj_cd_kernel(nct, x_ref, ctx_ref, mlat_ref, mctx_ref, w_ref, wvt_ref, cos_ref, sin_ref, bd_ref,
                    gq_ref, gk_ref, qc_ref, qd_ref, kc_ref, kd_ref, vt_ref):
    is_ctx = pl.program_id(1) < nct
    d = x_ref.shape[-1]
    h = _modulated(x_ref, ctx_ref, mlat_ref, mctx_ref, is_ctx, d)
    p = _dot(h, w_ref[...])
    cos = cos_ref[...]
    sin = sin_ref[...]
    bd = bd_ref[...]
    gq = gq_ref[...]
    gk = gk_ref[...]
    for j in range(4):
        sl = slice(j * LANES, (j + 1) * LANES)
        qc_ref[0, :, sl] = (_rope128(p[:, j * LANES:(j + 1) * LANES], cos, sin) * QSCALE).astype(BF16)
        c0 = 512 + j * LANES
        qd_ref[0, :, sl] = (_rope128(_rmsnorm128(p[:, c0:c0 + LANES], gq, bd), cos, sin) * QSCALE).astype(BF16)
        c0 = 1024 + j * LANES
        kc_ref[0, :, sl] = _rope128(p[:, c0:c0 + LANES], cos, sin).astype(BF16)
    for j in range(2):
        c0 = 1536 + j * LANES
        kd_ref[0, :, j * LANES:(j + 1) * LANES] = _rope128(
            _rmsnorm128(p[:, c0:c0 + LANES], gk, bd), cos, sin).astype(BF16)
    vt = _dot_nt(wvt_ref[...], h).astype(BF16)
    ones = jnp.ones((ONES_ROWS, vt.shape[1]), BF16)
    src = 0
    for dst, dv in _VT_LAYOUT:
        vt_ref[0, 0, dst:dst + dv, :] = vt[src:src + dv, :]
        vt_ref[0, 0, dst + dv:dst + dv + ONES_ROWS, :] = ones
        src += dv


def _proj_cd(x, ctx, mlat, mctx, w, wvt, cos, sin, bd, gq, gk, nct):
    b, d = x.shape[0], x.shape[2]
    tall = x.shape[1] + (0 if ctx is None else ctx.shape[1])
    stream, stream_specs = _stream_specs(x, ctx, nct, d)
    nt = tall // TM
    nw = w.shape[1]
    nv = VT_ROWS
    full = lambda bi, i: (0, 0)
    tile = lambda bi, i: (bi, i, 0)
    return pl.pallas_call(
        functools.partial(_proj_cd_kernel, nct),
        grid=(b, nt),
        in_specs=stream_specs + [
                  pl.BlockSpec((1, 1, N_MOD * d), lambda bi, i: (bi, 0, 0)),
                  pl.BlockSpec((1, N_MOD * d), full),
                  pl.BlockSpec((d, nw), full),
                  pl.BlockSpec(wvt.shape, full),
                  pl.BlockSpec((TM, LANES), lambda bi, i: (i, 0)),
                  pl.BlockSpec((TM, LANES), lambda bi, i: (i, 0)),
                  pl.BlockSpec((LANES, LANES), full),
                  pl.BlockSpec((1, LANES), full),
                  pl.BlockSpec((1, LANES), full)],
        out_specs=[pl.BlockSpec((1, TM, 512), tile),
                   pl.BlockSpec((1, TM, 512), tile),
                   pl.BlockSpec((1, TM, 512), tile),
                   pl.BlockSpec((1, TM, 256), tile),
                   pl.BlockSpec((1, 1, nv, TM), lambda bi, i: (bi, i, 0, 0))],
        out_shape=[jax.ShapeDtypeStruct((b, tall, 512), BF16),
                   jax.ShapeDtypeStruct((b, tall, 512), BF16),
                   jax.ShapeDtypeStruct((b, tall, 512), BF16),
                   jax.ShapeDtypeStruct((b, tall, 256), BF16),
                   jax.ShapeDtypeStruct((b, nt, nv, TM), BF16)],
        compiler_params=_cparams(("parallel", "parallel")),
        name="proj_cd",
    )(*stream, mlat, mctx, w, wvt, cos, sin, bd, gq, gk)


def _fourier_kernel(nb, scale, uw_ref, ca_ref, sa_ref, cb_ref, sb_ref, o_ref):
    j = pl.program_id(0)
    ca = ca_ref[pl.ds(j, 1), :]
    sa = sa_ref[pl.ds(j, 1), :]
    cb = cb_ref[...]
    sb = sb_ref[...]
    ct = (ca * cb - sa * sb).astype(BF16)
    nst = (-(sa * cb + ca * sb)).astype(BF16)
    for bi in range(nb):
        acc = _dot(ct, uw_ref[bi, :, 0:256]) + _dot(nst, uw_ref[bi, :, 256:512])
        o_ref[bi] = (acc * scale).astype(BF16)


def _dft_tables(t, tmf):
    k = np.arange(t, dtype=np.int64)
    j1 = np.arange(t // tmf, dtype=np.int64) * tmf
    j0 = np.arange(tmf, dtype=np.int64)
    aa = (2.0 * np.pi / t) * ((j1[:, None] * k[None, :]) % t)
    ab = (2.0 * np.pi / t) * ((j0[:, None] * k[None, :]) % t)
    f = lambda a: jnp.asarray(a, F32)
    return f(np.cos(aa)), f(np.sin(aa)), f(np.cos(ab)), f(np.sin(ab))


def _fourier(uw):
    b, t, _ = uw.shape
    tmf = min(128, t)
    ca, sa, cb, sb = _dft_tables(t, tmf)
    scale = 1.0 / math.sqrt(t * HEAD_DIM)
    full2 = lambda j: (0, 0)
    return pl.pallas_call(
        functools.partial(_fourier_kernel, b, scale),
        grid=(t // tmf,),
        in_specs=[pl.BlockSpec((b, t, 512), lambda j: (0, 0, 0)),
                  pl.BlockSpec((t // tmf, t), full2),
                  pl.BlockSpec((t // tmf, t), full2),
                  pl.BlockSpec((tmf, t), full2),
                  pl.BlockSpec((tmf, t), full2)],
        out_specs=pl.BlockSpec((b, tmf, 256), lambda j: (0, j, 0)),
        out_shape=jax.ShapeDtypeStruct((b, t, 256), BF16),
        compiler_params=_cparams(("parallel",)),
        name="fourier_mix",
    )(uw, ca, sa, cb, sb)


def _win_attn_kernel(nct, nchunk, sink_ref, q_ref, k_ref, vt_ref, o_ref, s_a, s_b):
    i = pl.program_id(1)
    is_ctx = i < nct
    j = i - nct
    ctx_rows = nct * TM
    ctx_chunks = ctx_rows // LANES
    lw = TM + 2 * WINDOW
    lchunks = lw // LANES
    cs = jnp.clip(ctx_chunks + (TM // LANES) * j - WINDOW // LANES, 0, nchunk - lchunks)
    rs = pl.multiple_of(cs * LANES, LANES)
    qpos = j * TM + lax.broadcasted_iota(jnp.int32, (1, TM), 1)
    kpos = cs * LANES - ctx_rows + lax.broadcasted_iota(jnp.int32, (lw, 1), 0)
    allowed = (jnp.abs(qpos - kpos) <= WINDOW) & (kpos >= 0) & jnp.logical_not(is_ctx)
    lane = lax.broadcasted_iota(jnp.int32, (TM, LANES), 1)
    n_heads = q_ref.shape[-1] // HEAD_DIM
    group = n_heads // (k_ref.shape[-1] // LANES)

    def scores(hq, dst_s):
        pair, half = divmod(hq, 2)
        g = hq // group
        qp = q_ref[0, :, pair * LANES:(pair + 1) * LANES]
        qm = jnp.where((lane >= HEAD_DIM) == (half == 1), qp, jnp.zeros_like(qp))
        dst_s[0:ctx_rows, :] = _dot_nt(k_ref[0, 0:ctx_rows, g * LANES:(g + 1) * LANES], qm)
        s_l = _dot_nt(k_ref[0, pl.ds(rs, lw), g * LANES:(g + 1) * LANES], qm)
        dst_s[ctx_rows:ctx_rows + lw, :] = jnp.where(allowed, s_l, NEG)

    def consume(hq, src_s):
        g = hq // group
        s = src_s[...]
        sk = sink_ref[hq] * LOG2E
        m = jnp.maximum(jnp.max(s, axis=0, keepdims=True), sk)
        p = jnp.exp2(s - m)
        l = jnp.sum(p, axis=0, keepdims=True) + jnp.exp2(sk - m)
        p = p.astype(BF16)
        vrows = slice(g * HEAD_DIM, (g + 1) * HEAD_DIM)
        acc = jnp.zeros((HEAD_DIM, TM), F32)
        for c in range(ctx_chunks):
            acc = acc + _dot(vt_ref[0, c, vrows, :], p[c * LANES:(c + 1) * LANES, :])
        for c in range(lchunks):
            r0 = ctx_rows + c * LANES
            acc = acc + _dot(vt_ref[0, cs + c, vrows, :], p[r0:r0 + LANES, :])
        return acc / l

    bufs = (s_a, s_b)
    scores(0, s_a)
    outs = []
    for hq in range(n_heads):
        if hq + 1 < n_heads:
            scores(hq + 1, bufs[(hq + 1) % 2])
        outs.append(consume(hq, bufs[hq % 2]))
        if hq % 2 == 1:
            pair = hq // 2
            o2 = jnp.concatenate(outs, axis=0)
            o_ref[0, :, pair * LANES:(pair + 1) * LANES] = o2.T.astype(BF16)
            outs = []


def _win_attn(sink, q, k, vt, nct):
    b, tall, qw = q.shape
    nt = tall // TM
    nchunk = vt.shape[1]
    return pl.pallas_call(
        functools.partial(_win_attn_kernel, nct, nchunk),
        grid=(b, nt),
        in_specs=[pl.BlockSpec(memory_space=pltpu.SMEM),
                  pl.BlockSpec((1, TM, qw), lambda bi, i: (bi, i, 0)),
                  pl.BlockSpec((1, tall, k.shape[-1]), lambda bi, i: (bi, 0, 0)),
                  pl.BlockSpec((1, nchunk, vt.shape[2], LANES), lambda bi, i: (bi, 0, 0, 0))],
        out_specs=pl.BlockSpec((1, TM, qw), lambda bi, i: (bi, i, 0)),
        out_shape=jax.ShapeDtypeStruct((b, tall, qw), BF16),
        scratch_shapes=[pltpu.VMEM((nct * TM + TM + 2 * WINDOW, TM), F32)] * 2,
        compiler_params=_cparams(("parallel", "parallel")),
        name="window_attention",
    )(sink, q, k, vt)


def _unit_operands(u):
    if u < 2 * N_C_HEADS:
        hc = u // 2
        dst, dv = _VT_LAYOUT[hc]
        return True, hc, dst, dv
    g = (u - 2 * N_C_HEADS) // (N_D_HEADS // N_D_KV)
    dst, dv = _VT_LAYOUT[N_C_HEADS + g]
    return False, g, dst, dv


def _attn_cd_kernel(nkt, lam_init, lam_ref, sg_ref, qc_ref, qd_ref, kc_ref, kd_ref, vt_ref, o_ref,
                    qm_s, m_s, acc_s, s_a, s_b):
    mq = qc_ref.shape[1]
    tk = vt_ref.shape[-1]
    lane = lax.broadcasted_iota(jnp.int32, (mq, LANES), 1)
    upper = lane >= HEAD_DIM
    for u in range(N_UNITS):
        src = qc_ref if u < 2 * N_C_HEADS else qd_ref
        pair = (u if u < 2 * N_C_HEADS else u - 2 * N_C_HEADS) // 2
        qp = src[0, :, pair * LANES:(pair + 1) * LANES]
        qm_s[u] = jnp.where(upper == (u % 2 == 1), qp, jnp.zeros_like(qp))
    m_s[...] = jnp.full(m_s.shape, NEG, F32)
    acc_s[...] = jnp.zeros(acc_s.shape, F32)

    def scores(tile, u, dst_s):
        is_c, slab, _, _ = _unit_operands(u)
        k_ref = kc_ref if is_c else kd_ref
        r0 = pl.multiple_of(tile * tk, tk)
        dst_s[u] = _dot_nt(k_ref[0, pl.ds(r0, tk), slab * LANES:(slab + 1) * LANES], qm_s[u])

    def consume(tile, u, src_s):
        _, _, row0, dv = _unit_operands(u)
        rows = dv + ONES_ROWS
        m_old = m_s[u]
        m_new = jnp.maximum(m_old, jnp.max(src_s[u], axis=0, keepdims=True))
        alpha = jnp.exp2(m_old - m_new)
        p = jnp.exp2(src_s[u] - m_new).astype(BF16)
        pv = _dot(vt_ref[0, tile, row0:row0 + rows, :], p)
        acc_s[u, 0:rows, :] = acc_s[u, 0:rows, :] * alpha + pv
        m_s[u] = m_new

    def stage(tile, src_s, dst_s):
        nxt = jnp.minimum(tile + 1, nkt - 1)
        for u in range(N_UNITS):
            scores(nxt, u, dst_s)
            consume(tile, u, src_s)

    for u in range(N_UNITS):
        scores(0, u, s_a)
    stage(0, s_a, s_b)

    bufs = (s_a, s_b)
    per_trip = 8

    def body(trip, carry):
        for j in range(per_trip):
            tile = per_trip * trip + 1 + j
            stage(tile, bufs[(1 + j) % 2], bufs[j % 2])
        return carry

    n_trips = (nkt - 1) // per_trip
    lax.fori_loop(0, n_trips, body, 0)
    for tile in range(1 + n_trips * per_trip, nkt):
        stage(tile, bufs[tile % 2], bufs[(tile + 1) % 2])

    lv = lam_ref[...]
    lam = (jnp.exp(jnp.sum(lv[0:1] * lv[1:2], axis=-1, keepdims=True))
           - jnp.exp(jnp.sum(lv[2:3] * lv[3:4], axis=-1, keepdims=True)) + lam_init)
    sg = sg_ref[...]
    dvc = 2 * HEAD_DIM
    for hc in range(N_C_HEADS):
        o1 = acc_s[2 * hc, 0:dvc, :] / acc_s[2 * hc, dvc:dvc + 1, :]
        o2 = acc_s[2 * hc + 1, 0:dvc, :] / acc_s[2 * hc + 1, dvc:dvc + 1, :]
        o = o1 - lam * o2
        ms = jnp.mean(o * o, axis=0, keepdims=True)
        o = o * lax.rsqrt(ms + RMS_EPS) * sg * (1.0 - lam_init)
        o_ref[0, :, hc * LANES:(hc + 1) * LANES] = o.T.astype(BF16)
    for pair in range(N_D_HEADS // 2):
        outs = []
        for half in range(2):
            u = 2 * N_C_HEADS + 2 * pair + half
            outs.append(acc_s[u, 0:HEAD_DIM, :] / acc_s[u, HEAD_DIM:HEAD_DIM + 1, :])
        o2 = jnp.concatenate(outs, axis=0)
        c0 = N_C_HEADS * LANES + pair * LANES
        o_ref[0, :, c0:c0 + LANES] = o2.T.astype(BF16)


def _attn_cd(lam_vec, sg, qc, qd, kc, kd, vt, nct, lam_init):
    b, tall, _ = qc.shape
    nkt, nv, tk = vt.shape[1], vt.shape[2], vt.shape[3]
    mq = TM
    nq = tall // mq - nct
    ow = N_C_HEADS * LANES + N_D_HEADS * HEAD_DIM
    qtile = lambda bi, i: (bi, i + nct, 0)
    return pl.pallas_call(
        functools.partial(_attn_cd_kernel, nkt, lam_init),
        grid=(b, nq),
        in_specs=[pl.BlockSpec((4, HEAD_DIM), lambda bi, i: (0, 0)),
                  pl.BlockSpec((LANES, 1), lambda bi, i: (0, 0)),
                  pl.BlockSpec((1, mq, qc.shape[-1]), qtile),
                  pl.BlockSpec((1, mq, qd.shape[-1]), qtile),
                  pl.BlockSpec((1, tall, kc.shape[-1]), lambda bi, i: (bi, 0, 0)),
                  pl.BlockSpec((1, tall, kd.shape[-1]), lambda bi, i: (bi, 0, 0)),
                  pl.BlockSpec((1, nkt, nv, tk), lambda bi, i: (bi, 0, 0, 0))],
        out_specs=pl.BlockSpec((1, mq, ow), lambda bi, i: (bi, i, 0)),
        out_shape=jax.ShapeDtypeStruct((b, nq * mq, ow), BF16),
        scratch_shapes=[pltpu.VMEM((N_UNITS, mq, LANES), BF16),
                        pltpu.VMEM((N_UNITS, 1, mq), F32),
                        pltpu.VMEM((N_UNITS, 2 * HEAD_DIM + ONES_ROWS, mq), F32),
                        pltpu.VMEM((N_UNITS, tk, mq), F32),
                        pltpu.VMEM((N_UNITS, tk, mq), F32)],
        compiler_params=_cparams(("parallel", "parallel")),
        name="attention_cd",
    )(lam_vec, sg, qc, qd, kc, kd, vt)


def _oproj_kernel(n_parts, widths, nct, alpha, n_exp, *refs):
    o_refs = refs[:n_parts]
    (w_ref, x_ref, ctx_ref, mlat_ref, mctx_ref, lng_ref, lnb_ref, rwt_ref, rb_ref, tri_ref, ltri_ref,
     x1_ref, tg_ref, n8_ref, pos_ref, xs_ref, ys_hbm, zero_s, sem) = refs[n_parts:]
    d = x_ref.shape[-1]
    is_ctx = pl.program_id(1) < nct
    zero_s[...] = jnp.zeros(zero_s.shape, zero_s.dtype)
    tail0 =_tile_step() * xs_ref.shape[0] + TOP_K * x_ref.shape[1]
    clear = pltpu.make_async_copy(zero_s, ys_hbm.at[pl.ds(pl.multiple_of(tail0, ROW_GROUP), zero_s.shape[0])], sem)
    clear.start()
    y = None
    r0 = 0
    for o_ref, wd in zip(o_refs, widths):
        part = _dot(o_ref[0], w_ref[r0:r0 + wd, :])
        y = part if y is None else y + part
        r0 += wd
    m = jnp.where(is_ctx, mctx_ref[...], mlat_ref[0])
    x = jnp.where(is_ctx, ctx_ref[0], x_ref[0])
    x1 = _layernorm(alpha * x + m[:, 2 * d:3 * d] * y, lng_ref[...], lnb_ref[...])
    x1_ref[0] = x1
    hb = (x1 * (1.0 + m[:, 4 * d:5 * d]) + m[:, 3 * d:4 * d]).astype(BF16)
    logits = _dot_nt(rwt_ref[...], hb) + rb_ref[...]
    tm = logits.shape[1]
    row = lax.broadcasted_iota(jnp.int32, logits.shape, 0)
    vals, idxs = [], []
    for _ in range(TOP_K):
        mx = jnp.max(logits, axis=0, keepdims=True)
        idx = jnp.min(jnp.where(logits == mx, row, n_exp), axis=0, keepdims=True)
        vals.append(mx)
        idxs.append(idx)
        logits = jnp.where(row == idx, NEG, logits)
    es = [jnp.exp(v - vals[0]) for v in vals]
    tot = es[0] + es[1] + es[2] + es[3]
    pad_i = jnp.zeros((8 - TOP_K, tm), jnp.int32)
    pad_f = jnp.zeros((8 - TOP_K, tm), F32)
    te = jnp.concatenate(idxs + [pad_i], axis=0)
    tg_ref[0] = jnp.concatenate([e / tot for e in es] + [pad_f], axis=0)
    _, oh = _expert_onehot(te, n_exp)
    n8 = _groups(jnp.sum(oh, axis=1, keepdims=True))
    n8_ref[0] = jnp.broadcast_to(n8, (n_exp, LANES)).astype(jnp.int32)
    lbase = _dot(ltri_ref[...], jnp.broadcast_to(n8, (n_exp, LANES)).astype(BF16))[:, 0:1]
    before = _dot(oh.astype(BF16), tri_ref[...])
    where_to = ROW_GROUP * lbase + before
    pos = [jnp.sum(jnp.where(row == te[k:k + 1, :], where_to, 0.0), axis=0, keepdims=True).astype(jnp.int32)
           for k in range(TOP_K)]
    pos_ref[0] = jnp.concatenate(pos + [pad_i], axis=0)
    for c in range(xs_ref.shape[0] // tm):
        r = lax.broadcasted_iota(jnp.int32, (tm, tm), 0) + c * tm
        hit = r == pos[0]
        for k in range(1, TOP_K):
            hit = hit | (r == pos[k])
        xs_ref[c * tm:(c + 1) * tm, :] = _pack_bf16_pairs(_dot(jnp.where(hit, 1.0, 0.0).astype(BF16), hb))
    clear.wait()


def _oproj(parts, w_out, x, ctx, mlat, mctx, lng, lnb, rwt, rb, nct_out, x_tile_off, alpha):
    b, t, _ = parts[0].shape
    d = x.shape[-1]
    nt = t // TM
    n_exp = rwt.shape[0]
    widths = tuple(p.shape[-1] for p in parts)
    tile = lambda bi, i: (bi, i, 0)
    full = lambda bi, i: (0, 0)
    stream, stream_specs = _stream_specs(x, ctx, nct_out, d, x_tile_off)
    in_specs = [pl.BlockSpec((1, TM, wd), tile) for wd in widths] + [
        pl.BlockSpec(w_out.shape, full)] + stream_specs + [
        pl.BlockSpec((1, 1, N_MOD * d), lambda bi, i: (bi, 0, 0)),
        pl.BlockSpec((1, N_MOD * d), full),
        pl.BlockSpec((1, d), full),
        pl.BlockSpec((1, d), full),
        pl.BlockSpec((n_exp, d), full),
        pl.BlockSpec((n_exp, 1), full),
        pl.BlockSpec((TM, TM), full),
        pl.BlockSpec((n_exp, n_exp), full)]
    sr = _sort_rows(n_exp)
    tri = jnp.asarray(np.triu(np.ones((TM, TM)), 1), BF16)
    ltri = jnp.asarray(np.tril(np.ones((n_exp, n_exp)), -1), BF16)
    small = pl.BlockSpec((1, 8, TM), lambda bi, i: (bi, 0, i))
    return pl.pallas_call(
        functools.partial(_oproj_kernel, len(parts), widths, nct_out, alpha, n_exp),
        grid=(b, nt),
        in_specs=in_specs,
        out_specs=[pl.BlockSpec((1, TM, d), tile),
                   small,
                   pl.BlockSpec((1, n_exp, LANES), lambda bi, i: (bi * nt + i, 0, 0)),
                   small,
                   pl.BlockSpec((sr, d // 2), lambda bi, i: (bi * nt + i, 0)),
                   pl.BlockSpec(memory_space=pl.ANY)],
        out_shape=[jax.ShapeDtypeStruct((b, t, d), F32),
                   jax.ShapeDtypeStruct((b, 8, t), F32),
                   jax.ShapeDtypeStruct((b * nt, n_exp, LANES), jnp.int32),
                   jax.ShapeDtypeStruct((b, 8, t), jnp.int32),
                   jax.ShapeDtypeStruct((b * nt * sr, d // 2), U32),
                   jax.ShapeDtypeStruct(((b * nt + 1) * sr, d // 2), U32)],
        scratch_shapes=[pltpu.VMEM((sr - TOP_K * TM, d // 2), U32), pltpu.SemaphoreType.DMA],
        compiler_params=_cparams(("parallel", "parallel")),
        name="out_proj_norm_router",
    )(*parts, w_out, *stream, mlat, mctx, lng, lnb, rwt, rb, tri, ltri)


def _moe_kernel(be_ref, nx_ref, bv_ref, nu_ref, rprev_ref, rcur_ref, rnext_ref, xs_hbm, wgu_hbm, bgu_ref,
                wd_hbm, bd_ref, ys_in_hbm, ys_hbm, wgu_f, wd_f, wgu_s, wd_s, xbuf, ybuf, sem, gsem, ssem):
    del ys_in_hbm
    blk = pl.program_id(0)
    ff = wd_s.shape[0]
    gpb = xbuf.shape[1] // ROW_GROUP
    par = blk % 2

    def gathers(rec, buf, wait):
        if wait:
            pltpu.make_async_copy(xs_hbm.at[pl.ds(0, xbuf.shape[1])], xbuf.at[buf], gsem.at[buf]).wait()
            return
        for g in range(gpb):
            _group_copy(xs_hbm, rec[g], xbuf.at[buf], g, gsem.at[buf]).start(priority=g % 2)

    def scatters(rec, buf, wait):
        if wait:
            pltpu.make_async_copy(ybuf.at[buf], ys_hbm.at[pl.ds(0, ybuf.shape[1])], ssem.at[buf]).wait()
            return
        for g in range(gpb):
            _group_copy(ybuf.at[buf], g, ys_hbm, rec[gpb + g], ssem.at[buf]).start(priority=g % 2)

    def fetch(e):
        return (pltpu.make_async_copy(wgu_hbm.at[e], wgu_f, sem.at[0]),
                pltpu.make_async_copy(wd_hbm.at[e], wd_f, sem.at[1]))

    @pl.when(blk < nu_ref[0])
    def _():
        e = be_ref[blk]
        new_expert = (blk == 0) | (e != be_ref[jnp.maximum(blk - 1, 0)])

        @pl.when(blk == 0)
        def _():
            for cp in fetch(e):
                cp.start()

        @pl.when(new_expert)
        def _():
            for cp in fetch(e):
                cp.wait()
            rows = 128

            def cast_gu(r, c):
                r0 = pl.multiple_of(r * rows, rows)
                wgu_s[pl.ds(r0, rows), :] = wgu_f[pl.ds(r0, rows), :].astype(BF16)
                return c

            def cast_d(r, c):
                r0 = pl.multiple_of(r * rows, rows)
                wd_s[pl.ds(r0, rows), :] = wd_f[pl.ds(r0, rows), :].astype(BF16)
                return c

            lax.fori_loop(0, wgu_s.shape[0] // rows, cast_gu, 0)
            lax.fori_loop(0, wd_s.shape[0] // rows, cast_d, 0)

            @pl.when(nx_ref[blk] != e)
            def _():
                for cp in fetch(nx_ref[blk]):
                    cp.start()

        @pl.when(blk == 0)
        def _():
            gathers(rcur_ref, 0, False)

        gathers(rcur_ref, par, True)
        half = xbuf.shape[1] // 2

        def ffn(rows):
            gathers(rnext_ref, 1 - par, False)
            x = jnp.concatenate(_unpack_bf16_pairs(xbuf[par, rows, :]), axis=1)
            gu = _dot(x, wgu_s[...]) + bgu_ref[0]
            g = jnp.minimum(gu[:, :ff], SWIGLU_LIMIT)
            u = jnp.clip(gu[:, ff:], -SWIGLU_LIMIT, SWIGLU_LIMIT)
            act = (u + 1.0) * (g / (1.0 + jnp.exp(-SWIGLU_ALPHA * g)))
            y = _dot(act.astype(BF16), wd_s[...]) + bd_ref[0]
            ybuf[par, rows, :] = _pack_bf16_pairs(y.astype(BF16).astype(F32))
            if rows != slice(None):
                ybuf[par, half:, :] = jnp.zeros((half, ybuf.shape[2]), ybuf.dtype)
            scatters(rcur_ref, par, False)

        @pl.when(bv_ref[blk] > half)
        def _():
            ffn(slice(None))

        @pl.when(bv_ref[blk] <= half)
        def _():
            ffn(slice(0, half))

        @pl.when(blk > 0)
        def _():
            scatters(rprev_ref, 1 - par, True)

        @pl.when(blk == nu_ref[0] - 1)
        def _():
            gathers(rnext_ref, 1 - par, True)
            scatters(rcur_ref, par, True)


def _moe_experts(place, layer, xs, ys, w_gu, b_gu, w_down, b_down):
    d = 2 * xs.shape[1]
    ff2 = w_gu.shape[-1]
    ff = w_down.shape[-2]
    off = layer * w_gu.shape[1]
    n_exp = w_gu.shape[0] * w_gu.shape[1]
    w_gu = w_gu.reshape(n_exp, d, ff2)
    w_down = w_down.reshape(n_exp, ff, d)
    nb = place["n_blocks"]
    last = lambda nu: nu[0] - 1
    bblk = lambda i, be, nx, bv, nu: (be[i], 0, 0)
    rec = lambda f: pl.BlockSpec((SMEM_BLOCK,), f, memory_space=pltpu.SMEM)
    grid_spec = pltpu.PrefetchScalarGridSpec(
        num_scalar_prefetch=4,
        grid=(nb,),
        in_specs=[rec(lambda i, be, nx, bv, nu: (jnp.clip(i - 1, 0, last(nu)),)),
                  rec(lambda i, be, nx, bv, nu: (jnp.minimum(i, last(nu)),)),
                  rec(lambda i, be, nx, bv, nu: (jnp.minimum(i + 1, last(nu)),)),
                  pl.BlockSpec(memory_space=pl.ANY),
                  pl.BlockSpec(memory_space=pl.ANY),
                  pl.BlockSpec((1, 1, ff2), bblk),
                  pl.BlockSpec(memory_space=pl.ANY),
                  pl.BlockSpec((1, 1, d), bblk),
                  pl.BlockSpec(memory_space=pl.ANY)],
        out_specs=pl.BlockSpec(memory_space=pl.ANY),
        scratch_shapes=[pltpu.VMEM((d, ff2), F32), pltpu.VMEM((ff, d), F32),
                        pltpu.VMEM((d, ff2), BF16), pltpu.VMEM((ff, d), BF16),
                        pltpu.VMEM((2, MOE_BM, d // 2), U32), pltpu.VMEM((2, MOE_BM, d // 2), U32),
                        pltpu.SemaphoreType.DMA((2,)), pltpu.SemaphoreType.DMA((2,)),
                        pltpu.SemaphoreType.DMA((2,))])
    records = place["rec"]
    return pl.pallas_call(
        _moe_kernel,
        grid_spec=grid_spec,
        out_shape=jax.ShapeDtypeStruct(ys.shape, ys.dtype),
        input_output_aliases={12: 0},
        compiler_params=_cparams(("arbitrary",)),
        name="moe_experts",
    )(place["block_e"] + off, place["block_next_e"] + off, place["block_rows"], place["n_used"],
      records, records, records, xs, w_gu, b_gu.reshape(n_exp, 1, ff2), w_down, b_down.reshape(n_exp, 1, d), ys)


def _expert_onehot(te, n_exp):
    row = lax.broadcasted_iota(jnp.int32, (n_exp, te.shape[1]), 0)
    oh = jnp.zeros(row.shape, F32)
    for k in range(TOP_K):
        oh = oh + (row == te[k:k + 1, :]).astype(F32)
    return row, oh


def _groups(cnt):
    return jnp.floor((cnt + (ROW_GROUP - 1)) / ROW_GROUP)


def _placement(n8, n_asg):
    n_tiles, n_exp = n8.shape
    gpt = _sort_rows(n_exp) // ROW_GROUP
    gpb = MOE_BM // ROW_GROUP
    assert (TOP_K * TM + n_exp * (ROW_GROUP - 1)) // ROW_GROUP < gpt
    ids = jnp.arange(n_exp, dtype=jnp.int32)
    lbase = jnp.cumsum(n8, axis=1) - n8
    cum_incl = jnp.cumsum(n8, axis=0)
    cum_excl = cum_incl - n8
    groups = cum_incl[-1]
    padded = (groups + gpb - 1) // gpb * gpb
    pend = jnp.cumsum(padded)
    pstart = pend - padded
    n_blocks = -(-(n_asg + n_tiles * n_exp * (ROW_GROUP - 1)) // MOE_BM) + n_exp
    n_used = (pend[-1] // gpb).astype(jnp.int32)
    blocks = jnp.arange(n_blocks, dtype=jnp.int32)
    blk = jnp.minimum(blocks, n_used - 1) * gpb
    block_e = jnp.minimum(jnp.sum((pend[None, :] <= blk[:, None]).astype(jnp.int32), axis=1), n_exp - 1)
    later = (padded[None, :] > 0) & (ids[None, :] > block_e[:, None])
    block_next_e = jnp.min(jnp.where(later, ids[None, :], n_exp), axis=1)
    block_next_e = jnp.where(block_next_e == n_exp, block_e, block_next_e).astype(jnp.int32)
    mine = (block_e[:, None] == ids[None, :]).astype(jnp.int32)
    pick = lambda table: jnp.sum(mine[:, :, None] * table.T[None, :, :], axis=1)
    b_start = jnp.sum(mine * pstart[None, :], axis=1)
    b_groups = jnp.sum(mine * groups[None, :], axis=1)
    j_e = (blocks * gpb - b_start)[:, None] + jnp.arange(gpb, dtype=jnp.int32)[None, :]
    valid = (j_e < b_groups[:, None]) & (blocks < n_used)[:, None]
    tile = jnp.sum((pick(cum_incl)[:, None, :] <= j_e[:, :, None]).astype(jnp.int32), axis=2)
    tile = jnp.minimum(tile, n_tiles - 1)
    in_tile = (tile[:, :, None] == jnp.arange(n_tiles, dtype=jnp.int32)[None, None, :]).astype(jnp.int32)
    shift = jnp.sum(in_tile * (pick(lbase) - pick(cum_excl))[:, None, :], axis=2)
    held = tile * gpt + shift + j_e
    pad_dst = n_tiles * gpt + jnp.arange(gpb, dtype=jnp.int32)[None, :]
    src = jnp.where(valid, held, gpt - 1)
    dst = jnp.where(valid, held, pad_dst)
    rec = jnp.concatenate([src, dst, jnp.zeros((n_blocks, SMEM_BLOCK - 2 * gpb), jnp.int32)], axis=1)
    left = b_groups - (blocks - b_start // gpb) * gpb
    block_rows = (jnp.clip(left, 0, gpb) * ROW_GROUP).astype(jnp.int32)
    return dict(rec=rec.reshape(-1), block_e=block_e, block_next_e=block_next_e, block_rows=block_rows,
                n_used=n_used.reshape(1), n_blocks=n_blocks)


def _group_copy(src_ref, src_group, dst_ref, dst_group, sem):
    s0 = pl.multiple_of(src_group * ROW_GROUP, ROW_GROUP)
    d0 = pl.multiple_of(dst_group * ROW_GROUP, ROW_GROUP)
    return pltpu.make_async_copy(src_ref.at[pl.ds(s0, ROW_GROUP)], dst_ref.at[pl.ds(d0, ROW_GROUP)], sem)


def _tile_step():
    return pl.program_id(0) * pl.num_programs(1) + pl.program_id(1)


def _sort_rows(n_exp):
    return -(-(TOP_K * TM + n_exp * ROW_GROUP) // TM) * TM


def _combine_kernel(nct, alpha, ys_ref, pos_ref, tg_ref, x1_ref, mlat_ref, mctx_ref, lng_ref, lnb_ref, o_ref):
    d = x1_ref.shape[-1]
    tm = x1_ref.shape[1]
    is_ctx = pl.program_id(1) < nct
    m = jnp.where(is_ctx, mctx_ref[...], mlat_ref[0])
    pos = pos_ref[0]
    gates = tg_ref[0]
    f_lo = f_hi = None
    for c in range(ys_ref.shape[0] // tm):
        col = lax.broadcasted_iota(jnp.int32, (tm, tm), 1) + c * tm
        w = jnp.where(col == pos[:, 0:1], gates[:, 0:1], 0.0)
        for k in range(1, TOP_K):
            w = w + jnp.where(col == pos[:, k:k + 1], gates[:, k:k + 1], 0.0)
        w = w.astype(BF16)
        y_lo, y_hi = _unpack_bf16_pairs(ys_ref[c * tm:(c + 1) * tm, :])
        f_lo = _dot(w, y_lo) if f_lo is None else f_lo + _dot(w, y_lo)
        f_hi = _dot(w, y_hi) if f_hi is None else f_hi + _dot(w, y_hi)
    f = jnp.concatenate([f_lo, f_hi], axis=1)
    o_ref[0] = _layernorm(alpha * x1_ref[0] + m[:, 5 * d:6 * d] * f, lng_ref[...], lnb_ref[...])


def _combine_postnorm(ys, pos_t, tg_t, x1, mlat, mctx, lng, lnb, nct, alpha, n_exp):
    b, t, d = x1.shape
    nt = t // TM
    tile = lambda bi, i: (bi, i, 0)
    full = lambda bi, i: (0, 0)
    return pl.pallas_call(
        functools.partial(_combine_kernel, nct, alpha),
        grid=(b, nt),
        in_specs=[pl.BlockSpec((_sort_rows(n_exp), d // 2), lambda bi, i: (bi * nt + i, 0)),
                  pl.BlockSpec((1, TM, 8), tile),
                  pl.BlockSpec((1, TM, 8), tile),
                  pl.BlockSpec((1, TM, d), tile),
                  pl.BlockSpec((1, 1, N_MOD * d), lambda bi, i: (bi, 0, 0)),
                  pl.BlockSpec((1, N_MOD * d), full),
                  pl.BlockSpec((1, d), full),
                  pl.BlockSpec((1, d), full)],
        out_specs=pl.BlockSpec((1, TM, d), tile),
        out_shape=jax.ShapeDtypeStruct((b, t, d), F32),
        compiler_params=_cparams(("parallel", "parallel")),
        name="combine_post_norm",
    )(ys, pos_t, tg_t, x1, mlat, mctx, lng, lnb)


def _rope_tables(ctx_len, seq):
    t = np.arange(seq)
    row = (t // GRID_W).astype(np.float64)
    col = (t % GRID_W).astype(np.float64)
    nf = HEAD_DIM // 4
    inv = ROPE_THETA ** (-np.arange(nf, dtype=np.float64) / nf)
    ar = row[:, None] * inv[None, :]
    ac = col[:, None] * inv[None, :]
    ang = np.concatenate([ar, ar, ac, ac], axis=-1)
    cos = np.concatenate([np.ones((ctx_len, HEAD_DIM)), np.cos(ang)], axis=0)
    sin = np.concatenate([np.zeros((ctx_len, HEAD_DIM)), np.sin(ang)], axis=0)
    sign = np.where((np.arange(HEAD_DIM) % 32) < 16, -1.0, 1.0)[None, :]
    cos2 = np.tile(cos, (1, LANES // HEAD_DIM))
    sin2 = np.tile(sin * sign, (1, LANES // HEAD_DIM))
    return jnp.asarray(cos2, F32), jnp.asarray(sin2, F32)


def _channel_dft():
    c = np.arange(HEAD_DIM)
    ang = 2.0 * np.pi * ((c[:, None] * c[None, :]) % HEAD_DIM) / HEAD_DIM
    eye = np.eye(4)
    cs = np.concatenate([np.kron(eye, np.cos(ang)), np.kron(eye, np.sin(ang))], axis=1)
    return jnp.asarray(cs, BF16)


def _group_mean_matrix():
    bd = np.kron(np.eye(LANES // HEAD_DIM), np.full((HEAD_DIM, HEAD_DIM), 1.0 / HEAD_DIM))
    return jnp.asarray(bd, BF16)


def _dup_heads(wk, n_heads):
    d = wk.shape[0]
    return jnp.broadcast_to(wk.reshape(d, n_heads, 1, HEAD_DIM), (d, n_heads, 2, HEAD_DIM)).reshape(d, n_heads * LANES)


def _lambda_init(layer):
    return 0.8 - 0.6 * math.exp(-0.3 * layer)


def _moe_postnorm(routed, mlat, mctx, lng, lnb, nct, alpha, layer, w_gu, b_gu, w_down, b_down):
    x1, tg, n8, pos, xs, ys = routed
    n_exp = w_gu.shape[1]
    b, t, _ = x1.shape
    place = _placement(n8[:, :, 0], b * t * TOP_K)
    ys = _moe_experts(place, layer, xs, ys, w_gu, b_gu, w_down, b_down)
    pos_t, tg_t = jnp.transpose(pos, (0, 2, 1)), jnp.transpose(tg, (0, 2, 1))
    return _combine_postnorm(ys, pos_t, tg_t, x1, mlat, mctx, lng, lnb, nct, alpha, n_exp)


def kernel(x, c, ctx, c_ctx, mod_w, mod_b, ln_g, ln_b, ab_w_in, ab_sink, ab_w_out,
           cd_w_in, cd_lambda, cd_subln_g, cd_q_norm_g, cd_k_norm_g, cd_w_out,
           router_w, router_b, expert_w_gu, expert_b_gu, expert_w_down, expert_b_down):
    b, s, d = x.shape
    n_ctx = ctx.shape[1]
    depth = mod_w.shape[0]
    n_exp = router_w.shape[-1]
    assert d == 16 * HEAD_DIM and n_ctx % TM == 0 and s % TM == 0 and s % GRID_W == 0
    nct = n_ctx // TM
    alpha = (2 * depth) ** 0.25

    cos, sin = _rope_tables(n_ctx, s)
    cs_dft = _channel_dft()
    bd = _group_mean_matrix()
    mods = _mod_vectors(c, c_ctx, mod_w, mod_b)

    xs, xc = x, ctx
    for l in range(depth):
        last = l == depth - 1
        i = l // 2
        mlat = mods[l, :b].reshape(b, 1, N_MOD * d)
        mctx = mods[l, b:b + 1]
        lng1, lnb1 = ln_g[l, 0].reshape(1, d), ln_b[l, 0].reshape(1, d)
        lng2, lnb2 = ln_g[l, 1].reshape(1, d), ln_b[l, 1].reshape(1, d)
        rwt = router_w[l].T.astype(BF16)
        rb = router_b[l].reshape(n_exp, 1)
        if l % 2 == 0:
            w = ab_w_in[i]
            w_n = jnp.concatenate([w[:, :1024], _dup_heads(w[:, 1024:1280], 4)], axis=1).astype(BF16)
            wvt = w[:, 1280:1536].T.astype(BF16)
            uw, q, k, vt = _proj_ab(xs, xc, mlat, mctx, w_n, wvt, cos, sin, cs_dft, nct)
            oa = jnp.concatenate([_fourier(uw[:, :n_ctx]), _fourier(uw[:, n_ctx:])], axis=1)
            ob = _win_attn(ab_sink[i], q, k, vt, nct)
            if last:
                parts = [oa[:, n_ctx:], ob[:, n_ctx:]]
            else:
                parts = [oa, ob]
            w_out = ab_w_out[i].astype(BF16)
        else:
            w = cd_w_in[i]
            w_n = jnp.concatenate([w[:, :1536], _dup_heads(w[:, 1536:1664], 2)], axis=1).astype(BF16)
            wvt = w[:, 1664:2304].T.astype(BF16)
            gq = jnp.tile(cd_q_norm_g[i], 2).reshape(1, LANES)
            gk = jnp.tile(cd_k_norm_g[i], 2).reshape(1, LANES)
            qc, qd, kc, kd, vt = _proj_cd(xs, xc, mlat, mctx, w_n, wvt, cos, sin, bd, gq, gk, nct)
            sg = cd_subln_g[i].reshape(LANES, 1)
            o_lat = _attn_cd(cd_lambda[i], sg, qc, qd, kc, kd, vt, nct, _lambda_init(l))
            if last:
                parts = [o_lat]
            else:
                raise NotImplementedError("context outputs of a differential/axial layer")
            w_out = cd_w_out[i].astype(BF16)
        experts = (l, expert_w_gu, expert_b_gu, expert_w_down, expert_b_down)
        if last:
            routed = _oproj(parts, w_out, xs, xc, mlat, mctx, lng1, lnb1, rwt, rb, 0, nct, alpha)
            return _moe_postnorm(routed, mlat, mctx, lng2, lnb2, 0, alpha, *experts)
        routed = _oproj(parts, w_out, xs, xc, mlat, mctx, lng1, lnb1, rwt, rb, nct, 0, alpha)
        xs, xc = _moe_postnorm(routed, mlat, mctx, lng2, lnb2, nct, alpha, *experts), None
    return xs[:, n_ctx:]
```

```python
import functools
import math

import jax
import jax.numpy as jnp
import numpy as np
from jax import lax
from jax.experimental import pallas as pl
from jax.experimental.pallas import tpu as pltpu

F32 = jnp.float32
BF16 = jnp.bfloat16

HEAD_DIM = 64
GRID_W = 64
WINDOW = 128
ROPE_THETA = 10000.0
LN_EPS = 1e-6
RMS_EPS = 1e-6
N_MOD = 6
TOP_K = 4
SWIGLU_LIMIT = 7.0
SWIGLU_ALPHA = 1.702
NEG = -1e30
LOG2E = 1.4426950408889634
QSCALE = HEAD_DIM ** -0.5 * LOG2E

N_C_HEADS = 4
N_D_HEADS = 8
N_D_KV = 2
N_UNITS = 2 * N_C_HEADS + N_D_HEADS
ONES_ROWS = 16
_VT_LAYOUT = tuple((h * (2 * HEAD_DIM + ONES_ROWS), 2 * HEAD_DIM) for h in range(N_C_HEADS)) + tuple(
    (N_C_HEADS * (2 * HEAD_DIM + ONES_ROWS) + g * (HEAD_DIM + ONES_ROWS), HEAD_DIM) for g in range(N_D_KV))
VT_ROWS = _VT_LAYOUT[-1][0] + HEAD_DIM + ONES_ROWS

LANES = 128
TM = 256
MOE_BM = 512
ROW_GROUP = 8
SMEM_BLOCK = 1024
VMEM_LIMIT = 56 * 1024 * 1024


def _dot(a, b):
    return jnp.dot(a, b, preferred_element_type=F32)


def _dot_nt(a, b):
    return lax.dot_general(a, b, (((1,), (1,)), ((), ())), preferred_element_type=F32)


def _cparams(sem):
    return pltpu.CompilerParams(dimension_semantics=sem, vmem_limit_bytes=VMEM_LIMIT)


def _layernorm(z, g, b):
    mu = jnp.mean(z, axis=-1, keepdims=True)
    d = z - mu
    var = jnp.mean(d * d, axis=-1, keepdims=True)
    return d * lax.rsqrt(var + LN_EPS) * g + b


def _rope128(x, cos, sin_signed):
    lane = lax.broadcasted_iota(jnp.int32, x.shape, 1)
    lo = (lane % 32) < 16
    rot = jnp.where(lo, pltpu.roll(x, LANES - 16, 1), pltpu.roll(x, 16, 1))
    return x * cos + rot * sin_signed


def _rmsnorm128(x, g, bd):
    x2 = x * x
    hi = x2.astype(BF16)
    lo = (x2 - hi.astype(F32)).astype(BF16)
    ms = _dot(hi, bd) + _dot(lo, bd)
    return x * lax.rsqrt(ms + RMS_EPS) * g


U32 = jnp.uint32
_HI16 = 0xFFFF0000


def _pack_bf16_pairs(x):
    n = x.shape[1] // 2
    bits = lax.bitcast_convert_type(x, U32)
    return (bits[:, :n] >> 16) | (bits[:, n:] & U32(_HI16))


def _unpack_bf16_pairs(p):
    lo = lax.bitcast_convert_type(p << 16, F32).astype(BF16)
    hi = lax.bitcast_convert_type(p & U32(_HI16), F32).astype(BF16)
    return lo, hi


def _modulated(x_ref, ctx_ref, mlat_ref, mctx_ref, is_ctx, d):
    m = jnp.where(is_ctx, mctx_ref[...], mlat_ref[0])
    x = jnp.where(is_ctx, ctx_ref[0], x_ref[0])
    return (x * (1.0 + m[:, d:2 * d]) + m[:, 0:d]).astype(BF16)


def _stream_specs(x, ctx, nct, d, off=0, sub=0, n_sub=1):
    last = x.shape[1] // TM - 1
    tile = lambda j: n_sub * j + sub
    if ctx is None:
        return (x, x), [pl.BlockSpec((1, TM, d), lambda bi, j: (bi, jnp.minimum(tile(j) + off, last), 0)),
                        pl.BlockSpec((1, TM, d), lambda bi, j: (bi, jnp.minimum(tile(j), max(nct - 1, 0)), 0))]
    return (x, ctx), [pl.BlockSpec((1, TM, d), lambda bi, j: (bi, jnp.clip(tile(j) - nct, 0, last), 0)),
                      pl.BlockSpec((1, TM, d), lambda bi, j: (bi, jnp.minimum(tile(j), nct - 1), 0))]


def _mod_kernel(c_ref, w_ref, b_ref, o_ref):
    c = c_ref[...]
    s = (c / (1.0 + jnp.exp(-c))).astype(BF16)
    o_ref[0] = _dot(s, w_ref[0].astype(BF16)) + b_ref[0]


def _mod_vectors(c, c_ctx, mod_w, mod_b):
    depth, d, n = mod_w.shape
    b = c.shape[0]
    rows = 8 * (-(-(b + 1) // 8))
    cs = jnp.zeros((rows, d), F32).at[:b].set(c).at[b].set(c_ctx)
    tn = 1536
    out = pl.pallas_call(
        _mod_kernel,
        grid=(depth, n // tn),
        in_specs=[pl.BlockSpec((rows, d), lambda l, j: (0, 0)),
                  pl.BlockSpec((1, d, tn), lambda l, j: (l, 0, j)),
                  pl.BlockSpec((1, 1, tn), lambda l, j: (l, 0, j))],
        out_specs=pl.BlockSpec((1, rows, tn), lambda l, j: (l, 0, j)),
        out_shape=jax.ShapeDtypeStruct((depth, rows, n), F32),
        compiler_params=_cparams(("parallel", "parallel")),
        name="mod_vectors",
    )(cs, mod_w, mod_b.reshape(depth, 1, n))
    return out


def _proj_ab_kernel(nct, x_ref, ctx_ref, mlat_ref, mctx_ref, w_ref, wvt_ref, cos_ref, sin_ref, cs_ref,
                    uw_ref, q_ref, k_ref, vt_ref):
    is_ctx = pl.program_id(1) < nct
    d = x_ref.shape[-1]
    h = _modulated(x_ref, ctx_ref, mlat_ref, mctx_ref, is_ctx, d)
    p = _dot(h, w_ref[...])
    uw_ref[0] = _dot(p[:, 0:256].astype(BF16), cs_ref[...]).astype(BF16)
    cos = cos_ref[...]
    sin = sin_ref[...]
    for j in range(6):
        c0 = 256 + j * LANES
        q_ref[0, :, j * LANES:(j + 1) * LANES] = (_rope128(p[:, c0:c0 + LANES], cos, sin) * QSCALE).astype(BF16)
    for j in range(4):
        c0 = 1024 + j * LANES
        k_ref[0, :, j * LANES:(j + 1) * LANES] = _rope128(p[:, c0:c0 + LANES], cos, sin).astype(BF16)
    vt = _dot_nt(wvt_ref[...], h).astype(BF16)
    for c in range(TM // LANES):
        vt_ref[0, c] = vt[:, c * LANES:(c + 1) * LANES]


def _proj_ab(x, ctx, mlat, mctx, w, wvt, cos, sin, cs, nct):
    b, d = x.shape[0], x.shape[2]
    tall = x.shape[1] + (0 if ctx is None else ctx.shape[1])
    stream, stream_specs = _stream_specs(x, ctx, nct, d)
    nt = tall // TM
    nw = w.shape[1]
    nchunk = tall // LANES
    cpt = TM // LANES
    return pl.pallas_call(
        functools.partial(_proj_ab_kernel, nct),
        grid=(b, nt),
        in_specs=stream_specs + [
                  pl.BlockSpec((1, 1, N_MOD * d), lambda bi, i: (bi, 0, 0)),
                  pl.BlockSpec((1, N_MOD * d), lambda bi, i: (0, 0)),
                  pl.BlockSpec((d, nw), lambda bi, i: (0, 0)),
                  pl.BlockSpec((256, d), lambda bi, i: (0, 0)),
                  pl.BlockSpec((TM, LANES), lambda bi, i: (i, 0)),
                  pl.BlockSpec((TM, LANES), lambda bi, i: (i, 0)),
                  pl.BlockSpec((256, 512), lambda bi, i: (0, 0))],
        out_specs=[pl.BlockSpec((1, TM, 512), lambda bi, i: (bi, i, 0)),
                   pl.BlockSpec((1, TM, 768), lambda bi, i: (bi, i, 0)),
                   pl.BlockSpec((1, TM, 512), lambda bi, i: (bi, i, 0)),
                   pl.BlockSpec((1, cpt, 256, LANES), lambda bi, i: (bi, i, 0, 0))],
        out_shape=[jax.ShapeDtypeStruct((b, tall, 512), BF16),
                   jax.ShapeDtypeStruct((b, tall, 768), BF16),
                   jax.ShapeDtypeStruct((b, tall, 512), BF16),
                   jax.ShapeDtypeStruct((b, nchunk, 256, LANES), BF16)],
        compiler_params=_cparams(("parallel", "parallel")),
        name="proj_ab",
    )(*stream, mlat, mctx, w, wvt, cos, sin, cs)


def _proj_cd_kernel(nct, x_ref, ctx_ref, mlat_ref, mctx_ref, w_ref, wvt_ref, cos_ref, sin_ref, bd_ref,
                    gq_ref, gk_ref, qc_ref, qd_ref, kc_ref, kd_ref, vt_ref):
    is_ctx = pl.program_id(1) < nct
    d = x_ref.shape[-1]
    h = _modulated(x_ref, ctx_ref, mlat_ref, mctx_ref, is_ctx, d)
    p = _dot(h, w_ref[...])
    cos = cos_ref[...]
    sin = sin_ref[...]
    bd = bd_ref[...]
    gq = gq_ref[...]
    gk = gk_ref[...]
    for j in range(4):
        sl = slice(j * LANES, (j + 1) * LANES)
        qc_ref[0, :, sl] = (_rope128(p[:, j * LANES:(j + 1) * LANES], cos, sin) * QSCALE).astype(BF16)
        c0 = 512 + j * LANES
        qd_ref[0, :, sl] = (_rope128(_rmsnorm128(p[:, c0:c0 + LANES], gq, bd), cos, sin) * QSCALE).astype(BF16)
        c0 = 1024 + j * LANES
        kc_ref[0, :, sl] = _rope128(p[:, c0:c0 + LANES], cos, sin).astype(BF16)
    for j in range(2):
        c0 = 1536 + j * LANES
        kd_ref[0, :, j * LANES:(j + 1) * LANES] = _rope128(
            _rmsnorm128(p[:, c0:c0 + LANES], gk, bd), cos, sin).astype(BF16)
    vt = _dot_nt(wvt_ref[...], h).astype(BF16)
    ones = jnp.ones((ONES_ROWS, vt.shape[1]), BF16)
    src = 0
    for dst, dv in _VT_LAYOUT:
        vt_ref[0, 0, dst:dst + dv, :] = vt[src:src + dv, :]
        vt_ref[0, 0, dst + dv:dst + dv + ONES_ROWS, :] = ones
        src += dv


def _proj_cd(x, ctx, mlat, mctx, w, wvt, cos, sin, bd, gq, gk, nct):
    b, d = x.shape[0], x.shape[2]
    tall = x.shape[1] + (0 if ctx is None else ctx.shape[1])
    stream, stream_specs = _stream_specs(x, ctx, nct, d)
    nt = tall // TM
    nw = w.shape[1]
    nv = VT_ROWS
    full = lambda bi, i: (0, 0)
    tile = lambda bi, i: (bi, i, 0)
    return pl.pallas_call(
        functools.partial(_proj_cd_kernel, nct),
        grid=(b, nt),
        in_specs=stream_specs + [
                  pl.BlockSpec((1, 1, N_MOD * d), lambda bi, i: (bi, 0, 0)),
                  pl.BlockSpec((1, N_MOD * d), full),
                  pl.BlockSpec((d, nw), full),
                  pl.BlockSpec(wvt.shape, full),
                  pl.BlockSpec((TM, LANES), lambda bi, i: (i, 0)),
                  pl.BlockSpec((TM, LANES), lambda bi, i: (i, 0)),
                  pl.BlockSpec((LANES, LANES), full),
                  pl.BlockSpec((1, LANES), full),
                  pl.BlockSpec((1, LANES), full)],
        out_specs=[pl.BlockSpec((1, TM, 512), tile),
                   pl.BlockSpec((1, TM, 512), tile),
                   pl.BlockSpec((1, TM, 512), tile),
                   pl.BlockSpec((1, TM, 256), tile),
                   pl.BlockSpec((1, 1, nv, TM), lambda bi, i: (bi, i, 0, 0))],
        out_shape=[jax.ShapeDtypeStruct((b, tall, 512), BF16),
                   jax.ShapeDtypeStruct((b, tall, 512), BF16),
                   jax.ShapeDtypeStruct((b, tall, 512), BF16),
                   jax.ShapeDtypeStruct((b, tall, 256), BF16),
                   jax.ShapeDtypeStruct((b, nt, nv, TM), BF16)],
        compiler_params=_cparams(("parallel", "parallel")),
        name="proj_cd",
    )(*stream, mlat, mctx, w, wvt, cos, sin, bd, gq, gk)


def _fourier_kernel(nb, scale, uw_ref, ca_ref, sa_ref, cb_ref, sb_ref, o_ref):
    j = pl.program_id(0)
    ca = ca_ref[pl.ds(j, 1), :]
    sa = sa_ref[pl.ds(j, 1), :]
    cb = cb_ref[...]
    sb = sb_ref[...]
    ct = (ca * cb - sa * sb).astype(BF16)
    nst = (-(sa * cb + ca * sb)).astype(BF16)
    for bi in range(nb):
        acc = _dot(ct, uw_ref[bi, :, 0:256]) + _dot(nst, uw_ref[bi, :, 256:512])
        o_ref[bi] = (acc * scale).astype(BF16)


def _dft_tables(t, tmf):
    k = np.arange(t, dtype=np.int64)
    j1 = np.arange(t // tmf, dtype=np.int64) * tmf
    j0 = np.arange(tmf, dtype=np.int64)
    aa = (2.0 * np.pi / t) * ((j1[:, None] * k[None, :]) % t)
    ab = (2.0 * np.pi / t) * ((j0[:, None] * k[None, :]) % t)
    f = lambda a: jnp.asarray(a, F32)
    return f(np.cos(aa)), f(np.sin(aa)), f(np.cos(ab)), f(np.sin(ab))


def _fourier(uw):
    b, t, _ = uw.shape
    tmf = min(128, t)
    ca, sa, cb, sb = _dft_tables(t, tmf)
    scale = 1.0 / math.sqrt(t * HEAD_DIM)
    full2 = lambda j: (0, 0)
    return pl.pallas_call(
        functools.partial(_fourier_kernel, b, scale),
        grid=(t // tmf,),
        in_specs=[pl.BlockSpec((b, t, 512), lambda j: (0, 0, 0)),
                  pl.BlockSpec((t // tmf, t), full2),
                  pl.BlockSpec((t // tmf, t), full2),
                  pl.BlockSpec((tmf, t), full2),
                  pl.BlockSpec((tmf, t), full2)],
        out_specs=pl.BlockSpec((b, tmf, 256), lambda j: (0, j, 0)),
        out_shape=jax.ShapeDtypeStruct((b, t, 256), BF16),
        compiler_params=_cparams(("parallel",)),
        name="fourier_mix",
    )(uw, ca, sa, cb, sb)


def _win_attn_kernel(nct, nchunk, sink_ref, q_ref, k_ref, vt_ref, o_ref, s_a, s_b):
    i = pl.program_id(1)
    is_ctx = i < nct
    j = i - nct
    ctx_rows = nct * TM
    ctx_chunks = ctx_rows // LANES
    lw = TM + 2 * WINDOW
    lchunks = lw // LANES
    cs = jnp.clip(ctx_chunks + (TM // LANES) * j - WINDOW // LANES, 0, nchunk - lchunks)
    rs = pl.multiple_of(cs * LANES, LANES)
    qpos = j * TM + lax.broadcasted_iota(jnp.int32, (1, TM), 1)
    kpos = cs * LANES - ctx_rows + lax.broadcasted_iota(jnp.int32, (lw, 1), 0)
    allowed = (jnp.abs(qpos - kpos) <= WINDOW) & (kpos >= 0) & jnp.logical_not(is_ctx)
    lane = lax.broadcasted_iota(jnp.int32, (TM, LANES), 1)
    n_heads = q_ref.shape[-1] // HEAD_DIM
    group = n_heads // (k_ref.shape[-1] // LANES)

    def scores(hq, dst_s):
        pair, half = divmod(hq, 2)
        g = hq // group
        qp = q_ref[0, :, pair * LANES:(pair + 1) * LANES]
        qm = jnp.where((lane >= HEAD_DIM) == (half == 1), qp, jnp.zeros_like(qp))
        dst_s[0:ctx_rows, :] = _dot_nt(k_ref[0, 0:ctx_rows, g * LANES:(g + 1) * LANES], qm)
        s_l = _dot_nt(k_ref[0, pl.ds(rs, lw), g * LANES:(g + 1) * LANES], qm)
        dst_s[ctx_rows:ctx_rows + lw, :] = jnp.where(allowed, s_l, NEG)

    def consume(hq, src_s):
        g = hq // group
        s = src_s[...]
        sk = sink_ref[hq] * LOG2E
        m = jnp.maximum(jnp.max(s, axis=0, keepdims=True), sk)
        p = jnp.exp2(s - m)
        l = jnp.sum(p, axis=0, keepdims=True) + jnp.exp2(sk - m)
        p = p.astype(BF16)
        vrows = slice(g * HEAD_DIM, (g + 1) * HEAD_DIM)
        acc = jnp.zeros((HEAD_DIM, TM), F32)
        for c in range(ctx_chunks):
            acc = acc + _dot(vt_ref[0, c, vrows, :], p[c * LANES:(c + 1) * LANES, :])
        for c in range(lchunks):
            r0 = ctx_rows + c * LANES
            acc = acc + _dot(vt_ref[0, cs + c, vrows, :], p[r0:r0 + LANES, :])
        return acc / l

    bufs = (s_a, s_b)
    scores(0, s_a)
    outs = []
    for hq in range(n_heads):
        if hq + 1 < n_heads:
            scores(hq + 1, bufs[(hq + 1) % 2])
        outs.append(consume(hq, bufs[hq % 2]))
        if hq % 2 == 1:
            pair = hq // 2
            o2 = jnp.concatenate(outs, axis=0)
            o_ref[0, :, pair * LANES:(pair + 1) * LANES] = o2.T.astype(BF16)
            outs = []


def _win_attn(sink, q, k, vt, nct):
    b, tall, qw = q.shape
    nt = tall // TM
    nchunk = vt.shape[1]
    return pl.pallas_call(
        functools.partial(_win_attn_kernel, nct, nchunk),
        grid=(b, nt),
        in_specs=[pl.BlockSpec(memory_space=pltpu.SMEM),
                  pl.BlockSpec((1, TM, qw), lambda bi, i: (bi, i, 0)),
                  pl.BlockSpec((1, tall, k.shape[-1]), lambda bi, i: (bi, 0, 0)),
                  pl.BlockSpec((1, nchunk, vt.shape[2], LANES), lambda bi, i: (bi, 0, 0, 0))],
        out_specs=pl.BlockSpec((1, TM, qw), lambda bi, i: (bi, i, 0)),
        out_shape=jax.ShapeDtypeStruct((b, tall, qw), BF16),
        scratch_shapes=[pltpu.VMEM((nct * TM + TM + 2 * WINDOW, TM), F32)] * 2,
        compiler_params=_cparams(("parallel", "parallel")),
        name="window_attention",
    )(sink, q, k, vt)


def _unit_operands(u):
    if u < 2 * N_C_HEADS:
        hc = u // 2
        dst, dv = _VT_LAYOUT[hc]
        return True, hc, dst, dv
    g = (u - 2 * N_C_HEADS) // (N_D_HEADS // N_D_KV)
    dst, dv = _VT_LAYOUT[N_C_HEADS + g]
    return False, g, dst, dv


def _attn_cd_kernel(nkt, lam_init, lam_ref, sg_ref, qc_ref, qd_ref, kc_ref, kd_ref, vt_ref, o_ref,
                    qm_s, m_s, acc_s, s_a, s_b):
    mq = qc_ref.shape[1]
    tk = vt_ref.shape[-1]
    lane = lax.broadcasted_iota(jnp.int32, (mq, LANES), 1)
    upper = lane >= HEAD_DIM
    for u in range(N_UNITS):
        src = qc_ref if u < 2 * N_C_HEADS else qd_ref
        pair = (u if u < 2 * N_C_HEADS else u - 2 * N_C_HEADS) // 2
        qp = src[0, :, pair * LANES:(pair + 1) * LANES]
        qm_s[u] = jnp.where(upper == (u % 2 == 1), qp, jnp.zeros_like(qp))
    m_s[...] = jnp.full(m_s.shape, NEG, F32)
    acc_s[...] = jnp.zeros(acc_s.shape, F32)

    def scores(tile, u, dst_s):
        is_c, slab, _, _ = _unit_operands(u)
        k_ref = kc_ref if is_c else kd_ref
        r0 = pl.multiple_of(tile * tk, tk)
        dst_s[u] = _dot_nt(k_ref[0, pl.ds(r0, tk), slab * LANES:(slab + 1) * LANES], qm_s[u])

    def consume(tile, u, src_s):
        _, _, row0, dv = _unit_operands(u)
        rows = dv + ONES_ROWS
        m_old = m_s[u]
        m_new = jnp.maximum(m_old, jnp.max(src_s[u], axis=0, keepdims=True))
        alpha = jnp.exp2(m_old - m_new)
        p = jnp.exp2(src_s[u] - m_new).astype(BF16)
        pv = _dot(vt_ref[0, tile, row0:row0 + rows, :], p)
        acc_s[u, 0:rows, :] = acc_s[u, 0:rows, :] * alpha + pv
        m_s[u] = m_new

    def stage(tile, src_s, dst_s):
        nxt = jnp.minimum(tile + 1, nkt - 1)
        for u in range(N_UNITS):
            scores(nxt, u, dst_s)
            consume(tile, u, src_s)

    for u in range(N_UNITS):
        scores(0, u, s_a)
    stage(0, s_a, s_b)

    bufs = (s_a, s_b)
    per_trip = 8

    def body(trip, carry):
        for j in range(per_trip):
            tile = per_trip * trip + 1 + j
            stage(tile, bufs[(1 + j) % 2], bufs[j % 2])
        return carry

    n_trips = (nkt - 1) // per_trip
    lax.fori_loop(0, n_trips, body, 0)
    for tile in range(1 + n_trips * per_trip, nkt):
        stage(tile, bufs[tile % 2], bufs[(tile + 1) % 2])

    lv = lam_ref[...]
    lam = (jnp.exp(jnp.sum(lv[0:1] * lv[1:2], axis=-1, keepdims=True))
           - jnp.exp(jnp.sum(lv[2:3] * lv[3:4], axis=-1, keepdims=True)) + lam_init)
    sg = sg_ref[...]
    dvc = 2 * HEAD_DIM
    for hc in range(N_C_HEADS):
        o1 = acc_s[2 * hc, 0:dvc, :] / acc_s[2 * hc, dvc:dvc + 1, :]
        o2 = acc_s[2 * hc + 1, 0:dvc, :] / acc_s[2 * hc + 1, dvc:dvc + 1, :]
        o = o1 - lam * o2
        ms = jnp.mean(o * o, axis=0, keepdims=True)
        o = o * lax.rsqrt(ms + RMS_EPS) * sg * (1.0 - lam_init)
        o_ref[0, :, hc * LANES:(hc + 1) * LANES] = o.T.astype(BF16)
    for pair in range(N_D_HEADS // 2):
        outs = []
        for half in range(2):
            u = 2 * N_C_HEADS + 2 * pair + half
            outs.append(acc_s[u, 0:HEAD_DIM, :] / acc_s[u, HEAD_DIM:HEAD_DIM + 1, :])
        o2 = jnp.concatenate(outs, axis=0)
        c0 = N_C_HEADS * LANES + pair * LANES
        o_ref[0, :, c0:c0 + LANES] = o2.T.astype(BF16)


def _attn_cd(lam_vec, sg, qc, qd, kc, kd, vt, nct, lam_init):
    b, tall, _ = qc.shape
    nkt, nv, tk = vt.shape[1], vt.shape[2], vt.shape[3]
    mq = TM
    nq = tall // mq - nct
    ow = N_C_HEADS * LANES + N_D_HEADS * HEAD_DIM
    qtile = lambda bi, i: (bi, i + nct, 0)
    return pl.pallas_call(
        functools.partial(_attn_cd_kernel, nkt, lam_init),
        grid=(b, nq),
        in_specs=[pl.BlockSpec((4, HEAD_DIM), lambda bi, i: (0, 0)),
                  pl.BlockSpec((LANES, 1), lambda bi, i: (0, 0)),
                  pl.BlockSpec((1, mq, qc.shape[-1]), qtile),
                  pl.BlockSpec((1, mq, qd.shape[-1]), qtile),
                  pl.BlockSpec((1, tall, kc.shape[-1]), lambda bi, i: (bi, 0, 0)),
                  pl.BlockSpec((1, tall, kd.shape[-1]), lambda bi, i: (bi, 0, 0)),
                  pl.BlockSpec((1, nkt, nv, tk), lambda bi, i: (bi, 0, 0, 0))],
        out_specs=pl.BlockSpec((1, mq, ow), lambda bi, i: (bi, i, 0)),
        out_shape=jax.ShapeDtypeStruct((b, nq * mq, ow), BF16),
        scratch_shapes=[pltpu.VMEM((N_UNITS, mq, LANES), BF16),
                        pltpu.VMEM((N_UNITS, 1, mq), F32),
                        pltpu.VMEM((N_UNITS, 2 * HEAD_DIM + ONES_ROWS, mq), F32),
                        pltpu.VMEM((N_UNITS, tk, mq), F32),
                        pltpu.VMEM((N_UNITS, tk, mq), F32)],
        compiler_params=_cparams(("parallel", "parallel")),
        name="attention_cd",
    )(lam_vec, sg, qc, qd, kc, kd, vt)


OPROJ_SUB = 2


def _oproj_kernel(n_parts, widths, nct, nt, alpha, n_exp, *refs):
    o_refs = refs[:n_parts]
    w_ref = refs[n_parts]
    stream = refs[n_parts + 1:n_parts + 1 + 2 * OPROJ_SUB]
    (mlat_ref, mctx_ref, lng_ref, lnb_ref, rwt_ref, rb_ref, tri_ref, ltri_ref,
     x1_ref, tg_ref, n8_ref, pos_ref, xs_ref, ys_hbm, zero_s, sem) = refs[n_parts + 1 + 2 * OPROJ_SUB:]
    d = x1_ref.shape[-1]
    tm = TM
    sr = xs_ref.shape[1] // OPROJ_SUB
    bi = pl.program_id(0)
    tiles = [OPROJ_SUB * pl.program_id(1) + s for s in range(OPROJ_SUB)]
    zero_s[...] = jnp.zeros(zero_s.shape, zero_s.dtype)
    clears = []
    for s, i in enumerate(tiles):
        tail0 = pl.multiple_of((bi * nt + i) * sr + TOP_K * tm, ROW_GROUP)
        clears.append(pltpu.make_async_copy(zero_s, ys_hbm.at[pl.ds(tail0, zero_s.shape[0])], sem.at[s]))
        pl.when(i < nt)(clears[s].start)
    state = [dict() for _ in tiles]

    def project(s, i):
        st, rows = state[s], slice(s * tm, (s + 1) * tm)
        is_ctx = i < nct
        y = None
        r0 = 0
        for o_ref, wd in zip(o_refs, widths):
            part = _dot(o_ref[0, rows, :], w_ref[r0:r0 + wd, :])
            y = part if y is None else y + part
            r0 += wd
        st["m"] = jnp.where(is_ctx, mctx_ref[...], mlat_ref[0])
        st["z"] = alpha * jnp.where(is_ctx, stream[2 * s + 1][0], stream[2 * s][0]) + st["m"][:, 2 * d:3 * d] * y

    def normalise(s, i):
        st, rows = state[s], slice(s * tm, (s + 1) * tm)
        m = st["m"]
        x1 = _layernorm(st["z"], lng_ref[...], lnb_ref[...])
        x1_ref[0, rows, :] = x1
        st["hb"] = (x1 * (1.0 + m[:, 4 * d:5 * d]) + m[:, 3 * d:4 * d]).astype(BF16)

    def route(s, i):
        st, rows = state[s], slice(s * tm, (s + 1) * tm)
        logits = _dot_nt(rwt_ref[...], st["hb"]) + rb_ref[...]
        row = lax.broadcasted_iota(jnp.int32, logits.shape, 0)
        vals, idxs = [], []
        for _ in range(TOP_K):
            mx = jnp.max(logits, axis=0, keepdims=True)
            idx = jnp.min(jnp.where(logits == mx, row, n_exp), axis=0, keepdims=True)
            vals.append(mx)
            idxs.append(idx)
            logits = jnp.where(row == idx, NEG, logits)
        es = [jnp.exp(v - vals[0]) for v in vals]
        tot = es[0] + es[1] + es[2] + es[3]
        pad_i = jnp.zeros((8 - TOP_K, tm), jnp.int32)
        pad_f = jnp.zeros((8 - TOP_K, tm), F32)
        te = jnp.concatenate(idxs + [pad_i], axis=0)
        tg_ref[0, :, rows] = jnp.concatenate([e / tot for e in es] + [pad_f], axis=0)
        _, oh = _expert_onehot(te, n_exp)
        n8 = _groups(jnp.sum(oh, axis=1, keepdims=True))
        n8_ref[0, s] = jnp.broadcast_to(n8, (n_exp, LANES)).astype(jnp.int32)
        lbase = _dot(ltri_ref[...], jnp.broadcast_to(n8, (n_exp, LANES)).astype(BF16))[:, 0:1]
        before = _dot(oh.astype(BF16), tri_ref[...])
        where_to = ROW_GROUP * lbase + before
        pos = [jnp.sum(jnp.where(row == te[k:k + 1, :], where_to, 0.0), axis=0, keepdims=True).astype(jnp.int32)
               for k in range(TOP_K)]
        pos_ref[0, :, rows] = jnp.concatenate(pos + [pad_i], axis=0)
        st["pos"] = pos

    def sort_chunk(s, c):
        st = state[s]
        r = lax.broadcasted_iota(jnp.int32, (tm, tm), 0) + c * tm
        hit = r == st["pos"][0]
        for k in range(1, TOP_K):
            hit = hit | (r == st["pos"][k])
        packed = _pack_bf16_pairs(_dot(jnp.where(hit, 1.0, 0.0).astype(BF16), st["hb"]))
        xs_ref[0, s * sr + c * tm:s * sr + (c + 1) * tm, :] = packed

    fronts = (project, normalise, route)
    n_chunks = sr // tm
    for f in fronts:
        f(0, tiles[0])
    for s in range(1, len(tiles) + 1):
        nxt = [functools.partial(f, s, tiles[s]) for f in fronts] if s < len(tiles) else []
        for c in range(n_chunks):
            sort_chunk(s - 1, c)
            if c < len(nxt):
                nxt[c]()
        for f in nxt[n_chunks:]:
            f()
    for s, i in enumerate(tiles):
        pl.when(i < nt)(clears[s].wait)


def _oproj(parts, w_out, x, ctx, mlat, mctx, lng, lnb, rwt, rb, nct_out, x_tile_off, alpha):
    b, t, _ = parts[0].shape
    d = x.shape[-1]
    nt = t // TM
    n_exp = rwt.shape[0]
    widths = tuple(p.shape[-1] for p in parts)
    tile = lambda bi, j: (bi, j, 0)
    full = lambda bi, j: (0, 0)
    ns = OPROJ_SUB
    stream, stream_specs = [], []
    for sub in range(ns):
        arrs, specs = _stream_specs(x, ctx, nct_out, d, x_tile_off, sub, ns)
        stream += arrs
        stream_specs += specs
    in_specs = [pl.BlockSpec((1, ns * TM, wd), tile) for wd in widths] + [
        pl.BlockSpec(w_out.shape, full)] + stream_specs + [
        pl.BlockSpec((1, 1, N_MOD * d), lambda bi, i: (bi, 0, 0)),
        pl.BlockSpec((1, N_MOD * d), full),
        pl.BlockSpec((1, d), full),
        pl.BlockSpec((1, d), full),
        pl.BlockSpec((n_exp, d), full),
        pl.BlockSpec((n_exp, 1), full),
        pl.BlockSpec((TM, TM), full),
        pl.BlockSpec((n_exp, n_exp), full)]
    sr = _sort_rows(n_exp)
    tri = jnp.asarray(np.triu(np.ones((TM, TM)), 1), BF16)
    ltri = jnp.asarray(np.tril(np.ones((n_exp, n_exp)), -1), BF16)
    small = pl.BlockSpec((1, 8, ns * TM), lambda bi, j: (bi, 0, j))
    x1, tg, n8, pos, xs, ys = pl.pallas_call(
        functools.partial(_oproj_kernel, len(parts), widths, nct_out, nt, alpha, n_exp),
        grid=(b, -(-nt // ns)),
        in_specs=in_specs,
        out_specs=[pl.BlockSpec((1, ns * TM, d), tile),
                   small,
                   pl.BlockSpec((1, ns, n_exp, LANES), lambda bi, j: (bi, j, 0, 0)),
                   small,
                   pl.BlockSpec((1, ns * sr, d // 2), tile),
                   pl.BlockSpec(memory_space=pl.ANY)],
        out_shape=[jax.ShapeDtypeStruct((b, t, d), F32),
                   jax.ShapeDtypeStruct((b, 8, t), F32),
                   jax.ShapeDtypeStruct((b, nt, n_exp, LANES), jnp.int32),
                   jax.ShapeDtypeStruct((b, 8, t), jnp.int32),
                   jax.ShapeDtypeStruct((b, nt * sr, d // 2), U32),
                   jax.ShapeDtypeStruct(((b * nt + 1) * sr, d // 2), U32)],
        scratch_shapes=[pltpu.VMEM((sr - TOP_K * TM, d // 2), U32), pltpu.SemaphoreType.DMA((ns,))],
        compiler_params=_cparams(("parallel", "parallel")),
        name="out_proj_norm_router",
    )(*parts, w_out, *stream, mlat, mctx, lng, lnb, rwt, rb, tri, ltri)
    return x1, tg, n8.reshape(b * nt, n_exp, LANES), pos, xs.reshape(b * nt * sr, d // 2), ys


def _moe_kernel(be_ref, nx_ref, bv_ref, nu_ref, rprev_ref, rcur_ref, rnext_ref, xs_hbm, wgu_hbm, bgu_ref,
                wd_hbm, bd_ref, ys_in_hbm, ys_hbm, wgu_f, wd_f, wgu_s, wd_s, xbuf, ybuf, sem, gsem, ssem):
    del ys_in_hbm
    blk = pl.program_id(0)
    ff = wd_s.shape[0]
    gpb = xbuf.shape[1] // ROW_GROUP
    par = blk % 2

    def gathers(rec, buf, wait):
        if wait:
            pltpu.make_async_copy(xs_hbm.at[pl.ds(0, xbuf.shape[1])], xbuf.at[buf], gsem.at[buf]).wait()
            return
        for g in range(gpb):
            _group_copy(xs_hbm, rec[g], xbuf.at[buf], g, gsem.at[buf]).start(priority=g % 2)

    def scatters(rec, buf, wait):
        if wait:
            pltpu.make_async_copy(ybuf.at[buf], ys_hbm.at[pl.ds(0, ybuf.shape[1])], ssem.at[buf]).wait()
            return
        for g in range(gpb):
            _group_copy(ybuf.at[buf], g, ys_hbm, rec[gpb + g], ssem.at[buf]).start(priority=g % 2)

    def fetch(e):
        return (pltpu.make_async_copy(wgu_hbm.at[e], wgu_f, sem.at[0]),
                pltpu.make_async_copy(wd_hbm.at[e], wd_f, sem.at[1]))

    @pl.when(blk < nu_ref[0])
    def _():
        e = be_ref[blk]
        new_expert = (blk == 0) | (e != be_ref[jnp.maximum(blk - 1, 0)])

        @pl.when(blk == 0)
        def _():
            for cp in fetch(e):
                cp.start()

        @pl.when(new_expert)
        def _():
            for cp in fetch(e):
                cp.wait()
            rows = 128

            def cast_gu(r, c):
                r0 = pl.multiple_of(r * rows, rows)
                wgu_s[pl.ds(r0, rows), :] = wgu_f[pl.ds(r0, rows), :].astype(BF16)
                return c

            def cast_d(r, c):
                r0 = pl.multiple_of(r * rows, rows)
                wd_s[pl.ds(r0, rows), :] = wd_f[pl.ds(r0, rows), :].astype(BF16)
                return c

            lax.fori_loop(0, wgu_s.shape[0] // rows, cast_gu, 0)
            lax.fori_loop(0, wd_s.shape[0] // rows, cast_d, 0)

            @pl.when(nx_ref[blk] != e)
            def _():
                for cp in fetch(nx_ref[blk]):
                    cp.start()

        @pl.when(blk == 0)
        def _():
            gathers(rcur_ref, 0, False)

        gathers(rcur_ref, par, True)
        half = xbuf.shape[1] // 2

        def ffn(rows):
            gathers(rnext_ref, 1 - par, False)
            x = jnp.concatenate(_unpack_bf16_pairs(xbuf[par, rows, :]), axis=1)
            gu = _dot(x, wgu_s[...]) + bgu_ref[0]
            g = jnp.minimum(gu[:, :ff], SWIGLU_LIMIT)
            u = jnp.clip(gu[:, ff:], -SWIGLU_LIMIT, SWIGLU_LIMIT)
            act = (u + 1.0) * (g / (1.0 + jnp.exp(-SWIGLU_ALPHA * g)))
            y = _dot(act.astype(BF16), wd_s[...]) + bd_ref[0]
            ybuf[par, rows, :] = _pack_bf16_pairs(y.astype(BF16).astype(F32))
            if rows != slice(None):
                ybuf[par, half:, :] = jnp.zeros((half, ybuf.shape[2]), ybuf.dtype)
            scatters(rcur_ref, par, False)

        @pl.when(bv_ref[blk] > half)
        def _():
            ffn(slice(None))

        @pl.when(bv_ref[blk] <= half)
        def _():
            ffn(slice(0, half))

        @pl.when(blk > 0)
        def _():
            scatters(rprev_ref, 1 - par, True)

        @pl.when(blk == nu_ref[0] - 1)
        def _():
            gathers(rnext_ref, 1 - par, True)
            scatters(rcur_ref, par, True)


def _moe_experts(place, layer, xs, ys, w_gu, b_gu, w_down, b_down):
    d = 2 * xs.shape[1]
    ff2 = w_gu.shape[-1]
    ff = w_down.shape[-2]
    off = layer * w_gu.shape[1]
    n_exp = w_gu.shape[0] * w_gu.shape[1]
    w_gu = w_gu.reshape(n_exp, d, ff2)
    w_down = w_down.reshape(n_exp, ff, d)
    nb = place["n_blocks"]
    last = lambda nu: nu[0] - 1
    bblk = lambda i, be, nx, bv, nu: (be[i], 0, 0)
    rec = lambda f: pl.BlockSpec((SMEM_BLOCK,), f, memory_space=pltpu.SMEM)
    grid_spec = pltpu.PrefetchScalarGridSpec(
        num_scalar_prefetch=4,
        grid=(nb,),
        in_specs=[rec(lambda i, be, nx, bv, nu: (jnp.clip(i - 1, 0, last(nu)),)),
                  rec(lambda i, be, nx, bv, nu: (jnp.minimum(i, last(nu)),)),
                  rec(lambda i, be, nx, bv, nu: (jnp.minimum(i + 1, last(nu)),)),
                  pl.BlockSpec(memory_space=pl.ANY),
                  pl.BlockSpec(memory_space=pl.ANY),
                  pl.BlockSpec((1, 1, ff2), bblk),
                  pl.BlockSpec(memory_space=pl.ANY),
                  pl.BlockSpec((1, 1, d), bblk),
                  pl.BlockSpec(memory_space=pl.ANY)],
        out_specs=pl.BlockSpec(memory_space=pl.ANY),
        scratch_shapes=[pltpu.VMEM((d, ff2), F32), pltpu.VMEM((ff, d), F32),
                        pltpu.VMEM((d, ff2), BF16), pltpu.VMEM((ff, d), BF16),
                        pltpu.VMEM((2, MOE_BM, d // 2), U32), pltpu.VMEM((2, MOE_BM, d // 2), U32),
                        pltpu.SemaphoreType.DMA((2,)), pltpu.SemaphoreType.DMA((2,)),
                        pltpu.SemaphoreType.DMA((2,))])
    records = place["rec"]
    return pl.pallas_call(
        _moe_kernel,
        grid_spec=grid_spec,
        out_shape=jax.ShapeDtypeStruct(ys.shape, ys.dtype),
        input_output_aliases={12: 0},
        compiler_params=_cparams(("arbitrary",)),
        name="moe_experts",
    )(place["block_e"] + off, place["block_next_e"] + off, place["block_rows"], place["n_used"],
      records, records, records, xs, w_gu, b_gu.reshape(n_exp, 1, ff2), w_down, b_down.reshape(n_exp, 1, d), ys)


def _expert_onehot(te, n_exp):
    row = lax.broadcasted_iota(jnp.int32, (n_exp, te.shape[1]), 0)
    oh = jnp.zeros(row.shape, F32)
    for k in range(TOP_K):
        oh = oh + (row == te[k:k + 1, :]).astype(F32)
    return row, oh


def _groups(cnt):
    return jnp.floor((cnt + (ROW_GROUP - 1)) / ROW_GROUP)


def _placement(n8, n_asg):
    n_tiles, n_exp = n8.shape
    gpt = _sort_rows(n_exp) // ROW_GROUP
    gpb = MOE_BM // ROW_GROUP
    assert (TOP_K * TM + n_exp * (ROW_GROUP - 1)) // ROW_GROUP < gpt
    ids = jnp.arange(n_exp, dtype=jnp.int32)
    lbase = jnp.cumsum(n8, axis=1) - n8
    cum_incl = jnp.cumsum(n8, axis=0)
    cum_excl = cum_incl - n8
    groups = cum_incl[-1]
    padded = (groups + gpb - 1) // gpb * gpb
    pend = jnp.cumsum(padded)
    pstart = pend - padded
    n_blocks = -(-(n_asg + n_tiles * n_exp * (ROW_GROUP - 1)) // MOE_BM) + n_exp
    n_used = (pend[-1] // gpb).astype(jnp.int32)
    blocks = jnp.arange(n_blocks, dtype=jnp.int32)
    blk = jnp.minimum(blocks, n_used - 1) * gpb
    block_e = jnp.minimum(jnp.sum((pend[None, :] <= blk[:, None]).astype(jnp.int32), axis=1), n_exp - 1)
    later = (padded[None, :] > 0) & (ids[None, :] > block_e[:, None])
    block_next_e = jnp.min(jnp.where(later, ids[None, :], n_exp), axis=1)
    block_next_e = jnp.where(block_next_e == n_exp, block_e, block_next_e).astype(jnp.int32)
    mine = (block_e[:, None] == ids[None, :]).astype(jnp.int32)
    pick = lambda table: jnp.sum(mine[:, :, None] * table.T[None, :, :], axis=1)
    b_start = jnp.sum(mine * pstart[None, :], axis=1)
    b_groups = jnp.sum(mine * groups[None, :], axis=1)
    j_e = (blocks * gpb - b_start)[:, None] + jnp.arange(gpb, dtype=jnp.int32)[None, :]
    valid = (j_e < b_groups[:, None]) & (blocks < n_used)[:, None]
    tile = jnp.sum((pick(cum_incl)[:, None, :] <= j_e[:, :, None]).astype(jnp.int32), axis=2)
    tile = jnp.minimum(tile, n_tiles - 1)
    in_tile = (tile[:, :, None] == jnp.arange(n_tiles, dtype=jnp.int32)[None, None, :]).astype(jnp.int32)
    shift = jnp.sum(in_tile * (pick(lbase) - pick(cum_excl))[:, None, :], axis=2)
    held = tile * gpt + shift + j_e
    pad_dst = n_tiles * gpt + jnp.arange(gpb, dtype=jnp.int32)[None, :]
    src = jnp.where(valid, held, gpt - 1)
    dst = jnp.where(valid, held, pad_dst)
    rec = jnp.concatenate([src, dst, jnp.zeros((n_blocks, SMEM_BLOCK - 2 * gpb), jnp.int32)], axis=1)
    left = b_groups - (blocks - b_start // gpb) * gpb
    block_rows = (jnp.clip(left, 0, gpb) * ROW_GROUP).astype(jnp.int32)
    return dict(rec=rec.reshape(-1), block_e=block_e, block_next_e=block_next_e, block_rows=block_rows,
                n_used=n_used.reshape(1), n_blocks=n_blocks)


def _group_copy(src_ref, src_group, dst_ref, dst_group, sem):
    s0 = pl.multiple_of(src_group * ROW_GROUP, ROW_GROUP)
    d0 = pl.multiple_of(dst_group * ROW_GROUP, ROW_GROUP)
    return pltpu.make_async_copy(src_ref.at[pl.ds(s0, ROW_GROUP)], dst_ref.at[pl.ds(d0, ROW_GROUP)], sem)


def _tile_step():
    return pl.program_id(0) * pl.num_programs(1) + pl.program_id(1)


def _sort_rows(n_exp):
    return -(-(TOP_K * TM + n_exp * ROW_GROUP) // TM) * TM


def _combine_kernel(nct, alpha, ys_ref, pos_ref, tg_ref, x1_ref, mlat_ref, mctx_ref, lng_ref, lnb_ref, o_ref):
    d = x1_ref.shape[-1]
    tm = x1_ref.shape[1]
    is_ctx = pl.program_id(1) < nct
    m = jnp.where(is_ctx, mctx_ref[...], mlat_ref[0])
    pos = pos_ref[0]
    gates = tg_ref[0]
    f_lo = f_hi = None
    for c in range(ys_ref.shape[0] // tm):
        col = lax.broadcasted_iota(jnp.int32, (tm, tm), 1) + c * tm
        w = jnp.where(col == pos[:, 0:1], gates[:, 0:1], 0.0)
        for k in range(1, TOP_K):
            w = w + jnp.where(col == pos[:, k:k + 1], gates[:, k:k + 1], 0.0)
        w = w.astype(BF16)
        y_lo, y_hi = _unpack_bf16_pairs(ys_ref[c * tm:(c + 1) * tm, :])
        f_lo = _dot(w, y_lo) if f_lo is None else f_lo + _dot(w, y_lo)
        f_hi = _dot(w, y_hi) if f_hi is None else f_hi + _dot(w, y_hi)
    f = jnp.concatenate([f_lo, f_hi], axis=1)
    o_ref[0] = _layernorm(alpha * x1_ref[0] + m[:, 5 * d:6 * d] * f, lng_ref[...], lnb_ref[...])


def _combine_postnorm(ys, pos_t, tg_t, x1, mlat, mctx, lng, lnb, nct, alpha, n_exp):
    b, t, d = x1.shape
    nt = t // TM
    tile = lambda bi, i: (bi, i, 0)
    full = lambda bi, i: (0, 0)
    return pl.pallas_call(
        functools.partial(_combine_kernel, nct, alpha),
        grid=(b, nt),
        in_specs=[pl.BlockSpec((_sort_rows(n_exp), d // 2), lambda bi, i: (bi * nt + i, 0)),
                  pl.BlockSpec((1, TM, 8), tile),
                  pl.BlockSpec((1, TM, 8), tile),
                  pl.BlockSpec((1, TM, d), tile),
                  pl.BlockSpec((1, 1, N_MOD * d), lambda bi, i: (bi, 0, 0)),
                  pl.BlockSpec((1, N_MOD * d), full),
                  pl.BlockSpec((1, d), full),
                  pl.BlockSpec((1, d), full)],
        out_specs=pl.BlockSpec((1, TM, d), tile),
        out_shape=jax.ShapeDtypeStruct((b, t, d), F32),
        compiler_params=_cparams(("parallel", "parallel")),
        name="combine_post_norm",
    )(ys, pos_t, tg_t, x1, mlat, mctx, lng, lnb)


def _rope_tables(ctx_len, seq):
    t = np.arange(seq)
    row = (t // GRID_W).astype(np.float64)
    col = (t % GRID_W).astype(np.float64)
    nf = HEAD_DIM // 4
    inv = ROPE_THETA ** (-np.arange(nf, dtype=np.float64) / nf)
    ar = row[:, None] * inv[None, :]
    ac = col[:, None] * inv[None, :]
    ang = np.concatenate([ar, ar, ac, ac], axis=-1)
    cos = np.concatenate([np.ones((ctx_len, HEAD_DIM)), np.cos(ang)], axis=0)
    sin = np.concatenate([np.zeros((ctx_len, HEAD_DIM)), np.sin(ang)], axis=0)
    sign = np.where((np.arange(HEAD_DIM) % 32) < 16, -1.0, 1.0)[None, :]
    cos2 = np.tile(cos, (1, LANES // HEAD_DIM))
    sin2 = np.tile(sin * sign, (1, LANES // HEAD_DIM))
    return jnp.asarray(cos2, F32), jnp.asarray(sin2, F32)


def _channel_dft():
    c = np.arange(HEAD_DIM)
    ang = 2.0 * np.pi * ((c[:, None] * c[None, :]) % HEAD_DIM) / HEAD_DIM
    eye = np.eye(4)
    cs = np.concatenate([np.kron(eye, np.cos(ang)), np.kron(eye, np.sin(ang))], axis=1)
    return jnp.asarray(cs, BF16)


def _group_mean_matrix():
    bd = np.kron(np.eye(LANES // HEAD_DIM), np.full((HEAD_DIM, HEAD_DIM), 1.0 / HEAD_DIM))
    return jnp.asarray(bd, BF16)


def _dup_heads(wk, n_heads):
    d = wk.shape[0]
    return jnp.broadcast_to(wk.reshape(d, n_heads, 1, HEAD_DIM), (d, n_heads, 2, HEAD_DIM)).reshape(d, n_heads * LANES)


def _lambda_init(layer):
    return 0.8 - 0.6 * math.exp(-0.3 * layer)


def _moe_postnorm(routed, mlat, mctx, lng, lnb, nct, alpha, layer, w_gu, b_gu, w_down, b_down):
    x1, tg, n8, pos, xs, ys = routed
    n_exp = w_gu.shape[1]
    b, t, _ = x1.shape
    place = _placement(n8[:, :, 0], b * t * TOP_K)
    ys = _moe_experts(place, layer, xs, ys, w_gu, b_gu, w_down, b_down)
    pos_t, tg_t = jnp.transpose(pos, (0, 2, 1)), jnp.transpose(tg, (0, 2, 1))
    return _combine_postnorm(ys, pos_t, tg_t, x1, mlat, mctx, lng, lnb, nct, alpha, n_exp)


def kernel(x, c, ctx, c_ctx, mod_w, mod_b, ln_g, ln_b, ab_w_in, ab_sink, ab_w_out,
           cd_w_in, cd_lambda, cd_subln_g, cd_q_norm_g, cd_k_norm_g, cd_w_out,
           router_w, router_b, expert_w_gu, expert_b_gu, expert_w_down, expert_b_down):
    b, s, d = x.shape
    n_ctx = ctx.shape[1]
    depth = mod_w.shape[0]
    n_exp = router_w.shape[-1]
    assert d == 16 * HEAD_DIM and n_ctx % TM == 0 and s % TM == 0 and s % GRID_W == 0
    nct = n_ctx // TM
    alpha = (2 * depth) ** 0.25

    cos, sin = _rope_tables(n_ctx, s)
    cs_dft = _channel_dft()
    bd = _group_mean_matrix()
    mods = _mod_vectors(c, c_ctx, mod_w, mod_b)

    xs, xc = x, ctx
    for l in range(depth):
        last = l == depth - 1
        i = l // 2
        mlat = mods[l, :b].reshape(b, 1, N_MOD * d)
        mctx = mods[l, b:b + 1]
        lng1, lnb1 = ln_g[l, 0].reshape(1, d), ln_b[l, 0].reshape(1, d)
        lng2, lnb2 = ln_g[l, 1].reshape(1, d), ln_b[l, 1].reshape(1, d)
        rwt = router_w[l].T.astype(BF16)
        rb = router_b[l].reshape(n_exp, 1)
        if l % 2 == 0:
            w = ab_w_in[i]
            w_n = jnp.concatenate([w[:, :1024], _dup_heads(w[:, 1024:1280], 4)], axis=1).astype(BF16)
            wvt = w[:, 1280:1536].T.astype(BF16)
            uw, q, k, vt = _proj_ab(xs, xc, mlat, mctx, w_n, wvt, cos, sin, cs_dft, nct)
            oa = jnp.concatenate([_fourier(uw[:, :n_ctx]), _fourier(uw[:, n_ctx:])], axis=1)
            ob = _win_attn(ab_sink[i], q, k, vt, nct)
            if last:
                parts = [oa[:, n_ctx:], ob[:, n_ctx:]]
            else:
                parts = [oa, ob]
            w_out = ab_w_out[i].astype(BF16)
        else:
            w = cd_w_in[i]
            w_n = jnp.concatenate([w[:, :1536], _dup_heads(w[:, 1536:1664], 2)], axis=1).astype(BF16)
            wvt = w[:, 1664:2304].T.astype(BF16)
            gq = jnp.tile(cd_q_norm_g[i], 2).reshape(1, LANES)
            gk = jnp.tile(cd_k_norm_g[i], 2).reshape(1, LANES)
            qc, qd, kc, kd, vt = _proj_cd(xs, xc, mlat, mctx, w_n, wvt, cos, sin, bd, gq, gk, nct)
            sg = cd_subln_g[i].reshape(LANES, 1)
            o_lat = _attn_cd(cd_lambda[i], sg, qc, qd, kc, kd, vt, nct, _lambda_init(l))
            if last:
                parts = [o_lat]
            else:
                raise NotImplementedError("context outputs of a differential/axial layer")
            w_out = cd_w_out[i].astype(BF16)
        experts = (l, expert_w_gu, expert_b_gu, expert_w_down, expert_b_down)
        if last:
            routed = _oproj(parts, w_out, xs, xc, mlat, mctx, lng1, lnb1, rwt, rb, 0, nct, alpha)
            return _moe_postnorm(routed, mlat, mctx, lng2, lnb2, 0, alpha, *experts)
        routed = _oproj(parts, w_out, xs, xc, mlat, mctx, lng1, lnb1, rwt, rb, nct, 0, alpha)
        xs, xc = _moe_postnorm(routed, mlat, mctx, lng2, lnb2, nct, alpha, *experts), None
    return xs[:, n_ctx:]
```

```python
import functools
import math

import jax
import jax.numpy as jnp
import numpy as np
from jax import lax
from jax.experimental import pallas as pl
from jax.experimental.pallas import tpu as pltpu

F32 = jnp.float32
BF16 = jnp.bfloat16

HEAD_DIM = 64
GRID_W = 64
WINDOW = 128
ROPE_THETA = 10000.0
LN_EPS = 1e-6
RMS_EPS = 1e-6
N_MOD = 6
TOP_K = 4
SWIGLU_LIMIT = 7.0
SWIGLU_ALPHA = 1.702
NEG = -1e30
LOG2E = 1.4426950408889634
QSCALE = HEAD_DIM ** -0.5 * LOG2E

N_C_HEADS = 4
N_D_HEADS = 8
N_D_KV = 2
N_UNITS = 2 * N_C_HEADS + N_D_HEADS
ONES_ROWS = 16
_VT_LAYOUT = tuple((h * (2 * HEAD_DIM + ONES_ROWS), 2 * HEAD_DIM) for h in range(N_C_HEADS)) + tuple(
    (N_C_HEADS * (2 * HEAD_DIM + ONES_ROWS) + g * (HEAD_DIM + ONES_ROWS), HEAD_DIM) for g in range(N_D_KV))
VT_ROWS = _VT_LAYOUT[-1][0] + HEAD_DIM + ONES_ROWS

LANES = 128
TM = 256
MOE_BM = 512
ROW_GROUP = 8
SMEM_BLOCK = 1024
VMEM_LIMIT = 56 * 1024 * 1024


def _dot(a, b):
    return jnp.dot(a, b, preferred_element_type=F32)


def _dot_nt(a, b):
    return lax.dot_general(a, b, (((1,), (1,)), ((), ())), preferred_element_type=F32)


def _cparams(sem):
    return pltpu.CompilerParams(dimension_semantics=sem, vmem_limit_bytes=VMEM_LIMIT)


def _layernorm(z, g, b):
    mu = jnp.mean(z, axis=-1, keepdims=True)
    d = z - mu
    var = jnp.mean(d * d, axis=-1, keepdims=True)
    return d * lax.rsqrt(var + LN_EPS) * g + b


def _rope128(x, cos, sin_signed):
    lane = lax.broadcasted_iota(jnp.int32, x.shape, 1)
    lo = (lane % 32) < 16
    rot = jnp.where(lo, pltpu.roll(x, LANES - 16, 1), pltpu.roll(x, 16, 1))
    return x * cos + rot * sin_signed


def _rmsnorm128(x, g, bd):
    x2 = x * x
    hi = x2.astype(BF16)
    lo = (x2 - hi.astype(F32)).astype(BF16)
    ms = _dot(hi, bd) + _dot(lo, bd)
    return x * lax.rsqrt(ms + RMS_EPS) * g


U32 = jnp.uint32
_HI16 = 0xFFFF0000


def _pack_bf16_pairs(x):
    n = x.shape[1] // 2
    bits = lax.bitcast_convert_type(x, U32)
    return (bits[:, :n] >> 16) | (bits[:, n:] & U32(_HI16))


def _unpack_bf16_pairs(p):
    lo = lax.bitcast_convert_type(p << 16, F32).astype(BF16)
    hi = lax.bitcast_convert_type(p & U32(_HI16), F32).astype(BF16)
    return lo, hi


def _modulated(x_ref, ctx_ref, mlat_ref, mctx_ref, is_ctx, d):
    m = jnp.where(is_ctx, mctx_ref[...], mlat_ref[0])
    x = jnp.where(is_ctx, ctx_ref[0], x_ref[0])
    return (x * (1.0 + m[:, d:2 * d]) + m[:, 0:d]).astype(BF16)


def _stream_specs(x, ctx, nct, d, off=0, sub=0, n_sub=1):
    last = x.shape[1] // TM - 1
    tile = lambda j: n_sub * j + sub
    if ctx is None:
        return (x, x), [pl.BlockSpec((1, TM, d), lambda bi, j: (bi, jnp.minimum(tile(j) + off, last), 0)),
                        pl.BlockSpec((1, TM, d), lambda bi, j: (bi, jnp.minimum(tile(j), max(nct - 1, 0)), 0))]
    return (x, ctx), [pl.BlockSpec((1, TM, d), lambda bi, j: (bi, jnp.clip(tile(j) - nct, 0, last), 0)),
                      pl.BlockSpec((1, TM, d), lambda bi, j: (bi, jnp.minimum(tile(j), nct - 1), 0))]


def _mod_kernel(c_ref, w_ref, b_ref, o_ref):
    c = c_ref[...]
    s = (c / (1.0 + jnp.exp(-c))).astype(BF16)
    o_ref[0] = _dot(s, w_ref[0].astype(BF16)) + b_ref[0]


def _mod_vectors(c, c_ctx, mod_w, mod_b):
    depth, d, n = mod_w.shape
    b = c.shape[0]
    rows = 8 * (-(-(b + 1) // 8))
    cs = jnp.zeros((rows, d), F32).at[:b].set(c).at[b].set(c_ctx)
    tn = 1536
    out = pl.pallas_call(
        _mod_kernel,
        grid=(depth, n // tn),
        in_specs=[pl.BlockSpec((rows, d), lambda l, j: (0, 0)),
                  pl.BlockSpec((1, d, tn), lambda l, j: (l, 0, j)),
                  pl.BlockSpec((1, 1, tn), lambda l, j: (l, 0, j))],
        out_specs=pl.BlockSpec((1, rows, tn), lambda l, j: (l, 0, j)),
        out_shape=jax.ShapeDtypeStruct((depth, rows, n), F32),
        compiler_params=_cparams(("parallel", "parallel")),
        name="mod_vectors",
    )(cs, mod_w, mod_b.reshape(depth, 1, n))
    return out


def _proj_ab_kernel(nct, x_ref, ctx_ref, mlat_ref, mctx_ref, w_ref, wvt_ref, cos_ref, sin_ref, cs_ref,
                    uw_ref, q_ref, k_ref, vt_ref):
    is_ctx = pl.program_id(1) < nct
    d = x_ref.shape[-1]
    h = _modulated(x_ref, ctx_ref, mlat_ref, mctx_ref, is_ctx, d)
    p = _dot(h, w_ref[...])
    uw_ref[0] = _dot(p[:, 0:256].astype(BF16), cs_ref[...]).astype(BF16)
    cos = cos_ref[...]
    sin = sin_ref[...]
    for j in range(6):
        c0 = 256 + j * LANES
        q_ref[0, :, j * LANES:(j + 1) * LANES] = (_rope128(p[:, c0:c0 + LANES], cos, sin) * QSCALE).astype(BF16)
    for j in range(4):
        c0 = 1024 + j * LANES
        k_ref[0, :, j * LANES:(j + 1) * LANES] = _rope128(p[:, c0:c0 + LANES], cos, sin).astype(BF16)
    vt = _dot_nt(wvt_ref[...], h).astype(BF16)
    for c in range(TM // LANES):
        vt_ref[0, c] = vt[:, c * LANES:(c + 1) * LANES]


def _proj_ab(x, ctx, mlat, mctx, w, wvt, cos, sin, cs, nct):
    b, d = x.shape[0], x.shape[2]
    tall = x.shape[1] + (0 if ctx is None else ctx.shape[1])
    stream, stream_specs = _stream_specs(x, ctx, nct, d)
    nt = tall // TM
    nw = w.shape[1]
    nchunk = tall // LANES
    cpt = TM // LANES
    return pl.pallas_call(
        functools.partial(_proj_ab_kernel, nct),
        grid=(b, nt),
        in_specs=stream_specs + [
                  pl.BlockSpec((1, 1, N_MOD * d), lambda bi, i: (bi, 0, 0)),
                  pl.BlockSpec((1, N_MOD * d), lambda bi, i: (0, 0)),
                  pl.BlockSpec((d, nw), lambda bi, i: (0, 0)),
                  pl.BlockSpec((256, d), lambda bi, i: (0, 0)),
                  pl.BlockSpec((TM, LANES), lambda bi, i: (i, 0)),
                  pl.BlockSpec((TM, LANES), lambda bi, i: (i, 0)),
                  pl.BlockSpec((256, 512), lambda bi, i: (0, 0))],
        out_specs=[pl.BlockSpec((1, TM, 512), lambda bi, i: (bi, i, 0)),
                   pl.BlockSpec((1, TM, 768), lambda bi, i: (bi, i, 0)),
                   pl.BlockSpec((1, TM, 512), lambda bi, i: (bi, i, 0)),
                   pl.BlockSpec((1, cpt, 256, LANES), lambda bi, i: (bi, i, 0, 0))],
        out_shape=[jax.ShapeDtypeStruct((b, tall, 512), BF16),
                   jax.ShapeDtypeStruct((b, tall, 768), BF16),
                   jax.ShapeDtypeStruct((b, tall, 512), BF16),
                   jax.ShapeDtypeStruct((b, nchunk, 256, LANES), BF16)],
        compiler_params=_cparams(("parallel", "parallel")),
        name="proj_ab",
    )(*stream, mlat, mctx, w, wvt, cos, sin, cs)


def _proj_cd_kernel(nct, x_ref, ctx_ref, mlat_ref, mctx_ref, w_ref, wvt_ref, cos_ref, sin_ref, bd_ref,
                    gq_ref, gk_ref, qc_ref, qd_ref, kc_ref, kd_ref, vt_ref):
    is_ctx = pl.program_id(1) < nct
    d = x_ref.shape[-1]
    h = _modulated(x_ref, ctx_ref, mlat_ref, mctx_ref, is_ctx, d)
    p = _dot(h, w_ref[...])
    cos = cos_ref[...]
    sin = sin_ref[...]
    bd = bd_ref[...]
    gq = gq_ref[...]
    gk = gk_ref[...]
    for j in range(4):
        sl = slice(j * LANES, (j + 1) * LANES)
        qc_ref[0, :, sl] = (_rope128(p[:, j * LANES:(j + 1) * LANES], cos, sin) * QSCALE).astype(BF16)
        c0 = 512 + j * LANES
        qd_ref[0, :, sl] = (_rope128(_rmsnorm128(p[:, c0:c0 + LANES], gq, bd), cos, sin) * QSCALE).astype(BF16)
        c0 = 1024 + j * LANES
        kc_ref[0, :, sl] = _rope128(p[:, c0:c0 + LANES], cos, sin).astype(BF16)
    for j in range(2):
        c0 = 1536 + j * LANES
        kd_ref[0, :, j * LANES:(j + 1) * LANES] = _rope128(
            _rmsnorm128(p[:, c0:c0 + LANES], gk, bd), cos, sin).astype(BF16)
    vt = _dot_nt(wvt_ref[...], h).astype(BF16)
    ones = jnp.ones((ONES_ROWS, vt.shape[1]), BF16)
    src = 0
    for dst, dv in _VT_LAYOUT:
        vt_ref[0, 0, dst:dst + dv, :] = vt[src:src + dv, :]
        vt_ref[0, 0, dst + dv:dst + dv + ONES_ROWS, :] = ones
        src += dv


def _proj_cd(x, ctx, mlat, mctx, w, wvt, cos, sin, bd, gq, gk, nct):
    b, d = x.shape[0], x.shape[2]
    tall = x.shape[1] + (0 if ctx is None else ctx.shape[1])
    stream, stream_specs = _stream_specs(x, ctx, nct, d)
    nt = tall // TM
    nw = w.shape[1]
    nv = VT_ROWS
    full = lambda bi, i: (0, 0)
    tile = lambda bi, i: (bi, i, 0)
    return pl.pallas_call(
        functools.partial(_proj_cd_kernel, nct),
        grid=(b, nt),
        in_specs=stream_specs + [
                  pl.BlockSpec((1, 1, N_MOD * d), lambda bi, i: (bi, 0, 0)),
                  pl.BlockSpec((1, N_MOD * d), full),
                  pl.BlockSpec((d, nw), full),
                  pl.BlockSpec(wvt.shape, full),
                  pl.BlockSpec((TM, LANES), lambda bi, i: (i, 0)),
                  pl.BlockSpec((TM, LANES), lambda bi, i: (i, 0)),
                  pl.BlockSpec((LANES, LANES), full),
                  pl.BlockSpec((1, LANES), full),
                  pl.BlockSpec((1, LANES), full)],
        out_specs=[pl.BlockSpec((1, TM, 512), tile),
                   pl.BlockSpec((1, TM, 512), tile),
                   pl.BlockSpec((1, TM, 512), tile),
                   pl.BlockSpec((1, TM, 256), tile),
                   pl.BlockSpec((1, 1, nv, TM), lambda bi, i: (bi, i, 0, 0))],
        out_shape=[jax.ShapeDtypeStruct((b, tall, 512), BF16),
                   jax.ShapeDtypeStruct((b, tall, 512), BF16),
                   jax.ShapeDtypeStruct((b, tall, 512), BF16),
                   jax.ShapeDtypeStruct((b, tall, 256), BF16),
                   jax.ShapeDtypeStruct((b, nt, nv, TM), BF16)],
        compiler_params=_cparams(("parallel", "parallel")),
        name="proj_cd",
    )(*stream, mlat, mctx, w, wvt, cos, sin, bd, gq, gk)


def _fourier_kernel(nb, scale, uw_ref, ca_ref, sa_ref, cb_ref, sb_ref, o_ref):
    j = pl.program_id(0)
    ca = ca_ref[pl.ds(j, 1), :]
    sa = sa_ref[pl.ds(j, 1), :]
    cb = cb_ref[...]
    sb = sb_ref[...]
    ct = (ca * cb - sa * sb).astype(BF16)
    nst = (-(sa * cb + ca * sb)).astype(BF16)
    for bi in range(nb):
        acc = _dot(ct, uw_ref[bi, :, 0:256]) + _dot(nst, uw_ref[bi, :, 256:512])
        o_ref[bi] = (acc * scale).astype(BF16)


def _dft_tables(t, tmf):
    k = np.arange(t, dtype=np.int64)
    j1 = np.arange(t // tmf, dtype=np.int64) * tmf
    j0 = np.arange(tmf, dtype=np.int64)
    aa = (2.0 * np.pi / t) * ((j1[:, None] * k[None, :]) % t)
    ab = (2.0 * np.pi / t) * ((j0[:, None] * k[None, :]) % t)
    f = lambda a: jnp.asarray(a, F32)
    return f(np.cos(aa)), f(np.sin(aa)), f(np.cos(ab)), f(np.sin(ab))


def _fourier(uw):
    b, t, _ = uw.shape
    tmf = min(128, t)
    ca, sa, cb, sb = _dft_tables(t, tmf)
    scale = 1.0 / math.sqrt(t * HEAD_DIM)
    full2 = lambda j: (0, 0)
    return pl.pallas_call(
        functools.partial(_fourier_kernel, b, scale),
        grid=(t // tmf,),
        in_specs=[pl.BlockSpec((b, t, 512), lambda j: (0, 0, 0)),
                  pl.BlockSpec((t // tmf, t), full2),
                  pl.BlockSpec((t // tmf, t), full2),
                  pl.BlockSpec((tmf, t), full2),
                  pl.BlockSpec((tmf, t), full2)],
        out_specs=pl.BlockSpec((b, tmf, 256), lambda j: (0, j, 0)),
        out_shape=jax.ShapeDtypeStruct((b, t, 256), BF16),
        compiler_params=_cparams(("parallel",)),
        name="fourier_mix",
    )(uw, ca, sa, cb, sb)


def _win_attn_kernel(nct, nchunk, sink_ref, q_ref, k_ref, vt_ref, o_ref, s_a, s_b):
    i = pl.program_id(1)
    is_ctx = i < nct
    j = i - nct
    ctx_rows = nct * TM
    ctx_chunks = ctx_rows // LANES
    lw = TM + 2 * WINDOW
    lchunks = lw // LANES
    cs = jnp.clip(ctx_chunks + (TM // LANES) * j - WINDOW // LANES, 0, nchunk - lchunks)
    rs = pl.multiple_of(cs * LANES, LANES)
    qpos = j * TM + lax.broadcasted_iota(jnp.int32, (1, TM), 1)
    kpos = cs * LANES - ctx_rows + lax.broadcasted_iota(jnp.int32, (lw, 1), 0)
    allowed = (jnp.abs(qpos - kpos) <= WINDOW) & (kpos >= 0) & jnp.logical_not(is_ctx)
    lane = lax.broadcasted_iota(jnp.int32, (TM, LANES), 1)
    n_heads = q_ref.shape[-1] // HEAD_DIM
    group = n_heads // (k_ref.shape[-1] // LANES)

    def scores(hq, dst_s):
        pair, half = divmod(hq, 2)
        g = hq // group
        qp = q_ref[0, :, pair * LANES:(pair + 1) * LANES]
        qm = jnp.where((lane >= HEAD_DIM) == (half == 1), qp, jnp.zeros_like(qp))
        dst_s[0:ctx_rows, :] = _dot_nt(k_ref[0, 0:ctx_rows, g * LANES:(g + 1) * LANES], qm)
        s_l = _dot_nt(k_ref[0, pl.ds(rs, lw), g * LANES:(g + 1) * LANES], qm)
        dst_s[ctx_rows:ctx_rows + lw, :] = jnp.where(allowed, s_l, NEG)

    def consume(hq, src_s):
        g = hq // group
        s = src_s[...]
        sk = sink_ref[hq] * LOG2E
        m = jnp.maximum(jnp.max(s, axis=0, keepdims=True), sk)
        p = jnp.exp2(s - m)
        l = jnp.sum(p, axis=0, keepdims=True) + jnp.exp2(sk - m)
        p = p.astype(BF16)
        vrows = slice(g * HEAD_DIM, (g + 1) * HEAD_DIM)
        acc = jnp.zeros((HEAD_DIM, TM), F32)
        for c in range(ctx_chunks):
            acc = acc + _dot(vt_ref[0, c, vrows, :], p[c * LANES:(c + 1) * LANES, :])
        for c in range(lchunks):
            r0 = ctx_rows + c * LANES
            acc = acc + _dot(vt_ref[0, cs + c, vrows, :], p[r0:r0 + LANES, :])
        return acc / l

    bufs = (s_a, s_b)
    scores(0, s_a)
    outs = []
    for hq in range(n_heads):
        if hq + 1 < n_heads:
            scores(hq + 1, bufs[(hq + 1) % 2])
        outs.append(consume(hq, bufs[hq % 2]))
        if hq % 2 == 1:
            pair = hq // 2
            o2 = jnp.concatenate(outs, axis=0)
            o_ref[0, :, pair * LANES:(pair + 1) * LANES] = o2.T.astype(BF16)
            outs = []


def _win_attn(sink, q, k, vt, nct):
    b, tall, qw = q.shape
    nt = tall // TM
    nchunk = vt.shape[1]
    return pl.pallas_call(
        functools.partial(_win_attn_kernel, nct, nchunk),
        grid=(b, nt),
        in_specs=[pl.BlockSpec(memory_space=pltpu.SMEM),
                  pl.BlockSpec((1, TM, qw), lambda bi, i: (bi, i, 0)),
                  pl.BlockSpec((1, tall, k.shape[-1]), lambda bi, i: (bi, 0, 0)),
                  pl.BlockSpec((1, nchunk, vt.shape[2], LANES), lambda bi, i: (bi, 0, 0, 0))],
        out_specs=pl.BlockSpec((1, TM, qw), lambda bi, i: (bi, i, 0)),
        out_shape=jax.ShapeDtypeStruct((b, tall, qw), BF16),
        scratch_shapes=[pltpu.VMEM((nct * TM + TM + 2 * WINDOW, TM), F32)] * 2,
        compiler_params=_cparams(("parallel", "parallel")),
        name="window_attention",
    )(sink, q, k, vt)


def _unit_operands(u):
    if u < 2 * N_C_HEADS:
        hc = u // 2
        dst, dv = _VT_LAYOUT[hc]
        return True, hc, dst, dv
    g = (u - 2 * N_C_HEADS) // (N_D_HEADS // N_D_KV)
    dst, dv = _VT_LAYOUT[N_C_HEADS + g]
    return False, g, dst, dv


def _attn_cd_kernel(nkt, lam_init, lam_ref, sg_ref, qc_ref, qd_ref, kc_ref, kd_ref, vt_ref, o_ref,
                    qm_s, m_s, acc_s, s_a, s_b):
    mq = qc_ref.shape[1]
    tk = vt_ref.shape[-1]
    lane = lax.broadcasted_iota(jnp.int32, (mq, LANES), 1)
    upper = lane >= HEAD_DIM
    for u in range(N_UNITS):
        src = qc_ref if u < 2 * N_C_HEADS else qd_ref
        pair = (u if u < 2 * N_C_HEADS else u - 2 * N_C_HEADS) // 2
        qp = src[0, :, pair * LANES:(pair + 1) * LANES]
        qm_s[u] = jnp.where(upper == (u % 2 == 1), qp, jnp.zeros_like(qp))
    m_s[...] = jnp.full(m_s.shape, NEG, F32)
    acc_s[...] = jnp.zeros(acc_s.shape, F32)

    def scores(tile, u, dst_s):
        is_c, slab, _, _ = _unit_operands(u)
        k_ref = kc_ref if is_c else kd_ref
        r0 = pl.multiple_of(tile * tk, tk)
        dst_s[u] = _dot_nt(k_ref[0, pl.ds(r0, tk), slab * LANES:(slab + 1) * LANES], qm_s[u])

    def consume(tile, u, src_s):
        _, _, row0, dv = _unit_operands(u)
        rows = dv + ONES_ROWS
        m_old = m_s[u]
        m_new = jnp.maximum(m_old, jnp.max(src_s[u], axis=0, keepdims=True))
        alpha = jnp.exp2(m_old - m_new)
        p = jnp.exp2(src_s[u] - m_new).astype(BF16)
        pv = _dot(vt_ref[0, tile, row0:row0 + rows, :], p)
        acc_s[u, 0:rows, :] = acc_s[u, 0:rows, :] * alpha + pv
        m_s[u] = m_new

    def stage(tile, src_s, dst_s):
        nxt = jnp.minimum(tile + 1, nkt - 1)
        for u in range(N_UNITS):
            scores(nxt, u, dst_s)
            consume(tile, u, src_s)

    for u in range(N_UNITS):
        scores(0, u, s_a)
    stage(0, s_a, s_b)

    bufs = (s_a, s_b)
    per_trip = 8

    def body(trip, carry):
        for j in range(per_trip):
            tile = per_trip * trip + 1 + j
            stage(tile, bufs[(1 + j) % 2], bufs[j % 2])
        return carry

    n_trips = (nkt - 1) // per_trip
    lax.fori_loop(0, n_trips, body, 0)
    for tile in range(1 + n_trips * per_trip, nkt):
        stage(tile, bufs[tile % 2], bufs[(tile + 1) % 2])

    lv = lam_ref[...]
    lam = (jnp.exp(jnp.sum(lv[0:1] * lv[1:2], axis=-1, keepdims=True))
           - jnp.exp(jnp.sum(lv[2:3] * lv[3:4], axis=-1, keepdims=True)) + lam_init)
    sg = sg_ref[...]
    dvc = 2 * HEAD_DIM
    for hc in range(N_C_HEADS):
        o1 = acc_s[2 * hc, 0:dvc, :] / acc_s[2 * hc, dvc:dvc + 1, :]
        o2 = acc_s[2 * hc + 1, 0:dvc, :] / acc_s[2 * hc + 1, dvc:dvc + 1, :]
        o = o1 - lam * o2
        ms = jnp.mean(o * o, axis=0, keepdims=True)
        o = o * lax.rsqrt(ms + RMS_EPS) * sg * (1.0 - lam_init)
        o_ref[0, :, hc * LANES:(hc + 1) * LANES] = o.T.astype(BF16)
    for pair in range(N_D_HEADS // 2):
        outs = []
        for half in range(2):
            u = 2 * N_C_HEADS + 2 * pair + half
            outs.append(acc_s[u, 0:HEAD_DIM, :] / acc_s[u, HEAD_DIM:HEAD_DIM + 1, :])
        o2 = jnp.concatenate(outs, axis=0)
        c0 = N_C_HEADS * LANES + pair * LANES
        o_ref[0, :, c0:c0 + LANES] = o2.T.astype(BF16)


def _attn_cd(lam_vec, sg, qc, qd, kc, kd, vt, nct, lam_init):
    b, tall, _ = qc.shape
    nkt, nv, tk = vt.shape[1], vt.shape[2], vt.shape[3]
    mq = TM
    nq = tall // mq - nct
    ow = N_C_HEADS * LANES + N_D_HEADS * HEAD_DIM
    qtile = lambda bi, i: (bi, i + nct, 0)
    return pl.pallas_call(
        functools.partial(_attn_cd_kernel, nkt, lam_init),
        grid=(b, nq),
        in_specs=[pl.BlockSpec((4, HEAD_DIM), lambda bi, i: (0, 0)),
                  pl.BlockSpec((LANES, 1), lambda bi, i: (0, 0)),
                  pl.BlockSpec((1, mq, qc.shape[-1]), qtile),
                  pl.BlockSpec((1, mq, qd.shape[-1]), qtile),
                  pl.BlockSpec((1, tall, kc.shape[-1]), lambda bi, i: (bi, 0, 0)),
                  pl.BlockSpec((1, tall, kd.shape[-1]), lambda bi, i: (bi, 0, 0)),
                  pl.BlockSpec((1, nkt, nv, tk), lambda bi, i: (bi, 0, 0, 0))],
        out_specs=pl.BlockSpec((1, mq, ow), lambda bi, i: (bi, i, 0)),
        out_shape=jax.ShapeDtypeStruct((b, nq * mq, ow), BF16),
        scratch_shapes=[pltpu.VMEM((N_UNITS, mq, LANES), BF16),
                        pltpu.VMEM((N_UNITS, 1, mq), F32),
                        pltpu.VMEM((N_UNITS, 2 * HEAD_DIM + ONES_ROWS, mq), F32),
                        pltpu.VMEM((N_UNITS, tk, mq), F32),
                        pltpu.VMEM((N_UNITS, tk, mq), F32)],
        compiler_params=_cparams(("parallel", "parallel")),
        name="attention_cd",
    )(lam_vec, sg, qc, qd, kc, kd, vt)


OPROJ_SUB = 2


def _oproj_kernel(n_parts, widths, nct, nt, alpha, n_exp, *refs):
    o_refs = refs[:n_parts]
    w_ref = refs[n_parts]
    stream = refs[n_parts + 1:n_parts + 1 + 2 * OPROJ_SUB]
    (mlat_ref, mctx_ref, lng_ref, lnb_ref, rwt_ref, rb_ref, tri_ref, ltri_ref,
     x1_ref, tg_ref, n8_ref, pos_ref, xs_ref, ys_hbm, zero_s, sem) = refs[n_parts + 1 + 2 * OPROJ_SUB:]
    d = x1_ref.shape[-1]
    tm = TM
    sr = xs_ref.shape[1] // OPROJ_SUB
    bi = pl.program_id(0)
    tiles = [OPROJ_SUB * pl.program_id(1) + s for s in range(OPROJ_SUB)]
    zero_s[...] = jnp.zeros(zero_s.shape, zero_s.dtype)
    clears = []
    for s, i in enumerate(tiles):
        tail0 = pl.multiple_of((bi * nt + i) * sr + TOP_K * tm, ROW_GROUP)
        clears.append(pltpu.make_async_copy(zero_s, ys_hbm.at[pl.ds(tail0, zero_s.shape[0])], sem.at[s]))
        pl.when(i < nt)(clears[s].start)
    state = [dict() for _ in tiles]

    def project(s, i):
        st, rows = state[s], slice(s * tm, (s + 1) * tm)
        is_ctx = i < nct
        y = None
        r0 = 0
        for o_ref, wd in zip(o_refs, widths):
            part = _dot(o_ref[0, rows, :], w_ref[r0:r0 + wd, :])
            y = part if y is None else y + part
            r0 += wd
        st["m"] = jnp.where(is_ctx, mctx_ref[...], mlat_ref[0])
        st["z"] = alpha * jnp.where(is_ctx, stream[2 * s + 1][0], stream[2 * s][0]) + st["m"][:, 2 * d:3 * d] * y

    def normalise(s, i):
        st, rows = state[s], slice(s * tm, (s + 1) * tm)
        m = st["m"]
        x1 = _layernorm(st["z"], lng_ref[...], lnb_ref[...])
        x1_ref[0, rows, :] = x1
        st["hb"] = (x1 * (1.0 + m[:, 4 * d:5 * d]) + m[:, 3 * d:4 * d]).astype(BF16)

    def route(s, i):
        st, rows = state[s], slice(s * tm, (s + 1) * tm)
        logits = _dot_nt(rwt_ref[...], st["hb"]) + rb_ref[...]
        row = lax.broadcasted_iota(jnp.int32, logits.shape, 0)
        vals, idxs = [], []
        for _ in range(TOP_K):
            mx = jnp.max(logits, axis=0, keepdims=True)
            idx = jnp.min(jnp.where(logits == mx, row, n_exp), axis=0, keepdims=True)
            vals.append(mx)
            idxs.append(idx)
            logits = jnp.where(row == idx, NEG, logits)
        es = [jnp.exp(v - vals[0]) for v in vals]
        tot = es[0] + es[1] + es[2] + es[3]
        pad_i = jnp.zeros((8 - TOP_K, tm), jnp.int32)
        pad_f = jnp.zeros((8 - TOP_K, tm), F32)
        te = jnp.concatenate(idxs + [pad_i], axis=0)
        tg_ref[0, :, rows] = jnp.concatenate([e / tot for e in es] + [pad_f], axis=0)
        _, oh = _expert_onehot(te, n_exp)
        n8 = _groups(jnp.sum(oh, axis=1, keepdims=True))
        n8_ref[0, s] = jnp.broadcast_to(n8, (n_exp, LANES)).astype(jnp.int32)
        lbase = _dot(ltri_ref[...], jnp.broadcast_to(n8, (n_exp, LANES)).astype(BF16))[:, 0:1]
        before = _dot(oh.astype(BF16), tri_ref[...])
        where_to = ROW_GROUP * lbase + before
        pos = [jnp.sum(jnp.where(row == te[k:k + 1, :], where_to, 0.0), axis=0, keepdims=True).astype(jnp.int32)
               for k in range(TOP_K)]
        pos_ref[0, :, rows] = jnp.concatenate(pos + [pad_i], axis=0)
        st["pos"] = pos

    def sort_chunk(s, c):
        st = state[s]
        r = lax.broadcasted_iota(jnp.int32, (tm, tm), 0) + c * tm
        hit = r == st["pos"][0]
        for k in range(1, TOP_K):
            hit = hit | (r == st["pos"][k])
        packed = _pack_bf16_pairs(_dot(jnp.where(hit, 1.0, 0.0).astype(BF16), st["hb"]))
        xs_ref[0, s * sr + c * tm:s * sr + (c + 1) * tm, :] = packed

    fronts = (project, normalise, route)
    n_chunks = sr // tm
    for f in fronts:
        f(0, tiles[0])
    for s in range(1, len(tiles) + 1):
        nxt = [functools.partial(f, s, tiles[s]) for f in fronts] if s < len(tiles) else []
        for c in range(n_chunks):
            sort_chunk(s - 1, c)
            if c < len(nxt):
                nxt[c]()
        for f in nxt[n_chunks:]:
            f()
    for s, i in enumerate(tiles):
        pl.when(i < nt)(clears[s].wait)


def _oproj(parts, w_out, x, ctx, mlat, mctx, lng, lnb, rwt, rb, nct_out, x_tile_off, alpha):
    b, t, _ = parts[0].shape
    d = x.shape[-1]
    nt = t // TM
    n_exp = rwt.shape[0]
    widths = tuple(p.shape[-1] for p in parts)
    tile = lambda bi, j: (bi, j, 0)
    full = lambda bi, j: (0, 0)
    ns = OPROJ_SUB
    stream, stream_specs = [], []
    for sub in range(ns):
        arrs, specs = _stream_specs(x, ctx, nct_out, d, x_tile_off, sub, ns)
        stream += arrs
        stream_specs += specs
    in_specs = [pl.BlockSpec((1, ns * TM, wd), tile) for wd in widths] + [
        pl.BlockSpec(w_out.shape, full)] + stream_specs + [
        pl.BlockSpec((1, 1, N_MOD * d), lambda bi, i: (bi, 0, 0)),
        pl.BlockSpec((1, N_MOD * d), full),
        pl.BlockSpec((1, d), full),
        pl.BlockSpec((1, d), full),
        pl.BlockSpec((n_exp, d), full),
        pl.BlockSpec((n_exp, 1), full),
        pl.BlockSpec((TM, TM), full),
        pl.BlockSpec((n_exp, n_exp), full)]
    sr = _sort_rows(n_exp)
    tri = jnp.asarray(np.triu(np.ones((TM, TM)), 1), BF16)
    ltri = jnp.asarray(np.tril(np.ones((n_exp, n_exp)), -1), BF16)
    small = pl.BlockSpec((1, 8, ns * TM), lambda bi, j: (bi, 0, j))
    x1, tg, n8, pos, xs, ys = pl.pallas_call(
        functools.partial(_oproj_kernel, len(parts), widths, nct_out, nt, alpha, n_exp),
        grid=(b, -(-nt // ns)),
        in_specs=in_specs,
        out_specs=[pl.BlockSpec((1, ns * TM, d), tile),
                   small,
                   pl.BlockSpec((1, ns, n_exp, LANES), lambda bi, j: (bi, j, 0, 0)),
                   small,
                   pl.BlockSpec((1, ns * sr, d // 2), tile),
                   pl.BlockSpec(memory_space=pl.ANY)],
        out_shape=[jax.ShapeDtypeStruct((b, t, d), F32),
                   jax.ShapeDtypeStruct((b, 8, t), F32),
                   jax.ShapeDtypeStruct((b, nt, n_exp, LANES), jnp.int32),
                   jax.ShapeDtypeStruct((b, 8, t), jnp.int32),
                   jax.ShapeDtypeStruct((b, nt * sr, d // 2), U32),
                   jax.ShapeDtypeStruct(((b * nt + 1) * sr, d // 2), U32)],
        scratch_shapes=[pltpu.VMEM((sr - TOP_K * TM, d // 2), U32), pltpu.SemaphoreType.DMA((ns,))],
        compiler_params=_cparams(("parallel", "parallel")),
        name="out_proj_norm_router",
    )(*parts, w_out, *stream, mlat, mctx, lng, lnb, rwt, rb, tri, ltri)
    return x1, tg, n8.reshape(b * nt, n_exp, LANES), pos, xs.reshape(b * nt * sr, d // 2), ys


def _moe_kernel(be_ref, nx_ref, bv_ref, nu_ref, rprev_ref, rcur_ref, rnext_ref, xs_hbm, wgu_hbm, bgu_ref,
                wd_hbm, bd_ref, ys_in_hbm, ys_hbm, wgu_f, wd_f, wgu_s, wd_s, xbuf, ybuf, sem, gsem, ssem):
    del ys_in_hbm
    blk = pl.program_id(0)
    ff = wd_s.shape[0]
    gpb = xbuf.shape[1] // ROW_GROUP
    par = blk % 2

    def gathers(rec, buf, wait):
        if wait:
            pltpu.make_async_copy(xs_hbm.at[pl.ds(0, xbuf.shape[1])], xbuf.at[buf], gsem.at[buf]).wait()
            return
        for g in range(gpb):
            _group_copy(xs_hbm, rec[g], xbuf.at[buf], g, gsem.at[buf]).start(priority=g % 2)

    def scatters(rec, buf, wait):
        if wait:
            pltpu.make_async_copy(ybuf.at[buf], ys_hbm.at[pl.ds(0, ybuf.shape[1])], ssem.at[buf]).wait()
            return
        for g in range(gpb):
            _group_copy(ybuf.at[buf], g, ys_hbm, rec[gpb + g], ssem.at[buf]).start(priority=g % 2)

    def fetch(e):
        return (pltpu.make_async_copy(wgu_hbm.at[e], wgu_f, sem.at[0]),
                pltpu.make_async_copy(wd_hbm.at[e], wd_f, sem.at[1]))

    @pl.when(blk < nu_ref[0])
    def _():
        e = be_ref[blk]
        new_expert = (blk == 0) | (e != be_ref[jnp.maximum(blk - 1, 0)])

        @pl.when(blk == 0)
        def _():
            for cp in fetch(e):
                cp.start()

        @pl.when(new_expert)
        def _():
            for cp in fetch(e):
                cp.wait()
            rows = 128

            def cast_gu(r, c):
                r0 = pl.multiple_of(r * rows, rows)
                wgu_s[pl.ds(r0, rows), :] = wgu_f[pl.ds(r0, rows), :].astype(BF16)
                return c

            def cast_d(r, c):
                r0 = pl.multiple_of(r * rows, rows)
                wd_s[pl.ds(r0, rows), :] = wd_f[pl.ds(r0, rows), :].astype(BF16)
                return c

            lax.fori_loop(0, wgu_s.shape[0] // rows, cast_gu, 0)
            lax.fori_loop(0, wd_s.shape[0] // rows, cast_d, 0)

            @pl.when(nx_ref[blk] != e)
            def _():
                for cp in fetch(nx_ref[blk]):
                    cp.start()

        @pl.when(blk == 0)
        def _():
            gathers(rcur_ref, 0, False)

        gathers(rcur_ref, par, True)
        bm = xbuf.shape[1]

        def ffn(n_rows):
            gathers(rnext_ref, 1 - par, False)
            x = jnp.concatenate(_unpack_bf16_pairs(xbuf[par, 0:n_rows, :]), axis=1)
            gu = _dot(x, wgu_s[...]) + bgu_ref[0]
            g = jnp.minimum(gu[:, :ff], SWIGLU_LIMIT)
            u = jnp.clip(gu[:, ff:], -SWIGLU_LIMIT, SWIGLU_LIMIT)
            act = (u + 1.0) * (g / (1.0 + jnp.exp(-SWIGLU_ALPHA * g)))
            y = _dot(act.astype(BF16), wd_s[...]) + bd_ref[0]
            ybuf[par, 0:n_rows, :] = _pack_bf16_pairs(y.astype(BF16).astype(F32))
            if n_rows < bm:
                ybuf[par, n_rows:, :] = jnp.zeros((bm - n_rows, ybuf.shape[2]), ybuf.dtype)
            scatters(rcur_ref, par, False)

        quarter = bm // 4
        for n_rows in range(quarter, bm + 1, quarter):
            pl.when((bv_ref[blk] > n_rows - quarter) & (bv_ref[blk] <= n_rows))(functools.partial(ffn, n_rows))

        @pl.when(blk > 0)
        def _():
            scatters(rprev_ref, 1 - par, True)

        @pl.when(blk == nu_ref[0] - 1)
        def _():
            gathers(rnext_ref, 1 - par, True)
            scatters(rcur_ref, par, True)


def _moe_experts(place, layer, xs, ys, w_gu, b_gu, w_down, b_down):
    d = 2 * xs.shape[1]
    ff2 = w_gu.shape[-1]
    ff = w_down.shape[-2]
    off = layer * w_gu.shape[1]
    n_exp = w_gu.shape[0] * w_gu.shape[1]
    w_gu = w_gu.reshape(n_exp, d, ff2)
    w_down = w_down.reshape(n_exp, ff, d)
    nb = place["n_blocks"]
    last = lambda nu: nu[0] - 1
    bblk = lambda i, be, nx, bv, nu: (be[i], 0, 0)
    rec = lambda f: pl.BlockSpec((SMEM_BLOCK,), f, memory_space=pltpu.SMEM)
    grid_spec = pltpu.PrefetchScalarGridSpec(
        num_scalar_prefetch=4,
        grid=(nb,),
        in_specs=[rec(lambda i, be, nx, bv, nu: (jnp.clip(i - 1, 0, last(nu)),)),
                  rec(lambda i, be, nx, bv, nu: (jnp.minimum(i, last(nu)),)),
                  rec(lambda i, be, nx, bv, nu: (jnp.minimum(i + 1, last(nu)),)),
                  pl.BlockSpec(memory_space=pl.ANY),
                  pl.BlockSpec(memory_space=pl.ANY),
                  pl.BlockSpec((1, 1, ff2), bblk),
                  pl.BlockSpec(memory_space=pl.ANY),
                  pl.BlockSpec((1, 1, d), bblk),
                  pl.BlockSpec(memory_space=pl.ANY)],
        out_specs=pl.BlockSpec(memory_space=pl.ANY),
        scratch_shapes=[pltpu.VMEM((d, ff2), F32), pltpu.VMEM((ff, d), F32),
                        pltpu.VMEM((d, ff2), BF16), pltpu.VMEM((ff, d), BF16),
                        pltpu.VMEM((2, MOE_BM, d // 2), U32), pltpu.VMEM((2, MOE_BM, d // 2), U32),
                        pltpu.SemaphoreType.DMA((2,)), pltpu.SemaphoreType.DMA((2,)),
                        pltpu.SemaphoreType.DMA((2,))])
    records = place["rec"]
    return pl.pallas_call(
        _moe_kernel,
        grid_spec=grid_spec,
        out_shape=jax.ShapeDtypeStruct(ys.shape, ys.dtype),
        input_output_aliases={12: 0},
        compiler_params=_cparams(("arbitrary",)),
        name="moe_experts",
    )(place["block_e"] + off, place["block_next_e"] + off, place["block_rows"], place["n_used"],
      records, records, records, xs, w_gu, b_gu.reshape(n_exp, 1, ff2), w_down, b_down.reshape(n_exp, 1, d), ys)


def _expert_onehot(te, n_exp):
    row = lax.broadcasted_iota(jnp.int32, (n_exp, te.shape[1]), 0)
    oh = jnp.zeros(row.shape, F32)
    for k in range(TOP_K):
        oh = oh + (row == te[k:k + 1, :]).astype(F32)
    return row, oh


def _groups(cnt):
    return jnp.floor((cnt + (ROW_GROUP - 1)) / ROW_GROUP)


def _placement(n8, n_asg):
    n_tiles, n_exp = n8.shape
    gpt = _sort_rows(n_exp) // ROW_GROUP
    gpb = MOE_BM // ROW_GROUP
    assert (TOP_K * TM + n_exp * (ROW_GROUP - 1)) // ROW_GROUP < gpt
    ids = jnp.arange(n_exp, dtype=jnp.int32)
    lbase = jnp.cumsum(n8, axis=1) - n8
    cum_incl = jnp.cumsum(n8, axis=0)
    cum_excl = cum_incl - n8
    groups = cum_incl[-1]
    padded = (groups + gpb - 1) // gpb * gpb
    pend = jnp.cumsum(padded)
    pstart = pend - padded
    n_blocks = -(-(n_asg + n_tiles * n_exp * (ROW_GROUP - 1)) // MOE_BM) + n_exp
    n_used = (pend[-1] // gpb).astype(jnp.int32)
    blocks = jnp.arange(n_blocks, dtype=jnp.int32)
    blk = jnp.minimum(blocks, n_used - 1) * gpb
    block_e = jnp.minimum(jnp.sum((pend[None, :] <= blk[:, None]).astype(jnp.int32), axis=1), n_exp - 1)
    later = (padded[None, :] > 0) & (ids[None, :] > block_e[:, None])
    block_next_e = jnp.min(jnp.where(later, ids[None, :], n_exp), axis=1)
    block_next_e = jnp.where(block_next_e == n_exp, block_e, block_next_e).astype(jnp.int32)
    mine = (block_e[:, None] == ids[None, :]).astype(jnp.int32)
    pick = lambda table: jnp.sum(mine[:, :, None] * table.T[None, :, :], axis=1)
    b_start = jnp.sum(mine * pstart[None, :], axis=1)
    b_groups = jnp.sum(mine * groups[None, :], axis=1)
    j_e = (blocks * gpb - b_start)[:, None] + jnp.arange(gpb, dtype=jnp.int32)[None, :]
    valid = (j_e < b_groups[:, None]) & (blocks < n_used)[:, None]
    tile = jnp.sum((pick(cum_incl)[:, None, :] <= j_e[:, :, None]).astype(jnp.int32), axis=2)
    tile = jnp.minimum(tile, n_tiles - 1)
    in_tile = (tile[:, :, None] == jnp.arange(n_tiles, dtype=jnp.int32)[None, None, :]).astype(jnp.int32)
    shift = jnp.sum(in_tile * (pick(lbase) - pick(cum_excl))[:, None, :], axis=2)
    held = tile * gpt + shift + j_e
    pad_dst = n_tiles * gpt + jnp.arange(gpb, dtype=jnp.int32)[None, :]
    src = jnp.where(valid, held, gpt - 1)
    dst = jnp.where(valid, held, pad_dst)
    rec = jnp.concatenate([src, dst, jnp.zeros((n_blocks, SMEM_BLOCK - 2 * gpb), jnp.int32)], axis=1)
    left = b_groups - (blocks - b_start // gpb) * gpb
    block_rows = (jnp.clip(left, 0, gpb) * ROW_GROUP).astype(jnp.int32)
    return dict(rec=rec.reshape(-1), block_e=block_e, block_next_e=block_next_e, block_rows=block_rows,
                n_used=n_used.reshape(1), n_blocks=n_blocks)


def _group_copy(src_ref, src_group, dst_ref, dst_group, sem):
    s0 = pl.multiple_of(src_group * ROW_GROUP, ROW_GROUP)
    d0 = pl.multiple_of(dst_group * ROW_GROUP, ROW_GROUP)
    return pltpu.make_async_copy(src_ref.at[pl.ds(s0, ROW_GROUP)], dst_ref.at[pl.ds(d0, ROW_GROUP)], sem)


def _tile_step():
    return pl.program_id(0) * pl.num_programs(1) + pl.program_id(1)


def _sort_rows(n_exp):
    return -(-(TOP_K * TM + n_exp * ROW_GROUP) // TM) * TM


def _combine_kernel(nct, alpha, ys_ref, pos_ref, tg_ref, x1_ref, mlat_ref, mctx_ref, lng_ref, lnb_ref, o_ref):
    d = x1_ref.shape[-1]
    tm = x1_ref.shape[1]
    is_ctx = pl.program_id(1) < nct
    m = jnp.where(is_ctx, mctx_ref[...], mlat_ref[0])
    pos = pos_ref[0]
    gates = tg_ref[0]
    f_lo = f_hi = None
    for c in range(ys_ref.shape[0] // tm):
        col = lax.broadcasted_iota(jnp.int32, (tm, tm), 1) + c * tm
        w = jnp.where(col == pos[:, 0:1], gates[:, 0:1], 0.0)
        for k in range(1, TOP_K):
            w = w + jnp.where(col == pos[:, k:k + 1], gates[:, k:k + 1], 0.0)
        w = w.astype(BF16)
        y_lo, y_hi = _unpack_bf16_pairs(ys_ref[c * tm:(c + 1) * tm, :])
        f_lo = _dot(w, y_lo) if f_lo is None else f_lo + _dot(w, y_lo)
        f_hi = _dot(w, y_hi) if f_hi is None else f_hi + _dot(w, y_hi)
    f = jnp.concatenate([f_lo, f_hi], axis=1)
    o_ref[0] = _layernorm(alpha * x1_ref[0] + m[:, 5 * d:6 * d] * f, lng_ref[...], lnb_ref[...])


def _combine_postnorm(ys, pos_t, tg_t, x1, mlat, mctx, lng, lnb, nct, alpha, n_exp):
    b, t, d = x1.shape
    nt = t // TM
    tile = lambda bi, i: (bi, i, 0)
    full = lambda bi, i: (0, 0)
    return pl.pallas_call(
        functools.partial(_combine_kernel, nct, alpha),
        grid=(b, nt),
        in_specs=[pl.BlockSpec((_sort_rows(n_exp), d // 2), lambda bi, i: (bi * nt + i, 0)),
                  pl.BlockSpec((1, TM, 8), tile),
                  pl.BlockSpec((1, TM, 8), tile),
                  pl.BlockSpec((1, TM, d), tile),
                  pl.BlockSpec((1, 1, N_MOD * d), lambda bi, i: (bi, 0, 0)),
                  pl.BlockSpec((1, N_MOD * d), full),
                  pl.BlockSpec((1, d), full),
                  pl.BlockSpec((1, d), full)],
        out_specs=pl.BlockSpec((1, TM, d), tile),
        out_shape=jax.ShapeDtypeStruct((b, t, d), F32),
        compiler_params=_cparams(("parallel", "parallel")),
        name="combine_post_norm",
    )(ys, pos_t, tg_t, x1, mlat, mctx, lng, lnb)


def _rope_tables(ctx_len, seq):
    t = np.arange(seq)
    row = (t // GRID_W).astype(np.float64)
    col = (t % GRID_W).astype(np.float64)
    nf = HEAD_DIM // 4
    inv = ROPE_THETA ** (-np.arange(nf, dtype=np.float64) / nf)
    ar = row[:, None] * inv[None, :]
    ac = col[:, None] * inv[None, :]
    ang = np.concatenate([ar, ar, ac, ac], axis=-1)
    cos = np.concatenate([np.ones((ctx_len, HEAD_DIM)), np.cos(ang)], axis=0)
    sin = np.concatenate([np.zeros((ctx_len, HEAD_DIM)), np.sin(ang)], axis=0)
    sign = np.where((np.arange(HEAD_DIM) % 32) < 16, -1.0, 1.0)[None, :]
    cos2 = np.tile(cos, (1, LANES // HEAD_DIM))
    sin2 = np.tile(sin * sign, (1, LANES // HEAD_DIM))
    return jnp.asarray(cos2, F32), jnp.asarray(sin2, F32)


def _channel_dft():
    c = np.arange(HEAD_DIM)
    ang = 2.0 * np.pi * ((c[:, None] * c[None, :]) % HEAD_DIM) / HEAD_DIM
    eye = np.eye(4)
    cs = np.concatenate([np.kron(eye, np.cos(ang)), np.kron(eye, np.sin(ang))], axis=1)
    return jnp.asarray(cs, BF16)


def _group_mean_matrix():
    bd = np.kron(np.eye(LANES // HEAD_DIM), np.full((HEAD_DIM, HEAD_DIM), 1.0 / HEAD_DIM))
    return jnp.asarray(bd, BF16)


def _dup_heads(wk, n_heads):
    d = wk.shape[0]
    return jnp.broadcast_to(wk.reshape(d, n_heads, 1, HEAD_DIM), (d, n_heads, 2, HEAD_DIM)).reshape(d, n_heads * LANES)


def _lambda_init(layer):
    return 0.8 - 0.6 * math.exp(-0.3 * layer)


def _moe_postnorm(routed, mlat, mctx, lng, lnb, nct, alpha, layer, w_gu, b_gu, w_down, b_down):
    x1, tg, n8, pos, xs, ys = routed
    n_exp = w_gu.shape[1]
    b, t, _ = x1.shape
    place = _placement(n8[:, :, 0], b * t * TOP_K)
    ys = _moe_experts(place, layer, xs, ys, w_gu, b_gu, w_down, b_down)
    pos_t, tg_t = jnp.transpose(pos, (0, 2, 1)), jnp.transpose(tg, (0, 2, 1))
    return _combine_postnorm(ys, pos_t, tg_t, x1, mlat, mctx, lng, lnb, nct, alpha, n_exp)


def kernel(x, c, ctx, c_ctx, mod_w, mod_b, ln_g, ln_b, ab_w_in, ab_sink, ab_w_out,
           cd_w_in, cd_lambda, cd_subln_g, cd_q_norm_g, cd_k_norm_g, cd_w_out,
           router_w, router_b, expert_w_gu, expert_b_gu, expert_w_down, expert_b_down):
    b, s, d = x.shape
    n_ctx = ctx.shape[1]
    depth = mod_w.shape[0]
    n_exp = router_w.shape[-1]
    assert d == 16 * HEAD_DIM and n_ctx % TM == 0 and s % TM == 0 and s % GRID_W == 0
    nct = n_ctx // TM
    alpha = (2 * depth) ** 0.25

    cos, sin = _rope_tables(n_ctx, s)
    cs_dft = _channel_dft()
    bd = _group_mean_matrix()
    mods = _mod_vectors(c, c_ctx, mod_w, mod_b)

    xs, xc = x, ctx
    for l in range(depth):
        last = l == depth - 1
        i = l // 2
        mlat = mods[l, :b].reshape(b, 1, N_MOD * d)
        mctx = mods[l, b:b + 1]
        lng1, lnb1 = ln_g[l, 0].reshape(1, d), ln_b[l, 0].reshape(1, d)
        lng2, lnb2 = ln_g[l, 1].reshape(1, d), ln_b[l, 1].reshape(1, d)
        rwt = router_w[l].T.astype(BF16)
        rb = router_b[l].reshape(n_exp, 1)
        if l % 2 == 0:
            w = ab_w_in[i]
            w_n = jnp.concatenate([w[:, :1024], _dup_heads(w[:, 1024:1280], 4)], axis=1).astype(BF16)
            wvt = w[:, 1280:1536].T.astype(BF16)
            uw, q, k, vt = _proj_ab(xs, xc, mlat, mctx, w_n, wvt, cos, sin, cs_dft, nct)
            oa = jnp.concatenate([_fourier(uw[:, :n_ctx]), _fourier(uw[:, n_ctx:])], axis=1)
            ob = _win_attn(ab_sink[i], q, k, vt, nct)
            if last:
                parts = [oa[:, n_ctx:], ob[:, n_ctx:]]
            else:
                parts = [oa, ob]
            w_out = ab_w_out[i].astype(BF16)
        else:
            w = cd_w_in[i]
            w_n = jnp.concatenate([w[:, :1536], _dup_heads(w[:, 1536:1664], 2)], axis=1).astype(BF16)
            wvt = w[:, 1664:2304].T.astype(BF16)
            gq = jnp.tile(cd_q_norm_g[i], 2).reshape(1, LANES)
            gk = jnp.tile(cd_k_norm_g[i], 2).reshape(1, LANES)
            qc, qd, kc, kd, vt = _proj_cd(xs, xc, mlat, mctx, w_n, wvt, cos, sin, bd, gq, gk, nct)
            sg = cd_subln_g[i].reshape(LANES, 1)
            o_lat = _attn_cd(cd_lambda[i], sg, qc, qd, kc, kd, vt, nct, _lambda_init(l))
            if last:
                parts = [o_lat]
            else:
                raise NotImplementedError("context outputs of a differential/axial layer")
            w_out = cd_w_out[i].astype(BF16)
        experts = (l, expert_w_gu, expert_b_gu, expert_w_down, expert_b_down)
        if last:
            routed = _oproj(parts, w_out, xs, xc, mlat, mctx, lng1, lnb1, rwt, rb, 0, nct, alpha)
            return _moe_postnorm(routed, mlat, mctx, lng2, lnb2, 0, alpha, *experts)
        routed = _oproj(parts, w_out, xs, xc, mlat, mctx, lng1, lnb1, rwt, rb, nct, 0, alpha)
        xs, xc = _moe_postnorm(routed, mlat, mctx, lng2, lnb2, nct, alpha, *experts), None
    return xs[:, n_ctx:]
```

```python
import functools
import math

import jax
import jax.numpy as jnp
import numpy as np
from jax import lax
from jax.experimental import pallas as pl
from jax.experimental.pallas import tpu as pltpu

F32 = jnp.float32
BF16 = jnp.bfloat16

HEAD_DIM = 64
GRID_W = 64
WINDOW = 128
ROPE_THETA = 10000.0
LN_EPS = 1e-6
RMS_EPS = 1e-6
N_MOD = 6
TOP_K = 4
SWIGLU_LIMIT = 7.0
SWIGLU_ALPHA = 1.702
NEG = -1e30
LOG2E = 1.4426950408889634
QSCALE = HEAD_DIM ** -0.5 * LOG2E

N_C_HEADS = 4
N_D_HEADS = 8
N_D_KV = 2
N_UNITS = 2 * N_C_HEADS + N_D_HEADS
ONES_ROWS = 16
_VT_LAYOUT = tuple((h * (2 * HEAD_DIM + ONES_ROWS), 2 * HEAD_DIM) for h in range(N_C_HEADS)) + tuple(
    (N_C_HEADS * (2 * HEAD_DIM + ONES_ROWS) + g * (HEAD_DIM + ONES_ROWS), HEAD_DIM) for g in range(N_D_KV))
VT_ROWS = _VT_LAYOUT[-1][0] + HEAD_DIM + ONES_ROWS

LANES = 128
TM = 256
MOE_BM = 512
ROW_GROUP = 8
SMEM_BLOCK = 1024
WIN_SCORES_AHEAD = 6
VMEM_LIMIT = 56 * 1024 * 1024


def _dot(a, b):
    return jnp.dot(a, b, preferred_element_type=F32)


def _dot_nt(a, b):
    return lax.dot_general(a, b, (((1,), (1,)), ((), ())), preferred_element_type=F32)


def _cparams(sem):
    return pltpu.CompilerParams(dimension_semantics=sem, vmem_limit_bytes=VMEM_LIMIT)


def _layernorm(z, g, b):
    mu = jnp.mean(z, axis=-1, keepdims=True)
    d = z - mu
    var = jnp.mean(d * d, axis=-1, keepdims=True)
    return d * lax.rsqrt(var + LN_EPS) * g + b


def _rope128(x, cos, sin_signed):
    lane = lax.broadcasted_iota(jnp.int32, x.shape, 1)
    lo = (lane % 32) < 16
    rot = jnp.where(lo, pltpu.roll(x, LANES - 16, 1), pltpu.roll(x, 16, 1))
    return x * cos + rot * sin_signed


def _rmsnorm128(x, g, bd):
    x2 = x * x
    hi = x2.astype(BF16)
    lo = (x2 - hi.astype(F32)).astype(BF16)
    ms = _dot(hi, bd) + _dot(lo, bd)
    return x * lax.rsqrt(ms + RMS_EPS) * g


U32 = jnp.uint32
_HI16 = 0xFFFF0000


def _pack_bf16_pairs(x):
    n = x.shape[1] // 2
    bits = lax.bitcast_convert_type(x, U32)
    return (bits[:, :n] >> 16) | (bits[:, n:] & U32(_HI16))


def _unpack_bf16_pairs(p):
    lo = lax.bitcast_convert_type(p << 16, F32).astype(BF16)
    hi = lax.bitcast_convert_type(p & U32(_HI16), F32).astype(BF16)
    return lo, hi


def _modulated(x_ref, ctx_ref, mlat_ref, mctx_ref, is_ctx, d):
    m = jnp.where(is_ctx, mctx_ref[...], mlat_ref[0])
    x = jnp.where(is_ctx, ctx_ref[0], x_ref[0])
    return (x * (1.0 + m[:, d:2 * d]) + m[:, 0:d]).astype(BF16)


def _stream_specs(x, ctx, nct, d, off=0, sub=0, n_sub=1):
    last = x.shape[1] // TM - 1
    tile = lambda j: n_sub * j + sub
    if ctx is None:
        return (x, x), [pl.BlockSpec((1, TM, d), lambda bi, j: (bi, jnp.minimum(tile(j) + off, last), 0)),
                        pl.BlockSpec((1, TM, d), lambda bi, j: (bi, jnp.minimum(tile(j), max(nct - 1, 0)), 0))]
    return (x, ctx), [pl.BlockSpec((1, TM, d), lambda bi, j: (bi, jnp.clip(tile(j) - nct, 0, last), 0)),
                      pl.BlockSpec((1, TM, d), lambda bi, j: (bi, jnp.minimum(tile(j), nct - 1), 0))]


def _mod_kernel(c_ref, w_ref, b_ref, o_ref):
    c = c_ref[...]
    s = (c / (1.0 + jnp.exp(-c))).astype(BF16)
    o_ref[0] = _dot(s, w_ref[0].astype(BF16)) + b_ref[0]


def _mod_vectors(c, c_ctx, mod_w, mod_b):
    depth, d, n = mod_w.shape
    b = c.shape[0]
    rows = 8 * (-(-(b + 1) // 8))
    cs = jnp.zeros((rows, d), F32).at[:b].set(c).at[b].set(c_ctx)
    tn = 1536
    out = pl.pallas_call(
        _mod_kernel,
        grid=(depth, n // tn),
        in_specs=[pl.BlockSpec((rows, d), lambda l, j: (0, 0)),
                  pl.BlockSpec((1, d, tn), lambda l, j: (l, 0, j)),
                  pl.BlockSpec((1, 1, tn), lambda l, j: (l, 0, j))],
        out_specs=pl.BlockSpec((1, rows, tn), lambda l, j: (l, 0, j)),
        out_shape=jax.ShapeDtypeStruct((depth, rows, n), F32),
        compiler_params=_cparams(("parallel", "parallel")),
        name="mod_vectors",
    )(cs, mod_w, mod_b.reshape(depth, 1, n))
    return out


def _proj_ab_kernel(nct, x_ref, ctx_ref, mlat_ref, mctx_ref, w_ref, wvt_ref, cos_ref, sin_ref, cs_ref,
                    uw_ref, q_ref, k_ref, vt_ref):
    is_ctx = pl.program_id(1) < nct
    d = x_ref.shape[-1]
    h = _modulated(x_ref, ctx_ref, mlat_ref, mctx_ref, is_ctx, d)
    p = _dot(h, w_ref[...])
    uw_ref[0] = _dot(p[:, 0:256].astype(BF16), cs_ref[...]).astype(BF16)
    cos = cos_ref[...]
    sin = sin_ref[...]
    for j in range(6):
        c0 = 256 + j * LANES
        q_ref[0, :, j * LANES:(j + 1) * LANES] = (_rope128(p[:, c0:c0 + LANES], cos, sin) * QSCALE).astype(BF16)
    for j in range(4):
        c0 = 1024 + j * LANES
        k_ref[0, :, j * LANES:(j + 1) * LANES] = _rope128(p[:, c0:c0 + LANES], cos, sin).astype(BF16)
    vt = _dot_nt(wvt_ref[...], h).astype(BF16)
    for c in range(TM // LANES):
        vt_ref[0, c] = vt[:, c * LANES:(c + 1) * LANES]


def _proj_ab(x, ctx, mlat, mctx, w, wvt, cos, sin, cs, nct):
    b, d = x.shape[0], x.shape[2]
    tall = x.shape[1] + (0 if ctx is None else ctx.shape[1])
    stream, stream_specs = _stream_specs(x, ctx, nct, d)
    nt = tall // TM
    nw = w.shape[1]
    nchunk = tall // LANES
    cpt = TM // LANES
    return pl.pallas_call(
        functools.partial(_proj_ab_kernel, nct),
        grid=(b, nt),
        in_specs=stream_specs + [
                  pl.BlockSpec((1, 1, N_MOD * d), lambda bi, i: (bi, 0, 0)),
                  pl.BlockSpec((1, N_MOD * d), lambda bi, i: (0, 0)),
                  pl.BlockSpec((d, nw), lambda bi, i: (0, 0)),
                  pl.BlockSpec((256, d), lambda bi, i: (0, 0)),
                  pl.BlockSpec((TM, LANES), lambda bi, i: (i, 0)),
                  pl.BlockSpec((TM, LANES), lambda bi, i: (i, 0)),
                  pl.BlockSpec((256, 512), lambda bi, i: (0, 0))],
        out_specs=[pl.BlockSpec((1, TM, 512), lambda bi, i: (bi, i, 0)),
                   pl.BlockSpec((1, TM, 768), lambda bi, i: (bi, i, 0)),
                   pl.BlockSpec((1, TM, 512), lambda bi, i: (bi, i, 0)),
                   pl.BlockSpec((1, cpt, 256, LANES), lambda bi, i: (bi, i, 0, 0))],
        out_shape=[jax.ShapeDtypeStruct((b, tall, 512), BF16),
                   jax.ShapeDtypeStruct((b, tall, 768), BF16),
                   jax.ShapeDtypeStruct((b, tall, 512), BF16),
                   jax.ShapeDtypeStruct((b, nchunk, 256, LANES), BF16)],
        compiler_params=_cparams(("parallel", "parallel")),
        name="proj_ab",
    )(*stream, mlat, mctx, w, wvt, cos, sin, cs)


def _proj_cd_kernel(nct, x_ref, ctx_ref, mlat_ref, mctx_ref, w_ref, wvt_ref, cos_ref, sin_ref, bd_ref,
                    gq_ref, gk_ref, qc_ref, qd_ref, kc_ref, kd_ref, vt_ref):
    is_ctx = pl.program_id(1) < nct
    d = x_ref.shape[-1]
    h = _modulated(x_ref, ctx_ref, mlat_ref, mctx_ref, is_ctx, d)
    p = _dot(h, w_ref[...])
    cos = cos_ref[...]
    sin = sin_ref[...]
    bd = bd_ref[...]
    gq = gq_ref[...]
    gk = gk_ref[...]
    for j in range(4):
        sl = slice(j * LANES, (j + 1) * LANES)
        qc_ref[0, :, sl] = (_rope128(p[:, j * LANES:(j + 1) * LANES], cos, sin) * QSCALE).astype(BF16)
        c0 = 512 + j * LANES
        qd_ref[0, :, sl] = (_rope128(_rmsnorm128(p[:, c0:c0 + LANES], gq, bd), cos, sin) * QSCALE).astype(BF16)
        c0 = 1024 + j * LANES
        kc_ref[0, :, sl] = _rope128(p[:, c0:c0 + LANES], cos, sin).astype(BF16)
    for j in range(2):
        c0 = 1536 + j * LANES
        kd_ref[0, :, j * LANES:(j + 1) * LANES] = _rope128(
            _rmsnorm128(p[:, c0:c0 + LANES], gk, bd), cos, sin).astype(BF16)
    vt = _dot_nt(wvt_ref[...], h).astype(BF16)
    ones = jnp.ones((ONES_ROWS, vt.shape[1]), BF16)
    src = 0
    for dst, dv in _VT_LAYOUT:
        vt_ref[0, 0, dst:dst + dv, :] = vt[src:src + dv, :]
        vt_ref[0, 0, dst + dv:dst + dv + ONES_ROWS, :] = ones
        src += dv


def _proj_cd(x, ctx, mlat, mctx, w, wvt, cos, sin, bd, gq, gk, nct):
    b, d = x.shape[0], x.shape[2]
    tall = x.shape[1] + (0 if ctx is None else ctx.shape[1])
    stream, stream_specs = _stream_specs(x, ctx, nct, d)
    nt = tall // TM
    nw = w.shape[1]
    nv = VT_ROWS
    full = lambda bi, i: (0, 0)
    tile = lambda bi, i: (bi, i, 0)
    return pl.pallas_call(
        functools.partial(_proj_cd_kernel, nct),
        grid=(b, nt),
        in_specs=stream_specs + [
                  pl.BlockSpec((1, 1, N_MOD * d), lambda bi, i: (bi, 0, 0)),
                  pl.BlockSpec((1, N_MOD * d), full),
                  pl.BlockSpec((d, nw), full),
                  pl.BlockSpec(wvt.shape, full),
                  pl.BlockSpec((TM, LANES), lambda bi, i: (i, 0)),
                  pl.BlockSpec((TM, LANES), lambda bi, i: (i, 0)),
                  pl.BlockSpec((LANES, LANES), full),
                  pl.BlockSpec((1, LANES), full),
                  pl.BlockSpec((1, LANES), full)],
        out_specs=[pl.BlockSpec((1, TM, 512), tile),
                   pl.BlockSpec((1, TM, 512), tile),
                   pl.BlockSpec((1, TM, 512), tile),
                   pl.BlockSpec((1, TM, 256), tile),
                   pl.BlockSpec((1, 1, nv, TM), lambda bi, i: (bi, i, 0, 0))],
        out_shape=[jax.ShapeDtypeStruct((b, tall, 512), BF16),
                   jax.ShapeDtypeStruct((b, tall, 512), BF16),
                   jax.ShapeDtypeStruct((b, tall, 512), BF16),
                   jax.ShapeDtypeStruct((b, tall, 256), BF16),
                   jax.ShapeDtypeStruct((b, nt, nv, TM), BF16)],
        compiler_params=_cparams(("parallel", "parallel")),
        name="proj_cd",
    )(*stream, mlat, mctx, w, wvt, cos, sin, bd, gq, gk)


def _fourier_kernel(nb, scale, uw_ref, ca_ref, sa_ref, cb_ref, sb_ref, o_ref):
    j = pl.program_id(0)
    ca = ca_ref[pl.ds(j, 1), :]
    sa = sa_ref[pl.ds(j, 1), :]
    cb = cb_ref[...]
    sb = sb_ref[...]
    ct = (ca * cb - sa * sb).astype(BF16)
    nst = (-(sa * cb + ca * sb)).astype(BF16)
    for bi in range(nb):
        acc = _dot(ct, uw_ref[bi, :, 0:256]) + _dot(nst, uw_ref[bi, :, 256:512])
        o_ref[bi] = (acc * scale).astype(BF16)


def _dft_tables(t, tmf):
    k = np.arange(t, dtype=np.int64)
    j1 = np.arange(t // tmf, dtype=np.int64) * tmf
    j0 = np.arange(tmf, dtype=np.int64)
    aa = (2.0 * np.pi / t) * ((j1[:, None] * k[None, :]) % t)
    ab = (2.0 * np.pi / t) * ((j0[:, None] * k[None, :]) % t)
    f = lambda a: jnp.asarray(a, F32)
    return f(np.cos(aa)), f(np.sin(aa)), f(np.cos(ab)), f(np.sin(ab))


def _fourier(uw):
    b, t, _ = uw.shape
    tmf = min(128, t)
    ca, sa, cb, sb = _dft_tables(t, tmf)
    scale = 1.0 / math.sqrt(t * HEAD_DIM)
    full2 = lambda j: (0, 0)
    return pl.pallas_call(
        functools.partial(_fourier_kernel, b, scale),
        grid=(t // tmf,),
        in_specs=[pl.BlockSpec((b, t, 512), lambda j: (0, 0, 0)),
                  pl.BlockSpec((t // tmf, t), full2),
                  pl.BlockSpec((t // tmf, t), full2),
                  pl.BlockSpec((tmf, t), full2),
                  pl.BlockSpec((tmf, t), full2)],
        out_specs=pl.BlockSpec((b, tmf, 256), lambda j: (0, j, 0)),
        out_shape=jax.ShapeDtypeStruct((b, t, 256), BF16),
        compiler_params=_cparams(("parallel",)),
        name="fourier_mix",
    )(uw, ca, sa, cb, sb)


def _win_attn_kernel(nct, nchunk, sink_ref, q_ref, k_ref, vt_ref, o_ref, *bufs):
    i = pl.program_id(1)
    is_ctx = i < nct
    j = i - nct
    ctx_rows = nct * TM
    ctx_chunks = ctx_rows // LANES
    lw = TM + 2 * WINDOW
    lchunks = lw // LANES
    cs = jnp.clip(ctx_chunks + (TM // LANES) * j - WINDOW // LANES, 0, nchunk - lchunks)
    rs = pl.multiple_of(cs * LANES, LANES)
    qpos = j * TM + lax.broadcasted_iota(jnp.int32, (1, TM), 1)
    kpos = cs * LANES - ctx_rows + lax.broadcasted_iota(jnp.int32, (lw, 1), 0)
    allowed = (jnp.abs(qpos - kpos) <= WINDOW) & (kpos >= 0) & jnp.logical_not(is_ctx)
    lane = lax.broadcasted_iota(jnp.int32, (TM, LANES), 1)
    n_heads = q_ref.shape[-1] // HEAD_DIM
    group = n_heads // (k_ref.shape[-1] // LANES)

    def scores(hq, dst_s):
        pair, half = divmod(hq, 2)
        g = hq // group
        qp = q_ref[0, :, pair * LANES:(pair + 1) * LANES]
        qm = jnp.where((lane >= HEAD_DIM) == (half == 1), qp, jnp.zeros_like(qp))
        dst_s[0:ctx_rows, :] = _dot_nt(k_ref[0, 0:ctx_rows, g * LANES:(g + 1) * LANES], qm)
        s_l = _dot_nt(k_ref[0, pl.ds(rs, lw), g * LANES:(g + 1) * LANES], qm)
        dst_s[ctx_rows:ctx_rows + lw, :] = jnp.where(allowed, s_l, NEG)

    def consume(hq, src_s):
        g = hq // group
        s = src_s[...]
        sk = sink_ref[hq] * LOG2E
        m = jnp.maximum(jnp.max(s, axis=0, keepdims=True), sk)
        p = jnp.exp2(s - m)
        l = jnp.sum(p, axis=0, keepdims=True) + jnp.exp2(sk - m)
        p = p.astype(BF16)
        vrows = slice(g * HEAD_DIM, (g + 1) * HEAD_DIM)
        acc = jnp.zeros((HEAD_DIM, TM), F32)
        for c in range(ctx_chunks):
            acc = acc + _dot(vt_ref[0, c, vrows, :], p[c * LANES:(c + 1) * LANES, :])
        for c in range(lchunks):
            r0 = ctx_rows + c * LANES
            acc = acc + _dot(vt_ref[0, cs + c, vrows, :], p[r0:r0 + LANES, :])
        return acc / l

    ahead = len(bufs) - 1
    for hq in range(min(ahead, n_heads)):
        scores(hq, bufs[hq])
    outs = []
    for hq in range(n_heads):
        if hq + ahead < n_heads:
            scores(hq + ahead, bufs[(hq + ahead) % len(bufs)])
        outs.append(consume(hq, bufs[hq % len(bufs)]))
        if hq % 2 == 1:
            pair = hq // 2
            o2 = jnp.concatenate(outs, axis=0)
            o_ref[0, :, pair * LANES:(pair + 1) * LANES] = o2.T.astype(BF16)
            outs = []


def _win_attn(sink, q, k, vt, nct):
    b, tall, qw = q.shape
    nt = tall // TM
    nchunk = vt.shape[1]
    return pl.pallas_call(
        functools.partial(_win_attn_kernel, nct, nchunk),
        grid=(b, nt),
        in_specs=[pl.BlockSpec(memory_space=pltpu.SMEM),
                  pl.BlockSpec((1, TM, qw), lambda bi, i: (bi, i, 0)),
                  pl.BlockSpec((1, tall, k.shape[-1]), lambda bi, i: (bi, 0, 0)),
                  pl.BlockSpec((1, nchunk, vt.shape[2], LANES), lambda bi, i: (bi, 0, 0, 0))],
        out_specs=pl.BlockSpec((1, TM, qw), lambda bi, i: (bi, i, 0)),
        out_shape=jax.ShapeDtypeStruct((b, tall, qw), BF16),
        scratch_shapes=[pltpu.VMEM((nct * TM + TM + 2 * WINDOW, TM), F32)] * (WIN_SCORES_AHEAD + 1),
        compiler_params=_cparams(("parallel", "parallel")),
        name="window_attention",
    )(sink, q, k, vt)


def _unit_operands(u):
    if u < 2 * N_C_HEADS:
        hc = u // 2
        dst, dv = _VT_LAYOUT[hc]
        return True, hc, dst, dv
    g = (u - 2 * N_C_HEADS) // (N_D_HEADS // N_D_KV)
    dst, dv = _VT_LAYOUT[N_C_HEADS + g]
    return False, g, dst, dv


def _attn_cd_kernel(nkt, lam_init, lam_ref, sg_ref, qc_ref, qd_ref, kc_ref, kd_ref, vt_ref, o_ref,
                    qm_s, m_s, acc_s, s_a, s_b):
    mq = qc_ref.shape[1]
    tk = vt_ref.shape[-1]
    lane = lax.broadcasted_iota(jnp.int32, (mq, LANES), 1)
    upper = lane >= HEAD_DIM
    for u in range(N_UNITS):
        src = qc_ref if u < 2 * N_C_HEADS else qd_ref
        pair = (u if u < 2 * N_C_HEADS else u - 2 * N_C_HEADS) // 2
        qp = src[0, :, pair * LANES:(pair + 1) * LANES]
        qm_s[u] = jnp.where(upper == (u % 2 == 1), qp, jnp.zeros_like(qp))
    m_s[...] = jnp.full(m_s.shape, NEG, F32)
    acc_s[...] = jnp.zeros(acc_s.shape, F32)

    def scores(tile, u, dst_s):
        is_c, slab, _, _ = _unit_operands(u)
        k_ref = kc_ref if is_c else kd_ref
        r0 = pl.multiple_of(tile * tk, tk)
        dst_s[u] = _dot_nt(k_ref[0, pl.ds(r0, tk), slab * LANES:(slab + 1) * LANES], qm_s[u])

    def consume(tile, u, src_s):
        _, _, row0, dv = _unit_operands(u)
        rows = dv + ONES_ROWS
        m_old = m_s[u]
        m_new = jnp.maximum(m_old, jnp.max(src_s[u], axis=0, keepdims=True))
        alpha = jnp.exp2(m_old - m_new)
        p = jnp.exp2(src_s[u] - m_new).astype(BF16)
        pv = _dot(vt_ref[0, tile, row0:row0 + rows, :], p)
        acc_s[u, 0:rows, :] = acc_s[u, 0:rows, :] * alpha + pv
        m_s[u] = m_new

    def stage(tile, src_s, dst_s):
        nxt = jnp.minimum(tile + 1, nkt - 1)
        for u in range(N_UNITS):
            scores(nxt, u, dst_s)
            consume(tile, u, src_s)

    for u in range(N_UNITS):
        scores(0, u, s_a)
    stage(0, s_a, s_b)

    bufs = (s_a, s_b)
    per_trip = 8

    def body(trip, carry):
        for j in range(per_trip):
            tile = per_trip * trip + 1 + j
            stage(tile, bufs[(1 + j) % 2], bufs[j % 2])
        return carry

    n_trips = (nkt - 1) // per_trip
    lax.fori_loop(0, n_trips, body, 0)
    for tile in range(1 + n_trips * per_trip, nkt):
        stage(tile, bufs[tile % 2], bufs[(tile + 1) % 2])

    lv = lam_ref[...]
    lam = (jnp.exp(jnp.sum(lv[0:1] * lv[1:2], axis=-1, keepdims=True))
           - jnp.exp(jnp.sum(lv[2:3] * lv[3:4], axis=-1, keepdims=True)) + lam_init)
    sg = sg_ref[...]
    dvc = 2 * HEAD_DIM
    for hc in range(N_C_HEADS):
        o1 = acc_s[2 * hc, 0:dvc, :] / acc_s[2 * hc, dvc:dvc + 1, :]
        o2 = acc_s[2 * hc + 1, 0:dvc, :] / acc_s[2 * hc + 1, dvc:dvc + 1, :]
        o = o1 - lam * o2
        ms = jnp.mean(o * o, axis=0, keepdims=True)
        o = o * lax.rsqrt(ms + RMS_EPS) * sg * (1.0 - lam_init)
        o_ref[0, :, hc * LANES:(hc + 1) * LANES] = o.T.astype(BF16)
    for pair in range(N_D_HEADS // 2):
        outs = []
        for half in range(2):
            u = 2 * N_C_HEADS + 2 * pair + half
            outs.append(acc_s[u, 0:HEAD_DIM, :] / acc_s[u, HEAD_DIM:HEAD_DIM + 1, :])
        o2 = jnp.concatenate(outs, axis=0)
        c0 = N_C_HEADS * LANES + pair * LANES
        o_ref[0, :, c0:c0 + LANES] = o2.T.astype(BF16)


def _attn_cd(lam_vec, sg, qc, qd, kc, kd, vt, nct, lam_init):
    b, tall, _ = qc.shape
    nkt, nv, tk = vt.shape[1], vt.shape[2], vt.shape[3]
    mq = TM
    nq = tall // mq - nct
    ow = N_C_HEADS * LANES + N_D_HEADS * HEAD_DIM
    qtile = lambda bi, i: (bi, i + nct, 0)
    return pl.pallas_call(
        functools.partial(_attn_cd_kernel, nkt, lam_init),
        grid=(b, nq),
        in_specs=[pl.BlockSpec((4, HEAD_DIM), lambda bi, i: (0, 0)),
                  pl.BlockSpec((LANES, 1), lambda bi, i: (0, 0)),
                  pl.BlockSpec((1, mq, qc.shape[-1]), qtile),
                  pl.BlockSpec((1, mq, qd.shape[-1]), qtile),
                  pl.BlockSpec((1, tall, kc.shape[-1]), lambda bi, i: (bi, 0, 0)),
                  pl.BlockSpec((1, tall, kd.shape[-1]), lambda bi, i: (bi, 0, 0)),
                  pl.BlockSpec((1, nkt, nv, tk), lambda bi, i: (bi, 0, 0, 0))],
        out_specs=pl.BlockSpec((1, mq, ow), lambda bi, i: (bi, i, 0)),
        out_shape=jax.ShapeDtypeStruct((b, nq * mq, ow), BF16),
        scratch_shapes=[pltpu.VMEM((N_UNITS, mq, LANES), BF16),
                        pltpu.VMEM((N_UNITS, 1, mq), F32),
                        pltpu.VMEM((N_UNITS, 2 * HEAD_DIM + ONES_ROWS, mq), F32),
                        pltpu.VMEM((N_UNITS, tk, mq), F32),
                        pltpu.VMEM((N_UNITS, tk, mq), F32)],
        compiler_params=_cparams(("parallel", "parallel")),
        name="attention_cd",
    )(lam_vec, sg, qc, qd, kc, kd, vt)


OPROJ_SUB = 2


def _oproj_kernel(n_parts, widths, nct, nt, alpha, n_exp, *refs):
    o_refs = refs[:n_parts]
    w_ref = refs[n_parts]
    stream = refs[n_parts + 1:n_parts + 1 + 2 * OPROJ_SUB]
    (mlat_ref, mctx_ref, lng_ref, lnb_ref, rwt_ref, rb_ref, tri_ref, ltri_ref,
     x1_ref, tg_ref, n8_ref, pos_ref, xs_ref, ys_hbm, zero_s, sem) = refs[n_parts + 1 + 2 * OPROJ_SUB:]
    d = x1_ref.shape[-1]
    tm = TM
    sr = xs_ref.shape[1] // OPROJ_SUB
    bi = pl.program_id(0)
    tiles = [OPROJ_SUB * pl.program_id(1) + s for s in range(OPROJ_SUB)]
    zero_s[...] = jnp.zeros(zero_s.shape, zero_s.dtype)
    clears = []
    for s, i in enumerate(tiles):
        tail0 = pl.multiple_of((bi * nt + i) * sr + TOP_K * tm, ROW_GROUP)
        clears.append(pltpu.make_async_copy(zero_s, ys_hbm.at[pl.ds(tail0, zero_s.shape[0])], sem.at[s]))
        pl.when(i < nt)(clears[s].start)
    state = [dict() for _ in tiles]

    def project(s, i):
        st, rows = state[s], slice(s * tm, (s + 1) * tm)
        is_ctx = i < nct
        y = None
        r0 = 0
        for o_ref, wd in zip(o_refs, widths):
            part = _dot(o_ref[0, rows, :], w_ref[r0:r0 + wd, :])
            y = part if y is None else y + part
            r0 += wd
        st["m"] = jnp.where(is_ctx, mctx_ref[...], mlat_ref[0])
        st["z"] = alpha * jnp.where(is_ctx, stream[2 * s + 1][0], stream[2 * s][0]) + st["m"][:, 2 * d:3 * d] * y

    def normalise(s, i):
        st, rows = state[s], slice(s * tm, (s + 1) * tm)
        m = st["m"]
        x1 = _layernorm(st["z"], lng_ref[...], lnb_ref[...])
        x1_ref[0, rows, :] = x1
        st["hb"] = (x1 * (1.0 + m[:, 4 * d:5 * d]) + m[:, 3 * d:4 * d]).astype(BF16)

    def route(s, i):
        st, rows = state[s], slice(s * tm, (s + 1) * tm)
        logits = _dot_nt(rwt_ref[...], st["hb"]) + rb_ref[...]
        row = lax.broadcasted_iota(jnp.int32, logits.shape, 0)
        vals, idxs = [], []
        for _ in range(TOP_K):
            mx = jnp.max(logits, axis=0, keepdims=True)
            idx = jnp.min(jnp.where(logits == mx, row, n_exp), axis=0, keepdims=True)
            vals.append(mx)
            idxs.append(idx)
            logits = jnp.where(row == idx, NEG, logits)
        es = [jnp.exp(v - vals[0]) for v in vals]
        tot = es[0] + es[1] + es[2] + es[3]
        pad_i = jnp.zeros((8 - TOP_K, tm), jnp.int32)
        pad_f = jnp.zeros((8 - TOP_K, tm), F32)
        te = jnp.concatenate(idxs + [pad_i], axis=0)
        tg_ref[0, :, rows] = jnp.concatenate([e / tot for e in es] + [pad_f], axis=0)
        _, oh = _expert_onehot(te, n_exp)
        n8 = _groups(jnp.sum(oh, axis=1, keepdims=True))
        n8_ref[0, s] = jnp.broadcast_to(n8, (n_exp, LANES)).astype(jnp.int32)
        lbase = _dot(ltri_ref[...], jnp.broadcast_to(n8, (n_exp, LANES)).astype(BF16))[:, 0:1]
        before = _dot(oh.astype(BF16), tri_ref[...])
        where_to = ROW_GROUP * lbase + before
        pos = [jnp.sum(jnp.where(row == te[k:k + 1, :], where_to, 0.0), axis=0, keepdims=True).astype(jnp.int32)
               for k in range(TOP_K)]
        pos_ref[0, :, rows] = jnp.concatenate(pos + [pad_i], axis=0)
        st["pos"] = pos

    def sort_chunk(s, c):
        st = state[s]
        r = lax.broadcasted_iota(jnp.int32, (tm, tm), 0) + c * tm
        hit = r == st["pos"][0]
        for k in range(1, TOP_K):
            hit = hit | (r == st["pos"][k])
        packed = _pack_bf16_pairs(_dot(jnp.where(hit, 1.0, 0.0).astype(BF16), st["hb"]))
        xs_ref[0, s * sr + c * tm:s * sr + (c + 1) * tm, :] = packed

    fronts = (project, normalise, route)
    n_chunks = sr // tm
    for f in fronts:
        f(0, tiles[0])
    for s in range(1, len(tiles) + 1):
        nxt = [functools.partial(f, s, tiles[s]) for f in fronts] if s < len(tiles) else []
        for c in range(n_chunks):
            sort_chunk(s - 1, c)
            if c < len(nxt):
                nxt[c]()
        for f in nxt[n_chunks:]:
            f()
    for s, i in enumerate(tiles):
        pl.when(i < nt)(clears[s].wait)


def _oproj(parts, w_out, x, ctx, mlat, mctx, lng, lnb, rwt, rb, nct_out, x_tile_off, alpha):
    b, t, _ = parts[0].shape
    d = x.shape[-1]
    nt = t // TM
    n_exp = rwt.shape[0]
    widths = tuple(p.shape[-1] for p in parts)
    tile = lambda bi, j: (bi, j, 0)
    full = lambda bi, j: (0, 0)
    ns = OPROJ_SUB
    stream, stream_specs = [], []
    for sub in range(ns):
        arrs, specs = _stream_specs(x, ctx, nct_out, d, x_tile_off, sub, ns)
        stream += arrs
        stream_specs += specs
    in_specs = [pl.BlockSpec((1, ns * TM, wd), tile) for wd in widths] + [
        pl.BlockSpec(w_out.shape, full)] + stream_specs + [
        pl.BlockSpec((1, 1, N_MOD * d), lambda bi, i: (bi, 0, 0)),
        pl.BlockSpec((1, N_MOD * d), full),
        pl.BlockSpec((1, d), full),
        pl.BlockSpec((1, d), full),
        pl.BlockSpec((n_exp, d), full),
        pl.BlockSpec((n_exp, 1), full),
        pl.BlockSpec((TM, TM), full),
        pl.BlockSpec((n_exp, n_exp), full)]
    sr = _sort_rows(n_exp)
    tri = jnp.asarray(np.triu(np.ones((TM, TM)), 1), BF16)
    ltri = jnp.asarray(np.tril(np.ones((n_exp, n_exp)), -1), BF16)
    small = pl.BlockSpec((1, 8, ns * TM), lambda bi, j: (bi, 0, j))
    x1, tg, n8, pos, xs, ys = pl.pallas_call(
        functools.partial(_oproj_kernel, len(parts), widths, nct_out, nt, alpha, n_exp),
        grid=(b, -(-nt // ns)),
        in_specs=in_specs,
        out_specs=[pl.BlockSpec((1, ns * TM, d), tile),
                   small,
                   pl.BlockSpec((1, ns, n_exp, LANES), lambda bi, j: (bi, j, 0, 0)),
                   small,
                   pl.BlockSpec((1, ns * sr, d // 2), tile),
                   pl.BlockSpec(memory_space=pl.ANY)],
        out_shape=[jax.ShapeDtypeStruct((b, t, d), F32),
                   jax.ShapeDtypeStruct((b, 8, t), F32),
                   jax.ShapeDtypeStruct((b, nt, n_exp, LANES), jnp.int32),
                   jax.ShapeDtypeStruct((b, 8, t), jnp.int32),
                   jax.ShapeDtypeStruct((b, nt * sr, d // 2), U32),
                   jax.ShapeDtypeStruct(((b * nt + 1) * sr, d // 2), U32)],
        scratch_shapes=[pltpu.VMEM((sr - TOP_K * TM, d // 2), U32), pltpu.SemaphoreType.DMA((ns,))],
        compiler_params=_cparams(("parallel", "parallel")),
        name="out_proj_norm_router",
    )(*parts, w_out, *stream, mlat, mctx, lng, lnb, rwt, rb, tri, ltri)
    return x1, tg, n8.reshape(b * nt, n_exp, LANES), pos, xs.reshape(b * nt * sr, d // 2), ys


def _moe_kernel(be_ref, nx_ref, bv_ref, nu_ref, rprev_ref, rcur_ref, rnext_ref, xs_hbm, wgu_hbm, bgu_ref,
                wd_hbm, bd_ref, ys_in_hbm, ys_hbm, wgu_f, wd_f, wgu_s, wd_s, xbuf, ybuf, sem, gsem, ssem):
    del ys_in_hbm
    blk = pl.program_id(0)
    ff = wd_s.shape[0]
    gpb = xbuf.shape[1] // ROW_GROUP
    par = blk % 2

    def gathers(rec, buf, wait):
        if wait:
            pltpu.make_async_copy(xs_hbm.at[pl.ds(0, xbuf.shape[1])], xbuf.at[buf], gsem.at[buf]).wait()
            return
        for g in range(gpb):
            _group_copy(xs_hbm, rec[g], xbuf.at[buf], g, gsem.at[buf]).start(priority=g % 2)

    def scatters(rec, buf, wait):
        if wait:
            pltpu.make_async_copy(ybuf.at[buf], ys_hbm.at[pl.ds(0, ybuf.shape[1])], ssem.at[buf]).wait()
            return
        for g in range(gpb):
            _group_copy(ybuf.at[buf], g, ys_hbm, rec[gpb + g], ssem.at[buf]).start(priority=g % 2)

    def fetch(e):
        return (pltpu.make_async_copy(wgu_hbm.at[e], wgu_f, sem.at[0]),
                pltpu.make_async_copy(wd_hbm.at[e], wd_f, sem.at[1]))

    @pl.when(blk < nu_ref[0])
    def _():
        e = be_ref[blk]
        new_expert = (blk == 0) | (e != be_ref[jnp.maximum(blk - 1, 0)])

        @pl.when(blk == 0)
        def _():
            for cp in fetch(e):
                cp.start()

        @pl.when(new_expert)
        def _():
            for cp in fetch(e):
                cp.wait()
            rows = 128

            def cast_gu(r, c):
                r0 = pl.multiple_of(r * rows, rows)
                wgu_s[pl.ds(r0, rows), :] = wgu_f[pl.ds(r0, rows), :].astype(BF16)
                return c

            def cast_d(r, c):
                r0 = pl.multiple_of(r * rows, rows)
                wd_s[pl.ds(r0, rows), :] = wd_f[pl.ds(r0, rows), :].astype(BF16)
                return c

            lax.fori_loop(0, wgu_s.shape[0] // rows, cast_gu, 0)
            lax.fori_loop(0, wd_s.shape[0] // rows, cast_d, 0)

            @pl.when(nx_ref[blk] != e)
            def _():
                for cp in fetch(nx_ref[blk]):
                    cp.start()

        @pl.when(blk == 0)
        def _():
            gathers(rcur_ref, 0, False)

        gathers(rcur_ref, par, True)
        bm = xbuf.shape[1]

        def ffn(n_rows):
            gathers(rnext_ref, 1 - par, False)
            x = jnp.concatenate(_unpack_bf16_pairs(xbuf[par, 0:n_rows, :]), axis=1)
            gu = _dot(x, wgu_s[...]) + bgu_ref[0]
            g = jnp.minimum(gu[:, :ff], SWIGLU_LIMIT)
            u = jnp.clip(gu[:, ff:], -SWIGLU_LIMIT, SWIGLU_LIMIT)
            act = (u + 1.0) * (g / (1.0 + jnp.exp(-SWIGLU_ALPHA * g)))
            y = _dot(act.astype(BF16), wd_s[...]) + bd_ref[0]
            ybuf[par, 0:n_rows, :] = _pack_bf16_pairs(y.astype(BF16).astype(F32))
            if n_rows < bm:
                ybuf[par, n_rows:, :] = jnp.zeros((bm - n_rows, ybuf.shape[2]), ybuf.dtype)
            scatters(rcur_ref, par, False)

        quarter = bm // 4
        for n_rows in range(quarter, bm + 1, quarter):
            pl.when((bv_ref[blk] > n_rows - quarter) & (bv_ref[blk] <= n_rows))(functools.partial(ffn, n_rows))

        @pl.when(blk > 0)
        def _():
            scatters(rprev_ref, 1 - par, True)

        @pl.when(blk == nu_ref[0] - 1)
        def _():
            gathers(rnext_ref, 1 - par, True)
            scatters(rcur_ref, par, True)


def _moe_experts(place, layer, xs, ys, w_gu, b_gu, w_down, b_down):
    d = 2 * xs.shape[1]
    ff2 = w_gu.shape[-1]
    ff = w_down.shape[-2]
    off = layer * w_gu.shape[1]
    n_exp = w_gu.shape[0] * w_gu.shape[1]
    w_gu = w_gu.reshape(n_exp, d, ff2)
    w_down = w_down.reshape(n_exp, ff, d)
    nb = place["n_blocks"]
    last = lambda nu: nu[0] - 1
    bblk = lambda i, be, nx, bv, nu: (be[i], 0, 0)
    rec = lambda f: pl.BlockSpec((SMEM_BLOCK,), f, memory_space=pltpu.SMEM)
    grid_spec = pltpu.PrefetchScalarGridSpec(
        num_scalar_prefetch=4,
        grid=(nb,),
        in_specs=[rec(lambda i, be, nx, bv, nu: (jnp.clip(i - 1, 0, last(nu)),)),
                  rec(lambda i, be, nx, bv, nu: (jnp.minimum(i, last(nu)),)),
                  rec(lambda i, be, nx, bv, nu: (jnp.minimum(i + 1, last(nu)),)),
                  pl.BlockSpec(memory_space=pl.ANY),
                  pl.BlockSpec(memory_space=pl.ANY),
                  pl.BlockSpec((1, 1, ff2), bblk),
                  pl.BlockSpec(memory_space=pl.ANY),
                  pl.BlockSpec((1, 1, d), bblk),
                  pl.BlockSpec(memory_space=pl.ANY)],
        out_specs=pl.BlockSpec(memory_space=pl.ANY),
        scratch_shapes=[pltpu.VMEM((d, ff2), F32), pltpu.VMEM((ff, d), F32),
                        pltpu.VMEM((d, ff2), BF16), pltpu.VMEM((ff, d), BF16),
                        pltpu.VMEM((2, MOE_BM, d // 2), U32), pltpu.VMEM((2, MOE_BM, d // 2), U32),
                        pltpu.SemaphoreType.DMA((2,)), pltpu.SemaphoreType.DMA((2,)),
                        pltpu.SemaphoreType.DMA((2,))])
    records = place["rec"]
    return pl.pallas_call(
        _moe_kernel,
        grid_spec=grid_spec,
        out_shape=jax.ShapeDtypeStruct(ys.shape, ys.dtype),
        input_output_aliases={12: 0},
        compiler_params=_cparams(("arbitrary",)),
        name="moe_experts",
    )(place["block_e"] + off, place["block_next_e"] + off, place["block_rows"], place["n_used"],
      records, records, records, xs, w_gu, b_gu.reshape(n_exp, 1, ff2), w_down, b_down.reshape(n_exp, 1, d), ys)


def _expert_onehot(te, n_exp):
    row = lax.broadcasted_iota(jnp.int32, (n_exp, te.shape[1]), 0)
    oh = jnp.zeros(row.shape, F32)
    for k in range(TOP_K):
        oh = oh + (row == te[k:k + 1, :]).astype(F32)
    return row, oh


def _groups(cnt):
    return jnp.floor((cnt + (ROW_GROUP - 1)) / ROW_GROUP)


def _placement(n8, n_asg):
    n_tiles, n_exp = n8.shape
    gpt = _sort_rows(n_exp) // ROW_GROUP
    gpb = MOE_BM // ROW_GROUP
    assert (TOP_K * TM + n_exp * (ROW_GROUP - 1)) // ROW_GROUP < gpt
    ids = jnp.arange(n_exp, dtype=jnp.int32)
    lbase = jnp.cumsum(n8, axis=1) - n8
    cum_incl = jnp.cumsum(n8, axis=0)
    cum_excl = cum_incl - n8
    groups = cum_incl[-1]
    padded = (groups + gpb - 1) // gpb * gpb
    pend = jnp.cumsum(padded)
    pstart = pend - padded
    n_blocks = -(-(n_asg + n_tiles * n_exp * (ROW_GROUP - 1)) // MOE_BM) + n_exp
    n_used = (pend[-1] // gpb).astype(jnp.int32)
    blocks = jnp.arange(n_blocks, dtype=jnp.int32)
    blk = jnp.minimum(blocks, n_used - 1) * gpb
    block_e = jnp.minimum(jnp.sum((pend[None, :] <= blk[:, None]).astype(jnp.int32), axis=1), n_exp - 1)
    later = (padded[None, :] > 0) & (ids[None, :] > block_e[:, None])
    block_next_e = jnp.min(jnp.where(later, ids[None, :], n_exp), axis=1)
    block_next_e = jnp.where(block_next_e == n_exp, block_e, block_next_e).astype(jnp.int32)
    mine = (block_e[:, None] == ids[None, :]).astype(jnp.int32)
    pick = lambda table: jnp.sum(mine[:, :, None] * table.T[None, :, :], axis=1)
    b_start = jnp.sum(mine * pstart[None, :], axis=1)
    b_groups = jnp.sum(mine * groups[None, :], axis=1)
    j_e = (blocks * gpb - b_start)[:, None] + jnp.arange(gpb, dtype=jnp.int32)[None, :]
    valid = (j_e < b_groups[:, None]) & (blocks < n_used)[:, None]
    tile = jnp.sum((pick(cum_incl)[:, None, :] <= j_e[:, :, None]).astype(jnp.int32), axis=2)
    tile = jnp.minimum(tile, n_tiles - 1)
    in_tile = (tile[:, :, None] == jnp.arange(n_tiles, dtype=jnp.int32)[None, None, :]).astype(jnp.int32)
    shift = jnp.sum(in_tile * (pick(lbase) - pick(cum_excl))[:, None, :], axis=2)
    held = tile * gpt + shift + j_e
    pad_dst = n_tiles * gpt + jnp.arange(gpb, dtype=jnp.int32)[None, :]
    src = jnp.where(valid, held, gpt - 1)
    dst = jnp.where(valid, held, pad_dst)
    rec = jnp.concatenate([src, dst, jnp.zeros((n_blocks, SMEM_BLOCK - 2 * gpb), jnp.int32)], axis=1)
    left = b_groups - (blocks - b_start // gpb) * gpb
    block_rows = (jnp.clip(left, 0, gpb) * ROW_GROUP).astype(jnp.int32)
    return dict(rec=rec.reshape(-1), block_e=block_e, block_next_e=block_next_e, block_rows=block_rows,
                n_used=n_used.reshape(1), n_blocks=n_blocks)


def _group_copy(src_ref, src_group, dst_ref, dst_group, sem):
    s0 = pl.multiple_of(src_group * ROW_GROUP, ROW_GROUP)
    d0 = pl.multiple_of(dst_group * ROW_GROUP, ROW_GROUP)
    return pltpu.make_async_copy(src_ref.at[pl.ds(s0, ROW_GROUP)], dst_ref.at[pl.ds(d0, ROW_GROUP)], sem)


def _tile_step():
    return pl.program_id(0) * pl.num_programs(1) + pl.program_id(1)


def _sort_rows(n_exp):
    return -(-(TOP_K * TM + n_exp * ROW_GROUP) // TM) * TM


def _combine_kernel(nct, alpha, ys_ref, pos_ref, tg_ref, x1_ref, mlat_ref, mctx_ref, lng_ref, lnb_ref, o_ref):
    d = x1_ref.shape[-1]
    tm = x1_ref.shape[1]
    is_ctx = pl.program_id(1) < nct
    m = jnp.where(is_ctx, mctx_ref[...], mlat_ref[0])
    pos = pos_ref[0]
    gates = tg_ref[0]
    f_lo = f_hi = None
    for c in range(ys_ref.shape[0] // tm):
        col = lax.broadcasted_iota(jnp.int32, (tm, tm), 1) + c * tm
        w = jnp.where(col == pos[:, 0:1], gates[:, 0:1], 0.0)
        for k in range(1, TOP_K):
            w = w + jnp.where(col == pos[:, k:k + 1], gates[:, k:k + 1], 0.0)
        w = w.astype(BF16)
        y_lo, y_hi = _unpack_bf16_pairs(ys_ref[c * tm:(c + 1) * tm, :])
        f_lo = _dot(w, y_lo) if f_lo is None else f_lo + _dot(w, y_lo)
        f_hi = _dot(w, y_hi) if f_hi is None else f_hi + _dot(w, y_hi)
    f = jnp.concatenate([f_lo, f_hi], axis=1)
    o_ref[0] = _layernorm(alpha * x1_ref[0] + m[:, 5 * d:6 * d] * f, lng_ref[...], lnb_ref[...])


def _combine_postnorm(ys, pos_t, tg_t, x1, mlat, mctx, lng, lnb, nct, alpha, n_exp):
    b, t, d = x1.shape
    nt = t // TM
    tile = lambda bi, i: (bi, i, 0)
    full = lambda bi, i: (0, 0)
    return pl.pallas_call(
        functools.partial(_combine_kernel, nct, alpha),
        grid=(b, nt),
        in_specs=[pl.BlockSpec((_sort_rows(n_exp), d // 2), lambda bi, i: (bi * nt + i, 0)),
                  pl.BlockSpec((1, TM, 8), tile),
                  pl.BlockSpec((1, TM, 8), tile),
                  pl.BlockSpec((1, TM, d), tile),
                  pl.BlockSpec((1, 1, N_MOD * d), lambda bi, i: (bi, 0, 0)),
                  pl.BlockSpec((1, N_MOD * d), full),
                  pl.BlockSpec((1, d), full),
                  pl.BlockSpec((1, d), full)],
        out_specs=pl.BlockSpec((1, TM, d), tile),
        out_shape=jax.ShapeDtypeStruct((b, t, d), F32),
        compiler_params=_cparams(("parallel", "parallel")),
        name="combine_post_norm",
    )(ys, pos_t, tg_t, x1, mlat, mctx, lng, lnb)


def _rope_tables(ctx_len, seq):
    t = np.arange(seq)
    row = (t // GRID_W).astype(np.float64)
    col = (t % GRID_W).astype(np.float64)
    nf = HEAD_DIM // 4
    inv = ROPE_THETA ** (-np.arange(nf, dtype=np.float64) / nf)
    ar = row[:, None] * inv[None, :]
    ac = col[:, None] * inv[None, :]
    ang = np.concatenate([ar, ar, ac, ac], axis=-1)
    cos = np.concatenate([np.ones((ctx_len, HEAD_DIM)), np.cos(ang)], axis=0)
    sin = np.concatenate([np.zeros((ctx_len, HEAD_DIM)), np.sin(ang)], axis=0)
    sign = np.where((np.arange(HEAD_DIM) % 32) < 16, -1.0, 1.0)[None, :]
    cos2 = np.tile(cos, (1, LANES // HEAD_DIM))
    sin2 = np.tile(sin * sign, (1, LANES // HEAD_DIM))
    return jnp.asarray(cos2, F32), jnp.asarray(sin2, F32)


def _channel_dft():
    c = np.arange(HEAD_DIM)
    ang = 2.0 * np.pi * ((c[:, None] * c[None, :]) % HEAD_DIM) / HEAD_DIM
    eye = np.eye(4)
    cs = np.concatenate([np.kron(eye, np.cos(ang)), np.kron(eye, np.sin(ang))], axis=1)
    return jnp.asarray(cs, BF16)


def _group_mean_matrix():
    bd = np.kron(np.eye(LANES // HEAD_DIM), np.full((HEAD_DIM, HEAD_DIM), 1.0 / HEAD_DIM))
    return jnp.asarray(bd, BF16)


def _dup_heads(wk, n_heads):
    d = wk.shape[0]
    return jnp.broadcast_to(wk.reshape(d, n_heads, 1, HEAD_DIM), (d, n_heads, 2, HEAD_DIM)).reshape(d, n_heads * LANES)


def _lambda_init(layer):
    return 0.8 - 0.6 * math.exp(-0.3 * layer)


def _moe_postnorm(routed, mlat, mctx, lng, lnb, nct, alpha, layer, w_gu, b_gu, w_down, b_down):
    x1, tg, n8, pos, xs, ys = routed
    n_exp = w_gu.shape[1]
    b, t, _ = x1.shape
    place = _placement(n8[:, :, 0], b * t * TOP_K)
    ys = _moe_experts(place, layer, xs, ys, w_gu, b_gu, w_down, b_down)
    pos_t, tg_t = jnp.transpose(pos, (0, 2, 1)), jnp.transpose(tg, (0, 2, 1))
    return _combine_postnorm(ys, pos_t, tg_t, x1, mlat, mctx, lng, lnb, nct, alpha, n_exp)


def kernel(x, c, ctx, c_ctx, mod_w, mod_b, ln_g, ln_b, ab_w_in, ab_sink, ab_w_out,
           cd_w_in, cd_lambda, cd_subln_g, cd_q_norm_g, cd_k_norm_g, cd_w_out,
           router_w, router_b, expert_w_gu, expert_b_gu, expert_w_down, expert_b_down):
    b, s, d = x.shape
    n_ctx = ctx.shape[1]
    depth = mod_w.shape[0]
    n_exp = router_w.shape[-1]
    assert d == 16 * HEAD_DIM and n_ctx % TM == 0 and s % TM == 0 and s % GRID_W == 0
    nct = n_ctx // TM
    alpha = (2 * depth) ** 0.25

    cos, sin = _rope_tables(n_ctx, s)
    cs_dft = _channel_dft()
    bd = _group_mean_matrix()
    mods = _mod_vectors(c, c_ctx, mod_w, mod_b)

    xs, xc = x, ctx
    for l in range(depth):
        last = l == depth - 1
        i = l // 2
        mlat = mods[l, :b].reshape(b, 1, N_MOD * d)
        mctx = mods[l, b:b + 1]
        lng1, lnb1 = ln_g[l, 0].reshape(1, d), ln_b[l, 0].reshape(1, d)
        lng2, lnb2 = ln_g[l, 1].reshape(1, d), ln_b[l, 1].reshape(1, d)
        rwt = router_w[l].T.astype(BF16)
        rb = router_b[l].reshape(n_exp, 1)
        if l % 2 == 0:
            w = ab_w_in[i]
            w_n = jnp.concatenate([w[:, :1024], _dup_heads(w[:, 1024:1280], 4)], axis=1).astype(BF16)
            wvt = w[:, 1280:1536].T.astype(BF16)
            uw, q, k, vt = _proj_ab(xs, xc, mlat, mctx, w_n, wvt, cos, sin, cs_dft, nct)
            oa = jnp.concatenate([_fourier(uw[:, :n_ctx]), _fourier(uw[:, n_ctx:])], axis=1)
            ob = _win_attn(ab_sink[i], q, k, vt, nct)
            if last:
                parts = [oa[:, n_ctx:], ob[:, n_ctx:]]
            else:
                parts = [oa, ob]
            w_out = ab_w_out[i].astype(BF16)
        else:
            w = cd_w_in[i]
            w_n = jnp.concatenate([w[:, :1536], _dup_heads(w[:, 1536:1664], 2)], axis=1).astype(BF16)
            wvt = w[:, 1664:2304].T.astype(BF16)
            gq = jnp.tile(cd_q_norm_g[i], 2).reshape(1, LANES)
            gk = jnp.tile(cd_k_norm_g[i], 2).reshape(1, LANES)
            qc, qd, kc, kd, vt = _proj_cd(xs, xc, mlat, mctx, w_n, wvt, cos, sin, bd, gq, gk, nct)
            sg = cd_subln_g[i].reshape(LANES, 1)
            o_lat = _attn_cd(cd_lambda[i], sg, qc, qd, kc, kd, vt, nct, _lambda_init(l))
            if last:
                parts = [o_lat]
            else:
                raise NotImplementedError("context outputs of a differential/axial layer")
            w_out = cd_w_out[i].astype(BF16)
        experts = (l, expert_w_gu, expert_b_gu, expert_w_down, expert_b_down)
        if last:
            routed = _oproj(parts, w_out, xs, xc, mlat, mctx, lng1, lnb1, rwt, rb, 0, nct, alpha)
            return _moe_postnorm(routed, mlat, mctx, lng2, lnb2, 0, alpha, *experts)
        routed = _oproj(parts, w_out, xs, xc, mlat, mctx, lng1, lnb1, rwt, rb, nct, 0, alpha)
        xs, xc = _moe_postnorm(routed, mlat, mctx, lng2, lnb2, nct, alpha, *experts), None
    return xs[:, n_ctx:]
```
